```python
import math
import jax, jax.numpy as jnp
from jax import lax
import numpy as np

D_MODEL = 1024
BATCH = 8
SEQ = 4096
DEPTH = 2

N_MIXERS = 2
POOL_WINDOWS = (2, 4, 8, 16)
N_POOL_GROUPS = len(POOL_WINDOWS)
POOL_GROUP_W = D_MODEL // N_POOL_GROUPS
CONV_WIDTH = 3
D_FF = int(math.ceil(8 * D_MODEL / 3 / 256) * 256)
RMS_EPS = 1e-6
N_POOL_LAYERS = (DEPTH + 1) // 2
N_CONV_LAYERS = DEPTH // 2

kernel_name = "hybrid_pool_shortconv_sandwich"


def rms_norm(x, g):
    xf = x.astype(jnp.float32)
    y = xf * lax.rsqrt(jnp.mean(xf * xf, axis=-1, keepdims=True) + RMS_EPS)
    return (y * g.astype(jnp.float32)).astype(x.dtype)


def pool_mixer(h, w_groups, scale):
    bsz, seq, d = h.shape
    hg = h.astype(jnp.float32).reshape(bsz, seq, N_POOL_GROUPS, POOL_GROUP_W)
    cs = jnp.cumsum(hg, axis=1)
    pos = jnp.arange(1, seq + 1, dtype=jnp.float32)
    outs = []
    for g, w in enumerate(POOL_WINDOWS):
        c = cs[:, :, g]
        lagged = jnp.pad(c, ((0, 0), (w, 0), (0, 0)))[:, :seq]
        mean = (c - lagged) / jnp.minimum(pos, float(w))[None, :, None]
        outs.append(mean - hg[:, :, g])
    pooled = jnp.stack(outs, axis=2).astype(h.dtype)
    mixed = jnp.einsum('bsgc,gcd->bsgd', pooled, w_groups).reshape(bsz, seq, d)
    return mixed * scale


def short_conv_mixer(h, w_in, taps, w_out):
    seq = h.shape[1]
    proj = jnp.einsum('bsd,de->bse', h, w_in)
    b_gate, c_gate, v = jnp.split(proj, 3, axis=-1)
    u = c_gate * v
    up = jnp.pad(u, ((0, 0), (CONV_WIDTH - 1, 0), (0, 0)))
    conv = up[:, 0:seq] * taps[0]
    for k in range(1, CONV_WIDTH):
        conv = conv + up[:, k:k + seq] * taps[k]
    return jnp.einsum('bsd,de->bse', b_gate * conv, w_out)


def swiglu_ffn(h, w_gate_up, w_down):
    gu = jnp.einsum('bsd,df->bsf', h, w_gate_up)
    gate, up = jnp.split(gu, 2, axis=-1)
    return jnp.einsum('bsf,fd->bsd', jax.nn.silu(gate) * up, w_down)


def _fwd_setup_inputs(seed: int = 0) -> dict:
    key = jax.random.key(seed)
    ks = jax.random.split(key, 10)
    f32 = jnp.float32
    x = jax.random.normal(ks[0], (BATCH, SEQ, D_MODEL), f32)
    norm_gains = 1.0 + 0.05 * jax.random.normal(ks[1], (DEPTH, 4, D_MODEL), f32)
    pool_w = jax.random.normal(ks[2], (N_POOL_LAYERS, N_POOL_GROUPS, POOL_GROUP_W, POOL_GROUP_W), f32) * POOL_GROUP_W ** -0.5
    pool_scale = 1.0 + 0.1 * jax.random.normal(ks[3], (N_POOL_LAYERS, D_MODEL), f32)
    conv_in_w = jax.random.normal(ks[4], (N_CONV_LAYERS, D_MODEL, 3 * D_MODEL), f32) * D_MODEL ** -0.5
    conv_w = jax.random.normal(ks[5], (N_CONV_LAYERS, CONV_WIDTH, D_MODEL), f32) * CONV_WIDTH ** -0.5
    conv_out_w = jax.random.normal(ks[6], (N_CONV_LAYERS, D_MODEL, D_MODEL), f32) * D_MODEL ** -0.5
    ffn_gate_up_w = jax.random.normal(ks[7], (DEPTH, D_MODEL, 2 * D_FF), f32) * D_MODEL ** -0.5
    ffn_down_w = jax.random.normal(ks[8], (DEPTH, D_FF, D_MODEL), f32) * D_FF ** -0.5
    return {"x": x, "norm_gains": norm_gains, "pool_w": pool_w, "pool_scale": pool_scale,
            "conv_in_w": conv_in_w, "conv_w": conv_w, "conv_out_w": conv_out_w,
            "ffn_gate_up_w": ffn_gate_up_w, "ffn_down_w": ffn_down_w}


def _fwd_reference(x, norm_gains, pool_w, pool_scale, conv_in_w, conv_w, conv_out_w, ffn_gate_up_w, ffn_down_w):
    h = x
    for i in range(DEPTH):
        j = i // N_MIXERS
        g = norm_gains[i]
        hn = rms_norm(h, g[0])
        if i % N_MIXERS == 0:
            mix = pool_mixer(hn, pool_w[j], pool_scale[j])
        else:
            mix = short_conv_mixer(hn, conv_in_w[j], conv_w[j], conv_out_w[j])
        h = h + rms_norm(mix, g[1])
        ff = swiglu_ffn(rms_norm(h, g[2]), ffn_gate_up_w[i], ffn_down_w[i])
        h = h + rms_norm(ff, g[3])
    return h


import jax as _jax
import jax.numpy as _jnp

TWIN_FORMAT = 'train_step'
FWD_PARAMS = ['x', 'norm_gains', 'pool_w', 'pool_scale', 'conv_in_w', 'conv_w', 'conv_out_w', 'ffn_gate_up_w', 'ffn_down_w']
TWIN_WEIGHTS = ['norm_gains', 'pool_w', 'pool_scale', 'conv_in_w', 'conv_w', 'conv_out_w', 'ffn_gate_up_w', 'ffn_down_w']
TWIN_DIFF_INPUT = 'x'
TWIN_INPUTS = ['x', 'norm_gains', 'pool_w', 'pool_scale', 'conv_in_w', 'conv_w', 'conv_out_w', 'ffn_gate_up_w', 'ffn_down_w', 'loss_target', 'm_norm_gains', 'm_pool_w', 'm_pool_scale', 'm_conv_in_w', 'm_conv_w', 'm_conv_out_w', 'm_ffn_gate_up_w', 'm_ffn_down_w', 'v_norm_gains', 'v_pool_w', 'v_pool_scale', 'v_conv_in_w', 'v_conv_w', 'v_conv_out_w', 'v_ffn_gate_up_w', 'v_ffn_down_w']
TWIN_OUTPUTS = ['loss', 'grad_x', 'grad_norm_gains', 'grad_pool_w', 'grad_pool_scale', 'grad_conv_in_w', 'grad_conv_w', 'grad_conv_out_w', 'grad_ffn_gate_up_w', 'grad_ffn_down_w', 'delta_norm_gains', 'delta_pool_w', 'delta_pool_scale', 'delta_conv_in_w', 'delta_conv_w', 'delta_conv_out_w', 'delta_ffn_gate_up_w', 'delta_ffn_down_w', 'new_m_norm_gains', 'new_m_pool_w', 'new_m_pool_scale', 'new_m_conv_in_w', 'new_m_conv_w', 'new_m_conv_out_w', 'new_m_ffn_gate_up_w', 'new_m_ffn_down_w', 'new_v_norm_gains', 'new_v_pool_w', 'new_v_pool_scale', 'new_v_conv_in_w', 'new_v_conv_w', 'new_v_conv_out_w', 'new_v_ffn_gate_up_w', 'new_v_ffn_down_w']
TWIN_LEAF_KINDS = {'loss': 'loss', 'grad_x': 'grad_x', 'grad_norm_gains': 'grad_w', 'grad_pool_w': 'grad_w', 'grad_pool_scale': 'grad_w', 'grad_conv_in_w': 'grad_w', 'grad_conv_w': 'grad_w', 'grad_conv_out_w': 'grad_w', 'grad_ffn_gate_up_w': 'grad_w', 'grad_ffn_down_w': 'grad_w', 'delta_norm_gains': 'delta_w', 'delta_pool_w': 'delta_w', 'delta_pool_scale': 'delta_w', 'delta_conv_in_w': 'delta_w', 'delta_conv_w': 'delta_w', 'delta_conv_out_w': 'delta_w', 'delta_ffn_gate_up_w': 'delta_w', 'delta_ffn_down_w': 'delta_w', 'new_m_norm_gains': 'new_m', 'new_m_pool_w': 'new_m', 'new_m_pool_scale': 'new_m', 'new_m_conv_in_w': 'new_m', 'new_m_conv_w': 'new_m', 'new_m_conv_out_w': 'new_m', 'new_m_ffn_gate_up_w': 'new_m', 'new_m_ffn_down_w': 'new_m', 'new_v_norm_gains': 'new_v', 'new_v_pool_w': 'new_v', 'new_v_pool_scale': 'new_v', 'new_v_conv_in_w': 'new_v', 'new_v_conv_w': 'new_v', 'new_v_conv_out_w': 'new_v', 'new_v_ffn_gate_up_w': 'new_v', 'new_v_ffn_down_w': 'new_v'}


def _forward(args):
    return _fwd_reference(*[args[k] for k in FWD_PARAMS])


def _output_shape():
    def fwd():
        inp = _fwd_setup_inputs(0)
        return _fwd_reference(*[inp[k] for k in FWD_PARAMS])
    out = _jax.eval_shape(fwd)
    return out.shape, out.dtype

N_MICROBATCH = 1
ADAM_LR = 0.001
ADAM_B1 = 0.9
ADAM_B2 = 0.999
ADAM_EPS = 1e-08
ADAM_WD = 0.01
ADAM_STEP = 10
PER_EXAMPLE_BATCH_AXIS = {'x': 0, 'loss_target': 0}
SHARED_INPUTS = []
_WEIGHT_DTYPES = {'norm_gains': _jnp.float32, 'pool_w': _jnp.float32, 'pool_scale': _jnp.float32, 'conv_in_w': _jnp.float32, 'conv_w': _jnp.float32, 'conv_out_w': _jnp.float32, 'ffn_gate_up_w': _jnp.float32, 'ffn_down_w': _jnp.float32}
MOMENT_SCALE = {'norm_gains': 2.289074e+01, 'pool_w': 1.718505e+00, 'pool_scale': 3.339379e+00, 'conv_in_w': 4.958597e-01, 'conv_w': 5.238209e-01, 'conv_out_w': 5.447258e-01, 'ffn_gate_up_w': 4.096937e-01, 'ffn_down_w': 7.747924e-01}


def _to_microbatches(a, axis):
    t = _jnp.moveaxis(a, axis, 0)
    t = t.reshape((N_MICROBATCH, t.shape[0] // N_MICROBATCH) + t.shape[1:])
    return _jnp.moveaxis(t, 1, axis + 1)


def setup_inputs(seed: int = 0) -> dict:
    inp = _fwd_setup_inputs(seed)
    key = _jax.random.fold_in(_jax.random.key(seed), 7919)
    shape, _ = _output_shape()
    out = dict(inp)
    out["loss_target"] = _jax.random.normal(_jax.random.fold_in(key, 0), shape, _jnp.float32)
    for i, name in enumerate(TWIN_WEIGHTS):
        w = inp[name].astype(_jnp.float32)
        if MOMENT_SCALE is None:
            s = _jnp.sqrt(_jnp.mean(_jnp.square(w)) + 1e-30)
        else:
            s = MOMENT_SCALE[name]
        km, kv = _jax.random.split(_jax.random.fold_in(key, i + 1))
        out[name] = w
        out["m_" + name] = s * _jax.random.normal(km, w.shape, _jnp.float32)
        out["v_" + name] = (s * s) * _jax.random.uniform(kv, w.shape, _jnp.float32, 0.5, 1.5)
    if N_MICROBATCH > 1:
        for name, axis in PER_EXAMPLE_BATCH_AXIS.items():
            out[name] = _to_microbatches(out[name], axis)
    return {'x': out['x'], 'norm_gains': out['norm_gains'], 'pool_w': out['pool_w'], 'pool_scale': out['pool_scale'], 'conv_in_w': out['conv_in_w'], 'conv_w': out['conv_w'], 'conv_out_w': out['conv_out_w'], 'ffn_gate_up_w': out['ffn_gate_up_w'], 'ffn_down_w': out['ffn_down_w'], 'loss_target': out['loss_target'], 'm_norm_gains': out['m_norm_gains'], 'm_pool_w': out['m_pool_w'], 'm_pool_scale': out['m_pool_scale'], 'm_conv_in_w': out['m_conv_in_w'], 'm_conv_w': out['m_conv_w'], 'm_conv_out_w': out['m_conv_out_w'], 'm_ffn_gate_up_w': out['m_ffn_gate_up_w'], 'm_ffn_down_w': out['m_ffn_down_w'], 'v_norm_gains': out['v_norm_gains'], 'v_pool_w': out['v_pool_w'], 'v_pool_scale': out['v_pool_scale'], 'v_conv_in_w': out['v_conv_in_w'], 'v_conv_w': out['v_conv_w'], 'v_conv_out_w': out['v_conv_out_w'], 'v_ffn_gate_up_w': out['v_ffn_gate_up_w'], 'v_ffn_down_w': out['v_ffn_down_w']}


def _loss(weights, diff, rest, loss_target):
    with _jax.named_scope("forward"):
        args = {**rest, TWIN_DIFF_INPUT: diff, **{k: w.astype(_WEIGHT_DTYPES[k]) for k, w in weights.items()}}
        y = _forward(args)
    with _jax.named_scope("loss_head"):
        err = _jnp.square(y.astype(_jnp.float32) - loss_target)
        return 0.5 * _jnp.sum(_jnp.mean(err, axis=-1)) if err.ndim else 0.5 * err


def _adamw(w, g, m, v):
    m = ADAM_B1 * m + (1.0 - ADAM_B1) * g
    v = ADAM_B2 * v + (1.0 - ADAM_B2) * _jnp.square(g)
    m_hat = m / (1.0 - ADAM_B1 ** ADAM_STEP)
    v_hat = v / (1.0 - ADAM_B2 ** ADAM_STEP)
    delta = -ADAM_LR * (m_hat / (_jnp.sqrt(v_hat) + ADAM_EPS) + ADAM_WD * w)
    return delta, m, v


def reference(x, norm_gains, pool_w, pool_scale, conv_in_w, conv_w, conv_out_w, ffn_gate_up_w, ffn_down_w, loss_target, m_norm_gains, m_pool_w, m_pool_scale, m_conv_in_w, m_conv_w, m_conv_out_w, m_ffn_gate_up_w, m_ffn_down_w, v_norm_gains, v_pool_w, v_pool_scale, v_conv_in_w, v_conv_w, v_conv_out_w, v_ffn_gate_up_w, v_ffn_down_w):
    given = dict(x=x, norm_gains=norm_gains, pool_w=pool_w, pool_scale=pool_scale, conv_in_w=conv_in_w, conv_w=conv_w, conv_out_w=conv_out_w, ffn_gate_up_w=ffn_gate_up_w, ffn_down_w=ffn_down_w, loss_target=loss_target, m_norm_gains=m_norm_gains, m_pool_w=m_pool_w, m_pool_scale=m_pool_scale, m_conv_in_w=m_conv_in_w, m_conv_w=m_conv_w, m_conv_out_w=m_conv_out_w, m_ffn_gate_up_w=m_ffn_gate_up_w, m_ffn_down_w=m_ffn_down_w, v_norm_gains=v_norm_gains, v_pool_w=v_pool_w, v_pool_scale=v_pool_scale, v_conv_in_w=v_conv_in_w, v_conv_w=v_conv_w, v_conv_out_w=v_conv_out_w, v_ffn_gate_up_w=v_ffn_gate_up_w, v_ffn_down_w=v_ffn_down_w)
    weights = {n: given[n] for n in TWIN_WEIGHTS}
    shared = {n: given[n] for n in SHARED_INPUTS}
    per_example = {n: given[n] for n in ['x']}
    grad_fn = _jax.value_and_grad(_loss, argnums=(0, 1))

    def one_microbatch(ex, loss_target):
        ex = dict(ex)
        diff = ex.pop(TWIN_DIFF_INPUT)
        return grad_fn(weights, diff, {**shared, **ex}, loss_target)

    if N_MICROBATCH == 1:
        loss, (grad_w, grad_x) = one_microbatch(per_example, given["loss_target"])
    else:
        def body(carry, xs):
            loss_sum, grad_sum = carry
            l_k, (gw_k, gx_k) = one_microbatch(xs[0], xs[1])
            with _jax.named_scope("update"):
                return (loss_sum + l_k, _jax.tree.map(_jnp.add, grad_sum, gw_k)), gx_k

        init = (_jnp.zeros((), _jnp.float32), _jax.tree.map(_jnp.zeros_like, weights))
        (loss, grad_w), grad_x = _jax.lax.scan(body, init, (per_example, given["loss_target"]))
    with _jax.named_scope("update"):
        delta_w, new_m, new_v = {}, {}, {}
        for n in TWIN_WEIGHTS:
            delta_w[n], new_m[n], new_v[n] = _adamw(weights[n], grad_w[n], given["m_" + n], given["v_" + n])
    return (loss, grad_x, *[grad_w[n] for n in TWIN_WEIGHTS], *[delta_w[n] for n in TWIN_WEIGHTS],
            *[new_m[n] for n in TWIN_WEIGHTS], *[new_v[n] for n in TWIN_WEIGHTS])
```

```python
import functools

import jax
import jax.numpy as jnp
from jax import lax
from jax.experimental import pallas as pl
from jax.experimental.pallas import tpu as pltpu

N_DEV = 8
RMS_EPS = 1e-6
POOL_WINDOWS = (2, 4, 8, 16)
POOL_HALO = 16
CONV_HALO = 16
ADAM_LR, ADAM_B1, ADAM_B2, ADAM_EPS, ADAM_WD, ADAM_STEP = 0.001, 0.9, 0.999, 1e-08, 0.01, 10

VMEM_LIMIT = 56 * 2**20
BF16 = jnp.bfloat16
F32 = jnp.float32
MESH = pl.DeviceIdType.MESH


def _params(**kw):
    return pltpu.CompilerParams(vmem_limit_bytes=VMEM_LIMIT, **kw)


def _resident(shape, index_map):
    return pl.BlockSpec(shape, index_map, pipeline_mode=pl.Buffered(1))


def _rms_fwd(x, g):
    r = lax.rsqrt(jnp.mean(x * x, axis=-1, keepdims=True) + RMS_EPS)
    return x * r * g


def _rms_bwd(x, g, dy):
    r = lax.rsqrt(jnp.mean(x * x, axis=-1, keepdims=True) + RMS_EPS)
    xhat = x * r
    dg = jnp.sum(dy * xhat, axis=0, keepdims=True)
    t = dy * g
    dx = r * (t - xhat * jnp.mean(t * xhat, axis=-1, keepdims=True))
    return dx, dg


def _dot(a, b):
    return jnp.dot(a, b, preferred_element_type=F32)


def _dot_nt(a, b):
    return lax.dot_general(a, b, (((1,), (1,)), ((), ())), preferred_element_type=F32)


def _dot_tn(a, b):
    return lax.dot_general(a, b, (((0,), (0,)), ((), ())), preferred_element_type=F32)


def _row_inverse_counts(tile, tm):
    pos = (lax.broadcasted_iota(jnp.int32, (tm, 1), 0) + tile * tm + 1).astype(F32)
    return [1.0 / jnp.minimum(pos, float(w)) for w in POOL_WINDOWS]


def _pool_from_ext(ext, a, invs, gw):
    s = ext
    outs = []
    for g, w in enumerate(POOL_WINDOWS):
        s = s[:, (gw if g else 0):]
        s = s + pltpu.roll(s, w // 2, 0)
        outs.append(s[POOL_HALO:, :gw] * invs[g] - a[:, g * gw:(g + 1) * gw])
    return outs


def _pool_fwd(h, pw, scale, g_pre, g_post, *, tm):
    T, D = h.shape
    gw = D // len(POOL_WINDOWS)
    hb = tm // POOL_HALO

    def body(h_ref, halo_ref, pw_ref, scale_ref, gpre_ref, gpost_ref, out_ref, ext_ref):
        i = pl.program_id(0)
        x = h_ref[...]
        a = _rms_fwd(x, gpre_ref[...])
        ah = _rms_fwd(halo_ref[...], gpre_ref[...])
        ext_ref[0:POOL_HALO, :] = jnp.where(i == 0, 0.0, ah)
        ext_ref[POOL_HALO:, :] = a
        pooled = _pool_from_ext(ext_ref[...], a, _row_inverse_counts(i, tm), gw)
        mixed = jnp.concatenate([_dot(p.astype(BF16), pw_ref[g]) for g, p in enumerate(pooled)], axis=1)
        out_ref[...] = x + _rms_fwd(mixed * scale_ref[...], gpost_ref[...])

    vec = _resident((1, D), lambda i: (0, 0))
    return pl.pallas_call(
        body, name="pool_fwd", grid=(T // tm,),
        in_specs=[pl.BlockSpec((tm, D), lambda i: (i, 0)),
                  pl.BlockSpec((POOL_HALO, D), lambda i: (jnp.maximum(i * hb - 1, 0), 0)),
                  _resident(pw.shape, lambda i: (0, 0, 0)), vec, vec, vec],
        out_specs=pl.BlockSpec((tm, D), lambda i: (i, 0)),
        out_shape=jax.ShapeDtypeStruct((T, D), F32),
        scratch_shapes=[pltpu.VMEM((tm + POOL_HALO, D), F32)],
        compiler_params=_params(dimension_semantics=("arbitrary",)),
    )(h, h, pw, scale, g_pre, g_post)


def _pool_bwd(dh, h, pw, scale, g_pre, g_post, *, tm):
    T, D = h.shape
    gw = D // len(POOL_WINDOWS)
    hb = tm // POOL_HALO
    nt = T // tm
    n_ext = tm + POOL_HALO

    def body(dh_ref, h_ref, halo_ref, pw_ref, scale_ref, gpre_ref, gpost_ref,
             dx_ref, dpw_ref, small_ref, ext_ref, ext2_ref, carry_ref, dpw_acc):
        i = pl.program_id(0)
        tile = nt - 1 - i

        @pl.when(i == 0)
        def _():
            small_ref[...] = jnp.zeros_like(small_ref)
            dpw_acc[...] = jnp.zeros_like(dpw_acc)
            carry_ref[...] = jnp.zeros_like(carry_ref)

        x = h_ref[...]
        dout = dh_ref[...]
        a = _rms_fwd(x, gpre_ref[...])
        ah = _rms_fwd(halo_ref[...], gpre_ref[...])
        ext_ref[0:POOL_HALO, :] = jnp.where(tile == 0, 0.0, ah)
        ext_ref[POOL_HALO:, :] = a
        invs = _row_inverse_counts(tile, tm)
        pooled = [p.astype(BF16) for p in _pool_from_ext(ext_ref[...], a, invs, gw)]
        mixed_pre = jnp.concatenate([_dot(p, pw_ref[g]) for g, p in enumerate(pooled)], axis=1)
        scale_v = scale_ref[...]
        dmixed, dg_post = _rms_bwd(mixed_pre * scale_v, gpost_ref[...], dout)
        small_ref[1:2, :] += dg_post
        small_ref[2:3, :] += jnp.sum(dmixed * mixed_pre, axis=0, keepdims=True)
        dpre = (dmixed * scale_v).astype(BF16)
        dpooled = []
        for g in range(len(POOL_WINDOWS)):
            dp = dpre[:, g * gw:(g + 1) * gw]
            dpw_acc[g] += _dot_tn(pooled[g], dp)
            dpooled.append(_dot_nt(dp, pw_ref[g]))
        q = jnp.concatenate([d * invs[g] for g, d in enumerate(dpooled)], axis=1)
        ext2_ref[0:tm, :] = q
        ext2_ref[tm:, :] = carry_ref[...]
        carry_ref[...] = q[0:POOL_HALO, :]
        s = ext2_ref[...]
        da = []
        for g, w in enumerate(POOL_WINDOWS):
            s = s[:, (gw if g else 0):]
            s = s + pltpu.roll(s, n_ext - w // 2, 0)
            da.append(s[0:tm, :gw] - dpooled[g])
        dx, dg_pre = _rms_bwd(x, gpre_ref[...], jnp.concatenate(da, axis=1))
        small_ref[0:1, :] += dg_pre
        dx_ref[...] = dout + dx

        @pl.when(i == nt - 1)
        def _():
            dpw_ref[...] = dpw_acc[...].astype(BF16)

    vec = _resident((1, D), lambda i: (0, 0))
    rev = lambda i: (nt - 1 - i, 0)
    return pl.pallas_call(
        body, name="pool_bwd", grid=(nt,),
        in_specs=[pl.BlockSpec((tm, D), rev), pl.BlockSpec((tm, D), rev),
                  pl.BlockSpec((POOL_HALO, D), lambda i: (jnp.maximum((nt - 1 - i) * hb - 1, 0), 0)),
                  _resident(pw.shape, lambda i: (0, 0, 0)), vec, vec, vec],
        out_specs=[pl.BlockSpec((tm, D), rev),
                   pl.BlockSpec(pw.shape, lambda i: (0, 0, 0)),
                   pl.BlockSpec((8, D), lambda i: (0, 0))],
        out_shape=[jax.ShapeDtypeStruct((T, D), F32), jax.ShapeDtypeStruct(pw.shape, BF16),
                   jax.ShapeDtypeStruct((8, D), F32)],
        scratch_shapes=[pltpu.VMEM((n_ext, D), F32), pltpu.VMEM((n_ext, D), F32),
                        pltpu.VMEM((POOL_HALO, D), F32), pltpu.VMEM(pw.shape, F32)],
        compiler_params=_params(dimension_semantics=("arbitrary",)),
    )(dh, h, h, pw, scale, g_pre, g_post)


def _ffn_fwd(h, g_pre, g_post, wgu, wd, layer, target, *, tm):
    T, D = h.shape
    nblk, fb = wgu.shape[1], wgu.shape[3]
    half = nblk // 2
    last = target is not None

    def body(*refs):
        if last:
            h_ref, gpre_ref, gpost_ref, wgu_ref, wd_ref, tgt_ref, out_ref, gu_ref, ff_ref, loss_ref = refs
        else:
            h_ref, gpre_ref, gpost_ref, wgu_ref, wd_ref, out_ref, gu_ref, ff_ref = refs
        x = h_ref[...]
        cb = _rms_fwd(x, gpre_ref[...]).astype(BF16)
        acc = jnp.zeros((tm, D), F32)
        for j in range(half):
            g = _dot(cb, wgu_ref[j])
            u = _dot(cb, wgu_ref[j + half])
            gu_ref[j] = g.astype(BF16)
            gu_ref[j + half] = u.astype(BF16)
            act = (g * jax.nn.sigmoid(g) * u).astype(BF16)
            acc = acc + _dot(act, wd_ref[j * fb:(j + 1) * fb, :])
        ff_ref[...] = acc.astype(BF16)
        hout = x + _rms_fwd(acc, gpost_ref[...])
        if last:
            diff = hout - tgt_ref[...]
            out_ref[...] = diff * (1.0 / D)

            @pl.when(pl.program_id(0) == 0)
            def _():
                loss_ref[...] = jnp.zeros_like(loss_ref)

            loss_ref[...] += jnp.sum(diff * diff) * (0.5 / D)
        else:
            out_ref[...] = hout

    vec = _resident((1, D), lambda i: (0, 0))
    tile = pl.BlockSpec((tm, D), lambda i: (i, 0))
    in_specs = [tile, vec, vec,
                _resident((None, nblk, D, fb), lambda i: (layer, 0, 0, 0)),
                _resident((None,) + wd.shape[1:], lambda i: (layer, 0, 0))]
    out_specs = [tile, pl.BlockSpec((nblk, tm, fb), lambda i: (0, i, 0)), tile]
    out_shape = [jax.ShapeDtypeStruct((T, D), F32), jax.ShapeDtypeStruct((nblk, T, fb), BF16),
                 jax.ShapeDtypeStruct((T, D), BF16)]
    args = [h, g_pre, g_post, wgu, wd]
    if last:
        in_specs.append(tile)
        args.append(target)
        out_specs.append(pl.BlockSpec((8, 128), lambda i: (0, 0)))
        out_shape.append(jax.ShapeDtypeStruct((8, 128), F32))
    return pl.pallas_call(
        body, name=f"ffn_fwd_{layer}", grid=(T // tm,), in_specs=in_specs, out_specs=out_specs,
        out_shape=out_shape, compiler_params=_params(dimension_semantics=("arbitrary",)),
    )(*args)


def _ffn_bwd(dh, h, ff, gu, g_pre, g_post, wgu, wd, layer, *, tm):
    T, D = h.shape
    nblk, fb = wgu.shape[1], wgu.shape[3]
    half = nblk // 2

    def body(dh_ref, h_ref, ff_ref, gu_ref, gpre_ref, gpost_ref, wgu_ref, wd_ref,
             dx_ref, dgu_ref, dff_ref, c_ref, act_ref, small_ref):
        @pl.when(pl.program_id(0) == 0)
        def _():
            small_ref[...] = jnp.zeros_like(small_ref)

        dout = dh_ref[...]
        dff, dg_post = _rms_bwd(ff_ref[...].astype(F32), gpost_ref[...], dout)
        small_ref[1:2, :] += dg_post
        dffb = dff.astype(BF16)
        dff_ref[...] = dffb
        dc = jnp.zeros((tm, D), F32)
        for j in range(half):
            g = gu_ref[j].astype(F32)
            u = gu_ref[j + half].astype(F32)
            s = jax.nn.sigmoid(g)
            silu = g * s
            act_ref[j] = (silu * u).astype(BF16)
            dact = _dot_nt(dffb, wd_ref[j * fb:(j + 1) * fb, :])
            dg = (dact * u * (s * (1.0 + g * (1.0 - s)))).astype(BF16)
            du = (dact * silu).astype(BF16)
            dgu_ref[j] = dg
            dgu_ref[j + half] = du
            dc = dc + _dot_nt(dg, wgu_ref[j]) + _dot_nt(du, wgu_ref[j + half])
        x = h_ref[...]
        c_ref[...] = _rms_fwd(x, gpre_ref[...]).astype(BF16)
        dx, dg_pre = _rms_bwd(x, gpre_ref[...], dc)
        small_ref[0:1, :] += dg_pre
        dx_ref[...] = dout + dx

    vec = _resident((1, D), lambda i: (0, 0))
    tile = pl.BlockSpec((tm, D), lambda i: (i, 0))
    blk = pl.BlockSpec((nblk, tm, fb), lambda i: (0, i, 0))
    return pl.pallas_call(
        body, name=f"ffn_bwd_{layer}", grid=(T // tm,),
        in_specs=[tile, tile, tile, blk, vec, vec,
                  _resident((None, nblk, D, fb), lambda i: (layer, 0, 0, 0)),
                  _resident((None,) + wd.shape[1:], lambda i: (layer, 0, 0))],
        out_specs=[tile, blk, tile, tile, pl.BlockSpec((half, tm, fb), lambda i: (0, i, 0)),
                   pl.BlockSpec((8, D), lambda i: (0, 0))],
        out_shape=[jax.ShapeDtypeStruct((T, D), F32), jax.ShapeDtypeStruct((nblk, T, fb), BF16),
                   jax.ShapeDtypeStruct((T, D), BF16), jax.ShapeDtypeStruct((T, D), BF16),
                   jax.ShapeDtypeStruct((half, T, fb), BF16), jax.ShapeDtypeStruct((8, D), F32)],
        compiler_params=_params(dimension_semantics=("arbitrary",)),
    )(dh, h, ff, gu, g_pre, g_post, wgu, wd)


def _conv_fwd(h, g_pre, g_post, win, taps, wout, *, tm):
    T, D = h.shape
    nblk, cb = win.shape[0], win.shape[2]

    def body(h_ref, gpre_ref, gpost_ref, win_ref, taps_ref, wout_ref,
             out_ref, proj_ref, y_ref, proj_scr, ext_ref, carry_ref):
        i = pl.program_id(0)

        @pl.when(i == 0)
        def _():
            carry_ref[...] = jnp.zeros_like(carry_ref)

        x = h_ref[...]
        a = _rms_fwd(x, gpre_ref[...]).astype(BF16)
        for k in range(nblk):
            proj_scr[:, k * cb:(k + 1) * cb] = _dot(a, win_ref[k])
        proj_ref[...] = proj_scr[...].astype(BF16)
        u = proj_scr[:, D:2 * D] * proj_scr[:, 2 * D:3 * D]
        ext_ref[0:CONV_HALO, :] = carry_ref[...]
        ext_ref[CONV_HALO:, :] = u
        carry_ref[...] = u[tm - CONV_HALO:, :]
        e = ext_ref[...]
        conv = (taps_ref[2:3, :] * u + taps_ref[1:2, :] * pltpu.roll(e, 1, 0)[CONV_HALO:, :]
                + taps_ref[0:1, :] * pltpu.roll(e, 2, 0)[CONV_HALO:, :])
        z = (proj_scr[:, 0:D] * conv).astype(BF16)
        y = _dot(z, wout_ref[...])
        y_ref[...] = y.astype(BF16)
        out_ref[...] = x + _rms_fwd(y, gpost_ref[...])

    vec = _resident((1, D), lambda i: (0, 0))
    tile = pl.BlockSpec((tm, D), lambda i: (i, 0))
    return pl.pallas_call(
        body, name="conv_fwd", grid=(T // tm,),
        in_specs=[tile, vec, vec, _resident(win.shape, lambda i: (0, 0, 0)),
                  _resident(taps.shape, lambda i: (0, 0)), _resident(wout.shape, lambda i: (0, 0))],
        out_specs=[tile, pl.BlockSpec((tm, 3 * D), lambda i: (i, 0)), tile],
        out_shape=[jax.ShapeDtypeStruct((T, D), F32), jax.ShapeDtypeStruct((T, 3 * D), BF16),
                   jax.ShapeDtypeStruct((T, D), BF16)],
        scratch_shapes=[pltpu.VMEM((tm, 3 * D), F32), pltpu.VMEM((tm + CONV_HALO, D), F32),
                        pltpu.VMEM((CONV_HALO, D), F32)],
        compiler_params=_params(dimension_semantics=("arbitrary",)),
    )(h, g_pre, g_post, win, taps, wout)


def _conv_bwd(dh, h, y, proj, g_pre, g_post, win, taps, wout, *, tm):
    T, D = h.shape
    nblk, cb = win.shape[0], win.shape[2]
    nt = T // tm
    hb = tm // CONV_HALO
    n_ext = tm + CONV_HALO

    def body(dh_ref, h_ref, y_ref, proj_ref, halo_ref, gpre_ref, gpost_ref, win_ref, taps_ref, wout_ref,
             dx_ref, dproj_ref, z_ref, a_ref, dy_ref, small_ref, ext_ref, ext2_ref, carry_ref):
        i = pl.program_id(0)
        tile = nt - 1 - i

        @pl.when(i == 0)
        def _():
            small_ref[...] = jnp.zeros_like(small_ref)
            carry_ref[...] = jnp.zeros_like(carry_ref)

        dout = dh_ref[...]
        dy, dg_post = _rms_bwd(y_ref[...].astype(F32), gpost_ref[...], dout)
        small_ref[1:2, :] += dg_post
        dyb = dy.astype(BF16)
        dy_ref[...] = dyb
        dz = _dot_nt(dyb, wout_ref[...])
        bgate = proj_ref[:, 0:D].astype(F32)
        cgate = proj_ref[:, D:2 * D].astype(F32)
        v = proj_ref[:, 2 * D:3 * D].astype(F32)
        u = cgate * v
        uh = halo_ref[:, D:2 * D].astype(F32) * halo_ref[:, 2 * D:3 * D].astype(F32)
        ext_ref[0:CONV_HALO, :] = jnp.where(tile == 0, 0.0, uh)
        ext_ref[CONV_HALO:, :] = u
        e = ext_ref[...]
        u1 = pltpu.roll(e, 1, 0)[CONV_HALO:, :]
        u2 = pltpu.roll(e, 2, 0)[CONV_HALO:, :]
        t0, t1, t2 = taps_ref[0:1, :], taps_ref[1:2, :], taps_ref[2:3, :]
        conv = t2 * u + t1 * u1 + t0 * u2
        z_ref[...] = (bgate * conv).astype(BF16)
        dconv = dz * bgate
        small_ref[2:3, :] += jnp.sum(dconv * u2, axis=0, keepdims=True)
        small_ref[3:4, :] += jnp.sum(dconv * u1, axis=0, keepdims=True)
        small_ref[4:5, :] += jnp.sum(dconv * u, axis=0, keepdims=True)
        ext2_ref[0:tm, :] = dconv
        ext2_ref[tm:, :] = carry_ref[...]
        carry_ref[...] = dconv[0:CONV_HALO, :]
        e2 = ext2_ref[...]
        du = (t2 * dconv + t1 * pltpu.roll(e2, n_ext - 1, 0)[0:tm, :]
              + t0 * pltpu.roll(e2, n_ext - 2, 0)[0:tm, :])
        dproj_ref[:, 0:D] = (dz * conv).astype(BF16)
        dproj_ref[:, D:2 * D] = (du * v).astype(BF16)
        dproj_ref[:, 2 * D:3 * D] = (du * cgate).astype(BF16)
        da = jnp.zeros((tm, D), F32)
        for k in range(nblk):
            da = da + _dot_nt(dproj_ref[:, k * cb:(k + 1) * cb], win_ref[k])
        x = h_ref[...]
        a_ref[...] = _rms_fwd(x, gpre_ref[...]).astype(BF16)
        dx, dg_pre = _rms_bwd(x, gpre_ref[...], da)
        small_ref[0:1, :] += dg_pre
        dx_ref[...] = dout + dx

    vec = _resident((1, D), lambda i: (0, 0))
    rev = lambda i: (nt - 1 - i, 0)
    tile = pl.BlockSpec((tm, D), rev)
    wide = pl.BlockSpec((tm, 3 * D), rev)
    return pl.pallas_call(
        body, name="conv_bwd", grid=(nt,),
        in_specs=[tile, tile, tile, wide,
                  pl.BlockSpec((CONV_HALO, 3 * D), lambda i: (jnp.maximum((nt - 1 - i) * hb - 1, 0), 0)),
                  vec, vec, _resident(win.shape, lambda i: (0, 0, 0)),
                  _resident(taps.shape, lambda i: (0, 0)), _resident(wout.shape, lambda i: (0, 0))],
        out_specs=[tile, wide, tile, tile, tile, pl.BlockSpec((8, D), lambda i: (0, 0))],
        out_shape=[jax.ShapeDtypeStruct((T, D), F32), jax.ShapeDtypeStruct((T, 3 * D), BF16),
                   jax.ShapeDtypeStruct((T, D), BF16), jax.ShapeDtypeStruct((T, D), BF16),
                   jax.ShapeDtypeStruct((T, D), BF16), jax.ShapeDtypeStruct((8, D), F32)],
        scratch_shapes=[pltpu.VMEM((n_ext, D), F32), pltpu.VMEM((n_ext, D), F32),
                        pltpu.VMEM((CONV_HALO, D), F32)],
        compiler_params=_params(dimension_semantics=("arbitrary",)),
    )(dh, h, y, proj, proj, g_pre, g_post, win, taps, wout)


def _wgrad(name, a, b, a_spec, b_spec, out_block, n_out, *, tk):
    T = a.shape[-2]
    nk = T // tk

    def body(a_ref, b_ref, out_ref, acc_ref):
        k = pl.program_id(1)

        @pl.when(k == 0)
        def _():
            acc_ref[...] = jnp.zeros_like(acc_ref)

        acc_ref[...] += _dot_tn(a_ref[...], b_ref[...])

        @pl.when(k == nk - 1)
        def _():
            out_ref[...] = acc_ref[...].astype(BF16)

    return pl.pallas_call(
        body, name=name, grid=(n_out, nk), in_specs=[a_spec, b_spec],
        out_specs=pl.BlockSpec((None,) + out_block, lambda i, k: (i, 0, 0)),
        out_shape=jax.ShapeDtypeStruct((n_out,) + out_block, BF16),
        scratch_shapes=[pltpu.VMEM(out_block, F32)],
        compiler_params=_params(dimension_semantics=("arbitrary", "arbitrary")),
    )(a, b)


def _exchange(name, srcs, src_at, out_shapes, dst_at):
    n_arr = len(srcs)

    def body(*refs):
        src_refs, out_refs = refs[:n_arr], refs[n_arr:2 * n_arr]
        send_sems, recv_sems, local_sems = refs[2 * n_arr:]
        x, y, c = lax.axis_index("x"), lax.axis_index("y"), lax.axis_index("c")
        me = 4 * x + 2 * y + c

        def peer(m):
            flip = lambda v, bit: (1 - v) if bit else v
            return (flip(x, m & 4), flip(y, m & 2), flip(c, m & 1)), me ^ m

        def remote(n, m):
            dev, idx = peer(m)
            return pltpu.make_async_remote_copy(
                src_ref=src_at[n](src_refs[n], idx), dst_ref=dst_at[n](out_refs[n], me),
                send_sem=send_sems.at[n, m - 1], recv_sem=recv_sems.at[n, m - 1],
                device_id=dev, device_id_type=MESH)

        def arrival(n, m):
            dev, idx = peer(m)
            return pltpu.make_async_remote_copy(
                src_ref=src_at[n](src_refs[n], me), dst_ref=dst_at[n](out_refs[n], idx),
                send_sem=send_sems.at[n, m - 1], recv_sem=recv_sems.at[n, m - 1],
                device_id=dev, device_id_type=MESH)

        local = [pltpu.make_async_copy(src_at[n](src_refs[n], me), dst_at[n](out_refs[n], me), local_sems.at[n])
                 for n in range(n_arr)]
        sends = [remote(n, m) for n in range(n_arr) for m in range(1, N_DEV)]
        for cp in local + sends:
            cp.start()
        for n in range(n_arr):
            for m in range(1, N_DEV):
                arrival(n, m).wait_recv()
        for cp in sends:
            cp.wait_send()
        for cp in local:
            cp.wait()

    hbm = pl.BlockSpec(memory_space=pltpu.HBM)
    return pl.pallas_call(
        body, name=name, in_specs=[hbm] * n_arr, out_specs=[hbm] * n_arr,
        out_shape=out_shapes,
        scratch_shapes=[pltpu.SemaphoreType.DMA((n_arr, N_DEV - 1)), pltpu.SemaphoreType.DMA((n_arr, N_DEV - 1)),
                        pltpu.SemaphoreType.DMA((n_arr,))],
        compiler_params=pltpu.CompilerParams(has_side_effects=True),
    )(*srcs)


def _adamw(w, g, m, v):
    m = ADAM_B1 * m + (1.0 - ADAM_B1) * g
    v = ADAM_B2 * v + (1.0 - ADAM_B2) * (g * g)
    m_hat = m / (1.0 - ADAM_B1 ** ADAM_STEP)
    v_hat = v / (1.0 - ADAM_B2 ** ADAM_STEP)
    delta = -ADAM_LR * (m_hat / (jnp.sqrt(v_hat) + ADAM_EPS) + ADAM_WD * w)
    return delta, m, v


def _reduce_adam(name, parts, w, m, v, *, tr):
    L, _, R, C = parts.shape
    tr = min(tr, R)

    def body(p_ref, w_ref, m_ref, v_ref, g_ref, d_ref, nm_ref, nv_ref):
        g = p_ref[0].astype(F32)
        for s in range(1, N_DEV):
            g = g + p_ref[s].astype(F32)
        g_ref[...] = g
        d_ref[...], nm_ref[...], nv_ref[...] = _adamw(w_ref[...], g, m_ref[...], v_ref[...])

    blk = pl.BlockSpec((None, tr, C), lambda l, r: (l, r, 0))
    out = jax.ShapeDtypeStruct((L, R, C), F32)
    return pl.pallas_call(
        body, name=name, grid=(L, R // tr),
        in_specs=[pl.BlockSpec((None, N_DEV, tr, C), lambda l, r: (l, 0, r, 0)), blk, blk, blk],
        out_specs=[blk] * 4, out_shape=[out] * 4,
        compiler_params=_params(dimension_semantics=("arbitrary", "arbitrary")),
    )(parts, w, m, v)


def _small_reduce(parts):
    D = parts.shape[2]
    rows = [0, 1, 8, 9, 16, 17, 24, 25, 18, 19, 20, 2]

    def body(p_ref, out_ref):
        s = p_ref[0]
        for d in range(1, N_DEV):
            s = s + p_ref[d]
        out_ref[...] = jnp.zeros_like(out_ref)
        for r, src in enumerate(rows):
            out_ref[r:r + 1, :] = s[src:src + 1, :]

    return pl.pallas_call(body, name="small_reduce", out_shape=jax.ShapeDtypeStruct((16, D), F32))(parts)


def _small_adam(g_gain, g_taps, g_scale, gains, taps, scale):
    def body(gg, gt, gs, wg, mg, vg, wt, mt, vt, ws, ms, vs, *outs):
        for k, (g, w, m, v) in enumerate(((gg, wg, mg, vg), (gt, wt, mt, vt), (gs, ws, ms, vs))):
            outs[3 * k][...], outs[3 * k + 1][...], outs[3 * k + 2][...] = _adamw(w[...], g[...], m[...], v[...])

    shapes = [jax.ShapeDtypeStruct(t[0].shape, F32) for t in (gains, taps, scale) for _ in range(3)]
    return pl.pallas_call(body, name="small_adam", out_shape=shapes)(g_gain, g_taps, g_scale, *gains, *taps, *scale)


def kernel(x, norm_gains, pool_w, pool_scale, conv_in_w, conv_w, conv_out_w, ffn_gate_up_w, ffn_down_w, loss_target, m_norm_gains, m_pool_w, m_pool_scale, m_conv_in_w, m_conv_w, m_conv_out_w, m_ffn_gate_up_w, m_ffn_down_w, v_norm_gains, v_pool_w, v_pool_scale, v_conv_in_w, v_conv_w, v_conv_out_w, v_ffn_gate_up_w, v_ffn_down_w):
    T, D = x.shape[1], x.shape[2]
    tm = min(256, T)
    tk = min(512, T)
    n_layers = ffn_gate_up_w.shape[0]
    fb = ffn_gate_up_w.shape[2]
    fr = ffn_down_w.shape[1]
    dcol = norm_gains.shape[2]
    cb = conv_in_w.shape[2]
    gw = pool_w.shape[3]
    me = 4 * lax.axis_index("x") + 2 * lax.axis_index("y") + lax.axis_index("c")

    small_w = jnp.concatenate([norm_gains.reshape(8, dcol), jnp.pad(conv_w[0], ((0, 5), (0, 0)))], axis=0)
    shards = [pool_w[0].astype(BF16), conv_in_w[0].astype(BF16), conv_out_w[0].astype(BF16),
              ffn_gate_up_w.astype(BF16), ffn_down_w.astype(BF16), small_w]
    whole = lambda ref, k: ref
    lead = lambda ref, k: ref.at[k]
    second = lambda ref, k: ref.at[:, k]
    pw_g, win_g, wout_g, wgu_g, wd_g, small_g = _exchange(
        "gather_weights", shards, [whole] * 6,
        [jax.ShapeDtypeStruct((4, N_DEV, gw // N_DEV, gw), BF16), jax.ShapeDtypeStruct((N_DEV, D, cb), BF16),
         jax.ShapeDtypeStruct((N_DEV, dcol, D), BF16), jax.ShapeDtypeStruct((n_layers, N_DEV, D, fb), BF16),
         jax.ShapeDtypeStruct((n_layers, N_DEV, fr, D), BF16), jax.ShapeDtypeStruct((N_DEV, 16, dcol), F32)],
        [second, lead, lead, second, second, lead])
    pw = pw_g.reshape(4, gw, gw)
    wout = wout_g.reshape(D, D)
    wd = wd_g.reshape(n_layers, N_DEV * fr, D)
    small_full = jnp.swapaxes(small_g, 0, 1).reshape(16, D)
    gain = lambda l, s: small_full[4 * l + s][None, :]
    taps = small_full[8:16]

    h0 = x[0]
    h1 = _pool_fwd(h0, pw, pool_scale, gain(0, 0), gain(0, 1), tm=tm)
    h2, gu0, ff0 = _ffn_fwd(h1, gain(0, 2), gain(0, 3), wgu_g, wd, 0, None, tm=tm)
    h3, proj, y = _conv_fwd(h2, gain(1, 0), gain(1, 1), win_g, taps, wout, tm=tm)
    dh4, gu1, ff1, loss_part = _ffn_fwd(h3, gain(1, 2), gain(1, 3), wgu_g, wd, 1, loss_target[0], tm=tm)
    loss = lax.psum(loss_part[0, 0], ("x", "y", "c"))

    seq = lambda i, k: (k, 0)
    dh3, dgu1, dff1, c1, act1, small_f1 = _ffn_bwd(dh4, h3, ff1, gu1, gain(1, 2), gain(1, 3), wgu_g, wd, 1, tm=tm)
    g_wgu1 = _wgrad("wgrad_gate_up_1", c1, dgu1, pl.BlockSpec((tk, D), seq),
                    pl.BlockSpec((None, tk, fb), lambda i, k: (i, k, 0)), (D, fb), N_DEV, tk=tk)
    g_wd1 = _wgrad("wgrad_down_1", act1, dff1, pl.BlockSpec((None, tk, fb), lambda i, k: (i, k, 0)),
                   pl.BlockSpec((tk, D), seq), (fb, D), N_DEV // 2, tk=tk)
    dh2, dproj, z, a1, dy, small_c = _conv_bwd(dh3, h2, y, proj, gain(1, 0), gain(1, 1), win_g, taps, wout, tm=tm)
    g_win = _wgrad("wgrad_conv_in", a1, dproj, pl.BlockSpec((tk, D), seq),
                   pl.BlockSpec((tk, cb), lambda i, k: (k, i)), (D, cb), N_DEV, tk=tk)
    g_wout = _wgrad("wgrad_conv_out", z, dy, pl.BlockSpec((tk, D), seq), pl.BlockSpec((tk, D), seq), (D, D), 1, tk=tk)
    dh1, dgu0, dff0, c0, act0, small_f0 = _ffn_bwd(dh2, h1, ff0, gu0, gain(0, 2), gain(0, 3), wgu_g, wd, 0, tm=tm)
    g_wgu0 = _wgrad("wgrad_gate_up_0", c0, dgu0, pl.BlockSpec((tk, D), seq),
                    pl.BlockSpec((None, tk, fb), lambda i, k: (i, k, 0)), (D, fb), N_DEV, tk=tk)
    g_wd0 = _wgrad("wgrad_down_0", act0, dff0, pl.BlockSpec((None, tk, fb), lambda i, k: (i, k, 0)),
                   pl.BlockSpec((tk, D), seq), (fb, D), N_DEV // 2, tk=tk)
    grad_x, g_pw, small_p = _pool_bwd(dh1, h0, pw, pool_scale, gain(0, 0), gain(0, 1), tm=tm)

    small_part = jnp.concatenate([small_p, small_f0, small_c, small_f1], axis=0)
    grads = [g_pw.reshape(4, N_DEV, gw // N_DEV, gw), g_win, g_wout.reshape(N_DEV, dcol, D),
             g_wgu0, g_wgu1, g_wd0.reshape(N_DEV, fr, D), g_wd1.reshape(N_DEV, fr, D), small_part]
    r_pw, r_win, r_wout, r_wgu0, r_wgu1, r_wd0, r_wd1, r_small = _exchange(
        "scatter_grads", grads, [second, lead, lead, lead, lead, lead, lead, whole],
        [jax.ShapeDtypeStruct((4, N_DEV, gw // N_DEV, gw), BF16), jax.ShapeDtypeStruct((N_DEV, D, cb), BF16),
         jax.ShapeDtypeStruct((N_DEV, dcol, D), BF16), jax.ShapeDtypeStruct((N_DEV, D, fb), BF16),
         jax.ShapeDtypeStruct((N_DEV, D, fb), BF16), jax.ShapeDtypeStruct((N_DEV, fr, D), BF16),
         jax.ShapeDtypeStruct((N_DEV, fr, D), BF16), jax.ShapeDtypeStruct((N_DEV, 32, D), F32)],
        [second, lead, lead, lead, lead, lead, lead, lead])

    o_pw = _reduce_adam("adam_pool_w", r_pw, pool_w[0], m_pool_w[0], v_pool_w[0], tr=32)
    o_win = _reduce_adam("adam_conv_in", r_win[None], conv_in_w, m_conv_in_w, v_conv_in_w, tr=256)
    o_wout = _reduce_adam("adam_conv_out", r_wout[None], conv_out_w, m_conv_out_w, v_conv_out_w, tr=128)
    o_wgu = _reduce_adam("adam_gate_up", jnp.stack([r_wgu0, r_wgu1]), ffn_gate_up_w, m_ffn_gate_up_w, v_ffn_gate_up_w, tr=256)
    o_wd = _reduce_adam("adam_down", jnp.stack([r_wd0, r_wd1]), ffn_down_w, m_ffn_down_w, v_ffn_down_w, tr=176)
    g_small = _small_reduce(r_small)
    g_cols = lax.dynamic_slice(g_small, (0, me * dcol), (16, dcol))
    o_small = _small_adam(
        g_cols[0:8], g_cols[8:11], g_small[11:12],
        (norm_gains.reshape(8, dcol), m_norm_gains.reshape(8, dcol), v_norm_gains.reshape(8, dcol)),
        (conv_w[0], m_conv_w[0], v_conv_w[0]), (pool_scale, m_pool_scale, v_pool_scale))
    d_gain, nm_gain, nv_gain, d_taps, nm_taps, nv_taps, d_scale, nm_scale, nv_scale = o_small

    gshape = norm_gains.shape
    per = lambda k: (
        (g_cols[0:8].reshape(gshape), d_gain.reshape(gshape), nm_gain.reshape(gshape), nv_gain.reshape(gshape))[k],
        o_pw[k][None], (g_small[11:12], d_scale, nm_scale, nv_scale)[k], o_win[k],
        (g_cols[8:11][None], d_taps[None], nm_taps[None], nv_taps[None])[k], o_wout[k], o_wgu[k], o_wd[k])
    return (loss, grad_x[None], *per(0), *per(1), *per(2), *per(3))
```

```python
import collections
import functools

import jax
import jax.numpy as jnp
from jax import lax
from jax.experimental import pallas as pl
from jax.experimental.pallas import tpu as pltpu

N_DEV = 8
RMS_EPS = 1e-6
POOL_WINDOWS = (2, 4, 8, 16)
POOL_HALO = 16
CONV_HALO = 16
ADAM_LR, ADAM_B1, ADAM_B2, ADAM_EPS, ADAM_WD, ADAM_STEP = 0.001, 0.9, 0.999, 1e-08, 0.01, 10

VMEM_LIMIT = 56 * 2**20
BF16 = jnp.bfloat16
F32 = jnp.float32
MESH = pl.DeviceIdType.MESH


def _params(**kw):
    return pltpu.CompilerParams(vmem_limit_bytes=VMEM_LIMIT, **kw)


def _resident(shape, index_map):
    return pl.BlockSpec(shape, index_map, pipeline_mode=pl.Buffered(1))


def _ordered(body, n_in, after):
    if after is None:
        return functools.partial(body), [], []
    return (lambda *refs: body(*refs[:n_in], *refs[n_in + 1:])), [after], [pl.BlockSpec(memory_space=pl.ANY)]


def _rms_fwd(x, g):
    r = lax.rsqrt(jnp.mean(x * x, axis=-1, keepdims=True) + RMS_EPS)
    return x * r * g


def _rms_bwd(x, g, dy):
    r = lax.rsqrt(jnp.mean(x * x, axis=-1, keepdims=True) + RMS_EPS)
    xhat = x * r
    dg = jnp.sum(dy * xhat, axis=0, keepdims=True)
    t = dy * g
    dx = r * (t - xhat * jnp.mean(t * xhat, axis=-1, keepdims=True))
    return dx, dg


def _dot(a, b):
    return jnp.dot(a, b, preferred_element_type=F32)


def _dot_nt(a, b):
    return lax.dot_general(a, b, (((1,), (1,)), ((), ())), preferred_element_type=F32)


def _dot_tn(a, b):
    return lax.dot_general(a, b, (((0,), (0,)), ((), ())), preferred_element_type=F32)


def _row_inverse_counts(tile, tm):
    pos = (lax.broadcasted_iota(jnp.int32, (tm, 1), 0) + tile * tm + 1).astype(F32)
    return [1.0 / jnp.minimum(pos, float(w)) for w in POOL_WINDOWS]


def _pool_from_ext(ext, a, invs, gw):
    s = ext
    outs = []
    for g, w in enumerate(POOL_WINDOWS):
        s = s[:, (gw if g else 0):]
        s = s + pltpu.roll(s, w // 2, 0)
        outs.append(s[POOL_HALO:, :gw] * invs[g] - a[:, g * gw:(g + 1) * gw])
    return outs


def _pool_fwd(h, pw, scale, g_pre, g_post, *, tm, after=None):
    T, D = h.shape
    gw = D // len(POOL_WINDOWS)
    hb = tm // POOL_HALO

    def body(h_ref, halo_ref, pw_ref, scale_ref, gpre_ref, gpost_ref, out_ref, ext_ref):
        i = pl.program_id(0)
        x = h_ref[...]
        a = _rms_fwd(x, gpre_ref[...])
        ah = _rms_fwd(halo_ref[...], gpre_ref[...])
        ext_ref[0:POOL_HALO, :] = jnp.where(i == 0, 0.0, ah)
        ext_ref[POOL_HALO:, :] = a
        pooled = _pool_from_ext(ext_ref[...], a, _row_inverse_counts(i, tm), gw)
        mixed = jnp.concatenate([_dot(p.astype(BF16), pw_ref[g]) for g, p in enumerate(pooled)], axis=1)
        out_ref[...] = x + _rms_fwd(mixed * scale_ref[...], gpost_ref[...])

    vec = _resident((1, D), lambda i: (0, 0))
    fn, xa, xs = _ordered(body, 6, after)
    return pl.pallas_call(
        fn, name="pool_fwd", grid=(T // tm,),
        in_specs=[pl.BlockSpec((tm, D), lambda i: (i, 0)),
                  pl.BlockSpec((POOL_HALO, D), lambda i: (jnp.maximum(i * hb - 1, 0), 0)),
                  _resident(pw.shape, lambda i: (0, 0, 0)), vec, vec, vec] + xs,
        out_specs=pl.BlockSpec((tm, D), lambda i: (i, 0)),
        out_shape=jax.ShapeDtypeStruct((T, D), F32),
        scratch_shapes=[pltpu.VMEM((tm + POOL_HALO, D), F32)],
        compiler_params=_params(dimension_semantics=("arbitrary",)),
    )(h, h, pw, scale, g_pre, g_post, *xa)


def _pool_bwd(dh, h, pw, scale, g_pre, g_post, *, tm, after=None):
    T, D = h.shape
    gw = D // len(POOL_WINDOWS)
    hb = tm // POOL_HALO
    nt = T // tm
    n_ext = tm + POOL_HALO

    def body(dh_ref, h_ref, halo_ref, pw_ref, scale_ref, gpre_ref, gpost_ref,
             dx_ref, dpw_ref, small_ref, ext_ref, ext2_ref, carry_ref, dpw_acc):
        i = pl.program_id(0)
        tile = nt - 1 - i

        @pl.when(i == 0)
        def _():
            small_ref[...] = jnp.zeros_like(small_ref)
            dpw_acc[...] = jnp.zeros_like(dpw_acc)
            carry_ref[...] = jnp.zeros_like(carry_ref)

        x = h_ref[...]
        dout = dh_ref[...]
        a = _rms_fwd(x, gpre_ref[...])
        ah = _rms_fwd(halo_ref[...], gpre_ref[...])
        ext_ref[0:POOL_HALO, :] = jnp.where(tile == 0, 0.0, ah)
        ext_ref[POOL_HALO:, :] = a
        invs = _row_inverse_counts(tile, tm)
        pooled = [p.astype(BF16) for p in _pool_from_ext(ext_ref[...], a, invs, gw)]
        mixed_pre = jnp.concatenate([_dot(p, pw_ref[g]) for g, p in enumerate(pooled)], axis=1)
        scale_v = scale_ref[...]
        dmixed, dg_post = _rms_bwd(mixed_pre * scale_v, gpost_ref[...], dout)
        small_ref[1:2, :] += dg_post
        small_ref[2:3, :] += jnp.sum(dmixed * mixed_pre, axis=0, keepdims=True)
        dpre = (dmixed * scale_v).astype(BF16)
        dpooled = []
        for g in range(len(POOL_WINDOWS)):
            dp = dpre[:, g * gw:(g + 1) * gw]
            dpw_acc[g] += _dot_tn(pooled[g], dp)
            dpooled.append(_dot_nt(dp, pw_ref[g]))
        q = jnp.concatenate([d * invs[g] for g, d in enumerate(dpooled)], axis=1)
        ext2_ref[0:tm, :] = q
        ext2_ref[tm:, :] = carry_ref[...]
        carry_ref[...] = q[0:POOL_HALO, :]
        s = ext2_ref[...]
        da = []
        for g, w in enumerate(POOL_WINDOWS):
            s = s[:, (gw if g else 0):]
            s = s + pltpu.roll(s, n_ext - w // 2, 0)
            da.append(s[0:tm, :gw] - dpooled[g])
        dx, dg_pre = _rms_bwd(x, gpre_ref[...], jnp.concatenate(da, axis=1))
        small_ref[0:1, :] += dg_pre
        dx_ref[...] = dout + dx

        @pl.when(i == nt - 1)
        def _():
            dpw_ref[...] = dpw_acc[...].astype(BF16)

    vec = _resident((1, D), lambda i: (0, 0))
    rev = lambda i: (nt - 1 - i, 0)
    fn, xa, xs = _ordered(body, 7, after)
    return pl.pallas_call(
        fn, name="pool_bwd", grid=(nt,),
        in_specs=[pl.BlockSpec((tm, D), rev), pl.BlockSpec((tm, D), rev),
                  pl.BlockSpec((POOL_HALO, D), lambda i: (jnp.maximum((nt - 1 - i) * hb - 1, 0), 0)),
                  _resident(pw.shape, lambda i: (0, 0, 0)), vec, vec, vec] + xs,
        out_specs=[pl.BlockSpec((tm, D), rev),
                   pl.BlockSpec(pw.shape, lambda i: (0, 0, 0)),
                   pl.BlockSpec((8, D), lambda i: (0, 0))],
        out_shape=[jax.ShapeDtypeStruct((T, D), F32), jax.ShapeDtypeStruct(pw.shape, BF16),
                   jax.ShapeDtypeStruct((8, D), F32)],
        scratch_shapes=[pltpu.VMEM((n_ext, D), F32), pltpu.VMEM((n_ext, D), F32),
                        pltpu.VMEM((POOL_HALO, D), F32), pltpu.VMEM(pw.shape, F32)],
        compiler_params=_params(dimension_semantics=("arbitrary",)),
    )(dh, h, h, pw, scale, g_pre, g_post, *xa)


def _ffn_fwd(h, g_pre, g_post, wgu, wd, layer, target, *, tm, after=None):
    T, D = h.shape
    nblk, fb = wgu.shape[0], wgu.shape[2]
    half = nblk // 2
    last = target is not None

    def body(*refs):
        if last:
            h_ref, gpre_ref, gpost_ref, wgu_ref, wd_ref, tgt_ref, out_ref, gu_ref, ff_ref, loss_ref = refs
        else:
            h_ref, gpre_ref, gpost_ref, wgu_ref, wd_ref, out_ref, gu_ref, ff_ref = refs
        x = h_ref[...]
        cb = _rms_fwd(x, gpre_ref[...]).astype(BF16)
        acc = jnp.zeros((tm, D), F32)
        for j in range(half):
            g = _dot(cb, wgu_ref[j])
            u = _dot(cb, wgu_ref[j + half])
            gu_ref[j] = g.astype(BF16)
            gu_ref[j + half] = u.astype(BF16)
            act = (g * jax.nn.sigmoid(g) * u).astype(BF16)
            acc = acc + _dot(act, wd_ref[j * fb:(j + 1) * fb, :])
        ff_ref[...] = acc.astype(BF16)
        hout = x + _rms_fwd(acc, gpost_ref[...])
        if last:
            diff = hout - tgt_ref[...]
            out_ref[...] = diff * (1.0 / D)

            @pl.when(pl.program_id(0) == 0)
            def _():
                loss_ref[...] = jnp.zeros_like(loss_ref)

            loss_ref[...] += jnp.sum(diff * diff) * (0.5 / D)
        else:
            out_ref[...] = hout

    vec = _resident((1, D), lambda i: (0, 0))
    tile = pl.BlockSpec((tm, D), lambda i: (i, 0))
    in_specs = [tile, vec, vec,
                _resident(wgu.shape, lambda i: (0, 0, 0)), _resident(wd.shape, lambda i: (0, 0))]
    out_specs = [tile, pl.BlockSpec((nblk, tm, fb), lambda i: (0, i, 0)), tile]
    out_shape = [jax.ShapeDtypeStruct((T, D), F32), jax.ShapeDtypeStruct((nblk, T, fb), BF16),
                 jax.ShapeDtypeStruct((T, D), BF16)]
    args = [h, g_pre, g_post, wgu, wd]
    if last:
        in_specs.append(tile)
        args.append(target)
        out_specs.append(pl.BlockSpec((8, 128), lambda i: (0, 0)))
        out_shape.append(jax.ShapeDtypeStruct((8, 128), F32))
    fn, xa, xs = _ordered(body, len(args), after)
    return pl.pallas_call(
        fn, name=f"ffn_fwd_{layer}", grid=(T // tm,), in_specs=in_specs + xs, out_specs=out_specs,
        out_shape=out_shape, compiler_params=_params(dimension_semantics=("arbitrary",)),
    )(*args, *xa)


def _ffn_bwd(dh, h, ff, gu, g_pre, g_post, wgu, wd, layer, *, tm, after=None):
    T, D = h.shape
    nblk, fb = wgu.shape[0], wgu.shape[2]
    half = nblk // 2

    def body(dh_ref, h_ref, ff_ref, gu_ref, gpre_ref, gpost_ref, wgu_ref, wd_ref,
             dx_ref, dgu_ref, dff_ref, c_ref, act_ref, small_ref):
        @pl.when(pl.program_id(0) == 0)
        def _():
            small_ref[...] = jnp.zeros_like(small_ref)

        dout = dh_ref[...]
        dff, dg_post = _rms_bwd(ff_ref[...].astype(F32), gpost_ref[...], dout)
        small_ref[1:2, :] += dg_post
        dffb = dff.astype(BF16)
        dff_ref[...] = dffb
        dc = jnp.zeros((tm, D), F32)
        for j in range(half):
            g = gu_ref[j].astype(F32)
            u = gu_ref[j + half].astype(F32)
            s = jax.nn.sigmoid(g)
            silu = g * s
            act_ref[j] = (silu * u).astype(BF16)
            dact = _dot_nt(dffb, wd_ref[j * fb:(j + 1) * fb, :])
            dg = (dact * u * (s * (1.0 + g * (1.0 - s)))).astype(BF16)
            du = (dact * silu).astype(BF16)
            dgu_ref[j] = dg
            dgu_ref[j + half] = du
            dc = dc + _dot_nt(dg, wgu_ref[j]) + _dot_nt(du, wgu_ref[j + half])
        x = h_ref[...]
        c_ref[...] = _rms_fwd(x, gpre_ref[...]).astype(BF16)
        dx, dg_pre = _rms_bwd(x, gpre_ref[...], dc)
        small_ref[0:1, :] += dg_pre
        dx_ref[...] = dout + dx

    vec = _resident((1, D), lambda i: (0, 0))
    tile = pl.BlockSpec((tm, D), lambda i: (i, 0))
    blk = pl.BlockSpec((nblk, tm, fb), lambda i: (0, i, 0))
    fn, xa, xs = _ordered(body, 8, after)
    return pl.pallas_call(
        fn, name=f"ffn_bwd_{layer}", grid=(T // tm,),
        in_specs=[tile, tile, tile, blk, vec, vec,
                  _resident(wgu.shape, lambda i: (0, 0, 0)), _resident(wd.shape, lambda i: (0, 0))] + xs,
        out_specs=[tile, blk, tile, tile, pl.BlockSpec((half, tm, fb), lambda i: (0, i, 0)),
                   pl.BlockSpec((8, D), lambda i: (0, 0))],
        out_shape=[jax.ShapeDtypeStruct((T, D), F32), jax.ShapeDtypeStruct((nblk, T, fb), BF16),
                   jax.ShapeDtypeStruct((T, D), BF16), jax.ShapeDtypeStruct((T, D), BF16),
                   jax.ShapeDtypeStruct((half, T, fb), BF16), jax.ShapeDtypeStruct((8, D), F32)],
        compiler_params=_params(dimension_semantics=("arbitrary",)),
    )(dh, h, ff, gu, g_pre, g_post, wgu, wd, *xa)


def _conv_fwd(h, g_pre, g_post, win, taps, wout, *, tm, after=None):
    T, D = h.shape
    nblk, cb = win.shape[0], win.shape[2]

    def body(h_ref, gpre_ref, gpost_ref, win_ref, taps_ref, wout_ref,
             out_ref, proj_ref, y_ref, proj_scr, ext_ref, carry_ref):
        i = pl.program_id(0)

        @pl.when(i == 0)
        def _():
            carry_ref[...] = jnp.zeros_like(carry_ref)

        x = h_ref[...]
        a = _rms_fwd(x, gpre_ref[...]).astype(BF16)
        for k in range(nblk):
            proj_scr[:, k * cb:(k + 1) * cb] = _dot(a, win_ref[k])
        proj_ref[...] = proj_scr[...].astype(BF16)
        u = proj_scr[:, D:2 * D] * proj_scr[:, 2 * D:3 * D]
        ext_ref[0:CONV_HALO, :] = carry_ref[...]
        ext_ref[CONV_HALO:, :] = u
        carry_ref[...] = u[tm - CONV_HALO:, :]
        e = ext_ref[...]
        conv = (taps_ref[2:3, :] * u + taps_ref[1:2, :] * pltpu.roll(e, 1, 0)[CONV_HALO:, :]
                + taps_ref[0:1, :] * pltpu.roll(e, 2, 0)[CONV_HALO:, :])
        z = (proj_scr[:, 0:D] * conv).astype(BF16)
        y = _dot(z, wout_ref[...])
        y_ref[...] = y.astype(BF16)
        out_ref[...] = x + _rms_fwd(y, gpost_ref[...])

    vec = _resident((1, D), lambda i: (0, 0))
    tile = pl.BlockSpec((tm, D), lambda i: (i, 0))
    fn, xa, xs = _ordered(body, 6, after)
    return pl.pallas_call(
        fn, name="conv_fwd", grid=(T // tm,),
        in_specs=[tile, vec, vec, _resident(win.shape, lambda i: (0, 0, 0)),
                  _resident(taps.shape, lambda i: (0, 0)), _resident(wout.shape, lambda i: (0, 0))] + xs,
        out_specs=[tile, pl.BlockSpec((tm, 3 * D), lambda i: (i, 0)), tile],
        out_shape=[jax.ShapeDtypeStruct((T, D), F32), jax.ShapeDtypeStruct((T, 3 * D), BF16),
                   jax.ShapeDtypeStruct((T, D), BF16)],
        scratch_shapes=[pltpu.VMEM((tm, 3 * D), F32), pltpu.VMEM((tm + CONV_HALO, D), F32),
                        pltpu.VMEM((CONV_HALO, D), F32)],
        compiler_params=_params(dimension_semantics=("arbitrary",)),
    )(h, g_pre, g_post, win, taps, wout, *xa)


def _conv_bwd(dh, h, y, proj, g_pre, g_post, win, taps, wout, *, tm, after=None):
    T, D = h.shape
    nblk, cb = win.shape[0], win.shape[2]
    nt = T // tm
    hb = tm // CONV_HALO
    n_ext = tm + CONV_HALO

    def body(dh_ref, h_ref, y_ref, proj_ref, halo_ref, gpre_ref, gpost_ref, win_ref, taps_ref, wout_ref,
             dx_ref, dproj_ref, z_ref, a_ref, dy_ref, small_ref, ext_ref, ext2_ref, carry_ref):
        i = pl.program_id(0)
        tile = nt - 1 - i

        @pl.when(i == 0)
        def _():
            small_ref[...] = jnp.zeros_like(small_ref)
            carry_ref[...] = jnp.zeros_like(carry_ref)

        dout = dh_ref[...]
        dy, dg_post = _rms_bwd(y_ref[...].astype(F32), gpost_ref[...], dout)
        small_ref[1:2, :] += dg_post
        dyb = dy.astype(BF16)
        dy_ref[...] = dyb
        dz = _dot_nt(dyb, wout_ref[...])
        bgate = proj_ref[:, 0:D].astype(F32)
        cgate = proj_ref[:, D:2 * D].astype(F32)
        v = proj_ref[:, 2 * D:3 * D].astype(F32)
        u = cgate * v
        uh = halo_ref[:, D:2 * D].astype(F32) * halo_ref[:, 2 * D:3 * D].astype(F32)
        ext_ref[0:CONV_HALO, :] = jnp.where(tile == 0, 0.0, uh)
        ext_ref[CONV_HALO:, :] = u
        e = ext_ref[...]
        u1 = pltpu.roll(e, 1, 0)[CONV_HALO:, :]
        u2 = pltpu.roll(e, 2, 0)[CONV_HALO:, :]
        t0, t1, t2 = taps_ref[0:1, :], taps_ref[1:2, :], taps_ref[2:3, :]
        conv = t2 * u + t1 * u1 + t0 * u2
        z_ref[...] = (bgate * conv).astype(BF16)
        dconv = dz * bgate
        small_ref[2:3, :] += jnp.sum(dconv * u2, axis=0, keepdims=True)
        small_ref[3:4, :] += jnp.sum(dconv * u1, axis=0, keepdims=True)
        small_ref[4:5, :] += jnp.sum(dconv * u, axis=0, keepdims=True)
        ext2_ref[0:tm, :] = dconv
        ext2_ref[tm:, :] = carry_ref[...]
        carry_ref[...] = dconv[0:CONV_HALO, :]
        e2 = ext2_ref[...]
        du = (t2 * dconv + t1 * pltpu.roll(e2, n_ext - 1, 0)[0:tm, :]
              + t0 * pltpu.roll(e2, n_ext - 2, 0)[0:tm, :])
        dproj_ref[:, 0:D] = (dz * conv).astype(BF16)
        dproj_ref[:, D:2 * D] = (du * v).astype(BF16)
        dproj_ref[:, 2 * D:3 * D] = (du * cgate).astype(BF16)
        da = jnp.zeros((tm, D), F32)
        for k in range(nblk):
            da = da + _dot_nt(dproj_ref[:, k * cb:(k + 1) * cb], win_ref[k])
        x = h_ref[...]
        a_ref[...] = _rms_fwd(x, gpre_ref[...]).astype(BF16)
        dx, dg_pre = _rms_bwd(x, gpre_ref[...], da)
        small_ref[0:1, :] += dg_pre
        dx_ref[...] = dout + dx

    vec = _resident((1, D), lambda i: (0, 0))
    rev = lambda i: (nt - 1 - i, 0)
    tile = pl.BlockSpec((tm, D), rev)
    wide = pl.BlockSpec((tm, 3 * D), rev)
    fn, xa, xs = _ordered(body, 10, after)
    return pl.pallas_call(
        fn, name="conv_bwd", grid=(nt,),
        in_specs=[tile, tile, tile, wide,
                  pl.BlockSpec((CONV_HALO, 3 * D), lambda i: (jnp.maximum((nt - 1 - i) * hb - 1, 0), 0)),
                  vec, vec, _resident(win.shape, lambda i: (0, 0, 0)),
                  _resident(taps.shape, lambda i: (0, 0)), _resident(wout.shape, lambda i: (0, 0))] + xs,
        out_specs=[tile, wide, tile, tile, tile, pl.BlockSpec((8, D), lambda i: (0, 0))],
        out_shape=[jax.ShapeDtypeStruct((T, D), F32), jax.ShapeDtypeStruct((T, 3 * D), BF16),
                   jax.ShapeDtypeStruct((T, D), BF16), jax.ShapeDtypeStruct((T, D), BF16),
                   jax.ShapeDtypeStruct((T, D), BF16), jax.ShapeDtypeStruct((8, D), F32)],
        scratch_shapes=[pltpu.VMEM((n_ext, D), F32), pltpu.VMEM((n_ext, D), F32),
                        pltpu.VMEM((CONV_HALO, D), F32)],
        compiler_params=_params(dimension_semantics=("arbitrary",)),
    )(dh, h, y, proj, proj, g_pre, g_post, win, taps, wout, *xa)


def _wgrad(name, a, b, a_spec, b_spec, out_block, n_out, *, tk):
    T = a.shape[-2]
    nk = T // tk

    def body(a_ref, b_ref, out_ref, acc_ref):
        k = pl.program_id(1)

        @pl.when(k == 0)
        def _():
            acc_ref[...] = jnp.zeros_like(acc_ref)

        acc_ref[...] += _dot_tn(a_ref[...], b_ref[...])

        @pl.when(k == nk - 1)
        def _():
            out_ref[...] = acc_ref[...].astype(BF16)

    return pl.pallas_call(
        body, name=name, grid=(n_out, nk), in_specs=[a_spec, b_spec],
        out_specs=pl.BlockSpec((None,) + out_block, lambda i, k: (i, 0, 0)),
        out_shape=jax.ShapeDtypeStruct((n_out,) + out_block, BF16),
        scratch_shapes=[pltpu.VMEM(out_block, F32)],
        compiler_params=_params(dimension_semantics=("arbitrary", "arbitrary")),
    )(a, b)


Copy = collections.namedtuple("Copy", "mask sb src db dst sem")
Local = collections.namedtuple("Local", "sb src db dst")

HBM_SPEC = pl.BlockSpec(memory_space=pltpu.HBM)
SEM_SPEC = pl.BlockSpec(memory_space=pltpu.SEMAPHORE)
SIBLING, X_PEER, Y_PEER, DIAGONAL = 1, 4, 2, 6
OTHER_CHIPS = (X_PEER, Y_PEER, DIAGONAL)


def _whole(ref, i):
    return ref


def _lead(ref, i):
    return ref.at[i]


def _second(ref, i):
    return ref.at[:, i]


def _place():
    x, y, c = lax.axis_index("x"), lax.axis_index("y"), lax.axis_index("c")
    return (x, y, c), 4 * x + 2 * y + c


def _descriptor(cp, bufs, xyc, me, sender, send_sems, recv_sems):
    x, y, c = xyc
    flip = lambda v, bit: (1 - v) if bit else v
    return pltpu.make_async_remote_copy(
        src_ref=cp.src(bufs[cp.sb], me), dst_ref=cp.dst(bufs[cp.db], sender),
        send_sem=send_sems.at[cp.sem], recv_sem=recv_sems.at[cp.sem],
        device_id=(flip(x, cp.mask & 4), flip(y, cp.mask & 2), flip(c, cp.mask & 1)), device_id_type=MESH)


def _exchange(name, bufs, plan, local=()):
    n = len(bufs)

    def body(*refs):
        ins = refs[:n]
        send_sems, recv_sems, local_sems = refs[2 * n:]
        xyc, me = _place()
        own = [pltpu.make_async_copy(lc.src(ins[lc.sb], me), lc.dst(ins[lc.db], me), local_sems.at[i])
               for i, lc in enumerate(local)]
        sends = [_descriptor(cp, ins, xyc, me, me, send_sems, recv_sems) for cp in plan]
        for cp in own + sends:
            cp.start()
        for cp in plan:
            _descriptor(cp, ins, xyc, me, me ^ cp.mask, send_sems, recv_sems).wait_recv()
        for cp in sends:
            cp.wait_send()
        for cp in own:
            cp.wait()

    return pl.pallas_call(
        body, name=name, in_specs=[HBM_SPEC] * n, out_specs=[HBM_SPEC] * n,
        out_shape=[jax.ShapeDtypeStruct(b.shape, b.dtype) for b in bufs],
        input_output_aliases={i: i for i in range(n)},
        scratch_shapes=[pltpu.SemaphoreType.DMA((len(plan),)), pltpu.SemaphoreType.DMA((len(plan),)),
                        pltpu.SemaphoreType.DMA((max(len(local), 1),))],
    )(*bufs)


def _split_call(name, bufs, *, wait=None, wait_sems=None, start=None, local=(), after=None, token=False):
    n = len(bufs)
    n_wait = 2 if wait else 0
    n_after = 1 if after is not None else 0
    n_start = 2 if start else 0

    def body(*refs):
        ins = refs[:n]
        wsend, wrecv = refs[n:n + n_wait] if wait else (None, None)
        outs = refs[n + n_wait + n_after:]
        ssend, srecv = outs[:n_start] if start else (None, None)
        rest = outs[n_start + n:]
        xyc, me = _place()
        for cp in wait or ():
            d = _descriptor(cp, ins, xyc, me, me ^ cp.mask, wsend, wrecv)
            d.wait_send()
            d.wait_recv()
        own = [pltpu.make_async_copy(lc.src(ins[lc.sb], me), lc.dst(ins[lc.db], me), rest[-1].at[i])
               for i, lc in enumerate(local)]
        for cp in own:
            cp.start()
        for cp in start or ():
            _descriptor(cp, ins, xyc, me, me, ssend, srecv).start()
        for cp in own:
            cp.wait()
        if token:
            rest[0][...] = jnp.zeros_like(rest[0])

    args = [pltpu.with_memory_space_constraint(b, pltpu.HBM) for b in bufs]
    in_specs = [HBM_SPEC] * n
    if wait:
        args += list(wait_sems)
        in_specs += [SEM_SPEC] * 2
    if after is not None:
        args.append(after)
        in_specs.append(pl.BlockSpec(memory_space=pl.ANY))
    out_shape, out_specs = [], []
    if start:
        out_shape += [pltpu.SemaphoreType.DMA((len(start),))] * 2
        out_specs += [SEM_SPEC] * 2
    out_shape += [pltpu.HBM(b.shape, b.dtype) for b in bufs]
    out_specs += [HBM_SPEC] * n
    if token:
        out_shape.append(jax.ShapeDtypeStruct((8, 128), F32))
        out_specs.append(pl.BlockSpec(memory_space=pltpu.VMEM))
    outs = pl.pallas_call(
        body, name=name, in_specs=in_specs, out_specs=out_specs, out_shape=out_shape,
        input_output_aliases={i: n_start + i for i in range(n)},
        scratch_shapes=[pltpu.SemaphoreType.DMA((len(local),))] if local else [],
        compiler_params=pltpu.CompilerParams(has_side_effects=pltpu.SideEffectType.DATAFLOW_SIDE_EFFECTING),
    )(*args)
    sems = tuple(outs[:n_start]) if start else None
    return sems, list(outs[n_start:n_start + n]), (outs[n_start + n] if token else None)


def _adamw(w, g, m, v):
    m = ADAM_B1 * m + (1.0 - ADAM_B1) * g
    v = ADAM_B2 * v + (1.0 - ADAM_B2) * (g * g)
    m_hat = m / (1.0 - ADAM_B1 ** ADAM_STEP)
    v_hat = v / (1.0 - ADAM_B2 ** ADAM_STEP)
    delta = -ADAM_LR * (m_hat / (jnp.sqrt(v_hat) + ADAM_EPS) + ADAM_WD * w)
    return delta, m, v


def _reduce_adam(name, parts, w, m, v, *, tr):
    L, R, C = w.shape
    S = parts[0].shape[0]
    tr = min(tr, R)

    def body(*refs):
        p_refs = refs[:L]
        w_ref, m_ref, v_ref, g_ref, d_ref, nm_ref, nv_ref = refs[L:]
        for l in range(L):
            g = p_refs[l][0].astype(F32)
            for s in range(1, S):
                g = g + p_refs[l][s].astype(F32)
            g_ref[l] = g
            d_ref[l], nm_ref[l], nv_ref[l] = _adamw(w_ref[l], g, m_ref[l], v_ref[l])

    blk = pl.BlockSpec((L, tr, C), lambda r: (0, r, 0))
    out = jax.ShapeDtypeStruct((L, R, C), F32)
    return pl.pallas_call(
        body, name=name, grid=(R // tr,),
        in_specs=[pl.BlockSpec((S, tr, C), lambda r: (0, r, 0))] * L + [blk, blk, blk],
        out_specs=[blk] * 4, out_shape=[out] * 4,
        compiler_params=_params(dimension_semantics=("arbitrary",)),
    )(*parts, w, m, v)


def _small_reduce(parts):
    D = parts.shape[2]
    rows = [0, 1, 8, 9, 16, 17, 24, 25, 18, 19, 20, 2]

    def body(p_ref, out_ref):
        s = p_ref[0]
        for d in range(1, N_DEV):
            s = s + p_ref[d]
        out_ref[...] = jnp.zeros_like(out_ref)
        for r, src in enumerate(rows):
            out_ref[r:r + 1, :] = s[src:src + 1, :]

    return pl.pallas_call(body, name="small_reduce", out_shape=jax.ShapeDtypeStruct((16, D), F32))(parts)


def _small_adam(g_gain, g_taps, g_scale, gains, taps, scale):
    def body(gg, gt, gs, wg, mg, vg, wt, mt, vt, ws, ms, vs, *outs):
        for k, (g, w, m, v) in enumerate(((gg, wg, mg, vg), (gt, wt, mt, vt), (gs, ws, ms, vs))):
            outs[3 * k][...], outs[3 * k + 1][...], outs[3 * k + 2][...] = _adamw(w[...], g[...], m[...], v[...])

    shapes = [jax.ShapeDtypeStruct(t[0].shape, F32) for t in (gains, taps, scale) for _ in range(3)]
    return pl.pallas_call(body, name="small_adam", out_shape=shapes)(g_gain, g_taps, g_scale, *gains, *taps, *scale)


def kernel(x, norm_gains, pool_w, pool_scale, conv_in_w, conv_w, conv_out_w, ffn_gate_up_w, ffn_down_w, loss_target, m_norm_gains, m_pool_w, m_pool_scale, m_conv_in_w, m_conv_w, m_conv_out_w, m_ffn_gate_up_w, m_ffn_down_w, v_norm_gains, v_pool_w, v_pool_scale, v_conv_in_w, v_conv_w, v_conv_out_w, v_ffn_gate_up_w, v_ffn_down_w):
    T, D = x.shape[1], x.shape[2]
    tm = min(256, T)
    tk = min(512, T)
    n_layers = ffn_gate_up_w.shape[0]
    fb = ffn_gate_up_w.shape[2]
    fr = ffn_down_w.shape[1]
    dcol = norm_gains.shape[2]
    cb = conv_in_w.shape[2]
    gw = pool_w.shape[3]
    me = 4 * lax.axis_index("x") + 2 * lax.axis_index("y") + lax.axis_index("c")

    small_w = jnp.concatenate([norm_gains.reshape(8, dcol), jnp.pad(conv_w[0], ((0, 5), (0, 0)))], axis=0)
    every = range(1, N_DEV)
    _, _, pw_g, small_g = _exchange(
        "gather_small",
        [pool_w[0].astype(BF16), small_w, lax.empty((4, N_DEV, gw // N_DEV, gw), BF16), lax.empty((N_DEV, 16, dcol), F32)],
        [Copy(m, 0, _whole, 2, _second, m - 1) for m in every] + [Copy(m, 1, _whole, 3, _lead, N_DEV - 2 + m) for m in every],
        [Local(0, _whole, 2, _second), Local(1, _whole, 3, _lead)])
    pw = pw_g.reshape(4, gw, gw)
    small_full = jnp.swapaxes(small_g, 0, 1).reshape(16, D)
    gain = lambda l, s: small_full[4 * l + s][None, :]
    taps = small_full[8:16]

    shards = [w.astype(BF16) for w in (ffn_gate_up_w[0], ffn_down_w[0], conv_in_w[0], conv_out_w[0],
                                       ffn_gate_up_w[1], ffn_down_w[1])]
    n_big = len(shards)
    level1 = [Copy(mask, n, _whole, n_big + n, _lead, 4 * n + j)
              for n in range(n_big) for j, mask in enumerate((SIBLING,) + OTHER_CHIPS)]
    sems1, bufs1, started = _split_call(
        "gather_start", shards + [lax.empty((N_DEV,) + s.shape, BF16) for s in shards],
        start=level1, local=[Local(n, _whole, n_big + n, _lead) for n in range(n_big)], after=small_g, token=True)

    def gathered(name, group, after):
        k = len(group)
        arrived = [Copy(cp.mask, group.index(cp.sb), cp.src, k + group.index(cp.sb), cp.dst, cp.sem)
                   for cp in level1 if cp.sb in group]
        onward = [Copy(SIBLING, i, (lambda ref, me, m=m: ref.at[me ^ m]), i, (lambda ref, sender, m=m: ref.at[sender ^ m]), 3 * i + j)
                  for i in range(k) for j, m in enumerate(OTHER_CHIPS)]
        sems2, bufs2, _ = _split_call(
            name + "_forward", [bufs1[n] for n in group] + [bufs1[n_big + n] for n in group],
            wait=arrived, wait_sems=sems1, start=[cp._replace(sb=k + cp.sb, db=k + cp.db) for cp in onward], after=after)
        return _split_call(name + "_done", bufs2[k:], wait=onward, wait_sems=sems2)[1]

    h0 = x[0]
    h1 = _pool_fwd(h0, pw, pool_scale, gain(0, 0), gain(0, 1), tm=tm, after=started)
    wgu0, wd0 = gathered("gather_ffn0", [0, 1], h1)
    wd0 = wd0.reshape(N_DEV * fr, D)
    h2, gu0, ff0 = _ffn_fwd(h1, gain(0, 2), gain(0, 3), wgu0, wd0, 0, None, tm=tm)
    win_g, wout_g = gathered("gather_conv", [2, 3], h2)
    wout = wout_g.reshape(D, D)
    h3, proj, y = _conv_fwd(h2, gain(1, 0), gain(1, 1), win_g, taps, wout, tm=tm)
    wgu1, wd1 = gathered("gather_ffn1", [4, 5], h3)
    wd1 = wd1.reshape(N_DEV * fr, D)
    dh4, gu1, ff1, loss_part = _ffn_fwd(h3, gain(1, 2), gain(1, 3), wgu1, wd1, 1, loss_target[0], tm=tm)
    loss = lax.psum(loss_part[0, 0], ("x", "y", "c"))

    seq = lambda i, k: (k, 0)
    dh3, dgu1, dff1, c1, act1, small_f1 = _ffn_bwd(dh4, h3, ff1, gu1, gain(1, 2), gain(1, 3), wgu1, wd1, 1, tm=tm)
    g_wgu1 = _wgrad("wgrad_gate_up_1", c1, dgu1, pl.BlockSpec((tk, D), seq),
                    pl.BlockSpec((None, tk, fb), lambda i, k: (i, k, 0)), (D, fb), N_DEV, tk=tk)
    g_wd1 = _wgrad("wgrad_down_1", act1, dff1, pl.BlockSpec((None, tk, fb), lambda i, k: (i, k, 0)),
                   pl.BlockSpec((tk, D), seq), (fb, D), N_DEV // 2, tk=tk)
    dh2, dproj, z, a1, dy, small_c = _conv_bwd(dh3, h2, y, proj, gain(1, 0), gain(1, 1), win_g, taps, wout, tm=tm)
    g_win = _wgrad("wgrad_conv_in", a1, dproj, pl.BlockSpec((tk, D), seq),
                   pl.BlockSpec((tk, cb), lambda i, k: (k, i)), (D, cb), N_DEV, tk=tk)
    g_wout = _wgrad("wgrad_conv_out", z, dy, pl.BlockSpec((tk, D), seq), pl.BlockSpec((tk, D), seq), (D, D), 1, tk=tk)
    dh1, dgu0, dff0, c0, act0, small_f0 = _ffn_bwd(dh2, h1, ff0, gu0, gain(0, 2), gain(0, 3), wgu0, wd0, 0, tm=tm)
    g_wgu0 = _wgrad("wgrad_gate_up_0", c0, dgu0, pl.BlockSpec((tk, D), seq),
                    pl.BlockSpec((None, tk, fb), lambda i, k: (i, k, 0)), (D, fb), N_DEV, tk=tk)
    g_wd0 = _wgrad("wgrad_down_0", act0, dff0, pl.BlockSpec((None, tk, fb), lambda i, k: (i, k, 0)),
                   pl.BlockSpec((tk, D), seq), (fb, D), N_DEV // 2, tk=tk)
    grad_x, g_pw, small_p = _pool_bwd(dh1, h0, pw, pool_scale, gain(0, 0), gain(0, 1), tm=tm)

    small_part = jnp.concatenate([small_p, small_f0, small_c, small_f1], axis=0)
    grads = [g_pw.reshape(4, N_DEV, gw // N_DEV, gw), g_win, g_wout.reshape(N_DEV, dcol, D),
             g_wgu0, g_wgu1, g_wd0.reshape(N_DEV, fr, D), g_wd1.reshape(N_DEV, fr, D), small_part]
    n_g = len(grads)
    lands = [lax.empty(g.shape, g.dtype) for g in grads[:-1]] + [lax.empty((N_DEV,) + small_part.shape, F32)]
    at = [_second] + [_lead] * (n_g - 1)
    send_at = at[:-1] + [_whole]
    plan = [Copy(m, n, (lambda ref, me, m=m, f=send_at[n]: f(ref, me ^ m)), n_g + n, at[n], (N_DEV - 1) * n + m - 1)
            for n in range(n_g) for m in every]
    r_pw, r_win, r_wout, r_wgu0, r_wgu1, r_wd0, r_wd1, r_small = _exchange(
        "scatter_grads", grads + lands, plan, [Local(n, send_at[n], n_g + n, at[n]) for n in range(n_g)])[n_g:]

    o_pw = _reduce_adam("adam_pool_w", [r_pw[g] for g in range(4)], pool_w[0], m_pool_w[0], v_pool_w[0], tr=32)
    o_win = _reduce_adam("adam_conv_in", [r_win], conv_in_w, m_conv_in_w, v_conv_in_w, tr=256)
    o_wout = _reduce_adam("adam_conv_out", [r_wout], conv_out_w, m_conv_out_w, v_conv_out_w, tr=128)
    o_wgu = _reduce_adam("adam_gate_up", [r_wgu0, r_wgu1], ffn_gate_up_w, m_ffn_gate_up_w, v_ffn_gate_up_w, tr=256)
    o_wd = _reduce_adam("adam_down", [r_wd0, r_wd1], ffn_down_w, m_ffn_down_w, v_ffn_down_w, tr=176)
    g_small = _small_reduce(r_small)
    g_cols = lax.dynamic_slice(g_small, (0, me * dcol), (16, dcol))
    o_small = _small_adam(
        g_cols[0:8], g_cols[8:11], g_small[11:12],
        (norm_gains.reshape(8, dcol), m_norm_gains.reshape(8, dcol), v_norm_gains.reshape(8, dcol)),
        (conv_w[0], m_conv_w[0], v_conv_w[0]), (pool_scale, m_pool_scale, v_pool_scale))
    d_gain, nm_gain, nv_gain, d_taps, nm_taps, nv_taps, d_scale, nm_scale, nv_scale = o_small

    gshape = norm_gains.shape
    per = lambda k: (
        (g_cols[0:8].reshape(gshape), d_gain.reshape(gshape), nm_gain.reshape(gshape), nv_gain.reshape(gshape))[k],
        o_pw[k][None], (g_small[11:12], d_scale, nm_scale, nv_scale)[k], o_win[k],
        (g_cols[8:11][None], d_taps[None], nm_taps[None], nv_taps[None])[k], o_wout[k], o_wgu[k], o_wd[k])
    return (loss, grad_x[None], *per(0), *per(1), *per(2), *per(3))
```

```python
import collections
import functools

import jax
import jax.numpy as jnp
from jax import lax
from jax.experimental import pallas as pl
from jax.experimental.pallas import tpu as pltpu

N_DEV = 8
RMS_EPS = 1e-6
POOL_WINDOWS = (2, 4, 8, 16)
POOL_HALO = 16
CONV_HALO = 16
ADAM_LR, ADAM_B1, ADAM_B2, ADAM_EPS, ADAM_WD, ADAM_STEP = 0.001, 0.9, 0.999, 1e-08, 0.01, 10

VMEM_LIMIT = 56 * 2**20
BF16 = jnp.bfloat16
F32 = jnp.float32
MESH = pl.DeviceIdType.MESH


def _params(**kw):
    return pltpu.CompilerParams(vmem_limit_bytes=VMEM_LIMIT, **kw)


def _resident(shape, index_map):
    return pl.BlockSpec(shape, index_map, pipeline_mode=pl.Buffered(1))


def _ordered(body, n_in, after):
    if after is None:
        return functools.partial(body), [], []
    return (lambda *refs: body(*refs[:n_in], *refs[n_in + 1:])), [after], [pl.BlockSpec(memory_space=pl.ANY)]


def _rms_fwd(x, g):
    r = lax.rsqrt(jnp.mean(x * x, axis=-1, keepdims=True) + RMS_EPS)
    return x * r * g


def _rms_bwd(x, g, dy):
    r = lax.rsqrt(jnp.mean(x * x, axis=-1, keepdims=True) + RMS_EPS)
    xhat = x * r
    dg = jnp.sum(dy * xhat, axis=0, keepdims=True)
    t = dy * g
    dx = r * (t - xhat * jnp.mean(t * xhat, axis=-1, keepdims=True))
    return dx, dg


def _dot(a, b):
    return jnp.dot(a, b, preferred_element_type=F32)


def _dot_nt(a, b):
    return lax.dot_general(a, b, (((1,), (1,)), ((), ())), preferred_element_type=F32)


def _dot_tn(a, b):
    return lax.dot_general(a, b, (((0,), (0,)), ((), ())), preferred_element_type=F32)


def _row_inverse_counts(tile, tm):
    pos = (lax.broadcasted_iota(jnp.int32, (tm, 1), 0) + tile * tm + 1).astype(F32)
    return [1.0 / jnp.minimum(pos, float(w)) for w in POOL_WINDOWS]


def _pool_from_ext(ext, a, invs, gw):
    s = ext
    outs = []
    for g, w in enumerate(POOL_WINDOWS):
        s = s[:, (gw if g else 0):]
        s = s + pltpu.roll(s, w // 2, 0)
        outs.append(s[POOL_HALO:, :gw] * invs[g] - a[:, g * gw:(g + 1) * gw])
    return outs


def _pool_fwd(h, pw, scale, g_pre, g_post, *, tm, after=None):
    T, D = h.shape
    gw = D // len(POOL_WINDOWS)
    hb = tm // POOL_HALO

    def body(h_ref, halo_ref, pw_ref, scale_ref, gpre_ref, gpost_ref, out_ref, ext_ref):
        i = pl.program_id(0)
        x = h_ref[...]
        a = _rms_fwd(x, gpre_ref[...])
        ah = _rms_fwd(halo_ref[...], gpre_ref[...])
        ext_ref[0:POOL_HALO, :] = jnp.where(i == 0, 0.0, ah)
        ext_ref[POOL_HALO:, :] = a
        pooled = _pool_from_ext(ext_ref[...], a, _row_inverse_counts(i, tm), gw)
        mixed = jnp.concatenate([_dot(p.astype(BF16), pw_ref[g]) for g, p in enumerate(pooled)], axis=1)
        out_ref[...] = x + _rms_fwd(mixed * scale_ref[...], gpost_ref[...])

    vec = _resident((1, D), lambda i: (0, 0))
    fn, xa, xs = _ordered(body, 6, after)
    return pl.pallas_call(
        fn, name="pool_fwd", grid=(T // tm,),
        in_specs=[pl.BlockSpec((tm, D), lambda i: (i, 0)),
                  pl.BlockSpec((POOL_HALO, D), lambda i: (jnp.maximum(i * hb - 1, 0), 0)),
                  _resident(pw.shape, lambda i: (0, 0, 0)), vec, vec, vec] + xs,
        out_specs=pl.BlockSpec((tm, D), lambda i: (i, 0)),
        out_shape=jax.ShapeDtypeStruct((T, D), F32),
        scratch_shapes=[pltpu.VMEM((tm + POOL_HALO, D), F32)],
        compiler_params=_params(dimension_semantics=("arbitrary",)),
    )(h, h, pw, scale, g_pre, g_post, *xa)


def _pool_bwd(dh, h, pw, scale, g_pre, g_post, *, tm, after=None):
    T, D = h.shape
    gw = D // len(POOL_WINDOWS)
    hb = tm // POOL_HALO
    nt = T // tm
    n_ext = tm + POOL_HALO

    def body(dh_ref, h_ref, halo_ref, pw_ref, scale_ref, gpre_ref, gpost_ref,
             dx_ref, dpw_ref, small_ref, ext_ref, ext2_ref, carry_ref, dpw_acc):
        i = pl.program_id(0)
        tile = nt - 1 - i

        @pl.when(i == 0)
        def _():
            small_ref[...] = jnp.zeros_like(small_ref)
            dpw_acc[...] = jnp.zeros_like(dpw_acc)
            carry_ref[...] = jnp.zeros_like(carry_ref)

        x = h_ref[...]
        dout = dh_ref[...]
        a = _rms_fwd(x, gpre_ref[...])
        ah = _rms_fwd(halo_ref[...], gpre_ref[...])
        ext_ref[0:POOL_HALO, :] = jnp.where(tile == 0, 0.0, ah)
        ext_ref[POOL_HALO:, :] = a
        invs = _row_inverse_counts(tile, tm)
        pooled = [p.astype(BF16) for p in _pool_from_ext(ext_ref[...], a, invs, gw)]
        mixed_pre = jnp.concatenate([_dot(p, pw_ref[g]) for g, p in enumerate(pooled)], axis=1)
        scale_v = scale_ref[...]
        dmixed, dg_post = _rms_bwd(mixed_pre * scale_v, gpost_ref[...], dout)
        small_ref[1:2, :] += dg_post
        small_ref[2:3, :] += jnp.sum(dmixed * mixed_pre, axis=0, keepdims=True)
        dpre = (dmixed * scale_v).astype(BF16)
        dpooled = []
        for g in range(len(POOL_WINDOWS)):
            dp = dpre[:, g * gw:(g + 1) * gw]
            dpw_acc[g] += _dot_tn(pooled[g], dp)
            dpooled.append(_dot_nt(dp, pw_ref[g]))
        q = jnp.concatenate([d * invs[g] for g, d in enumerate(dpooled)], axis=1)
        ext2_ref[0:tm, :] = q
        ext2_ref[tm:, :] = carry_ref[...]
        carry_ref[...] = q[0:POOL_HALO, :]
        s = ext2_ref[...]
        da = []
        for g, w in enumerate(POOL_WINDOWS):
            s = s[:, (gw if g else 0):]
            s = s + pltpu.roll(s, n_ext - w // 2, 0)
            da.append(s[0:tm, :gw] - dpooled[g])
        dx, dg_pre = _rms_bwd(x, gpre_ref[...], jnp.concatenate(da, axis=1))
        small_ref[0:1, :] += dg_pre
        dx_ref[...] = dout + dx

        @pl.when(i == nt - 1)
        def _():
            dpw_ref[...] = dpw_acc[...].astype(BF16)

    vec = _resident((1, D), lambda i: (0, 0))
    rev = lambda i: (nt - 1 - i, 0)
    fn, xa, xs = _ordered(body, 7, after)
    return pl.pallas_call(
        fn, name="pool_bwd", grid=(nt,),
        in_specs=[pl.BlockSpec((tm, D), rev), pl.BlockSpec((tm, D), rev),
                  pl.BlockSpec((POOL_HALO, D), lambda i: (jnp.maximum((nt - 1 - i) * hb - 1, 0), 0)),
                  _resident(pw.shape, lambda i: (0, 0, 0)), vec, vec, vec] + xs,
        out_specs=[pl.BlockSpec((tm, D), rev),
                   pl.BlockSpec(pw.shape, lambda i: (0, 0, 0)),
                   pl.BlockSpec((8, D), lambda i: (0, 0))],
        out_shape=[jax.ShapeDtypeStruct((T, D), F32), jax.ShapeDtypeStruct(pw.shape, BF16),
                   jax.ShapeDtypeStruct((8, D), F32)],
        scratch_shapes=[pltpu.VMEM((n_ext, D), F32), pltpu.VMEM((n_ext, D), F32),
                        pltpu.VMEM((POOL_HALO, D), F32), pltpu.VMEM(pw.shape, F32)],
        compiler_params=_params(dimension_semantics=("arbitrary",)),
    )(dh, h, h, pw, scale, g_pre, g_post, *xa)


def _ffn_fwd(h, g_pre, g_post, wgu, wd, layer, target, *, tm, after=None):
    T, D = h.shape
    nblk, fb = wgu.shape[0], wgu.shape[1]
    half = nblk // 2
    last = target is not None

    def body(*refs):
        if last:
            h_ref, gpre_ref, gpost_ref, wgu_ref, wd_ref, tgt_ref, out_ref, gu_ref, ff_ref, loss_ref = refs
        else:
            h_ref, gpre_ref, gpost_ref, wgu_ref, wd_ref, out_ref, gu_ref, ff_ref = refs
        x = h_ref[...]
        cb = _rms_fwd(x, gpre_ref[...]).astype(BF16)
        acc = jnp.zeros((tm, D), F32)
        for j in range(half):
            g = _dot_nt(cb, wgu_ref[j])
            u = _dot_nt(cb, wgu_ref[j + half])
            gu_ref[j] = g.astype(BF16)
            gu_ref[j + half] = u.astype(BF16)
            act = (g * jax.nn.sigmoid(g) * u).astype(BF16)
            acc = acc + _dot(act, wd_ref[j * fb:(j + 1) * fb, :])
        ff_ref[...] = acc.astype(BF16)
        hout = x + _rms_fwd(acc, gpost_ref[...])
        if last:
            diff = hout - tgt_ref[...]
            out_ref[...] = diff * (1.0 / D)

            @pl.when(pl.program_id(0) == 0)
            def _():
                loss_ref[...] = jnp.zeros_like(loss_ref)

            loss_ref[...] += jnp.sum(diff * diff) * (0.5 / D)
        else:
            out_ref[...] = hout

    vec = _resident((1, D), lambda i: (0, 0))
    tile = pl.BlockSpec((tm, D), lambda i: (i, 0))
    in_specs = [tile, vec, vec,
                _resident(wgu.shape, lambda i: (0, 0, 0)), _resident(wd.shape, lambda i: (0, 0))]
    out_specs = [tile, pl.BlockSpec((nblk, tm, fb), lambda i: (0, i, 0)), tile]
    out_shape = [jax.ShapeDtypeStruct((T, D), F32), jax.ShapeDtypeStruct((nblk, T, fb), BF16),
                 jax.ShapeDtypeStruct((T, D), BF16)]
    args = [h, g_pre, g_post, wgu, wd]
    if last:
        in_specs.append(tile)
        args.append(target)
        out_specs.append(pl.BlockSpec((8, 128), lambda i: (0, 0)))
        out_shape.append(jax.ShapeDtypeStruct((8, 128), F32))
    fn, xa, xs = _ordered(body, len(args), after)
    return pl.pallas_call(
        fn, name=f"ffn_fwd_{layer}", grid=(T // tm,), in_specs=in_specs + xs, out_specs=out_specs,
        out_shape=out_shape, compiler_params=_params(dimension_semantics=("arbitrary",)),
    )(*args, *xa)


def _ffn_bwd(dh, h, ff, gu, g_pre, g_post, wgu, wd, layer, *, tm, after=None):
    T, D = h.shape
    nblk, fb = wgu.shape[0], wgu.shape[1]
    half = nblk // 2

    def body(dh_ref, h_ref, ff_ref, gu_ref, gpre_ref, gpost_ref, wgu_ref, wd_ref,
             dx_ref, dgu_ref, dff_ref, c_ref, act_ref, small_ref):
        @pl.when(pl.program_id(0) == 0)
        def _():
            small_ref[...] = jnp.zeros_like(small_ref)

        dout = dh_ref[...]
        dff, dg_post = _rms_bwd(ff_ref[...].astype(F32), gpost_ref[...], dout)
        small_ref[1:2, :] += dg_post
        dffb = dff.astype(BF16)
        dff_ref[...] = dffb
        dc = jnp.zeros((tm, D), F32)
        for j in range(half):
            g = gu_ref[j].astype(F32)
            u = gu_ref[j + half].astype(F32)
            s = jax.nn.sigmoid(g)
            silu = g * s
            act_ref[j] = (silu * u).astype(BF16)
            dact = _dot_nt(dffb, wd_ref[j * fb:(j + 1) * fb, :])
            dg = (dact * u * (s * (1.0 + g * (1.0 - s)))).astype(BF16)
            du = (dact * silu).astype(BF16)
            dgu_ref[j] = dg
            dgu_ref[j + half] = du
            dc = dc + _dot(dg, wgu_ref[j]) + _dot(du, wgu_ref[j + half])
        x = h_ref[...]
        c_ref[...] = _rms_fwd(x, gpre_ref[...]).astype(BF16)
        dx, dg_pre = _rms_bwd(x, gpre_ref[...], dc)
        small_ref[0:1, :] += dg_pre
        dx_ref[...] = dout + dx

    vec = _resident((1, D), lambda i: (0, 0))
    tile = pl.BlockSpec((tm, D), lambda i: (i, 0))
    blk = pl.BlockSpec((nblk, tm, fb), lambda i: (0, i, 0))
    fn, xa, xs = _ordered(body, 8, after)
    return pl.pallas_call(
        fn, name=f"ffn_bwd_{layer}", grid=(T // tm,),
        in_specs=[tile, tile, tile, blk, vec, vec,
                  _resident(wgu.shape, lambda i: (0, 0, 0)), _resident(wd.shape, lambda i: (0, 0))] + xs,
        out_specs=[tile, blk, tile, tile, pl.BlockSpec((half, tm, fb), lambda i: (0, i, 0)),
                   pl.BlockSpec((8, D), lambda i: (0, 0))],
        out_shape=[jax.ShapeDtypeStruct((T, D), F32), jax.ShapeDtypeStruct((nblk, T, fb), BF16),
                   jax.ShapeDtypeStruct((T, D), BF16), jax.ShapeDtypeStruct((T, D), BF16),
                   jax.ShapeDtypeStruct((half, T, fb), BF16), jax.ShapeDtypeStruct((8, D), F32)],
        compiler_params=_params(dimension_semantics=("arbitrary",)),
    )(dh, h, ff, gu, g_pre, g_post, wgu, wd, *xa)


def _conv_fwd(h, g_pre, g_post, win, taps, wout, *, tm, after=None):
    T, D = h.shape
    nblk, cb = win.shape[0], win.shape[2]

    def body(h_ref, gpre_ref, gpost_ref, win_ref, taps_ref, wout_ref,
             out_ref, proj_ref, y_ref, proj_scr, ext_ref, carry_ref):
        i = pl.program_id(0)

        @pl.when(i == 0)
        def _():
            carry_ref[...] = jnp.zeros_like(carry_ref)

        x = h_ref[...]
        a = _rms_fwd(x, gpre_ref[...]).astype(BF16)
        for k in range(nblk):
            proj_scr[:, k * cb:(k + 1) * cb] = _dot(a, win_ref[k])
        proj_ref[...] = proj_scr[...].astype(BF16)
        u = proj_scr[:, D:2 * D] * proj_scr[:, 2 * D:3 * D]
        ext_ref[0:CONV_HALO, :] = carry_ref[...]
        ext_ref[CONV_HALO:, :] = u
        carry_ref[...] = u[tm - CONV_HALO:, :]
        e = ext_ref[...]
        conv = (taps_ref[2:3, :] * u + taps_ref[1:2, :] * pltpu.roll(e, 1, 0)[CONV_HALO:, :]
                + taps_ref[0:1, :] * pltpu.roll(e, 2, 0)[CONV_HALO:, :])
        z = (proj_scr[:, 0:D] * conv).astype(BF16)
        y = _dot(z, wout_ref[...])
        y_ref[...] = y.astype(BF16)
        out_ref[...] = x + _rms_fwd(y, gpost_ref[...])

    vec = _resident((1, D), lambda i: (0, 0))
    tile = pl.BlockSpec((tm, D), lambda i: (i, 0))
    fn, xa, xs = _ordered(body, 6, after)
    return pl.pallas_call(
        fn, name="conv_fwd", grid=(T // tm,),
        in_specs=[tile, vec, vec, _resident(win.shape, lambda i: (0, 0, 0)),
                  _resident(taps.shape, lambda i: (0, 0)), _resident(wout.shape, lambda i: (0, 0))] + xs,
        out_specs=[tile, pl.BlockSpec((tm, 3 * D), lambda i: (i, 0)), tile],
        out_shape=[jax.ShapeDtypeStruct((T, D), F32), jax.ShapeDtypeStruct((T, 3 * D), BF16),
                   jax.ShapeDtypeStruct((T, D), BF16)],
        scratch_shapes=[pltpu.VMEM((tm, 3 * D), F32), pltpu.VMEM((tm + CONV_HALO, D), F32),
                        pltpu.VMEM((CONV_HALO, D), F32)],
        compiler_params=_params(dimension_semantics=("arbitrary",)),
    )(h, g_pre, g_post, win, taps, wout, *xa)


def _conv_bwd(dh, h, y, proj, g_pre, g_post, win, taps, wout, *, tm, after=None):
    T, D = h.shape
    nblk, cb = win.shape[0], win.shape[2]
    nt = T // tm
    hb = tm // CONV_HALO
    n_ext = tm + CONV_HALO

    def body(dh_ref, h_ref, y_ref, proj_ref, halo_ref, gpre_ref, gpost_ref, win_ref, taps_ref, wout_ref,
             dx_ref, dproj_ref, z_ref, a_ref, dy_ref, small_ref, ext_ref, ext2_ref, carry_ref):
        i = pl.program_id(0)
        tile = nt - 1 - i

        @pl.when(i == 0)
        def _():
            small_ref[...] = jnp.zeros_like(small_ref)
            carry_ref[...] = jnp.zeros_like(carry_ref)

        dout = dh_ref[...]
        dy, dg_post = _rms_bwd(y_ref[...].astype(F32), gpost_ref[...], dout)
        small_ref[1:2, :] += dg_post
        dyb = dy.astype(BF16)
        dy_ref[...] = dyb
        dz = _dot_nt(dyb, wout_ref[...])
        bgate = proj_ref[:, 0:D].astype(F32)
        cgate = proj_ref[:, D:2 * D].astype(F32)
        v = proj_ref[:, 2 * D:3 * D].astype(F32)
        u = cgate * v
        uh = halo_ref[:, D:2 * D].astype(F32) * halo_ref[:, 2 * D:3 * D].astype(F32)
        ext_ref[0:CONV_HALO, :] = jnp.where(tile == 0, 0.0, uh)
        ext_ref[CONV_HALO:, :] = u
        e = ext_ref[...]
        u1 = pltpu.roll(e, 1, 0)[CONV_HALO:, :]
        u2 = pltpu.roll(e, 2, 0)[CONV_HALO:, :]
        t0, t1, t2 = taps_ref[0:1, :], taps_ref[1:2, :], taps_ref[2:3, :]
        conv = t2 * u + t1 * u1 + t0 * u2
        z_ref[...] = (bgate * conv).astype(BF16)
        dconv = dz * bgate
        small_ref[2:3, :] += jnp.sum(dconv * u2, axis=0, keepdims=True)
        small_ref[3:4, :] += jnp.sum(dconv * u1, axis=0, keepdims=True)
        small_ref[4:5, :] += jnp.sum(dconv * u, axis=0, keepdims=True)
        ext2_ref[0:tm, :] = dconv
        ext2_ref[tm:, :] = carry_ref[...]
        carry_ref[...] = dconv[0:CONV_HALO, :]
        e2 = ext2_ref[...]
        du = (t2 * dconv + t1 * pltpu.roll(e2, n_ext - 1, 0)[0:tm, :]
              + t0 * pltpu.roll(e2, n_ext - 2, 0)[0:tm, :])
        dproj_ref[:, 0:D] = (dz * conv).astype(BF16)
        dproj_ref[:, D:2 * D] = (du * v).astype(BF16)
        dproj_ref[:, 2 * D:3 * D] = (du * cgate).astype(BF16)
        da = jnp.zeros((tm, D), F32)
        for k in range(nblk):
            da = da + _dot_nt(dproj_ref[:, k * cb:(k + 1) * cb], win_ref[k])
        x = h_ref[...]
        a_ref[...] = _rms_fwd(x, gpre_ref[...]).astype(BF16)
        dx, dg_pre = _rms_bwd(x, gpre_ref[...], da)
        small_ref[0:1, :] += dg_pre
        dx_ref[...] = dout + dx

    vec = _resident((1, D), lambda i: (0, 0))
    rev = lambda i: (nt - 1 - i, 0)
    tile = pl.BlockSpec((tm, D), rev)
    wide = pl.BlockSpec((tm, 3 * D), rev)
    fn, xa, xs = _ordered(body, 10, after)
    return pl.pallas_call(
        fn, name="conv_bwd", grid=(nt,),
        in_specs=[tile, tile, tile, wide,
                  pl.BlockSpec((CONV_HALO, 3 * D), lambda i: (jnp.maximum((nt - 1 - i) * hb - 1, 0), 0)),
                  vec, vec, _resident(win.shape, lambda i: (0, 0, 0)),
                  _resident(taps.shape, lambda i: (0, 0)), _resident(wout.shape, lambda i: (0, 0))] + xs,
        out_specs=[tile, wide, tile, tile, tile, pl.BlockSpec((8, D), lambda i: (0, 0))],
        out_shape=[jax.ShapeDtypeStruct((T, D), F32), jax.ShapeDtypeStruct((T, 3 * D), BF16),
                   jax.ShapeDtypeStruct((T, D), BF16), jax.ShapeDtypeStruct((T, D), BF16),
                   jax.ShapeDtypeStruct((T, D), BF16), jax.ShapeDtypeStruct((8, D), F32)],
        scratch_shapes=[pltpu.VMEM((n_ext, D), F32), pltpu.VMEM((n_ext, D), F32),
                        pltpu.VMEM((CONV_HALO, D), F32)],
        compiler_params=_params(dimension_semantics=("arbitrary",)),
    )(dh, h, y, proj, proj, g_pre, g_post, win, taps, wout, *xa)


def _wgrad(name, a, b, a_spec, b_spec, out_block, n_out, *, tk):
    T = a.shape[-2]
    nk = T // tk

    def body(a_ref, b_ref, out_ref, acc_ref):
        k = pl.program_id(1)

        @pl.when(k == 0)
        def _():
            acc_ref[...] = jnp.zeros_like(acc_ref)

        acc_ref[...] += _dot_tn(a_ref[...], b_ref[...])

        @pl.when(k == nk - 1)
        def _():
            out_ref[...] = acc_ref[...].astype(BF16)

    return pl.pallas_call(
        body, name=name, grid=(n_out, nk), in_specs=[a_spec, b_spec],
        out_specs=pl.BlockSpec((None,) + out_block, lambda i, k: (i, 0, 0)),
        out_shape=jax.ShapeDtypeStruct((n_out,) + out_block, BF16),
        scratch_shapes=[pltpu.VMEM(out_block, F32)],
        compiler_params=_params(dimension_semantics=("arbitrary", "arbitrary")),
    )(a, b)


Copy = collections.namedtuple("Copy", "mask sb src db dst sem")
Local = collections.namedtuple("Local", "sb src db dst")

HBM_SPEC = pl.BlockSpec(memory_space=pltpu.HBM)
SEM_SPEC = pl.BlockSpec(memory_space=pltpu.SEMAPHORE)
SIBLING, X_PEER, Y_PEER, DIAGONAL = 1, 4, 2, 6
OTHER_CHIPS = (X_PEER, Y_PEER, DIAGONAL)


def _whole(ref, i):
    return ref


def _lead(ref, i):
    return ref.at[i]


def _second(ref, i):
    return ref.at[:, i]


def _place():
    x, y, c = lax.axis_index("x"), lax.axis_index("y"), lax.axis_index("c")
    return (x, y, c), 4 * x + 2 * y + c


def _descriptor(cp, bufs, xyc, me, sender, send_sems, recv_sems):
    x, y, c = xyc
    flip = lambda v, bit: (1 - v) if bit else v
    return pltpu.make_async_remote_copy(
        src_ref=cp.src(bufs[cp.sb], me), dst_ref=cp.dst(bufs[cp.db], sender),
        send_sem=send_sems.at[cp.sem], recv_sem=recv_sems.at[cp.sem],
        device_id=(flip(x, cp.mask & 4), flip(y, cp.mask & 2), flip(c, cp.mask & 1)), device_id_type=MESH)


def _exchange(name, bufs, plan, local=()):
    n = len(bufs)

    def body(*refs):
        ins = refs[:n]
        send_sems, recv_sems, local_sems = refs[2 * n:]
        xyc, me = _place()
        own = [pltpu.make_async_copy(lc.src(ins[lc.sb], me), lc.dst(ins[lc.db], me), local_sems.at[i])
               for i, lc in enumerate(local)]
        sends = [_descriptor(cp, ins, xyc, me, me, send_sems, recv_sems) for cp in plan]
        for cp in own + sends:
            cp.start()
        for cp in plan:
            _descriptor(cp, ins, xyc, me, me ^ cp.mask, send_sems, recv_sems).wait_recv()
        for cp in sends:
            cp.wait_send()
        for cp in own:
            cp.wait()

    return pl.pallas_call(
        body, name=name, in_specs=[HBM_SPEC] * n, out_specs=[HBM_SPEC] * n,
        out_shape=[jax.ShapeDtypeStruct(b.shape, b.dtype) for b in bufs],
        input_output_aliases={i: i for i in range(n)},
        scratch_shapes=[pltpu.SemaphoreType.DMA((len(plan),)), pltpu.SemaphoreType.DMA((len(plan),)),
                        pltpu.SemaphoreType.DMA((max(len(local), 1),))],
    )(*bufs)


def _split_call(name, bufs, *, wait=None, wait_sems=None, start=None, local=(), after=None, token=False):
    n = len(bufs)
    n_wait = 2 if wait else 0
    n_after = 1 if after is not None else 0
    n_start = 2 if start else 0

    def body(*refs):
        ins = refs[:n]
        wsend, wrecv = refs[n:n + n_wait] if wait else (None, None)
        outs = refs[n + n_wait + n_after:]
        ssend, srecv = outs[:n_start] if start else (None, None)
        rest = outs[n_start + n:]
        xyc, me = _place()
        for cp in wait or ():
            d = _descriptor(cp, ins, xyc, me, me ^ cp.mask, wsend, wrecv)
            d.wait_send()
            d.wait_recv()
        own = [pltpu.make_async_copy(lc.src(ins[lc.sb], me), lc.dst(ins[lc.db], me), rest[-1].at[i])
               for i, lc in enumerate(local)]
        for cp in own:
            cp.start()
        for cp in start or ():
            _descriptor(cp, ins, xyc, me, me, ssend, srecv).start()
        for cp in own:
            cp.wait()
        if token:
            rest[0][...] = jnp.zeros_like(rest[0])

    args = [pltpu.with_memory_space_constraint(b, pltpu.HBM) for b in bufs]
    in_specs = [HBM_SPEC] * n
    if wait:
        args += list(wait_sems)
        in_specs += [SEM_SPEC] * 2
    if after is not None:
        args.append(after)
        in_specs.append(pl.BlockSpec(memory_space=pl.ANY))
    out_shape, out_specs = [], []
    if start:
        out_shape += [pltpu.SemaphoreType.DMA((len(start),))] * 2
        out_specs += [SEM_SPEC] * 2
    out_shape += [pltpu.HBM(b.shape, b.dtype) for b in bufs]
    out_specs += [HBM_SPEC] * n
    if token:
        out_shape.append(jax.ShapeDtypeStruct((8, 128), F32))
        out_specs.append(pl.BlockSpec(memory_space=pltpu.VMEM))
    outs = pl.pallas_call(
        body, name=name, in_specs=in_specs, out_specs=out_specs, out_shape=out_shape,
        input_output_aliases={i: n_start + i for i in range(n)},
        scratch_shapes=[pltpu.SemaphoreType.DMA((len(local),))] if local else [],
        compiler_params=pltpu.CompilerParams(has_side_effects=pltpu.SideEffectType.DATAFLOW_SIDE_EFFECTING),
    )(*args)
    sems = tuple(outs[:n_start]) if start else None
    return sems, list(outs[n_start:n_start + n]), (outs[n_start + n] if token else None)


def _adamw(w, g, m, v):
    m = ADAM_B1 * m + (1.0 - ADAM_B1) * g
    v = ADAM_B2 * v + (1.0 - ADAM_B2) * (g * g)
    m_hat = m / (1.0 - ADAM_B1 ** ADAM_STEP)
    v_hat = v / (1.0 - ADAM_B2 ** ADAM_STEP)
    delta = -ADAM_LR * (m_hat / (jnp.sqrt(v_hat) + ADAM_EPS) + ADAM_WD * w)
    return delta, m, v


def _reduce_adam(name, parts, w, m, v, *, tr):
    L, R, C = w.shape
    S = parts[0].shape[0]
    tr = min(tr, R)

    def body(*refs):
        p_refs = refs[:L]
        w_ref, m_ref, v_ref, g_ref, d_ref, nm_ref, nv_ref = refs[L:]
        for l in range(L):
            g = p_refs[l][0].astype(F32)
            for s in range(1, S):
                g = g + p_refs[l][s].astype(F32)
            g_ref[l] = g
            d_ref[l], nm_ref[l], nv_ref[l] = _adamw(w_ref[l], g, m_ref[l], v_ref[l])

    blk = pl.BlockSpec((L, tr, C), lambda r: (0, r, 0))
    out = jax.ShapeDtypeStruct((L, R, C), F32)
    return pl.pallas_call(
        body, name=name, grid=(R // tr,),
        in_specs=[pl.BlockSpec((S, tr, C), lambda r: (0, r, 0))] * L + [blk, blk, blk],
        out_specs=[blk] * 4, out_shape=[out] * 4,
        compiler_params=_params(dimension_semantics=("arbitrary",)),
    )(*parts, w, m, v)


def _small_reduce(parts):
    D = parts.shape[2]
    rows = [0, 1, 8, 9, 16, 17, 24, 25, 18, 19, 20, 2]

    def body(p_ref, out_ref):
        s = p_ref[0]
        for d in range(1, N_DEV):
            s = s + p_ref[d]
        out_ref[...] = jnp.zeros_like(out_ref)
        for r, src in enumerate(rows):
            out_ref[r:r + 1, :] = s[src:src + 1, :]

    return pl.pallas_call(body, name="small_reduce", out_shape=jax.ShapeDtypeStruct((16, D), F32))(parts)


def _small_adam(g_gain, g_taps, g_scale, gains, taps, scale):
    def body(gg, gt, gs, wg, mg, vg, wt, mt, vt, ws, ms, vs, *outs):
        for k, (g, w, m, v) in enumerate(((gg, wg, mg, vg), (gt, wt, mt, vt), (gs, ws, ms, vs))):
            outs[3 * k][...], outs[3 * k + 1][...], outs[3 * k + 2][...] = _adamw(w[...], g[...], m[...], v[...])

    shapes = [jax.ShapeDtypeStruct(t[0].shape, F32) for t in (gains, taps, scale) for _ in range(3)]
    return pl.pallas_call(body, name="small_adam", out_shape=shapes)(g_gain, g_taps, g_scale, *gains, *taps, *scale)


def kernel(x, norm_gains, pool_w, pool_scale, conv_in_w, conv_w, conv_out_w, ffn_gate_up_w, ffn_down_w, loss_target, m_norm_gains, m_pool_w, m_pool_scale, m_conv_in_w, m_conv_w, m_conv_out_w, m_ffn_gate_up_w, m_ffn_down_w, v_norm_gains, v_pool_w, v_pool_scale, v_conv_in_w, v_conv_w, v_conv_out_w, v_ffn_gate_up_w, v_ffn_down_w):
    T, D = x.shape[1], x.shape[2]
    tm = min(256, T)
    tk = min(512, T)
    n_layers = ffn_gate_up_w.shape[0]
    fb = ffn_gate_up_w.shape[2]
    fr = ffn_down_w.shape[1]
    dcol = norm_gains.shape[2]
    cb = conv_in_w.shape[2]
    gw = pool_w.shape[3]
    me = 4 * lax.axis_index("x") + 2 * lax.axis_index("y") + lax.axis_index("c")

    small_w = jnp.concatenate([norm_gains.reshape(8, dcol), jnp.pad(conv_w[0], ((0, 5), (0, 0)))], axis=0)
    every = range(1, N_DEV)
    _, _, pw_g, small_g = _exchange(
        "gather_small",
        [pool_w[0].astype(BF16), small_w, lax.empty((4, N_DEV, gw // N_DEV, gw), BF16), lax.empty((N_DEV, 16, dcol), F32)],
        [Copy(m, 0, _whole, 2, _second, m - 1) for m in every] + [Copy(m, 1, _whole, 3, _lead, N_DEV - 2 + m) for m in every],
        [Local(0, _whole, 2, _second), Local(1, _whole, 3, _lead)])
    pw = pw_g.reshape(4, gw, gw)
    small_full = jnp.swapaxes(small_g, 0, 1).reshape(16, D)
    gain = lambda l, s: small_full[4 * l + s][None, :]
    taps = small_full[8:16]

    wgu_t, m_wgu_t, v_wgu_t = (jnp.swapaxes(a, 1, 2) for a in (ffn_gate_up_w, m_ffn_gate_up_w, v_ffn_gate_up_w))
    shards = [w.astype(BF16) for w in (wgu_t[0], ffn_down_w[0], conv_in_w[0], conv_out_w[0], wgu_t[1], ffn_down_w[1])]
    n_big = len(shards)
    level1 = [Copy(mask, n, _whole, n_big + n, _lead, 4 * n + j)
              for n in range(n_big) for j, mask in enumerate((SIBLING,) + OTHER_CHIPS)]
    lands = [lax.dynamic_update_slice(lax.empty((N_DEV,) + s.shape, BF16), s[None], (me,) + (0,) * s.ndim) for s in shards]
    sems1, bufs1, started = _split_call("gather_start", shards + lands, start=level1, after=small_g, token=True)

    def gathered(name, group, after):
        k = len(group)
        arrived = [Copy(cp.mask, group.index(cp.sb), cp.src, k + group.index(cp.sb), cp.dst, cp.sem)
                   for cp in level1 if cp.sb in group]
        onward = [Copy(SIBLING, i, (lambda ref, me, m=m: ref.at[me ^ m]), i, (lambda ref, sender, m=m: ref.at[sender ^ m]), 3 * i + j)
                  for i in range(k) for j, m in enumerate(OTHER_CHIPS)]
        sems2, bufs2, _ = _split_call(
            name + "_forward", [bufs1[n] for n in group] + [bufs1[n_big + n] for n in group],
            wait=arrived, wait_sems=sems1, start=[cp._replace(sb=k + cp.sb, db=k + cp.db) for cp in onward], after=after)
        return _split_call(name + "_done", bufs2[k:], wait=onward, wait_sems=sems2)[1]

    h0 = x[0]
    h1 = _pool_fwd(h0, pw, pool_scale, gain(0, 0), gain(0, 1), tm=tm, after=started)
    wgu0, wd0 = gathered("gather_ffn0", [0, 1], h1)
    wd0 = wd0.reshape(N_DEV * fr, D)
    h2, gu0, ff0 = _ffn_fwd(h1, gain(0, 2), gain(0, 3), wgu0, wd0, 0, None, tm=tm)
    win_g, wout_g = gathered("gather_conv", [2, 3], h2)
    wout = wout_g.reshape(D, D)
    h3, proj, y = _conv_fwd(h2, gain(1, 0), gain(1, 1), win_g, taps, wout, tm=tm)
    wgu1, wd1 = gathered("gather_ffn1", [4, 5], h3)
    wd1 = wd1.reshape(N_DEV * fr, D)
    dh4, gu1, ff1, loss_part = _ffn_fwd(h3, gain(1, 2), gain(1, 3), wgu1, wd1, 1, loss_target[0], tm=tm)
    loss = lax.psum(loss_part[0, 0], ("x", "y", "c"))

    seq = lambda i, k: (k, 0)
    dh3, dgu1, dff1, c1, act1, small_f1 = _ffn_bwd(dh4, h3, ff1, gu1, gain(1, 2), gain(1, 3), wgu1, wd1, 1, tm=tm)
    g_wgu1 = _wgrad("wgrad_gate_up_1", dgu1, c1, pl.BlockSpec((None, tk, fb), lambda i, k: (i, k, 0)),
                    pl.BlockSpec((tk, D), seq), (fb, D), N_DEV, tk=tk)
    g_wd1 = _wgrad("wgrad_down_1", act1, dff1, pl.BlockSpec((None, tk, fb), lambda i, k: (i, k, 0)),
                   pl.BlockSpec((tk, D), seq), (fb, D), N_DEV // 2, tk=tk)
    dh2, dproj, z, a1, dy, small_c = _conv_bwd(dh3, h2, y, proj, gain(1, 0), gain(1, 1), win_g, taps, wout, tm=tm)
    g_win = _wgrad("wgrad_conv_in", a1, dproj, pl.BlockSpec((tk, D), seq),
                   pl.BlockSpec((tk, cb), lambda i, k: (k, i)), (D, cb), N_DEV, tk=tk)
    g_wout = _wgrad("wgrad_conv_out", z, dy, pl.BlockSpec((tk, D), seq), pl.BlockSpec((tk, D), seq), (D, D), 1, tk=tk)
    dh1, dgu0, dff0, c0, act0, small_f0 = _ffn_bwd(dh2, h1, ff0, gu0, gain(0, 2), gain(0, 3), wgu0, wd0, 0, tm=tm)
    g_wgu0 = _wgrad("wgrad_gate_up_0", dgu0, c0, pl.BlockSpec((None, tk, fb), lambda i, k: (i, k, 0)),
                    pl.BlockSpec((tk, D), seq), (fb, D), N_DEV, tk=tk)
    g_wd0 = _wgrad("wgrad_down_0", act0, dff0, pl.BlockSpec((None, tk, fb), lambda i, k: (i, k, 0)),
                   pl.BlockSpec((tk, D), seq), (fb, D), N_DEV // 2, tk=tk)
    grad_x, g_pw, small_p = _pool_bwd(dh1, h0, pw, pool_scale, gain(0, 0), gain(0, 1), tm=tm)

    small_part = jnp.concatenate([small_p, small_f0, small_c, small_f1], axis=0)
    grads = [g_pw.reshape(4, N_DEV, gw // N_DEV, gw), g_win, g_wout.reshape(N_DEV, dcol, D),
             g_wgu0, g_wgu1, g_wd0.reshape(N_DEV, fr, D), g_wd1.reshape(N_DEV, fr, D), small_part]
    n_g = len(grads)
    lands = [lax.empty(g.shape, g.dtype) for g in grads[:-1]] + [lax.empty((N_DEV,) + small_part.shape, F32)]
    at = [_second] + [_lead] * (n_g - 1)
    send_at = at[:-1] + [_whole]
    plan = [Copy(m, n, (lambda ref, me, m=m, f=send_at[n]: f(ref, me ^ m)), n_g + n, at[n], (N_DEV - 1) * n + m - 1)
            for n in range(n_g) for m in every]
    r_pw, r_win, r_wout, r_wgu0, r_wgu1, r_wd0, r_wd1, r_small = _exchange(
        "scatter_grads", grads + lands, plan, [Local(n, send_at[n], n_g + n, at[n]) for n in range(n_g)])[n_g:]

    o_pw = _reduce_adam("adam_pool_w", [r_pw[g] for g in range(4)], pool_w[0], m_pool_w[0], v_pool_w[0], tr=32)
    o_win = _reduce_adam("adam_conv_in", [r_win], conv_in_w, m_conv_in_w, v_conv_in_w, tr=256)
    o_wout = _reduce_adam("adam_conv_out", [r_wout], conv_out_w, m_conv_out_w, v_conv_out_w, tr=128)
    o_wgu = [jnp.swapaxes(o, 1, 2) for o in _reduce_adam("adam_gate_up", [r_wgu0, r_wgu1], wgu_t, m_wgu_t, v_wgu_t, tr=176)]
    o_wd = _reduce_adam("adam_down", [r_wd0, r_wd1], ffn_down_w, m_ffn_down_w, v_ffn_down_w, tr=176)
    g_small = _small_reduce(r_small)
    g_cols = lax.dynamic_slice(g_small, (0, me * dcol), (16, dcol))
    o_small = _small_adam(
        g_cols[0:8], g_cols[8:11], g_small[11:12],
        (norm_gains.reshape(8, dcol), m_norm_gains.reshape(8, dcol), v_norm_gains.reshape(8, dcol)),
        (conv_w[0], m_conv_w[0], v_conv_w[0]), (pool_scale, m_pool_scale, v_pool_scale))
    d_gain, nm_gain, nv_gain, d_taps, nm_taps, nv_taps, d_scale, nm_scale, nv_scale = o_small

    gshape = norm_gains.shape
    per = lambda k: (
        (g_cols[0:8].reshape(gshape), d_gain.reshape(gshape), nm_gain.reshape(gshape), nv_gain.reshape(gshape))[k],
        o_pw[k][None], (g_small[11:12], d_scale, nm_scale, nv_scale)[k], o_win[k],
        (g_cols[8:11][None], d_taps[None], nm_taps[None], nv_taps[None])[k], o_wout[k], o_wgu[k], o_wd[k])
    return (loss, grad_x[None], *per(0), *per(1), *per(2), *per(3))
```

```python
import collections
import functools

import jax
import jax.numpy as jnp
from jax import lax
from jax.experimental import pallas as pl
from jax.experimental.pallas import tpu as pltpu

N_DEV = 8
RMS_EPS = 1e-6
POOL_WINDOWS = (2, 4, 8, 16)
POOL_HALO = 16
CONV_HALO = 16
ADAM_LR, ADAM_B1, ADAM_B2, ADAM_EPS, ADAM_WD, ADAM_STEP = 0.001, 0.9, 0.999, 1e-08, 0.01, 10

VMEM_LIMIT = 56 * 2**20
BF16 = jnp.bfloat16
F32 = jnp.float32
MESH = pl.DeviceIdType.MESH


def _params(**kw):
    return pltpu.CompilerParams(vmem_limit_bytes=VMEM_LIMIT, **kw)


def _resident(shape, index_map):
    return pl.BlockSpec(shape, index_map, pipeline_mode=pl.Buffered(1))


def _ordered(body, n_in, after):
    if after is None:
        return functools.partial(body), [], []
    return (lambda *refs: body(*refs[:n_in], *refs[n_in + 1:])), [after], [pl.BlockSpec(memory_space=pl.ANY)]


def _rms_fwd(x, g):
    r = lax.rsqrt(jnp.mean(x * x, axis=-1, keepdims=True) + RMS_EPS)
    return x * r * g


def _rms_bwd(x, g, dy):
    r = lax.rsqrt(jnp.mean(x * x, axis=-1, keepdims=True) + RMS_EPS)
    xhat = x * r
    dg = jnp.sum(dy * xhat, axis=0, keepdims=True)
    t = dy * g
    dx = r * (t - xhat * jnp.mean(t * xhat, axis=-1, keepdims=True))
    return dx, dg


def _dot(a, b):
    return jnp.dot(a, b, preferred_element_type=F32)


def _dot_nt(a, b):
    return lax.dot_general(a, b, (((1,), (1,)), ((), ())), preferred_element_type=F32)


def _dot_tn(a, b):
    return lax.dot_general(a, b, (((0,), (0,)), ((), ())), preferred_element_type=F32)


def _row_inverse_counts(tile, tm):
    pos = (lax.broadcasted_iota(jnp.int32, (tm, 1), 0) + tile * tm + 1).astype(F32)
    return [1.0 / jnp.minimum(pos, float(w)) for w in POOL_WINDOWS]


def _pool_from_ext(ext, a, invs, gw):
    s = ext
    outs = []
    for g, w in enumerate(POOL_WINDOWS):
        s = s[:, (gw if g else 0):]
        s = s + pltpu.roll(s, w // 2, 0)
        outs.append(s[POOL_HALO:, :gw] * invs[g] - a[:, g * gw:(g + 1) * gw])
    return outs


def _pool_fwd(h, pw, scale, g_pre, g_post, *, tm, after=None):
    T, D = h.shape
    gw = D // len(POOL_WINDOWS)
    hb = tm // POOL_HALO

    def body(h_ref, halo_ref, pw_ref, scale_ref, gpre_ref, gpost_ref, out_ref, ext_ref):
        i = pl.program_id(0)
        x = h_ref[...]
        a = _rms_fwd(x, gpre_ref[...])
        ah = _rms_fwd(halo_ref[...], gpre_ref[...])
        ext_ref[0:POOL_HALO, :] = jnp.where(i == 0, 0.0, ah)
        ext_ref[POOL_HALO:, :] = a
        pooled = _pool_from_ext(ext_ref[...], a, _row_inverse_counts(i, tm), gw)
        mixed = jnp.concatenate([_dot(p.astype(BF16), pw_ref[g]) for g, p in enumerate(pooled)], axis=1)
        out_ref[...] = x + _rms_fwd(mixed * scale_ref[...], gpost_ref[...])

    vec = _resident((1, D), lambda i: (0, 0))
    fn, xa, xs = _ordered(body, 6, after)
    return pl.pallas_call(
        fn, name="pool_fwd", grid=(T // tm,),
        in_specs=[pl.BlockSpec((tm, D), lambda i: (i, 0)),
                  pl.BlockSpec((POOL_HALO, D), lambda i: (jnp.maximum(i * hb - 1, 0), 0)),
                  _resident(pw.shape, lambda i: (0, 0, 0)), vec, vec, vec] + xs,
        out_specs=pl.BlockSpec((tm, D), lambda i: (i, 0)),
        out_shape=jax.ShapeDtypeStruct((T, D), F32),
        scratch_shapes=[pltpu.VMEM((tm + POOL_HALO, D), F32)],
        compiler_params=_params(dimension_semantics=("arbitrary",)),
    )(h, h, pw, scale, g_pre, g_post, *xa)


def _pool_bwd(dh, h, pw, scale, g_pre, g_post, *, tm, after=None):
    T, D = h.shape
    gw = D // len(POOL_WINDOWS)
    hb = tm // POOL_HALO
    nt = T // tm
    n_ext = tm + POOL_HALO

    def body(dh_ref, h_ref, halo_ref, pw_ref, scale_ref, gpre_ref, gpost_ref,
             dx_ref, dpw_ref, small_ref, ext_ref, ext2_ref, carry_ref, dpw_acc):
        i = pl.program_id(0)
        tile = nt - 1 - i

        @pl.when(i == 0)
        def _():
            small_ref[...] = jnp.zeros_like(small_ref)
            dpw_acc[...] = jnp.zeros_like(dpw_acc)
            carry_ref[...] = jnp.zeros_like(carry_ref)

        x = h_ref[...]
        dout = dh_ref[...]
        a = _rms_fwd(x, gpre_ref[...])
        ah = _rms_fwd(halo_ref[...], gpre_ref[...])
        ext_ref[0:POOL_HALO, :] = jnp.where(tile == 0, 0.0, ah)
        ext_ref[POOL_HALO:, :] = a
        invs = _row_inverse_counts(tile, tm)
        pooled = [p.astype(BF16) for p in _pool_from_ext(ext_ref[...], a, invs, gw)]
        mixed_pre = jnp.concatenate([_dot(p, pw_ref[g]) for g, p in enumerate(pooled)], axis=1)
        scale_v = scale_ref[...]
        dmixed, dg_post = _rms_bwd(mixed_pre * scale_v, gpost_ref[...], dout)
        small_ref[1:2, :] += dg_post
        small_ref[2:3, :] += jnp.sum(dmixed * mixed_pre, axis=0, keepdims=True)
        dpre = (dmixed * scale_v).astype(BF16)
        dpooled = []
        for g in range(len(POOL_WINDOWS)):
            dp = dpre[:, g * gw:(g + 1) * gw]
            dpw_acc[g] += _dot_tn(pooled[g], dp)
            dpooled.append(_dot_nt(dp, pw_ref[g]))
        q = jnp.concatenate([d * invs[g] for g, d in enumerate(dpooled)], axis=1)
        ext2_ref[0:tm, :] = q
        ext2_ref[tm:, :] = carry_ref[...]
        carry_ref[...] = q[0:POOL_HALO, :]
        s = ext2_ref[...]
        da = []
        for g, w in enumerate(POOL_WINDOWS):
            s = s[:, (gw if g else 0):]
            s = s + pltpu.roll(s, n_ext - w // 2, 0)
            da.append(s[0:tm, :gw] - dpooled[g])
        dx, dg_pre = _rms_bwd(x, gpre_ref[...], jnp.concatenate(da, axis=1))
        small_ref[0:1, :] += dg_pre
        dx_ref[...] = dout + dx

        @pl.when(i == nt - 1)
        def _():
            dpw_ref[...] = dpw_acc[...].astype(BF16)

    vec = _resident((1, D), lambda i: (0, 0))
    rev = lambda i: (nt - 1 - i, 0)
    fn, xa, xs = _ordered(body, 7, after)
    return pl.pallas_call(
        fn, name="pool_bwd", grid=(nt,),
        in_specs=[pl.BlockSpec((tm, D), rev), pl.BlockSpec((tm, D), rev),
                  pl.BlockSpec((POOL_HALO, D), lambda i: (jnp.maximum((nt - 1 - i) * hb - 1, 0), 0)),
                  _resident(pw.shape, lambda i: (0, 0, 0)), vec, vec, vec] + xs,
        out_specs=[pl.BlockSpec((tm, D), rev),
                   pl.BlockSpec(pw.shape, lambda i: (0, 0, 0)),
                   pl.BlockSpec((8, D), lambda i: (0, 0))],
        out_shape=[jax.ShapeDtypeStruct((T, D), F32), jax.ShapeDtypeStruct(pw.shape, BF16),
                   jax.ShapeDtypeStruct((8, D), F32)],
        scratch_shapes=[pltpu.VMEM((n_ext, D), F32), pltpu.VMEM((n_ext, D), F32),
                        pltpu.VMEM((POOL_HALO, D), F32), pltpu.VMEM(pw.shape, F32)],
        compiler_params=_params(dimension_semantics=("arbitrary",)),
    )(dh, h, h, pw, scale, g_pre, g_post, *xa)


def _ffn_fwd(h, g_pre, g_post, wgu, wd, layer, target, *, tm, after=None):
    T, D = h.shape
    nblk, fb = wgu.shape[0], wgu.shape[1]
    half = nblk // 2
    last = target is not None

    def body(*refs):
        if last:
            h_ref, gpre_ref, gpost_ref, wgu_ref, wd_ref, tgt_ref, out_ref, gu_ref, ff_ref, loss_ref = refs
        else:
            h_ref, gpre_ref, gpost_ref, wgu_ref, wd_ref, out_ref, gu_ref, ff_ref = refs
        x = h_ref[...]
        cb = _rms_fwd(x, gpre_ref[...]).astype(BF16)
        acc = jnp.zeros((tm, D), F32)
        for j in range(half):
            g = _dot_nt(cb, wgu_ref[j])
            u = _dot_nt(cb, wgu_ref[j + half])
            gu_ref[j] = g.astype(BF16)
            gu_ref[j + half] = u.astype(BF16)
            act = (g * jax.nn.sigmoid(g) * u).astype(BF16)
            acc = acc + _dot(act, wd_ref[j * fb:(j + 1) * fb, :])
        ff_ref[...] = acc.astype(BF16)
        hout = x + _rms_fwd(acc, gpost_ref[...])
        if last:
            diff = hout - tgt_ref[...]
            out_ref[...] = diff * (1.0 / D)

            @pl.when(pl.program_id(0) == 0)
            def _():
                loss_ref[...] = jnp.zeros_like(loss_ref)

            loss_ref[...] += jnp.sum(diff * diff) * (0.5 / D)
        else:
            out_ref[...] = hout

    vec = _resident((1, D), lambda i: (0, 0))
    tile = pl.BlockSpec((tm, D), lambda i: (i, 0))
    in_specs = [tile, vec, vec,
                _resident(wgu.shape, lambda i: (0, 0, 0)), _resident(wd.shape, lambda i: (0, 0))]
    out_specs = [tile, pl.BlockSpec((nblk, tm, fb), lambda i: (0, i, 0)), tile]
    out_shape = [jax.ShapeDtypeStruct((T, D), F32), jax.ShapeDtypeStruct((nblk, T, fb), BF16),
                 jax.ShapeDtypeStruct((T, D), BF16)]
    args = [h, g_pre, g_post, wgu, wd]
    if last:
        in_specs.append(tile)
        args.append(target)
        out_specs.append(pl.BlockSpec((8, 128), lambda i: (0, 0)))
        out_shape.append(jax.ShapeDtypeStruct((8, 128), F32))
    fn, xa, xs = _ordered(body, len(args), after)
    return pl.pallas_call(
        fn, name=f"ffn_fwd_{layer}", grid=(T // tm,), in_specs=in_specs + xs, out_specs=out_specs,
        out_shape=out_shape, compiler_params=_params(dimension_semantics=("arbitrary",)),
    )(*args, *xa)


def _ffn_bwd(dh, h, ff, gu, g_pre, g_post, wgu, wd, layer, *, tm, after=None):
    T, D = h.shape
    nblk, fb = wgu.shape[0], wgu.shape[1]
    half = nblk // 2

    def body(dh_ref, h_ref, ff_ref, gu_ref, gpre_ref, gpost_ref, wgu_ref, wd_ref,
             dx_ref, dgu_ref, dff_ref, c_ref, act_ref, small_ref):
        @pl.when(pl.program_id(0) == 0)
        def _():
            small_ref[...] = jnp.zeros_like(small_ref)

        dout = dh_ref[...]
        dff, dg_post = _rms_bwd(ff_ref[...].astype(F32), gpost_ref[...], dout)
        small_ref[1:2, :] += dg_post
        dffb = dff.astype(BF16)
        dff_ref[...] = dffb
        dc = jnp.zeros((tm, D), F32)
        for j in range(half):
            g = gu_ref[j].astype(F32)
            u = gu_ref[j + half].astype(F32)
            s = jax.nn.sigmoid(g)
            silu = g * s
            act_ref[j] = (silu * u).astype(BF16)
            dact = _dot_nt(dffb, wd_ref[j * fb:(j + 1) * fb, :])
            dg = (dact * u * (s * (1.0 + g * (1.0 - s)))).astype(BF16)
            du = (dact * silu).astype(BF16)
            dgu_ref[j] = dg
            dgu_ref[j + half] = du
            dc = dc + _dot(dg, wgu_ref[j]) + _dot(du, wgu_ref[j + half])
        x = h_ref[...]
        c_ref[...] = _rms_fwd(x, gpre_ref[...]).astype(BF16)
        dx, dg_pre = _rms_bwd(x, gpre_ref[...], dc)
        small_ref[0:1, :] += dg_pre
        dx_ref[...] = dout + dx

    vec = _resident((1, D), lambda i: (0, 0))
    tile = pl.BlockSpec((tm, D), lambda i: (i, 0))
    blk = pl.BlockSpec((nblk, tm, fb), lambda i: (0, i, 0))
    fn, xa, xs = _ordered(body, 8, after)
    return pl.pallas_call(
        fn, name=f"ffn_bwd_{layer}", grid=(T // tm,),
        in_specs=[tile, tile, tile, blk, vec, vec,
                  _resident(wgu.shape, lambda i: (0, 0, 0)), _resident(wd.shape, lambda i: (0, 0))] + xs,
        out_specs=[tile, blk, tile, tile, pl.BlockSpec((half, tm, fb), lambda i: (0, i, 0)),
                   pl.BlockSpec((8, D), lambda i: (0, 0))],
        out_shape=[jax.ShapeDtypeStruct((T, D), F32), jax.ShapeDtypeStruct((nblk, T, fb), BF16),
                   jax.ShapeDtypeStruct((T, D), BF16), jax.ShapeDtypeStruct((T, D), BF16),
                   jax.ShapeDtypeStruct((half, T, fb), BF16), jax.ShapeDtypeStruct((8, D), F32)],
        compiler_params=_params(dimension_semantics=("arbitrary",)),
    )(dh, h, ff, gu, g_pre, g_post, wgu, wd, *xa)


def _conv_fwd(h, g_pre, g_post, win, taps, wout, *, tm, after=None):
    T, D = h.shape
    nblk, cb = win.shape[0], win.shape[2]

    def body(h_ref, gpre_ref, gpost_ref, win_ref, taps_ref, wout_ref,
             out_ref, proj_ref, y_ref, proj_scr, ext_ref, carry_ref):
        i = pl.program_id(0)

        @pl.when(i == 0)
        def _():
            carry_ref[...] = jnp.zeros_like(carry_ref)

        x = h_ref[...]
        a = _rms_fwd(x, gpre_ref[...]).astype(BF16)
        for k in range(nblk):
            proj_scr[:, k * cb:(k + 1) * cb] = _dot(a, win_ref[k])
        proj_ref[...] = proj_scr[...].astype(BF16)
        u = proj_scr[:, D:2 * D] * proj_scr[:, 2 * D:3 * D]
        ext_ref[0:CONV_HALO, :] = carry_ref[...]
        ext_ref[CONV_HALO:, :] = u
        carry_ref[...] = u[tm - CONV_HALO:, :]
        e = ext_ref[...]
        conv = (taps_ref[2:3, :] * u + taps_ref[1:2, :] * pltpu.roll(e, 1, 0)[CONV_HALO:, :]
                + taps_ref[0:1, :] * pltpu.roll(e, 2, 0)[CONV_HALO:, :])
        z = (proj_scr[:, 0:D] * conv).astype(BF16)
        y = _dot(z, wout_ref[...])
        y_ref[...] = y.astype(BF16)
        out_ref[...] = x + _rms_fwd(y, gpost_ref[...])

    vec = _resident((1, D), lambda i: (0, 0))
    tile = pl.BlockSpec((tm, D), lambda i: (i, 0))
    fn, xa, xs = _ordered(body, 6, after)
    return pl.pallas_call(
        fn, name="conv_fwd", grid=(T // tm,),
        in_specs=[tile, vec, vec, _resident(win.shape, lambda i: (0, 0, 0)),
                  _resident(taps.shape, lambda i: (0, 0)), _resident(wout.shape, lambda i: (0, 0))] + xs,
        out_specs=[tile, pl.BlockSpec((tm, 3 * D), lambda i: (i, 0)), tile],
        out_shape=[jax.ShapeDtypeStruct((T, D), F32), jax.ShapeDtypeStruct((T, 3 * D), BF16),
                   jax.ShapeDtypeStruct((T, D), BF16)],
        scratch_shapes=[pltpu.VMEM((tm, 3 * D), F32), pltpu.VMEM((tm + CONV_HALO, D), F32),
                        pltpu.VMEM((CONV_HALO, D), F32)],
        compiler_params=_params(dimension_semantics=("arbitrary",)),
    )(h, g_pre, g_post, win, taps, wout, *xa)


def _conv_bwd(dh, h, y, proj, g_pre, g_post, win, taps, wout, *, tm, after=None):
    T, D = h.shape
    nblk, cb = win.shape[0], win.shape[2]
    nt = T // tm
    hb = tm // CONV_HALO
    n_ext = tm + CONV_HALO

    def body(dh_ref, h_ref, y_ref, proj_ref, halo_ref, gpre_ref, gpost_ref, win_ref, taps_ref, wout_ref,
             dx_ref, dproj_ref, z_ref, a_ref, dy_ref, small_ref, ext_ref, ext2_ref, carry_ref):
        i = pl.program_id(0)
        tile = nt - 1 - i

        @pl.when(i == 0)
        def _():
            small_ref[...] = jnp.zeros_like(small_ref)
            carry_ref[...] = jnp.zeros_like(carry_ref)

        dout = dh_ref[...]
        dy, dg_post = _rms_bwd(y_ref[...].astype(F32), gpost_ref[...], dout)
        small_ref[1:2, :] += dg_post
        dyb = dy.astype(BF16)
        dy_ref[...] = dyb
        dz = _dot_nt(dyb, wout_ref[...])
        bgate = proj_ref[:, 0:D].astype(F32)
        cgate = proj_ref[:, D:2 * D].astype(F32)
        v = proj_ref[:, 2 * D:3 * D].astype(F32)
        u = cgate * v
        uh = halo_ref[:, D:2 * D].astype(F32) * halo_ref[:, 2 * D:3 * D].astype(F32)
        ext_ref[0:CONV_HALO, :] = jnp.where(tile == 0, 0.0, uh)
        ext_ref[CONV_HALO:, :] = u
        e = ext_ref[...]
        u1 = pltpu.roll(e, 1, 0)[CONV_HALO:, :]
        u2 = pltpu.roll(e, 2, 0)[CONV_HALO:, :]
        t0, t1, t2 = taps_ref[0:1, :], taps_ref[1:2, :], taps_ref[2:3, :]
        conv = t2 * u + t1 * u1 + t0 * u2
        z_ref[...] = (bgate * conv).astype(BF16)
        dconv = dz * bgate
        small_ref[2:3, :] += jnp.sum(dconv * u2, axis=0, keepdims=True)
        small_ref[3:4, :] += jnp.sum(dconv * u1, axis=0, keepdims=True)
        small_ref[4:5, :] += jnp.sum(dconv * u, axis=0, keepdims=True)
        ext2_ref[0:tm, :] = dconv
        ext2_ref[tm:, :] = carry_ref[...]
        carry_ref[...] = dconv[0:CONV_HALO, :]
        e2 = ext2_ref[...]
        du = (t2 * dconv + t1 * pltpu.roll(e2, n_ext - 1, 0)[0:tm, :]
              + t0 * pltpu.roll(e2, n_ext - 2, 0)[0:tm, :])
        dproj_ref[:, 0:D] = (dz * conv).astype(BF16)
        dproj_ref[:, D:2 * D] = (du * v).astype(BF16)
        dproj_ref[:, 2 * D:3 * D] = (du * cgate).astype(BF16)
        da = jnp.zeros((tm, D), F32)
        for k in range(nblk):
            da = da + _dot_nt(dproj_ref[:, k * cb:(k + 1) * cb], win_ref[k])
        x = h_ref[...]
        a_ref[...] = _rms_fwd(x, gpre_ref[...]).astype(BF16)
        dx, dg_pre = _rms_bwd(x, gpre_ref[...], da)
        small_ref[0:1, :] += dg_pre
        dx_ref[...] = dout + dx

    vec = _resident((1, D), lambda i: (0, 0))
    rev = lambda i: (nt - 1 - i, 0)
    tile = pl.BlockSpec((tm, D), rev)
    wide = pl.BlockSpec((tm, 3 * D), rev)
    fn, xa, xs = _ordered(body, 10, after)
    return pl.pallas_call(
        fn, name="conv_bwd", grid=(nt,),
        in_specs=[tile, tile, tile, wide,
                  pl.BlockSpec((CONV_HALO, 3 * D), lambda i: (jnp.maximum((nt - 1 - i) * hb - 1, 0), 0)),
                  vec, vec, _resident(win.shape, lambda i: (0, 0, 0)),
                  _resident(taps.shape, lambda i: (0, 0)), _resident(wout.shape, lambda i: (0, 0))] + xs,
        out_specs=[tile, wide, tile, tile, tile, pl.BlockSpec((8, D), lambda i: (0, 0))],
        out_shape=[jax.ShapeDtypeStruct((T, D), F32), jax.ShapeDtypeStruct((T, 3 * D), BF16),
                   jax.ShapeDtypeStruct((T, D), BF16), jax.ShapeDtypeStruct((T, D), BF16),
                   jax.ShapeDtypeStruct((T, D), BF16), jax.ShapeDtypeStruct((8, D), F32)],
        scratch_shapes=[pltpu.VMEM((n_ext, D), F32), pltpu.VMEM((n_ext, D), F32),
                        pltpu.VMEM((CONV_HALO, D), F32)],
        compiler_params=_params(dimension_semantics=("arbitrary",)),
    )(dh, h, y, proj, proj, g_pre, g_post, win, taps, wout, *xa)


def _wgrad(name, a, b, a_spec, b_spec, out_block, n_out, *, tk):
    T = a.shape[-2]
    nk = T // tk

    def body(a_ref, b_ref, out_ref, acc_ref):
        k = pl.program_id(1)

        @pl.when(k == 0)
        def _():
            acc_ref[...] = jnp.zeros_like(acc_ref)

        acc_ref[...] += _dot_tn(a_ref[...], b_ref[...])

        @pl.when(k == nk - 1)
        def _():
            out_ref[...] = acc_ref[...].astype(BF16)

    return pl.pallas_call(
        body, name=name, grid=(n_out, nk), in_specs=[a_spec, b_spec],
        out_specs=pl.BlockSpec((None,) + out_block, lambda i, k: (i, 0, 0)),
        out_shape=jax.ShapeDtypeStruct((n_out,) + out_block, BF16),
        scratch_shapes=[pltpu.VMEM(out_block, F32)],
        compiler_params=_params(dimension_semantics=("arbitrary", "arbitrary")),
    )(a, b)


Copy = collections.namedtuple("Copy", "mask sb src db dst sem")
Local = collections.namedtuple("Local", "sb src db dst")

HBM_SPEC = pl.BlockSpec(memory_space=pltpu.HBM)
SEM_SPEC = pl.BlockSpec(memory_space=pltpu.SEMAPHORE)
SIBLING, X_PEER, Y_PEER, DIAGONAL = 1, 4, 2, 6
OTHER_CHIPS = (X_PEER, Y_PEER, DIAGONAL)


def _whole(ref, i):
    return ref


def _lead(ref, i):
    return ref.at[i]


def _second(ref, i):
    return ref.at[:, i]


def _place():
    x, y, c = lax.axis_index("x"), lax.axis_index("y"), lax.axis_index("c")
    return (x, y, c), 4 * x + 2 * y + c


def _descriptor(cp, bufs, xyc, me, sender, send_sems, recv_sems):
    x, y, c = xyc
    flip = lambda v, bit: (1 - v) if bit else v
    return pltpu.make_async_remote_copy(
        src_ref=cp.src(bufs[cp.sb], me), dst_ref=cp.dst(bufs[cp.db], sender),
        send_sem=send_sems.at[cp.sem], recv_sem=recv_sems.at[cp.sem],
        device_id=(flip(x, cp.mask & 4), flip(y, cp.mask & 2), flip(c, cp.mask & 1)), device_id_type=MESH)


def _exchange(name, bufs, plan, local=()):
    n = len(bufs)

    def body(*refs):
        ins = refs[:n]
        send_sems, recv_sems, local_sems = refs[2 * n:]
        xyc, me = _place()
        own = [pltpu.make_async_copy(lc.src(ins[lc.sb], me), lc.dst(ins[lc.db], me), local_sems.at[i])
               for i, lc in enumerate(local)]
        sends = [_descriptor(cp, ins, xyc, me, me, send_sems, recv_sems) for cp in plan]
        for cp in own + sends:
            cp.start()
        for cp in plan:
            _descriptor(cp, ins, xyc, me, me ^ cp.mask, send_sems, recv_sems).wait_recv()
        for cp in sends:
            cp.wait_send()
        for cp in own:
            cp.wait()

    return pl.pallas_call(
        body, name=name, in_specs=[HBM_SPEC] * n, out_specs=[HBM_SPEC] * n,
        out_shape=[jax.ShapeDtypeStruct(b.shape, b.dtype) for b in bufs],
        input_output_aliases={i: i for i in range(n)},
        scratch_shapes=[pltpu.SemaphoreType.DMA((len(plan),)), pltpu.SemaphoreType.DMA((len(plan),)),
                        pltpu.SemaphoreType.DMA((max(len(local), 1),))],
    )(*bufs)


def _split_call(name, bufs, *, wait=None, wait_sems=None, start=None, local=(), after=None, token=False):
    n = len(bufs)
    n_wait = 2 if wait else 0
    n_after = 1 if after is not None else 0
    n_start = 2 if start else 0

    def body(*refs):
        ins = refs[:n]
        wsend, wrecv = refs[n:n + n_wait] if wait else (None, None)
        outs = refs[n + n_wait + n_after:]
        ssend, srecv = outs[:n_start] if start else (None, None)
        rest = outs[n_start + n:]
        xyc, me = _place()
        for cp in wait or ():
            d = _descriptor(cp, ins, xyc, me, me ^ cp.mask, wsend, wrecv)
            d.wait_send()
            d.wait_recv()
        own = [pltpu.make_async_copy(lc.src(ins[lc.sb], me), lc.dst(ins[lc.db], me), rest[-1].at[i])
               for i, lc in enumerate(local)]
        for cp in own:
            cp.start()
        for cp in start or ():
            _descriptor(cp, ins, xyc, me, me, ssend, srecv).start()
        for cp in own:
            cp.wait()
        if token:
            rest[0][...] = jnp.zeros_like(rest[0])

    args = [pltpu.with_memory_space_constraint(b, pltpu.HBM) for b in bufs]
    in_specs = [HBM_SPEC] * n
    if wait:
        args += list(wait_sems)
        in_specs += [SEM_SPEC] * 2
    if after is not None:
        args.append(after)
        in_specs.append(pl.BlockSpec(memory_space=pl.ANY))
    out_shape, out_specs = [], []
    if start:
        out_shape += [pltpu.SemaphoreType.DMA((len(start),))] * 2
        out_specs += [SEM_SPEC] * 2
    out_shape += [pltpu.HBM(b.shape, b.dtype) for b in bufs]
    out_specs += [HBM_SPEC] * n
    if token:
        out_shape.append(jax.ShapeDtypeStruct((8, 128), F32))
        out_specs.append(pl.BlockSpec(memory_space=pltpu.VMEM))
    outs = pl.pallas_call(
        body, name=name, in_specs=in_specs, out_specs=out_specs, out_shape=out_shape,
        input_output_aliases={i: n_start + i for i in range(n)},
        scratch_shapes=[pltpu.SemaphoreType.DMA((len(local),))] if local else [],
        compiler_params=pltpu.CompilerParams(has_side_effects=pltpu.SideEffectType.DATAFLOW_SIDE_EFFECTING),
    )(*args)
    sems = tuple(outs[:n_start]) if start else None
    return sems, list(outs[n_start:n_start + n]), (outs[n_start + n] if token else None)


def _adamw(w, g, m, v):
    m = ADAM_B1 * m + (1.0 - ADAM_B1) * g
    v = ADAM_B2 * v + (1.0 - ADAM_B2) * (g * g)
    m_hat = m / (1.0 - ADAM_B1 ** ADAM_STEP)
    v_hat = v / (1.0 - ADAM_B2 ** ADAM_STEP)
    delta = -ADAM_LR * (m_hat / (jnp.sqrt(v_hat) + ADAM_EPS) + ADAM_WD * w)
    return delta, m, v


def _combine(name, g, r, *, tr):
    _, R, C = g.shape

    def body(g_ref, r_ref, p_ref):
        mine = g_ref[lax.axis_index("c")]
        p_ref[...] = (mine.astype(F32) + r_ref[...].astype(F32)).astype(BF16)

    blk = pl.BlockSpec((None, tr, C), lambda q, i: (q, i, 0))
    return pl.pallas_call(
        body, name=name, grid=(4, R // tr),
        in_specs=[pl.BlockSpec((None, 2, tr, C), lambda q, i: (q, 0, i, 0)), blk], out_specs=blk,
        out_shape=jax.ShapeDtypeStruct((4, R, C), BF16),
        compiler_params=_params(dimension_semantics=("arbitrary", "arbitrary")),
    )(g.reshape(4, 2, R, C), r)


def _reduce_adam(name, parts, w, m, v, *, tr):
    L, R, C = w.shape
    S = parts[0].shape[0]
    tr = min(tr, R)

    def body(*refs):
        p_refs = refs[:L]
        w_ref, m_ref, v_ref, g_ref, d_ref, nm_ref, nv_ref = refs[L:]
        for l in range(L):
            g = p_refs[l][0].astype(F32)
            for s in range(1, S):
                g = g + p_refs[l][s].astype(F32)
            g_ref[l] = g
            d_ref[l], nm_ref[l], nv_ref[l] = _adamw(w_ref[l], g, m_ref[l], v_ref[l])

    blk = pl.BlockSpec((L, tr, C), lambda r: (0, r, 0))
    out = jax.ShapeDtypeStruct((L, R, C), F32)
    return pl.pallas_call(
        body, name=name, grid=(R // tr,),
        in_specs=[pl.BlockSpec((S, tr, C), lambda r: (0, r, 0))] * L + [blk, blk, blk],
        out_specs=[blk] * 4, out_shape=[out] * 4,
        compiler_params=_params(dimension_semantics=("arbitrary",)),
    )(*parts, w, m, v)


def _small_reduce(parts):
    D = parts.shape[2]
    rows = [0, 1, 8, 9, 16, 17, 24, 25, 18, 19, 20, 2]

    def body(p_ref, out_ref):
        s = p_ref[0]
        for d in range(1, N_DEV):
            s = s + p_ref[d]
        out_ref[...] = jnp.zeros_like(out_ref)
        for r, src in enumerate(rows):
            out_ref[r:r + 1, :] = s[src:src + 1, :]

    return pl.pallas_call(body, name="small_reduce", out_shape=jax.ShapeDtypeStruct((16, D), F32))(parts)


def _small_adam(g_gain, g_taps, g_scale, gains, taps, scale):
    def body(gg, gt, gs, wg, mg, vg, wt, mt, vt, ws, ms, vs, *outs):
        for k, (g, w, m, v) in enumerate(((gg, wg, mg, vg), (gt, wt, mt, vt), (gs, ws, ms, vs))):
            outs[3 * k][...], outs[3 * k + 1][...], outs[3 * k + 2][...] = _adamw(w[...], g[...], m[...], v[...])

    shapes = [jax.ShapeDtypeStruct(t[0].shape, F32) for t in (gains, taps, scale) for _ in range(3)]
    return pl.pallas_call(body, name="small_adam", out_shape=shapes)(g_gain, g_taps, g_scale, *gains, *taps, *scale)


def kernel(x, norm_gains, pool_w, pool_scale, conv_in_w, conv_w, conv_out_w, ffn_gate_up_w, ffn_down_w, loss_target, m_norm_gains, m_pool_w, m_pool_scale, m_conv_in_w, m_conv_w, m_conv_out_w, m_ffn_gate_up_w, m_ffn_down_w, v_norm_gains, v_pool_w, v_pool_scale, v_conv_in_w, v_conv_w, v_conv_out_w, v_ffn_gate_up_w, v_ffn_down_w):
    T, D = x.shape[1], x.shape[2]
    tm = min(256, T)
    tk = min(512, T)
    n_layers = ffn_gate_up_w.shape[0]
    fb = ffn_gate_up_w.shape[2]
    fr = ffn_down_w.shape[1]
    dcol = norm_gains.shape[2]
    cb = conv_in_w.shape[2]
    gw = pool_w.shape[3]
    me = 4 * lax.axis_index("x") + 2 * lax.axis_index("y") + lax.axis_index("c")

    small_w = jnp.concatenate([norm_gains.reshape(8, dcol), jnp.pad(conv_w[0], ((0, 5), (0, 0)))], axis=0)
    every = range(1, N_DEV)
    _, _, pw_g, small_g = _exchange(
        "gather_small",
        [pool_w[0].astype(BF16), small_w, lax.empty((4, N_DEV, gw // N_DEV, gw), BF16), lax.empty((N_DEV, 16, dcol), F32)],
        [Copy(m, 0, _whole, 2, _second, m - 1) for m in every] + [Copy(m, 1, _whole, 3, _lead, N_DEV - 2 + m) for m in every],
        [Local(0, _whole, 2, _second), Local(1, _whole, 3, _lead)])
    pw = pw_g.reshape(4, gw, gw)
    small_full = jnp.swapaxes(small_g, 0, 1).reshape(16, D)
    gain = lambda l, s: small_full[4 * l + s][None, :]
    taps = small_full[8:16]

    wgu_t, m_wgu_t, v_wgu_t = (jnp.swapaxes(a, 1, 2) for a in (ffn_gate_up_w, m_ffn_gate_up_w, v_ffn_gate_up_w))
    shards = [w.astype(BF16) for w in (wgu_t[0], ffn_down_w[0], conv_in_w[0], conv_out_w[0], wgu_t[1], ffn_down_w[1])]
    n_big = len(shards)
    level1 = [Copy(mask, n, _whole, n_big + n, _lead, 4 * n + j)
              for n in range(n_big) for j, mask in enumerate((SIBLING,) + OTHER_CHIPS)]
    lands = [lax.dynamic_update_slice(lax.empty((N_DEV,) + s.shape, BF16), s[None], (me,) + (0,) * s.ndim) for s in shards]
    sems1, bufs1, started = _split_call("gather_start", shards + lands, start=level1, after=small_g, token=True)

    def gathered(name, group, after):
        k = len(group)
        arrived = [Copy(cp.mask, group.index(cp.sb), cp.src, k + group.index(cp.sb), cp.dst, cp.sem)
                   for cp in level1 if cp.sb in group]
        onward = [Copy(SIBLING, i, (lambda ref, me, m=m: ref.at[me ^ m]), i, (lambda ref, sender, m=m: ref.at[sender ^ m]), 3 * i + j)
                  for i in range(k) for j, m in enumerate(OTHER_CHIPS)]
        sems2, bufs2, _ = _split_call(
            name + "_forward", [bufs1[n] for n in group] + [bufs1[n_big + n] for n in group],
            wait=arrived, wait_sems=sems1, start=[cp._replace(sb=k + cp.sb, db=k + cp.db) for cp in onward], after=after)
        return _split_call(name + "_done", bufs2[k:], wait=onward, wait_sems=sems2)[1]

    h0 = x[0]
    h1 = _pool_fwd(h0, pw, pool_scale, gain(0, 0), gain(0, 1), tm=tm, after=started)
    wgu0, wd0 = gathered("gather_ffn0", [0, 1], h1)
    wd0 = wd0.reshape(N_DEV * fr, D)
    h2, gu0, ff0 = _ffn_fwd(h1, gain(0, 2), gain(0, 3), wgu0, wd0, 0, None, tm=tm)
    win_g, wout_g = gathered("gather_conv", [2, 3], h2)
    wout = wout_g.reshape(D, D)
    h3, proj, y = _conv_fwd(h2, gain(1, 0), gain(1, 1), win_g, taps, wout, tm=tm)
    wgu1, wd1 = gathered("gather_ffn1", [4, 5], h3)
    wd1 = wd1.reshape(N_DEV * fr, D)
    dh4, gu1, ff1, loss_part = _ffn_fwd(h3, gain(1, 2), gain(1, 3), wgu1, wd1, 1, loss_target[0], tm=tm)
    loss = lax.psum(loss_part[0, 0], ("x", "y", "c"))

    chip = me >> 1

    def scatter_start(name, grads, trs):
        k = len(grads)
        swap = [Copy(SIBLING, n, (lambda ref, i, q=q: ref.at[2 * q + 1 - (i & 1)]), k + n, (lambda ref, i, q=q: ref.at[q]), 4 * n + q)
                for n in range(k) for q in range(4)]
        from_sibling = _exchange(name + "_swap", grads + [lax.empty((4,) + g.shape[1:], BF16) for g in grads], swap)[k:]
        sums = [_combine(f"{name}_add{n}", g, r, tr=tr) for n, (g, r, tr) in enumerate(zip(grads, from_sibling, trs))]
        lands = [lax.dynamic_update_slice(lax.empty(p.shape, BF16), lax.dynamic_index_in_dim(p, chip, 0), (chip, 0, 0)) for p in sums]
        plan = [Copy(m, n, (lambda ref, i, m=m: ref.at[(i ^ m) >> 1]), k + n, (lambda ref, i: ref.at[i >> 1]), 3 * n + j)
                for n in range(k) for j, m in enumerate(OTHER_CHIPS)]
        sems, bufs, tok = _split_call(name + "_start", sums + lands, start=plan, token=True)
        return (plan, sems, bufs), tok

    def scatter_done(name, state, after):
        plan, sems, bufs = state
        return _split_call(name + "_done", bufs, wait=plan, wait_sems=sems, after=after)[1][len(bufs) // 2:]

    seq = lambda i, k: (k, 0)
    dh3, dgu1, dff1, c1, act1, small_f1 = _ffn_bwd(dh4, h3, ff1, gu1, gain(1, 2), gain(1, 3), wgu1, wd1, 1, tm=tm)
    g_wgu1 = _wgrad("wgrad_gate_up_1", dgu1, c1, pl.BlockSpec((None, tk, fb), lambda i, k: (i, k, 0)),
                    pl.BlockSpec((tk, D), seq), (fb, D), N_DEV, tk=tk)
    g_wd1 = _wgrad("wgrad_down_1", act1, dff1, pl.BlockSpec((None, tk, fb), lambda i, k: (i, k, 0)),
                   pl.BlockSpec((tk, D), seq), (fb, D), N_DEV // 2, tk=tk)
    rs_ffn1, tok = scatter_start("scatter_ffn1", [g_wgu1, g_wd1.reshape(N_DEV, fr, D)], [176, 176])
    dh2, dproj, z, a1, dy, small_c = _conv_bwd(dh3, h2, y, proj, gain(1, 0), gain(1, 1), win_g, taps, wout, tm=tm, after=tok)
    g_win = _wgrad("wgrad_conv_in", a1, dproj, pl.BlockSpec((tk, D), seq),
                   pl.BlockSpec((tk, cb), lambda i, k: (k, i)), (D, cb), N_DEV, tk=tk)
    g_wout = _wgrad("wgrad_conv_out", z, dy, pl.BlockSpec((tk, D), seq), pl.BlockSpec((tk, D), seq), (D, D), 1, tk=tk)
    rs_conv, tok = scatter_start("scatter_conv", [g_win, g_wout.reshape(N_DEV, dcol, D)], [256, 128])
    dh1, dgu0, dff0, c0, act0, small_f0 = _ffn_bwd(dh2, h1, ff0, gu0, gain(0, 2), gain(0, 3), wgu0, wd0, 0, tm=tm, after=tok)
    g_wgu0 = _wgrad("wgrad_gate_up_0", dgu0, c0, pl.BlockSpec((None, tk, fb), lambda i, k: (i, k, 0)),
                    pl.BlockSpec((tk, D), seq), (fb, D), N_DEV, tk=tk)
    g_wd0 = _wgrad("wgrad_down_0", act0, dff0, pl.BlockSpec((None, tk, fb), lambda i, k: (i, k, 0)),
                   pl.BlockSpec((tk, D), seq), (fb, D), N_DEV // 2, tk=tk)
    rs_ffn0, tok = scatter_start("scatter_ffn0", [g_wgu0, g_wd0.reshape(N_DEV, fr, D)], [176, 176])
    grad_x, g_pw, small_p = _pool_bwd(dh1, h0, pw, pool_scale, gain(0, 0), gain(0, 1), tm=tm, after=tok)

    small_part = jnp.concatenate([small_p, small_f0, small_c, small_f1], axis=0)
    g_pw = g_pw.reshape(4, N_DEV, gw // N_DEV, gw)
    r_pw, r_small = _exchange(
        "scatter_small", [g_pw, small_part, lax.empty(g_pw.shape, BF16), lax.empty((N_DEV,) + small_part.shape, F32)],
        [Copy(m, 0, (lambda ref, i, m=m: ref.at[:, i ^ m]), 2, _second, m - 1) for m in every]
        + [Copy(m, 1, _whole, 3, _lead, N_DEV - 2 + m) for m in every],
        [Local(0, _second, 2, _second), Local(1, _whole, 3, _lead)])[2:]
    r_wgu1, r_wd1 = scatter_done("scatter_ffn1", rs_ffn1, r_small)
    r_win, r_wout = scatter_done("scatter_conv", rs_conv, r_small)
    r_wgu0, r_wd0 = scatter_done("scatter_ffn0", rs_ffn0, r_small)

    o_pw = _reduce_adam("adam_pool_w", [r_pw[g] for g in range(4)], pool_w[0], m_pool_w[0], v_pool_w[0], tr=32)
    o_win = _reduce_adam("adam_conv_in", [r_win], conv_in_w, m_conv_in_w, v_conv_in_w, tr=256)
    o_wout = _reduce_adam("adam_conv_out", [r_wout], conv_out_w, m_conv_out_w, v_conv_out_w, tr=128)
    o_wgu = [jnp.swapaxes(o, 1, 2) for o in _reduce_adam("adam_gate_up", [r_wgu0, r_wgu1], wgu_t, m_wgu_t, v_wgu_t, tr=176)]
    o_wd = _reduce_adam("adam_down", [r_wd0, r_wd1], ffn_down_w, m_ffn_down_w, v_ffn_down_w, tr=176)
    g_small = _small_reduce(r_small)
    g_cols = lax.dynamic_slice(g_small, (0, me * dcol), (16, dcol))
    o_small = _small_adam(
        g_cols[0:8], g_cols[8:11], g_small[11:12],
        (norm_gains.reshape(8, dcol), m_norm_gains.reshape(8, dcol), v_norm_gains.reshape(8, dcol)),
        (conv_w[0], m_conv_w[0], v_conv_w[0]), (pool_scale, m_pool_scale, v_pool_scale))
    d_gain, nm_gain, nv_gain, d_taps, nm_taps, nv_taps, d_scale, nm_scale, nv_scale = o_small

    gshape = norm_gains.shape
    per = lambda k: (
        (g_cols[0:8].reshape(gshape), d_gain.reshape(gshape), nm_gain.reshape(gshape), nv_gain.reshape(gshape))[k],
        o_pw[k][None], (g_small[11:12], d_scale, nm_scale, nv_scale)[k], o_win[k],
        (g_cols[8:11][None], d_taps[None], nm_taps[None], nv_taps[None])[k], o_wout[k], o_wgu[k], o_wd[k])
    return (loss, grad_x[None], *per(0), *per(1), *per(2), *per(3))
```

```python
import collections
import functools

import jax
import jax.numpy as jnp
from jax import lax
from jax.experimental import pallas as pl
from jax.experimental.pallas import tpu as pltpu

N_DEV = 8
RMS_EPS = 1e-6
POOL_WINDOWS = (2, 4, 8, 16)
POOL_HALO = 16
CONV_HALO = 16
ADAM_LR, ADAM_B1, ADAM_B2, ADAM_EPS, ADAM_WD, ADAM_STEP = 0.001, 0.9, 0.999, 1e-08, 0.01, 10

VMEM_LIMIT = 56 * 2**20
BF16 = jnp.bfloat16
F32 = jnp.float32
MESH = pl.DeviceIdType.MESH


def _params(**kw):
    return pltpu.CompilerParams(vmem_limit_bytes=VMEM_LIMIT, **kw)


def _resident(shape, index_map):
    return pl.BlockSpec(shape, index_map, pipeline_mode=pl.Buffered(1))


def _ordered(body, n_in, after):
    if after is None:
        return functools.partial(body), [], []
    return (lambda *refs: body(*refs[:n_in], *refs[n_in + 1:])), [after], [pl.BlockSpec(memory_space=pl.ANY)]


def _rms_fwd(x, g):
    r = lax.rsqrt(jnp.mean(x * x, axis=-1, keepdims=True) + RMS_EPS)
    return x * r * g


def _rms_bwd(x, g, dy):
    r = lax.rsqrt(jnp.mean(x * x, axis=-1, keepdims=True) + RMS_EPS)
    xhat = x * r
    dg = jnp.sum(dy * xhat, axis=0, keepdims=True)
    t = dy * g
    dx = r * (t - xhat * jnp.mean(t * xhat, axis=-1, keepdims=True))
    return dx, dg


def _dot(a, b):
    return jnp.dot(a, b, preferred_element_type=F32)


def _dot_nt(a, b):
    return lax.dot_general(a, b, (((1,), (1,)), ((), ())), preferred_element_type=F32)


def _dot_tn(a, b):
    return lax.dot_general(a, b, (((0,), (0,)), ((), ())), preferred_element_type=F32)


def _row_inverse_counts(tile, tm):
    pos = (lax.broadcasted_iota(jnp.int32, (tm, 1), 0) + tile * tm + 1).astype(F32)
    return [1.0 / jnp.minimum(pos, float(w)) for w in POOL_WINDOWS]


def _pool_from_ext(ext, a, invs, gw):
    s = ext
    outs = []
    for g, w in enumerate(POOL_WINDOWS):
        s = s[:, (gw if g else 0):]
        s = s + pltpu.roll(s, w // 2, 0)
        outs.append(s[POOL_HALO:, :gw] * invs[g] - a[:, g * gw:(g + 1) * gw])
    return outs


def _pool_fwd(h, pw, scale, g_pre, g_post, *, tm, after=None):
    T, D = h.shape
    gw = D // len(POOL_WINDOWS)
    hb = tm // POOL_HALO

    def body(h_ref, halo_ref, pw_ref, scale_ref, gpre_ref, gpost_ref, out_ref, ext_ref):
        i = pl.program_id(0)
        x = h_ref[...]
        a = _rms_fwd(x, gpre_ref[...])
        ah = _rms_fwd(halo_ref[...], gpre_ref[...])
        ext_ref[0:POOL_HALO, :] = jnp.where(i == 0, 0.0, ah)
        ext_ref[POOL_HALO:, :] = a
        pooled = _pool_from_ext(ext_ref[...], a, _row_inverse_counts(i, tm), gw)
        mixed = jnp.concatenate([_dot(p.astype(BF16), pw_ref[g]) for g, p in enumerate(pooled)], axis=1)
        out_ref[...] = x + _rms_fwd(mixed * scale_ref[...], gpost_ref[...])

    vec = _resident((1, D), lambda i: (0, 0))
    fn, xa, xs = _ordered(body, 6, after)
    return pl.pallas_call(
        fn, name="pool_fwd", grid=(T // tm,),
        in_specs=[pl.BlockSpec((tm, D), lambda i: (i, 0)),
                  pl.BlockSpec((POOL_HALO, D), lambda i: (jnp.maximum(i * hb - 1, 0), 0)),
                  _resident(pw.shape, lambda i: (0, 0, 0)), vec, vec, vec] + xs,
        out_specs=pl.BlockSpec((tm, D), lambda i: (i, 0)),
        out_shape=jax.ShapeDtypeStruct((T, D), F32),
        scratch_shapes=[pltpu.VMEM((tm + POOL_HALO, D), F32)],
        compiler_params=_params(dimension_semantics=("arbitrary",)),
    )(h, h, pw, scale, g_pre, g_post, *xa)


def _pool_bwd(dh, h, pw, scale, g_pre, g_post, *, tm, after=None):
    T, D = h.shape
    gw = D // len(POOL_WINDOWS)
    hb = tm // POOL_HALO
    nt = T // tm
    n_ext = tm + POOL_HALO

    def body(dh_ref, h_ref, halo_ref, pw_ref, scale_ref, gpre_ref, gpost_ref,
             dx_ref, dpw_ref, small_ref, ext_ref, ext2_ref, carry_ref, dpw_acc):
        i = pl.program_id(0)
        tile = nt - 1 - i

        @pl.when(i == 0)
        def _():
            small_ref[...] = jnp.zeros_like(small_ref)
            dpw_acc[...] = jnp.zeros_like(dpw_acc)
            carry_ref[...] = jnp.zeros_like(carry_ref)

        x = h_ref[...]
        dout = dh_ref[...]
        a = _rms_fwd(x, gpre_ref[...])
        ah = _rms_fwd(halo_ref[...], gpre_ref[...])
        ext_ref[0:POOL_HALO, :] = jnp.where(tile == 0, 0.0, ah)
        ext_ref[POOL_HALO:, :] = a
        invs = _row_inverse_counts(tile, tm)
        pooled = [p.astype(BF16) for p in _pool_from_ext(ext_ref[...], a, invs, gw)]
        mixed_pre = jnp.concatenate([_dot(p, pw_ref[g]) for g, p in enumerate(pooled)], axis=1)
        scale_v = scale_ref[...]
        dmixed, dg_post = _rms_bwd(mixed_pre * scale_v, gpost_ref[...], dout)
        small_ref[1:2, :] += dg_post
        small_ref[2:3, :] += jnp.sum(dmixed * mixed_pre, axis=0, keepdims=True)
        dpre = (dmixed * scale_v).astype(BF16)
        dpooled = []
        for g in range(len(POOL_WINDOWS)):
            dp = dpre[:, g * gw:(g + 1) * gw]
            dpw_acc[g] += _dot_tn(pooled[g], dp)
            dpooled.append(_dot_nt(dp, pw_ref[g]))
        q = jnp.concatenate([d * invs[g] for g, d in enumerate(dpooled)], axis=1)
        ext2_ref[0:tm, :] = q
        ext2_ref[tm:, :] = carry_ref[...]
        carry_ref[...] = q[0:POOL_HALO, :]
        s = ext2_ref[...]
        da = []
        for g, w in enumerate(POOL_WINDOWS):
            s = s[:, (gw if g else 0):]
            s = s + pltpu.roll(s, n_ext - w // 2, 0)
            da.append(s[0:tm, :gw] - dpooled[g])
        dx, dg_pre = _rms_bwd(x, gpre_ref[...], jnp.concatenate(da, axis=1))
        small_ref[0:1, :] += dg_pre
        dx_ref[...] = dout + dx

        @pl.when(i == nt - 1)
        def _():
            dpw_ref[...] = dpw_acc[...].astype(BF16)

    vec = _resident((1, D), lambda i: (0, 0))
    rev = lambda i: (nt - 1 - i, 0)
    fn, xa, xs = _ordered(body, 7, after)
    return pl.pallas_call(
        fn, name="pool_bwd", grid=(nt,),
        in_specs=[pl.BlockSpec((tm, D), rev), pl.BlockSpec((tm, D), rev),
                  pl.BlockSpec((POOL_HALO, D), lambda i: (jnp.maximum((nt - 1 - i) * hb - 1, 0), 0)),
                  _resident(pw.shape, lambda i: (0, 0, 0)), vec, vec, vec] + xs,
        out_specs=[pl.BlockSpec((tm, D), rev),
                   pl.BlockSpec(pw.shape, lambda i: (0, 0, 0)),
                   pl.BlockSpec((8, D), lambda i: (0, 0))],
        out_shape=[jax.ShapeDtypeStruct((T, D), F32), jax.ShapeDtypeStruct(pw.shape, BF16),
                   jax.ShapeDtypeStruct((8, D), F32)],
        scratch_shapes=[pltpu.VMEM((n_ext, D), F32), pltpu.VMEM((n_ext, D), F32),
                        pltpu.VMEM((POOL_HALO, D), F32), pltpu.VMEM(pw.shape, F32)],
        compiler_params=_params(dimension_semantics=("arbitrary",)),
    )(dh, h, h, pw, scale, g_pre, g_post, *xa)


def _ffn_fwd(h, g_pre, g_post, wgu, wd, layer, target, *, tm, after=None):
    T, D = h.shape
    nblk, fb = wgu.shape[0], wgu.shape[1]
    half = nblk // 2
    last = target is not None

    def body(*refs):
        if last:
            h_ref, gpre_ref, gpost_ref, wgu_ref, wd_ref, tgt_ref, out_ref, gu_ref, ff_ref, loss_ref = refs
        else:
            h_ref, gpre_ref, gpost_ref, wgu_ref, wd_ref, out_ref, gu_ref, ff_ref = refs
        x = h_ref[...]
        cb = _rms_fwd(x, gpre_ref[...]).astype(BF16)
        acc = jnp.zeros((tm, D), F32)
        for j in range(half):
            g = _dot_nt(cb, wgu_ref[j])
            u = _dot_nt(cb, wgu_ref[j + half])
            gu_ref[j] = g.astype(BF16)
            gu_ref[j + half] = u.astype(BF16)
            act = (g * jax.nn.sigmoid(g) * u).astype(BF16)
            acc = acc + _dot(act, wd_ref[j * fb:(j + 1) * fb, :])
        ff_ref[...] = acc.astype(BF16)
        hout = x + _rms_fwd(acc, gpost_ref[...])
        if last:
            diff = hout - tgt_ref[...]
            out_ref[...] = diff * (1.0 / D)

            @pl.when(pl.program_id(0) == 0)
            def _():
                loss_ref[...] = jnp.zeros_like(loss_ref)

            loss_ref[...] += jnp.sum(diff * diff) * (0.5 / D)
        else:
            out_ref[...] = hout

    vec = _resident((1, D), lambda i: (0, 0))
    tile = pl.BlockSpec((tm, D), lambda i: (i, 0))
    in_specs = [tile, vec, vec,
                _resident(wgu.shape, lambda i: (0, 0, 0)), _resident(wd.shape, lambda i: (0, 0))]
    out_specs = [tile, pl.BlockSpec((nblk, tm, fb), lambda i: (0, i, 0)), tile]
    out_shape = [jax.ShapeDtypeStruct((T, D), F32), jax.ShapeDtypeStruct((nblk, T, fb), BF16),
                 jax.ShapeDtypeStruct((T, D), BF16)]
    args = [h, g_pre, g_post, wgu, wd]
    if last:
        in_specs.append(tile)
        args.append(target)
        out_specs.append(pl.BlockSpec((8, 128), lambda i: (0, 0)))
        out_shape.append(jax.ShapeDtypeStruct((8, 128), F32))
    fn, xa, xs = _ordered(body, len(args), after)
    return pl.pallas_call(
        fn, name=f"ffn_fwd_{layer}", grid=(T // tm,), in_specs=in_specs + xs, out_specs=out_specs,
        out_shape=out_shape, compiler_params=_params(dimension_semantics=("arbitrary",)),
    )(*args, *xa)


def _ffn_bwd(dh, h, ff, gu, g_pre, g_post, wgu, wd, layer, *, tm, after=None):
    T, D = h.shape
    nblk, fb = wgu.shape[0], wgu.shape[1]
    half = nblk // 2

    def body(dh_ref, h_ref, ff_ref, gu_ref, gpre_ref, gpost_ref, wgu_ref, wd_ref,
             dx_ref, dgu_ref, dff_ref, c_ref, act_ref, small_ref):
        @pl.when(pl.program_id(0) == 0)
        def _():
            small_ref[...] = jnp.zeros_like(small_ref)

        dout = dh_ref[...]
        dff, dg_post = _rms_bwd(ff_ref[...].astype(F32), gpost_ref[...], dout)
        small_ref[1:2, :] += dg_post
        dffb = dff.astype(BF16)
        dff_ref[...] = dffb
        dc = jnp.zeros((tm, D), F32)
        for j in range(half):
            g = gu_ref[j].astype(F32)
            u = gu_ref[j + half].astype(F32)
            s = jax.nn.sigmoid(g)
            silu = g * s
            act_ref[j] = (silu * u).astype(BF16)
            dact = _dot_nt(dffb, wd_ref[j * fb:(j + 1) * fb, :])
            dg = (dact * u * (s * (1.0 + g * (1.0 - s)))).astype(BF16)
            du = (dact * silu).astype(BF16)
            dgu_ref[j] = dg
            dgu_ref[j + half] = du
            dc = dc + _dot(dg, wgu_ref[j]) + _dot(du, wgu_ref[j + half])
        x = h_ref[...]
        c_ref[...] = _rms_fwd(x, gpre_ref[...]).astype(BF16)
        dx, dg_pre = _rms_bwd(x, gpre_ref[...], dc)
        small_ref[0:1, :] += dg_pre
        dx_ref[...] = dout + dx

    vec = _resident((1, D), lambda i: (0, 0))
    tile = pl.BlockSpec((tm, D), lambda i: (i, 0))
    blk = pl.BlockSpec((nblk, tm, fb), lambda i: (0, i, 0))
    fn, xa, xs = _ordered(body, 8, after)
    return pl.pallas_call(
        fn, name=f"ffn_bwd_{layer}", grid=(T // tm,),
        in_specs=[tile, tile, tile, blk, vec, vec,
                  _resident(wgu.shape, lambda i: (0, 0, 0)), _resident(wd.shape, lambda i: (0, 0))] + xs,
        out_specs=[tile, blk, tile, tile, pl.BlockSpec((half, tm, fb), lambda i: (0, i, 0)),
                   pl.BlockSpec((8, D), lambda i: (0, 0))],
        out_shape=[jax.ShapeDtypeStruct((T, D), F32), jax.ShapeDtypeStruct((nblk, T, fb), BF16),
                   jax.ShapeDtypeStruct((T, D), BF16), jax.ShapeDtypeStruct((T, D), BF16),
                   jax.ShapeDtypeStruct((half, T, fb), BF16), jax.ShapeDtypeStruct((8, D), F32)],
        compiler_params=_params(dimension_semantics=("arbitrary",)),
    )(dh, h, ff, gu, g_pre, g_post, wgu, wd, *xa)


def _conv_fwd(h, g_pre, g_post, win, taps, wout, *, tm, after=None):
    T, D = h.shape
    nblk, cb = win.shape[0], win.shape[2]

    def body(h_ref, gpre_ref, gpost_ref, win_ref, taps_ref, wout_ref,
             out_ref, proj_ref, y_ref, proj_scr, ext_ref, carry_ref):
        i = pl.program_id(0)

        @pl.when(i == 0)
        def _():
            carry_ref[...] = jnp.zeros_like(carry_ref)

        x = h_ref[...]
        a = _rms_fwd(x, gpre_ref[...]).astype(BF16)
        for k in range(nblk):
            proj_scr[:, k * cb:(k + 1) * cb] = _dot(a, win_ref[k])
        proj_ref[...] = proj_scr[...].astype(BF16)
        u = proj_scr[:, D:2 * D] * proj_scr[:, 2 * D:3 * D]
        ext_ref[0:CONV_HALO, :] = carry_ref[...]
        ext_ref[CONV_HALO:, :] = u
        carry_ref[...] = u[tm - CONV_HALO:, :]
        e = ext_ref[...]
        conv = (taps_ref[2:3, :] * u + taps_ref[1:2, :] * pltpu.roll(e, 1, 0)[CONV_HALO:, :]
                + taps_ref[0:1, :] * pltpu.roll(e, 2, 0)[CONV_HALO:, :])
        z = (proj_scr[:, 0:D] * conv).astype(BF16)
        y = _dot(z, wout_ref[...])
        y_ref[...] = y.astype(BF16)
        out_ref[...] = x + _rms_fwd(y, gpost_ref[...])

    vec = _resident((1, D), lambda i: (0, 0))
    tile = pl.BlockSpec((tm, D), lambda i: (i, 0))
    fn, xa, xs = _ordered(body, 6, after)
    return pl.pallas_call(
        fn, name="conv_fwd", grid=(T // tm,),
        in_specs=[tile, vec, vec, _resident(win.shape, lambda i: (0, 0, 0)),
                  _resident(taps.shape, lambda i: (0, 0)), _resident(wout.shape, lambda i: (0, 0))] + xs,
        out_specs=[tile, pl.BlockSpec((tm, 3 * D), lambda i: (i, 0)), tile],
        out_shape=[jax.ShapeDtypeStruct((T, D), F32), jax.ShapeDtypeStruct((T, 3 * D), BF16),
                   jax.ShapeDtypeStruct((T, D), BF16)],
        scratch_shapes=[pltpu.VMEM((tm, 3 * D), F32), pltpu.VMEM((tm + CONV_HALO, D), F32),
                        pltpu.VMEM((CONV_HALO, D), F32)],
        compiler_params=_params(dimension_semantics=("arbitrary",)),
    )(h, g_pre, g_post, win, taps, wout, *xa)


def _conv_bwd(dh, h, y, proj, g_pre, g_post, win, taps, wout, *, tm, after=None):
    T, D = h.shape
    nblk, cb = win.shape[0], win.shape[2]
    nt = T // tm
    hb = tm // CONV_HALO
    n_ext = tm + CONV_HALO

    def body(dh_ref, h_ref, y_ref, proj_ref, halo_ref, gpre_ref, gpost_ref, win_ref, taps_ref, wout_ref,
             dx_ref, dproj_ref, z_ref, a_ref, dy_ref, small_ref, ext_ref, ext2_ref, carry_ref):
        i = pl.program_id(0)
        tile = nt - 1 - i

        @pl.when(i == 0)
        def _():
            small_ref[...] = jnp.zeros_like(small_ref)
            carry_ref[...] = jnp.zeros_like(carry_ref)

        dout = dh_ref[...]
        dy, dg_post = _rms_bwd(y_ref[...].astype(F32), gpost_ref[...], dout)
        small_ref[1:2, :] += dg_post
        dyb = dy.astype(BF16)
        dy_ref[...] = dyb
        dz = _dot_nt(dyb, wout_ref[...])
        bgate = proj_ref[:, 0:D].astype(F32)
        cgate = proj_ref[:, D:2 * D].astype(F32)
        v = proj_ref[:, 2 * D:3 * D].astype(F32)
        u = cgate * v
        uh = halo_ref[:, D:2 * D].astype(F32) * halo_ref[:, 2 * D:3 * D].astype(F32)
        ext_ref[0:CONV_HALO, :] = jnp.where(tile == 0, 0.0, uh)
        ext_ref[CONV_HALO:, :] = u
        e = ext_ref[...]
        u1 = pltpu.roll(e, 1, 0)[CONV_HALO:, :]
        u2 = pltpu.roll(e, 2, 0)[CONV_HALO:, :]
        t0, t1, t2 = taps_ref[0:1, :], taps_ref[1:2, :], taps_ref[2:3, :]
        conv = t2 * u + t1 * u1 + t0 * u2
        z_ref[...] = (bgate * conv).astype(BF16)
        dconv = dz * bgate
        small_ref[2:3, :] += jnp.sum(dconv * u2, axis=0, keepdims=True)
        small_ref[3:4, :] += jnp.sum(dconv * u1, axis=0, keepdims=True)
        small_ref[4:5, :] += jnp.sum(dconv * u, axis=0, keepdims=True)
        ext2_ref[0:tm, :] = dconv
        ext2_ref[tm:, :] = carry_ref[...]
        carry_ref[...] = dconv[0:CONV_HALO, :]
        e2 = ext2_ref[...]
        du = (t2 * dconv + t1 * pltpu.roll(e2, n_ext - 1, 0)[0:tm, :]
              + t0 * pltpu.roll(e2, n_ext - 2, 0)[0:tm, :])
        dproj_ref[:, 0:D] = (dz * conv).astype(BF16)
        dproj_ref[:, D:2 * D] = (du * v).astype(BF16)
        dproj_ref[:, 2 * D:3 * D] = (du * cgate).astype(BF16)
        da = jnp.zeros((tm, D), F32)
        for k in range(nblk):
            da = da + _dot_nt(dproj_ref[:, k * cb:(k + 1) * cb], win_ref[k])
        x = h_ref[...]
        a_ref[...] = _rms_fwd(x, gpre_ref[...]).astype(BF16)
        dx, dg_pre = _rms_bwd(x, gpre_ref[...], da)
        small_ref[0:1, :] += dg_pre
        dx_ref[...] = dout + dx

    vec = _resident((1, D), lambda i: (0, 0))
    rev = lambda i: (nt - 1 - i, 0)
    tile = pl.BlockSpec((tm, D), rev)
    wide = pl.BlockSpec((tm, 3 * D), rev)
    fn, xa, xs = _ordered(body, 10, after)
    return pl.pallas_call(
        fn, name="conv_bwd", grid=(nt,),
        in_specs=[tile, tile, tile, wide,
                  pl.BlockSpec((CONV_HALO, 3 * D), lambda i: (jnp.maximum((nt - 1 - i) * hb - 1, 0), 0)),
                  vec, vec, _resident(win.shape, lambda i: (0, 0, 0)),
                  _resident(taps.shape, lambda i: (0, 0)), _resident(wout.shape, lambda i: (0, 0))] + xs,
        out_specs=[tile, wide, tile, tile, tile, pl.BlockSpec((8, D), lambda i: (0, 0))],
        out_shape=[jax.ShapeDtypeStruct((T, D), F32), jax.ShapeDtypeStruct((T, 3 * D), BF16),
                   jax.ShapeDtypeStruct((T, D), BF16), jax.ShapeDtypeStruct((T, D), BF16),
                   jax.ShapeDtypeStruct((T, D), BF16), jax.ShapeDtypeStruct((8, D), F32)],
        scratch_shapes=[pltpu.VMEM((n_ext, D), F32), pltpu.VMEM((n_ext, D), F32),
                        pltpu.VMEM((CONV_HALO, D), F32)],
        compiler_params=_params(dimension_semantics=("arbitrary",)),
    )(dh, h, y, proj, proj, g_pre, g_post, win, taps, wout, *xa)


def _wgrad(name, a, b, a_spec, b_spec, out_block, n_out, *, tk, transpose=False, after=None):
    T = a.shape[-2]
    nk = T // tk
    acc_block = out_block[::-1] if transpose else out_block

    def body(a_ref, b_ref, out_ref, acc_ref):
        k = pl.program_id(1)

        @pl.when(k == 0)
        def _():
            acc_ref[...] = jnp.zeros_like(acc_ref)

        acc_ref[...] += _dot_tn(a_ref[...], b_ref[...])

        @pl.when(k == nk - 1)
        def _():
            acc = acc_ref[...]
            out_ref[...] = (acc.T if transpose else acc).astype(BF16)

    fn, xa, xs = _ordered(body, 2, after)
    return pl.pallas_call(
        fn, name=name, grid=(n_out, nk), in_specs=[a_spec, b_spec] + xs,
        out_specs=pl.BlockSpec((None,) + out_block, lambda i, k: (i, 0, 0)),
        out_shape=jax.ShapeDtypeStruct((n_out,) + out_block, BF16),
        scratch_shapes=[pltpu.VMEM(acc_block, F32)],
        compiler_params=_params(dimension_semantics=("arbitrary", "arbitrary")),
    )(a, b, *xa)


Copy = collections.namedtuple("Copy", "mask sb src db dst sem")
Local = collections.namedtuple("Local", "sb src db dst")

HBM_SPEC = pl.BlockSpec(memory_space=pltpu.HBM)
SEM_SPEC = pl.BlockSpec(memory_space=pltpu.SEMAPHORE)
SIBLING, X_PEER, Y_PEER, DIAGONAL = 1, 4, 2, 6
OTHER_CHIPS = (X_PEER, Y_PEER, DIAGONAL)


def _whole(ref, i):
    return ref


def _lead(ref, i):
    return ref.at[i]


def _second(ref, i):
    return ref.at[:, i]


def _place():
    x, y, c = lax.axis_index("x"), lax.axis_index("y"), lax.axis_index("c")
    return (x, y, c), 4 * x + 2 * y + c


def _descriptor(cp, bufs, xyc, me, sender, send_sems, recv_sems):
    x, y, c = xyc
    flip = lambda v, bit: (1 - v) if bit else v
    return pltpu.make_async_remote_copy(
        src_ref=cp.src(bufs[cp.sb], me), dst_ref=cp.dst(bufs[cp.db], sender),
        send_sem=send_sems.at[cp.sem], recv_sem=recv_sems.at[cp.sem],
        device_id=(flip(x, cp.mask & 4), flip(y, cp.mask & 2), flip(c, cp.mask & 1)), device_id_type=MESH)


def _exchange(name, bufs, plan, local=()):
    n = len(bufs)

    def body(*refs):
        ins = refs[:n]
        send_sems, recv_sems, local_sems = refs[2 * n:]
        xyc, me = _place()
        own = [pltpu.make_async_copy(lc.src(ins[lc.sb], me), lc.dst(ins[lc.db], me), local_sems.at[i])
               for i, lc in enumerate(local)]
        sends = [_descriptor(cp, ins, xyc, me, me, send_sems, recv_sems) for cp in plan]
        for cp in own + sends:
            cp.start()
        for cp in plan:
            _descriptor(cp, ins, xyc, me, me ^ cp.mask, send_sems, recv_sems).wait_recv()
        for cp in sends:
            cp.wait_send()
        for cp in own:
            cp.wait()

    return pl.pallas_call(
        body, name=name, in_specs=[HBM_SPEC] * n, out_specs=[HBM_SPEC] * n,
        out_shape=[jax.ShapeDtypeStruct(b.shape, b.dtype) for b in bufs],
        input_output_aliases={i: i for i in range(n)},
        scratch_shapes=[pltpu.SemaphoreType.DMA((len(plan),)), pltpu.SemaphoreType.DMA((len(plan),)),
                        pltpu.SemaphoreType.DMA((max(len(local), 1),))],
    )(*bufs)


def _split_call(name, bufs, *, wait=None, wait_sems=None, start=None, local=(), after=None, token=False):
    n = len(bufs)
    n_wait = 2 if wait else 0
    n_after = 1 if after is not None else 0
    n_start = 2 if start else 0

    def body(*refs):
        ins = refs[:n]
        wsend, wrecv = refs[n:n + n_wait] if wait else (None, None)
        outs = refs[n + n_wait + n_after:]
        ssend, srecv = outs[:n_start] if start else (None, None)
        rest = outs[n_start + n:]
        xyc, me = _place()
        for cp in wait or ():
            d = _descriptor(cp, ins, xyc, me, me ^ cp.mask, wsend, wrecv)
            d.wait_send()
            d.wait_recv()
        own = [pltpu.make_async_copy(lc.src(ins[lc.sb], me), lc.dst(ins[lc.db], me), rest[-1].at[i])
               for i, lc in enumerate(local)]
        for cp in own:
            cp.start()
        for cp in start or ():
            _descriptor(cp, ins, xyc, me, me, ssend, srecv).start()
        for cp in own:
            cp.wait()
        if token:
            rest[0][...] = jnp.zeros_like(rest[0])

    args = [pltpu.with_memory_space_constraint(b, pltpu.HBM) for b in bufs]
    in_specs = [HBM_SPEC] * n
    if wait:
        args += list(wait_sems)
        in_specs += [SEM_SPEC] * 2
    if after is not None:
        args.append(after)
        in_specs.append(pl.BlockSpec(memory_space=pl.ANY))
    out_shape, out_specs = [], []
    if start:
        out_shape += [pltpu.SemaphoreType.DMA((len(start),))] * 2
        out_specs += [SEM_SPEC] * 2
    out_shape += [pltpu.HBM(b.shape, b.dtype) for b in bufs]
    out_specs += [HBM_SPEC] * n
    if token:
        out_shape.append(jax.ShapeDtypeStruct((8, 128), F32))
        out_specs.append(pl.BlockSpec(memory_space=pltpu.VMEM))
    outs = pl.pallas_call(
        body, name=name, in_specs=in_specs, out_specs=out_specs, out_shape=out_shape,
        input_output_aliases={i: n_start + i for i in range(n)},
        scratch_shapes=[pltpu.SemaphoreType.DMA((len(local),))] if local else [],
        compiler_params=pltpu.CompilerParams(has_side_effects=pltpu.SideEffectType.DATAFLOW_SIDE_EFFECTING),
    )(*args)
    sems = tuple(outs[:n_start]) if start else None
    return sems, list(outs[n_start:n_start + n]), (outs[n_start + n] if token else None)


def _adamw(w, g, m, v):
    m = ADAM_B1 * m + (1.0 - ADAM_B1) * g
    v = ADAM_B2 * v + (1.0 - ADAM_B2) * (g * g)
    m_hat = m / (1.0 - ADAM_B1 ** ADAM_STEP)
    v_hat = v / (1.0 - ADAM_B2 ** ADAM_STEP)
    delta = -ADAM_LR * (m_hat / (jnp.sqrt(v_hat) + ADAM_EPS) + ADAM_WD * w)
    return delta, m, v


def _combine(name, g, r, *, tr):
    _, R, C = g.shape

    def body(g_ref, r_ref, p_ref):
        mine = g_ref[lax.axis_index("c")]
        p_ref[...] = (mine.astype(F32) + r_ref[...].astype(F32)).astype(BF16)

    blk = pl.BlockSpec((None, tr, C), lambda q, i: (q, i, 0))
    return pl.pallas_call(
        body, name=name, grid=(4, R // tr),
        in_specs=[pl.BlockSpec((None, 2, tr, C), lambda q, i: (q, 0, i, 0)), blk], out_specs=blk,
        out_shape=jax.ShapeDtypeStruct((4, R, C), BF16),
        compiler_params=_params(dimension_semantics=("arbitrary", "arbitrary")),
    )(g.reshape(4, 2, R, C), r)


def _reduce_adam(name, parts, w, m, v, *, tr):
    L, R, C = w.shape
    S = parts[0].shape[0]
    tr = min(tr, R)

    def body(*refs):
        p_refs = refs[:L]
        w_ref, m_ref, v_ref, g_ref, d_ref, nm_ref, nv_ref = refs[L:]
        for l in range(L):
            g = p_refs[l][0].astype(F32)
            for s in range(1, S):
                g = g + p_refs[l][s].astype(F32)
            g_ref[l] = g
            d_ref[l], nm_ref[l], nv_ref[l] = _adamw(w_ref[l], g, m_ref[l], v_ref[l])

    blk = pl.BlockSpec((L, tr, C), lambda r: (0, r, 0))
    out = jax.ShapeDtypeStruct((L, R, C), F32)
    return pl.pallas_call(
        body, name=name, grid=(R // tr,),
        in_specs=[pl.BlockSpec((S, tr, C), lambda r: (0, r, 0))] * L + [blk, blk, blk],
        out_specs=[blk] * 4, out_shape=[out] * 4,
        compiler_params=_params(dimension_semantics=("arbitrary",)),
    )(*parts, w, m, v)


def _small_reduce(parts):
    D = parts.shape[2]
    rows = [0, 1, 8, 9, 16, 17, 24, 25, 18, 19, 20, 2]

    def body(p_ref, out_ref):
        s = p_ref[0]
        for d in range(1, N_DEV):
            s = s + p_ref[d]
        out_ref[...] = jnp.zeros_like(out_ref)
        for r, src in enumerate(rows):
            out_ref[r:r + 1, :] = s[src:src + 1, :]

    return pl.pallas_call(body, name="small_reduce", out_shape=jax.ShapeDtypeStruct((16, D), F32))(parts)


def _small_adam(g_gain, g_taps, g_scale, gains, taps, scale):
    def body(gg, gt, gs, wg, mg, vg, wt, mt, vt, ws, ms, vs, *outs):
        for k, (g, w, m, v) in enumerate(((gg, wg, mg, vg), (gt, wt, mt, vt), (gs, ws, ms, vs))):
            outs[3 * k][...], outs[3 * k + 1][...], outs[3 * k + 2][...] = _adamw(w[...], g[...], m[...], v[...])

    shapes = [jax.ShapeDtypeStruct(t[0].shape, F32) for t in (gains, taps, scale) for _ in range(3)]
    return pl.pallas_call(body, name="small_adam", out_shape=shapes)(g_gain, g_taps, g_scale, *gains, *taps, *scale)


def kernel(x, norm_gains, pool_w, pool_scale, conv_in_w, conv_w, conv_out_w, ffn_gate_up_w, ffn_down_w, loss_target, m_norm_gains, m_pool_w, m_pool_scale, m_conv_in_w, m_conv_w, m_conv_out_w, m_ffn_gate_up_w, m_ffn_down_w, v_norm_gains, v_pool_w, v_pool_scale, v_conv_in_w, v_conv_w, v_conv_out_w, v_ffn_gate_up_w, v_ffn_down_w):
    T, D = x.shape[1], x.shape[2]
    tm = min(256, T)
    tk = min(2048, T)
    n_layers = ffn_gate_up_w.shape[0]
    fb = ffn_gate_up_w.shape[2]
    fr = ffn_down_w.shape[1]
    dcol = norm_gains.shape[2]
    cb = conv_in_w.shape[2]
    gw = pool_w.shape[3]
    me = 4 * lax.axis_index("x") + 2 * lax.axis_index("y") + lax.axis_index("c")

    small_w = jnp.concatenate([norm_gains.reshape(8, dcol), jnp.pad(conv_w[0], ((0, 5), (0, 0)))], axis=0)
    every = range(1, N_DEV)
    _, _, pw_g, small_g = _exchange(
        "gather_small",
        [pool_w[0].astype(BF16), small_w, lax.empty((4, N_DEV, gw // N_DEV, gw), BF16), lax.empty((N_DEV, 16, dcol), F32)],
        [Copy(m, 0, _whole, 2, _second, m - 1) for m in every] + [Copy(m, 1, _whole, 3, _lead, N_DEV - 2 + m) for m in every],
        [Local(0, _whole, 2, _second), Local(1, _whole, 3, _lead)])
    pw = pw_g.reshape(4, gw, gw)
    small_full = jnp.swapaxes(small_g, 0, 1).reshape(16, D)
    gain = lambda l, s: small_full[4 * l + s][None, :]
    taps = small_full[8:16]

    wgu_t, m_wgu_t, v_wgu_t = (jnp.swapaxes(a, 1, 2) for a in (ffn_gate_up_w, m_ffn_gate_up_w, v_ffn_gate_up_w))
    shards = [w.astype(BF16) for w in (wgu_t[0], ffn_down_w[0], conv_in_w[0], conv_out_w[0], wgu_t[1], ffn_down_w[1])]
    n_big = len(shards)
    level1 = [Copy(mask, n, _whole, n_big + n, _lead, 4 * n + j)
              for n in range(n_big) for j, mask in enumerate((SIBLING,) + OTHER_CHIPS)]
    lands = [lax.dynamic_update_slice(lax.empty((N_DEV,) + s.shape, BF16), s[None], (me,) + (0,) * s.ndim) for s in shards]
    sems1, bufs1, started = _split_call("gather_start", shards + lands, start=level1, after=small_g, token=True)

    def forward_on(name, group, after):
        k = len(group)
        landed = [Copy(cp.mask, group.index(cp.sb), cp.src, k + group.index(cp.sb), cp.dst, cp.sem)
                  for cp in level1 if cp.sb in group]
        onward = [Copy(SIBLING, i, (lambda ref, me, m=m: ref.at[me ^ m]), i, (lambda ref, sender, m=m: ref.at[sender ^ m]), 3 * i + j)
                  for i in range(k) for j, m in enumerate(OTHER_CHIPS)]
        sems2, bufs2, tok = _split_call(
            name + "_forward", [bufs1[n] for n in group] + [bufs1[n_big + n] for n in group], wait=landed, wait_sems=sems1,
            start=[cp._replace(sb=k + cp.sb, db=k + cp.db) for cp in onward], after=after, token=True)
        return (name, onward, sems2, bufs2[k:]), tok

    def arrived(state, after=None):
        name, onward, sems2, lands2 = state
        return _split_call(name + "_done", lands2, wait=onward, wait_sems=sems2, after=after)[1]

    h0 = x[0]
    h1 = _pool_fwd(h0, pw, pool_scale, gain(0, 0), gain(0, 1), tm=tm, after=started)
    ag_ffn0, _ = forward_on("gather_ffn0", [0, 1], h1)
    wgu0, wd0 = arrived(ag_ffn0)
    wd0 = wd0.reshape(N_DEV * fr, D)
    h2, gu0, ff0 = _ffn_fwd(h1, gain(0, 2), gain(0, 3), wgu0, wd0, 0, None, tm=tm)
    ag_conv, tok = forward_on("gather_conv", [2, 3], h2)
    ag_ffn1, tok = forward_on("gather_ffn1", [4, 5], tok)
    win_g, wout_g = arrived(ag_conv, tok)
    wout = wout_g.reshape(D, D)
    h3, proj, y = _conv_fwd(h2, gain(1, 0), gain(1, 1), win_g, taps, wout, tm=tm)
    wgu1, wd1 = arrived(ag_ffn1, h3)
    wd1 = wd1.reshape(N_DEV * fr, D)
    dh4, gu1, ff1, loss_part = _ffn_fwd(h3, gain(1, 2), gain(1, 3), wgu1, wd1, 1, loss_target[0], tm=tm)
    loss = lax.psum(loss_part[0, 0], ("x", "y", "c"))

    chip = me >> 1

    def scatter_start(name, grad, tr):
        swap = [Copy(SIBLING, 0, (lambda ref, i, q=q: ref.at[2 * q + 1 - (i & 1)]), 1, (lambda ref, i, q=q: ref.at[q]), q)
                for q in range(4)]
        grad, from_sibling = _exchange(name + "_swap", [grad, lax.empty((4,) + grad.shape[1:], BF16)], swap)
        sums = _combine(name + "_add", grad, from_sibling, tr=tr)
        land = lax.dynamic_update_slice(lax.empty(sums.shape, BF16), lax.dynamic_index_in_dim(sums, chip, 0), (chip, 0, 0))
        plan = [Copy(m, 0, (lambda ref, i, m=m: ref.at[(i ^ m) >> 1]), 1, (lambda ref, i: ref.at[i >> 1]), j)
                for j, m in enumerate(OTHER_CHIPS)]
        sems, bufs, tok = _split_call(name + "_start", [sums, land], start=plan, token=True)
        return (name, plan, sems, bufs), tok

    def scatter_done(state, after):
        name, plan, sems, bufs = state
        return _split_call(name + "_done", bufs, wait=plan, wait_sems=sems, after=after)[1][1]

    seq = lambda i, k: (k, 0)
    by_block = pl.BlockSpec((None, tk, fb), lambda i, k: (i, k, 0))
    rows = pl.BlockSpec((tk, D), seq)
    dh3, dgu1, dff1, c1, act1, small_f1 = _ffn_bwd(dh4, h3, ff1, gu1, gain(1, 2), gain(1, 3), wgu1, wd1, 1, tm=tm)
    g_wgu1 = _wgrad("wgrad_gate_up_1", dgu1, c1, by_block, rows, (fb, D), N_DEV, tk=tk)
    rs_wgu1, tok = scatter_start("scatter_gate_up_1", g_wgu1, 176)
    g_wd1 = _wgrad("wgrad_down_1", act1, dff1, by_block, rows, (fb, D), N_DEV // 2, tk=tk, after=tok)
    rs_wd1, tok = scatter_start("scatter_down_1", g_wd1.reshape(N_DEV, fr, D), 176)
    dh2, dproj, z, a1, dy, small_c = _conv_bwd(dh3, h2, y, proj, gain(1, 0), gain(1, 1), win_g, taps, wout, tm=tm, after=tok)
    g_win = _wgrad("wgrad_conv_in", dproj, a1, pl.BlockSpec((tk, cb), lambda i, k: (k, i)), rows, (D, cb), N_DEV,
                   tk=tk, transpose=True)
    rs_win, tok = scatter_start("scatter_conv_in", g_win, 256)
    g_wout = _wgrad("wgrad_conv_out", z, dy, rows, rows, (D, D), 1, tk=tk, after=tok)
    rs_wout, tok = scatter_start("scatter_conv_out", g_wout.reshape(N_DEV, dcol, D), 128)
    dh1, dgu0, dff0, c0, act0, small_f0 = _ffn_bwd(dh2, h1, ff0, gu0, gain(0, 2), gain(0, 3), wgu0, wd0, 0, tm=tm, after=tok)
    g_wgu0 = _wgrad("wgrad_gate_up_0", dgu0, c0, by_block, rows, (fb, D), N_DEV, tk=tk)
    rs_wgu0, tok = scatter_start("scatter_gate_up_0", g_wgu0, 176)
    g_wd0 = _wgrad("wgrad_down_0", act0, dff0, by_block, rows, (fb, D), N_DEV // 2, tk=tk, after=tok)
    rs_wd0, tok = scatter_start("scatter_down_0", g_wd0.reshape(N_DEV, fr, D), 176)
    grad_x, g_pw, small_p = _pool_bwd(dh1, h0, pw, pool_scale, gain(0, 0), gain(0, 1), tm=tm, after=tok)

    small_part = jnp.concatenate([small_p, small_f0, small_c, small_f1], axis=0)
    g_pw = g_pw.reshape(4, N_DEV, gw // N_DEV, gw)
    r_pw, r_small = _exchange(
        "scatter_small", [g_pw, small_part, lax.empty(g_pw.shape, BF16), lax.empty((N_DEV,) + small_part.shape, F32)],
        [Copy(m, 0, (lambda ref, i, m=m: ref.at[:, i ^ m]), 2, _second, m - 1) for m in every]
        + [Copy(m, 1, _whole, 3, _lead, N_DEV - 2 + m) for m in every],
        [Local(0, _second, 2, _second), Local(1, _whole, 3, _lead)])[2:]
    r_wgu1, r_wd1, r_win, r_wout, r_wgu0, r_wd0 = (
        scatter_done(state, r_small) for state in (rs_wgu1, rs_wd1, rs_win, rs_wout, rs_wgu0, rs_wd0))

    o_pw = _reduce_adam("adam_pool_w", [r_pw[g] for g in range(4)], pool_w[0], m_pool_w[0], v_pool_w[0], tr=32)
    o_win = _reduce_adam("adam_conv_in", [r_win], conv_in_w, m_conv_in_w, v_conv_in_w, tr=256)
    o_wout = _reduce_adam("adam_conv_out", [r_wout], conv_out_w, m_conv_out_w, v_conv_out_w, tr=128)
    o_wgu = [jnp.swapaxes(o, 1, 2) for o in _reduce_adam("adam_gate_up", [r_wgu0, r_wgu1], wgu_t, m_wgu_t, v_wgu_t, tr=176)]
    o_wd = _reduce_adam("adam_down", [r_wd0, r_wd1], ffn_down_w, m_ffn_down_w, v_ffn_down_w, tr=176)
    g_small = _small_reduce(r_small)
    g_cols = lax.dynamic_slice(g_small, (0, me * dcol), (16, dcol))
    o_small = _small_adam(
        g_cols[0:8], g_cols[8:11], g_small[11:12],
        (norm_gains.reshape(8, dcol), m_norm_gains.reshape(8, dcol), v_norm_gains.reshape(8, dcol)),
        (conv_w[0], m_conv_w[0], v_conv_w[0]), (pool_scale, m_pool_scale, v_pool_scale))
    d_gain, nm_gain, nv_gain, d_taps, nm_taps, nv_taps, d_scale, nm_scale, nv_scale = o_small

    gshape = norm_gains.shape
    per = lambda k: (
        (g_cols[0:8].reshape(gshape), d_gain.reshape(gshape), nm_gain.reshape(gshape), nv_gain.reshape(gshape))[k],
        o_pw[k][None], (g_small[11:12], d_scale, nm_scale, nv_scale)[k], o_win[k],
        (g_cols[8:11][None], d_taps[None], nm_taps[None], nv_taps[None])[k], o_wout[k], o_wgu[k], o_wd[k])
    return (loss, grad_x[None], *per(0), *per(1), *per(2), *per(3))
```

```python
import collections
import functools

import jax
import jax.numpy as jnp
from jax import lax
from jax.experimental import pallas as pl
from jax.experimental.pallas import tpu as pltpu

N_DEV = 8
RMS_EPS = 1e-6
POOL_WINDOWS = (2, 4, 8, 16)
POOL_HALO = 16
CONV_HALO = 16
ADAM_LR, ADAM_B1, ADAM_B2, ADAM_EPS, ADAM_WD, ADAM_STEP = 0.001, 0.9, 0.999, 1e-08, 0.01, 10

VMEM_LIMIT = 56 * 2**20
BF16 = jnp.bfloat16
F32 = jnp.float32
MESH = pl.DeviceIdType.MESH


def _params(**kw):
    return pltpu.CompilerParams(vmem_limit_bytes=VMEM_LIMIT, **kw)


def _resident(shape, index_map):
    return pl.BlockSpec(shape, index_map, pipeline_mode=pl.Buffered(1))


def _ordered(body, n_in, after):
    if after is None:
        return functools.partial(body), [], []
    return (lambda *refs: body(*refs[:n_in], *refs[n_in + 1:])), [after], [pl.BlockSpec(memory_space=pl.ANY)]


def _rms_fwd(x, g):
    r = lax.rsqrt(jnp.mean(x * x, axis=-1, keepdims=True) + RMS_EPS)
    return x * r * g


def _rms_bwd(x, g, dy):
    r = lax.rsqrt(jnp.mean(x * x, axis=-1, keepdims=True) + RMS_EPS)
    xhat = x * r
    dg = jnp.sum(dy * xhat, axis=0, keepdims=True)
    t = dy * g
    dx = r * (t - xhat * jnp.mean(t * xhat, axis=-1, keepdims=True))
    return dx, dg


def _sigmoid(x):
    return 0.5 * jnp.tanh(0.5 * x) + 0.5


def _dot(a, b):
    return jnp.dot(a, b, preferred_element_type=F32)


def _dot_nt(a, b):
    return lax.dot_general(a, b, (((1,), (1,)), ((), ())), preferred_element_type=F32)


def _dot_tn(a, b):
    return lax.dot_general(a, b, (((0,), (0,)), ((), ())), preferred_element_type=F32)


def _row_inverse_counts(tile, tm):
    pos = (lax.broadcasted_iota(jnp.int32, (tm, 1), 0) + tile * tm + 1).astype(F32)
    return [1.0 / jnp.minimum(pos, float(w)) for w in POOL_WINDOWS]


def _pool_from_ext(ext, a, invs, gw):
    s = ext
    outs = []
    for g, w in enumerate(POOL_WINDOWS):
        s = s[:, (gw if g else 0):]
        s = s + pltpu.roll(s, w // 2, 0)
        outs.append(s[POOL_HALO:, :gw] * invs[g] - a[:, g * gw:(g + 1) * gw])
    return outs


def _pool_fwd(h, pw, scale, g_pre, g_post, *, tm, after=None):
    T, D = h.shape
    gw = D // len(POOL_WINDOWS)
    hb = tm // POOL_HALO

    def body(h_ref, halo_ref, pw_ref, scale_ref, gpre_ref, gpost_ref, out_ref, ext_ref):
        i = pl.program_id(0)
        x = h_ref[...]
        a = _rms_fwd(x, gpre_ref[...])
        ah = _rms_fwd(halo_ref[...], gpre_ref[...])
        ext_ref[0:POOL_HALO, :] = jnp.where(i == 0, 0.0, ah)
        ext_ref[POOL_HALO:, :] = a
        pooled = _pool_from_ext(ext_ref[...], a, _row_inverse_counts(i, tm), gw)
        mixed = jnp.concatenate([_dot(p.astype(BF16), pw_ref[g]) for g, p in enumerate(pooled)], axis=1)
        out_ref[...] = x + _rms_fwd(mixed * scale_ref[...], gpost_ref[...])

    vec = _resident((1, D), lambda i: (0, 0))
    fn, xa, xs = _ordered(body, 6, after)
    return pl.pallas_call(
        fn, name="pool_fwd", grid=(T // tm,),
        in_specs=[pl.BlockSpec((tm, D), lambda i: (i, 0)),
                  pl.BlockSpec((POOL_HALO, D), lambda i: (jnp.maximum(i * hb - 1, 0), 0)),
                  _resident(pw.shape, lambda i: (0, 0, 0)), vec, vec, vec] + xs,
        out_specs=pl.BlockSpec((tm, D), lambda i: (i, 0)),
        out_shape=jax.ShapeDtypeStruct((T, D), F32),
        scratch_shapes=[pltpu.VMEM((tm + POOL_HALO, D), F32)],
        compiler_params=_params(dimension_semantics=("arbitrary",)),
    )(h, h, pw, scale, g_pre, g_post, *xa)


def _pool_bwd(dh, h, pw, scale, g_pre, g_post, *, tm, after=None):
    T, D = h.shape
    gw = D // len(POOL_WINDOWS)
    hb = tm // POOL_HALO
    nt = T // tm
    n_ext = tm + POOL_HALO

    def body(dh_ref, h_ref, halo_ref, pw_ref, scale_ref, gpre_ref, gpost_ref,
             dx_ref, dpw_ref, small_ref, ext_ref, ext2_ref, carry_ref, dpw_acc):
        i = pl.program_id(0)
        tile = nt - 1 - i

        @pl.when(i == 0)
        def _():
            small_ref[...] = jnp.zeros_like(small_ref)
            dpw_acc[...] = jnp.zeros_like(dpw_acc)
            carry_ref[...] = jnp.zeros_like(carry_ref)

        x = h_ref[...]
        dout = dh_ref[...]
        a = _rms_fwd(x, gpre_ref[...])
        ah = _rms_fwd(halo_ref[...], gpre_ref[...])
        ext_ref[0:POOL_HALO, :] = jnp.where(tile == 0, 0.0, ah)
        ext_ref[POOL_HALO:, :] = a
        invs = _row_inverse_counts(tile, tm)
        pooled = [p.astype(BF16) for p in _pool_from_ext(ext_ref[...], a, invs, gw)]
        mixed_pre = jnp.concatenate([_dot(p, pw_ref[g]) for g, p in enumerate(pooled)], axis=1)
        scale_v = scale_ref[...]
        dmixed, dg_post = _rms_bwd(mixed_pre * scale_v, gpost_ref[...], dout)
        small_ref[1:2, :] += dg_post
        small_ref[2:3, :] += jnp.sum(dmixed * mixed_pre, axis=0, keepdims=True)
        dpre = (dmixed * scale_v).astype(BF16)
        dpooled = []
        for g in range(len(POOL_WINDOWS)):
            dp = dpre[:, g * gw:(g + 1) * gw]
            dpw_acc[g] += _dot_tn(pooled[g], dp)
            dpooled.append(_dot_nt(dp, pw_ref[g]))
        q = jnp.concatenate([d * invs[g] for g, d in enumerate(dpooled)], axis=1)
        ext2_ref[0:tm, :] = q
        ext2_ref[tm:, :] = carry_ref[...]
        carry_ref[...] = q[0:POOL_HALO, :]
        s = ext2_ref[...]
        da = []
        for g, w in enumerate(POOL_WINDOWS):
            s = s[:, (gw if g else 0):]
            s = s + pltpu.roll(s, n_ext - w // 2, 0)
            da.append(s[0:tm, :gw] - dpooled[g])
        dx, dg_pre = _rms_bwd(x, gpre_ref[...], jnp.concatenate(da, axis=1))
        small_ref[0:1, :] += dg_pre
        dx_ref[...] = dout + dx

        @pl.when(i == nt - 1)
        def _():
            dpw_ref[...] = dpw_acc[...].astype(BF16)

    vec = _resident((1, D), lambda i: (0, 0))
    rev = lambda i: (nt - 1 - i, 0)
    fn, xa, xs = _ordered(body, 7, after)
    return pl.pallas_call(
        fn, name="pool_bwd", grid=(nt,),
        in_specs=[pl.BlockSpec((tm, D), rev), pl.BlockSpec((tm, D), rev),
                  pl.BlockSpec((POOL_HALO, D), lambda i: (jnp.maximum((nt - 1 - i) * hb - 1, 0), 0)),
                  _resident(pw.shape, lambda i: (0, 0, 0)), vec, vec, vec] + xs,
        out_specs=[pl.BlockSpec((tm, D), rev),
                   pl.BlockSpec(pw.shape, lambda i: (0, 0, 0)),
                   pl.BlockSpec((8, D), lambda i: (0, 0))],
        out_shape=[jax.ShapeDtypeStruct((T, D), F32), jax.ShapeDtypeStruct(pw.shape, BF16),
                   jax.ShapeDtypeStruct((8, D), F32)],
        scratch_shapes=[pltpu.VMEM((n_ext, D), F32), pltpu.VMEM((n_ext, D), F32),
                        pltpu.VMEM((POOL_HALO, D), F32), pltpu.VMEM(pw.shape, F32)],
        compiler_params=_params(dimension_semantics=("arbitrary",)),
    )(dh, h, h, pw, scale, g_pre, g_post, *xa)


def _ffn_fwd(h, g_pre, g_post, wgu, wd, layer, target, *, tm, after=None):
    T, D = h.shape
    nblk, fb = wgu.shape[0], wgu.shape[1]
    half = nblk // 2
    last = target is not None

    def body(*refs):
        if last:
            h_ref, gpre_ref, gpost_ref, wgu_ref, wd_ref, tgt_ref, out_ref, gu_ref, ff_ref, loss_ref = refs
        else:
            h_ref, gpre_ref, gpost_ref, wgu_ref, wd_ref, out_ref, gu_ref, ff_ref = refs
        x = h_ref[...]
        cb = _rms_fwd(x, gpre_ref[...]).astype(BF16)
        acc = jnp.zeros((tm, D), F32)
        for j in range(half):
            g = _dot_nt(cb, wgu_ref[j])
            u = _dot_nt(cb, wgu_ref[j + half])
            gu_ref[j] = g.astype(BF16)
            gu_ref[j + half] = u.astype(BF16)
            act = (g * _sigmoid(g) * u).astype(BF16)
            acc = acc + _dot(act, wd_ref[j * fb:(j + 1) * fb, :])
        ff_ref[...] = acc.astype(BF16)
        hout = x + _rms_fwd(acc, gpost_ref[...])
        if last:
            diff = hout - tgt_ref[...]
            out_ref[...] = diff * (1.0 / D)

            @pl.when(pl.program_id(0) == 0)
            def _():
                loss_ref[...] = jnp.zeros_like(loss_ref)

            loss_ref[...] += jnp.sum(diff * diff) * (0.5 / D)
        else:
            out_ref[...] = hout

    vec = _resident((1, D), lambda i: (0, 0))
    tile = pl.BlockSpec((tm, D), lambda i: (i, 0))
    in_specs = [tile, vec, vec,
                _resident(wgu.shape, lambda i: (0, 0, 0)), _resident(wd.shape, lambda i: (0, 0))]
    out_specs = [tile, pl.BlockSpec((nblk, tm, fb), lambda i: (0, i, 0)), tile]
    out_shape = [jax.ShapeDtypeStruct((T, D), F32), jax.ShapeDtypeStruct((nblk, T, fb), BF16),
                 jax.ShapeDtypeStruct((T, D), BF16)]
    args = [h, g_pre, g_post, wgu, wd]
    if last:
        in_specs.append(tile)
        args.append(target)
        out_specs.append(pl.BlockSpec((8, 128), lambda i: (0, 0)))
        out_shape.append(jax.ShapeDtypeStruct((8, 128), F32))
    fn, xa, xs = _ordered(body, len(args), after)
    return pl.pallas_call(
        fn, name=f"ffn_fwd_{layer}", grid=(T // tm,), in_specs=in_specs + xs, out_specs=out_specs,
        out_shape=out_shape, compiler_params=_params(dimension_semantics=("arbitrary",)),
    )(*args, *xa)


def _ffn_bwd(dh, h, ff, gu, g_pre, g_post, wgu, wd, layer, *, tm, after=None):
    T, D = h.shape
    nblk, fb = wgu.shape[0], wgu.shape[1]
    half = nblk // 2

    def body(dh_ref, h_ref, ff_ref, gu_ref, gpre_ref, gpost_ref, wgu_ref, wd_ref,
             dx_ref, dgu_ref, dff_ref, c_ref, act_ref, small_ref):
        @pl.when(pl.program_id(0) == 0)
        def _():
            small_ref[...] = jnp.zeros_like(small_ref)

        dout = dh_ref[...]
        dff, dg_post = _rms_bwd(ff_ref[...].astype(F32), gpost_ref[...], dout)
        small_ref[1:2, :] += dg_post
        dffb = dff.astype(BF16)
        dff_ref[...] = dffb
        dc = jnp.zeros((tm, D), F32)
        for j in range(half):
            g = gu_ref[j].astype(F32)
            u = gu_ref[j + half].astype(F32)
            s = _sigmoid(g)
            silu = g * s
            act_ref[j] = (silu * u).astype(BF16)
            dact = _dot_nt(dffb, wd_ref[j * fb:(j + 1) * fb, :])
            dg = (dact * u * (s * (1.0 + g * (1.0 - s)))).astype(BF16)
            du = (dact * silu).astype(BF16)
            dgu_ref[j] = dg
            dgu_ref[j + half] = du
            dc = dc + _dot(dg, wgu_ref[j]) + _dot(du, wgu_ref[j + half])
        x = h_ref[...]
        c_ref[...] = _rms_fwd(x, gpre_ref[...]).astype(BF16)
        dx, dg_pre = _rms_bwd(x, gpre_ref[...], dc)
        small_ref[0:1, :] += dg_pre
        dx_ref[...] = dout + dx

    vec = _resident((1, D), lambda i: (0, 0))
    tile = pl.BlockSpec((tm, D), lambda i: (i, 0))
    blk = pl.BlockSpec((nblk, tm, fb), lambda i: (0, i, 0))
    fn, xa, xs = _ordered(body, 8, after)
    return pl.pallas_call(
        fn, name=f"ffn_bwd_{layer}", grid=(T // tm,),
        in_specs=[tile, tile, tile, blk, vec, vec,
                  _resident(wgu.shape, lambda i: (0, 0, 0)), _resident(wd.shape, lambda i: (0, 0))] + xs,
        out_specs=[tile, blk, tile, tile, pl.BlockSpec((half, tm, fb), lambda i: (0, i, 0)),
                   pl.BlockSpec((8, D), lambda i: (0, 0))],
        out_shape=[jax.ShapeDtypeStruct((T, D), F32), jax.ShapeDtypeStruct((nblk, T, fb), BF16),
                   jax.ShapeDtypeStruct((T, D), BF16), jax.ShapeDtypeStruct((T, D), BF16),
                   jax.ShapeDtypeStruct((half, T, fb), BF16), jax.ShapeDtypeStruct((8, D), F32)],
        compiler_params=_params(dimension_semantics=("arbitrary",)),
    )(dh, h, ff, gu, g_pre, g_post, wgu, wd, *xa)


def _conv_fwd(h, g_pre, g_post, win, taps, wout, *, tm, after=None):
    T, D = h.shape
    nblk, cb = win.shape[0], win.shape[2]

    def body(h_ref, gpre_ref, gpost_ref, win_ref, taps_ref, wout_ref,
             out_ref, proj_ref, y_ref, proj_scr, ext_ref, carry_ref):
        i = pl.program_id(0)

        @pl.when(i == 0)
        def _():
            carry_ref[...] = jnp.zeros_like(carry_ref)

        x = h_ref[...]
        a = _rms_fwd(x, gpre_ref[...]).astype(BF16)
        for k in range(nblk):
            proj_scr[:, k * cb:(k + 1) * cb] = _dot(a, win_ref[k])
        proj_ref[...] = proj_scr[...].astype(BF16)
        u = proj_scr[:, D:2 * D] * proj_scr[:, 2 * D:3 * D]
        ext_ref[0:CONV_HALO, :] = carry_ref[...]
        ext_ref[CONV_HALO:, :] = u
        carry_ref[...] = u[tm - CONV_HALO:, :]
        e = ext_ref[...]
        conv = (taps_ref[2:3, :] * u + taps_ref[1:2, :] * pltpu.roll(e, 1, 0)[CONV_HALO:, :]
                + taps_ref[0:1, :] * pltpu.roll(e, 2, 0)[CONV_HALO:, :])
        z = (proj_scr[:, 0:D] * conv).astype(BF16)
        y = _dot(z, wout_ref[...])
        y_ref[...] = y.astype(BF16)
        out_ref[...] = x + _rms_fwd(y, gpost_ref[...])

    vec = _resident((1, D), lambda i: (0, 0))
    tile = pl.BlockSpec((tm, D), lambda i: (i, 0))
    fn, xa, xs = _ordered(body, 6, after)
    return pl.pallas_call(
        fn, name="conv_fwd", grid=(T // tm,),
        in_specs=[tile, vec, vec, _resident(win.shape, lambda i: (0, 0, 0)),
                  _resident(taps.shape, lambda i: (0, 0)), _resident(wout.shape, lambda i: (0, 0))] + xs,
        out_specs=[tile, pl.BlockSpec((tm, 3 * D), lambda i: (i, 0)), tile],
        out_shape=[jax.ShapeDtypeStruct((T, D), F32), jax.ShapeDtypeStruct((T, 3 * D), BF16),
                   jax.ShapeDtypeStruct((T, D), BF16)],
        scratch_shapes=[pltpu.VMEM((tm, 3 * D), F32), pltpu.VMEM((tm + CONV_HALO, D), F32),
                        pltpu.VMEM((CONV_HALO, D), F32)],
        compiler_params=_params(dimension_semantics=("arbitrary",)),
    )(h, g_pre, g_post, win, taps, wout, *xa)


def _conv_bwd(dh, h, y, proj, g_pre, g_post, win, taps, wout, *, tm, after=None):
    T, D = h.shape
    nblk, cb = win.shape[0], win.shape[2]
    nt = T // tm
    hb = tm // CONV_HALO
    n_ext = tm + CONV_HALO

    def body(dh_ref, h_ref, y_ref, proj_ref, halo_ref, gpre_ref, gpost_ref, win_ref, taps_ref, wout_ref,
             dx_ref, dproj_ref, z_ref, a_ref, dy_ref, small_ref, ext_ref, ext2_ref, carry_ref):
        i = pl.program_id(0)
        tile = nt - 1 - i

        @pl.when(i == 0)
        def _():
            small_ref[...] = jnp.zeros_like(small_ref)
            carry_ref[...] = jnp.zeros_like(carry_ref)

        dout = dh_ref[...]
        dy, dg_post = _rms_bwd(y_ref[...].astype(F32), gpost_ref[...], dout)
        small_ref[1:2, :] += dg_post
        dyb = dy.astype(BF16)
        dy_ref[...] = dyb
        dz = _dot_nt(dyb, wout_ref[...])
        bgate = proj_ref[:, 0:D].astype(F32)
        cgate = proj_ref[:, D:2 * D].astype(F32)
        v = proj_ref[:, 2 * D:3 * D].astype(F32)
        u = cgate * v
        uh = halo_ref[:, D:2 * D].astype(F32) * halo_ref[:, 2 * D:3 * D].astype(F32)
        ext_ref[0:CONV_HALO, :] = jnp.where(tile == 0, 0.0, uh)
        ext_ref[CONV_HALO:, :] = u
        e = ext_ref[...]
        u1 = pltpu.roll(e, 1, 0)[CONV_HALO:, :]
        u2 = pltpu.roll(e, 2, 0)[CONV_HALO:, :]
        t0, t1, t2 = taps_ref[0:1, :], taps_ref[1:2, :], taps_ref[2:3, :]
        conv = t2 * u + t1 * u1 + t0 * u2
        z_ref[...] = (bgate * conv).astype(BF16)
        dconv = dz * bgate
        small_ref[2:3, :] += jnp.sum(dconv * u2, axis=0, keepdims=True)
        small_ref[3:4, :] += jnp.sum(dconv * u1, axis=0, keepdims=True)
        small_ref[4:5, :] += jnp.sum(dconv * u, axis=0, keepdims=True)
        ext2_ref[0:tm, :] = dconv
        ext2_ref[tm:, :] = carry_ref[...]
        carry_ref[...] = dconv[0:CONV_HALO, :]
        e2 = ext2_ref[...]
        du = (t2 * dconv + t1 * pltpu.roll(e2, n_ext - 1, 0)[0:tm, :]
              + t0 * pltpu.roll(e2, n_ext - 2, 0)[0:tm, :])
        dproj_ref[:, 0:D] = (dz * conv).astype(BF16)
        dproj_ref[:, D:2 * D] = (du * v).astype(BF16)
        dproj_ref[:, 2 * D:3 * D] = (du * cgate).astype(BF16)
        da = jnp.zeros((tm, D), F32)
        for k in range(nblk):
            da = da + _dot_nt(dproj_ref[:, k * cb:(k + 1) * cb], win_ref[k])
        x = h_ref[...]
        a_ref[...] = _rms_fwd(x, gpre_ref[...]).astype(BF16)
        dx, dg_pre = _rms_bwd(x, gpre_ref[...], da)
        small_ref[0:1, :] += dg_pre
        dx_ref[...] = dout + dx

    vec = _resident((1, D), lambda i: (0, 0))
    rev = lambda i: (nt - 1 - i, 0)
    tile = pl.BlockSpec((tm, D), rev)
    wide = pl.BlockSpec((tm, 3 * D), rev)
    fn, xa, xs = _ordered(body, 10, after)
    return pl.pallas_call(
        fn, name="conv_bwd", grid=(nt,),
        in_specs=[tile, tile, tile, wide,
                  pl.BlockSpec((CONV_HALO, 3 * D), lambda i: (jnp.maximum((nt - 1 - i) * hb - 1, 0), 0)),
                  vec, vec, _resident(win.shape, lambda i: (0, 0, 0)),
                  _resident(taps.shape, lambda i: (0, 0)), _resident(wout.shape, lambda i: (0, 0))] + xs,
        out_specs=[tile, wide, tile, tile, tile, pl.BlockSpec((8, D), lambda i: (0, 0))],
        out_shape=[jax.ShapeDtypeStruct((T, D), F32), jax.ShapeDtypeStruct((T, 3 * D), BF16),
                   jax.ShapeDtypeStruct((T, D), BF16), jax.ShapeDtypeStruct((T, D), BF16),
                   jax.ShapeDtypeStruct((T, D), BF16), jax.ShapeDtypeStruct((8, D), F32)],
        scratch_shapes=[pltpu.VMEM((n_ext, D), F32), pltpu.VMEM((n_ext, D), F32),
                        pltpu.VMEM((CONV_HALO, D), F32)],
        compiler_params=_params(dimension_semantics=("arbitrary",)),
    )(dh, h, y, proj, proj, g_pre, g_post, win, taps, wout, *xa)


def _wgrad(name, a, b, a_spec, b_spec, block, n_blocks, *, tk, transpose=False, after=None):
    T = a.shape[-2]
    nk = T // tk
    M, N = block
    m = N_DEV // n_blocks
    R = M // m
    acc_block = block[::-1] if transpose else block

    def body(a_ref, b_ref, out_ref, acc_ref, stage_ref, recv_ref, send_sems, recv_sems):
        i, k = pl.program_id(0), pl.program_id(1)
        x, y, c = lax.axis_index("x"), lax.axis_index("y"), lax.axis_index("c")

        def sent(blk, p):
            owner = blk * m + p
            q = owner // 2
            return (owner % 2) != c, pltpu.make_async_remote_copy(
                src_ref=stage_ref.at[p * R:(p + 1) * R], dst_ref=recv_ref.at[q], send_sem=send_sems.at[q],
                recv_sem=recv_sems.at[q], device_id=(x, y, 1 - c), device_id_type=MESH)

        @pl.when(k == 0)
        def _():
            acc_ref[...] = jnp.zeros_like(acc_ref)

        acc_ref[...] += _dot_tn(a_ref[...], b_ref[...])

        @pl.when(k == nk - 1)
        def _():
            for p in range(m):
                away, copy = sent(jnp.maximum(i - 1, 0), p)

                @pl.when(jnp.logical_and(i > 0, away))
                def _():
                    copy.wait_send()

            acc = acc_ref[...]
            stage_ref[...] = (acc.T if transpose else acc).astype(BF16)
            for p in range(m):
                away, copy = sent(i, p)

                @pl.when(away)
                def _():
                    copy.start()

                @pl.when(jnp.logical_not(away))
                def _():
                    out_ref[(i * m + p) // 2] = stage_ref[p * R:(p + 1) * R, :]

        @pl.when(jnp.logical_and(i == n_blocks - 1, k == nk - 1))
        def _():
            for p in range(m):
                away, copy = sent(i, p)

                @pl.when(away)
                def _():
                    copy.wait_send()

            for q in range(N_DEV // 2):
                pltpu.make_async_remote_copy(
                    src_ref=stage_ref.at[0:R], dst_ref=recv_ref.at[q], send_sem=send_sems.at[q],
                    recv_sem=recv_sems.at[q], device_id=(x, y, 1 - c), device_id_type=MESH).wait_recv()
                out_ref[q] = (out_ref[q].astype(F32) + recv_ref[q].astype(F32)).astype(BF16)

    fn, xa, xs = _ordered(body, 2, after)
    return pl.pallas_call(
        fn, name=name, grid=(n_blocks, nk), in_specs=[a_spec, b_spec] + xs,
        out_specs=pl.BlockSpec((N_DEV // 2, R, N), lambda i, k: (0, 0, 0)),
        out_shape=jax.ShapeDtypeStruct((N_DEV // 2, R, N), BF16),
        scratch_shapes=[pltpu.VMEM(acc_block, F32), pltpu.VMEM(block, BF16), pltpu.VMEM((N_DEV // 2, R, N), BF16),
                        pltpu.SemaphoreType.DMA((N_DEV // 2,)), pltpu.SemaphoreType.DMA((N_DEV // 2,))],
        compiler_params=_params(dimension_semantics=("arbitrary", "arbitrary")),
    )(a, b, *xa)


Copy = collections.namedtuple("Copy", "mask sb src db dst sem")
Local = collections.namedtuple("Local", "sb src db dst")

HBM_SPEC = pl.BlockSpec(memory_space=pltpu.HBM)
SEM_SPEC = pl.BlockSpec(memory_space=pltpu.SEMAPHORE)
SIBLING, X_PEER, Y_PEER, DIAGONAL = 1, 4, 2, 6
OTHER_CHIPS = (X_PEER, Y_PEER, DIAGONAL)


def _whole(ref, i):
    return ref


def _lead(ref, i):
    return ref.at[i]


def _second(ref, i):
    return ref.at[:, i]


def _place():
    x, y, c = lax.axis_index("x"), lax.axis_index("y"), lax.axis_index("c")
    return (x, y, c), 4 * x + 2 * y + c


def _descriptor(cp, bufs, xyc, me, sender, send_sems, recv_sems):
    x, y, c = xyc
    flip = lambda v, bit: (1 - v) if bit else v
    return pltpu.make_async_remote_copy(
        src_ref=cp.src(bufs[cp.sb], me), dst_ref=cp.dst(bufs[cp.db], sender),
        send_sem=send_sems.at[cp.sem], recv_sem=recv_sems.at[cp.sem],
        device_id=(flip(x, cp.mask & 4), flip(y, cp.mask & 2), flip(c, cp.mask & 1)), device_id_type=MESH)


def _exchange(name, bufs, plan, local=()):
    n = len(bufs)

    def body(*refs):
        ins = refs[:n]
        send_sems, recv_sems, local_sems = refs[2 * n:]
        xyc, me = _place()
        own = [pltpu.make_async_copy(lc.src(ins[lc.sb], me), lc.dst(ins[lc.db], me), local_sems.at[i])
               for i, lc in enumerate(local)]
        sends = [_descriptor(cp, ins, xyc, me, me, send_sems, recv_sems) for cp in plan]
        for cp in own + sends:
            cp.start()
        for cp in plan:
            _descriptor(cp, ins, xyc, me, me ^ cp.mask, send_sems, recv_sems).wait_recv()
        for cp in sends:
            cp.wait_send()
        for cp in own:
            cp.wait()

    return pl.pallas_call(
        body, name=name, in_specs=[HBM_SPEC] * n, out_specs=[HBM_SPEC] * n,
        out_shape=[jax.ShapeDtypeStruct(b.shape, b.dtype) for b in bufs],
        input_output_aliases={i: i for i in range(n)},
        scratch_shapes=[pltpu.SemaphoreType.DMA((len(plan),)), pltpu.SemaphoreType.DMA((len(plan),)),
                        pltpu.SemaphoreType.DMA((max(len(local), 1),))],
    )(*bufs)


def _split_call(name, bufs, *, wait=None, wait_sems=None, start=None, local=(), after=None, token=False):
    n = len(bufs)
    n_wait = 2 if wait else 0
    n_after = 1 if after is not None else 0
    n_start = 2 if start else 0

    def body(*refs):
        ins = refs[:n]
        wsend, wrecv = refs[n:n + n_wait] if wait else (None, None)
        outs = refs[n + n_wait + n_after:]
        ssend, srecv = outs[:n_start] if start else (None, None)
        rest = outs[n_start + n:]
        xyc, me = _place()
        for cp in wait or ():
            d = _descriptor(cp, ins, xyc, me, me ^ cp.mask, wsend, wrecv)
            d.wait_send()
            d.wait_recv()
        own = [pltpu.make_async_copy(lc.src(ins[lc.sb], me), lc.dst(ins[lc.db], me), rest[-1].at[i])
               for i, lc in enumerate(local)]
        for cp in own:
            cp.start()
        for cp in start or ():
            _descriptor(cp, ins, xyc, me, me, ssend, srecv).start()
        for cp in own:
            cp.wait()
        if token:
            rest[0][...] = jnp.zeros_like(rest[0])

    args = [pltpu.with_memory_space_constraint(b, pltpu.HBM) for b in bufs]
    in_specs = [HBM_SPEC] * n
    if wait:
        args += list(wait_sems)
        in_specs += [SEM_SPEC] * 2
    if after is not None:
        args.append(after)
        in_specs.append(pl.BlockSpec(memory_space=pl.ANY))
    out_shape, out_specs = [], []
    if start:
        out_shape += [pltpu.SemaphoreType.DMA((len(start),))] * 2
        out_specs += [SEM_SPEC] * 2
    out_shape += [pltpu.HBM(b.shape, b.dtype) for b in bufs]
    out_specs += [HBM_SPEC] * n
    if token:
        out_shape.append(jax.ShapeDtypeStruct((8, 128), F32))
        out_specs.append(pl.BlockSpec(memory_space=pltpu.VMEM))
    outs = pl.pallas_call(
        body, name=name, in_specs=in_specs, out_specs=out_specs, out_shape=out_shape,
        input_output_aliases={i: n_start + i for i in range(n)},
        scratch_shapes=[pltpu.SemaphoreType.DMA((len(local),))] if local else [],
        compiler_params=pltpu.CompilerParams(has_side_effects=pltpu.SideEffectType.DATAFLOW_SIDE_EFFECTING),
    )(*args)
    sems = tuple(outs[:n_start]) if start else None
    return sems, list(outs[n_start:n_start + n]), (outs[n_start + n] if token else None)


def _adamw(w, g, m, v):
    m = ADAM_B1 * m + (1.0 - ADAM_B1) * g
    v = ADAM_B2 * v + (1.0 - ADAM_B2) * (g * g)
    m_hat = m / (1.0 - ADAM_B1 ** ADAM_STEP)
    v_hat = v / (1.0 - ADAM_B2 ** ADAM_STEP)
    delta = -ADAM_LR * (m_hat / (jnp.sqrt(v_hat) + ADAM_EPS) + ADAM_WD * w)
    return delta, m, v


def _reduce_adam(name, parts, w, m, v, *, tr):
    L, R, C = w.shape
    S = parts[0].shape[0]
    tr = min(tr, R)

    def body(*refs):
        p_refs = refs[:L]
        w_ref, m_ref, v_ref, g_ref, d_ref, nm_ref, nv_ref = refs[L:]
        for l in range(L):
            g = p_refs[l][0].astype(F32)
            for s in range(1, S):
                g = g + p_refs[l][s].astype(F32)
            g_ref[l] = g
            d_ref[l], nm_ref[l], nv_ref[l] = _adamw(w_ref[l], g, m_ref[l], v_ref[l])

    blk = pl.BlockSpec((L, tr, C), lambda r: (0, r, 0))
    out = jax.ShapeDtypeStruct((L, R, C), F32)
    return pl.pallas_call(
        body, name=name, grid=(R // tr,),
        in_specs=[pl.BlockSpec((S, tr, C), lambda r: (0, r, 0))] * L + [blk, blk, blk],
        out_specs=[blk] * 4, out_shape=[out] * 4,
        compiler_params=_params(dimension_semantics=("arbitrary",)),
    )(*parts, w, m, v)


def _small_reduce(parts):
    D = parts.shape[2]
    rows = [0, 1, 8, 9, 16, 17, 24, 25, 18, 19, 20, 2]

    def body(p_ref, out_ref):
        s = p_ref[0]
        for d in range(1, N_DEV):
            s = s + p_ref[d]
        out_ref[...] = jnp.zeros_like(out_ref)
        for r, src in enumerate(rows):
            out_ref[r:r + 1, :] = s[src:src + 1, :]

    return pl.pallas_call(body, name="small_reduce", out_shape=jax.ShapeDtypeStruct((16, D), F32))(parts)


def _small_adam(g_gain, g_taps, g_scale, gains, taps, scale):
    def body(gg, gt, gs, wg, mg, vg, wt, mt, vt, ws, ms, vs, *outs):
        for k, (g, w, m, v) in enumerate(((gg, wg, mg, vg), (gt, wt, mt, vt), (gs, ws, ms, vs))):
            outs[3 * k][...], outs[3 * k + 1][...], outs[3 * k + 2][...] = _adamw(w[...], g[...], m[...], v[...])

    shapes = [jax.ShapeDtypeStruct(t[0].shape, F32) for t in (gains, taps, scale) for _ in range(3)]
    return pl.pallas_call(body, name="small_adam", out_shape=shapes)(g_gain, g_taps, g_scale, *gains, *taps, *scale)


def kernel(x, norm_gains, pool_w, pool_scale, conv_in_w, conv_w, conv_out_w, ffn_gate_up_w, ffn_down_w, loss_target, m_norm_gains, m_pool_w, m_pool_scale, m_conv_in_w, m_conv_w, m_conv_out_w, m_ffn_gate_up_w, m_ffn_down_w, v_norm_gains, v_pool_w, v_pool_scale, v_conv_in_w, v_conv_w, v_conv_out_w, v_ffn_gate_up_w, v_ffn_down_w):
    T, D = x.shape[1], x.shape[2]
    tm = min(256, T)
    tk = min(2048, T)
    n_layers = ffn_gate_up_w.shape[0]
    fb = ffn_gate_up_w.shape[2]
    fr = ffn_down_w.shape[1]
    dcol = norm_gains.shape[2]
    cb = conv_in_w.shape[2]
    gw = pool_w.shape[3]
    me = 4 * lax.axis_index("x") + 2 * lax.axis_index("y") + lax.axis_index("c")

    small_w = jnp.concatenate([norm_gains.reshape(8, dcol), jnp.pad(conv_w[0], ((0, 5), (0, 0)))], axis=0)
    every = range(1, N_DEV)
    _, _, pw_g, small_g = _exchange(
        "gather_small",
        [pool_w[0].astype(BF16), small_w, lax.empty((4, N_DEV, gw // N_DEV, gw), BF16), lax.empty((N_DEV, 16, dcol), F32)],
        [Copy(m, 0, _whole, 2, _second, m - 1) for m in every] + [Copy(m, 1, _whole, 3, _lead, N_DEV - 2 + m) for m in every],
        [Local(0, _whole, 2, _second), Local(1, _whole, 3, _lead)])
    pw = pw_g.reshape(4, gw, gw)
    small_full = jnp.swapaxes(small_g, 0, 1).reshape(16, D)
    gain = lambda l, s: small_full[4 * l + s][None, :]
    taps = small_full[8:16]

    wgu_t, m_wgu_t, v_wgu_t = (jnp.swapaxes(a, 1, 2) for a in (ffn_gate_up_w, m_ffn_gate_up_w, v_ffn_gate_up_w))
    shards = [w.astype(BF16) for w in (wgu_t[0], ffn_down_w[0], conv_in_w[0], conv_out_w[0], wgu_t[1], ffn_down_w[1])]
    n_big = len(shards)
    level1 = [Copy(mask, n, _whole, n_big + n, _lead, 4 * n + j)
              for n in range(n_big) for j, mask in enumerate((SIBLING,) + OTHER_CHIPS)]
    lands = [lax.dynamic_update_slice(lax.empty((N_DEV,) + s.shape, BF16), s[None], (me,) + (0,) * s.ndim) for s in shards]
    sems1, bufs1, started = _split_call("gather_start", shards + lands, start=level1, after=small_g, token=True)

    def forward_on(name, group, after):
        k = len(group)
        landed = [Copy(cp.mask, group.index(cp.sb), cp.src, k + group.index(cp.sb), cp.dst, cp.sem)
                  for cp in level1 if cp.sb in group]
        onward = [Copy(SIBLING, i, (lambda ref, me, m=m: ref.at[me ^ m]), i, (lambda ref, sender, m=m: ref.at[sender ^ m]), 3 * i + j)
                  for i in range(k) for j, m in enumerate(OTHER_CHIPS)]
        sems2, bufs2, tok = _split_call(
            name + "_forward", [bufs1[n] for n in group] + [bufs1[n_big + n] for n in group], wait=landed, wait_sems=sems1,
            start=[cp._replace(sb=k + cp.sb, db=k + cp.db) for cp in onward], after=after, token=True)
        return (name, onward, sems2, bufs2[k:]), tok

    def arrived(state, after=None):
        name, onward, sems2, lands2 = state
        return _split_call(name + "_done", lands2, wait=onward, wait_sems=sems2, after=after)[1]

    h0 = x[0]
    h1 = _pool_fwd(h0, pw, pool_scale, gain(0, 0), gain(0, 1), tm=tm, after=started)
    ag_ffn0, _ = forward_on("gather_ffn0", [0, 1], h1)
    wgu0, wd0 = arrived(ag_ffn0)
    wd0 = wd0.reshape(N_DEV * fr, D)
    h2, gu0, ff0 = _ffn_fwd(h1, gain(0, 2), gain(0, 3), wgu0, wd0, 0, None, tm=tm)
    ag_conv, tok = forward_on("gather_conv", [2, 3], h2)
    ag_ffn1, tok = forward_on("gather_ffn1", [4, 5], tok)
    win_g, wout_g = arrived(ag_conv, tok)
    wout = wout_g.reshape(D, D)
    h3, proj, y = _conv_fwd(h2, gain(1, 0), gain(1, 1), win_g, taps, wout, tm=tm)
    wgu1, wd1 = arrived(ag_ffn1, h3)
    wd1 = wd1.reshape(N_DEV * fr, D)
    dh4, gu1, ff1, loss_part = _ffn_fwd(h3, gain(1, 2), gain(1, 3), wgu1, wd1, 1, loss_target[0], tm=tm)
    loss = lax.psum(loss_part[0, 0], ("x", "y", "c"))

    chip = me >> 1

    def scatter_start(name, sums):
        land = lax.dynamic_update_slice(lax.empty(sums.shape, BF16), lax.dynamic_index_in_dim(sums, chip, 0), (chip, 0, 0))
        plan = [Copy(m, 0, (lambda ref, i, m=m: ref.at[(i ^ m) >> 1]), 1, (lambda ref, i: ref.at[i >> 1]), j)
                for j, m in enumerate(OTHER_CHIPS)]
        sems, bufs, tok = _split_call(name + "_start", [sums, land], start=plan, token=True)
        return (name, plan, sems, bufs), tok

    def scatter_done(state, after):
        name, plan, sems, bufs = state
        return _split_call(name + "_done", bufs, wait=plan, wait_sems=sems, after=after)[1][1]

    seq = lambda i, k: (k, 0)
    by_block = pl.BlockSpec((None, tk, fb), lambda i, k: (i, k, 0))
    rows = pl.BlockSpec((tk, D), seq)
    dh3, dgu1, dff1, c1, act1, small_f1 = _ffn_bwd(dh4, h3, ff1, gu1, gain(1, 2), gain(1, 3), wgu1, wd1, 1, tm=tm)
    g_wgu1 = _wgrad("wgrad_gate_up_1", dgu1, c1, by_block, rows, (fb, D), N_DEV, tk=tk)
    rs_wgu1, tok = scatter_start("scatter_gate_up_1", g_wgu1)
    g_wd1 = _wgrad("wgrad_down_1", act1, dff1, by_block, rows, (fb, D), N_DEV // 2, tk=tk, after=tok)
    rs_wd1, tok = scatter_start("scatter_down_1", g_wd1)
    dh2, dproj, z, a1, dy, small_c = _conv_bwd(dh3, h2, y, proj, gain(1, 0), gain(1, 1), win_g, taps, wout, tm=tm, after=tok)
    g_win = _wgrad("wgrad_conv_in", dproj, a1, pl.BlockSpec((tk, cb), lambda i, k: (k, i)), rows, (D, cb), N_DEV,
                   tk=tk, transpose=True)
    rs_win, tok = scatter_start("scatter_conv_in", g_win)
    g_wout = _wgrad("wgrad_conv_out", z, dy, rows, rows, (D, D), 1, tk=tk, after=tok)
    rs_wout, tok = scatter_start("scatter_conv_out", g_wout)
    dh1, dgu0, dff0, c0, act0, small_f0 = _ffn_bwd(dh2, h1, ff0, gu0, gain(0, 2), gain(0, 3), wgu0, wd0, 0, tm=tm, after=tok)
    g_wgu0 = _wgrad("wgrad_gate_up_0", dgu0, c0, by_block, rows, (fb, D), N_DEV, tk=tk)
    rs_wgu0, tok = scatter_start("scatter_gate_up_0", g_wgu0)
    g_wd0 = _wgrad("wgrad_down_0", act0, dff0, by_block, rows, (fb, D), N_DEV // 2, tk=tk, after=tok)
    rs_wd0, tok = scatter_start("scatter_down_0", g_wd0)
    grad_x, g_pw, small_p = _pool_bwd(dh1, h0, pw, pool_scale, gain(0, 0), gain(0, 1), tm=tm, after=tok)

    small_part = jnp.concatenate([small_p, small_f0, small_c, small_f1], axis=0)
    g_pw = g_pw.reshape(4, N_DEV, gw // N_DEV, gw)
    r_pw, r_small = _exchange(
        "scatter_small", [g_pw, small_part, lax.empty(g_pw.shape, BF16), lax.empty((N_DEV,) + small_part.shape, F32)],
        [Copy(m, 0, (lambda ref, i, m=m: ref.at[:, i ^ m]), 2, _second, m - 1) for m in every]
        + [Copy(m, 1, _whole, 3, _lead, N_DEV - 2 + m) for m in every],
        [Local(0, _second, 2, _second), Local(1, _whole, 3, _lead)])[2:]
    r_wgu1, r_wd1, r_win, r_wout, r_wgu0, r_wd0 = (
        scatter_done(state, r_small) for state in (rs_wgu1, rs_wd1, rs_win, rs_wout, rs_wgu0, rs_wd0))

    o_pw = _reduce_adam("adam_pool_w", [r_pw[g] for g in range(4)], pool_w[0], m_pool_w[0], v_pool_w[0], tr=32)
    o_win = _reduce_adam("adam_conv_in", [r_win], conv_in_w, m_conv_in_w, v_conv_in_w, tr=256)
    o_wout = _reduce_adam("adam_conv_out", [r_wout], conv_out_w, m_conv_out_w, v_conv_out_w, tr=128)
    o_wgu = [jnp.swapaxes(o, 1, 2) for o in _reduce_adam("adam_gate_up", [r_wgu0, r_wgu1], wgu_t, m_wgu_t, v_wgu_t, tr=176)]
    o_wd = _reduce_adam("adam_down", [r_wd0, r_wd1], ffn_down_w, m_ffn_down_w, v_ffn_down_w, tr=176)
    g_small = _small_reduce(r_small)
    g_cols = lax.dynamic_slice(g_small, (0, me * dcol), (16, dcol))
    o_small = _small_adam(
        g_cols[0:8], g_cols[8:11], g_small[11:12],
        (norm_gains.reshape(8, dcol), m_norm_gains.reshape(8, dcol), v_norm_gains.reshape(8, dcol)),
        (conv_w[0], m_conv_w[0], v_conv_w[0]), (pool_scale, m_pool_scale, v_pool_scale))
    d_gain, nm_gain, nv_gain, d_taps, nm_taps, nv_taps, d_scale, nm_scale, nv_scale = o_small

    gshape = norm_gains.shape
    per = lambda k: (
        (g_cols[0:8].reshape(gshape), d_gain.reshape(gshape), nm_gain.reshape(gshape), nv_gain.reshape(gshape))[k],
        o_pw[k][None], (g_small[11:12], d_scale, nm_scale, nv_scale)[k], o_win[k],
        (g_cols[8:11][None], d_taps[None], nm_taps[None], nv_taps[None])[k], o_wout[k], o_wgu[k], o_wd[k])
    return (loss, grad_x[None], *per(0), *per(1), *per(2), *per(3))
```

```python
import collections
import functools

import jax
import jax.numpy as jnp
from jax import lax
from jax.experimental import pallas as pl
from jax.experimental.pallas import tpu as pltpu

N_DEV = 8
RMS_EPS = 1e-6
POOL_WINDOWS = (2, 4, 8, 16)
POOL_HALO = 16
CONV_HALO = 16
ADAM_LR, ADAM_B1, ADAM_B2, ADAM_EPS, ADAM_WD, ADAM_STEP = 0.001, 0.9, 0.999, 1e-08, 0.01, 10

VMEM_LIMIT = 56 * 2**20
BF16 = jnp.bfloat16
F32 = jnp.float32
MESH = pl.DeviceIdType.MESH


def _params(**kw):
    return pltpu.CompilerParams(vmem_limit_bytes=VMEM_LIMIT, **kw)


def _resident(shape, index_map):
    return pl.BlockSpec(shape, index_map, pipeline_mode=pl.Buffered(1))


def _ordered(body, n_in, after):
    if after is None:
        return functools.partial(body), [], []
    return (lambda *refs: body(*refs[:n_in], *refs[n_in + 1:])), [after], [pl.BlockSpec(memory_space=pl.ANY)]


def _rms_fwd(x, g):
    r = lax.rsqrt(jnp.mean(x * x, axis=-1, keepdims=True) + RMS_EPS)
    return x * r * g


def _rms_bwd(x, g, dy):
    r = lax.rsqrt(jnp.mean(x * x, axis=-1, keepdims=True) + RMS_EPS)
    xhat = x * r
    dg = jnp.sum(dy * xhat, axis=0, keepdims=True)
    t = dy * g
    dx = r * (t - xhat * jnp.mean(t * xhat, axis=-1, keepdims=True))
    return dx, dg


def _sigmoid(x):
    return 0.5 * jnp.tanh(0.5 * x) + 0.5


def _dot(a, b):
    return jnp.dot(a, b, preferred_element_type=F32)


def _dot_nt(a, b):
    return lax.dot_general(a, b, (((1,), (1,)), ((), ())), preferred_element_type=F32)


def _dot_tn(a, b):
    return lax.dot_general(a, b, (((0,), (0,)), ((), ())), preferred_element_type=F32)


def _row_inverse_counts(tile, tm):
    pos = (lax.broadcasted_iota(jnp.int32, (tm, 1), 0) + tile * tm + 1).astype(F32)
    return [1.0 / jnp.minimum(pos, float(w)) for w in POOL_WINDOWS]


def _pool_from_ext(ext, a, invs, gw):
    s = ext
    outs = []
    for g, w in enumerate(POOL_WINDOWS):
        s = s[:, (gw if g else 0):]
        s = s + pltpu.roll(s, w // 2, 0)
        outs.append(s[POOL_HALO:, :gw] * invs[g] - a[:, g * gw:(g + 1) * gw])
    return outs


def _pool_fwd(h, pw, scale, g_pre, g_post, *, tm, after=None):
    T, D = h.shape
    gw = D // len(POOL_WINDOWS)
    hb = tm // POOL_HALO

    def body(h_ref, halo_ref, pw_ref, scale_ref, gpre_ref, gpost_ref, out_ref, ext_ref):
        i = pl.program_id(0)
        x = h_ref[...]
        a = _rms_fwd(x, gpre_ref[...])
        ah = _rms_fwd(halo_ref[...], gpre_ref[...])
        ext_ref[0:POOL_HALO, :] = jnp.where(i == 0, 0.0, ah)
        ext_ref[POOL_HALO:, :] = a
        pooled = _pool_from_ext(ext_ref[...], a, _row_inverse_counts(i, tm), gw)
        mixed = jnp.concatenate([_dot(p.astype(BF16), pw_ref[g]) for g, p in enumerate(pooled)], axis=1)
        out_ref[...] = x + _rms_fwd(mixed * scale_ref[...], gpost_ref[...])

    vec = _resident((1, D), lambda i: (0, 0))
    fn, xa, xs = _ordered(body, 6, after)
    return pl.pallas_call(
        fn, name="pool_fwd", grid=(T // tm,),
        in_specs=[pl.BlockSpec((tm, D), lambda i: (i, 0)),
                  pl.BlockSpec((POOL_HALO, D), lambda i: (jnp.maximum(i * hb - 1, 0), 0)),
                  _resident(pw.shape, lambda i: (0, 0, 0)), vec, vec, vec] + xs,
        out_specs=pl.BlockSpec((tm, D), lambda i: (i, 0)),
        out_shape=jax.ShapeDtypeStruct((T, D), F32),
        scratch_shapes=[pltpu.VMEM((tm + POOL_HALO, D), F32)],
        compiler_params=_params(dimension_semantics=("arbitrary",)),
    )(h, h, pw, scale, g_pre, g_post, *xa)


def _pool_bwd(dh, h, pw, scale, g_pre, g_post, *, tm, after=None):
    T, D = h.shape
    gw = D // len(POOL_WINDOWS)
    hb = tm // POOL_HALO
    nt = T // tm
    n_ext = tm + POOL_HALO

    def body(dh_ref, h_ref, halo_ref, pw_ref, scale_ref, gpre_ref, gpost_ref,
             dx_ref, dpw_ref, small_ref, ext_ref, ext2_ref, carry_ref, dpw_acc):
        i = pl.program_id(0)
        tile = nt - 1 - i

        @pl.when(i == 0)
        def _():
            small_ref[...] = jnp.zeros_like(small_ref)
            dpw_acc[...] = jnp.zeros_like(dpw_acc)
            carry_ref[...] = jnp.zeros_like(carry_ref)

        x = h_ref[...]
        dout = dh_ref[...]
        a = _rms_fwd(x, gpre_ref[...])
        ah = _rms_fwd(halo_ref[...], gpre_ref[...])
        ext_ref[0:POOL_HALO, :] = jnp.where(tile == 0, 0.0, ah)
        ext_ref[POOL_HALO:, :] = a
        invs = _row_inverse_counts(tile, tm)
        pooled = [p.astype(BF16) for p in _pool_from_ext(ext_ref[...], a, invs, gw)]
        mixed_pre = jnp.concatenate([_dot(p, pw_ref[g]) for g, p in enumerate(pooled)], axis=1)
        scale_v = scale_ref[...]
        dmixed, dg_post = _rms_bwd(mixed_pre * scale_v, gpost_ref[...], dout)
        small_ref[1:2, :] += dg_post
        small_ref[2:3, :] += jnp.sum(dmixed * mixed_pre, axis=0, keepdims=True)
        dpre = (dmixed * scale_v).astype(BF16)
        dpooled = []
        for g in range(len(POOL_WINDOWS)):
            dp = dpre[:, g * gw:(g + 1) * gw]
            dpw_acc[g] += _dot_tn(pooled[g], dp)
            dpooled.append(_dot_nt(dp, pw_ref[g]))
        q = jnp.concatenate([d * invs[g] for g, d in enumerate(dpooled)], axis=1)
        ext2_ref[0:tm, :] = q
        ext2_ref[tm:, :] = carry_ref[...]
        carry_ref[...] = q[0:POOL_HALO, :]
        s = ext2_ref[...]
        da = []
        for g, w in enumerate(POOL_WINDOWS):
            s = s[:, (gw if g else 0):]
            s = s + pltpu.roll(s, n_ext - w // 2, 0)
            da.append(s[0:tm, :gw] - dpooled[g])
        dx, dg_pre = _rms_bwd(x, gpre_ref[...], jnp.concatenate(da, axis=1))
        small_ref[0:1, :] += dg_pre
        dx_ref[...] = dout + dx

        @pl.when(i == nt - 1)
        def _():
            dpw_ref[...] = dpw_acc[...].astype(BF16)

    vec = _resident((1, D), lambda i: (0, 0))
    rev = lambda i: (nt - 1 - i, 0)
    fn, xa, xs = _ordered(body, 7, after)
    return pl.pallas_call(
        fn, name="pool_bwd", grid=(nt,),
        in_specs=[pl.BlockSpec((tm, D), rev), pl.BlockSpec((tm, D), rev),
                  pl.BlockSpec((POOL_HALO, D), lambda i: (jnp.maximum((nt - 1 - i) * hb - 1, 0), 0)),
                  _resident(pw.shape, lambda i: (0, 0, 0)), vec, vec, vec] + xs,
        out_specs=[pl.BlockSpec((tm, D), rev),
                   pl.BlockSpec(pw.shape, lambda i: (0, 0, 0)),
                   pl.BlockSpec((8, D), lambda i: (0, 0))],
        out_shape=[jax.ShapeDtypeStruct((T, D), F32), jax.ShapeDtypeStruct(pw.shape, BF16),
                   jax.ShapeDtypeStruct((8, D), F32)],
        scratch_shapes=[pltpu.VMEM((n_ext, D), F32), pltpu.VMEM((n_ext, D), F32),
                        pltpu.VMEM((POOL_HALO, D), F32), pltpu.VMEM(pw.shape, F32)],
        compiler_params=_params(dimension_semantics=("arbitrary",)),
    )(dh, h, h, pw, scale, g_pre, g_post, *xa)


def _ffn_fwd(h, g_pre, g_post, wgu, wd, layer, target, *, tm, after=None):
    T, D = h.shape
    nblk, fb = wgu.shape[0], wgu.shape[1]
    half = nblk // 2
    last = target is not None

    def body(*refs):
        if last:
            h_ref, gpre_ref, gpost_ref, wgu_ref, wd_ref, tgt_ref, out_ref, gu_ref, ff_ref, loss_ref = refs
        else:
            h_ref, gpre_ref, gpost_ref, wgu_ref, wd_ref, out_ref, gu_ref, ff_ref = refs
        x = h_ref[...]
        cb = _rms_fwd(x, gpre_ref[...]).astype(BF16)
        acc = jnp.zeros((tm, D), F32)
        for j in range(half):
            g = _dot_nt(cb, wgu_ref[j])
            u = _dot_nt(cb, wgu_ref[j + half])
            gu_ref[j] = g.astype(BF16)
            gu_ref[j + half] = u.astype(BF16)
            act = (g * _sigmoid(g) * u).astype(BF16)
            acc = acc + _dot(act, wd_ref[j * fb:(j + 1) * fb, :])
        ff_ref[...] = acc.astype(BF16)
        hout = x + _rms_fwd(acc, gpost_ref[...])
        if last:
            diff = hout - tgt_ref[...]
            out_ref[...] = diff * (1.0 / D)

            @pl.when(pl.program_id(0) == 0)
            def _():
                loss_ref[...] = jnp.zeros_like(loss_ref)

            loss_ref[...] += jnp.sum(diff * diff) * (0.5 / D)
        else:
            out_ref[...] = hout

    vec = _resident((1, D), lambda i: (0, 0))
    tile = pl.BlockSpec((tm, D), lambda i: (i, 0))
    in_specs = [tile, vec, vec,
                _resident(wgu.shape, lambda i: (0, 0, 0)), _resident(wd.shape, lambda i: (0, 0))]
    out_specs = [tile, pl.BlockSpec((nblk, tm, fb), lambda i: (0, i, 0)), tile]
    out_shape = [jax.ShapeDtypeStruct((T, D), F32), jax.ShapeDtypeStruct((nblk, T, fb), BF16),
                 jax.ShapeDtypeStruct((T, D), BF16)]
    args = [h, g_pre, g_post, wgu, wd]
    if last:
        in_specs.append(tile)
        args.append(target)
        out_specs.append(pl.BlockSpec((8, 128), lambda i: (0, 0)))
        out_shape.append(jax.ShapeDtypeStruct((8, 128), F32))
    fn, xa, xs = _ordered(body, len(args), after)
    return pl.pallas_call(
        fn, name=f"ffn_fwd_{layer}", grid=(T // tm,), in_specs=in_specs + xs, out_specs=out_specs,
        out_shape=out_shape, compiler_params=_params(dimension_semantics=("arbitrary",)),
    )(*args, *xa)


def _ffn_bwd(dh, h, ff, gu, g_pre, g_post, wgu, wd, layer, *, tm, after=None):
    T, D = h.shape
    nblk, fb = wgu.shape[0], wgu.shape[1]
    half = nblk // 2

    def body(dh_ref, h_ref, ff_ref, gu_ref, gpre_ref, gpost_ref, wgu_ref, wd_ref,
             dx_ref, dgu_ref, dff_ref, c_ref, act_ref, small_ref):
        @pl.when(pl.program_id(0) == 0)
        def _():
            small_ref[...] = jnp.zeros_like(small_ref)

        dout = dh_ref[...]
        dff, dg_post = _rms_bwd(ff_ref[...].astype(F32), gpost_ref[...], dout)
        small_ref[1:2, :] += dg_post
        dffb = dff.astype(BF16)
        dff_ref[...] = dffb
        dc = jnp.zeros((tm, D), F32)
        for j in range(half):
            g = gu_ref[j].astype(F32)
            u = gu_ref[j + half].astype(F32)
            s = _sigmoid(g)
            silu = g * s
            act_ref[j] = (silu * u).astype(BF16)
            dact = _dot_nt(dffb, wd_ref[j * fb:(j + 1) * fb, :])
            dg = (dact * u * (s * (1.0 + g * (1.0 - s)))).astype(BF16)
            du = (dact * silu).astype(BF16)
            dgu_ref[j] = dg
            dgu_ref[j + half] = du
            dc = dc + _dot(dg, wgu_ref[j]) + _dot(du, wgu_ref[j + half])
        x = h_ref[...]
        c_ref[...] = _rms_fwd(x, gpre_ref[...]).astype(BF16)
        dx, dg_pre = _rms_bwd(x, gpre_ref[...], dc)
        small_ref[0:1, :] += dg_pre
        dx_ref[...] = dout + dx

    vec = _resident((1, D), lambda i: (0, 0))
    tile = pl.BlockSpec((tm, D), lambda i: (i, 0))
    blk = pl.BlockSpec((nblk, tm, fb), lambda i: (0, i, 0))
    fn, xa, xs = _ordered(body, 8, after)
    return pl.pallas_call(
        fn, name=f"ffn_bwd_{layer}", grid=(T // tm,),
        in_specs=[tile, tile, tile, blk, vec, vec,
                  _resident(wgu.shape, lambda i: (0, 0, 0)), _resident(wd.shape, lambda i: (0, 0))] + xs,
        out_specs=[tile, blk, tile, tile, pl.BlockSpec((half, tm, fb), lambda i: (0, i, 0)),
                   pl.BlockSpec((8, D), lambda i: (0, 0))],
        out_shape=[jax.ShapeDtypeStruct((T, D), F32), jax.ShapeDtypeStruct((nblk, T, fb), BF16),
                   jax.ShapeDtypeStruct((T, D), BF16), jax.ShapeDtypeStruct((T, D), BF16),
                   jax.ShapeDtypeStruct((half, T, fb), BF16), jax.ShapeDtypeStruct((8, D), F32)],
        compiler_params=_params(dimension_semantics=("arbitrary",)),
    )(dh, h, ff, gu, g_pre, g_post, wgu, wd, *xa)


def _conv_fwd(h, g_pre, g_post, win, taps, wout, *, tm, after=None):
    T, D = h.shape
    nblk, cb = win.shape[0], win.shape[2]

    def body(h_ref, gpre_ref, gpost_ref, win_ref, taps_ref, wout_ref,
             out_ref, proj_ref, y_ref, proj_scr, ext_ref, carry_ref):
        i = pl.program_id(0)

        @pl.when(i == 0)
        def _():
            carry_ref[...] = jnp.zeros_like(carry_ref)

        x = h_ref[...]
        a = _rms_fwd(x, gpre_ref[...]).astype(BF16)
        for k in range(nblk):
            proj_scr[:, k * cb:(k + 1) * cb] = _dot(a, win_ref[k])
        proj_ref[...] = proj_scr[...].astype(BF16)
        u = proj_scr[:, D:2 * D] * proj_scr[:, 2 * D:3 * D]
        ext_ref[0:CONV_HALO, :] = carry_ref[...]
        ext_ref[CONV_HALO:, :] = u
        carry_ref[...] = u[tm - CONV_HALO:, :]
        e = ext_ref[...]
        conv = (taps_ref[2:3, :] * u + taps_ref[1:2, :] * pltpu.roll(e, 1, 0)[CONV_HALO:, :]
                + taps_ref[0:1, :] * pltpu.roll(e, 2, 0)[CONV_HALO:, :])
        z = (proj_scr[:, 0:D] * conv).astype(BF16)
        y = _dot(z, wout_ref[...])
        y_ref[...] = y.astype(BF16)
        out_ref[...] = x + _rms_fwd(y, gpost_ref[...])

    vec = _resident((1, D), lambda i: (0, 0))
    tile = pl.BlockSpec((tm, D), lambda i: (i, 0))
    fn, xa, xs = _ordered(body, 6, after)
    return pl.pallas_call(
        fn, name="conv_fwd", grid=(T // tm,),
        in_specs=[tile, vec, vec, _resident(win.shape, lambda i: (0, 0, 0)),
                  _resident(taps.shape, lambda i: (0, 0)), _resident(wout.shape, lambda i: (0, 0))] + xs,
        out_specs=[tile, pl.BlockSpec((tm, 3 * D), lambda i: (i, 0)), tile],
        out_shape=[jax.ShapeDtypeStruct((T, D), F32), jax.ShapeDtypeStruct((T, 3 * D), BF16),
                   jax.ShapeDtypeStruct((T, D), BF16)],
        scratch_shapes=[pltpu.VMEM((tm, 3 * D), F32), pltpu.VMEM((tm + CONV_HALO, D), F32),
                        pltpu.VMEM((CONV_HALO, D), F32)],
        compiler_params=_params(dimension_semantics=("arbitrary",)),
    )(h, g_pre, g_post, win, taps, wout, *xa)


def _conv_bwd(dh, h, y, proj, g_pre, g_post, win, taps, wout, *, tm, after=None):
    T, D = h.shape
    nblk, cb = win.shape[0], win.shape[2]
    nt = T // tm
    hb = tm // CONV_HALO
    n_ext = tm + CONV_HALO

    def body(dh_ref, h_ref, y_ref, proj_ref, halo_ref, gpre_ref, gpost_ref, win_ref, taps_ref, wout_ref,
             dx_ref, dproj_ref, z_ref, a_ref, dy_ref, small_ref, ext_ref, ext2_ref, carry_ref):
        i = pl.program_id(0)
        tile = nt - 1 - i

        @pl.when(i == 0)
        def _():
            small_ref[...] = jnp.zeros_like(small_ref)
            carry_ref[...] = jnp.zeros_like(carry_ref)

        dout = dh_ref[...]
        dy, dg_post = _rms_bwd(y_ref[...].astype(F32), gpost_ref[...], dout)
        small_ref[1:2, :] += dg_post
        dyb = dy.astype(BF16)
        dy_ref[...] = dyb
        dz = _dot_nt(dyb, wout_ref[...])
        bgate = proj_ref[:, 0:D].astype(F32)
        cgate = proj_ref[:, D:2 * D].astype(F32)
        v = proj_ref[:, 2 * D:3 * D].astype(F32)
        u = cgate * v
        uh = halo_ref[:, D:2 * D].astype(F32) * halo_ref[:, 2 * D:3 * D].astype(F32)
        ext_ref[0:CONV_HALO, :] = jnp.where(tile == 0, 0.0, uh)
        ext_ref[CONV_HALO:, :] = u
        e = ext_ref[...]
        u1 = pltpu.roll(e, 1, 0)[CONV_HALO:, :]
        u2 = pltpu.roll(e, 2, 0)[CONV_HALO:, :]
        t0, t1, t2 = taps_ref[0:1, :], taps_ref[1:2, :], taps_ref[2:3, :]
        conv = t2 * u + t1 * u1 + t0 * u2
        z_ref[...] = (bgate * conv).astype(BF16)
        dconv = dz * bgate
        small_ref[2:3, :] += jnp.sum(dconv * u2, axis=0, keepdims=True)
        small_ref[3:4, :] += jnp.sum(dconv * u1, axis=0, keepdims=True)
        small_ref[4:5, :] += jnp.sum(dconv * u, axis=0, keepdims=True)
        ext2_ref[0:tm, :] = dconv
        ext2_ref[tm:, :] = carry_ref[...]
        carry_ref[...] = dconv[0:CONV_HALO, :]
        e2 = ext2_ref[...]
        du = (t2 * dconv + t1 * pltpu.roll(e2, n_ext - 1, 0)[0:tm, :]
              + t0 * pltpu.roll(e2, n_ext - 2, 0)[0:tm, :])
        dproj_ref[:, 0:D] = (dz * conv).astype(BF16)
        dproj_ref[:, D:2 * D] = (du * v).astype(BF16)
        dproj_ref[:, 2 * D:3 * D] = (du * cgate).astype(BF16)
        da = jnp.zeros((tm, D), F32)
        for k in range(nblk):
            da = da + _dot_nt(dproj_ref[:, k * cb:(k + 1) * cb], win_ref[k])
        x = h_ref[...]
        a_ref[...] = _rms_fwd(x, gpre_ref[...]).astype(BF16)
        dx, dg_pre = _rms_bwd(x, gpre_ref[...], da)
        small_ref[0:1, :] += dg_pre
        dx_ref[...] = dout + dx

    vec = _resident((1, D), lambda i: (0, 0))
    rev = lambda i: (nt - 1 - i, 0)
    tile = pl.BlockSpec((tm, D), rev)
    wide = pl.BlockSpec((tm, 3 * D), rev)
    fn, xa, xs = _ordered(body, 10, after)
    return pl.pallas_call(
        fn, name="conv_bwd", grid=(nt,),
        in_specs=[tile, tile, tile, wide,
                  pl.BlockSpec((CONV_HALO, 3 * D), lambda i: (jnp.maximum((nt - 1 - i) * hb - 1, 0), 0)),
                  vec, vec, _resident(win.shape, lambda i: (0, 0, 0)),
                  _resident(taps.shape, lambda i: (0, 0)), _resident(wout.shape, lambda i: (0, 0))] + xs,
        out_specs=[tile, wide, tile, tile, tile, pl.BlockSpec((8, D), lambda i: (0, 0))],
        out_shape=[jax.ShapeDtypeStruct((T, D), F32), jax.ShapeDtypeStruct((T, 3 * D), BF16),
                   jax.ShapeDtypeStruct((T, D), BF16), jax.ShapeDtypeStruct((T, D), BF16),
                   jax.ShapeDtypeStruct((T, D), BF16), jax.ShapeDtypeStruct((8, D), F32)],
        scratch_shapes=[pltpu.VMEM((n_ext, D), F32), pltpu.VMEM((n_ext, D), F32),
                        pltpu.VMEM((CONV_HALO, D), F32)],
        compiler_params=_params(dimension_semantics=("arbitrary",)),
    )(dh, h, y, proj, proj, g_pre, g_post, win, taps, wout, *xa)


def _wgrad(name, a, b, a_spec, b_spec, block, n_blocks, *, tk, transpose=False, after=None):
    T = a.shape[-2]
    nk = T // tk
    M, N = block
    m = N_DEV // n_blocks
    R = M // m
    acc_block = block[::-1] if transpose else block

    def body(a_ref, b_ref, out_ref, acc_ref, stage_ref, recv_ref, send_sems, recv_sems):
        i, k = pl.program_id(0), pl.program_id(1)
        x, y, c = lax.axis_index("x"), lax.axis_index("y"), lax.axis_index("c")

        def sent(blk, p):
            owner = blk * m + p
            q = owner // 2
            return (owner % 2) != c, pltpu.make_async_remote_copy(
                src_ref=stage_ref.at[p * R:(p + 1) * R], dst_ref=recv_ref.at[q], send_sem=send_sems.at[q],
                recv_sem=recv_sems.at[q], device_id=(x, y, 1 - c), device_id_type=MESH)

        @pl.when(k == 0)
        def _():
            acc_ref[...] = jnp.zeros_like(acc_ref)

        acc_ref[...] += _dot_tn(a_ref[...], b_ref[...])

        @pl.when(k == nk - 1)
        def _():
            for p in range(m):
                away, copy = sent(jnp.maximum(i - 1, 0), p)

                @pl.when(jnp.logical_and(i > 0, away))
                def _():
                    copy.wait_send()

            acc = acc_ref[...]
            stage_ref[...] = (acc.T if transpose else acc).astype(BF16)
            for p in range(m):
                away, copy = sent(i, p)

                @pl.when(away)
                def _():
                    copy.start()

                @pl.when(jnp.logical_not(away))
                def _():
                    out_ref[(i * m + p) // 2] = stage_ref[p * R:(p + 1) * R, :]

        @pl.when(jnp.logical_and(i == n_blocks - 1, k == nk - 1))
        def _():
            for p in range(m):
                away, copy = sent(i, p)

                @pl.when(away)
                def _():
                    copy.wait_send()

            for q in range(N_DEV // 2):
                pltpu.make_async_remote_copy(
                    src_ref=stage_ref.at[0:R], dst_ref=recv_ref.at[q], send_sem=send_sems.at[q],
                    recv_sem=recv_sems.at[q], device_id=(x, y, 1 - c), device_id_type=MESH).wait_recv()
                out_ref[q] = (out_ref[q].astype(F32) + recv_ref[q].astype(F32)).astype(BF16)

    fn, xa, xs = _ordered(body, 2, after)
    return pl.pallas_call(
        fn, name=name, grid=(n_blocks, nk), in_specs=[a_spec, b_spec] + xs,
        out_specs=pl.BlockSpec((N_DEV // 2, R, N), lambda i, k: (0, 0, 0)),
        out_shape=jax.ShapeDtypeStruct((N_DEV // 2, R, N), BF16),
        scratch_shapes=[pltpu.VMEM(acc_block, F32), pltpu.VMEM(block, BF16), pltpu.VMEM((N_DEV // 2, R, N), BF16),
                        pltpu.SemaphoreType.DMA((N_DEV // 2,)), pltpu.SemaphoreType.DMA((N_DEV // 2,))],
        compiler_params=_params(dimension_semantics=("arbitrary", "arbitrary")),
    )(a, b, *xa)


Copy = collections.namedtuple("Copy", "mask sb src db dst sem")
Local = collections.namedtuple("Local", "sb src db dst")

HBM_SPEC = pl.BlockSpec(memory_space=pltpu.HBM)
SEM_SPEC = pl.BlockSpec(memory_space=pltpu.SEMAPHORE)
SIBLING, X_PEER, Y_PEER, DIAGONAL = 1, 4, 2, 6
OTHER_CHIPS = (X_PEER, Y_PEER, DIAGONAL)


def _whole(ref, i):
    return ref


def _lead(ref, i):
    return ref.at[i]


def _second(ref, i):
    return ref.at[:, i]


def _place():
    x, y, c = lax.axis_index("x"), lax.axis_index("y"), lax.axis_index("c")
    return (x, y, c), 4 * x + 2 * y + c


def _descriptor(cp, bufs, xyc, me, sender, send_sems, recv_sems):
    x, y, c = xyc
    flip = lambda v, bit: (1 - v) if bit else v
    return pltpu.make_async_remote_copy(
        src_ref=cp.src(bufs[cp.sb], me), dst_ref=cp.dst(bufs[cp.db], sender),
        send_sem=send_sems.at[cp.sem], recv_sem=recv_sems.at[cp.sem],
        device_id=(flip(x, cp.mask & 4), flip(y, cp.mask & 2), flip(c, cp.mask & 1)), device_id_type=MESH)


def _exchange(name, bufs, plan, local=()):
    n = len(bufs)

    def body(*refs):
        ins = refs[:n]
        send_sems, recv_sems, local_sems = refs[2 * n:]
        xyc, me = _place()
        own = [pltpu.make_async_copy(lc.src(ins[lc.sb], me), lc.dst(ins[lc.db], me), local_sems.at[i])
               for i, lc in enumerate(local)]
        sends = [_descriptor(cp, ins, xyc, me, me, send_sems, recv_sems) for cp in plan]
        for cp in own + sends:
            cp.start()
        for cp in plan:
            _descriptor(cp, ins, xyc, me, me ^ cp.mask, send_sems, recv_sems).wait_recv()
        for cp in sends:
            cp.wait_send()
        for cp in own:
            cp.wait()

    return pl.pallas_call(
        body, name=name, in_specs=[HBM_SPEC] * n, out_specs=[HBM_SPEC] * n,
        out_shape=[jax.ShapeDtypeStruct(b.shape, b.dtype) for b in bufs],
        input_output_aliases={i: i for i in range(n)},
        scratch_shapes=[pltpu.SemaphoreType.DMA((len(plan),)), pltpu.SemaphoreType.DMA((len(plan),)),
                        pltpu.SemaphoreType.DMA((max(len(local), 1),))],
    )(*bufs)


def _split_call(name, bufs, *, wait=None, wait_sems=None, start=None, local=(), after=None, token=False):
    n = len(bufs)
    n_wait = 2 if wait else 0
    n_after = 1 if after is not None else 0
    n_start = 2 if start else 0

    def body(*refs):
        ins = refs[:n]
        wsend, wrecv = refs[n:n + n_wait] if wait else (None, None)
        outs = refs[n + n_wait + n_after:]
        ssend, srecv = outs[:n_start] if start else (None, None)
        rest = outs[n_start + n:]
        xyc, me = _place()
        for cp in wait or ():
            d = _descriptor(cp, ins, xyc, me, me ^ cp.mask, wsend, wrecv)
            d.wait_send()
            d.wait_recv()
        own = [pltpu.make_async_copy(lc.src(ins[lc.sb], me), lc.dst(ins[lc.db], me), rest[-1].at[i])
               for i, lc in enumerate(local)]
        for cp in own:
            cp.start()
        for cp in start or ():
            _descriptor(cp, ins, xyc, me, me, ssend, srecv).start()
        for cp in own:
            cp.wait()
        if token:
            rest[0][...] = jnp.zeros_like(rest[0])

    args = [pltpu.with_memory_space_constraint(b, pltpu.HBM) for b in bufs]
    in_specs = [HBM_SPEC] * n
    if wait:
        args += list(wait_sems)
        in_specs += [SEM_SPEC] * 2
    if after is not None:
        args.append(after)
        in_specs.append(pl.BlockSpec(memory_space=pl.ANY))
    out_shape, out_specs = [], []
    if start:
        out_shape += [pltpu.SemaphoreType.DMA((len(start),))] * 2
        out_specs += [SEM_SPEC] * 2
    out_shape += [pltpu.HBM(b.shape, b.dtype) for b in bufs]
    out_specs += [HBM_SPEC] * n
    if token:
        out_shape.append(jax.ShapeDtypeStruct((8, 128), F32))
        out_specs.append(pl.BlockSpec(memory_space=pltpu.VMEM))
    outs = pl.pallas_call(
        body, name=name, in_specs=in_specs, out_specs=out_specs, out_shape=out_shape,
        input_output_aliases={i: n_start + i for i in range(n)},
        scratch_shapes=[pltpu.SemaphoreType.DMA((len(local),))] if local else [],
        compiler_params=pltpu.CompilerParams(has_side_effects=pltpu.SideEffectType.DATAFLOW_SIDE_EFFECTING),
    )(*args)
    sems = tuple(outs[:n_start]) if start else None
    return sems, list(outs[n_start:n_start + n]), (outs[n_start + n] if token else None)


def _adamw(w, g, m, v):
    m = ADAM_B1 * m + (1.0 - ADAM_B1) * g
    v = ADAM_B2 * v + (1.0 - ADAM_B2) * (g * g)
    m_hat = m / (1.0 - ADAM_B1 ** ADAM_STEP)
    v_hat = v / (1.0 - ADAM_B2 ** ADAM_STEP)
    delta = -ADAM_LR * (m_hat / (jnp.sqrt(v_hat) + ADAM_EPS) + ADAM_WD * w)
    return delta, m, v


def _reduce_adam(name, parts, w, m, v, *, tr):
    L, R, C = w.shape
    S = parts[0].shape[0]
    tr = min(tr, R)

    def body(*refs):
        p_refs = refs[:L]
        w_ref, m_ref, v_ref, g_ref, d_ref, nm_ref, nv_ref = refs[L:]
        for l in range(L):
            g = p_refs[l][0].astype(F32)
            for s in range(1, S):
                g = g + p_refs[l][s].astype(F32)
            g_ref[l] = g
            d_ref[l], nm_ref[l], nv_ref[l] = _adamw(w_ref[l], g, m_ref[l], v_ref[l])

    blk = pl.BlockSpec((L, tr, C), lambda r: (0, r, 0))
    out = jax.ShapeDtypeStruct((L, R, C), F32)
    return pl.pallas_call(
        body, name=name, grid=(R // tr,),
        in_specs=[pl.BlockSpec((S, tr, C), lambda r: (0, r, 0))] * L + [blk, blk, blk],
        out_specs=[blk] * 4, out_shape=[out] * 4,
        compiler_params=_params(dimension_semantics=("arbitrary",)),
    )(*parts, w, m, v)


def _small_reduce(parts):
    D = parts.shape[2]
    rows = [0, 1, 8, 9, 16, 17, 24, 25, 18, 19, 20, 2]

    def body(p_ref, out_ref):
        s = p_ref[0]
        for d in range(1, N_DEV):
            s = s + p_ref[d]
        out_ref[...] = jnp.zeros_like(out_ref)
        for r, src in enumerate(rows):
            out_ref[r:r + 1, :] = s[src:src + 1, :]

    return pl.pallas_call(body, name="small_reduce", out_shape=jax.ShapeDtypeStruct((16, D), F32))(parts)


def _small_adam(g_gain, g_taps, g_scale, gains, taps, scale):
    def body(gg, gt, gs, wg, mg, vg, wt, mt, vt, ws, ms, vs, *outs):
        for k, (g, w, m, v) in enumerate(((gg, wg, mg, vg), (gt, wt, mt, vt), (gs, ws, ms, vs))):
            outs[3 * k][...], outs[3 * k + 1][...], outs[3 * k + 2][...] = _adamw(w[...], g[...], m[...], v[...])

    shapes = [jax.ShapeDtypeStruct(t[0].shape, F32) for t in (gains, taps, scale) for _ in range(3)]
    return pl.pallas_call(body, name="small_adam", out_shape=shapes)(g_gain, g_taps, g_scale, *gains, *taps, *scale)


def kernel(x, norm_gains, pool_w, pool_scale, conv_in_w, conv_w, conv_out_w, ffn_gate_up_w, ffn_down_w, loss_target, m_norm_gains, m_pool_w, m_pool_scale, m_conv_in_w, m_conv_w, m_conv_out_w, m_ffn_gate_up_w, m_ffn_down_w, v_norm_gains, v_pool_w, v_pool_scale, v_conv_in_w, v_conv_w, v_conv_out_w, v_ffn_gate_up_w, v_ffn_down_w):
    T, D = x.shape[1], x.shape[2]
    tm = min(512, T)
    tm_b = min(256, T)
    tk = min(2048, T)
    n_layers = ffn_gate_up_w.shape[0]
    fb = ffn_gate_up_w.shape[2]
    fr = ffn_down_w.shape[1]
    dcol = norm_gains.shape[2]
    cb = conv_in_w.shape[2]
    gw = pool_w.shape[3]
    me = 4 * lax.axis_index("x") + 2 * lax.axis_index("y") + lax.axis_index("c")

    small_w = jnp.concatenate([norm_gains.reshape(8, dcol), jnp.pad(conv_w[0], ((0, 5), (0, 0)))], axis=0)
    every = range(1, N_DEV)
    wgu_t, m_wgu_t, v_wgu_t = (jnp.swapaxes(a, 1, 2) for a in (ffn_gate_up_w, m_ffn_gate_up_w, v_ffn_gate_up_w))
    pw_shard = pool_w[0].astype(BF16)
    shards = [w.astype(BF16) for w in (wgu_t[0], ffn_down_w[0], conv_in_w[0], conv_out_w[0], wgu_t[1], ffn_down_w[1])]
    n_big = len(shards)
    own_lead = lambda s: lax.dynamic_update_slice(lax.empty((N_DEV,) + s.shape, s.dtype), s[None], (me,) + (0,) * s.ndim)
    pw_land = lax.dynamic_update_slice(lax.empty((4, N_DEV, gw // N_DEV, gw), BF16), pw_shard[:, None], (0, me, 0, 0))
    first = [pw_shard, small_w, pw_land, own_lead(small_w)]
    shard_at, land_at = len(first), len(first) + n_big
    direct = ([Copy(m, 0, _whole, 2, _second, m - 1) for m in every]
              + [Copy(m, 1, _whole, 3, _lead, N_DEV - 2 + m) for m in every])
    level1 = [Copy(mask, shard_at + n, _whole, land_at + n, _lead, len(direct) + 4 * n + j)
              for n in range(n_big) for j, mask in enumerate((SIBLING,) + OTHER_CHIPS)]
    sems1, bufs1, _ = _split_call("gather_start", first + shards + [own_lead(s) for s in shards], start=direct + level1)
    _, _, pw_g, small_g = _split_call("gather_small_done", bufs1[:len(first)], wait=direct, wait_sems=sems1)[1]
    pw = pw_g.reshape(4, gw, gw)
    small_full = jnp.swapaxes(small_g, 0, 1).reshape(16, D)
    gain = lambda l, s: small_full[4 * l + s][None, :]
    taps = small_full[8:16]

    def forward_on(name, group, after):
        k = len(group)
        landed = [Copy(cp.mask, i, cp.src, k + i, cp.dst, cp.sem)
                  for i, n in enumerate(group) for cp in level1 if cp.sb == shard_at + n]
        onward = [Copy(SIBLING, i, (lambda ref, me, m=m: ref.at[me ^ m]), i, (lambda ref, sender, m=m: ref.at[sender ^ m]), 3 * i + j)
                  for i in range(k) for j, m in enumerate(OTHER_CHIPS)]
        sems2, bufs2, _ = _split_call(
            name + "_forward", [bufs1[shard_at + n] for n in group] + [bufs1[land_at + n] for n in group], wait=landed,
            wait_sems=sems1, start=[cp._replace(sb=k + cp.sb, db=k + cp.db) for cp in onward], after=after)
        return name, onward, sems2, bufs2[k:]

    def arrived(state):
        name, onward, sems2, lands2 = state
        return _split_call(name + "_done", lands2, wait=onward, wait_sems=sems2)[1]

    h0 = x[0]
    h1 = _pool_fwd(h0, pw, pool_scale, gain(0, 0), gain(0, 1), tm=tm)
    wgu0, wd0 = arrived(forward_on("gather_ffn0", [0, 1], h1))
    wd0 = wd0.reshape(N_DEV * fr, D)
    h2, gu0, ff0 = _ffn_fwd(h1, gain(0, 2), gain(0, 3), wgu0, wd0, 0, None, tm=tm)
    win_g, wout_g = arrived(forward_on("gather_conv", [2, 3], h2))
    wout = wout_g.reshape(D, D)
    h3, proj, y = _conv_fwd(h2, gain(1, 0), gain(1, 1), win_g, taps, wout, tm=tm)
    wgu1, wd1 = arrived(forward_on("gather_ffn1", [4, 5], h3))
    wd1 = wd1.reshape(N_DEV * fr, D)
    dh4, gu1, ff1, loss_part = _ffn_fwd(h3, gain(1, 2), gain(1, 3), wgu1, wd1, 1, loss_target[0], tm=tm)
    loss = lax.psum(loss_part[0, 0], ("x", "y", "c"))

    chip = me >> 1

    def scatter_start(name, sums):
        k = len(sums)
        lands = [lax.dynamic_update_slice(lax.empty(s.shape, BF16), lax.dynamic_index_in_dim(s, chip, 0), (chip, 0, 0))
                 for s in sums]
        plan = [Copy(m, n, (lambda ref, i, m=m: ref.at[(i ^ m) >> 1]), k + n, (lambda ref, i: ref.at[i >> 1]), 3 * n + j)
                for n in range(k) for j, m in enumerate(OTHER_CHIPS)]
        sems, bufs, tok = _split_call(name + "_start", sums + lands, start=plan, token=True)
        return (name, plan, sems, bufs), tok

    def scatter_done(state, after):
        name, plan, sems, bufs = state
        return _split_call(name + "_done", bufs, wait=plan, wait_sems=sems, after=after)[1][len(bufs) // 2:]

    seq = lambda i, k: (k, 0)
    by_block = pl.BlockSpec((None, tk, fb), lambda i, k: (i, k, 0))
    rows = pl.BlockSpec((tk, D), seq)
    dh3, dgu1, dff1, c1, act1, small_f1 = _ffn_bwd(dh4, h3, ff1, gu1, gain(1, 2), gain(1, 3), wgu1, wd1, 1, tm=tm_b)
    g_wgu1 = _wgrad("wgrad_gate_up_1", dgu1, c1, by_block, rows, (fb, D), N_DEV, tk=tk)
    g_wd1 = _wgrad("wgrad_down_1", act1, dff1, by_block, rows, (fb, D), N_DEV // 2, tk=tk)
    rs_ffn1, tok = scatter_start("scatter_ffn1", [g_wgu1, g_wd1])
    dh2, dproj, z, a1, dy, small_c = _conv_bwd(dh3, h2, y, proj, gain(1, 0), gain(1, 1), win_g, taps, wout, tm=tm, after=tok)
    g_win = _wgrad("wgrad_conv_in", dproj, a1, pl.BlockSpec((tk, cb), lambda i, k: (k, i)), rows, (D, cb), N_DEV,
                   tk=tk, transpose=True)
    g_wout = _wgrad("wgrad_conv_out", z, dy, rows, rows, (D, D), 1, tk=tk)
    rs_conv, tok = scatter_start("scatter_conv", [g_win, g_wout])
    dh1, dgu0, dff0, c0, act0, small_f0 = _ffn_bwd(dh2, h1, ff0, gu0, gain(0, 2), gain(0, 3), wgu0, wd0, 0, tm=tm_b, after=tok)
    g_wgu0 = _wgrad("wgrad_gate_up_0", dgu0, c0, by_block, rows, (fb, D), N_DEV, tk=tk)
    rs_wgu0, tok = scatter_start("scatter_gate_up_0", [g_wgu0])
    g_wd0 = _wgrad("wgrad_down_0", act0, dff0, by_block, rows, (fb, D), N_DEV // 2, tk=tk, after=tok)
    rs_wd0, tok = scatter_start("scatter_down_0", [g_wd0])
    grad_x, g_pw, small_p = _pool_bwd(dh1, h0, pw, pool_scale, gain(0, 0), gain(0, 1), tm=tm, after=tok)

    small_part = jnp.concatenate([small_p, small_f0, small_c, small_f1], axis=0)
    g_pw = g_pw.reshape(4, N_DEV, gw // N_DEV, gw)
    r_pw, r_small = _exchange(
        "scatter_small", [g_pw, small_part, lax.empty(g_pw.shape, BF16), lax.empty((N_DEV,) + small_part.shape, F32)],
        [Copy(m, 0, (lambda ref, i, m=m: ref.at[:, i ^ m]), 2, _second, m - 1) for m in every]
        + [Copy(m, 1, _whole, 3, _lead, N_DEV - 2 + m) for m in every],
        [Local(0, _second, 2, _second), Local(1, _whole, 3, _lead)])[2:]
    (r_wgu1, r_wd1), (r_win, r_wout), (r_wgu0,), (r_wd0,) = (
        scatter_done(state, r_small) for state in (rs_ffn1, rs_conv, rs_wgu0, rs_wd0))

    o_pw = _reduce_adam("adam_pool_w", [r_pw[g] for g in range(4)], pool_w[0], m_pool_w[0], v_pool_w[0], tr=32)
    o_win = _reduce_adam("adam_conv_in", [r_win], conv_in_w, m_conv_in_w, v_conv_in_w, tr=256)
    o_wout = _reduce_adam("adam_conv_out", [r_wout], conv_out_w, m_conv_out_w, v_conv_out_w, tr=128)
    o_wgu = [jnp.swapaxes(o, 1, 2) for o in _reduce_adam("adam_gate_up", [r_wgu0, r_wgu1], wgu_t, m_wgu_t, v_wgu_t, tr=176)]
    o_wd = _reduce_adam("adam_down", [r_wd0, r_wd1], ffn_down_w, m_ffn_down_w, v_ffn_down_w, tr=176)
    g_small = _small_reduce(r_small)
    g_cols = lax.dynamic_slice(g_small, (0, me * dcol), (16, dcol))
    o_small = _small_adam(
        g_cols[0:8], g_cols[8:11], g_small[11:12],
        (norm_gains.reshape(8, dcol), m_norm_gains.reshape(8, dcol), v_norm_gains.reshape(8, dcol)),
        (conv_w[0], m_conv_w[0], v_conv_w[0]), (pool_scale, m_pool_scale, v_pool_scale))
    d_gain, nm_gain, nv_gain, d_taps, nm_taps, nv_taps, d_scale, nm_scale, nv_scale = o_small

    gshape = norm_gains.shape
    per = lambda k: (
        (g_cols[0:8].reshape(gshape), d_gain.reshape(gshape), nm_gain.reshape(gshape), nv_gain.reshape(gshape))[k],
        o_pw[k][None], (g_small[11:12], d_scale, nm_scale, nv_scale)[k], o_win[k],
        (g_cols[8:11][None], d_taps[None], nm_taps[None], nv_taps[None])[k], o_wout[k], o_wgu[k], o_wd[k])
    return (loss, grad_x[None], *per(0), *per(1), *per(2), *per(3))
```

```python
import collections
import functools

import jax
import jax.numpy as jnp
from jax import lax
from jax.experimental import pallas as pl
from jax.experimental.pallas import tpu as pltpu

N_DEV = 8
RMS_EPS = 1e-6
POOL_WINDOWS = (2, 4, 8, 16)
POOL_HALO = 16
CONV_HALO = 16
ADAM_LR, ADAM_B1, ADAM_B2, ADAM_EPS, ADAM_WD, ADAM_STEP = 0.001, 0.9, 0.999, 1e-08, 0.01, 10

VMEM_LIMIT = 56 * 2**20
BF16 = jnp.bfloat16
F32 = jnp.float32
MESH = pl.DeviceIdType.MESH


def _params(**kw):
    return pltpu.CompilerParams(vmem_limit_bytes=VMEM_LIMIT, **kw)


def _resident(shape, index_map):
    return pl.BlockSpec(shape, index_map, pipeline_mode=pl.Buffered(1))


def _ordered(body, n_in, after):
    if after is None:
        return functools.partial(body), [], []
    return (lambda *refs: body(*refs[:n_in], *refs[n_in + 1:])), [after], [pl.BlockSpec(memory_space=pl.ANY)]


def _rms_fwd(x, g):
    r = lax.rsqrt(jnp.mean(x * x, axis=-1, keepdims=True) + RMS_EPS)
    return x * r * g


def _rms_bwd(x, g, dy):
    r = lax.rsqrt(jnp.mean(x * x, axis=-1, keepdims=True) + RMS_EPS)
    xhat = x * r
    dg = jnp.sum(dy * xhat, axis=0, keepdims=True)
    t = dy * g
    dx = r * (t - xhat * jnp.mean(t * xhat, axis=-1, keepdims=True))
    return dx, dg


def _sigmoid(x):
    return 0.5 * jnp.tanh(0.5 * x) + 0.5


def _dot(a, b):
    return jnp.dot(a, b, preferred_element_type=F32)


def _dot_nt(a, b):
    return lax.dot_general(a, b, (((1,), (1,)), ((), ())), preferred_element_type=F32)


def _dot_tn(a, b):
    return lax.dot_general(a, b, (((0,), (0,)), ((), ())), preferred_element_type=F32)


def _row_inverse_counts(tile, tm):
    pos = (lax.broadcasted_iota(jnp.int32, (tm, 1), 0) + tile * tm + 1).astype(F32)
    return [1.0 / jnp.minimum(pos, float(w)) for w in POOL_WINDOWS]


def _pool_from_ext(ext, a, invs, gw):
    s = ext
    outs = []
    for g, w in enumerate(POOL_WINDOWS):
        s = s[:, (gw if g else 0):]
        s = s + pltpu.roll(s, w // 2, 0)
        outs.append(s[POOL_HALO:, :gw] * invs[g] - a[:, g * gw:(g + 1) * gw])
    return outs


def _pool_fwd(h, pw, scale, g_pre, g_post, *, tm, after=None):
    T, D = h.shape
    gw = D // len(POOL_WINDOWS)
    hb = tm // POOL_HALO

    def body(h_ref, halo_ref, pw_ref, scale_ref, gpre_ref, gpost_ref, out_ref, ext_ref):
        i = pl.program_id(0)
        x = h_ref[...]
        a = _rms_fwd(x, gpre_ref[...])
        ah = _rms_fwd(halo_ref[...], gpre_ref[...])
        ext_ref[0:POOL_HALO, :] = jnp.where(i == 0, 0.0, ah)
        ext_ref[POOL_HALO:, :] = a
        pooled = _pool_from_ext(ext_ref[...], a, _row_inverse_counts(i, tm), gw)
        mixed = jnp.concatenate([_dot(p.astype(BF16), pw_ref[g]) for g, p in enumerate(pooled)], axis=1)
        out_ref[...] = x + _rms_fwd(mixed * scale_ref[...], gpost_ref[...])

    vec = _resident((1, D), lambda i: (0, 0))
    fn, xa, xs = _ordered(body, 6, after)
    return pl.pallas_call(
        fn, name="pool_fwd", grid=(T // tm,),
        in_specs=[pl.BlockSpec((tm, D), lambda i: (i, 0)),
                  pl.BlockSpec((POOL_HALO, D), lambda i: (jnp.maximum(i * hb - 1, 0), 0)),
                  _resident(pw.shape, lambda i: (0, 0, 0)), vec, vec, vec] + xs,
        out_specs=pl.BlockSpec((tm, D), lambda i: (i, 0)),
        out_shape=jax.ShapeDtypeStruct((T, D), F32),
        scratch_shapes=[pltpu.VMEM((tm + POOL_HALO, D), F32)],
        compiler_params=_params(dimension_semantics=("arbitrary",)),
    )(h, h, pw, scale, g_pre, g_post, *xa)


def _pool_bwd(dh, h, pw, scale, g_pre, g_post, *, tm, after=None):
    T, D = h.shape
    gw = D // len(POOL_WINDOWS)
    hb = tm // POOL_HALO
    nt = T // tm
    n_ext = tm + POOL_HALO

    def body(dh_ref, h_ref, halo_ref, pw_ref, scale_ref, gpre_ref, gpost_ref,
             dx_ref, dpw_ref, small_ref, ext_ref, ext2_ref, carry_ref, dpw_acc):
        i = pl.program_id(0)
        tile = nt - 1 - i

        @pl.when(i == 0)
        def _():
            small_ref[...] = jnp.zeros_like(small_ref)
            dpw_acc[...] = jnp.zeros_like(dpw_acc)
            carry_ref[...] = jnp.zeros_like(carry_ref)

        x = h_ref[...]
        dout = dh_ref[...]
        a = _rms_fwd(x, gpre_ref[...])
        ah = _rms_fwd(halo_ref[...], gpre_ref[...])
        ext_ref[0:POOL_HALO, :] = jnp.where(tile == 0, 0.0, ah)
        ext_ref[POOL_HALO:, :] = a
        invs = _row_inverse_counts(tile, tm)
        pooled = [p.astype(BF16) for p in _pool_from_ext(ext_ref[...], a, invs, gw)]
        mixed_pre = jnp.concatenate([_dot(p, pw_ref[g]) for g, p in enumerate(pooled)], axis=1)
        scale_v = scale_ref[...]
        dmixed, dg_post = _rms_bwd(mixed_pre * scale_v, gpost_ref[...], dout)
        small_ref[1:2, :] += dg_post
        small_ref[2:3, :] += jnp.sum(dmixed * mixed_pre, axis=0, keepdims=True)
        dpre = (dmixed * scale_v).astype(BF16)
        dpooled = []
        for g in range(len(POOL_WINDOWS)):
            dp = dpre[:, g * gw:(g + 1) * gw]
            dpw_acc[g] += _dot_tn(pooled[g], dp)
            dpooled.append(_dot_nt(dp, pw_ref[g]))
        q = jnp.concatenate([d * invs[g] for g, d in enumerate(dpooled)], axis=1)
        ext2_ref[0:tm, :] = q
        ext2_ref[tm:, :] = carry_ref[...]
        carry_ref[...] = q[0:POOL_HALO, :]
        s = ext2_ref[...]
        da = []
        for g, w in enumerate(POOL_WINDOWS):
            s = s[:, (gw if g else 0):]
            s = s + pltpu.roll(s, n_ext - w // 2, 0)
            da.append(s[0:tm, :gw] - dpooled[g])
        dx, dg_pre = _rms_bwd(x, gpre_ref[...], jnp.concatenate(da, axis=1))
        small_ref[0:1, :] += dg_pre
        dx_ref[...] = dout + dx

        @pl.when(i == nt - 1)
        def _():
            dpw_ref[...] = dpw_acc[...].astype(BF16)

    vec = _resident((1, D), lambda i: (0, 0))
    rev = lambda i: (nt - 1 - i, 0)
    fn, xa, xs = _ordered(body, 7, after)
    return pl.pallas_call(
        fn, name="pool_bwd", grid=(nt,),
        in_specs=[pl.BlockSpec((tm, D), rev), pl.BlockSpec((tm, D), rev),
                  pl.BlockSpec((POOL_HALO, D), lambda i: (jnp.maximum((nt - 1 - i) * hb - 1, 0), 0)),
                  _resident(pw.shape, lambda i: (0, 0, 0)), vec, vec, vec] + xs,
        out_specs=[pl.BlockSpec((tm, D), rev),
                   pl.BlockSpec(pw.shape, lambda i: (0, 0, 0)),
                   pl.BlockSpec((8, D), lambda i: (0, 0))],
        out_shape=[jax.ShapeDtypeStruct((T, D), F32), jax.ShapeDtypeStruct(pw.shape, BF16),
                   jax.ShapeDtypeStruct((8, D), F32)],
        scratch_shapes=[pltpu.VMEM((n_ext, D), F32), pltpu.VMEM((n_ext, D), F32),
                        pltpu.VMEM((POOL_HALO, D), F32), pltpu.VMEM(pw.shape, F32)],
        compiler_params=_params(dimension_semantics=("arbitrary",)),
    )(dh, h, h, pw, scale, g_pre, g_post, *xa)


def _ffn_fwd(h, g_pre, g_post, wgu, wd, layer, target, *, tm, after=None):
    T, D = h.shape
    nblk, fb = wgu.shape[0], wgu.shape[1]
    half = nblk // 2
    last = target is not None

    def body(*refs):
        if last:
            h_ref, gpre_ref, gpost_ref, wgu_ref, wd_ref, tgt_ref, out_ref, gu_ref, ff_ref, loss_ref = refs
        else:
            h_ref, gpre_ref, gpost_ref, wgu_ref, wd_ref, out_ref, gu_ref, ff_ref = refs
        x = h_ref[...]
        cb = _rms_fwd(x, gpre_ref[...]).astype(BF16)
        acc = jnp.zeros((tm, D), F32)
        for j in range(half):
            g = _dot_nt(cb, wgu_ref[j])
            u = _dot_nt(cb, wgu_ref[j + half])
            gu_ref[j] = g.astype(BF16)
            gu_ref[j + half] = u.astype(BF16)
            act = (g * _sigmoid(g) * u).astype(BF16)
            acc = acc + _dot(act, wd_ref[j * fb:(j + 1) * fb, :])
        ff_ref[...] = acc.astype(BF16)
        hout = x + _rms_fwd(acc, gpost_ref[...])
        if last:
            diff = hout - tgt_ref[...]
            out_ref[...] = diff * (1.0 / D)

            @pl.when(pl.program_id(0) == 0)
            def _():
                loss_ref[...] = jnp.zeros_like(loss_ref)

            loss_ref[...] += jnp.sum(diff * diff) * (0.5 / D)
        else:
            out_ref[...] = hout

    vec = _resident((1, D), lambda i: (0, 0))
    tile = pl.BlockSpec((tm, D), lambda i: (i, 0))
    in_specs = [tile, vec, vec,
                _resident(wgu.shape, lambda i: (0, 0, 0)), _resident(wd.shape, lambda i: (0, 0))]
    out_specs = [tile, pl.BlockSpec((nblk, tm, fb), lambda i: (0, i, 0)), tile]
    out_shape = [jax.ShapeDtypeStruct((T, D), F32), jax.ShapeDtypeStruct((nblk, T, fb), BF16),
                 jax.ShapeDtypeStruct((T, D), BF16)]
    args = [h, g_pre, g_post, wgu, wd]
    if last:
        in_specs.append(tile)
        args.append(target)
        out_specs.append(pl.BlockSpec((8, 128), lambda i: (0, 0)))
        out_shape.append(jax.ShapeDtypeStruct((8, 128), F32))
    fn, xa, xs = _ordered(body, len(args), after)
    return pl.pallas_call(
        fn, name=f"ffn_fwd_{layer}", grid=(T // tm,), in_specs=in_specs + xs, out_specs=out_specs,
        out_shape=out_shape, compiler_params=_params(dimension_semantics=("arbitrary",)),
    )(*args, *xa)


def _ffn_bwd(dh, h, ff, gu, g_pre, g_post, wgu, wd, layer, *, tm, after=None):
    T, D = h.shape
    nblk, fb = wgu.shape[0], wgu.shape[1]
    half = nblk // 2

    def body(dh_ref, h_ref, ff_ref, gu_ref, gpre_ref, gpost_ref, wgu_ref, wd_ref,
             dx_ref, dgu_ref, dff_ref, c_ref, act_ref, small_ref):
        @pl.when(pl.program_id(0) == 0)
        def _():
            small_ref[...] = jnp.zeros_like(small_ref)

        dout = dh_ref[...]
        dff, dg_post = _rms_bwd(ff_ref[...].astype(F32), gpost_ref[...], dout)
        small_ref[1:2, :] += dg_post
        dffb = dff.astype(BF16)
        dff_ref[...] = dffb
        dc = jnp.zeros((tm, D), F32)
        last = None
        for j in range(half + 1):
            if j < half:
                dact = _dot_nt(dffb, wd_ref[j * fb:(j + 1) * fb, :])
            if last is not None:
                dc = dc + _dot(last[0], wgu_ref[j - 1]) + _dot(last[1], wgu_ref[j - 1 + half])
            if j < half:
                g = gu_ref[j].astype(F32)
                u = gu_ref[j + half].astype(F32)
                s = _sigmoid(g)
                silu = g * s
                act_ref[j] = (silu * u).astype(BF16)
                dg = (dact * u * (s * (1.0 + g * (1.0 - s)))).astype(BF16)
                du = (dact * silu).astype(BF16)
                dgu_ref[j] = dg
                dgu_ref[j + half] = du
                last = (dg, du)
        x = h_ref[...]
        c_ref[...] = _rms_fwd(x, gpre_ref[...]).astype(BF16)
        dx, dg_pre = _rms_bwd(x, gpre_ref[...], dc)
        small_ref[0:1, :] += dg_pre
        dx_ref[...] = dout + dx

    vec = _resident((1, D), lambda i: (0, 0))
    tile = pl.BlockSpec((tm, D), lambda i: (i, 0))
    blk = pl.BlockSpec((nblk, tm, fb), lambda i: (0, i, 0))
    fn, xa, xs = _ordered(body, 8, after)
    return pl.pallas_call(
        fn, name=f"ffn_bwd_{layer}", grid=(T // tm,),
        in_specs=[tile, tile, tile, blk, vec, vec,
                  _resident(wgu.shape, lambda i: (0, 0, 0)), _resident(wd.shape, lambda i: (0, 0))] + xs,
        out_specs=[tile, blk, tile, tile, pl.BlockSpec((half, tm, fb), lambda i: (0, i, 0)),
                   pl.BlockSpec((8, D), lambda i: (0, 0))],
        out_shape=[jax.ShapeDtypeStruct((T, D), F32), jax.ShapeDtypeStruct((nblk, T, fb), BF16),
                   jax.ShapeDtypeStruct((T, D), BF16), jax.ShapeDtypeStruct((T, D), BF16),
                   jax.ShapeDtypeStruct((half, T, fb), BF16), jax.ShapeDtypeStruct((8, D), F32)],
        compiler_params=_params(dimension_semantics=("arbitrary",)),
    )(dh, h, ff, gu, g_pre, g_post, wgu, wd, *xa)


def _conv_fwd(h, g_pre, g_post, win, taps, wout, *, tm, after=None):
    T, D = h.shape
    nblk, cb = win.shape[0], win.shape[2]

    def body(h_ref, gpre_ref, gpost_ref, win_ref, taps_ref, wout_ref,
             out_ref, proj_ref, y_ref, proj_scr, ext_ref, carry_ref):
        i = pl.program_id(0)

        @pl.when(i == 0)
        def _():
            carry_ref[...] = jnp.zeros_like(carry_ref)

        x = h_ref[...]
        a = _rms_fwd(x, gpre_ref[...]).astype(BF16)
        for k in range(nblk):
            proj_scr[:, k * cb:(k + 1) * cb] = _dot(a, win_ref[k])
        proj_ref[...] = proj_scr[...].astype(BF16)
        u = proj_scr[:, D:2 * D] * proj_scr[:, 2 * D:3 * D]
        ext_ref[0:CONV_HALO, :] = carry_ref[...]
        ext_ref[CONV_HALO:, :] = u
        carry_ref[...] = u[tm - CONV_HALO:, :]
        e = ext_ref[...]
        conv = (taps_ref[2:3, :] * u + taps_ref[1:2, :] * pltpu.roll(e, 1, 0)[CONV_HALO:, :]
                + taps_ref[0:1, :] * pltpu.roll(e, 2, 0)[CONV_HALO:, :])
        z = (proj_scr[:, 0:D] * conv).astype(BF16)
        y = _dot(z, wout_ref[...])
        y_ref[...] = y.astype(BF16)
        out_ref[...] = x + _rms_fwd(y, gpost_ref[...])

    vec = _resident((1, D), lambda i: (0, 0))
    tile = pl.BlockSpec((tm, D), lambda i: (i, 0))
    fn, xa, xs = _ordered(body, 6, after)
    return pl.pallas_call(
        fn, name="conv_fwd", grid=(T // tm,),
        in_specs=[tile, vec, vec, _resident(win.shape, lambda i: (0, 0, 0)),
                  _resident(taps.shape, lambda i: (0, 0)), _resident(wout.shape, lambda i: (0, 0))] + xs,
        out_specs=[tile, pl.BlockSpec((tm, 3 * D), lambda i: (i, 0)), tile],
        out_shape=[jax.ShapeDtypeStruct((T, D), F32), jax.ShapeDtypeStruct((T, 3 * D), BF16),
                   jax.ShapeDtypeStruct((T, D), BF16)],
        scratch_shapes=[pltpu.VMEM((tm, 3 * D), F32), pltpu.VMEM((tm + CONV_HALO, D), F32),
                        pltpu.VMEM((CONV_HALO, D), F32)],
        compiler_params=_params(dimension_semantics=("arbitrary",)),
    )(h, g_pre, g_post, win, taps, wout, *xa)


def _conv_bwd(dh, h, y, proj, g_pre, g_post, win, taps, wout, *, tm, after=None):
    T, D = h.shape
    nblk, cb = win.shape[0], win.shape[2]
    nt = T // tm
    hb = tm // CONV_HALO
    n_ext = tm + CONV_HALO

    def body(dh_ref, h_ref, y_ref, proj_ref, halo_ref, gpre_ref, gpost_ref, win_ref, taps_ref, wout_ref,
             dx_ref, dproj_ref, z_ref, a_ref, dy_ref, small_ref, ext_ref, ext2_ref, carry_ref):
        i = pl.program_id(0)
        tile = nt - 1 - i

        @pl.when(i == 0)
        def _():
            small_ref[...] = jnp.zeros_like(small_ref)
            carry_ref[...] = jnp.zeros_like(carry_ref)

        dout = dh_ref[...]
        dy, dg_post = _rms_bwd(y_ref[...].astype(F32), gpost_ref[...], dout)
        small_ref[1:2, :] += dg_post
        dyb = dy.astype(BF16)
        dy_ref[...] = dyb
        dz = _dot_nt(dyb, wout_ref[...])
        bgate = proj_ref[:, 0:D].astype(F32)
        cgate = proj_ref[:, D:2 * D].astype(F32)
        v = proj_ref[:, 2 * D:3 * D].astype(F32)
        u = cgate * v
        uh = halo_ref[:, D:2 * D].astype(F32) * halo_ref[:, 2 * D:3 * D].astype(F32)
        ext_ref[0:CONV_HALO, :] = jnp.where(tile == 0, 0.0, uh)
        ext_ref[CONV_HALO:, :] = u
        e = ext_ref[...]
        u1 = pltpu.roll(e, 1, 0)[CONV_HALO:, :]
        u2 = pltpu.roll(e, 2, 0)[CONV_HALO:, :]
        t0, t1, t2 = taps_ref[0:1, :], taps_ref[1:2, :], taps_ref[2:3, :]
        conv = t2 * u + t1 * u1 + t0 * u2
        z_ref[...] = (bgate * conv).astype(BF16)
        dconv = dz * bgate
        small_ref[2:3, :] += jnp.sum(dconv * u2, axis=0, keepdims=True)
        small_ref[3:4, :] += jnp.sum(dconv * u1, axis=0, keepdims=True)
        small_ref[4:5, :] += jnp.sum(dconv * u, axis=0, keepdims=True)
        ext2_ref[0:tm, :] = dconv
        ext2_ref[tm:, :] = carry_ref[...]
        carry_ref[...] = dconv[0:CONV_HALO, :]
        e2 = ext2_ref[...]
        du = (t2 * dconv + t1 * pltpu.roll(e2, n_ext - 1, 0)[0:tm, :]
              + t0 * pltpu.roll(e2, n_ext - 2, 0)[0:tm, :])
        dproj_ref[:, 0:D] = (dz * conv).astype(BF16)
        dproj_ref[:, D:2 * D] = (du * v).astype(BF16)
        dproj_ref[:, 2 * D:3 * D] = (du * cgate).astype(BF16)
        da = jnp.zeros((tm, D), F32)
        for k in range(nblk):
            da = da + _dot_nt(dproj_ref[:, k * cb:(k + 1) * cb], win_ref[k])
        x = h_ref[...]
        a_ref[...] = _rms_fwd(x, gpre_ref[...]).astype(BF16)
        dx, dg_pre = _rms_bwd(x, gpre_ref[...], da)
        small_ref[0:1, :] += dg_pre
        dx_ref[...] = dout + dx

    vec = _resident((1, D), lambda i: (0, 0))
    rev = lambda i: (nt - 1 - i, 0)
    tile = pl.BlockSpec((tm, D), rev)
    wide = pl.BlockSpec((tm, 3 * D), rev)
    fn, xa, xs = _ordered(body, 10, after)
    return pl.pallas_call(
        fn, name="conv_bwd", grid=(nt,),
        in_specs=[tile, tile, tile, wide,
                  pl.BlockSpec((CONV_HALO, 3 * D), lambda i: (jnp.maximum((nt - 1 - i) * hb - 1, 0), 0)),
                  vec, vec, _resident(win.shape, lambda i: (0, 0, 0)),
                  _resident(taps.shape, lambda i: (0, 0)), _resident(wout.shape, lambda i: (0, 0))] + xs,
        out_specs=[tile, wide, tile, tile, tile, pl.BlockSpec((8, D), lambda i: (0, 0))],
        out_shape=[jax.ShapeDtypeStruct((T, D), F32), jax.ShapeDtypeStruct((T, 3 * D), BF16),
                   jax.ShapeDtypeStruct((T, D), BF16), jax.ShapeDtypeStruct((T, D), BF16),
                   jax.ShapeDtypeStruct((T, D), BF16), jax.ShapeDtypeStruct((8, D), F32)],
        scratch_shapes=[pltpu.VMEM((n_ext, D), F32), pltpu.VMEM((n_ext, D), F32),
                        pltpu.VMEM((CONV_HALO, D), F32)],
        compiler_params=_params(dimension_semantics=("arbitrary",)),
    )(dh, h, y, proj, proj, g_pre, g_post, win, taps, wout, *xa)


def _wgrad(name, a, b, a_spec, b_spec, block, n_blocks, *, tk, transpose=False, after=None):
    T = a.shape[-2]
    nk = T // tk
    M, N = block
    m = N_DEV // n_blocks
    R = M // m
    acc_block = block[::-1] if transpose else block

    def body(a_ref, b_ref, out_ref, acc_ref, stage_ref, recv_ref, send_sems, recv_sems):
        i, k = pl.program_id(0), pl.program_id(1)
        x, y, c = lax.axis_index("x"), lax.axis_index("y"), lax.axis_index("c")

        def sent(blk, p):
            owner = blk * m + p
            q = owner // 2
            return (owner % 2) != c, pltpu.make_async_remote_copy(
                src_ref=stage_ref.at[p * R:(p + 1) * R], dst_ref=recv_ref.at[q], send_sem=send_sems.at[q],
                recv_sem=recv_sems.at[q], device_id=(x, y, 1 - c), device_id_type=MESH)

        @pl.when(k == 0)
        def _():
            acc_ref[...] = jnp.zeros_like(acc_ref)

        acc_ref[...] += _dot_tn(a_ref[...], b_ref[...])

        @pl.when(k == nk - 1)
        def _():
            for p in range(m):
                away, copy = sent(jnp.maximum(i - 1, 0), p)

                @pl.when(jnp.logical_and(i > 0, away))
                def _():
                    copy.wait_send()

            acc = acc_ref[...]
            stage_ref[...] = (acc.T if transpose else acc).astype(BF16)
            for p in range(m):
                away, copy = sent(i, p)

                @pl.when(away)
                def _():
                    copy.start()

                @pl.when(jnp.logical_not(away))
                def _():
                    out_ref[(i * m + p) // 2] = stage_ref[p * R:(p + 1) * R, :]

        @pl.when(jnp.logical_and(i == n_blocks - 1, k == nk - 1))
        def _():
            for p in range(m):
                away, copy = sent(i, p)

                @pl.when(away)
                def _():
                    copy.wait_send()

            for q in range(N_DEV // 2):
                pltpu.make_async_remote_copy(
                    src_ref=stage_ref.at[0:R], dst_ref=recv_ref.at[q], send_sem=send_sems.at[q],
                    recv_sem=recv_sems.at[q], device_id=(x, y, 1 - c), device_id_type=MESH).wait_recv()
                out_ref[q] = (out_ref[q].astype(F32) + recv_ref[q].astype(F32)).astype(BF16)

    fn, xa, xs = _ordered(body, 2, after)
    return pl.pallas_call(
        fn, name=name, grid=(n_blocks, nk), in_specs=[a_spec, b_spec] + xs,
        out_specs=pl.BlockSpec((N_DEV // 2, R, N), lambda i, k: (0, 0, 0)),
        out_shape=jax.ShapeDtypeStruct((N_DEV // 2, R, N), BF16),
        scratch_shapes=[pltpu.VMEM(acc_block, F32), pltpu.VMEM(block, BF16), pltpu.VMEM((N_DEV // 2, R, N), BF16),
                        pltpu.SemaphoreType.DMA((N_DEV // 2,)), pltpu.SemaphoreType.DMA((N_DEV // 2,))],
        compiler_params=_params(dimension_semantics=("arbitrary", "arbitrary")),
    )(a, b, *xa)


Copy = collections.namedtuple("Copy", "mask sb src db dst sem")
Local = collections.namedtuple("Local", "sb src db dst")

HBM_SPEC = pl.BlockSpec(memory_space=pltpu.HBM)
SEM_SPEC = pl.BlockSpec(memory_space=pltpu.SEMAPHORE)
SIBLING, X_PEER, Y_PEER, DIAGONAL = 1, 4, 2, 6
OTHER_CHIPS = (X_PEER, Y_PEER, DIAGONAL)


def _whole(ref, i):
    return ref


def _lead(ref, i):
    return ref.at[i]


def _second(ref, i):
    return ref.at[:, i]


def _place():
    x, y, c = lax.axis_index("x"), lax.axis_index("y"), lax.axis_index("c")
    return (x, y, c), 4 * x + 2 * y + c


def _descriptor(cp, bufs, xyc, me, sender, send_sems, recv_sems):
    x, y, c = xyc
    flip = lambda v, bit: (1 - v) if bit else v
    return pltpu.make_async_remote_copy(
        src_ref=cp.src(bufs[cp.sb], me), dst_ref=cp.dst(bufs[cp.db], sender),
        send_sem=send_sems.at[cp.sem], recv_sem=recv_sems.at[cp.sem],
        device_id=(flip(x, cp.mask & 4), flip(y, cp.mask & 2), flip(c, cp.mask & 1)), device_id_type=MESH)


def _exchange(name, bufs, plan, local=()):
    n = len(bufs)

    def body(*refs):
        ins = refs[:n]
        send_sems, recv_sems, local_sems = refs[2 * n:]
        xyc, me = _place()
        own = [pltpu.make_async_copy(lc.src(ins[lc.sb], me), lc.dst(ins[lc.db], me), local_sems.at[i])
               for i, lc in enumerate(local)]
        sends = [_descriptor(cp, ins, xyc, me, me, send_sems, recv_sems) for cp in plan]
        for cp in own + sends:
            cp.start()
        for cp in plan:
            _descriptor(cp, ins, xyc, me, me ^ cp.mask, send_sems, recv_sems).wait_recv()
        for cp in sends:
            cp.wait_send()
        for cp in own:
            cp.wait()

    return pl.pallas_call(
        body, name=name, in_specs=[HBM_SPEC] * n, out_specs=[HBM_SPEC] * n,
        out_shape=[jax.ShapeDtypeStruct(b.shape, b.dtype) for b in bufs],
        input_output_aliases={i: i for i in range(n)},
        scratch_shapes=[pltpu.SemaphoreType.DMA((len(plan),)), pltpu.SemaphoreType.DMA((len(plan),)),
                        pltpu.SemaphoreType.DMA((max(len(local), 1),))],
    )(*bufs)


def _split_call(name, bufs, *, wait=None, wait_sems=None, start=None, local=(), after=None, token=False):
    n = len(bufs)
    n_wait = 2 if wait else 0
    n_after = 1 if after is not None else 0
    n_start = 2 if start else 0

    def body(*refs):
        ins = refs[:n]
        wsend, wrecv = refs[n:n + n_wait] if wait else (None, None)
        outs = refs[n + n_wait + n_after:]
        ssend, srecv = outs[:n_start] if start else (None, None)
        rest = outs[n_start + n:]
        xyc, me = _place()
        for cp in wait or ():
            d = _descriptor(cp, ins, xyc, me, me ^ cp.mask, wsend, wrecv)
            d.wait_send()
            d.wait_recv()
        own = [pltpu.make_async_copy(lc.src(ins[lc.sb], me), lc.dst(ins[lc.db], me), rest[-1].at[i])
               for i, lc in enumerate(local)]
        for cp in own:
            cp.start()
        for cp in start or ():
            _descriptor(cp, ins, xyc, me, me, ssend, srecv).start()
        for cp in own:
            cp.wait()
        if token:
            rest[0][...] = jnp.zeros_like(rest[0])

    args = [pltpu.with_memory_space_constraint(b, pltpu.HBM) for b in bufs]
    in_specs = [HBM_SPEC] * n
    if wait:
        args += list(wait_sems)
        in_specs += [SEM_SPEC] * 2
    if after is not None:
        args.append(after)
        in_specs.append(pl.BlockSpec(memory_space=pl.ANY))
    out_shape, out_specs = [], []
    if start:
        out_shape += [pltpu.SemaphoreType.DMA((len(start),))] * 2
        out_specs += [SEM_SPEC] * 2
    out_shape += [pltpu.HBM(b.shape, b.dtype) for b in bufs]
    out_specs += [HBM_SPEC] * n
    if token:
        out_shape.append(jax.ShapeDtypeStruct((8, 128), F32))
        out_specs.append(pl.BlockSpec(memory_space=pltpu.VMEM))
    outs = pl.pallas_call(
        body, name=name, in_specs=in_specs, out_specs=out_specs, out_shape=out_shape,
        input_output_aliases={i: n_start + i for i in range(n)},
        scratch_shapes=[pltpu.SemaphoreType.DMA((len(local),))] if local else [],
        compiler_params=pltpu.CompilerParams(has_side_effects=pltpu.SideEffectType.DATAFLOW_SIDE_EFFECTING),
    )(*args)
    sems = tuple(outs[:n_start]) if start else None
    return sems, list(outs[n_start:n_start + n]), (outs[n_start + n] if token else None)


def _adamw(w, g, m, v):
    m = ADAM_B1 * m + (1.0 - ADAM_B1) * g
    v = ADAM_B2 * v + (1.0 - ADAM_B2) * (g * g)
    m_hat = m / (1.0 - ADAM_B1 ** ADAM_STEP)
    v_hat = v / (1.0 - ADAM_B2 ** ADAM_STEP)
    delta = -ADAM_LR * (m_hat / (jnp.sqrt(v_hat) + ADAM_EPS) + ADAM_WD * w)
    return delta, m, v


def _reduce_adam(name, parts, w, m, v, *, tr):
    L, R, C = w.shape
    S = parts[0].shape[0]
    tr = min(tr, R)

    def body(*refs):
        p_refs = refs[:L]
        w_ref, m_ref, v_ref, g_ref, d_ref, nm_ref, nv_ref = refs[L:]
        for l in range(L):
            g = p_refs[l][0].astype(F32)
            for s in range(1, S):
                g = g + p_refs[l][s].astype(F32)
            g_ref[l] = g
            d_ref[l], nm_ref[l], nv_ref[l] = _adamw(w_ref[l], g, m_ref[l], v_ref[l])

    blk = pl.BlockSpec((L, tr, C), lambda r: (0, r, 0))
    out = jax.ShapeDtypeStruct((L, R, C), F32)
    return pl.pallas_call(
        body, name=name, grid=(R // tr,),
        in_specs=[pl.BlockSpec((S, tr, C), lambda r: (0, r, 0))] * L + [blk, blk, blk],
        out_specs=[blk] * 4, out_shape=[out] * 4,
        compiler_params=_params(dimension_semantics=("arbitrary",)),
    )(*parts, w, m, v)


def _small_reduce(parts):
    D = parts.shape[2]
    rows = [0, 1, 8, 9, 16, 17, 24, 25, 18, 19, 20, 2, 32]

    def body(p_ref, out_ref):
        s = p_ref[0]
        for d in range(1, N_DEV):
            s = s + p_ref[d]
        out_ref[...] = jnp.zeros_like(out_ref)
        for r, src in enumerate(rows):
            out_ref[r:r + 1, :] = s[src:src + 1, :]

    return pl.pallas_call(body, name="small_reduce", out_shape=jax.ShapeDtypeStruct((16, D), F32))(parts)


def _small_adam(g_gain, g_taps, g_scale, gains, taps, scale):
    def body(gg, gt, gs, wg, mg, vg, wt, mt, vt, ws, ms, vs, *outs):
        for k, (g, w, m, v) in enumerate(((gg, wg, mg, vg), (gt, wt, mt, vt), (gs, ws, ms, vs))):
            outs[3 * k][...], outs[3 * k + 1][...], outs[3 * k + 2][...] = _adamw(w[...], g[...], m[...], v[...])

    shapes = [jax.ShapeDtypeStruct(t[0].shape, F32) for t in (gains, taps, scale) for _ in range(3)]
    return pl.pallas_call(body, name="small_adam", out_shape=shapes)(g_gain, g_taps, g_scale, *gains, *taps, *scale)


def kernel(x, norm_gains, pool_w, pool_scale, conv_in_w, conv_w, conv_out_w, ffn_gate_up_w, ffn_down_w, loss_target, m_norm_gains, m_pool_w, m_pool_scale, m_conv_in_w, m_conv_w, m_conv_out_w, m_ffn_gate_up_w, m_ffn_down_w, v_norm_gains, v_pool_w, v_pool_scale, v_conv_in_w, v_conv_w, v_conv_out_w, v_ffn_gate_up_w, v_ffn_down_w):
    T, D = x.shape[1], x.shape[2]
    tm = min(512, T)
    tm_b = min(256, T)
    tk = min(2048, T)
    n_layers = ffn_gate_up_w.shape[0]
    fb = ffn_gate_up_w.shape[2]
    fr = ffn_down_w.shape[1]
    dcol = norm_gains.shape[2]
    cb = conv_in_w.shape[2]
    gw = pool_w.shape[3]
    me = 4 * lax.axis_index("x") + 2 * lax.axis_index("y") + lax.axis_index("c")

    small_w = jnp.concatenate([norm_gains.reshape(8, dcol), jnp.pad(conv_w[0], ((0, 5), (0, 0)))], axis=0)
    every = range(1, N_DEV)
    wgu_t, m_wgu_t, v_wgu_t = (jnp.swapaxes(a, 1, 2) for a in (ffn_gate_up_w, m_ffn_gate_up_w, v_ffn_gate_up_w))
    pw_shard = pool_w[0].astype(BF16)
    shards = [w.astype(BF16) for w in (wgu_t[0], ffn_down_w[0], conv_in_w[0], conv_out_w[0], wgu_t[1], ffn_down_w[1])]
    n_big = len(shards)
    own_lead = lambda s: lax.dynamic_update_slice(lax.empty((N_DEV,) + s.shape, s.dtype), s[None], (me,) + (0,) * s.ndim)
    pw_land = lax.dynamic_update_slice(lax.empty((4, N_DEV, gw // N_DEV, gw), BF16), pw_shard[:, None], (0, me, 0, 0))
    first = [pw_shard, small_w, pw_land, own_lead(small_w)]
    shard_at, land_at = len(first), len(first) + n_big
    direct = ([Copy(m, 0, _whole, 2, _second, m - 1) for m in every]
              + [Copy(m, 1, _whole, 3, _lead, N_DEV - 2 + m) for m in every])
    level1 = [Copy(mask, shard_at + n, _whole, land_at + n, _lead, len(direct) + 4 * n + j)
              for n in range(n_big) for j, mask in enumerate((SIBLING,) + OTHER_CHIPS)]
    sems1, bufs1, _ = _split_call("gather_start", first + shards + [own_lead(s) for s in shards], start=direct + level1)
    _, _, pw_g, small_g = _split_call("gather_small_done", bufs1[:len(first)], wait=direct, wait_sems=sems1)[1]
    pw = pw_g.reshape(4, gw, gw)
    small_full = jnp.swapaxes(small_g, 0, 1).reshape(16, D)
    gain = lambda l, s: small_full[4 * l + s][None, :]
    taps = small_full[8:16]

    def forward_on(name, group, after):
        k = len(group)
        landed = [Copy(cp.mask, i, cp.src, k + i, cp.dst, cp.sem)
                  for i, n in enumerate(group) for cp in level1 if cp.sb == shard_at + n]
        onward = [Copy(SIBLING, i, (lambda ref, me, m=m: ref.at[me ^ m]), i, (lambda ref, sender, m=m: ref.at[sender ^ m]), 3 * i + j)
                  for i in range(k) for j, m in enumerate(OTHER_CHIPS)]
        sems2, bufs2, _ = _split_call(
            name + "_forward", [bufs1[shard_at + n] for n in group] + [bufs1[land_at + n] for n in group], wait=landed,
            wait_sems=sems1, start=[cp._replace(sb=k + cp.sb, db=k + cp.db) for cp in onward], after=after)
        return name, onward, sems2, bufs2[k:]

    def arrived(state):
        name, onward, sems2, lands2 = state
        return _split_call(name + "_done", lands2, wait=onward, wait_sems=sems2)[1]

    h0 = x[0]
    h1 = _pool_fwd(h0, pw, pool_scale, gain(0, 0), gain(0, 1), tm=tm)
    wgu0, wd0 = arrived(forward_on("gather_ffn0", [0, 1], h1))
    wd0 = wd0.reshape(N_DEV * fr, D)
    h2, gu0, ff0 = _ffn_fwd(h1, gain(0, 2), gain(0, 3), wgu0, wd0, 0, None, tm=tm)
    win_g, wout_g = arrived(forward_on("gather_conv", [2, 3], h2))
    wout = wout_g.reshape(D, D)
    h3, proj, y = _conv_fwd(h2, gain(1, 0), gain(1, 1), win_g, taps, wout, tm=tm)
    wgu1, wd1 = arrived(forward_on("gather_ffn1", [4, 5], h3))
    wd1 = wd1.reshape(N_DEV * fr, D)
    dh4, gu1, ff1, loss_part = _ffn_fwd(h3, gain(1, 2), gain(1, 3), wgu1, wd1, 1, loss_target[0], tm=tm)

    chip = me >> 1

    def scatter_start(name, sums):
        k = len(sums)
        lands = [lax.dynamic_update_slice(lax.empty(s.shape, BF16), lax.dynamic_index_in_dim(s, chip, 0), (chip, 0, 0))
                 for s in sums]
        plan = [Copy(m, n, (lambda ref, i, m=m: ref.at[(i ^ m) >> 1]), k + n, (lambda ref, i: ref.at[i >> 1]), 3 * n + j)
                for n in range(k) for j, m in enumerate(OTHER_CHIPS)]
        sems, bufs, tok = _split_call(name + "_start", sums + lands, start=plan, token=True)
        return (name, plan, sems, bufs), tok

    def scatter_done(state, after):
        name, plan, sems, bufs = state
        return _split_call(name + "_done", bufs, wait=plan, wait_sems=sems, after=after)[1][len(bufs) // 2:]

    seq = lambda i, k: (k, 0)
    by_block = pl.BlockSpec((None, tk, fb), lambda i, k: (i, k, 0))
    rows = pl.BlockSpec((tk, D), seq)
    dh3, dgu1, dff1, c1, act1, small_f1 = _ffn_bwd(dh4, h3, ff1, gu1, gain(1, 2), gain(1, 3), wgu1, wd1, 1, tm=tm_b)
    g_wgu1 = _wgrad("wgrad_gate_up_1", dgu1, c1, by_block, rows, (fb, D), N_DEV, tk=tk)
    g_wd1 = _wgrad("wgrad_down_1", act1, dff1, by_block, rows, (fb, D), N_DEV // 2, tk=tk)
    rs_ffn1, tok = scatter_start("scatter_ffn1", [g_wgu1, g_wd1])
    dh2, dproj, z, a1, dy, small_c = _conv_bwd(dh3, h2, y, proj, gain(1, 0), gain(1, 1), win_g, taps, wout, tm=tm, after=tok)
    g_win = _wgrad("wgrad_conv_in", dproj, a1, pl.BlockSpec((tk, cb), lambda i, k: (k, i)), rows, (D, cb), N_DEV,
                   tk=tk, transpose=True)
    g_wout = _wgrad("wgrad_conv_out", z, dy, rows, rows, (D, D), 1, tk=tk)
    rs_conv, tok = scatter_start("scatter_conv", [g_win, g_wout])
    dh1, dgu0, dff0, c0, act0, small_f0 = _ffn_bwd(dh2, h1, ff0, gu0, gain(0, 2), gain(0, 3), wgu0, wd0, 0, tm=tm_b, after=tok)
    g_wgu0 = _wgrad("wgrad_gate_up_0", dgu0, c0, by_block, rows, (fb, D), N_DEV, tk=tk)
    rs_wgu0, tok = scatter_start("scatter_gate_up_0", [g_wgu0])
    g_wd0 = _wgrad("wgrad_down_0", act0, dff0, by_block, rows, (fb, D), N_DEV // 2, tk=tk, after=tok)
    rs_wd0, tok = scatter_start("scatter_down_0", [g_wd0])
    grad_x, g_pw, small_p = _pool_bwd(dh1, h0, pw, pool_scale, gain(0, 0), gain(0, 1), tm=tm, after=tok)

    loss_rows = jnp.broadcast_to(loss_part[0:1, 0:1], (8, D))
    small_part = jnp.concatenate([small_p, small_f0, small_c, small_f1, loss_rows], axis=0)
    g_pw = g_pw.reshape(4, N_DEV, gw // N_DEV, gw)
    r_pw, r_small = _exchange(
        "scatter_small", [g_pw, small_part, lax.empty(g_pw.shape, BF16), lax.empty((N_DEV,) + small_part.shape, F32)],
        [Copy(m, 0, (lambda ref, i, m=m: ref.at[:, i ^ m]), 2, _second, m - 1) for m in every]
        + [Copy(m, 1, _whole, 3, _lead, N_DEV - 2 + m) for m in every],
        [Local(0, _second, 2, _second), Local(1, _whole, 3, _lead)])[2:]
    (r_wgu1, r_wd1), (r_win, r_wout), (r_wgu0,), (r_wd0,) = (
        scatter_done(state, r_small) for state in (rs_ffn1, rs_conv, rs_wgu0, rs_wd0))

    o_pw = _reduce_adam("adam_pool_w", [r_pw[g] for g in range(4)], pool_w[0], m_pool_w[0], v_pool_w[0], tr=32)
    o_win = _reduce_adam("adam_conv_in", [r_win], conv_in_w, m_conv_in_w, v_conv_in_w, tr=256)
    o_wout = _reduce_adam("adam_conv_out", [r_wout], conv_out_w, m_conv_out_w, v_conv_out_w, tr=128)
    o_wgu = [jnp.swapaxes(o, 1, 2) for o in _reduce_adam("adam_gate_up", [r_wgu0, r_wgu1], wgu_t, m_wgu_t, v_wgu_t, tr=176)]
    o_wd = _reduce_adam("adam_down", [r_wd0, r_wd1], ffn_down_w, m_ffn_down_w, v_ffn_down_w, tr=176)
    g_small = _small_reduce(r_small)
    loss = g_small[12, 0]
    g_cols = lax.dynamic_slice(g_small, (0, me * dcol), (16, dcol))
    o_small = _small_adam(
        g_cols[0:8], g_cols[8:11], g_small[11:12],
        (norm_gains.reshape(8, dcol), m_norm_gains.reshape(8, dcol), v_norm_gains.reshape(8, dcol)),
        (conv_w[0], m_conv_w[0], v_conv_w[0]), (pool_scale, m_pool_scale, v_pool_scale))
    d_gain, nm_gain, nv_gain, d_taps, nm_taps, nv_taps, d_scale, nm_scale, nv_scale = o_small

    gshape = norm_gains.shape
    per = lambda k: (
        (g_cols[0:8].reshape(gshape), d_gain.reshape(gshape), nm_gain.reshape(gshape), nv_gain.reshape(gshape))[k],
        o_pw[k][None], (g_small[11:12], d_scale, nm_scale, nv_scale)[k], o_win[k],
        (g_cols[8:11][None], d_taps[None], nm_taps[None], nv_taps[None])[k], o_wout[k], o_wgu[k], o_wd[k])
    return (loss, grad_x[None], *per(0), *per(1), *per(2), *per(3))
```

```python
import collections
import functools

import jax
import jax.numpy as jnp
from jax import lax
from jax.experimental import pallas as pl
from jax.experimental.pallas import tpu as pltpu

N_DEV = 8
RMS_EPS = 1e-6
POOL_WINDOWS = (2, 4, 8, 16)
POOL_HALO = 16
CONV_HALO = 16
ADAM_LR, ADAM_B1, ADAM_B2, ADAM_EPS, ADAM_WD, ADAM_STEP = 0.001, 0.9, 0.999, 1e-08, 0.01, 10

VMEM_LIMIT = 56 * 2**20
BF16 = jnp.bfloat16
F32 = jnp.float32
MESH = pl.DeviceIdType.MESH


def _params(**kw):
    return pltpu.CompilerParams(vmem_limit_bytes=VMEM_LIMIT, **kw)


def _resident(shape, index_map):
    return pl.BlockSpec(shape, index_map, pipeline_mode=pl.Buffered(1))


def _ordered(body, n_in, after):
    if after is None:
        return functools.partial(body), [], []
    return (lambda *refs: body(*refs[:n_in], *refs[n_in + 1:])), [after], [pl.BlockSpec(memory_space=pl.ANY)]


def _rms_fwd(x, g):
    r = lax.rsqrt(jnp.mean(x * x, axis=-1, keepdims=True) + RMS_EPS)
    return x * r * g


def _rms_bwd(x, g, dy):
    r = lax.rsqrt(jnp.mean(x * x, axis=-1, keepdims=True) + RMS_EPS)
    xhat = x * r
    dg = jnp.sum(dy * xhat, axis=0, keepdims=True)
    t = dy * g
    dx = r * (t - xhat * jnp.mean(t * xhat, axis=-1, keepdims=True))
    return dx, dg


def _sigmoid(x):
    return 0.5 * jnp.tanh(0.5 * x) + 0.5


def _dot(a, b):
    return jnp.dot(a, b, preferred_element_type=F32)


def _dot_nt(a, b):
    return lax.dot_general(a, b, (((1,), (1,)), ((), ())), preferred_element_type=F32)


def _dot_tn(a, b):
    return lax.dot_general(a, b, (((0,), (0,)), ((), ())), preferred_element_type=F32)


def _row_inverse_counts(tile, tm):
    pos = (lax.broadcasted_iota(jnp.int32, (tm, 1), 0) + tile * tm + 1).astype(F32)
    return [1.0 / jnp.minimum(pos, float(w)) for w in POOL_WINDOWS]


def _pool_from_ext(ext, a, invs, gw):
    s = ext
    outs = []
    for g, w in enumerate(POOL_WINDOWS):
        s = s[:, (gw if g else 0):]
        s = s + pltpu.roll(s, w // 2, 0)
        outs.append(s[POOL_HALO:, :gw] * invs[g] - a[:, g * gw:(g + 1) * gw])
    return outs


def _pool_fwd(h, pw, scale, g_pre, g_post, *, tm, after=None):
    T, D = h.shape
    gw = D // len(POOL_WINDOWS)
    hb = tm // POOL_HALO

    def body(h_ref, halo_ref, pw_ref, scale_ref, gpre_ref, gpost_ref, out_ref, ext_ref):
        i = pl.program_id(0)
        x = h_ref[...]
        a = _rms_fwd(x, gpre_ref[...])
        ah = _rms_fwd(halo_ref[...], gpre_ref[...])
        ext_ref[0:POOL_HALO, :] = jnp.where(i == 0, 0.0, ah)
        ext_ref[POOL_HALO:, :] = a
        pooled = _pool_from_ext(ext_ref[...], a, _row_inverse_counts(i, tm), gw)
        mixed = jnp.concatenate([_dot(p.astype(BF16), pw_ref[g]) for g, p in enumerate(pooled)], axis=1)
        out_ref[...] = x + _rms_fwd(mixed * scale_ref[...], gpost_ref[...])

    vec = _resident((1, D), lambda i: (0, 0))
    fn, xa, xs = _ordered(body, 6, after)
    return pl.pallas_call(
        fn, name="pool_fwd", grid=(T // tm,),
        in_specs=[pl.BlockSpec((tm, D), lambda i: (i, 0)),
                  pl.BlockSpec((POOL_HALO, D), lambda i: (jnp.maximum(i * hb - 1, 0), 0)),
                  _resident(pw.shape, lambda i: (0, 0, 0)), vec, vec, vec] + xs,
        out_specs=pl.BlockSpec((tm, D), lambda i: (i, 0)),
        out_shape=jax.ShapeDtypeStruct((T, D), F32),
        scratch_shapes=[pltpu.VMEM((tm + POOL_HALO, D), F32)],
        compiler_params=_params(dimension_semantics=("arbitrary",)),
    )(h, h, pw, scale, g_pre, g_post, *xa)


def _pool_bwd(dh, h, pw, scale, g_pre, g_post, *, tm, after=None):
    T, D = h.shape
    gw = D // len(POOL_WINDOWS)
    hb = tm // POOL_HALO
    nt = T // tm
    n_ext = tm + POOL_HALO

    def body(dh_ref, h_ref, halo_ref, pw_ref, scale_ref, gpre_ref, gpost_ref,
             dx_ref, dpw_ref, small_ref, ext_ref, ext2_ref, carry_ref, dpw_acc):
        i = pl.program_id(0)
        tile = nt - 1 - i

        @pl.when(i == 0)
        def _():
            small_ref[...] = jnp.zeros_like(small_ref)
            dpw_acc[...] = jnp.zeros_like(dpw_acc)
            carry_ref[...] = jnp.zeros_like(carry_ref)

        x = h_ref[...]
        dout = dh_ref[...]
        a = _rms_fwd(x, gpre_ref[...])
        ah = _rms_fwd(halo_ref[...], gpre_ref[...])
        ext_ref[0:POOL_HALO, :] = jnp.where(tile == 0, 0.0, ah)
        ext_ref[POOL_HALO:, :] = a
        invs = _row_inverse_counts(tile, tm)
        pooled = [p.astype(BF16) for p in _pool_from_ext(ext_ref[...], a, invs, gw)]
        mixed_pre = jnp.concatenate([_dot(p, pw_ref[g]) for g, p in enumerate(pooled)], axis=1)
        scale_v = scale_ref[...]
        dmixed, dg_post = _rms_bwd(mixed_pre * scale_v, gpost_ref[...], dout)
        small_ref[1:2, :] += dg_post
        small_ref[2:3, :] += jnp.sum(dmixed * mixed_pre, axis=0, keepdims=True)
        dpre = (dmixed * scale_v).astype(BF16)
        dpooled = []
        for g in range(len(POOL_WINDOWS)):
            dp = dpre[:, g * gw:(g + 1) * gw]
            dpw_acc[g] += _dot_tn(pooled[g], dp)
            dpooled.append(_dot_nt(dp, pw_ref[g]))
        q = jnp.concatenate([d * invs[g] for g, d in enumerate(dpooled)], axis=1)
        ext2_ref[0:tm, :] = q
        ext2_ref[tm:, :] = carry_ref[...]
        carry_ref[...] = q[0:POOL_HALO, :]
        s = ext2_ref[...]
        da = []
        for g, w in enumerate(POOL_WINDOWS):
            s = s[:, (gw if g else 0):]
            s = s + pltpu.roll(s, n_ext - w // 2, 0)
            da.append(s[0:tm, :gw] - dpooled[g])
        dx, dg_pre = _rms_bwd(x, gpre_ref[...], jnp.concatenate(da, axis=1))
        small_ref[0:1, :] += dg_pre
        dx_ref[...] = dout + dx

        @pl.when(i == nt - 1)
        def _():
            dpw_ref[...] = dpw_acc[...].astype(BF16)

    vec = _resident((1, D), lambda i: (0, 0))
    rev = lambda i: (nt - 1 - i, 0)
    fn, xa, xs = _ordered(body, 7, after)
    return pl.pallas_call(
        fn, name="pool_bwd", grid=(nt,),
        in_specs=[pl.BlockSpec((tm, D), rev), pl.BlockSpec((tm, D), rev),
                  pl.BlockSpec((POOL_HALO, D), lambda i: (jnp.maximum((nt - 1 - i) * hb - 1, 0), 0)),
                  _resident(pw.shape, lambda i: (0, 0, 0)), vec, vec, vec] + xs,
        out_specs=[pl.BlockSpec((tm, D), rev),
                   pl.BlockSpec(pw.shape, lambda i: (0, 0, 0)),
                   pl.BlockSpec((8, D), lambda i: (0, 0))],
        out_shape=[jax.ShapeDtypeStruct((T, D), F32), jax.ShapeDtypeStruct(pw.shape, BF16),
                   jax.ShapeDtypeStruct((8, D), F32)],
        scratch_shapes=[pltpu.VMEM((n_ext, D), F32), pltpu.VMEM((n_ext, D), F32),
                        pltpu.VMEM((POOL_HALO, D), F32), pltpu.VMEM(pw.shape, F32)],
        compiler_params=_params(dimension_semantics=("arbitrary",)),
    )(dh, h, h, pw, scale, g_pre, g_post, *xa)


def _ffn_fwd(h, g_pre, g_post, wgu, wd, layer, target, *, tm, after=None):
    T, D = h.shape
    nblk, fb = wgu.shape[0], wgu.shape[1]
    half = nblk // 2
    last = target is not None

    def body(*refs):
        if last:
            h_ref, gpre_ref, gpost_ref, wgu_ref, wd_ref, tgt_ref, out_ref, gu_ref, ff_ref, loss_ref = refs
        else:
            h_ref, gpre_ref, gpost_ref, wgu_ref, wd_ref, out_ref, gu_ref, ff_ref = refs
        x = h_ref[...]
        cb = _rms_fwd(x, gpre_ref[...]).astype(BF16)
        acc = jnp.zeros((tm, D), F32)
        for j in range(half):
            g = _dot_nt(cb, wgu_ref[j])
            u = _dot_nt(cb, wgu_ref[j + half])
            gu_ref[j] = g.astype(BF16)
            gu_ref[j + half] = u.astype(BF16)
            act = (g * _sigmoid(g) * u).astype(BF16)
            acc = acc + _dot(act, wd_ref[j * fb:(j + 1) * fb, :])
        ff_ref[...] = acc.astype(BF16)
        hout = x + _rms_fwd(acc, gpost_ref[...])
        if last:
            diff = hout - tgt_ref[...]
            out_ref[...] = diff * (1.0 / D)

            @pl.when(pl.program_id(0) == 0)
            def _():
                loss_ref[...] = jnp.zeros_like(loss_ref)

            loss_ref[...] += jnp.sum(diff * diff) * (0.5 / D)
        else:
            out_ref[...] = hout

    vec = _resident((1, D), lambda i: (0, 0))
    tile = pl.BlockSpec((tm, D), lambda i: (i, 0))
    in_specs = [tile, vec, vec,
                _resident(wgu.shape, lambda i: (0, 0, 0)), _resident(wd.shape, lambda i: (0, 0))]
    out_specs = [tile, pl.BlockSpec((nblk, tm, fb), lambda i: (0, i, 0)), tile]
    out_shape = [jax.ShapeDtypeStruct((T, D), F32), jax.ShapeDtypeStruct((nblk, T, fb), BF16),
                 jax.ShapeDtypeStruct((T, D), BF16)]
    args = [h, g_pre, g_post, wgu, wd]
    if last:
        in_specs.append(tile)
        args.append(target)
        out_specs.append(pl.BlockSpec((8, 128), lambda i: (0, 0)))
        out_shape.append(jax.ShapeDtypeStruct((8, 128), F32))
    fn, xa, xs = _ordered(body, len(args), after)
    return pl.pallas_call(
        fn, name=f"ffn_fwd_{layer}", grid=(T // tm,), in_specs=in_specs + xs, out_specs=out_specs,
        out_shape=out_shape, compiler_params=_params(dimension_semantics=("arbitrary",)),
    )(*args, *xa)


def _ffn_up(h, g_pre, wgu, *, tm):
    T, D = h.shape
    nblk, fb = wgu.shape[0], wgu.shape[1]
    half = nblk // 2

    def body(h_ref, gpre_ref, wgu_ref, gu_ref, act_ref):
        cb = _rms_fwd(h_ref[...], gpre_ref[...]).astype(BF16)
        for j in range(half):
            g = _dot_nt(cb, wgu_ref[j])
            u = _dot_nt(cb, wgu_ref[j + half])
            gu_ref[j] = g.astype(BF16)
            gu_ref[j + half] = u.astype(BF16)
            act_ref[j] = (g * _sigmoid(g) * u).astype(BF16)

    return pl.pallas_call(
        body, name="ffn_up_0", grid=(T // tm,),
        in_specs=[pl.BlockSpec((tm, D), lambda i: (i, 0)), _resident((1, D), lambda i: (0, 0)),
                  _resident(wgu.shape, lambda i: (0, 0, 0))],
        out_specs=[pl.BlockSpec((nblk, tm, fb), lambda i: (0, i, 0)), pl.BlockSpec((half, tm, fb), lambda i: (0, i, 0))],
        out_shape=[jax.ShapeDtypeStruct((nblk, T, fb), BF16), jax.ShapeDtypeStruct((half, T, fb), BF16)],
        compiler_params=_params(dimension_semantics=("arbitrary",)),
    )(h, g_pre, wgu)


def _ffn_down(h, act, g_post, wd, *, tm):
    T, D = h.shape
    half, fb = act.shape[0], act.shape[2]

    def body(h_ref, act_ref, gpost_ref, wd_ref, out_ref, ff_ref):
        acc = _dot(act_ref[0], wd_ref[0:fb, :])
        for j in range(1, half):
            acc = acc + _dot(act_ref[j], wd_ref[j * fb:(j + 1) * fb, :])
        ff_ref[...] = acc.astype(BF16)
        out_ref[...] = h_ref[...] + _rms_fwd(acc, gpost_ref[...])

    tile = pl.BlockSpec((tm, D), lambda i: (i, 0))
    return pl.pallas_call(
        body, name="ffn_down_0", grid=(T // tm,),
        in_specs=[tile, pl.BlockSpec((half, tm, fb), lambda i: (0, i, 0)), _resident((1, D), lambda i: (0, 0)),
                  _resident(wd.shape, lambda i: (0, 0))],
        out_specs=[tile, tile],
        out_shape=[jax.ShapeDtypeStruct((T, D), F32), jax.ShapeDtypeStruct((T, D), BF16)],
        compiler_params=_params(dimension_semantics=("arbitrary",)),
    )(h, act, g_post, wd)


def _ffn_bwd(dh, h, ff, gu, g_pre, g_post, wgu, wd, layer, *, tm, after=None):
    T, D = h.shape
    nblk, fb = wgu.shape[0], wgu.shape[1]
    half = nblk // 2

    def body(dh_ref, h_ref, ff_ref, gu_ref, gpre_ref, gpost_ref, wgu_ref, wd_ref,
             dx_ref, dgu_ref, dff_ref, c_ref, act_ref, small_ref):
        @pl.when(pl.program_id(0) == 0)
        def _():
            small_ref[...] = jnp.zeros_like(small_ref)

        dout = dh_ref[...]
        dff, dg_post = _rms_bwd(ff_ref[...].astype(F32), gpost_ref[...], dout)
        small_ref[1:2, :] += dg_post
        dffb = dff.astype(BF16)
        dff_ref[...] = dffb
        dc = jnp.zeros((tm, D), F32)
        last = None
        for j in range(half + 1):
            if j < half:
                dact = _dot_nt(dffb, wd_ref[j * fb:(j + 1) * fb, :])
            if last is not None:
                dc = dc + _dot(last[0], wgu_ref[j - 1]) + _dot(last[1], wgu_ref[j - 1 + half])
            if j < half:
                g = gu_ref[j].astype(F32)
                u = gu_ref[j + half].astype(F32)
                s = _sigmoid(g)
                silu = g * s
                act_ref[j] = (silu * u).astype(BF16)
                dg = (dact * u * (s * (1.0 + g * (1.0 - s)))).astype(BF16)
                du = (dact * silu).astype(BF16)
                dgu_ref[j] = dg
                dgu_ref[j + half] = du
                last = (dg, du)
        x = h_ref[...]
        c_ref[...] = _rms_fwd(x, gpre_ref[...]).astype(BF16)
        dx, dg_pre = _rms_bwd(x, gpre_ref[...], dc)
        small_ref[0:1, :] += dg_pre
        dx_ref[...] = dout + dx

    vec = _resident((1, D), lambda i: (0, 0))
    tile = pl.BlockSpec((tm, D), lambda i: (i, 0))
    blk = pl.BlockSpec((nblk, tm, fb), lambda i: (0, i, 0))
    fn, xa, xs = _ordered(body, 8, after)
    return pl.pallas_call(
        fn, name=f"ffn_bwd_{layer}", grid=(T // tm,),
        in_specs=[tile, tile, tile, blk, vec, vec,
                  _resident(wgu.shape, lambda i: (0, 0, 0)), _resident(wd.shape, lambda i: (0, 0))] + xs,
        out_specs=[tile, blk, tile, tile, pl.BlockSpec((half, tm, fb), lambda i: (0, i, 0)),
                   pl.BlockSpec((8, D), lambda i: (0, 0))],
        out_shape=[jax.ShapeDtypeStruct((T, D), F32), jax.ShapeDtypeStruct((nblk, T, fb), BF16),
                   jax.ShapeDtypeStruct((T, D), BF16), jax.ShapeDtypeStruct((T, D), BF16),
                   jax.ShapeDtypeStruct((half, T, fb), BF16), jax.ShapeDtypeStruct((8, D), F32)],
        compiler_params=_params(dimension_semantics=("arbitrary",)),
    )(dh, h, ff, gu, g_pre, g_post, wgu, wd, *xa)


def _conv_fwd(h, g_pre, g_post, win, taps, wout, *, tm, after=None):
    T, D = h.shape
    nblk, cb = win.shape[0], win.shape[2]

    def body(h_ref, gpre_ref, gpost_ref, win_ref, taps_ref, wout_ref,
             out_ref, proj_ref, y_ref, proj_scr, ext_ref, carry_ref):
        i = pl.program_id(0)

        @pl.when(i == 0)
        def _():
            carry_ref[...] = jnp.zeros_like(carry_ref)

        x = h_ref[...]
        a = _rms_fwd(x, gpre_ref[...]).astype(BF16)
        for k in range(nblk):
            proj_scr[:, k * cb:(k + 1) * cb] = _dot(a, win_ref[k])
        proj_ref[...] = proj_scr[...].astype(BF16)
        u = proj_scr[:, D:2 * D] * proj_scr[:, 2 * D:3 * D]
        ext_ref[0:CONV_HALO, :] = carry_ref[...]
        ext_ref[CONV_HALO:, :] = u
        carry_ref[...] = u[tm - CONV_HALO:, :]
        e = ext_ref[...]
        conv = (taps_ref[2:3, :] * u + taps_ref[1:2, :] * pltpu.roll(e, 1, 0)[CONV_HALO:, :]
                + taps_ref[0:1, :] * pltpu.roll(e, 2, 0)[CONV_HALO:, :])
        z = (proj_scr[:, 0:D] * conv).astype(BF16)
        y = _dot(z, wout_ref[...])
        y_ref[...] = y.astype(BF16)
        out_ref[...] = x + _rms_fwd(y, gpost_ref[...])

    vec = _resident((1, D), lambda i: (0, 0))
    tile = pl.BlockSpec((tm, D), lambda i: (i, 0))
    fn, xa, xs = _ordered(body, 6, after)
    return pl.pallas_call(
        fn, name="conv_fwd", grid=(T // tm,),
        in_specs=[tile, vec, vec, _resident(win.shape, lambda i: (0, 0, 0)),
                  _resident(taps.shape, lambda i: (0, 0)), _resident(wout.shape, lambda i: (0, 0))] + xs,
        out_specs=[tile, pl.BlockSpec((tm, 3 * D), lambda i: (i, 0)), tile],
        out_shape=[jax.ShapeDtypeStruct((T, D), F32), jax.ShapeDtypeStruct((T, 3 * D), BF16),
                   jax.ShapeDtypeStruct((T, D), BF16)],
        scratch_shapes=[pltpu.VMEM((tm, 3 * D), F32), pltpu.VMEM((tm + CONV_HALO, D), F32),
                        pltpu.VMEM((CONV_HALO, D), F32)],
        compiler_params=_params(dimension_semantics=("arbitrary",)),
    )(h, g_pre, g_post, win, taps, wout, *xa)


def _conv_bwd(dh, h, y, proj, g_pre, g_post, win, taps, wout, *, tm, after=None):
    T, D = h.shape
    nblk, cb = win.shape[0], win.shape[2]
    nt = T // tm
    hb = tm // CONV_HALO
    n_ext = tm + CONV_HALO

    def body(dh_ref, h_ref, y_ref, proj_ref, halo_ref, gpre_ref, gpost_ref, win_ref, taps_ref, wout_ref,
             dx_ref, dproj_ref, z_ref, a_ref, dy_ref, small_ref, ext_ref, ext2_ref, carry_ref):
        i = pl.program_id(0)
        tile = nt - 1 - i

        @pl.when(i == 0)
        def _():
            small_ref[...] = jnp.zeros_like(small_ref)
            carry_ref[...] = jnp.zeros_like(carry_ref)

        dout = dh_ref[...]
        dy, dg_post = _rms_bwd(y_ref[...].astype(F32), gpost_ref[...], dout)
        small_ref[1:2, :] += dg_post
        dyb = dy.astype(BF16)
        dy_ref[...] = dyb
        dz = _dot_nt(dyb, wout_ref[...])
        bgate = proj_ref[:, 0:D].astype(F32)
        cgate = proj_ref[:, D:2 * D].astype(F32)
        v = proj_ref[:, 2 * D:3 * D].astype(F32)
        u = cgate * v
        uh = halo_ref[:, D:2 * D].astype(F32) * halo_ref[:, 2 * D:3 * D].astype(F32)
        ext_ref[0:CONV_HALO, :] = jnp.where(tile == 0, 0.0, uh)
        ext_ref[CONV_HALO:, :] = u
        e = ext_ref[...]
        u1 = pltpu.roll(e, 1, 0)[CONV_HALO:, :]
        u2 = pltpu.roll(e, 2, 0)[CONV_HALO:, :]
        t0, t1, t2 = taps_ref[0:1, :], taps_ref[1:2, :], taps_ref[2:3, :]
        conv = t2 * u + t1 * u1 + t0 * u2
        z_ref[...] = (bgate * conv).astype(BF16)
        dconv = dz * bgate
        small_ref[2:3, :] += jnp.sum(dconv * u2, axis=0, keepdims=True)
        small_ref[3:4, :] += jnp.sum(dconv * u1, axis=0, keepdims=True)
        small_ref[4:5, :] += jnp.sum(dconv * u, axis=0, keepdims=True)
        ext2_ref[0:tm, :] = dconv
        ext2_ref[tm:, :] = carry_ref[...]
        carry_ref[...] = dconv[0:CONV_HALO, :]
        e2 = ext2_ref[...]
        du = (t2 * dconv + t1 * pltpu.roll(e2, n_ext - 1, 0)[0:tm, :]
              + t0 * pltpu.roll(e2, n_ext - 2, 0)[0:tm, :])
        dproj_ref[:, 0:D] = (dz * conv).astype(BF16)
        dproj_ref[:, D:2 * D] = (du * v).astype(BF16)
        dproj_ref[:, 2 * D:3 * D] = (du * cgate).astype(BF16)
        da = jnp.zeros((tm, D), F32)
        for k in range(nblk):
            da = da + _dot_nt(dproj_ref[:, k * cb:(k + 1) * cb], win_ref[k])
        x = h_ref[...]
        a_ref[...] = _rms_fwd(x, gpre_ref[...]).astype(BF16)
        dx, dg_pre = _rms_bwd(x, gpre_ref[...], da)
        small_ref[0:1, :] += dg_pre
        dx_ref[...] = dout + dx

    vec = _resident((1, D), lambda i: (0, 0))
    rev = lambda i: (nt - 1 - i, 0)
    tile = pl.BlockSpec((tm, D), rev)
    wide = pl.BlockSpec((tm, 3 * D), rev)
    fn, xa, xs = _ordered(body, 10, after)
    return pl.pallas_call(
        fn, name="conv_bwd", grid=(nt,),
        in_specs=[tile, tile, tile, wide,
                  pl.BlockSpec((CONV_HALO, 3 * D), lambda i: (jnp.maximum((nt - 1 - i) * hb - 1, 0), 0)),
                  vec, vec, _resident(win.shape, lambda i: (0, 0, 0)),
                  _resident(taps.shape, lambda i: (0, 0)), _resident(wout.shape, lambda i: (0, 0))] + xs,
        out_specs=[tile, wide, tile, tile, tile, pl.BlockSpec((8, D), lambda i: (0, 0))],
        out_shape=[jax.ShapeDtypeStruct((T, D), F32), jax.ShapeDtypeStruct((T, 3 * D), BF16),
                   jax.ShapeDtypeStruct((T, D), BF16), jax.ShapeDtypeStruct((T, D), BF16),
                   jax.ShapeDtypeStruct((T, D), BF16), jax.ShapeDtypeStruct((8, D), F32)],
        scratch_shapes=[pltpu.VMEM((n_ext, D), F32), pltpu.VMEM((n_ext, D), F32),
                        pltpu.VMEM((CONV_HALO, D), F32)],
        compiler_params=_params(dimension_semantics=("arbitrary",)),
    )(dh, h, y, proj, proj, g_pre, g_post, win, taps, wout, *xa)


def _wgrad(name, a, b, a_spec, b_spec, block, n_blocks, *, tk, transpose=False, after=None):
    T = a.shape[-2]
    nk = T // tk
    M, N = block
    m = N_DEV // n_blocks
    R = M // m
    acc_block = block[::-1] if transpose else block

    def body(a_ref, b_ref, out_ref, acc_ref, stage_ref, recv_ref, send_sems, recv_sems):
        i, k = pl.program_id(0), pl.program_id(1)
        x, y, c = lax.axis_index("x"), lax.axis_index("y"), lax.axis_index("c")

        def sent(blk, p):
            owner = blk * m + p
            q = owner // 2
            return (owner % 2) != c, pltpu.make_async_remote_copy(
                src_ref=stage_ref.at[p * R:(p + 1) * R], dst_ref=recv_ref.at[q], send_sem=send_sems.at[q],
                recv_sem=recv_sems.at[q], device_id=(x, y, 1 - c), device_id_type=MESH)

        @pl.when(k == 0)
        def _():
            acc_ref[...] = jnp.zeros_like(acc_ref)

        acc_ref[...] += _dot_tn(a_ref[...], b_ref[...])

        @pl.when(k == nk - 1)
        def _():
            for p in range(m):
                away, copy = sent(jnp.maximum(i - 1, 0), p)

                @pl.when(jnp.logical_and(i > 0, away))
                def _():
                    copy.wait_send()

            acc = acc_ref[...]
            stage_ref[...] = (acc.T if transpose else acc).astype(BF16)
            for p in range(m):
                away, copy = sent(i, p)

                @pl.when(away)
                def _():
                    copy.start()

                @pl.when(jnp.logical_not(away))
                def _():
                    out_ref[(i * m + p) // 2] = stage_ref[p * R:(p + 1) * R, :]

        @pl.when(jnp.logical_and(i == n_blocks - 1, k == nk - 1))
        def _():
            for p in range(m):
                away, copy = sent(i, p)

                @pl.when(away)
                def _():
                    copy.wait_send()

            for q in range(N_DEV // 2):
                pltpu.make_async_remote_copy(
                    src_ref=stage_ref.at[0:R], dst_ref=recv_ref.at[q], send_sem=send_sems.at[q],
                    recv_sem=recv_sems.at[q], device_id=(x, y, 1 - c), device_id_type=MESH).wait_recv()
                out_ref[q] = (out_ref[q].astype(F32) + recv_ref[q].astype(F32)).astype(BF16)

    fn, xa, xs = _ordered(body, 2, after)
    return pl.pallas_call(
        fn, name=name, grid=(n_blocks, nk), in_specs=[a_spec, b_spec] + xs,
        out_specs=pl.BlockSpec((N_DEV // 2, R, N), lambda i, k: (0, 0, 0)),
        out_shape=jax.ShapeDtypeStruct((N_DEV // 2, R, N), BF16),
        scratch_shapes=[pltpu.VMEM(acc_block, F32), pltpu.VMEM(block, BF16), pltpu.VMEM((N_DEV // 2, R, N), BF16),
                        pltpu.SemaphoreType.DMA((N_DEV // 2,)), pltpu.SemaphoreType.DMA((N_DEV // 2,))],
        compiler_params=_params(dimension_semantics=("arbitrary", "arbitrary")),
    )(a, b, *xa)


Copy = collections.namedtuple("Copy", "mask sb src db dst sem")
Local = collections.namedtuple("Local", "sb src db dst")

HBM_SPEC = pl.BlockSpec(memory_space=pltpu.HBM)
SEM_SPEC = pl.BlockSpec(memory_space=pltpu.SEMAPHORE)
SIBLING, X_PEER, Y_PEER, DIAGONAL = 1, 4, 2, 6
OTHER_CHIPS = (X_PEER, Y_PEER, DIAGONAL)


def _whole(ref, i):
    return ref


def _lead(ref, i):
    return ref.at[i]


def _second(ref, i):
    return ref.at[:, i]


def _place():
    x, y, c = lax.axis_index("x"), lax.axis_index("y"), lax.axis_index("c")
    return (x, y, c), 4 * x + 2 * y + c


def _descriptor(cp, bufs, xyc, me, sender, send_sems, recv_sems):
    x, y, c = xyc
    flip = lambda v, bit: (1 - v) if bit else v
    return pltpu.make_async_remote_copy(
        src_ref=cp.src(bufs[cp.sb], me), dst_ref=cp.dst(bufs[cp.db], sender),
        send_sem=send_sems.at[cp.sem], recv_sem=recv_sems.at[cp.sem],
        device_id=(flip(x, cp.mask & 4), flip(y, cp.mask & 2), flip(c, cp.mask & 1)), device_id_type=MESH)


def _exchange(name, bufs, plan, local=()):
    n = len(bufs)

    def body(*refs):
        ins = refs[:n]
        send_sems, recv_sems, local_sems = refs[2 * n:]
        xyc, me = _place()
        own = [pltpu.make_async_copy(lc.src(ins[lc.sb], me), lc.dst(ins[lc.db], me), local_sems.at[i])
               for i, lc in enumerate(local)]
        sends = [_descriptor(cp, ins, xyc, me, me, send_sems, recv_sems) for cp in plan]
        for cp in own + sends:
            cp.start()
        for cp in plan:
            _descriptor(cp, ins, xyc, me, me ^ cp.mask, send_sems, recv_sems).wait_recv()
        for cp in sends:
            cp.wait_send()
        for cp in own:
            cp.wait()

    return pl.pallas_call(
        body, name=name, in_specs=[HBM_SPEC] * n, out_specs=[HBM_SPEC] * n,
        out_shape=[jax.ShapeDtypeStruct(b.shape, b.dtype) for b in bufs],
        input_output_aliases={i: i for i in range(n)},
        scratch_shapes=[pltpu.SemaphoreType.DMA((len(plan),)), pltpu.SemaphoreType.DMA((len(plan),)),
                        pltpu.SemaphoreType.DMA((max(len(local), 1),))],
    )(*bufs)


def _split_call(name, bufs, *, wait=None, wait_sems=None, start=None, local=(), after=None, token=False):
    n = len(bufs)
    n_wait = 2 if wait else 0
    n_after = 1 if after is not None else 0
    n_start = 2 if start else 0

    def body(*refs):
        ins = refs[:n]
        wsend, wrecv = refs[n:n + n_wait] if wait else (None, None)
        outs = refs[n + n_wait + n_after:]
        ssend, srecv = outs[:n_start] if start else (None, None)
        rest = outs[n_start + n:]
        xyc, me = _place()
        for cp in wait or ():
            d = _descriptor(cp, ins, xyc, me, me ^ cp.mask, wsend, wrecv)
            d.wait_send()
            d.wait_recv()
        own = [pltpu.make_async_copy(lc.src(ins[lc.sb], me), lc.dst(ins[lc.db], me), rest[-1].at[i])
               for i, lc in enumerate(local)]
        for cp in own:
            cp.start()
        for cp in start or ():
            _descriptor(cp, ins, xyc, me, me, ssend, srecv).start()
        for cp in own:
            cp.wait()
        if token:
            rest[0][...] = jnp.zeros_like(rest[0])

    args = [pltpu.with_memory_space_constraint(b, pltpu.HBM) for b in bufs]
    in_specs = [HBM_SPEC] * n
    if wait:
        args += list(wait_sems)
        in_specs += [SEM_SPEC] * 2
    if after is not None:
        args.append(after)
        in_specs.append(pl.BlockSpec(memory_space=pl.ANY))
    out_shape, out_specs = [], []
    if start:
        out_shape += [pltpu.SemaphoreType.DMA((len(start),))] * 2
        out_specs += [SEM_SPEC] * 2
    out_shape += [pltpu.HBM(b.shape, b.dtype) for b in bufs]
    out_specs += [HBM_SPEC] * n
    if token:
        out_shape.append(jax.ShapeDtypeStruct((8, 128), F32))
        out_specs.append(pl.BlockSpec(memory_space=pltpu.VMEM))
    outs = pl.pallas_call(
        body, name=name, in_specs=in_specs, out_specs=out_specs, out_shape=out_shape,
        input_output_aliases={i: n_start + i for i in range(n)},
        scratch_shapes=[pltpu.SemaphoreType.DMA((len(local),))] if local else [],
        compiler_params=pltpu.CompilerParams(has_side_effects=pltpu.SideEffectType.DATAFLOW_SIDE_EFFECTING),
    )(*args)
    sems = tuple(outs[:n_start]) if start else None
    return sems, list(outs[n_start:n_start + n]), (outs[n_start + n] if token else None)


def _adamw(w, g, m, v):
    m = ADAM_B1 * m + (1.0 - ADAM_B1) * g
    v = ADAM_B2 * v + (1.0 - ADAM_B2) * (g * g)
    m_hat = m / (1.0 - ADAM_B1 ** ADAM_STEP)
    v_hat = v / (1.0 - ADAM_B2 ** ADAM_STEP)
    delta = -ADAM_LR * (m_hat / (jnp.sqrt(v_hat) + ADAM_EPS) + ADAM_WD * w)
    return delta, m, v


def _reduce_adam(name, parts, w, m, v, *, tr, layer=None, into=None, after=None):
    L, R, C = w.shape
    S = parts[0].shape[0]
    tr = min(tr, R)
    n_l = L if layer is None else 1
    first = 0 if layer is None else layer

    def body(*refs):
        p_refs = refs[:n_l]
        w_ref, m_ref, v_ref = refs[n_l:n_l + 3]
        g_ref, d_ref, nm_ref, nv_ref = refs[-4:]
        for l in range(n_l):
            g = p_refs[l][0].astype(F32)
            for s in range(1, S):
                g = g + p_refs[l][s].astype(F32)
            g_ref[l] = g
            d_ref[l], nm_ref[l], nv_ref[l] = _adamw(w_ref[l], g, m_ref[l], v_ref[l])

    blk = pl.BlockSpec((n_l, tr, C), lambda r: (first, r, 0))
    out = jax.ShapeDtypeStruct((L, R, C), F32)
    extra = list(into or []) + ([after] if after is not None else [])
    return pl.pallas_call(
        body, name=name, grid=(R // tr,),
        in_specs=[pl.BlockSpec((S, tr, C), lambda r: (0, r, 0))] * n_l + [blk, blk, blk]
        + [pl.BlockSpec(memory_space=pl.ANY)] * len(extra),
        out_specs=[blk] * 4, out_shape=[out] * 4,
        input_output_aliases={n_l + 3 + i: i for i in range(4)} if into else {},
        compiler_params=_params(dimension_semantics=("arbitrary",)),
    )(*parts, w, m, v, *extra)


def _small_reduce(parts):
    D = parts.shape[2]
    rows = [0, 1, 8, 9, 16, 17, 24, 25, 18, 19, 20, 2, 32]

    def body(p_ref, out_ref):
        s = p_ref[0]
        for d in range(1, N_DEV):
            s = s + p_ref[d]
        out_ref[...] = jnp.zeros_like(out_ref)
        for r, src in enumerate(rows):
            out_ref[r:r + 1, :] = s[src:src + 1, :]

    return pl.pallas_call(body, name="small_reduce", out_shape=jax.ShapeDtypeStruct((16, D), F32))(parts)


def _small_adam(g_gain, g_taps, g_scale, gains, taps, scale):
    def body(gg, gt, gs, wg, mg, vg, wt, mt, vt, ws, ms, vs, *outs):
        for k, (g, w, m, v) in enumerate(((gg, wg, mg, vg), (gt, wt, mt, vt), (gs, ws, ms, vs))):
            outs[3 * k][...], outs[3 * k + 1][...], outs[3 * k + 2][...] = _adamw(w[...], g[...], m[...], v[...])

    shapes = [jax.ShapeDtypeStruct(t[0].shape, F32) for t in (gains, taps, scale) for _ in range(3)]
    return pl.pallas_call(body, name="small_adam", out_shape=shapes)(g_gain, g_taps, g_scale, *gains, *taps, *scale)


def kernel(x, norm_gains, pool_w, pool_scale, conv_in_w, conv_w, conv_out_w, ffn_gate_up_w, ffn_down_w, loss_target, m_norm_gains, m_pool_w, m_pool_scale, m_conv_in_w, m_conv_w, m_conv_out_w, m_ffn_gate_up_w, m_ffn_down_w, v_norm_gains, v_pool_w, v_pool_scale, v_conv_in_w, v_conv_w, v_conv_out_w, v_ffn_gate_up_w, v_ffn_down_w):
    T, D = x.shape[1], x.shape[2]
    tm = min(512, T)
    tm_b = min(256, T)
    tk = min(2048, T)
    n_layers = ffn_gate_up_w.shape[0]
    fb = ffn_gate_up_w.shape[2]
    fr = ffn_down_w.shape[1]
    dcol = norm_gains.shape[2]
    cb = conv_in_w.shape[2]
    gw = pool_w.shape[3]
    me = 4 * lax.axis_index("x") + 2 * lax.axis_index("y") + lax.axis_index("c")

    small_w = jnp.concatenate([norm_gains.reshape(8, dcol), jnp.pad(conv_w[0], ((0, 5), (0, 0)))], axis=0)
    every = range(1, N_DEV)
    wgu_t, m_wgu_t, v_wgu_t = (jnp.swapaxes(a, 1, 2) for a in (ffn_gate_up_w, m_ffn_gate_up_w, v_ffn_gate_up_w))
    pw_shard = pool_w[0].astype(BF16)
    shards = [w.astype(BF16) for w in (wgu_t[0], ffn_down_w[0], conv_in_w[0], conv_out_w[0], wgu_t[1], ffn_down_w[1])]
    n_big = len(shards)
    own_lead = lambda s: lax.dynamic_update_slice(lax.empty((N_DEV,) + s.shape, s.dtype), s[None], (me,) + (0,) * s.ndim)
    pw_land = lax.dynamic_update_slice(lax.empty((4, N_DEV, gw // N_DEV, gw), BF16), pw_shard[:, None], (0, me, 0, 0))
    first = [pw_shard, small_w, pw_land, own_lead(small_w)]
    shard_at, land_at = len(first), len(first) + n_big
    direct = ([Copy(m, 0, _whole, 2, _second, m - 1) for m in every]
              + [Copy(m, 1, _whole, 3, _lead, N_DEV - 2 + m) for m in every])
    level1 = [Copy(mask, shard_at + n, _whole, land_at + n, _lead, len(direct) + 4 * n + j)
              for n in range(n_big) for j, mask in enumerate((SIBLING,) + OTHER_CHIPS)]
    sems1, bufs1, _ = _split_call("gather_start", first + shards + [own_lead(s) for s in shards], start=direct + level1)
    _, _, pw_g, small_g = _split_call("gather_small_done", bufs1[:len(first)], wait=direct, wait_sems=sems1)[1]
    pw = pw_g.reshape(4, gw, gw)
    small_full = jnp.swapaxes(small_g, 0, 1).reshape(16, D)
    gain = lambda l, s: small_full[4 * l + s][None, :]
    taps = small_full[8:16]

    def forward_on(name, group, after):
        k = len(group)
        landed = [Copy(cp.mask, i, cp.src, k + i, cp.dst, cp.sem)
                  for i, n in enumerate(group) for cp in level1 if cp.sb == shard_at + n]
        onward = [Copy(SIBLING, i, (lambda ref, me, m=m: ref.at[me ^ m]), i, (lambda ref, sender, m=m: ref.at[sender ^ m]), 3 * i + j)
                  for i in range(k) for j, m in enumerate(OTHER_CHIPS)]
        sems2, bufs2, _ = _split_call(
            name + "_forward", [bufs1[shard_at + n] for n in group] + [bufs1[land_at + n] for n in group], wait=landed,
            wait_sems=sems1, start=[cp._replace(sb=k + cp.sb, db=k + cp.db) for cp in onward], after=after)
        return name, onward, sems2, bufs2[k:]

    def arrived(state):
        name, onward, sems2, lands2 = state
        return _split_call(name + "_done", lands2, wait=onward, wait_sems=sems2)[1]

    h0 = x[0]
    h1 = _pool_fwd(h0, pw, pool_scale, gain(0, 0), gain(0, 1), tm=tm)
    (wgu0,) = arrived(forward_on("gather_gate_up_0", [0], h1))
    gu0, act0 = _ffn_up(h1, gain(0, 2), wgu0, tm=tm)
    (wd0,) = arrived(forward_on("gather_down_0", [1], act0))
    wd0 = wd0.reshape(N_DEV * fr, D)
    h2, ff0 = _ffn_down(h1, act0, gain(0, 3), wd0, tm=tm)
    win_g, wout_g = arrived(forward_on("gather_conv", [2, 3], h2))
    wout = wout_g.reshape(D, D)
    h3, proj, y = _conv_fwd(h2, gain(1, 0), gain(1, 1), win_g, taps, wout, tm=tm)
    wgu1, wd1 = arrived(forward_on("gather_ffn1", [4, 5], h3))
    wd1 = wd1.reshape(N_DEV * fr, D)
    dh4, gu1, ff1, loss_part = _ffn_fwd(h3, gain(1, 2), gain(1, 3), wgu1, wd1, 1, loss_target[0], tm=tm)

    chip = me >> 1

    def scatter_start(name, sums):
        k = len(sums)
        lands = [lax.dynamic_update_slice(lax.empty(s.shape, BF16), lax.dynamic_index_in_dim(s, chip, 0), (chip, 0, 0))
                 for s in sums]
        plan = [Copy(m, n, (lambda ref, i, m=m: ref.at[(i ^ m) >> 1]), k + n, (lambda ref, i: ref.at[i >> 1]), 3 * n + j)
                for n in range(k) for j, m in enumerate(OTHER_CHIPS)]
        sems, bufs, tok = _split_call(name + "_start", sums + lands, start=plan, token=True)
        return (name, plan, sems, bufs), tok

    def scatter_done(state, after):
        name, plan, sems, bufs = state
        return _split_call(name + "_done", bufs, wait=plan, wait_sems=sems, after=after)[1][len(bufs) // 2:]

    seq = lambda i, k: (k, 0)
    by_block = pl.BlockSpec((None, tk, fb), lambda i, k: (i, k, 0))
    rows = pl.BlockSpec((tk, D), seq)
    dh3, dgu1, dff1, c1, act1, small_f1 = _ffn_bwd(dh4, h3, ff1, gu1, gain(1, 2), gain(1, 3), wgu1, wd1, 1, tm=tm_b)
    g_wgu1 = _wgrad("wgrad_gate_up_1", dgu1, c1, by_block, rows, (fb, D), N_DEV, tk=tk)
    g_wd1 = _wgrad("wgrad_down_1", act1, dff1, by_block, rows, (fb, D), N_DEV // 2, tk=tk)
    rs_ffn1, tok = scatter_start("scatter_ffn1", [g_wgu1, g_wd1])
    dh2, dproj, z, a1, dy, small_c = _conv_bwd(dh3, h2, y, proj, gain(1, 0), gain(1, 1), win_g, taps, wout, tm=tm, after=tok)
    g_win = _wgrad("wgrad_conv_in", dproj, a1, pl.BlockSpec((tk, cb), lambda i, k: (k, i)), rows, (D, cb), N_DEV,
                   tk=tk, transpose=True)
    g_wout = _wgrad("wgrad_conv_out", z, dy, rows, rows, (D, D), 1, tk=tk)
    rs_conv, tok = scatter_start("scatter_conv", [g_win, g_wout])
    dh1, dgu0, dff0, c0, act0, small_f0 = _ffn_bwd(dh2, h1, ff0, gu0, gain(0, 2), gain(0, 3), wgu0, wd0, 0, tm=tm_b, after=tok)
    g_wgu0 = _wgrad("wgrad_gate_up_0", dgu0, c0, by_block, rows, (fb, D), N_DEV, tk=tk)
    rs_wgu0, tok = scatter_start("scatter_gate_up_0", [g_wgu0])
    g_wd0 = _wgrad("wgrad_down_0", act0, dff0, by_block, rows, (fb, D), N_DEV // 2, tk=tk, after=tok)
    rs_wd0, tok = scatter_start("scatter_down_0", [g_wd0])
    grad_x, g_pw, small_p = _pool_bwd(dh1, h0, pw, pool_scale, gain(0, 0), gain(0, 1), tm=tm, after=tok)

    loss_rows = jnp.broadcast_to(loss_part[0:1, 0:1], (8, D))
    small_part = jnp.concatenate([small_p, small_f0, small_c, small_f1, loss_rows], axis=0)
    g_pw = g_pw.reshape(4, N_DEV, gw // N_DEV, gw)
    pw_land = lax.dynamic_update_slice(lax.empty(g_pw.shape, BF16), lax.dynamic_slice_in_dim(g_pw, me, 1, 1), (0, me, 0, 0))
    last = ([Copy(m, 0, (lambda ref, i, m=m: ref.at[:, i ^ m]), 2, _second, m - 1) for m in every]
            + [Copy(m, 1, _whole, 3, _lead, N_DEV - 2 + m) for m in every])
    sems_l, bufs_l, tok = _split_call("scatter_small_start", [g_pw, small_part, pw_land, own_lead(small_part)],
                                      start=last, token=True)

    (r_wgu1, r_wd1), (r_win, r_wout) = scatter_done(rs_ffn1, tok), scatter_done(rs_conv, tok)
    o_win = _reduce_adam("adam_conv_in", [r_win], conv_in_w, m_conv_in_w, v_conv_in_w, tr=256)
    o_wout = _reduce_adam("adam_conv_out", [r_wout], conv_out_w, m_conv_out_w, v_conv_out_w, tr=128, after=o_win[0])
    o_wgu = _reduce_adam("adam_gate_up_1", [r_wgu1], wgu_t, m_wgu_t, v_wgu_t, tr=176, layer=1, after=o_wout[0])
    o_wd = _reduce_adam("adam_down_1", [r_wd1], ffn_down_w, m_ffn_down_w, v_ffn_down_w, tr=176, layer=1, after=o_wgu[0])
    r_pw, r_small = _split_call("scatter_small_done", bufs_l, wait=last, wait_sems=sems_l, after=o_wd[0])[1][2:]
    (r_wgu0,), (r_wd0,) = scatter_done(rs_wgu0, r_small), scatter_done(rs_wd0, r_small)
    o_wgu = _reduce_adam("adam_gate_up_0", [r_wgu0], wgu_t, m_wgu_t, v_wgu_t, tr=176, layer=0, into=o_wgu)
    o_wgu = [jnp.swapaxes(o, 1, 2) for o in o_wgu]
    o_wd = _reduce_adam("adam_down_0", [r_wd0], ffn_down_w, m_ffn_down_w, v_ffn_down_w, tr=176, layer=0, into=o_wd)
    o_pw = _reduce_adam("adam_pool_w", [r_pw[g] for g in range(4)], pool_w[0], m_pool_w[0], v_pool_w[0], tr=32)
    g_small = _small_reduce(r_small)
    loss = g_small[12, 0]
    g_cols = lax.dynamic_slice(g_small, (0, me * dcol), (16, dcol))
    o_small = _small_adam(
        g_cols[0:8], g_cols[8:11], g_small[11:12],
        (norm_gains.reshape(8, dcol), m_norm_gains.reshape(8, dcol), v_norm_gains.reshape(8, dcol)),
        (conv_w[0], m_conv_w[0], v_conv_w[0]), (pool_scale, m_pool_scale, v_pool_scale))
    d_gain, nm_gain, nv_gain, d_taps, nm_taps, nv_taps, d_scale, nm_scale, nv_scale = o_small

    gshape = norm_gains.shape
    per = lambda k: (
        (g_cols[0:8].reshape(gshape), d_gain.reshape(gshape), nm_gain.reshape(gshape), nv_gain.reshape(gshape))[k],
        o_pw[k][None], (g_small[11:12], d_scale, nm_scale, nv_scale)[k], o_win[k],
        (g_cols[8:11][None], d_taps[None], nm_taps[None], nv_taps[None])[k], o_wout[k], o_wgu[k], o_wd[k])
    return (loss, grad_x[None], *per(0), *per(1), *per(2), *per(3))
```

```python
import collections
import functools

import jax
import jax.numpy as jnp
from jax import lax
from jax.experimental import pallas as pl
from jax.experimental.pallas import tpu as pltpu

N_DEV = 8
RMS_EPS = 1e-6
POOL_WINDOWS = (2, 4, 8, 16)
POOL_HALO = 16
CONV_HALO = 16
ADAM_LR, ADAM_B1, ADAM_B2, ADAM_EPS, ADAM_WD, ADAM_STEP = 0.001, 0.9, 0.999, 1e-08, 0.01, 10

VMEM_LIMIT = 56 * 2**20
BF16 = jnp.bfloat16
F32 = jnp.float32
MESH = pl.DeviceIdType.MESH
SIBLING_PAIR_ID = 0


def _params(**kw):
    return pltpu.CompilerParams(vmem_limit_bytes=VMEM_LIMIT, **kw)


def _resident(shape, index_map):
    return pl.BlockSpec(shape, index_map, pipeline_mode=pl.Buffered(1))


def _ordered(body, n_in, after):
    if after is None:
        return functools.partial(body), [], []
    return (lambda *refs: body(*refs[:n_in], *refs[n_in + 1:])), [after], [pl.BlockSpec(memory_space=pl.ANY)]


def _rms_fwd(x, g):
    r = lax.rsqrt(jnp.mean(x * x, axis=-1, keepdims=True) + RMS_EPS)
    return x * r * g


def _rms_bwd(x, g, dy):
    r = lax.rsqrt(jnp.mean(x * x, axis=-1, keepdims=True) + RMS_EPS)
    xhat = x * r
    dg = jnp.sum(dy * xhat, axis=0, keepdims=True)
    t = dy * g
    dx = r * (t - xhat * jnp.mean(t * xhat, axis=-1, keepdims=True))
    return dx, dg


def _sigmoid(x):
    return 0.5 * jnp.tanh(0.5 * x) + 0.5


def _dot(a, b):
    return jnp.dot(a, b, preferred_element_type=F32)


def _dot_nt(a, b):
    return lax.dot_general(a, b, (((1,), (1,)), ((), ())), preferred_element_type=F32)


def _dot_tn(a, b):
    return lax.dot_general(a, b, (((0,), (0,)), ((), ())), preferred_element_type=F32)


def _row_inverse_counts(tile, tm):
    pos = (lax.broadcasted_iota(jnp.int32, (tm, 1), 0) + tile * tm + 1).astype(F32)
    return [1.0 / jnp.minimum(pos, float(w)) for w in POOL_WINDOWS]


def _pool_from_ext(ext, a, invs, gw):
    s = ext
    outs = []
    for g, w in enumerate(POOL_WINDOWS):
        s = s[:, (gw if g else 0):]
        s = s + pltpu.roll(s, w // 2, 0)
        outs.append(s[POOL_HALO:, :gw] * invs[g] - a[:, g * gw:(g + 1) * gw])
    return outs


def _pool_fwd(h, pw, scale, g_pre, g_post, *, tm, after=None):
    T, D = h.shape
    gw = D // len(POOL_WINDOWS)
    hb = tm // POOL_HALO

    def body(h_ref, halo_ref, pw_ref, scale_ref, gpre_ref, gpost_ref, out_ref, ext_ref):
        i = pl.program_id(0)
        x = h_ref[...]
        a = _rms_fwd(x, gpre_ref[...])
        ah = _rms_fwd(halo_ref[...], gpre_ref[...])
        ext_ref[0:POOL_HALO, :] = jnp.where(i == 0, 0.0, ah)
        ext_ref[POOL_HALO:, :] = a
        pooled = _pool_from_ext(ext_ref[...], a, _row_inverse_counts(i, tm), gw)
        mixed = jnp.concatenate([_dot(p.astype(BF16), pw_ref[g]) for g, p in enumerate(pooled)], axis=1)
        out_ref[...] = x + _rms_fwd(mixed * scale_ref[...], gpost_ref[...])

    vec = _resident((1, D), lambda i: (0, 0))
    fn, xa, xs = _ordered(body, 6, after)
    return pl.pallas_call(
        fn, name="pool_fwd", grid=(T // tm,),
        in_specs=[pl.BlockSpec((tm, D), lambda i: (i, 0)),
                  pl.BlockSpec((POOL_HALO, D), lambda i: (jnp.maximum(i * hb - 1, 0), 0)),
                  _resident(pw.shape, lambda i: (0, 0, 0)), vec, vec, vec] + xs,
        out_specs=pl.BlockSpec((tm, D), lambda i: (i, 0)),
        out_shape=jax.ShapeDtypeStruct((T, D), F32),
        scratch_shapes=[pltpu.VMEM((tm + POOL_HALO, D), F32)],
        compiler_params=_params(dimension_semantics=("arbitrary",)),
    )(h, h, pw, scale, g_pre, g_post, *xa)


def _pool_bwd(dh, h, pw, scale, g_pre, g_post, *, tm, after=None):
    T, D = h.shape
    gw = D // len(POOL_WINDOWS)
    hb = tm // POOL_HALO
    nt = T // tm
    n_ext = tm + POOL_HALO

    def body(dh_ref, h_ref, halo_ref, pw_ref, scale_ref, gpre_ref, gpost_ref,
             dx_ref, dpw_ref, small_ref, ext_ref, ext2_ref, carry_ref, dpw_acc):
        i = pl.program_id(0)
        tile = nt - 1 - i

        @pl.when(i == 0)
        def _():
            small_ref[...] = jnp.zeros_like(small_ref)
            dpw_acc[...] = jnp.zeros_like(dpw_acc)
            carry_ref[...] = jnp.zeros_like(carry_ref)

        x = h_ref[...]
        dout = dh_ref[...]
        a = _rms_fwd(x, gpre_ref[...])
        ah = _rms_fwd(halo_ref[...], gpre_ref[...])
        ext_ref[0:POOL_HALO, :] = jnp.where(tile == 0, 0.0, ah)
        ext_ref[POOL_HALO:, :] = a
        invs = _row_inverse_counts(tile, tm)
        pooled = [p.astype(BF16) for p in _pool_from_ext(ext_ref[...], a, invs, gw)]
        mixed_pre = jnp.concatenate([_dot(p, pw_ref[g]) for g, p in enumerate(pooled)], axis=1)
        scale_v = scale_ref[...]
        dmixed, dg_post = _rms_bwd(mixed_pre * scale_v, gpost_ref[...], dout)
        small_ref[1:2, :] += dg_post
        small_ref[2:3, :] += jnp.sum(dmixed * mixed_pre, axis=0, keepdims=True)
        dpre = (dmixed * scale_v).astype(BF16)
        dpooled = []
        for g in range(len(POOL_WINDOWS)):
            dp = dpre[:, g * gw:(g + 1) * gw]
            dpw_acc[g] += _dot_tn(pooled[g], dp)
            dpooled.append(_dot_nt(dp, pw_ref[g]))
        q = jnp.concatenate([d * invs[g] for g, d in enumerate(dpooled)], axis=1)
        ext2_ref[0:tm, :] = q
        ext2_ref[tm:, :] = carry_ref[...]
        carry_ref[...] = q[0:POOL_HALO, :]
        s = ext2_ref[...]
        da = []
        for g, w in enumerate(POOL_WINDOWS):
            s = s[:, (gw if g else 0):]
            s = s + pltpu.roll(s, n_ext - w // 2, 0)
            da.append(s[0:tm, :gw] - dpooled[g])
        dx, dg_pre = _rms_bwd(x, gpre_ref[...], jnp.concatenate(da, axis=1))
        small_ref[0:1, :] += dg_pre
        dx_ref[...] = dout + dx

        @pl.when(i == nt - 1)
        def _():
            dpw_ref[...] = dpw_acc[...].astype(BF16)

    vec = _resident((1, D), lambda i: (0, 0))
    rev = lambda i: (nt - 1 - i, 0)
    fn, xa, xs = _ordered(body, 7, after)
    return pl.pallas_call(
        fn, name="pool_bwd", grid=(nt,),
        in_specs=[pl.BlockSpec((tm, D), rev), pl.BlockSpec((tm, D), rev),
                  pl.BlockSpec((POOL_HALO, D), lambda i: (jnp.maximum((nt - 1 - i) * hb - 1, 0), 0)),
                  _resident(pw.shape, lambda i: (0, 0, 0)), vec, vec, vec] + xs,
        out_specs=[pl.BlockSpec((tm, D), rev),
                   pl.BlockSpec(pw.shape, lambda i: (0, 0, 0)),
                   pl.BlockSpec((8, D), lambda i: (0, 0))],
        out_shape=[jax.ShapeDtypeStruct((T, D), F32), jax.ShapeDtypeStruct(pw.shape, BF16),
                   jax.ShapeDtypeStruct((8, D), F32)],
        scratch_shapes=[pltpu.VMEM((n_ext, D), F32), pltpu.VMEM((n_ext, D), F32),
                        pltpu.VMEM((POOL_HALO, D), F32), pltpu.VMEM(pw.shape, F32)],
        compiler_params=_params(dimension_semantics=("arbitrary",)),
    )(dh, h, h, pw, scale, g_pre, g_post, *xa)


def _ffn_fwd(h, g_pre, g_post, wgu, wd, layer, target, *, tm, after=None):
    T, D = h.shape
    nblk, fb = wgu.shape[0], wgu.shape[1]
    half = nblk // 2
    last = target is not None

    def body(*refs):
        if last:
            h_ref, gpre_ref, gpost_ref, wgu_ref, wd_ref, tgt_ref, out_ref, gu_ref, ff_ref, loss_ref = refs
        else:
            h_ref, gpre_ref, gpost_ref, wgu_ref, wd_ref, out_ref, gu_ref, ff_ref = refs
        x = h_ref[...]
        cb = _rms_fwd(x, gpre_ref[...]).astype(BF16)
        acc = jnp.zeros((tm, D), F32)
        for j in range(half):
            g = _dot_nt(cb, wgu_ref[j])
            u = _dot_nt(cb, wgu_ref[j + half])
            gu_ref[j] = g.astype(BF16)
            gu_ref[j + half] = u.astype(BF16)
            act = (g * _sigmoid(g) * u).astype(BF16)
            acc = acc + _dot(act, wd_ref[j * fb:(j + 1) * fb, :])
        ff_ref[...] = acc.astype(BF16)
        hout = x + _rms_fwd(acc, gpost_ref[...])
        if last:
            diff = hout - tgt_ref[...]
            out_ref[...] = diff * (1.0 / D)

            @pl.when(pl.program_id(0) == 0)
            def _():
                loss_ref[...] = jnp.zeros_like(loss_ref)

            loss_ref[...] += jnp.sum(diff * diff) * (0.5 / D)
        else:
            out_ref[...] = hout

    vec = _resident((1, D), lambda i: (0, 0))
    tile = pl.BlockSpec((tm, D), lambda i: (i, 0))
    in_specs = [tile, vec, vec,
                _resident(wgu.shape, lambda i: (0, 0, 0)), _resident(wd.shape, lambda i: (0, 0))]
    out_specs = [tile, pl.BlockSpec((nblk, tm, fb), lambda i: (0, i, 0)), tile]
    out_shape = [jax.ShapeDtypeStruct((T, D), F32), jax.ShapeDtypeStruct((nblk, T, fb), BF16),
                 jax.ShapeDtypeStruct((T, D), BF16)]
    args = [h, g_pre, g_post, wgu, wd]
    if last:
        in_specs.append(tile)
        args.append(target)
        out_specs.append(pl.BlockSpec((8, 128), lambda i: (0, 0)))
        out_shape.append(jax.ShapeDtypeStruct((8, 128), F32))
    fn, xa, xs = _ordered(body, len(args), after)
    return pl.pallas_call(
        fn, name=f"ffn_fwd_{layer}", grid=(T // tm,), in_specs=in_specs + xs, out_specs=out_specs,
        out_shape=out_shape, compiler_params=_params(dimension_semantics=("arbitrary",)),
    )(*args, *xa)


def _ffn_up(h, g_pre, wgu, *, tm):
    T, D = h.shape
    nblk, fb = wgu.shape[0], wgu.shape[1]
    half = nblk // 2

    def body(h_ref, gpre_ref, wgu_ref, gu_ref, act_ref):
        cb = _rms_fwd(h_ref[...], gpre_ref[...]).astype(BF16)
        for j in range(half):
            g = _dot_nt(cb, wgu_ref[j])
            u = _dot_nt(cb, wgu_ref[j + half])
            gu_ref[j] = g.astype(BF16)
            gu_ref[j + half] = u.astype(BF16)
            act_ref[j] = (g * _sigmoid(g) * u).astype(BF16)

    return pl.pallas_call(
        body, name="ffn_up_0", grid=(T // tm,),
        in_specs=[pl.BlockSpec((tm, D), lambda i: (i, 0)), _resident((1, D), lambda i: (0, 0)),
                  _resident(wgu.shape, lambda i: (0, 0, 0))],
        out_specs=[pl.BlockSpec((nblk, tm, fb), lambda i: (0, i, 0)), pl.BlockSpec((half, tm, fb), lambda i: (0, i, 0))],
        out_shape=[jax.ShapeDtypeStruct((nblk, T, fb), BF16), jax.ShapeDtypeStruct((half, T, fb), BF16)],
        compiler_params=_params(dimension_semantics=("arbitrary",)),
    )(h, g_pre, wgu)


def _ffn_down(h, act, g_post, wd, *, tm):
    T, D = h.shape
    half, fb = act.shape[0], act.shape[2]

    def body(h_ref, act_ref, gpost_ref, wd_ref, out_ref, ff_ref):
        acc = _dot(act_ref[0], wd_ref[0:fb, :])
        for j in range(1, half):
            acc = acc + _dot(act_ref[j], wd_ref[j * fb:(j + 1) * fb, :])
        ff_ref[...] = acc.astype(BF16)
        out_ref[...] = h_ref[...] + _rms_fwd(acc, gpost_ref[...])

    tile = pl.BlockSpec((tm, D), lambda i: (i, 0))
    return pl.pallas_call(
        body, name="ffn_down_0", grid=(T // tm,),
        in_specs=[tile, pl.BlockSpec((half, tm, fb), lambda i: (0, i, 0)), _resident((1, D), lambda i: (0, 0)),
                  _resident(wd.shape, lambda i: (0, 0))],
        out_specs=[tile, tile],
        out_shape=[jax.ShapeDtypeStruct((T, D), F32), jax.ShapeDtypeStruct((T, D), BF16)],
        compiler_params=_params(dimension_semantics=("arbitrary",)),
    )(h, act, g_post, wd)


def _ffn_bwd(dh, h, ff, gu, g_pre, g_post, wgu, wd, layer, *, tm, after=None):
    T, D = h.shape
    nblk, fb = wgu.shape[0], wgu.shape[1]
    half = nblk // 2

    def body(dh_ref, h_ref, ff_ref, gu_ref, gpre_ref, gpost_ref, wgu_ref, wd_ref,
             dx_ref, dgu_ref, dff_ref, c_ref, act_ref, small_ref):
        @pl.when(pl.program_id(0) == 0)
        def _():
            small_ref[...] = jnp.zeros_like(small_ref)

        dout = dh_ref[...]
        dff, dg_post = _rms_bwd(ff_ref[...].astype(F32), gpost_ref[...], dout)
        small_ref[1:2, :] += dg_post
        dffb = dff.astype(BF16)
        dff_ref[...] = dffb
        dc = jnp.zeros((tm, D), F32)
        last = None
        for j in range(half + 1):
            if j < half:
                dact = _dot_nt(dffb, wd_ref[j * fb:(j + 1) * fb, :])
            if last is not None:
                dc = dc + _dot(last[0], wgu_ref[j - 1]) + _dot(last[1], wgu_ref[j - 1 + half])
            if j < half:
                g = gu_ref[j].astype(F32)
                u = gu_ref[j + half].astype(F32)
                s = _sigmoid(g)
                silu = g * s
                act_ref[j] = (silu * u).astype(BF16)
                dg = (dact * u * (s * (1.0 + g * (1.0 - s)))).astype(BF16)
                du = (dact * silu).astype(BF16)
                dgu_ref[j] = dg
                dgu_ref[j + half] = du
                last = (dg, du)
        x = h_ref[...]
        c_ref[...] = _rms_fwd(x, gpre_ref[...]).astype(BF16)
        dx, dg_pre = _rms_bwd(x, gpre_ref[...], dc)
        small_ref[0:1, :] += dg_pre
        dx_ref[...] = dout + dx

    vec = _resident((1, D), lambda i: (0, 0))
    tile = pl.BlockSpec((tm, D), lambda i: (i, 0))
    blk = pl.BlockSpec((nblk, tm, fb), lambda i: (0, i, 0))
    fn, xa, xs = _ordered(body, 8, after)
    return pl.pallas_call(
        fn, name=f"ffn_bwd_{layer}", grid=(T // tm,),
        in_specs=[tile, tile, tile, blk, vec, vec,
                  _resident(wgu.shape, lambda i: (0, 0, 0)), _resident(wd.shape, lambda i: (0, 0))] + xs,
        out_specs=[tile, blk, tile, tile, pl.BlockSpec((half, tm, fb), lambda i: (0, i, 0)),
                   pl.BlockSpec((8, D), lambda i: (0, 0))],
        out_shape=[jax.ShapeDtypeStruct((T, D), F32), jax.ShapeDtypeStruct((nblk, T, fb), BF16),
                   jax.ShapeDtypeStruct((T, D), BF16), jax.ShapeDtypeStruct((T, D), BF16),
                   jax.ShapeDtypeStruct((half, T, fb), BF16), jax.ShapeDtypeStruct((8, D), F32)],
        compiler_params=_params(dimension_semantics=("arbitrary",)),
    )(dh, h, ff, gu, g_pre, g_post, wgu, wd, *xa)


def _conv_fwd(h, g_pre, g_post, win, taps, wout, *, tm, after=None):
    T, D = h.shape
    nblk, cb = win.shape[0], win.shape[2]

    def body(h_ref, gpre_ref, gpost_ref, win_ref, taps_ref, wout_ref,
             out_ref, proj_ref, y_ref, proj_scr, ext_ref, carry_ref):
        i = pl.program_id(0)

        @pl.when(i == 0)
        def _():
            carry_ref[...] = jnp.zeros_like(carry_ref)

        x = h_ref[...]
        a = _rms_fwd(x, gpre_ref[...]).astype(BF16)
        for k in range(nblk):
            proj_scr[:, k * cb:(k + 1) * cb] = _dot(a, win_ref[k])
        proj_ref[...] = proj_scr[...].astype(BF16)
        u = proj_scr[:, D:2 * D] * proj_scr[:, 2 * D:3 * D]
        ext_ref[0:CONV_HALO, :] = carry_ref[...]
        ext_ref[CONV_HALO:, :] = u
        carry_ref[...] = u[tm - CONV_HALO:, :]
        e = ext_ref[...]
        conv = (taps_ref[2:3, :] * u + taps_ref[1:2, :] * pltpu.roll(e, 1, 0)[CONV_HALO:, :]
                + taps_ref[0:1, :] * pltpu.roll(e, 2, 0)[CONV_HALO:, :])
        z = (proj_scr[:, 0:D] * conv).astype(BF16)
        y = _dot(z, wout_ref[...])
        y_ref[...] = y.astype(BF16)
        out_ref[...] = x + _rms_fwd(y, gpost_ref[...])

    vec = _resident((1, D), lambda i: (0, 0))
    tile = pl.BlockSpec((tm, D), lambda i: (i, 0))
    fn, xa, xs = _ordered(body, 6, after)
    return pl.pallas_call(
        fn, name="conv_fwd", grid=(T // tm,),
        in_specs=[tile, vec, vec, _resident(win.shape, lambda i: (0, 0, 0)),
                  _resident(taps.shape, lambda i: (0, 0)), _resident(wout.shape, lambda i: (0, 0))] + xs,
        out_specs=[tile, pl.BlockSpec((tm, 3 * D), lambda i: (i, 0)), tile],
        out_shape=[jax.ShapeDtypeStruct((T, D), F32), jax.ShapeDtypeStruct((T, 3 * D), BF16),
                   jax.ShapeDtypeStruct((T, D), BF16)],
        scratch_shapes=[pltpu.VMEM((tm, 3 * D), F32), pltpu.VMEM((tm + CONV_HALO, D), F32),
                        pltpu.VMEM((CONV_HALO, D), F32)],
        compiler_params=_params(dimension_semantics=("arbitrary",)),
    )(h, g_pre, g_post, win, taps, wout, *xa)


def _conv_bwd(dh, h, y, proj, g_pre, g_post, win, taps, wout, *, tm, after=None):
    T, D = h.shape
    nblk, cb = win.shape[0], win.shape[2]
    nt = T // tm
    hb = tm // CONV_HALO
    n_ext = tm + CONV_HALO

    def body(dh_ref, h_ref, y_ref, proj_ref, halo_ref, gpre_ref, gpost_ref, win_ref, taps_ref, wout_ref,
             dx_ref, dproj_ref, z_ref, a_ref, dy_ref, small_ref, ext_ref, ext2_ref, carry_ref):
        i = pl.program_id(0)
        tile = nt - 1 - i

        @pl.when(i == 0)
        def _():
            small_ref[...] = jnp.zeros_like(small_ref)
            carry_ref[...] = jnp.zeros_like(carry_ref)

        dout = dh_ref[...]
        dy, dg_post = _rms_bwd(y_ref[...].astype(F32), gpost_ref[...], dout)
        small_ref[1:2, :] += dg_post
        dyb = dy.astype(BF16)
        dy_ref[...] = dyb
        dz = _dot_nt(dyb, wout_ref[...])
        bgate = proj_ref[:, 0:D].astype(F32)
        cgate = proj_ref[:, D:2 * D].astype(F32)
        v = proj_ref[:, 2 * D:3 * D].astype(F32)
        u = cgate * v
        uh = halo_ref[:, D:2 * D].astype(F32) * halo_ref[:, 2 * D:3 * D].astype(F32)
        ext_ref[0:CONV_HALO, :] = jnp.where(tile == 0, 0.0, uh)
        ext_ref[CONV_HALO:, :] = u
        e = ext_ref[...]
        u1 = pltpu.roll(e, 1, 0)[CONV_HALO:, :]
        u2 = pltpu.roll(e, 2, 0)[CONV_HALO:, :]
        t0, t1, t2 = taps_ref[0:1, :], taps_ref[1:2, :], taps_ref[2:3, :]
        conv = t2 * u + t1 * u1 + t0 * u2
        z_ref[...] = (bgate * conv).astype(BF16)
        dconv = dz * bgate
        small_ref[2:3, :] += jnp.sum(dconv * u2, axis=0, keepdims=True)
        small_ref[3:4, :] += jnp.sum(dconv * u1, axis=0, keepdims=True)
        small_ref[4:5, :] += jnp.sum(dconv * u, axis=0, keepdims=True)
        ext2_ref[0:tm, :] = dconv
        ext2_ref[tm:, :] = carry_ref[...]
        carry_ref[...] = dconv[0:CONV_HALO, :]
        e2 = ext2_ref[...]
        du = (t2 * dconv + t1 * pltpu.roll(e2, n_ext - 1, 0)[0:tm, :]
              + t0 * pltpu.roll(e2, n_ext - 2, 0)[0:tm, :])
        dproj_ref[:, 0:D] = (dz * conv).astype(BF16)
        dproj_ref[:, D:2 * D] = (du * v).astype(BF16)
        dproj_ref[:, 2 * D:3 * D] = (du * cgate).astype(BF16)
        da = jnp.zeros((tm, D), F32)
        for k in range(nblk):
            da = da + _dot_nt(dproj_ref[:, k * cb:(k + 1) * cb], win_ref[k])
        x = h_ref[...]
        a_ref[...] = _rms_fwd(x, gpre_ref[...]).astype(BF16)
        dx, dg_pre = _rms_bwd(x, gpre_ref[...], da)
        small_ref[0:1, :] += dg_pre
        dx_ref[...] = dout + dx

    vec = _resident((1, D), lambda i: (0, 0))
    rev = lambda i: (nt - 1 - i, 0)
    tile = pl.BlockSpec((tm, D), rev)
    wide = pl.BlockSpec((tm, 3 * D), rev)
    fn, xa, xs = _ordered(body, 10, after)
    return pl.pallas_call(
        fn, name="conv_bwd", grid=(nt,),
        in_specs=[tile, tile, tile, wide,
                  pl.BlockSpec((CONV_HALO, 3 * D), lambda i: (jnp.maximum((nt - 1 - i) * hb - 1, 0), 0)),
                  vec, vec, _resident(win.shape, lambda i: (0, 0, 0)),
                  _resident(taps.shape, lambda i: (0, 0)), _resident(wout.shape, lambda i: (0, 0))] + xs,
        out_specs=[tile, wide, tile, tile, tile, pl.BlockSpec((8, D), lambda i: (0, 0))],
        out_shape=[jax.ShapeDtypeStruct((T, D), F32), jax.ShapeDtypeStruct((T, 3 * D), BF16),
                   jax.ShapeDtypeStruct((T, D), BF16), jax.ShapeDtypeStruct((T, D), BF16),
                   jax.ShapeDtypeStruct((T, D), BF16), jax.ShapeDtypeStruct((8, D), F32)],
        scratch_shapes=[pltpu.VMEM((n_ext, D), F32), pltpu.VMEM((n_ext, D), F32),
                        pltpu.VMEM((CONV_HALO, D), F32)],
        compiler_params=_params(dimension_semantics=("arbitrary",)),
    )(dh, h, y, proj, proj, g_pre, g_post, win, taps, wout, *xa)


def _wgrad(name, a, b, a_spec, b_spec, block, n_blocks, *, tk, transpose=False, after=None):
    T = a.shape[-2]
    nk = T // tk
    M, N = block
    m = N_DEV // n_blocks
    R = M // m
    acc_block = block[::-1] if transpose else block

    def body(a_ref, b_ref, out_ref, acc_ref, stage_ref, recv_ref, send_sems, recv_sems):
        i, k = pl.program_id(0), pl.program_id(1)
        x, y, c = lax.axis_index("x"), lax.axis_index("y"), lax.axis_index("c")

        def sent(blk, p):
            owner = blk * m + p
            q = owner // 2
            return (owner % 2) != c, pltpu.make_async_remote_copy(
                src_ref=stage_ref.at[p * R:(p + 1) * R], dst_ref=recv_ref.at[q], send_sem=send_sems.at[q],
                recv_sem=recv_sems.at[q], device_id=(x, y, 1 - c), device_id_type=MESH)

        @pl.when(jnp.logical_and(i == 0, k == 0))
        def _():
            barrier = pltpu.get_barrier_semaphore()
            pl.semaphore_signal(barrier, inc=1, device_id=(x, y, 1 - c), device_id_type=MESH)
            pl.semaphore_wait(barrier, 1)

        @pl.when(k == 0)
        def _():
            acc_ref[...] = jnp.zeros_like(acc_ref)

        acc_ref[...] += _dot_tn(a_ref[...], b_ref[...])

        @pl.when(k == nk - 1)
        def _():
            for p in range(m):
                away, copy = sent(jnp.maximum(i - 1, 0), p)

                @pl.when(jnp.logical_and(i > 0, away))
                def _():
                    copy.wait_send()

            acc = acc_ref[...]
            stage_ref[...] = (acc.T if transpose else acc).astype(BF16)
            for p in range(m):
                away, copy = sent(i, p)

                @pl.when(away)
                def _():
                    copy.start()

                @pl.when(jnp.logical_not(away))
                def _():
                    out_ref[(i * m + p) // 2] = stage_ref[p * R:(p + 1) * R, :]

        @pl.when(jnp.logical_and(i == n_blocks - 1, k == nk - 1))
        def _():
            for p in range(m):
                away, copy = sent(i, p)

                @pl.when(away)
                def _():
                    copy.wait_send()

            for q in range(N_DEV // 2):
                pltpu.make_async_remote_copy(
                    src_ref=stage_ref.at[0:R], dst_ref=recv_ref.at[q], send_sem=send_sems.at[q],
                    recv_sem=recv_sems.at[q], device_id=(x, y, 1 - c), device_id_type=MESH).wait_recv()
                out_ref[q] = (out_ref[q].astype(F32) + recv_ref[q].astype(F32)).astype(BF16)

    fn, xa, xs = _ordered(body, 2, after)
    return pl.pallas_call(
        fn, name=name, grid=(n_blocks, nk), in_specs=[a_spec, b_spec] + xs,
        out_specs=pl.BlockSpec((N_DEV // 2, R, N), lambda i, k: (0, 0, 0)),
        out_shape=jax.ShapeDtypeStruct((N_DEV // 2, R, N), BF16),
        scratch_shapes=[pltpu.VMEM(acc_block, F32), pltpu.VMEM(block, BF16), pltpu.VMEM((N_DEV // 2, R, N), BF16),
                        pltpu.SemaphoreType.DMA((N_DEV // 2,)), pltpu.SemaphoreType.DMA((N_DEV // 2,))],
        compiler_params=_params(dimension_semantics=("arbitrary", "arbitrary"), collective_id=SIBLING_PAIR_ID),
    )(a, b, *xa)


Copy = collections.namedtuple("Copy", "mask sb src db dst sem")
Local = collections.namedtuple("Local", "sb src db dst")

HBM_SPEC = pl.BlockSpec(memory_space=pltpu.HBM)
SEM_SPEC = pl.BlockSpec(memory_space=pltpu.SEMAPHORE)
SIBLING, X_PEER, Y_PEER, DIAGONAL = 1, 4, 2, 6
OTHER_CHIPS = (X_PEER, Y_PEER, DIAGONAL)


def _whole(ref, i):
    return ref


def _lead(ref, i):
    return ref.at[i]


def _second(ref, i):
    return ref.at[:, i]


def _place():
    x, y, c = lax.axis_index("x"), lax.axis_index("y"), lax.axis_index("c")
    return (x, y, c), 4 * x + 2 * y + c


def _descriptor(cp, bufs, xyc, me, sender, send_sems, recv_sems):
    x, y, c = xyc
    flip = lambda v, bit: (1 - v) if bit else v
    return pltpu.make_async_remote_copy(
        src_ref=cp.src(bufs[cp.sb], me), dst_ref=cp.dst(bufs[cp.db], sender),
        send_sem=send_sems.at[cp.sem], recv_sem=recv_sems.at[cp.sem],
        device_id=(flip(x, cp.mask & 4), flip(y, cp.mask & 2), flip(c, cp.mask & 1)), device_id_type=MESH)


def _exchange(name, bufs, plan, local=()):
    n = len(bufs)

    def body(*refs):
        ins = refs[:n]
        send_sems, recv_sems, local_sems = refs[2 * n:]
        xyc, me = _place()
        own = [pltpu.make_async_copy(lc.src(ins[lc.sb], me), lc.dst(ins[lc.db], me), local_sems.at[i])
               for i, lc in enumerate(local)]
        sends = [_descriptor(cp, ins, xyc, me, me, send_sems, recv_sems) for cp in plan]
        for cp in own + sends:
            cp.start()
        for cp in plan:
            _descriptor(cp, ins, xyc, me, me ^ cp.mask, send_sems, recv_sems).wait_recv()
        for cp in sends:
            cp.wait_send()
        for cp in own:
            cp.wait()

    return pl.pallas_call(
        body, name=name, in_specs=[HBM_SPEC] * n, out_specs=[HBM_SPEC] * n,
        out_shape=[jax.ShapeDtypeStruct(b.shape, b.dtype) for b in bufs],
        input_output_aliases={i: i for i in range(n)},
        scratch_shapes=[pltpu.SemaphoreType.DMA((len(plan),)), pltpu.SemaphoreType.DMA((len(plan),)),
                        pltpu.SemaphoreType.DMA((max(len(local), 1),))],
    )(*bufs)


def _split_call(name, bufs, *, wait=None, wait_sems=None, start=None, local=(), after=None, token=False):
    n = len(bufs)
    n_wait = 2 if wait else 0
    n_after = 1 if after is not None else 0
    n_start = 2 if start else 0

    def body(*refs):
        ins = refs[:n]
        wsend, wrecv = refs[n:n + n_wait] if wait else (None, None)
        outs = refs[n + n_wait + n_after:]
        ssend, srecv = outs[:n_start] if start else (None, None)
        rest = outs[n_start + n:]
        xyc, me = _place()
        for cp in wait or ():
            d = _descriptor(cp, ins, xyc, me, me ^ cp.mask, wsend, wrecv)
            d.wait_send()
            d.wait_recv()
        own = [pltpu.make_async_copy(lc.src(ins[lc.sb], me), lc.dst(ins[lc.db], me), rest[-1].at[i])
               for i, lc in enumerate(local)]
        for cp in own:
            cp.start()
        for cp in start or ():
            _descriptor(cp, ins, xyc, me, me, ssend, srecv).start()
        for cp in own:
            cp.wait()
        if token:
            rest[0][...] = jnp.zeros_like(rest[0])

    args = [pltpu.with_memory_space_constraint(b, pltpu.HBM) for b in bufs]
    in_specs = [HBM_SPEC] * n
    if wait:
        args += list(wait_sems)
        in_specs += [SEM_SPEC] * 2
    if after is not None:
        args.append(after)
        in_specs.append(pl.BlockSpec(memory_space=pl.ANY))
    out_shape, out_specs = [], []
    if start:
        out_shape += [pltpu.SemaphoreType.DMA((len(start),))] * 2
        out_specs += [SEM_SPEC] * 2
    out_shape += [pltpu.HBM(b.shape, b.dtype) for b in bufs]
    out_specs += [HBM_SPEC] * n
    if token:
        out_shape.append(jax.ShapeDtypeStruct((8, 128), F32))
        out_specs.append(pl.BlockSpec(memory_space=pltpu.VMEM))
    outs = pl.pallas_call(
        body, name=name, in_specs=in_specs, out_specs=out_specs, out_shape=out_shape,
        input_output_aliases={i: n_start + i for i in range(n)},
        scratch_shapes=[pltpu.SemaphoreType.DMA((len(local),))] if local else [],
        compiler_params=pltpu.CompilerParams(has_side_effects=pltpu.SideEffectType.DATAFLOW_SIDE_EFFECTING),
    )(*args)
    sems = tuple(outs[:n_start]) if start else None
    return sems, list(outs[n_start:n_start + n]), (outs[n_start + n] if token else None)


def _adamw(w, g, m, v):
    m = ADAM_B1 * m + (1.0 - ADAM_B1) * g
    v = ADAM_B2 * v + (1.0 - ADAM_B2) * (g * g)
    m_hat = m / (1.0 - ADAM_B1 ** ADAM_STEP)
    v_hat = v / (1.0 - ADAM_B2 ** ADAM_STEP)
    delta = -ADAM_LR * (m_hat / (jnp.sqrt(v_hat) + ADAM_EPS) + ADAM_WD * w)
    return delta, m, v


def _reduce_adam(name, parts, w, m, v, *, tr, layer=None, into=None, after=None):
    L, R, C = w.shape
    S = parts[0].shape[0]
    tr = min(tr, R)
    n_l = L if layer is None else 1
    first = 0 if layer is None else layer

    def body(*refs):
        p_refs = refs[:n_l]
        w_ref, m_ref, v_ref = refs[n_l:n_l + 3]
        g_ref, d_ref, nm_ref, nv_ref = refs[-4:]
        for l in range(n_l):
            g = p_refs[l][0].astype(F32)
            for s in range(1, S):
                g = g + p_refs[l][s].astype(F32)
            g_ref[l] = g
            d_ref[l], nm_ref[l], nv_ref[l] = _adamw(w_ref[l], g, m_ref[l], v_ref[l])

    blk = pl.BlockSpec((n_l, tr, C), lambda r: (first, r, 0))
    out = jax.ShapeDtypeStruct((L, R, C), F32)
    extra = list(into or []) + ([after] if after is not None else [])
    return pl.pallas_call(
        body, name=name, grid=(R // tr,),
        in_specs=[pl.BlockSpec((S, tr, C), lambda r: (0, r, 0))] * n_l + [blk, blk, blk]
        + [pl.BlockSpec(memory_space=pl.ANY)] * len(extra),
        out_specs=[blk] * 4, out_shape=[out] * 4,
        input_output_aliases={n_l + 3 + i: i for i in range(4)} if into else {},
        compiler_params=_params(dimension_semantics=("arbitrary",)),
    )(*parts, w, m, v, *extra)


def _small_reduce(parts):
    D = parts.shape[2]
    rows = [0, 1, 8, 9, 16, 17, 24, 25, 18, 19, 20, 2, 32]

    def body(p_ref, out_ref):
        s = p_ref[0]
        for d in range(1, N_DEV):
            s = s + p_ref[d]
        out_ref[...] = jnp.zeros_like(out_ref)
        for r, src in enumerate(rows):
            out_ref[r:r + 1, :] = s[src:src + 1, :]

    return pl.pallas_call(body, name="small_reduce", out_shape=jax.ShapeDtypeStruct((16, D), F32))(parts)


def _small_adam(g_gain, g_taps, g_scale, gains, taps, scale):
    def body(gg, gt, gs, wg, mg, vg, wt, mt, vt, ws, ms, vs, *outs):
        for k, (g, w, m, v) in enumerate(((gg, wg, mg, vg), (gt, wt, mt, vt), (gs, ws, ms, vs))):
            outs[3 * k][...], outs[3 * k + 1][...], outs[3 * k + 2][...] = _adamw(w[...], g[...], m[...], v[...])

    shapes = [jax.ShapeDtypeStruct(t[0].shape, F32) for t in (gains, taps, scale) for _ in range(3)]
    return pl.pallas_call(body, name="small_adam", out_shape=shapes)(g_gain, g_taps, g_scale, *gains, *taps, *scale)


def kernel(x, norm_gains, pool_w, pool_scale, conv_in_w, conv_w, conv_out_w, ffn_gate_up_w, ffn_down_w, loss_target, m_norm_gains, m_pool_w, m_pool_scale, m_conv_in_w, m_conv_w, m_conv_out_w, m_ffn_gate_up_w, m_ffn_down_w, v_norm_gains, v_pool_w, v_pool_scale, v_conv_in_w, v_conv_w, v_conv_out_w, v_ffn_gate_up_w, v_ffn_down_w):
    T, D = x.shape[1], x.shape[2]
    tm = min(512, T)
    tm_b = min(256, T)
    tk = min(2048, T)
    n_layers = ffn_gate_up_w.shape[0]
    fb = ffn_gate_up_w.shape[2]
    fr = ffn_down_w.shape[1]
    dcol = norm_gains.shape[2]
    cb = conv_in_w.shape[2]
    gw = pool_w.shape[3]
    me = 4 * lax.axis_index("x") + 2 * lax.axis_index("y") + lax.axis_index("c")

    small_w = jnp.concatenate([norm_gains.reshape(8, dcol), jnp.pad(conv_w[0], ((0, 5), (0, 0)))], axis=0)
    every = range(1, N_DEV)
    wgu_t, m_wgu_t, v_wgu_t = (jnp.swapaxes(a, 1, 2) for a in (ffn_gate_up_w, m_ffn_gate_up_w, v_ffn_gate_up_w))
    pw_shard = pool_w[0].astype(BF16)
    shards = [w.astype(BF16) for w in (wgu_t[0], ffn_down_w[0], conv_in_w[0], conv_out_w[0], wgu_t[1], ffn_down_w[1])]
    n_big = len(shards)
    own_lead = lambda s: lax.dynamic_update_slice(lax.empty((N_DEV,) + s.shape, s.dtype), s[None], (me,) + (0,) * s.ndim)
    pw_land = lax.dynamic_update_slice(lax.empty((4, N_DEV, gw // N_DEV, gw), BF16), pw_shard[:, None], (0, me, 0, 0))
    first = [pw_shard, small_w, pw_land, own_lead(small_w)]
    shard_at, land_at = len(first), len(first) + n_big
    direct = ([Copy(m, 0, _whole, 2, _second, m - 1) for m in every]
              + [Copy(m, 1, _whole, 3, _lead, N_DEV - 2 + m) for m in every])
    level1 = [Copy(mask, shard_at + n, _whole, land_at + n, _lead, len(direct) + 4 * n + j)
              for n in range(n_big) for j, mask in enumerate((SIBLING,) + OTHER_CHIPS)]
    sems1, bufs1, _ = _split_call("gather_start", first + shards + [own_lead(s) for s in shards], start=direct + level1)
    _, _, pw_g, small_g = _split_call("gather_small_done", bufs1[:len(first)], wait=direct, wait_sems=sems1)[1]
    pw = pw_g.reshape(4, gw, gw)
    small_full = jnp.swapaxes(small_g, 0, 1).reshape(16, D)
    gain = lambda l, s: small_full[4 * l + s][None, :]
    taps = small_full[8:16]

    def forward_on(name, group, after):
        k = len(group)
        landed = [Copy(cp.mask, i, cp.src, k + i, cp.dst, cp.sem)
                  for i, n in enumerate(group) for cp in level1 if cp.sb == shard_at + n]
        onward = [Copy(SIBLING, i, (lambda ref, me, m=m: ref.at[me ^ m]), i, (lambda ref, sender, m=m: ref.at[sender ^ m]), 3 * i + j)
                  for i in range(k) for j, m in enumerate(OTHER_CHIPS)]
        sems2, bufs2, tok = _split_call(
            name + "_forward", [bufs1[shard_at + n] for n in group] + [bufs1[land_at + n] for n in group], wait=landed,
            wait_sems=sems1, start=[cp._replace(sb=k + cp.sb, db=k + cp.db) for cp in onward], after=after, token=True)
        return (name, onward, sems2, bufs2[k:]), tok

    def arrived(state, after=None):
        name, onward, sems2, lands2 = state
        return _split_call(name + "_done", lands2, wait=onward, wait_sems=sems2, after=after)[1]

    h0 = x[0]
    h1 = _pool_fwd(h0, pw, pool_scale, gain(0, 0), gain(0, 1), tm=tm)
    (wgu0,) = arrived(forward_on("gather_gate_up_0", [0], h1)[0])
    gu0, act0 = _ffn_up(h1, gain(0, 2), wgu0, tm=tm)
    ag_down0, tok = forward_on("gather_down_0", [1], act0)
    ag_conv, tok = forward_on("gather_conv", [2, 3], tok)
    (wd0,) = arrived(ag_down0, tok)
    wd0 = wd0.reshape(N_DEV * fr, D)
    h2, ff0 = _ffn_down(h1, act0, gain(0, 3), wd0, tm=tm)
    win_g, wout_g = arrived(ag_conv, h2)
    wout = wout_g.reshape(D, D)
    ag_gate_up1, tok = forward_on("gather_gate_up_1", [4], win_g)
    h3, proj, y = _conv_fwd(h2, gain(1, 0), gain(1, 1), win_g, taps, wout, tm=tm, after=tok)
    ag_down1, tok = forward_on("gather_down_1", [5], h3)
    (wgu1,), (wd1,) = arrived(ag_gate_up1, tok), arrived(ag_down1, tok)
    wd1 = wd1.reshape(N_DEV * fr, D)
    dh4, gu1, ff1, loss_part = _ffn_fwd(h3, gain(1, 2), gain(1, 3), wgu1, wd1, 1, loss_target[0], tm=tm)

    chip = me >> 1

    def scatter_start(name, sums):
        k = len(sums)
        lands = [lax.dynamic_update_slice(lax.empty(s.shape, BF16), lax.dynamic_index_in_dim(s, chip, 0), (chip, 0, 0))
                 for s in sums]
        plan = [Copy(m, n, (lambda ref, i, m=m: ref.at[(i ^ m) >> 1]), k + n, (lambda ref, i: ref.at[i >> 1]), 3 * n + j)
                for n in range(k) for j, m in enumerate(OTHER_CHIPS)]
        sems, bufs, tok = _split_call(name + "_start", sums + lands, start=plan, token=True)
        return (name, plan, sems, bufs), tok

    def scatter_done(state, after):
        name, plan, sems, bufs = state
        return _split_call(name + "_done", bufs, wait=plan, wait_sems=sems, after=after)[1][len(bufs) // 2:]

    seq = lambda i, k: (k, 0)
    by_block = pl.BlockSpec((None, tk, fb), lambda i, k: (i, k, 0))
    rows = pl.BlockSpec((tk, D), seq)
    dh3, dgu1, dff1, c1, act1, small_f1 = _ffn_bwd(dh4, h3, ff1, gu1, gain(1, 2), gain(1, 3), wgu1, wd1, 1, tm=tm_b)
    g_wgu1 = _wgrad("wgrad_gate_up_1", dgu1, c1, by_block, rows, (fb, D), N_DEV, tk=tk)
    g_wd1 = _wgrad("wgrad_down_1", act1, dff1, by_block, rows, (fb, D), N_DEV // 2, tk=tk)
    rs_ffn1, tok = scatter_start("scatter_ffn1", [g_wgu1, g_wd1])
    dh2, dproj, z, a1, dy, small_c = _conv_bwd(dh3, h2, y, proj, gain(1, 0), gain(1, 1), win_g, taps, wout, tm=tm, after=tok)
    g_win = _wgrad("wgrad_conv_in", dproj, a1, pl.BlockSpec((tk, cb), lambda i, k: (k, i)), rows, (D, cb), N_DEV,
                   tk=tk, transpose=True)
    g_wout = _wgrad("wgrad_conv_out", z, dy, rows, rows, (D, D), 1, tk=tk)
    rs_conv, tok = scatter_start("scatter_conv", [g_win, g_wout])
    dh1, dgu0, dff0, c0, act0, small_f0 = _ffn_bwd(dh2, h1, ff0, gu0, gain(0, 2), gain(0, 3), wgu0, wd0, 0, tm=tm_b, after=tok)
    g_wgu0 = _wgrad("wgrad_gate_up_0", dgu0, c0, by_block, rows, (fb, D), N_DEV, tk=tk)
    rs_wgu0, tok = scatter_start("scatter_gate_up_0", [g_wgu0])
    g_wd0 = _wgrad("wgrad_down_0", act0, dff0, by_block, rows, (fb, D), N_DEV // 2, tk=tk, after=tok)
    rs_wd0, tok = scatter_start("scatter_down_0", [g_wd0])
    grad_x, g_pw, small_p = _pool_bwd(dh1, h0, pw, pool_scale, gain(0, 0), gain(0, 1), tm=tm, after=tok)

    loss_rows = jnp.broadcast_to(loss_part[0:1, 0:1], (8, D))
    small_part = jnp.concatenate([small_p, small_f0, small_c, small_f1, loss_rows], axis=0)
    g_pw = g_pw.reshape(4, N_DEV, gw // N_DEV, gw)
    pw_land = lax.dynamic_update_slice(lax.empty(g_pw.shape, BF16), lax.dynamic_slice_in_dim(g_pw, me, 1, 1), (0, me, 0, 0))
    last = ([Copy(m, 0, (lambda ref, i, m=m: ref.at[:, i ^ m]), 2, _second, m - 1) for m in every]
            + [Copy(m, 1, _whole, 3, _lead, N_DEV - 2 + m) for m in every])
    sems_l, bufs_l, tok = _split_call("scatter_small_start", [g_pw, small_part, pw_land, own_lead(small_part)],
                                      start=last, token=True)

    (r_wgu1, r_wd1), (r_win, r_wout) = scatter_done(rs_ffn1, tok), scatter_done(rs_conv, tok)
    o_win = _reduce_adam("adam_conv_in", [r_win], conv_in_w, m_conv_in_w, v_conv_in_w, tr=256)
    o_wout = _reduce_adam("adam_conv_out", [r_wout], conv_out_w, m_conv_out_w, v_conv_out_w, tr=128, after=o_win[0])
    o_wgu = _reduce_adam("adam_gate_up_1", [r_wgu1], wgu_t, m_wgu_t, v_wgu_t, tr=176, layer=1, after=o_wout[0])
    o_wd = _reduce_adam("adam_down_1", [r_wd1], ffn_down_w, m_ffn_down_w, v_ffn_down_w, tr=176, layer=1, after=o_wgu[0])
    r_pw, r_small = _split_call("scatter_small_done", bufs_l, wait=last, wait_sems=sems_l, after=o_wd[0])[1][2:]
    (r_wgu0,), (r_wd0,) = scatter_done(rs_wgu0, r_small), scatter_done(rs_wd0, r_small)
    o_wgu = _reduce_adam("adam_gate_up_0", [r_wgu0], wgu_t, m_wgu_t, v_wgu_t, tr=176, layer=0, into=o_wgu)
    o_wgu = [jnp.swapaxes(o, 1, 2) for o in o_wgu]
    o_wd = _reduce_adam("adam_down_0", [r_wd0], ffn_down_w, m_ffn_down_w, v_ffn_down_w, tr=176, layer=0, into=o_wd)
    o_pw = _reduce_adam("adam_pool_w", [r_pw[g] for g in range(4)], pool_w[0], m_pool_w[0], v_pool_w[0], tr=32)
    g_small = _small_reduce(r_small)
    loss = g_small[12, 0]
    g_cols = lax.dynamic_slice(g_small, (0, me * dcol), (16, dcol))
    o_small = _small_adam(
        g_cols[0:8], g_cols[8:11], g_small[11:12],
        (norm_gains.reshape(8, dcol), m_norm_gains.reshape(8, dcol), v_norm_gains.reshape(8, dcol)),
        (conv_w[0], m_conv_w[0], v_conv_w[0]), (pool_scale, m_pool_scale, v_pool_scale))
    d_gain, nm_gain, nv_gain, d_taps, nm_taps, nv_taps, d_scale, nm_scale, nv_scale = o_small

    gshape = norm_gains.shape
    per = lambda k: (
        (g_cols[0:8].reshape(gshape), d_gain.reshape(gshape), nm_gain.reshape(gshape), nv_gain.reshape(gshape))[k],
        o_pw[k][None], (g_small[11:12], d_scale, nm_scale, nv_scale)[k], o_win[k],
        (g_cols[8:11][None], d_taps[None], nm_taps[None], nv_taps[None])[k], o_wout[k], o_wgu[k], o_wd[k])
    return (loss, grad_x[None], *per(0), *per(1), *per(2), *per(3))
```

```python
import collections
import functools

import jax
import jax.numpy as jnp
from jax import lax
from jax.experimental import pallas as pl
from jax.experimental.pallas import tpu as pltpu

N_DEV = 8
RMS_EPS = 1e-6
POOL_WINDOWS = (2, 4, 8, 16)
POOL_HALO = 16
CONV_HALO = 16
ADAM_LR, ADAM_B1, ADAM_B2, ADAM_EPS, ADAM_WD, ADAM_STEP = 0.001, 0.9, 0.999, 1e-08, 0.01, 10

VMEM_LIMIT = 56 * 2**20
BF16 = jnp.bfloat16
F32 = jnp.float32
MESH = pl.DeviceIdType.MESH
SIBLING_PAIR_ID = 0


def _params(**kw):
    return pltpu.CompilerParams(vmem_limit_bytes=VMEM_LIMIT, **kw)


def _resident(shape, index_map):
    return pl.BlockSpec(shape, index_map, pipeline_mode=pl.Buffered(1))


def _ordered(body, n_in, after):
    if after is None:
        return functools.partial(body), [], []
    return (lambda *refs: body(*refs[:n_in], *refs[n_in + 1:])), [after], [pl.BlockSpec(memory_space=pl.ANY)]


def _rms_fwd(x, g):
    r = lax.rsqrt(jnp.mean(x * x, axis=-1, keepdims=True) + RMS_EPS)
    return x * r * g


def _rms_bwd(x, g, dy):
    r = lax.rsqrt(jnp.mean(x * x, axis=-1, keepdims=True) + RMS_EPS)
    xhat = x * r
    dg = jnp.sum(dy * xhat, axis=0, keepdims=True)
    t = dy * g
    dx = r * (t - xhat * jnp.mean(t * xhat, axis=-1, keepdims=True))
    return dx, dg


def _sigmoid(x):
    return 0.5 * jnp.tanh(0.5 * x) + 0.5


def _dot(a, b):
    return jnp.dot(a, b, preferred_element_type=F32)


def _dot_nt(a, b):
    return lax.dot_general(a, b, (((1,), (1,)), ((), ())), preferred_element_type=F32)


def _dot_tn(a, b):
    return lax.dot_general(a, b, (((0,), (0,)), ((), ())), preferred_element_type=F32)


def _row_inverse_counts(tile, tm):
    pos = (lax.broadcasted_iota(jnp.int32, (tm, 1), 0) + tile * tm + 1).astype(F32)
    return [1.0 / jnp.minimum(pos, float(w)) for w in POOL_WINDOWS]


def _pool_from_ext(ext, a, invs, gw):
    s = ext
    outs = []
    for g, w in enumerate(POOL_WINDOWS):
        s = s[:, (gw if g else 0):]
        s = s + pltpu.roll(s, w // 2, 0)
        outs.append(s[POOL_HALO:, :gw] * invs[g] - a[:, g * gw:(g + 1) * gw])
    return outs


def _pool_fwd(h, pw, scale, g_pre, g_post, *, tm, after=None):
    T, D = h.shape
    gw = D // len(POOL_WINDOWS)
    hb = tm // POOL_HALO

    def body(h_ref, halo_ref, pw_ref, scale_ref, gpre_ref, gpost_ref, out_ref, ext_ref):
        i = pl.program_id(0)
        x = h_ref[...]
        a = _rms_fwd(x, gpre_ref[...])
        ah = _rms_fwd(halo_ref[...], gpre_ref[...])
        ext_ref[0:POOL_HALO, :] = jnp.where(i == 0, 0.0, ah)
        ext_ref[POOL_HALO:, :] = a
        pooled = _pool_from_ext(ext_ref[...], a, _row_inverse_counts(i, tm), gw)
        mixed = jnp.concatenate([_dot(p.astype(BF16), pw_ref[g]) for g, p in enumerate(pooled)], axis=1)
        out_ref[...] = x + _rms_fwd(mixed * scale_ref[...], gpost_ref[...])

    vec = _resident((1, D), lambda i: (0, 0))
    fn, xa, xs = _ordered(body, 6, after)
    return pl.pallas_call(
        fn, name="pool_fwd", grid=(T // tm,),
        in_specs=[pl.BlockSpec((tm, D), lambda i: (i, 0)),
                  pl.BlockSpec((POOL_HALO, D), lambda i: (jnp.maximum(i * hb - 1, 0), 0)),
                  _resident(pw.shape, lambda i: (0, 0, 0)), vec, vec, vec] + xs,
        out_specs=pl.BlockSpec((tm, D), lambda i: (i, 0)),
        out_shape=jax.ShapeDtypeStruct((T, D), F32),
        scratch_shapes=[pltpu.VMEM((tm + POOL_HALO, D), F32)],
        compiler_params=_params(dimension_semantics=("arbitrary",)),
    )(h, h, pw, scale, g_pre, g_post, *xa)


def _pool_bwd(dh, h, pw, scale, g_pre, g_post, *, tm, after=None):
    T, D = h.shape
    gw = D // len(POOL_WINDOWS)
    hb = tm // POOL_HALO
    nt = T // tm
    n_ext = tm + POOL_HALO

    def body(dh_ref, h_ref, halo_ref, pw_ref, scale_ref, gpre_ref, gpost_ref,
             dx_ref, dpw_ref, small_ref, ext_ref, ext2_ref, carry_ref, dpw_acc):
        i = pl.program_id(0)
        tile = nt - 1 - i

        @pl.when(i == 0)
        def _():
            small_ref[...] = jnp.zeros_like(small_ref)
            dpw_acc[...] = jnp.zeros_like(dpw_acc)
            carry_ref[...] = jnp.zeros_like(carry_ref)

        x = h_ref[...]
        dout = dh_ref[...]
        a = _rms_fwd(x, gpre_ref[...])
        ah = _rms_fwd(halo_ref[...], gpre_ref[...])
        ext_ref[0:POOL_HALO, :] = jnp.where(tile == 0, 0.0, ah)
        ext_ref[POOL_HALO:, :] = a
        invs = _row_inverse_counts(tile, tm)
        pooled = [p.astype(BF16) for p in _pool_from_ext(ext_ref[...], a, invs, gw)]
        mixed_pre = jnp.concatenate([_dot(p, pw_ref[g]) for g, p in enumerate(pooled)], axis=1)
        scale_v = scale_ref[...]
        dmixed, dg_post = _rms_bwd(mixed_pre * scale_v, gpost_ref[...], dout)
        small_ref[1:2, :] += dg_post
        small_ref[2:3, :] += jnp.sum(dmixed * mixed_pre, axis=0, keepdims=True)
        dpre = (dmixed * scale_v).astype(BF16)
        dpooled = []
        for g in range(len(POOL_WINDOWS)):
            dp = dpre[:, g * gw:(g + 1) * gw]
            dpw_acc[g] += _dot_tn(pooled[g], dp)
            dpooled.append(_dot_nt(dp, pw_ref[g]))
        q = jnp.concatenate([d * invs[g] for g, d in enumerate(dpooled)], axis=1)
        ext2_ref[0:tm, :] = q
        ext2_ref[tm:, :] = carry_ref[...]
        carry_ref[...] = q[0:POOL_HALO, :]
        s = ext2_ref[...]
        da = []
        for g, w in enumerate(POOL_WINDOWS):
            s = s[:, (gw if g else 0):]
            s = s + pltpu.roll(s, n_ext - w // 2, 0)
            da.append(s[0:tm, :gw] - dpooled[g])
        dx, dg_pre = _rms_bwd(x, gpre_ref[...], jnp.concatenate(da, axis=1))
        small_ref[0:1, :] += dg_pre
        dx_ref[...] = dout + dx

        @pl.when(i == nt - 1)
        def _():
            dpw_ref[...] = dpw_acc[...].astype(BF16)

    vec = _resident((1, D), lambda i: (0, 0))
    rev = lambda i: (nt - 1 - i, 0)
    fn, xa, xs = _ordered(body, 7, after)
    return pl.pallas_call(
        fn, name="pool_bwd", grid=(nt,),
        in_specs=[pl.BlockSpec((tm, D), rev), pl.BlockSpec((tm, D), rev),
                  pl.BlockSpec((POOL_HALO, D), lambda i: (jnp.maximum((nt - 1 - i) * hb - 1, 0), 0)),
                  _resident(pw.shape, lambda i: (0, 0, 0)), vec, vec, vec] + xs,
        out_specs=[pl.BlockSpec((tm, D), rev),
                   pl.BlockSpec(pw.shape, lambda i: (0, 0, 0)),
                   pl.BlockSpec((8, D), lambda i: (0, 0))],
        out_shape=[jax.ShapeDtypeStruct((T, D), F32), jax.ShapeDtypeStruct(pw.shape, BF16),
                   jax.ShapeDtypeStruct((8, D), F32)],
        scratch_shapes=[pltpu.VMEM((n_ext, D), F32), pltpu.VMEM((n_ext, D), F32),
                        pltpu.VMEM((POOL_HALO, D), F32), pltpu.VMEM(pw.shape, F32)],
        compiler_params=_params(dimension_semantics=("arbitrary",)),
    )(dh, h, h, pw, scale, g_pre, g_post, *xa)


def _ffn_fwd(h, g_pre, g_post, wgu, wd, layer, target, *, tm, after=None):
    T, D = h.shape
    nblk, fb = wgu.shape[0], wgu.shape[1]
    half = nblk // 2
    last = target is not None

    def body(*refs):
        if last:
            h_ref, gpre_ref, gpost_ref, wgu_ref, wd_ref, tgt_ref, out_ref, gu_ref, ff_ref, loss_ref = refs
        else:
            h_ref, gpre_ref, gpost_ref, wgu_ref, wd_ref, out_ref, gu_ref, ff_ref = refs
        x = h_ref[...]
        cb = _rms_fwd(x, gpre_ref[...]).astype(BF16)
        acc = jnp.zeros((tm, D), F32)
        for j in range(half):
            g = _dot_nt(cb, wgu_ref[j])
            u = _dot_nt(cb, wgu_ref[j + half])
            gu_ref[j] = g.astype(BF16)
            gu_ref[j + half] = u.astype(BF16)
            act = (g * _sigmoid(g) * u).astype(BF16)
            acc = acc + _dot(act, wd_ref[j * fb:(j + 1) * fb, :])
        ff_ref[...] = acc.astype(BF16)
        hout = x + _rms_fwd(acc, gpost_ref[...])
        if last:
            diff = hout - tgt_ref[...]
            out_ref[...] = diff * (1.0 / D)

            @pl.when(pl.program_id(0) == 0)
            def _():
                loss_ref[...] = jnp.zeros_like(loss_ref)

            loss_ref[...] += jnp.sum(diff * diff) * (0.5 / D)
        else:
            out_ref[...] = hout

    vec = _resident((1, D), lambda i: (0, 0))
    tile = pl.BlockSpec((tm, D), lambda i: (i, 0))
    in_specs = [tile, vec, vec,
                _resident(wgu.shape, lambda i: (0, 0, 0)), _resident(wd.shape, lambda i: (0, 0))]
    out_specs = [tile, pl.BlockSpec((nblk, tm, fb), lambda i: (0, i, 0)), tile]
    out_shape = [jax.ShapeDtypeStruct((T, D), F32), jax.ShapeDtypeStruct((nblk, T, fb), BF16),
                 jax.ShapeDtypeStruct((T, D), BF16)]
    args = [h, g_pre, g_post, wgu, wd]
    if last:
        in_specs.append(tile)
        args.append(target)
        out_specs.append(pl.BlockSpec((8, 128), lambda i: (0, 0)))
        out_shape.append(jax.ShapeDtypeStruct((8, 128), F32))
    fn, xa, xs = _ordered(body, len(args), after)
    return pl.pallas_call(
        fn, name=f"ffn_fwd_{layer}", grid=(T // tm,), in_specs=in_specs + xs, out_specs=out_specs,
        out_shape=out_shape, compiler_params=_params(dimension_semantics=("arbitrary",)),
    )(*args, *xa)


def _ffn_up(h, g_pre, wgu, *, tm):
    T, D = h.shape
    nblk, fb = wgu.shape[0], wgu.shape[1]
    half = nblk // 2

    def body(h_ref, gpre_ref, wgu_ref, gu_ref, act_ref):
        cb = _rms_fwd(h_ref[...], gpre_ref[...]).astype(BF16)
        for j in range(half):
            g = _dot_nt(cb, wgu_ref[j])
            u = _dot_nt(cb, wgu_ref[j + half])
            gu_ref[j] = g.astype(BF16)
            gu_ref[j + half] = u.astype(BF16)
            act_ref[j] = (g * _sigmoid(g) * u).astype(BF16)

    return pl.pallas_call(
        body, name="ffn_up_0", grid=(T // tm,),
        in_specs=[pl.BlockSpec((tm, D), lambda i: (i, 0)), _resident((1, D), lambda i: (0, 0)),
                  _resident(wgu.shape, lambda i: (0, 0, 0))],
        out_specs=[pl.BlockSpec((nblk, tm, fb), lambda i: (0, i, 0)), pl.BlockSpec((half, tm, fb), lambda i: (0, i, 0))],
        out_shape=[jax.ShapeDtypeStruct((nblk, T, fb), BF16), jax.ShapeDtypeStruct((half, T, fb), BF16)],
        compiler_params=_params(dimension_semantics=("arbitrary",)),
    )(h, g_pre, wgu)


def _ffn_down(h, act, g_post, wd, *, tm):
    T, D = h.shape
    half, fb = act.shape[0], act.shape[2]

    def body(h_ref, act_ref, gpost_ref, wd_ref, out_ref, ff_ref):
        acc = _dot(act_ref[0], wd_ref[0:fb, :])
        for j in range(1, half):
            acc = acc + _dot(act_ref[j], wd_ref[j * fb:(j + 1) * fb, :])
        ff_ref[...] = acc.astype(BF16)
        out_ref[...] = h_ref[...] + _rms_fwd(acc, gpost_ref[...])

    tile = pl.BlockSpec((tm, D), lambda i: (i, 0))
    return pl.pallas_call(
        body, name="ffn_down_0", grid=(T // tm,),
        in_specs=[tile, pl.BlockSpec((half, tm, fb), lambda i: (0, i, 0)), _resident((1, D), lambda i: (0, 0)),
                  _resident(wd.shape, lambda i: (0, 0))],
        out_specs=[tile, tile],
        out_shape=[jax.ShapeDtypeStruct((T, D), F32), jax.ShapeDtypeStruct((T, D), BF16)],
        compiler_params=_params(dimension_semantics=("arbitrary",)),
    )(h, act, g_post, wd)


def _ffn_bwd(dh, h, ff, gu, g_pre, g_post, wgu, wd, layer, *, tm, after=None):
    T, D = h.shape
    nblk, fb = wgu.shape[0], wgu.shape[1]
    half = nblk // 2

    def body(dh_ref, h_ref, ff_ref, gu_ref, gpre_ref, gpost_ref, wgu_ref, wd_ref,
             dx_ref, dgu_ref, dff_ref, c_ref, act_ref, small_ref):
        @pl.when(pl.program_id(0) == 0)
        def _():
            small_ref[...] = jnp.zeros_like(small_ref)

        dout = dh_ref[...]
        dff, dg_post = _rms_bwd(ff_ref[...].astype(F32), gpost_ref[...], dout)
        small_ref[1:2, :] += dg_post
        dffb = dff.astype(BF16)
        dff_ref[...] = dffb
        dc = jnp.zeros((tm, D), F32)
        last = None
        for j in range(half + 1):
            if j < half:
                dact = _dot_nt(dffb, wd_ref[j * fb:(j + 1) * fb, :])
            if last is not None:
                dc = dc + _dot(last[0], wgu_ref[j - 1]) + _dot(last[1], wgu_ref[j - 1 + half])
            if j < half:
                g = gu_ref[j].astype(F32)
                u = gu_ref[j + half].astype(F32)
                s = _sigmoid(g)
                silu = g * s
                act_ref[j] = (silu * u).astype(BF16)
                dg = (dact * u * (s * (1.0 + g * (1.0 - s)))).astype(BF16)
                du = (dact * silu).astype(BF16)
                dgu_ref[j] = dg
                dgu_ref[j + half] = du
                last = (dg, du)
        x = h_ref[...]
        c_ref[...] = _rms_fwd(x, gpre_ref[...]).astype(BF16)
        dx, dg_pre = _rms_bwd(x, gpre_ref[...], dc)
        small_ref[0:1, :] += dg_pre
        dx_ref[...] = dout + dx

    vec = _resident((1, D), lambda i: (0, 0))
    tile = pl.BlockSpec((tm, D), lambda i: (i, 0))
    blk = pl.BlockSpec((nblk, tm, fb), lambda i: (0, i, 0))
    fn, xa, xs = _ordered(body, 8, after)
    return pl.pallas_call(
        fn, name=f"ffn_bwd_{layer}", grid=(T // tm,),
        in_specs=[tile, tile, tile, blk, vec, vec,
                  _resident(wgu.shape, lambda i: (0, 0, 0)), _resident(wd.shape, lambda i: (0, 0))] + xs,
        out_specs=[tile, blk, tile, tile, pl.BlockSpec((half, tm, fb), lambda i: (0, i, 0)),
                   pl.BlockSpec((8, D), lambda i: (0, 0))],
        out_shape=[jax.ShapeDtypeStruct((T, D), F32), jax.ShapeDtypeStruct((nblk, T, fb), BF16),
                   jax.ShapeDtypeStruct((T, D), BF16), jax.ShapeDtypeStruct((T, D), BF16),
                   jax.ShapeDtypeStruct((half, T, fb), BF16), jax.ShapeDtypeStruct((8, D), F32)],
        compiler_params=_params(dimension_semantics=("arbitrary",)),
    )(dh, h, ff, gu, g_pre, g_post, wgu, wd, *xa)


def _conv_fwd(h, g_pre, g_post, win, taps, wout, *, tm, after=None):
    T, D = h.shape
    nblk, cb = win.shape[0], win.shape[2]

    def body(h_ref, gpre_ref, gpost_ref, win_ref, taps_ref, wout_ref,
             out_ref, proj_ref, y_ref, proj_scr, ext_ref, carry_ref):
        i = pl.program_id(0)

        @pl.when(i == 0)
        def _():
            carry_ref[...] = jnp.zeros_like(carry_ref)

        x = h_ref[...]
        a = _rms_fwd(x, gpre_ref[...]).astype(BF16)
        for k in range(nblk):
            proj_scr[:, k * cb:(k + 1) * cb] = _dot(a, win_ref[k])
        proj_ref[...] = proj_scr[...].astype(BF16)
        u = proj_scr[:, D:2 * D] * proj_scr[:, 2 * D:3 * D]
        ext_ref[0:CONV_HALO, :] = carry_ref[...]
        ext_ref[CONV_HALO:, :] = u
        carry_ref[...] = u[tm - CONV_HALO:, :]
        e = ext_ref[...]
        conv = (taps_ref[2:3, :] * u + taps_ref[1:2, :] * pltpu.roll(e, 1, 0)[CONV_HALO:, :]
                + taps_ref[0:1, :] * pltpu.roll(e, 2, 0)[CONV_HALO:, :])
        z = (proj_scr[:, 0:D] * conv).astype(BF16)
        y = _dot(z, wout_ref[...])
        y_ref[...] = y.astype(BF16)
        out_ref[...] = x + _rms_fwd(y, gpost_ref[...])

    vec = _resident((1, D), lambda i: (0, 0))
    tile = pl.BlockSpec((tm, D), lambda i: (i, 0))
    fn, xa, xs = _ordered(body, 6, after)
    return pl.pallas_call(
        fn, name="conv_fwd", grid=(T // tm,),
        in_specs=[tile, vec, vec, _resident(win.shape, lambda i: (0, 0, 0)),
                  _resident(taps.shape, lambda i: (0, 0)), _resident(wout.shape, lambda i: (0, 0))] + xs,
        out_specs=[tile, pl.BlockSpec((tm, 3 * D), lambda i: (i, 0)), tile],
        out_shape=[jax.ShapeDtypeStruct((T, D), F32), jax.ShapeDtypeStruct((T, 3 * D), BF16),
                   jax.ShapeDtypeStruct((T, D), BF16)],
        scratch_shapes=[pltpu.VMEM((tm, 3 * D), F32), pltpu.VMEM((tm + CONV_HALO, D), F32),
                        pltpu.VMEM((CONV_HALO, D), F32)],
        compiler_params=_params(dimension_semantics=("arbitrary",)),
    )(h, g_pre, g_post, win, taps, wout, *xa)


def _conv_bwd(dh, h, y, proj, g_pre, g_post, win, taps, wout, *, tm, after=None):
    T, D = h.shape
    nblk, cb = win.shape[0], win.shape[2]
    nt = T // tm
    hb = tm // CONV_HALO
    n_ext = tm + CONV_HALO

    def body(dh_ref, h_ref, y_ref, proj_ref, halo_ref, gpre_ref, gpost_ref, win_ref, taps_ref, wout_ref,
             dx_ref, dproj_ref, z_ref, a_ref, dy_ref, small_ref, ext_ref, ext2_ref, carry_ref):
        i = pl.program_id(0)
        tile = nt - 1 - i

        @pl.when(i == 0)
        def _():
            small_ref[...] = jnp.zeros_like(small_ref)
            carry_ref[...] = jnp.zeros_like(carry_ref)

        dout = dh_ref[...]
        dy, dg_post = _rms_bwd(y_ref[...].astype(F32), gpost_ref[...], dout)
        small_ref[1:2, :] += dg_post
        dyb = dy.astype(BF16)
        dy_ref[...] = dyb
        dz = _dot_nt(dyb, wout_ref[...])
        bgate = proj_ref[:, 0:D].astype(F32)
        cgate = proj_ref[:, D:2 * D].astype(F32)
        v = proj_ref[:, 2 * D:3 * D].astype(F32)
        u = cgate * v
        uh = halo_ref[:, D:2 * D].astype(F32) * halo_ref[:, 2 * D:3 * D].astype(F32)
        ext_ref[0:CONV_HALO, :] = jnp.where(tile == 0, 0.0, uh)
        ext_ref[CONV_HALO:, :] = u
        e = ext_ref[...]
        u1 = pltpu.roll(e, 1, 0)[CONV_HALO:, :]
        u2 = pltpu.roll(e, 2, 0)[CONV_HALO:, :]
        t0, t1, t2 = taps_ref[0:1, :], taps_ref[1:2, :], taps_ref[2:3, :]
        conv = t2 * u + t1 * u1 + t0 * u2
        z_ref[...] = (bgate * conv).astype(BF16)
        dconv = dz * bgate
        small_ref[2:3, :] += jnp.sum(dconv * u2, axis=0, keepdims=True)
        small_ref[3:4, :] += jnp.sum(dconv * u1, axis=0, keepdims=True)
        small_ref[4:5, :] += jnp.sum(dconv * u, axis=0, keepdims=True)
        ext2_ref[0:tm, :] = dconv
        ext2_ref[tm:, :] = carry_ref[...]
        carry_ref[...] = dconv[0:CONV_HALO, :]
        e2 = ext2_ref[...]
        du = (t2 * dconv + t1 * pltpu.roll(e2, n_ext - 1, 0)[0:tm, :]
              + t0 * pltpu.roll(e2, n_ext - 2, 0)[0:tm, :])
        dproj_ref[:, 0:D] = (dz * conv).astype(BF16)
        dproj_ref[:, D:2 * D] = (du * v).astype(BF16)
        dproj_ref[:, 2 * D:3 * D] = (du * cgate).astype(BF16)
        da = jnp.zeros((tm, D), F32)
        for k in range(nblk):
            da = da + _dot_nt(dproj_ref[:, k * cb:(k + 1) * cb], win_ref[k])
        x = h_ref[...]
        a_ref[...] = _rms_fwd(x, gpre_ref[...]).astype(BF16)
        dx, dg_pre = _rms_bwd(x, gpre_ref[...], da)
        small_ref[0:1, :] += dg_pre
        dx_ref[...] = dout + dx

    vec = _resident((1, D), lambda i: (0, 0))
    rev = lambda i: (nt - 1 - i, 0)
    tile = pl.BlockSpec((tm, D), rev)
    wide = pl.BlockSpec((tm, 3 * D), rev)
    fn, xa, xs = _ordered(body, 10, after)
    return pl.pallas_call(
        fn, name="conv_bwd", grid=(nt,),
        in_specs=[tile, tile, tile, wide,
                  pl.BlockSpec((CONV_HALO, 3 * D), lambda i: (jnp.maximum((nt - 1 - i) * hb - 1, 0), 0)),
                  vec, vec, _resident(win.shape, lambda i: (0, 0, 0)),
                  _resident(taps.shape, lambda i: (0, 0)), _resident(wout.shape, lambda i: (0, 0))] + xs,
        out_specs=[tile, wide, tile, tile, tile, pl.BlockSpec((8, D), lambda i: (0, 0))],
        out_shape=[jax.ShapeDtypeStruct((T, D), F32), jax.ShapeDtypeStruct((T, 3 * D), BF16),
                   jax.ShapeDtypeStruct((T, D), BF16), jax.ShapeDtypeStruct((T, D), BF16),
                   jax.ShapeDtypeStruct((T, D), BF16), jax.ShapeDtypeStruct((8, D), F32)],
        scratch_shapes=[pltpu.VMEM((n_ext, D), F32), pltpu.VMEM((n_ext, D), F32),
                        pltpu.VMEM((CONV_HALO, D), F32)],
        compiler_params=_params(dimension_semantics=("arbitrary",)),
    )(dh, h, y, proj, proj, g_pre, g_post, win, taps, wout, *xa)


def _wgrad(name, a, b, a_spec, b_spec, block, n_blocks, *, tk, transpose=False, after=None):
    T = a.shape[-2]
    nk = T // tk
    M, N = block
    m = N_DEV // n_blocks
    R = M // m
    acc_block = block[::-1] if transpose else block

    def body(a_ref, b_ref, out_ref, acc_ref, stage_ref, recv_ref, send_sems, recv_sems):
        i, k = pl.program_id(0), pl.program_id(1)
        x, y, c = lax.axis_index("x"), lax.axis_index("y"), lax.axis_index("c")

        def sent(blk, p):
            owner = blk * m + p
            q = owner // 2
            return (owner % 2) != c, pltpu.make_async_remote_copy(
                src_ref=stage_ref.at[p * R:(p + 1) * R], dst_ref=recv_ref.at[q], send_sem=send_sems.at[q],
                recv_sem=recv_sems.at[q], device_id=(x, y, 1 - c), device_id_type=MESH)

        @pl.when(jnp.logical_and(i == 0, k == 0))
        def _():
            barrier = pltpu.get_barrier_semaphore()
            pl.semaphore_signal(barrier, inc=1, device_id=(x, y, 1 - c), device_id_type=MESH)
            pl.semaphore_wait(barrier, 1)

        @pl.when(k == 0)
        def _():
            acc_ref[...] = jnp.zeros_like(acc_ref)

        acc_ref[...] += _dot_tn(a_ref[...], b_ref[...])

        @pl.when(k == nk - 1)
        def _():
            for p in range(m):
                away, copy = sent(jnp.maximum(i - 1, 0), p)

                @pl.when(jnp.logical_and(i > 0, away))
                def _():
                    copy.wait_send()

            acc = acc_ref[...]
            stage_ref[...] = (acc.T if transpose else acc).astype(BF16)
            for p in range(m):
                away, copy = sent(i, p)

                @pl.when(away)
                def _():
                    copy.start()

                @pl.when(jnp.logical_not(away))
                def _():
                    out_ref[(i * m + p) // 2] = stage_ref[p * R:(p + 1) * R, :]

        @pl.when(jnp.logical_and(i == n_blocks - 1, k == nk - 1))
        def _():
            for p in range(m):
                away, copy = sent(i, p)

                @pl.when(away)
                def _():
                    copy.wait_send()

            for q in range(N_DEV // 2):
                pltpu.make_async_remote_copy(
                    src_ref=stage_ref.at[0:R], dst_ref=recv_ref.at[q], send_sem=send_sems.at[q],
                    recv_sem=recv_sems.at[q], device_id=(x, y, 1 - c), device_id_type=MESH).wait_recv()
                out_ref[q] = (out_ref[q].astype(F32) + recv_ref[q].astype(F32)).astype(BF16)

    fn, xa, xs = _ordered(body, 2, after)
    return pl.pallas_call(
        fn, name=name, grid=(n_blocks, nk), in_specs=[a_spec, b_spec] + xs,
        out_specs=pl.BlockSpec((N_DEV // 2, R, N), lambda i, k: (0, 0, 0)),
        out_shape=jax.ShapeDtypeStruct((N_DEV // 2, R, N), BF16),
        scratch_shapes=[pltpu.VMEM(acc_block, F32), pltpu.VMEM(block, BF16), pltpu.VMEM((N_DEV // 2, R, N), BF16),
                        pltpu.SemaphoreType.DMA((N_DEV // 2,)), pltpu.SemaphoreType.DMA((N_DEV // 2,))],
        compiler_params=_params(dimension_semantics=("arbitrary", "arbitrary"), collective_id=SIBLING_PAIR_ID),
    )(a, b, *xa)


Copy = collections.namedtuple("Copy", "mask sb src db dst sem")
Local = collections.namedtuple("Local", "sb src db dst")

HBM_SPEC = pl.BlockSpec(memory_space=pltpu.HBM)
SEM_SPEC = pl.BlockSpec(memory_space=pltpu.SEMAPHORE)
SIBLING, X_PEER, Y_PEER, DIAGONAL = 1, 4, 2, 6
OTHER_CHIPS = (X_PEER, Y_PEER, DIAGONAL)


def _whole(ref, i):
    return ref


def _lead(ref, i):
    return ref.at[i]


def _second(ref, i):
    return ref.at[:, i]


def _place():
    x, y, c = lax.axis_index("x"), lax.axis_index("y"), lax.axis_index("c")
    return (x, y, c), 4 * x + 2 * y + c


def _descriptor(cp, bufs, xyc, me, sender, send_sems, recv_sems):
    x, y, c = xyc
    flip = lambda v, bit: (1 - v) if bit else v
    return pltpu.make_async_remote_copy(
        src_ref=cp.src(bufs[cp.sb], me), dst_ref=cp.dst(bufs[cp.db], sender),
        send_sem=send_sems.at[cp.sem], recv_sem=recv_sems.at[cp.sem],
        device_id=(flip(x, cp.mask & 4), flip(y, cp.mask & 2), flip(c, cp.mask & 1)), device_id_type=MESH)


def _exchange(name, bufs, plan, local=()):
    n = len(bufs)

    def body(*refs):
        ins = refs[:n]
        send_sems, recv_sems, local_sems = refs[2 * n:]
        xyc, me = _place()
        own = [pltpu.make_async_copy(lc.src(ins[lc.sb], me), lc.dst(ins[lc.db], me), local_sems.at[i])
               for i, lc in enumerate(local)]
        sends = [_descriptor(cp, ins, xyc, me, me, send_sems, recv_sems) for cp in plan]
        for cp in own + sends:
            cp.start()
        for cp in plan:
            _descriptor(cp, ins, xyc, me, me ^ cp.mask, send_sems, recv_sems).wait_recv()
        for cp in sends:
            cp.wait_send()
        for cp in own:
            cp.wait()

    return pl.pallas_call(
        body, name=name, in_specs=[HBM_SPEC] * n, out_specs=[HBM_SPEC] * n,
        out_shape=[jax.ShapeDtypeStruct(b.shape, b.dtype) for b in bufs],
        input_output_aliases={i: i for i in range(n)},
        scratch_shapes=[pltpu.SemaphoreType.DMA((len(plan),)), pltpu.SemaphoreType.DMA((len(plan),)),
                        pltpu.SemaphoreType.DMA((max(len(local), 1),))],
    )(*bufs)


def _place_own(me, items):
    def body(me_ref, *refs):
        for src, dst in zip(refs[:len(items)], refs[len(items):]):
            dst[...] = src[...].astype(dst.dtype)

    return pl.pallas_call(
        body, name="place_own",
        grid_spec=pltpu.PrefetchScalarGridSpec(
            num_scalar_prefetch=1, grid=(1,),
            in_specs=[pl.BlockSpec(blk, functools.partial(lambda i, m, idx: idx, idx=idx)) for _, blk, idx, _, _, _, _ in items],
            out_specs=[pl.BlockSpec(oblk, functools.partial(lambda i, m, at: at(m[0]), at=at)) for *_, oblk, at in items]),
        out_shape=[jax.ShapeDtypeStruct(shape, dtype) for _, _, _, shape, dtype, _, _ in items],
        compiler_params=_params(dimension_semantics=("arbitrary",)),
    )(jnp.reshape(me, (1,)).astype(jnp.int32), *[a for a, *_ in items])


def _split_call(name, bufs, *, wait=None, wait_sems=None, start=None, local=(), after=None, token=False):
    n = len(bufs)
    n_wait = 2 if wait else 0
    n_after = 1 if after is not None else 0
    n_start = 2 if start else 0

    def body(*refs):
        ins = refs[:n]
        wsend, wrecv = refs[n:n + n_wait] if wait else (None, None)
        outs = refs[n + n_wait + n_after:]
        ssend, srecv = outs[:n_start] if start else (None, None)
        rest = outs[n_start + n:]
        xyc, me = _place()
        for cp in wait or ():
            d = _descriptor(cp, ins, xyc, me, me ^ cp.mask, wsend, wrecv)
            d.wait_send()
            d.wait_recv()
        own = [pltpu.make_async_copy(lc.src(ins[lc.sb], me), lc.dst(ins[lc.db], me), rest[-1].at[i])
               for i, lc in enumerate(local)]
        for cp in own:
            cp.start()
        for cp in start or ():
            _descriptor(cp, ins, xyc, me, me, ssend, srecv).start()
        for cp in own:
            cp.wait()
        if token:
            rest[0][...] = jnp.zeros_like(rest[0])

    args = [pltpu.with_memory_space_constraint(b, pltpu.HBM) for b in bufs]
    in_specs = [HBM_SPEC] * n
    if wait:
        args += list(wait_sems)
        in_specs += [SEM_SPEC] * 2
    if after is not None:
        args.append(after)
        in_specs.append(pl.BlockSpec(memory_space=pl.ANY))
    out_shape, out_specs = [], []
    if start:
        out_shape += [pltpu.SemaphoreType.DMA((len(start),))] * 2
        out_specs += [SEM_SPEC] * 2
    out_shape += [pltpu.HBM(b.shape, b.dtype) for b in bufs]
    out_specs += [HBM_SPEC] * n
    if token:
        out_shape.append(jax.ShapeDtypeStruct((8, 128), F32))
        out_specs.append(pl.BlockSpec(memory_space=pltpu.VMEM))
    outs = pl.pallas_call(
        body, name=name, in_specs=in_specs, out_specs=out_specs, out_shape=out_shape,
        input_output_aliases={i: n_start + i for i in range(n)},
        scratch_shapes=[pltpu.SemaphoreType.DMA((len(local),))] if local else [],
        compiler_params=pltpu.CompilerParams(has_side_effects=pltpu.SideEffectType.DATAFLOW_SIDE_EFFECTING),
    )(*args)
    sems = tuple(outs[:n_start]) if start else None
    return sems, list(outs[n_start:n_start + n]), (outs[n_start + n] if token else None)


def _adamw(w, g, m, v):
    m = ADAM_B1 * m + (1.0 - ADAM_B1) * g
    v = ADAM_B2 * v + (1.0 - ADAM_B2) * (g * g)
    m_hat = m / (1.0 - ADAM_B1 ** ADAM_STEP)
    v_hat = v / (1.0 - ADAM_B2 ** ADAM_STEP)
    delta = -ADAM_LR * (m_hat / (jnp.sqrt(v_hat) + ADAM_EPS) + ADAM_WD * w)
    return delta, m, v


def _reduce_adam(name, parts, w, m, v, *, tr, layer=None, into=None, after=None):
    L, R, C = w.shape
    S = parts[0].shape[0]
    tr = min(tr, R)
    n_l = L if layer is None else 1
    first = 0 if layer is None else layer

    def body(*refs):
        p_refs = refs[:n_l]
        w_ref, m_ref, v_ref = refs[n_l:n_l + 3]
        g_ref, d_ref, nm_ref, nv_ref = refs[-4:]
        for l in range(n_l):
            g = p_refs[l][0].astype(F32)
            for s in range(1, S):
                g = g + p_refs[l][s].astype(F32)
            g_ref[l] = g
            d_ref[l], nm_ref[l], nv_ref[l] = _adamw(w_ref[l], g, m_ref[l], v_ref[l])

    blk = pl.BlockSpec((n_l, tr, C), lambda r: (first, r, 0))
    out = jax.ShapeDtypeStruct((L, R, C), F32)
    extra = list(into or []) + ([after] if after is not None else [])
    return pl.pallas_call(
        body, name=name, grid=(R // tr,),
        in_specs=[pl.BlockSpec((S, tr, C), lambda r: (0, r, 0))] * n_l + [blk, blk, blk]
        + [pl.BlockSpec(memory_space=pl.ANY)] * len(extra),
        out_specs=[blk] * 4, out_shape=[out] * 4,
        input_output_aliases={n_l + 3 + i: i for i in range(4)} if into else {},
        compiler_params=_params(dimension_semantics=("arbitrary",)),
    )(*parts, w, m, v, *extra)


def _small_reduce(parts):
    D = parts.shape[2]
    rows = [0, 1, 8, 9, 16, 17, 24, 25, 18, 19, 20, 2, 32]

    def body(p_ref, out_ref):
        s = p_ref[0]
        for d in range(1, N_DEV):
            s = s + p_ref[d]
        out_ref[...] = jnp.zeros_like(out_ref)
        for r, src in enumerate(rows):
            out_ref[r:r + 1, :] = s[src:src + 1, :]

    return pl.pallas_call(body, name="small_reduce", out_shape=jax.ShapeDtypeStruct((16, D), F32))(parts)


def _small_adam(g_gain, g_taps, g_scale, gains, taps, scale):
    def body(gg, gt, gs, wg, mg, vg, wt, mt, vt, ws, ms, vs, *outs):
        for k, (g, w, m, v) in enumerate(((gg, wg, mg, vg), (gt, wt, mt, vt), (gs, ws, ms, vs))):
            outs[3 * k][...], outs[3 * k + 1][...], outs[3 * k + 2][...] = _adamw(w[...], g[...], m[...], v[...])

    shapes = [jax.ShapeDtypeStruct(t[0].shape, F32) for t in (gains, taps, scale) for _ in range(3)]
    return pl.pallas_call(body, name="small_adam", out_shape=shapes)(g_gain, g_taps, g_scale, *gains, *taps, *scale)


def kernel(x, norm_gains, pool_w, pool_scale, conv_in_w, conv_w, conv_out_w, ffn_gate_up_w, ffn_down_w, loss_target, m_norm_gains, m_pool_w, m_pool_scale, m_conv_in_w, m_conv_w, m_conv_out_w, m_ffn_gate_up_w, m_ffn_down_w, v_norm_gains, v_pool_w, v_pool_scale, v_conv_in_w, v_conv_w, v_conv_out_w, v_ffn_gate_up_w, v_ffn_down_w):
    T, D = x.shape[1], x.shape[2]
    tm = min(512, T)
    tm_b = min(256, T)
    tk = min(2048, T)
    n_layers = ffn_gate_up_w.shape[0]
    fb = ffn_gate_up_w.shape[2]
    fr = ffn_down_w.shape[1]
    dcol = norm_gains.shape[2]
    cb = conv_in_w.shape[2]
    gw = pool_w.shape[3]
    me = 4 * lax.axis_index("x") + 2 * lax.axis_index("y") + lax.axis_index("c")

    small_w = jnp.concatenate([norm_gains.reshape(8, dcol), jnp.pad(conv_w[0], ((0, 5), (0, 0)))], axis=0)
    every = range(1, N_DEV)
    wgu_t, m_wgu_t, v_wgu_t = (jnp.swapaxes(a, 1, 2) for a in (ffn_gate_up_w, m_ffn_gate_up_w, v_ffn_gate_up_w))
    own_lead = lambda s: lax.dynamic_update_slice(lax.empty((N_DEV,) + s.shape, s.dtype), s[None], (me,) + (0,) * s.ndim)
    lead_item = lambda a, l, dtype: (a, (None,) + a.shape[1:], (l, 0, 0), (N_DEV,) + a.shape[1:], dtype,
                                     (None,) + a.shape[1:], lambda i: (i, 0, 0))
    lands = _place_own(me, [
        (pool_w, (None,) + pool_w.shape[1:], (0, 0, 0, 0), (4, N_DEV, gw // N_DEV, gw), BF16,
         (4, None, gw // N_DEV, gw), lambda i: (0, i, 0, 0)),
        (small_w[None], (None,) + small_w.shape, (0, 0, 0), (N_DEV,) + small_w.shape, F32, (None,) + small_w.shape, lambda i: (i, 0, 0)),
        lead_item(wgu_t, 0, BF16), lead_item(ffn_down_w, 0, BF16), lead_item(conv_in_w, 0, BF16),
        lead_item(conv_out_w, 0, BF16), lead_item(wgu_t, 1, BF16), lead_item(ffn_down_w, 1, BF16)])
    n_first, n_big = 2, len(lands) - 2
    direct = ([Copy(m, 0, _second, 0, _second, m - 1) for m in every]
              + [Copy(m, 1, _lead, 1, _lead, N_DEV - 2 + m) for m in every])
    level1 = [Copy(mask, n_first + n, _lead, n_first + n, _lead, len(direct) + 4 * n + j)
              for n in range(n_big) for j, mask in enumerate((SIBLING,) + OTHER_CHIPS)]
    sems1, bufs1, _ = _split_call("gather_start", lands, start=direct + level1)
    pw_g, small_g = _split_call("gather_small_done", bufs1[:n_first], wait=direct, wait_sems=sems1)[1]
    pw = pw_g.reshape(4, gw, gw)
    small_full = jnp.swapaxes(small_g, 0, 1).reshape(16, D)
    gain = lambda l, s: small_full[4 * l + s][None, :]
    taps = small_full[8:16]

    def forward_on(name, group, after):
        k = len(group)
        landed = [Copy(cp.mask, i, cp.src, i, cp.dst, cp.sem)
                  for i, n in enumerate(group) for cp in level1 if cp.sb == n_first + n]
        onward = [Copy(SIBLING, i, (lambda ref, me, m=m: ref.at[me ^ m]), i, (lambda ref, sender, m=m: ref.at[sender ^ m]), 3 * i + j)
                  for i in range(k) for j, m in enumerate(OTHER_CHIPS)]
        sems2, bufs2, tok = _split_call(name + "_forward", [bufs1[n_first + n] for n in group], wait=landed,
                                        wait_sems=sems1, start=onward, after=after, token=True)
        return (name, onward, sems2, bufs2), tok

    def arrived(state, after=None):
        name, onward, sems2, lands2 = state
        return _split_call(name + "_done", lands2, wait=onward, wait_sems=sems2, after=after)[1]

    h0 = x[0]
    h1 = _pool_fwd(h0, pw, pool_scale, gain(0, 0), gain(0, 1), tm=tm)
    (wgu0,) = arrived(forward_on("gather_gate_up_0", [0], h1)[0])
    gu0, act0 = _ffn_up(h1, gain(0, 2), wgu0, tm=tm)
    ag_down0, tok = forward_on("gather_down_0", [1], act0)
    ag_conv, tok = forward_on("gather_conv", [2, 3], tok)
    (wd0,) = arrived(ag_down0, tok)
    wd0 = wd0.reshape(N_DEV * fr, D)
    h2, ff0 = _ffn_down(h1, act0, gain(0, 3), wd0, tm=tm)
    win_g, wout_g = arrived(ag_conv, h2)
    wout = wout_g.reshape(D, D)
    h3, proj, y = _conv_fwd(h2, gain(1, 0), gain(1, 1), win_g, taps, wout, tm=tm)
    ag_ffn1, tok = forward_on("gather_ffn1", [4, 5], h3)
    wgu1, wd1 = arrived(ag_ffn1, tok)
    wd1 = wd1.reshape(N_DEV * fr, D)
    dh4, gu1, ff1, loss_part = _ffn_fwd(h3, gain(1, 2), gain(1, 3), wgu1, wd1, 1, loss_target[0], tm=tm)

    chip = me >> 1

    def scatter_start(name, sums):
        k = len(sums)
        lands = [lax.dynamic_update_slice(lax.empty(s.shape, BF16), lax.dynamic_index_in_dim(s, chip, 0), (chip, 0, 0))
                 for s in sums]
        plan = [Copy(m, n, (lambda ref, i, m=m: ref.at[(i ^ m) >> 1]), k + n, (lambda ref, i: ref.at[i >> 1]), 3 * n + j)
                for n in range(k) for j, m in enumerate(OTHER_CHIPS)]
        sems, bufs, tok = _split_call(name + "_start", sums + lands, start=plan, token=True)
        return (name, plan, sems, bufs), tok

    def scatter_done(state, after):
        name, plan, sems, bufs = state
        return _split_call(name + "_done", bufs, wait=plan, wait_sems=sems, after=after)[1][len(bufs) // 2:]

    seq = lambda i, k: (k, 0)
    by_block = pl.BlockSpec((None, tk, fb), lambda i, k: (i, k, 0))
    rows = pl.BlockSpec((tk, D), seq)
    dh3, dgu1, dff1, c1, act1, small_f1 = _ffn_bwd(dh4, h3, ff1, gu1, gain(1, 2), gain(1, 3), wgu1, wd1, 1, tm=tm_b)
    g_wgu1 = _wgrad("wgrad_gate_up_1", dgu1, c1, by_block, rows, (fb, D), N_DEV, tk=tk)
    g_wd1 = _wgrad("wgrad_down_1", act1, dff1, by_block, rows, (fb, D), N_DEV // 2, tk=tk)
    rs_ffn1, tok = scatter_start("scatter_ffn1", [g_wgu1, g_wd1])
    dh2, dproj, z, a1, dy, small_c = _conv_bwd(dh3, h2, y, proj, gain(1, 0), gain(1, 1), win_g, taps, wout, tm=tm, after=tok)
    g_win = _wgrad("wgrad_conv_in", dproj, a1, pl.BlockSpec((tk, cb), lambda i, k: (k, i)), rows, (D, cb), N_DEV,
                   tk=tk, transpose=True)
    g_wout = _wgrad("wgrad_conv_out", z, dy, rows, rows, (D, D), 1, tk=tk)
    rs_conv, tok = scatter_start("scatter_conv", [g_win, g_wout])
    dh1, dgu0, dff0, c0, act0, small_f0 = _ffn_bwd(dh2, h1, ff0, gu0, gain(0, 2), gain(0, 3), wgu0, wd0, 0, tm=tm_b, after=tok)
    g_wgu0 = _wgrad("wgrad_gate_up_0", dgu0, c0, by_block, rows, (fb, D), N_DEV, tk=tk)
    rs_wgu0, tok = scatter_start("scatter_gate_up_0", [g_wgu0])
    g_wd0 = _wgrad("wgrad_down_0", act0, dff0, by_block, rows, (fb, D), N_DEV // 2, tk=tk, after=tok)
    rs_wd0, tok = scatter_start("scatter_down_0", [g_wd0])
    grad_x, g_pw, small_p = _pool_bwd(dh1, h0, pw, pool_scale, gain(0, 0), gain(0, 1), tm=tm, after=tok)

    loss_rows = jnp.broadcast_to(loss_part[0:1, 0:1], (8, D))
    small_part = jnp.concatenate([small_p, small_f0, small_c, small_f1, loss_rows], axis=0)
    g_pw = g_pw.reshape(4, N_DEV, gw // N_DEV, gw)
    pw_land = lax.dynamic_update_slice(lax.empty(g_pw.shape, BF16), lax.dynamic_slice_in_dim(g_pw, me, 1, 1), (0, me, 0, 0))
    last = ([Copy(m, 0, (lambda ref, i, m=m: ref.at[:, i ^ m]), 2, _second, m - 1) for m in every]
            + [Copy(m, 1, _whole, 3, _lead, N_DEV - 2 + m) for m in every])
    sems_l, bufs_l, tok = _split_call("scatter_small_start", [g_pw, small_part, pw_land, own_lead(small_part)],
                                      start=last, token=True)

    (r_wgu1, r_wd1), (r_win, r_wout) = scatter_done(rs_ffn1, tok), scatter_done(rs_conv, tok)
    o_win = _reduce_adam("adam_conv_in", [r_win], conv_in_w, m_conv_in_w, v_conv_in_w, tr=256)
    o_wout = _reduce_adam("adam_conv_out", [r_wout], conv_out_w, m_conv_out_w, v_conv_out_w, tr=128, after=o_win[0])
    o_wgu = _reduce_adam("adam_gate_up_1", [r_wgu1], wgu_t, m_wgu_t, v_wgu_t, tr=176, layer=1, after=o_wout[0])
    o_wd = _reduce_adam("adam_down_1", [r_wd1], ffn_down_w, m_ffn_down_w, v_ffn_down_w, tr=176, layer=1, after=o_wgu[0])
    r_pw, r_small = _split_call("scatter_small_done", bufs_l, wait=last, wait_sems=sems_l, after=o_wd[0])[1][2:]
    (r_wgu0,), (r_wd0,) = scatter_done(rs_wgu0, r_small), scatter_done(rs_wd0, r_small)
    o_wgu = _reduce_adam("adam_gate_up_0", [r_wgu0], wgu_t, m_wgu_t, v_wgu_t, tr=176, layer=0, into=o_wgu)
    o_wgu = [jnp.swapaxes(o, 1, 2) for o in o_wgu]
    o_wd = _reduce_adam("adam_down_0", [r_wd0], ffn_down_w, m_ffn_down_w, v_ffn_down_w, tr=176, layer=0, into=o_wd)
    o_pw = _reduce_adam("adam_pool_w", [r_pw[g] for g in range(4)], pool_w[0], m_pool_w[0], v_pool_w[0], tr=32)
    g_small = _small_reduce(r_small)
    loss = g_small[12, 0]
    g_cols = lax.dynamic_slice(g_small, (0, me * dcol), (16, dcol))
    o_small = _small_adam(
        g_cols[0:8], g_cols[8:11], g_small[11:12],
        (norm_gains.reshape(8, dcol), m_norm_gains.reshape(8, dcol), v_norm_gains.reshape(8, dcol)),
        (conv_w[0], m_conv_w[0], v_conv_w[0]), (pool_scale, m_pool_scale, v_pool_scale))
    d_gain, nm_gain, nv_gain, d_taps, nm_taps, nv_taps, d_scale, nm_scale, nv_scale = o_small

    gshape = norm_gains.shape
    per = lambda k: (
        (g_cols[0:8].reshape(gshape), d_gain.reshape(gshape), nm_gain.reshape(gshape), nv_gain.reshape(gshape))[k],
        o_pw[k][None], (g_small[11:12], d_scale, nm_scale, nv_scale)[k], o_win[k],
        (g_cols[8:11][None], d_taps[None], nm_taps[None], nv_taps[None])[k], o_wout[k], o_wgu[k], o_wd[k])
    return (loss, grad_x[None], *per(0), *per(1), *per(2), *per(3))
```

```python
import collections
import functools

import jax
import jax.numpy as jnp
from jax import lax
from jax.experimental import pallas as pl
from jax.experimental.pallas import tpu as pltpu

N_DEV = 8
RMS_EPS = 1e-6
POOL_WINDOWS = (2, 4, 8, 16)
POOL_HALO = 16
CONV_HALO = 16
ADAM_LR, ADAM_B1, ADAM_B2, ADAM_EPS, ADAM_WD, ADAM_STEP = 0.001, 0.9, 0.999, 1e-08, 0.01, 10

VMEM_LIMIT = 56 * 2**20
BF16 = jnp.bfloat16
F32 = jnp.float32
MESH = pl.DeviceIdType.MESH
SIBLING_PAIR_ID = 0


def _params(**kw):
    return pltpu.CompilerParams(vmem_limit_bytes=VMEM_LIMIT, **kw)


def _resident(shape, index_map):
    return pl.BlockSpec(shape, index_map, pipeline_mode=pl.Buffered(1))


def _ordered(body, n_in, after):
    if after is None:
        return functools.partial(body), [], []
    return (lambda *refs: body(*refs[:n_in], *refs[n_in + 1:])), [after], [pl.BlockSpec(memory_space=pl.ANY)]


def _rms_fwd(x, g):
    r = lax.rsqrt(jnp.mean(x * x, axis=-1, keepdims=True) + RMS_EPS)
    return x * r * g


def _rms_bwd(x, g, dy):
    r = lax.rsqrt(jnp.mean(x * x, axis=-1, keepdims=True) + RMS_EPS)
    xhat = x * r
    dg = jnp.sum(dy * xhat, axis=0, keepdims=True)
    t = dy * g
    dx = r * (t - xhat * jnp.mean(t * xhat, axis=-1, keepdims=True))
    return dx, dg


def _sigmoid(x):
    return 0.5 * jnp.tanh(0.5 * x) + 0.5


def _dot(a, b):
    return jnp.dot(a, b, preferred_element_type=F32)


def _dot_nt(a, b):
    return lax.dot_general(a, b, (((1,), (1,)), ((), ())), preferred_element_type=F32)


def _dot_tn(a, b):
    return lax.dot_general(a, b, (((0,), (0,)), ((), ())), preferred_element_type=F32)


def _join_blocks(blocks_hbm, joined_ref, sems):
    n, _, C = blocks_hbm.shape
    copies = [pltpu.make_async_copy(blocks_hbm.at[k], joined_ref.at[:, k * C:(k + 1) * C], sems.at[k]) for k in range(n)]
    for cp in copies:
        cp.start()
    for cp in copies:
        cp.wait()


def _row_inverse_counts(tile, tm):
    pos = (lax.broadcasted_iota(jnp.int32, (tm, 1), 0) + tile * tm + 1).astype(F32)
    return [1.0 / jnp.minimum(pos, float(w)) for w in POOL_WINDOWS]


def _pool_from_ext(ext, a, invs, gw):
    s = ext
    outs = []
    for g, w in enumerate(POOL_WINDOWS):
        s = s[:, (gw if g else 0):]
        s = s + pltpu.roll(s, w // 2, 0)
        outs.append(s[POOL_HALO:, :gw] * invs[g] - a[:, g * gw:(g + 1) * gw])
    return outs


def _pool_fwd(h, pw, scale, g_pre, g_post, *, tm, after=None):
    T, D = h.shape
    gw = D // len(POOL_WINDOWS)
    hb = tm // POOL_HALO

    def body(h_ref, halo_ref, pw_ref, scale_ref, gpre_ref, gpost_ref, out_ref, ext_ref):
        i = pl.program_id(0)
        x = h_ref[...]
        a = _rms_fwd(x, gpre_ref[...])
        ah = _rms_fwd(halo_ref[...], gpre_ref[...])
        ext_ref[0:POOL_HALO, :] = jnp.where(i == 0, 0.0, ah)
        ext_ref[POOL_HALO:, :] = a
        pooled = _pool_from_ext(ext_ref[...], a, _row_inverse_counts(i, tm), gw)
        mixed = jnp.concatenate([_dot(p.astype(BF16), pw_ref[g]) for g, p in enumerate(pooled)], axis=1)
        out_ref[...] = x + _rms_fwd(mixed * scale_ref[...], gpost_ref[...])

    vec = _resident((1, D), lambda i: (0, 0))
    fn, xa, xs = _ordered(body, 6, after)
    return pl.pallas_call(
        fn, name="pool_fwd", grid=(T // tm,),
        in_specs=[pl.BlockSpec((tm, D), lambda i: (i, 0)),
                  pl.BlockSpec((POOL_HALO, D), lambda i: (jnp.maximum(i * hb - 1, 0), 0)),
                  _resident(pw.shape, lambda i: (0, 0, 0)), vec, vec, vec] + xs,
        out_specs=pl.BlockSpec((tm, D), lambda i: (i, 0)),
        out_shape=jax.ShapeDtypeStruct((T, D), F32),
        scratch_shapes=[pltpu.VMEM((tm + POOL_HALO, D), F32)],
        compiler_params=_params(dimension_semantics=("arbitrary",)),
    )(h, h, pw, scale, g_pre, g_post, *xa)


def _pool_bwd(dh, h, pw, scale, g_pre, g_post, *, tm, after=None):
    T, D = h.shape
    gw = D // len(POOL_WINDOWS)
    hb = tm // POOL_HALO
    nt = T // tm
    n_ext = tm + POOL_HALO

    def body(dh_ref, h_ref, halo_ref, pw_ref, scale_ref, gpre_ref, gpost_ref,
             dx_ref, dpw_ref, small_ref, ext_ref, ext2_ref, carry_ref, dpw_acc):
        i = pl.program_id(0)
        tile = nt - 1 - i

        @pl.when(i == 0)
        def _():
            small_ref[...] = jnp.zeros_like(small_ref)
            dpw_acc[...] = jnp.zeros_like(dpw_acc)
            carry_ref[...] = jnp.zeros_like(carry_ref)

        x = h_ref[...]
        dout = dh_ref[...]
        a = _rms_fwd(x, gpre_ref[...])
        ah = _rms_fwd(halo_ref[...], gpre_ref[...])
        ext_ref[0:POOL_HALO, :] = jnp.where(tile == 0, 0.0, ah)
        ext_ref[POOL_HALO:, :] = a
        invs = _row_inverse_counts(tile, tm)
        pooled = [p.astype(BF16) for p in _pool_from_ext(ext_ref[...], a, invs, gw)]
        mixed_pre = jnp.concatenate([_dot(p, pw_ref[g]) for g, p in enumerate(pooled)], axis=1)
        scale_v = scale_ref[...]
        dmixed, dg_post = _rms_bwd(mixed_pre * scale_v, gpost_ref[...], dout)
        small_ref[1:2, :] += dg_post
        small_ref[2:3, :] += jnp.sum(dmixed * mixed_pre, axis=0, keepdims=True)
        dpre = (dmixed * scale_v).astype(BF16)
        dpooled = []
        for g in range(len(POOL_WINDOWS)):
            dp = dpre[:, g * gw:(g + 1) * gw]
            dpw_acc[g] += _dot_tn(pooled[g], dp)
            dpooled.append(_dot_nt(dp, pw_ref[g]))
        q = jnp.concatenate([d * invs[g] for g, d in enumerate(dpooled)], axis=1)
        ext2_ref[0:tm, :] = q
        ext2_ref[tm:, :] = carry_ref[...]
        carry_ref[...] = q[0:POOL_HALO, :]
        s = ext2_ref[...]
        da = []
        for g, w in enumerate(POOL_WINDOWS):
            s = s[:, (gw if g else 0):]
            s = s + pltpu.roll(s, n_ext - w // 2, 0)
            da.append(s[0:tm, :gw] - dpooled[g])
        dx, dg_pre = _rms_bwd(x, gpre_ref[...], jnp.concatenate(da, axis=1))
        small_ref[0:1, :] += dg_pre
        dx_ref[...] = dout + dx

        @pl.when(i == nt - 1)
        def _():
            dpw_ref[...] = dpw_acc[...].astype(BF16)

    vec = _resident((1, D), lambda i: (0, 0))
    rev = lambda i: (nt - 1 - i, 0)
    fn, xa, xs = _ordered(body, 7, after)
    return pl.pallas_call(
        fn, name="pool_bwd", grid=(nt,),
        in_specs=[pl.BlockSpec((tm, D), rev), pl.BlockSpec((tm, D), rev),
                  pl.BlockSpec((POOL_HALO, D), lambda i: (jnp.maximum((nt - 1 - i) * hb - 1, 0), 0)),
                  _resident(pw.shape, lambda i: (0, 0, 0)), vec, vec, vec] + xs,
        out_specs=[pl.BlockSpec((tm, D), rev),
                   pl.BlockSpec(pw.shape, lambda i: (0, 0, 0)),
                   pl.BlockSpec((8, D), lambda i: (0, 0))],
        out_shape=[jax.ShapeDtypeStruct((T, D), F32), jax.ShapeDtypeStruct(pw.shape, BF16),
                   jax.ShapeDtypeStruct((8, D), F32)],
        scratch_shapes=[pltpu.VMEM((n_ext, D), F32), pltpu.VMEM((n_ext, D), F32),
                        pltpu.VMEM((POOL_HALO, D), F32), pltpu.VMEM(pw.shape, F32)],
        compiler_params=_params(dimension_semantics=("arbitrary",)),
    )(dh, h, h, pw, scale, g_pre, g_post, *xa)


def _ffn_fwd(h, g_pre, g_post, wgu, wd, layer, target, *, tm, after=None):
    T, D = h.shape
    nblk, fb = wgu.shape[0], wgu.shape[1]
    half = nblk // 2
    last = target is not None

    def body(*refs):
        if last:
            h_ref, gpre_ref, gpost_ref, wgu_ref, wd_ref, tgt_ref, out_ref, gu_ref, ff_ref, loss_ref = refs
        else:
            h_ref, gpre_ref, gpost_ref, wgu_ref, wd_ref, out_ref, gu_ref, ff_ref = refs
        x = h_ref[...]
        cb = _rms_fwd(x, gpre_ref[...]).astype(BF16)
        acc = jnp.zeros((tm, D), F32)
        for j in range(half):
            g = _dot_nt(cb, wgu_ref[j])
            u = _dot_nt(cb, wgu_ref[j + half])
            gu_ref[j] = g.astype(BF16)
            gu_ref[j + half] = u.astype(BF16)
            act = (g * _sigmoid(g) * u).astype(BF16)
            acc = acc + _dot(act, wd_ref[j * fb:(j + 1) * fb, :])
        ff_ref[...] = acc.astype(BF16)
        hout = x + _rms_fwd(acc, gpost_ref[...])
        if last:
            diff = hout - tgt_ref[...]
            out_ref[...] = diff * (1.0 / D)

            @pl.when(pl.program_id(0) == 0)
            def _():
                loss_ref[...] = jnp.zeros_like(loss_ref)

            loss_ref[...] += jnp.sum(diff * diff) * (0.5 / D)
        else:
            out_ref[...] = hout

    vec = _resident((1, D), lambda i: (0, 0))
    tile = pl.BlockSpec((tm, D), lambda i: (i, 0))
    in_specs = [tile, vec, vec,
                _resident(wgu.shape, lambda i: (0, 0, 0)), _resident(wd.shape, lambda i: (0, 0))]
    out_specs = [tile, pl.BlockSpec((nblk, tm, fb), lambda i: (0, i, 0)), tile]
    out_shape = [jax.ShapeDtypeStruct((T, D), F32), jax.ShapeDtypeStruct((nblk, T, fb), BF16),
                 jax.ShapeDtypeStruct((T, D), BF16)]
    args = [h, g_pre, g_post, wgu, wd]
    if last:
        in_specs.append(tile)
        args.append(target)
        out_specs.append(pl.BlockSpec((8, 128), lambda i: (0, 0)))
        out_shape.append(jax.ShapeDtypeStruct((8, 128), F32))
    fn, xa, xs = _ordered(body, len(args), after)
    return pl.pallas_call(
        fn, name=f"ffn_fwd_{layer}", grid=(T // tm,), in_specs=in_specs + xs, out_specs=out_specs,
        out_shape=out_shape, compiler_params=_params(dimension_semantics=("arbitrary",)),
    )(*args, *xa)


def _ffn_up(h, g_pre, wgu, *, tm):
    T, D = h.shape
    nblk, fb = wgu.shape[0], wgu.shape[1]
    half = nblk // 2

    def body(h_ref, gpre_ref, wgu_ref, gu_ref, act_ref):
        cb = _rms_fwd(h_ref[...], gpre_ref[...]).astype(BF16)
        for j in range(half):
            g = _dot_nt(cb, wgu_ref[j])
            u = _dot_nt(cb, wgu_ref[j + half])
            gu_ref[j] = g.astype(BF16)
            gu_ref[j + half] = u.astype(BF16)
            act_ref[j] = (g * _sigmoid(g) * u).astype(BF16)

    return pl.pallas_call(
        body, name="ffn_up_0", grid=(T // tm,),
        in_specs=[pl.BlockSpec((tm, D), lambda i: (i, 0)), _resident((1, D), lambda i: (0, 0)),
                  _resident(wgu.shape, lambda i: (0, 0, 0))],
        out_specs=[pl.BlockSpec((nblk, tm, fb), lambda i: (0, i, 0)), pl.BlockSpec((half, tm, fb), lambda i: (0, i, 0))],
        out_shape=[jax.ShapeDtypeStruct((nblk, T, fb), BF16), jax.ShapeDtypeStruct((half, T, fb), BF16)],
        compiler_params=_params(dimension_semantics=("arbitrary",)),
    )(h, g_pre, wgu)


def _ffn_down(h, act, g_post, wd, *, tm):
    T, D = h.shape
    half, fb = act.shape[0], act.shape[2]

    def body(h_ref, act_ref, gpost_ref, wd_ref, out_ref, ff_ref):
        acc = _dot(act_ref[0], wd_ref[0:fb, :])
        for j in range(1, half):
            acc = acc + _dot(act_ref[j], wd_ref[j * fb:(j + 1) * fb, :])
        ff_ref[...] = acc.astype(BF16)
        out_ref[...] = h_ref[...] + _rms_fwd(acc, gpost_ref[...])

    tile = pl.BlockSpec((tm, D), lambda i: (i, 0))
    return pl.pallas_call(
        body, name="ffn_down_0", grid=(T // tm,),
        in_specs=[tile, pl.BlockSpec((half, tm, fb), lambda i: (0, i, 0)), _resident((1, D), lambda i: (0, 0)),
                  _resident(wd.shape, lambda i: (0, 0))],
        out_specs=[tile, tile],
        out_shape=[jax.ShapeDtypeStruct((T, D), F32), jax.ShapeDtypeStruct((T, D), BF16)],
        compiler_params=_params(dimension_semantics=("arbitrary",)),
    )(h, act, g_post, wd)


def _ffn_bwd(dh, h, ff, gu, g_pre, g_post, wgu, wd, layer, *, tm, after=None):
    T, D = h.shape
    nblk, fb = wgu.shape[0], wgu.shape[1]
    half = nblk // 2

    def body(dh_ref, h_ref, ff_ref, gu_ref, gpre_ref, gpost_ref, wgu_ref, wd_ref,
             dx_ref, dgu_ref, dff_ref, c_ref, act_ref, small_ref):
        @pl.when(pl.program_id(0) == 0)
        def _():
            small_ref[...] = jnp.zeros_like(small_ref)

        dout = dh_ref[...]
        dff, dg_post = _rms_bwd(ff_ref[...].astype(F32), gpost_ref[...], dout)
        small_ref[1:2, :] += dg_post
        dffb = dff.astype(BF16)
        dff_ref[...] = dffb
        dc = jnp.zeros((tm, D), F32)
        last = None
        for j in range(half + 1):
            if j < half:
                dact = _dot_nt(dffb, wd_ref[j * fb:(j + 1) * fb, :])
            if last is not None:
                dc = dc + _dot(last[0], wgu_ref[j - 1]) + _dot(last[1], wgu_ref[j - 1 + half])
            if j < half:
                g = gu_ref[j].astype(F32)
                u = gu_ref[j + half].astype(F32)
                s = _sigmoid(g)
                silu = g * s
                act_ref[j] = (silu * u).astype(BF16)
                dg = (dact * u * (s * (1.0 + g * (1.0 - s)))).astype(BF16)
                du = (dact * silu).astype(BF16)
                dgu_ref[j] = dg
                dgu_ref[j + half] = du
                last = (dg, du)
        x = h_ref[...]
        c_ref[...] = _rms_fwd(x, gpre_ref[...]).astype(BF16)
        dx, dg_pre = _rms_bwd(x, gpre_ref[...], dc)
        small_ref[0:1, :] += dg_pre
        dx_ref[...] = dout + dx

    vec = _resident((1, D), lambda i: (0, 0))
    tile = pl.BlockSpec((tm, D), lambda i: (i, 0))
    blk = pl.BlockSpec((nblk, tm, fb), lambda i: (0, i, 0))
    fn, xa, xs = _ordered(body, 8, after)
    return pl.pallas_call(
        fn, name=f"ffn_bwd_{layer}", grid=(T // tm,),
        in_specs=[tile, tile, tile, blk, vec, vec,
                  _resident(wgu.shape, lambda i: (0, 0, 0)), _resident(wd.shape, lambda i: (0, 0))] + xs,
        out_specs=[tile, blk, tile, tile, pl.BlockSpec((half, tm, fb), lambda i: (0, i, 0)),
                   pl.BlockSpec((8, D), lambda i: (0, 0))],
        out_shape=[jax.ShapeDtypeStruct((T, D), F32), jax.ShapeDtypeStruct((nblk, T, fb), BF16),
                   jax.ShapeDtypeStruct((T, D), BF16), jax.ShapeDtypeStruct((T, D), BF16),
                   jax.ShapeDtypeStruct((half, T, fb), BF16), jax.ShapeDtypeStruct((8, D), F32)],
        compiler_params=_params(dimension_semantics=("arbitrary",)),
    )(dh, h, ff, gu, g_pre, g_post, wgu, wd, *xa)


def _conv_fwd(h, g_pre, g_post, win, taps, wout, *, tm, after=None):
    T, D = h.shape
    nblk, cb = win.shape[0], win.shape[2]

    def body(h_ref, gpre_ref, gpost_ref, win_hbm, taps_ref, wout_ref,
             out_ref, proj_ref, y_ref, proj_scr, ext_ref, carry_ref, win_ref, win_sems):
        i = pl.program_id(0)

        @pl.when(i == 0)
        def _():
            carry_ref[...] = jnp.zeros_like(carry_ref)
            _join_blocks(win_hbm, win_ref, win_sems)

        x = h_ref[...]
        a = _rms_fwd(x, gpre_ref[...]).astype(BF16)
        proj_scr[...] = _dot(a, win_ref[...])
        proj_ref[...] = proj_scr[...].astype(BF16)
        u = proj_scr[:, D:2 * D] * proj_scr[:, 2 * D:3 * D]
        ext_ref[0:CONV_HALO, :] = carry_ref[...]
        ext_ref[CONV_HALO:, :] = u
        carry_ref[...] = u[tm - CONV_HALO:, :]
        e = ext_ref[...]
        conv = (taps_ref[2:3, :] * u + taps_ref[1:2, :] * pltpu.roll(e, 1, 0)[CONV_HALO:, :]
                + taps_ref[0:1, :] * pltpu.roll(e, 2, 0)[CONV_HALO:, :])
        z = (proj_scr[:, 0:D] * conv).astype(BF16)
        y = _dot(z, wout_ref[...])
        y_ref[...] = y.astype(BF16)
        out_ref[...] = x + _rms_fwd(y, gpost_ref[...])

    vec = _resident((1, D), lambda i: (0, 0))
    tile = pl.BlockSpec((tm, D), lambda i: (i, 0))
    fn, xa, xs = _ordered(body, 6, after)
    return pl.pallas_call(
        fn, name="conv_fwd", grid=(T // tm,),
        in_specs=[tile, vec, vec, pl.BlockSpec(memory_space=pl.ANY),
                  _resident(taps.shape, lambda i: (0, 0)), _resident(wout.shape, lambda i: (0, 0))] + xs,
        out_specs=[tile, pl.BlockSpec((tm, 3 * D), lambda i: (i, 0)), tile],
        out_shape=[jax.ShapeDtypeStruct((T, D), F32), jax.ShapeDtypeStruct((T, 3 * D), BF16),
                   jax.ShapeDtypeStruct((T, D), BF16)],
        scratch_shapes=[pltpu.VMEM((tm, 3 * D), F32), pltpu.VMEM((tm + CONV_HALO, D), F32),
                        pltpu.VMEM((CONV_HALO, D), F32), pltpu.VMEM((D, nblk * cb), BF16),
                        pltpu.SemaphoreType.DMA((nblk,))],
        compiler_params=_params(dimension_semantics=("arbitrary",)),
    )(h, g_pre, g_post, win, taps, wout, *xa)


def _conv_bwd(dh, h, y, proj, g_pre, g_post, win, taps, wout, *, tm, after=None):
    T, D = h.shape
    nblk, cb = win.shape[0], win.shape[2]
    nt = T // tm
    hb = tm // CONV_HALO
    n_ext = tm + CONV_HALO

    def body(dh_ref, h_ref, y_ref, proj_ref, halo_ref, gpre_ref, gpost_ref, win_hbm, taps_ref, wout_ref,
             dx_ref, dproj_ref, z_ref, a_ref, dy_ref, small_ref, ext_ref, ext2_ref, carry_ref, win_ref, win_sems):
        i = pl.program_id(0)
        tile = nt - 1 - i

        @pl.when(i == 0)
        def _():
            small_ref[...] = jnp.zeros_like(small_ref)
            carry_ref[...] = jnp.zeros_like(carry_ref)
            _join_blocks(win_hbm, win_ref, win_sems)

        dout = dh_ref[...]
        dy, dg_post = _rms_bwd(y_ref[...].astype(F32), gpost_ref[...], dout)
        small_ref[1:2, :] += dg_post
        dyb = dy.astype(BF16)
        dy_ref[...] = dyb
        dz = _dot_nt(dyb, wout_ref[...])
        bgate = proj_ref[:, 0:D].astype(F32)
        cgate = proj_ref[:, D:2 * D].astype(F32)
        v = proj_ref[:, 2 * D:3 * D].astype(F32)
        u = cgate * v
        uh = halo_ref[:, D:2 * D].astype(F32) * halo_ref[:, 2 * D:3 * D].astype(F32)
        ext_ref[0:CONV_HALO, :] = jnp.where(tile == 0, 0.0, uh)
        ext_ref[CONV_HALO:, :] = u
        e = ext_ref[...]
        u1 = pltpu.roll(e, 1, 0)[CONV_HALO:, :]
        u2 = pltpu.roll(e, 2, 0)[CONV_HALO:, :]
        t0, t1, t2 = taps_ref[0:1, :], taps_ref[1:2, :], taps_ref[2:3, :]
        conv = t2 * u + t1 * u1 + t0 * u2
        z_ref[...] = (bgate * conv).astype(BF16)
        dconv = dz * bgate
        small_ref[2:3, :] += jnp.sum(dconv * u2, axis=0, keepdims=True)
        small_ref[3:4, :] += jnp.sum(dconv * u1, axis=0, keepdims=True)
        small_ref[4:5, :] += jnp.sum(dconv * u, axis=0, keepdims=True)
        ext2_ref[0:tm, :] = dconv
        ext2_ref[tm:, :] = carry_ref[...]
        carry_ref[...] = dconv[0:CONV_HALO, :]
        e2 = ext2_ref[...]
        du = (t2 * dconv + t1 * pltpu.roll(e2, n_ext - 1, 0)[0:tm, :]
              + t0 * pltpu.roll(e2, n_ext - 2, 0)[0:tm, :])
        dproj_ref[:, 0:D] = (dz * conv).astype(BF16)
        dproj_ref[:, D:2 * D] = (du * v).astype(BF16)
        dproj_ref[:, 2 * D:3 * D] = (du * cgate).astype(BF16)
        da = _dot_nt(dproj_ref[...], win_ref[...])
        x = h_ref[...]
        a_ref[...] = _rms_fwd(x, gpre_ref[...]).astype(BF16)
        dx, dg_pre = _rms_bwd(x, gpre_ref[...], da)
        small_ref[0:1, :] += dg_pre
        dx_ref[...] = dout + dx

    vec = _resident((1, D), lambda i: (0, 0))
    rev = lambda i: (nt - 1 - i, 0)
    tile = pl.BlockSpec((tm, D), rev)
    wide = pl.BlockSpec((tm, 3 * D), rev)
    fn, xa, xs = _ordered(body, 10, after)
    return pl.pallas_call(
        fn, name="conv_bwd", grid=(nt,),
        in_specs=[tile, tile, tile, wide,
                  pl.BlockSpec((CONV_HALO, 3 * D), lambda i: (jnp.maximum((nt - 1 - i) * hb - 1, 0), 0)),
                  vec, vec, pl.BlockSpec(memory_space=pl.ANY),
                  _resident(taps.shape, lambda i: (0, 0)), _resident(wout.shape, lambda i: (0, 0))] + xs,
        out_specs=[tile, wide, tile, tile, tile, pl.BlockSpec((8, D), lambda i: (0, 0))],
        out_shape=[jax.ShapeDtypeStruct((T, D), F32), jax.ShapeDtypeStruct((T, 3 * D), BF16),
                   jax.ShapeDtypeStruct((T, D), BF16), jax.ShapeDtypeStruct((T, D), BF16),
                   jax.ShapeDtypeStruct((T, D), BF16), jax.ShapeDtypeStruct((8, D), F32)],
        scratch_shapes=[pltpu.VMEM((n_ext, D), F32), pltpu.VMEM((n_ext, D), F32),
                        pltpu.VMEM((CONV_HALO, D), F32), pltpu.VMEM((D, nblk * cb), BF16),
                        pltpu.SemaphoreType.DMA((nblk,))],
        compiler_params=_params(dimension_semantics=("arbitrary",)),
    )(dh, h, y, proj, proj, g_pre, g_post, win, taps, wout, *xa)


def _wgrad(name, a, b, a_spec, b_spec, block, n_blocks, *, tk, transpose=False, after=None):
    T = a.shape[-2]
    nk = T // tk
    M, N = block
    m = N_DEV // n_blocks
    R = M // m
    acc_block = block[::-1] if transpose else block

    def body(a_ref, b_ref, out_ref, acc_ref, stage_ref, recv_ref, send_sems, recv_sems):
        i, k = pl.program_id(0), pl.program_id(1)
        x, y, c = lax.axis_index("x"), lax.axis_index("y"), lax.axis_index("c")

        def sent(blk, p):
            owner = blk * m + p
            q = owner // 2
            return (owner % 2) != c, pltpu.make_async_remote_copy(
                src_ref=stage_ref.at[p * R:(p + 1) * R], dst_ref=recv_ref.at[q], send_sem=send_sems.at[q],
                recv_sem=recv_sems.at[q], device_id=(x, y, 1 - c), device_id_type=MESH)

        @pl.when(jnp.logical_and(i == 0, k == 0))
        def _():
            barrier = pltpu.get_barrier_semaphore()
            pl.semaphore_signal(barrier, inc=1, device_id=(x, y, 1 - c), device_id_type=MESH)
            pl.semaphore_wait(barrier, 1)

        @pl.when(k == 0)
        def _():
            acc_ref[...] = jnp.zeros_like(acc_ref)

        acc_ref[...] += _dot_tn(a_ref[...], b_ref[...])

        @pl.when(k == nk - 1)
        def _():
            for p in range(m):
                away, copy = sent(jnp.maximum(i - 1, 0), p)

                @pl.when(jnp.logical_and(i > 0, away))
                def _():
                    copy.wait_send()

            acc = acc_ref[...]
            stage_ref[...] = (acc.T if transpose else acc).astype(BF16)
            for p in range(m):
                away, copy = sent(i, p)

                @pl.when(away)
                def _():
                    copy.start()

                @pl.when(jnp.logical_not(away))
                def _():
                    out_ref[(i * m + p) // 2] = stage_ref[p * R:(p + 1) * R, :]

        @pl.when(jnp.logical_and(i == n_blocks - 1, k == nk - 1))
        def _():
            for p in range(m):
                away, copy = sent(i, p)

                @pl.when(away)
                def _():
                    copy.wait_send()

            for q in range(N_DEV // 2):
                pltpu.make_async_remote_copy(
                    src_ref=stage_ref.at[0:R], dst_ref=recv_ref.at[q], send_sem=send_sems.at[q],
                    recv_sem=recv_sems.at[q], device_id=(x, y, 1 - c), device_id_type=MESH).wait_recv()
                out_ref[q] = (out_ref[q].astype(F32) + recv_ref[q].astype(F32)).astype(BF16)

    fn, xa, xs = _ordered(body, 2, after)
    return pl.pallas_call(
        fn, name=name, grid=(n_blocks, nk), in_specs=[a_spec, b_spec] + xs,
        out_specs=pl.BlockSpec((N_DEV // 2, R, N), lambda i, k: (0, 0, 0)),
        out_shape=jax.ShapeDtypeStruct((N_DEV // 2, R, N), BF16),
        scratch_shapes=[pltpu.VMEM(acc_block, F32), pltpu.VMEM(block, BF16), pltpu.VMEM((N_DEV // 2, R, N), BF16),
                        pltpu.SemaphoreType.DMA((N_DEV // 2,)), pltpu.SemaphoreType.DMA((N_DEV // 2,))],
        compiler_params=_params(dimension_semantics=("arbitrary", "arbitrary"), collective_id=SIBLING_PAIR_ID),
    )(a, b, *xa)


Copy = collections.namedtuple("Copy", "mask sb src db dst sem")
Local = collections.namedtuple("Local", "sb src db dst")

HBM_SPEC = pl.BlockSpec(memory_space=pltpu.HBM)
SEM_SPEC = pl.BlockSpec(memory_space=pltpu.SEMAPHORE)
SIBLING, X_PEER, Y_PEER, DIAGONAL = 1, 4, 2, 6
OTHER_CHIPS = (X_PEER, Y_PEER, DIAGONAL)


def _whole(ref, i):
    return ref


def _lead(ref, i):
    return ref.at[i]


def _second(ref, i):
    return ref.at[:, i]


def _place():
    x, y, c = lax.axis_index("x"), lax.axis_index("y"), lax.axis_index("c")
    return (x, y, c), 4 * x + 2 * y + c


def _descriptor(cp, bufs, xyc, me, sender, send_sems, recv_sems):
    x, y, c = xyc
    flip = lambda v, bit: (1 - v) if bit else v
    return pltpu.make_async_remote_copy(
        src_ref=cp.src(bufs[cp.sb], me), dst_ref=cp.dst(bufs[cp.db], sender),
        send_sem=send_sems.at[cp.sem], recv_sem=recv_sems.at[cp.sem],
        device_id=(flip(x, cp.mask & 4), flip(y, cp.mask & 2), flip(c, cp.mask & 1)), device_id_type=MESH)


def _exchange(name, bufs, plan, local=()):
    n = len(bufs)

    def body(*refs):
        ins = refs[:n]
        send_sems, recv_sems, local_sems = refs[2 * n:]
        xyc, me = _place()
        own = [pltpu.make_async_copy(lc.src(ins[lc.sb], me), lc.dst(ins[lc.db], me), local_sems.at[i])
               for i, lc in enumerate(local)]
        sends = [_descriptor(cp, ins, xyc, me, me, send_sems, recv_sems) for cp in plan]
        for cp in own + sends:
            cp.start()
        for cp in plan:
            _descriptor(cp, ins, xyc, me, me ^ cp.mask, send_sems, recv_sems).wait_recv()
        for cp in sends:
            cp.wait_send()
        for cp in own:
            cp.wait()

    return pl.pallas_call(
        body, name=name, in_specs=[HBM_SPEC] * n, out_specs=[HBM_SPEC] * n,
        out_shape=[jax.ShapeDtypeStruct(b.shape, b.dtype) for b in bufs],
        input_output_aliases={i: i for i in range(n)},
        scratch_shapes=[pltpu.SemaphoreType.DMA((len(plan),)), pltpu.SemaphoreType.DMA((len(plan),)),
                        pltpu.SemaphoreType.DMA((max(len(local), 1),))],
    )(*bufs)


def _place_own(me, items):
    def body(me_ref, *refs):
        for src, dst in zip(refs[:len(items)], refs[len(items):]):
            dst[...] = src[...].astype(dst.dtype)

    return pl.pallas_call(
        body, name="place_own",
        grid_spec=pltpu.PrefetchScalarGridSpec(
            num_scalar_prefetch=1, grid=(1,),
            in_specs=[pl.BlockSpec(blk, functools.partial(lambda i, m, idx: idx, idx=idx)) for _, blk, idx, _, _, _, _ in items],
            out_specs=[pl.BlockSpec(oblk, functools.partial(lambda i, m, at: at(m[0]), at=at)) for *_, oblk, at in items]),
        out_shape=[jax.ShapeDtypeStruct(shape, dtype) for _, _, _, shape, dtype, _, _ in items],
        compiler_params=_params(dimension_semantics=("arbitrary",)),
    )(jnp.reshape(me, (1,)).astype(jnp.int32), *[a for a, *_ in items])


def _split_call(name, bufs, *, wait=None, wait_sems=None, start=None, local=(), after=None, token=False):
    n = len(bufs)
    n_wait = 2 if wait else 0
    n_after = 1 if after is not None else 0
    n_start = 2 if start else 0

    def body(*refs):
        ins = refs[:n]
        wsend, wrecv = refs[n:n + n_wait] if wait else (None, None)
        outs = refs[n + n_wait + n_after:]
        ssend, srecv = outs[:n_start] if start else (None, None)
        rest = outs[n_start + n:]
        xyc, me = _place()
        for cp in wait or ():
            d = _descriptor(cp, ins, xyc, me, me ^ cp.mask, wsend, wrecv)
            d.wait_send()
            d.wait_recv()
        own = [pltpu.make_async_copy(lc.src(ins[lc.sb], me), lc.dst(ins[lc.db], me), rest[-1].at[i])
               for i, lc in enumerate(local)]
        for cp in own:
            cp.start()
        for cp in start or ():
            _descriptor(cp, ins, xyc, me, me, ssend, srecv).start()
        for cp in own:
            cp.wait()
        if token:
            rest[0][...] = jnp.zeros_like(rest[0])

    args = [pltpu.with_memory_space_constraint(b, pltpu.HBM) for b in bufs]
    in_specs = [HBM_SPEC] * n
    if wait:
        args += list(wait_sems)
        in_specs += [SEM_SPEC] * 2
    if after is not None:
        args.append(after)
        in_specs.append(pl.BlockSpec(memory_space=pl.ANY))
    out_shape, out_specs = [], []
    if start:
        out_shape += [pltpu.SemaphoreType.DMA((len(start),))] * 2
        out_specs += [SEM_SPEC] * 2
    out_shape += [pltpu.HBM(b.shape, b.dtype) for b in bufs]
    out_specs += [HBM_SPEC] * n
    if token:
        out_shape.append(jax.ShapeDtypeStruct((8, 128), F32))
        out_specs.append(pl.BlockSpec(memory_space=pltpu.VMEM))
    outs = pl.pallas_call(
        body, name=name, in_specs=in_specs, out_specs=out_specs, out_shape=out_shape,
        input_output_aliases={i: n_start + i for i in range(n)},
        scratch_shapes=[pltpu.SemaphoreType.DMA((len(local),))] if local else [],
        compiler_params=pltpu.CompilerParams(has_side_effects=pltpu.SideEffectType.DATAFLOW_SIDE_EFFECTING),
    )(*args)
    sems = tuple(outs[:n_start]) if start else None
    return sems, list(outs[n_start:n_start + n]), (outs[n_start + n] if token else None)


def _adamw(w, g, m, v):
    m = ADAM_B1 * m + (1.0 - ADAM_B1) * g
    v = ADAM_B2 * v + (1.0 - ADAM_B2) * (g * g)
    m_hat = m / (1.0 - ADAM_B1 ** ADAM_STEP)
    v_hat = v / (1.0 - ADAM_B2 ** ADAM_STEP)
    delta = -ADAM_LR * (m_hat / (jnp.sqrt(v_hat) + ADAM_EPS) + ADAM_WD * w)
    return delta, m, v


def _reduce_adam(name, parts, w, m, v, *, tr, layer=None, into=None, after=None):
    L, R, C = w.shape
    S = parts[0].shape[0]
    tr = min(tr, R)
    n_l = L if layer is None else 1
    first = 0 if layer is None else layer

    def body(*refs):
        p_refs = refs[:n_l]
        w_ref, m_ref, v_ref = refs[n_l:n_l + 3]
        g_ref, d_ref, nm_ref, nv_ref = refs[-4:]
        for l in range(n_l):
            g = p_refs[l][0].astype(F32)
            for s in range(1, S):
                g = g + p_refs[l][s].astype(F32)
            g_ref[l] = g
            d_ref[l], nm_ref[l], nv_ref[l] = _adamw(w_ref[l], g, m_ref[l], v_ref[l])

    blk = pl.BlockSpec((n_l, tr, C), lambda r: (first, r, 0))
    out = jax.ShapeDtypeStruct((L, R, C), F32)
    extra = list(into or []) + ([after] if after is not None else [])
    return pl.pallas_call(
        body, name=name, grid=(R // tr,),
        in_specs=[pl.BlockSpec((S, tr, C), lambda r: (0, r, 0))] * n_l + [blk, blk, blk]
        + [pl.BlockSpec(memory_space=pl.ANY)] * len(extra),
        out_specs=[blk] * 4, out_shape=[out] * 4,
        input_output_aliases={n_l + 3 + i: i for i in range(4)} if into else {},
        compiler_params=_params(dimension_semantics=("arbitrary",)),
    )(*parts, w, m, v, *extra)


def _small_reduce(parts):
    D = parts.shape[2]
    rows = [0, 1, 8, 9, 16, 17, 24, 25, 18, 19, 20, 2, 32]

    def body(p_ref, out_ref):
        s = p_ref[0]
        for d in range(1, N_DEV):
            s = s + p_ref[d]
        out_ref[...] = jnp.zeros_like(out_ref)
        for r, src in enumerate(rows):
            out_ref[r:r + 1, :] = s[src:src + 1, :]

    return pl.pallas_call(body, name="small_reduce", out_shape=jax.ShapeDtypeStruct((16, D), F32))(parts)


def _small_adam(g_gain, g_taps, g_scale, gains, taps, scale):
    def body(gg, gt, gs, wg, mg, vg, wt, mt, vt, ws, ms, vs, *outs):
        for k, (g, w, m, v) in enumerate(((gg, wg, mg, vg), (gt, wt, mt, vt), (gs, ws, ms, vs))):
            outs[3 * k][...], outs[3 * k + 1][...], outs[3 * k + 2][...] = _adamw(w[...], g[...], m[...], v[...])

    shapes = [jax.ShapeDtypeStruct(t[0].shape, F32) for t in (gains, taps, scale) for _ in range(3)]
    return pl.pallas_call(body, name="small_adam", out_shape=shapes)(g_gain, g_taps, g_scale, *gains, *taps, *scale)


def kernel(x, norm_gains, pool_w, pool_scale, conv_in_w, conv_w, conv_out_w, ffn_gate_up_w, ffn_down_w, loss_target, m_norm_gains, m_pool_w, m_pool_scale, m_conv_in_w, m_conv_w, m_conv_out_w, m_ffn_gate_up_w, m_ffn_down_w, v_norm_gains, v_pool_w, v_pool_scale, v_conv_in_w, v_conv_w, v_conv_out_w, v_ffn_gate_up_w, v_ffn_down_w):
    T, D = x.shape[1], x.shape[2]
    tm = min(512, T)
    tm_b = min(256, T)
    tk = min(2048, T)
    n_layers = ffn_gate_up_w.shape[0]
    fb = ffn_gate_up_w.shape[2]
    fr = ffn_down_w.shape[1]
    dcol = norm_gains.shape[2]
    cb = conv_in_w.shape[2]
    gw = pool_w.shape[3]
    me = 4 * lax.axis_index("x") + 2 * lax.axis_index("y") + lax.axis_index("c")

    small_w = jnp.concatenate([norm_gains.reshape(8, dcol), jnp.pad(conv_w[0], ((0, 5), (0, 0)))], axis=0)
    every = range(1, N_DEV)
    wgu_t, m_wgu_t, v_wgu_t = (jnp.swapaxes(a, 1, 2) for a in (ffn_gate_up_w, m_ffn_gate_up_w, v_ffn_gate_up_w))
    own_lead = lambda s: lax.dynamic_update_slice(lax.empty((N_DEV,) + s.shape, s.dtype), s[None], (me,) + (0,) * s.ndim)
    lead_item = lambda a, l, dtype: (a, (None,) + a.shape[1:], (l, 0, 0), (N_DEV,) + a.shape[1:], dtype,
                                     (None,) + a.shape[1:], lambda i: (i, 0, 0))
    lands = _place_own(me, [
        (pool_w, (None,) + pool_w.shape[1:], (0, 0, 0, 0), (4, N_DEV, gw // N_DEV, gw), BF16,
         (4, None, gw // N_DEV, gw), lambda i: (0, i, 0, 0)),
        (small_w[None], (None,) + small_w.shape, (0, 0, 0), (N_DEV,) + small_w.shape, F32, (None,) + small_w.shape, lambda i: (i, 0, 0)),
        lead_item(wgu_t, 0, BF16), lead_item(ffn_down_w, 0, BF16), lead_item(conv_in_w, 0, BF16),
        lead_item(conv_out_w, 0, BF16), lead_item(wgu_t, 1, BF16), lead_item(ffn_down_w, 1, BF16)])
    n_first, n_big = 2, len(lands) - 2
    direct = ([Copy(m, 0, _second, 0, _second, m - 1) for m in every]
              + [Copy(m, 1, _lead, 1, _lead, N_DEV - 2 + m) for m in every])
    level1 = [Copy(mask, n_first + n, _lead, n_first + n, _lead, len(direct) + 4 * n + j)
              for n in range(n_big) for j, mask in enumerate((SIBLING,) + OTHER_CHIPS)]
    sems1, bufs1, _ = _split_call("gather_start", lands, start=direct + level1)
    pw_g, small_g = _split_call("gather_small_done", bufs1[:n_first], wait=direct, wait_sems=sems1)[1]
    pw = pw_g.reshape(4, gw, gw)
    small_full = jnp.swapaxes(small_g, 0, 1).reshape(16, D)
    gain = lambda l, s: small_full[4 * l + s][None, :]
    taps = small_full[8:16]

    def forward_on(name, group, after):
        k = len(group)
        landed = [Copy(cp.mask, i, cp.src, i, cp.dst, cp.sem)
                  for i, n in enumerate(group) for cp in level1 if cp.sb == n_first + n]
        onward = [Copy(SIBLING, i, (lambda ref, me, m=m: ref.at[me ^ m]), i, (lambda ref, sender, m=m: ref.at[sender ^ m]), 3 * i + j)
                  for i in range(k) for j, m in enumerate(OTHER_CHIPS)]
        sems2, bufs2, tok = _split_call(name + "_forward", [bufs1[n_first + n] for n in group], wait=landed,
                                        wait_sems=sems1, start=onward, after=after, token=True)
        return (name, onward, sems2, bufs2), tok

    def arrived(state, after=None):
        name, onward, sems2, lands2 = state
        return _split_call(name + "_done", lands2, wait=onward, wait_sems=sems2, after=after)[1]

    h0 = x[0]
    h1 = _pool_fwd(h0, pw, pool_scale, gain(0, 0), gain(0, 1), tm=tm)
    (wgu0,) = arrived(forward_on("gather_gate_up_0", [0], h1)[0])
    gu0, act0 = _ffn_up(h1, gain(0, 2), wgu0, tm=tm)
    ag_down0, tok = forward_on("gather_down_0", [1], act0)
    ag_conv, tok = forward_on("gather_conv", [2, 3], tok)
    (wd0,) = arrived(ag_down0, tok)
    wd0 = wd0.reshape(N_DEV * fr, D)
    h2, ff0 = _ffn_down(h1, act0, gain(0, 3), wd0, tm=tm)
    win_g, wout_g = arrived(ag_conv, h2)
    wout = wout_g.reshape(D, D)
    h3, proj, y = _conv_fwd(h2, gain(1, 0), gain(1, 1), win_g, taps, wout, tm=tm)
    ag_ffn1, tok = forward_on("gather_ffn1", [4, 5], h3)
    wgu1, wd1 = arrived(ag_ffn1, tok)
    wd1 = wd1.reshape(N_DEV * fr, D)
    dh4, gu1, ff1, loss_part = _ffn_fwd(h3, gain(1, 2), gain(1, 3), wgu1, wd1, 1, loss_target[0], tm=tm)

    chip = me >> 1

    def scatter_start(name, sums):
        k = len(sums)
        lands = [lax.dynamic_update_slice(lax.empty(s.shape, BF16), lax.dynamic_index_in_dim(s, chip, 0), (chip, 0, 0))
                 for s in sums]
        plan = [Copy(m, n, (lambda ref, i, m=m: ref.at[(i ^ m) >> 1]), k + n, (lambda ref, i: ref.at[i >> 1]), 3 * n + j)
                for n in range(k) for j, m in enumerate(OTHER_CHIPS)]
        sems, bufs, tok = _split_call(name + "_start", sums + lands, start=plan, token=True)
        return (name, plan, sems, bufs), tok

    def scatter_done(state, after):
        name, plan, sems, bufs = state
        return _split_call(name + "_done", bufs, wait=plan, wait_sems=sems, after=after)[1][len(bufs) // 2:]

    seq = lambda i, k: (k, 0)
    by_block = pl.BlockSpec((None, tk, fb), lambda i, k: (i, k, 0))
    rows = pl.BlockSpec((tk, D), seq)
    dh3, dgu1, dff1, c1, act1, small_f1 = _ffn_bwd(dh4, h3, ff1, gu1, gain(1, 2), gain(1, 3), wgu1, wd1, 1, tm=tm_b)
    g_wgu1 = _wgrad("wgrad_gate_up_1", dgu1, c1, by_block, rows, (fb, D), N_DEV, tk=tk)
    g_wd1 = _wgrad("wgrad_down_1", act1, dff1, by_block, rows, (fb, D), N_DEV // 2, tk=tk)
    rs_ffn1, tok = scatter_start("scatter_ffn1", [g_wgu1, g_wd1])
    dh2, dproj, z, a1, dy, small_c = _conv_bwd(dh3, h2, y, proj, gain(1, 0), gain(1, 1), win_g, taps, wout, tm=tm, after=tok)
    g_win = _wgrad("wgrad_conv_in", dproj, a1, pl.BlockSpec((tk, cb), lambda i, k: (k, i)), rows, (D, cb), N_DEV,
                   tk=tk, transpose=True)
    g_wout = _wgrad("wgrad_conv_out", z, dy, rows, rows, (D, D), 1, tk=tk)
    rs_conv, tok = scatter_start("scatter_conv", [g_win, g_wout])
    dh1, dgu0, dff0, c0, act0, small_f0 = _ffn_bwd(dh2, h1, ff0, gu0, gain(0, 2), gain(0, 3), wgu0, wd0, 0, tm=tm_b, after=tok)
    g_wgu0 = _wgrad("wgrad_gate_up_0", dgu0, c0, by_block, rows, (fb, D), N_DEV, tk=tk)
    rs_wgu0, tok = scatter_start("scatter_gate_up_0", [g_wgu0])
    g_wd0 = _wgrad("wgrad_down_0", act0, dff0, by_block, rows, (fb, D), N_DEV // 2, tk=tk, after=tok)
    rs_wd0, tok = scatter_start("scatter_down_0", [g_wd0])
    grad_x, g_pw, small_p = _pool_bwd(dh1, h0, pw, pool_scale, gain(0, 0), gain(0, 1), tm=tm, after=tok)

    loss_rows = jnp.broadcast_to(loss_part[0:1, 0:1], (8, D))
    small_part = jnp.concatenate([small_p, small_f0, small_c, small_f1, loss_rows], axis=0)
    g_pw = g_pw.reshape(4, N_DEV, gw // N_DEV, gw)
    pw_land = lax.dynamic_update_slice(lax.empty(g_pw.shape, BF16), lax.dynamic_slice_in_dim(g_pw, me, 1, 1), (0, me, 0, 0))
    last = ([Copy(m, 0, (lambda ref, i, m=m: ref.at[:, i ^ m]), 2, _second, m - 1) for m in every]
            + [Copy(m, 1, _whole, 3, _lead, N_DEV - 2 + m) for m in every])
    sems_l, bufs_l, tok = _split_call("scatter_small_start", [g_pw, small_part, pw_land, own_lead(small_part)],
                                      start=last, token=True)

    (r_wgu1, r_wd1), (r_win, r_wout) = scatter_done(rs_ffn1, tok), scatter_done(rs_conv, tok)
    o_win = _reduce_adam("adam_conv_in", [r_win], conv_in_w, m_conv_in_w, v_conv_in_w, tr=256)
    o_wout = _reduce_adam("adam_conv_out", [r_wout], conv_out_w, m_conv_out_w, v_conv_out_w, tr=128, after=o_win[0])
    o_wgu = _reduce_adam("adam_gate_up_1", [r_wgu1], wgu_t, m_wgu_t, v_wgu_t, tr=176, layer=1, after=o_wout[0])
    o_wd = _reduce_adam("adam_down_1", [r_wd1], ffn_down_w, m_ffn_down_w, v_ffn_down_w, tr=176, layer=1, after=o_wgu[0])
    r_pw, r_small = _split_call("scatter_small_done", bufs_l, wait=last, wait_sems=sems_l, after=o_wd[0])[1][2:]
    (r_wgu0,), (r_wd0,) = scatter_done(rs_wgu0, r_small), scatter_done(rs_wd0, r_small)
    o_wgu = _reduce_adam("adam_gate_up_0", [r_wgu0], wgu_t, m_wgu_t, v_wgu_t, tr=176, layer=0, into=o_wgu)
    o_wgu = [jnp.swapaxes(o, 1, 2) for o in o_wgu]
    o_wd = _reduce_adam("adam_down_0", [r_wd0], ffn_down_w, m_ffn_down_w, v_ffn_down_w, tr=176, layer=0, into=o_wd)
    o_pw = _reduce_adam("adam_pool_w", [r_pw[g] for g in range(4)], pool_w[0], m_pool_w[0], v_pool_w[0], tr=32)
    g_small = _small_reduce(r_small)
    loss = g_small[12, 0]
    g_cols = lax.dynamic_slice(g_small, (0, me * dcol), (16, dcol))
    o_small = _small_adam(
        g_cols[0:8], g_cols[8:11], g_small[11:12],
        (norm_gains.reshape(8, dcol), m_norm_gains.reshape(8, dcol), v_norm_gains.reshape(8, dcol)),
        (conv_w[0], m_conv_w[0], v_conv_w[0]), (pool_scale, m_pool_scale, v_pool_scale))
    d_gain, nm_gain, nv_gain, d_taps, nm_taps, nv_taps, d_scale, nm_scale, nv_scale = o_small

    gshape = norm_gains.shape
    per = lambda k: (
        (g_cols[0:8].reshape(gshape), d_gain.reshape(gshape), nm_gain.reshape(gshape), nv_gain.reshape(gshape))[k],
        o_pw[k][None], (g_small[11:12], d_scale, nm_scale, nv_scale)[k], o_win[k],
        (g_cols[8:11][None], d_taps[None], nm_taps[None], nv_taps[None])[k], o_wout[k], o_wgu[k], o_wd[k])
    return (loss, grad_x[None], *per(0), *per(1), *per(2), *per(3))
```

```python
import collections
import functools

import jax
import jax.numpy as jnp
from jax import lax
from jax.experimental import pallas as pl
from jax.experimental.pallas import tpu as pltpu

N_DEV = 8
RMS_EPS = 1e-6
POOL_WINDOWS = (2, 4, 8, 16)
POOL_HALO = 16
CONV_HALO = 16
ADAM_LR, ADAM_B1, ADAM_B2, ADAM_EPS, ADAM_WD, ADAM_STEP = 0.001, 0.9, 0.999, 1e-08, 0.01, 10

VMEM_LIMIT = 56 * 2**20
MXU_COLUMNS = 256
BF16 = jnp.bfloat16
F32 = jnp.float32
MESH = pl.DeviceIdType.MESH
SIBLING_PAIR_ID = 0


def _params(**kw):
    return pltpu.CompilerParams(vmem_limit_bytes=VMEM_LIMIT, **kw)


def _resident(shape, index_map):
    return pl.BlockSpec(shape, index_map, pipeline_mode=pl.Buffered(1))


def _ordered(body, n_in, after):
    if after is None:
        return functools.partial(body), [], []
    return (lambda *refs: body(*refs[:n_in], *refs[n_in + 1:])), [after], [pl.BlockSpec(memory_space=pl.ANY)]


def _rms_fwd(x, g):
    r = lax.rsqrt(jnp.mean(x * x, axis=-1, keepdims=True) + RMS_EPS)
    return x * r * g


def _rms_bwd(x, g, dy):
    r = lax.rsqrt(jnp.mean(x * x, axis=-1, keepdims=True) + RMS_EPS)
    xhat = x * r
    dg = jnp.sum(dy * xhat, axis=0, keepdims=True)
    t = dy * g
    dx = r * (t - xhat * jnp.mean(t * xhat, axis=-1, keepdims=True))
    return dx, dg


def _sigmoid(x):
    return 0.5 * jnp.tanh(0.5 * x) + 0.5


def _dot(a, b):
    return jnp.dot(a, b, preferred_element_type=F32)


def _dot_nt(a, b):
    return lax.dot_general(a, b, (((1,), (1,)), ((), ())), preferred_element_type=F32)


def _dot_tn(a, b):
    return lax.dot_general(a, b, (((0,), (0,)), ((), ())), preferred_element_type=F32)


def _join_blocks(blocks_hbm, joined_ref, sems):
    n, _, C = blocks_hbm.shape
    copies = [pltpu.make_async_copy(blocks_hbm.at[k], joined_ref.at[:, k * C:(k + 1) * C], sems.at[k]) for k in range(n)]
    for cp in copies:
        cp.start()
    for cp in copies:
        cp.wait()


def _row_inverse_counts(tile, tm):
    pos = (lax.broadcasted_iota(jnp.int32, (tm, 1), 0) + tile * tm + 1).astype(F32)
    return [1.0 / jnp.minimum(pos, float(w)) for w in POOL_WINDOWS]


def _pool_from_ext(ext, a, invs, gw):
    s = ext
    outs = []
    for g, w in enumerate(POOL_WINDOWS):
        s = s[:, (gw if g else 0):]
        s = s + pltpu.roll(s, w // 2, 0)
        outs.append(s[POOL_HALO:, :gw] * invs[g] - a[:, g * gw:(g + 1) * gw])
    return outs


def _pool_fwd(h, pw, scale, g_pre, g_post, *, tm, after=None):
    T, D = h.shape
    gw = D // len(POOL_WINDOWS)
    hb = tm // POOL_HALO

    def body(h_ref, halo_ref, pw_ref, scale_ref, gpre_ref, gpost_ref, out_ref, ext_ref):
        i = pl.program_id(0)
        x = h_ref[...]
        a = _rms_fwd(x, gpre_ref[...])
        ah = _rms_fwd(halo_ref[...], gpre_ref[...])
        ext_ref[0:POOL_HALO, :] = jnp.where(i == 0, 0.0, ah)
        ext_ref[POOL_HALO:, :] = a
        pooled = _pool_from_ext(ext_ref[...], a, _row_inverse_counts(i, tm), gw)
        mixed = jnp.concatenate([_dot(p.astype(BF16), pw_ref[g]) for g, p in enumerate(pooled)], axis=1)
        out_ref[...] = x + _rms_fwd(mixed * scale_ref[...], gpost_ref[...])

    vec = _resident((1, D), lambda i: (0, 0))
    fn, xa, xs = _ordered(body, 6, after)
    return pl.pallas_call(
        fn, name="pool_fwd", grid=(T // tm,),
        in_specs=[pl.BlockSpec((tm, D), lambda i: (i, 0)),
                  pl.BlockSpec((POOL_HALO, D), lambda i: (jnp.maximum(i * hb - 1, 0), 0)),
                  _resident(pw.shape, lambda i: (0, 0, 0)), vec, vec, vec] + xs,
        out_specs=pl.BlockSpec((tm, D), lambda i: (i, 0)),
        out_shape=jax.ShapeDtypeStruct((T, D), F32),
        scratch_shapes=[pltpu.VMEM((tm + POOL_HALO, D), F32)],
        compiler_params=_params(dimension_semantics=("arbitrary",)),
    )(h, h, pw, scale, g_pre, g_post, *xa)


def _pool_bwd(dh, h, pw, scale, g_pre, g_post, *, tm, after=None):
    T, D = h.shape
    gw = D // len(POOL_WINDOWS)
    hb = tm // POOL_HALO
    nt = T // tm
    n_ext = tm + POOL_HALO

    def body(dh_ref, h_ref, halo_ref, pw_ref, scale_ref, gpre_ref, gpost_ref,
             dx_ref, dpw_ref, small_ref, ext_ref, ext2_ref, carry_ref, dpw_acc):
        i = pl.program_id(0)
        tile = nt - 1 - i

        @pl.when(i == 0)
        def _():
            small_ref[...] = jnp.zeros_like(small_ref)
            dpw_acc[...] = jnp.zeros_like(dpw_acc)
            carry_ref[...] = jnp.zeros_like(carry_ref)

        x = h_ref[...]
        dout = dh_ref[...]
        a = _rms_fwd(x, gpre_ref[...])
        ah = _rms_fwd(halo_ref[...], gpre_ref[...])
        ext_ref[0:POOL_HALO, :] = jnp.where(tile == 0, 0.0, ah)
        ext_ref[POOL_HALO:, :] = a
        invs = _row_inverse_counts(tile, tm)
        pooled = [p.astype(BF16) for p in _pool_from_ext(ext_ref[...], a, invs, gw)]
        mixed_pre = jnp.concatenate([_dot(p, pw_ref[g]) for g, p in enumerate(pooled)], axis=1)
        scale_v = scale_ref[...]
        dmixed, dg_post = _rms_bwd(mixed_pre * scale_v, gpost_ref[...], dout)
        small_ref[1:2, :] += dg_post
        small_ref[2:3, :] += jnp.sum(dmixed * mixed_pre, axis=0, keepdims=True)
        dpre = (dmixed * scale_v).astype(BF16)
        dpooled = []
        for g in range(len(POOL_WINDOWS)):
            dp = dpre[:, g * gw:(g + 1) * gw]
            dpw_acc[g] += _dot_tn(pooled[g], dp)
            dpooled.append(_dot_nt(dp, pw_ref[g]))
        q = jnp.concatenate([d * invs[g] for g, d in enumerate(dpooled)], axis=1)
        ext2_ref[0:tm, :] = q
        ext2_ref[tm:, :] = carry_ref[...]
        carry_ref[...] = q[0:POOL_HALO, :]
        s = ext2_ref[...]
        da = []
        for g, w in enumerate(POOL_WINDOWS):
            s = s[:, (gw if g else 0):]
            s = s + pltpu.roll(s, n_ext - w // 2, 0)
            da.append(s[0:tm, :gw] - dpooled[g])
        dx, dg_pre = _rms_bwd(x, gpre_ref[...], jnp.concatenate(da, axis=1))
        small_ref[0:1, :] += dg_pre
        dx_ref[...] = dout + dx

        @pl.when(i == nt - 1)
        def _():
            dpw_ref[...] = dpw_acc[...].astype(BF16)

    vec = _resident((1, D), lambda i: (0, 0))
    rev = lambda i: (nt - 1 - i, 0)
    fn, xa, xs = _ordered(body, 7, after)
    return pl.pallas_call(
        fn, name="pool_bwd", grid=(nt,),
        in_specs=[pl.BlockSpec((tm, D), rev), pl.BlockSpec((tm, D), rev),
                  pl.BlockSpec((POOL_HALO, D), lambda i: (jnp.maximum((nt - 1 - i) * hb - 1, 0), 0)),
                  _resident(pw.shape, lambda i: (0, 0, 0)), vec, vec, vec] + xs,
        out_specs=[pl.BlockSpec((tm, D), rev),
                   pl.BlockSpec(pw.shape, lambda i: (0, 0, 0)),
                   pl.BlockSpec((8, D), lambda i: (0, 0))],
        out_shape=[jax.ShapeDtypeStruct((T, D), F32), jax.ShapeDtypeStruct(pw.shape, BF16),
                   jax.ShapeDtypeStruct((8, D), F32)],
        scratch_shapes=[pltpu.VMEM((n_ext, D), F32), pltpu.VMEM((n_ext, D), F32),
                        pltpu.VMEM((POOL_HALO, D), F32), pltpu.VMEM(pw.shape, F32)],
        compiler_params=_params(dimension_semantics=("arbitrary",)),
    )(dh, h, h, pw, scale, g_pre, g_post, *xa)


def _ffn_fwd(h, g_pre, g_post, wgu, wd, layer, target, *, tm, after=None):
    T, D = h.shape
    F = wd.shape[0]
    last = target is not None

    def body(*refs):
        if last:
            h_ref, gpre_ref, gpost_ref, wgu_ref, wd_ref, tgt_ref, out_ref, gu_ref, act_ref, ff_ref, loss_ref = refs
        else:
            h_ref, gpre_ref, gpost_ref, wgu_ref, wd_ref, out_ref, gu_ref, act_ref, ff_ref = refs
        x = h_ref[...]
        cb = _rms_fwd(x, gpre_ref[...]).astype(BF16)
        g = _dot_nt(cb, wgu_ref[0:F, :])
        u = _dot_nt(cb, wgu_ref[F:2 * F, :])
        gu_ref[0] = g.astype(BF16)
        gu_ref[1] = u.astype(BF16)
        act = (g * _sigmoid(g) * u).astype(BF16)
        act_ref[...] = act
        acc = _dot(act, wd_ref[...])
        ff_ref[...] = acc.astype(BF16)
        hout = x + _rms_fwd(acc, gpost_ref[...])
        if last:
            diff = hout - tgt_ref[...]
            out_ref[...] = diff * (1.0 / D)

            @pl.when(pl.program_id(0) == 0)
            def _():
                loss_ref[...] = jnp.zeros_like(loss_ref)

            loss_ref[...] += jnp.sum(diff * diff) * (0.5 / D)
        else:
            out_ref[...] = hout

    vec = _resident((1, D), lambda i: (0, 0))
    tile = pl.BlockSpec((tm, D), lambda i: (i, 0))
    in_specs = [tile, vec, vec, _resident(wgu.shape, lambda i: (0, 0)), _resident(wd.shape, lambda i: (0, 0))]
    out_specs = [tile, pl.BlockSpec((2, tm, F), lambda i: (0, i, 0)), pl.BlockSpec((tm, F), lambda i: (i, 0)), tile]
    out_shape = [jax.ShapeDtypeStruct((T, D), F32), jax.ShapeDtypeStruct((2, T, F), BF16),
                 jax.ShapeDtypeStruct((T, F), BF16), jax.ShapeDtypeStruct((T, D), BF16)]
    args = [h, g_pre, g_post, wgu, wd]
    if last:
        in_specs.append(tile)
        args.append(target)
        out_specs.append(pl.BlockSpec((8, 128), lambda i: (0, 0)))
        out_shape.append(jax.ShapeDtypeStruct((8, 128), F32))
    fn, xa, xs = _ordered(body, len(args), after)
    return pl.pallas_call(
        fn, name=f"ffn_fwd_{layer}", grid=(T // tm,), in_specs=in_specs + xs, out_specs=out_specs,
        out_shape=out_shape, compiler_params=_params(dimension_semantics=("arbitrary",)),
    )(*args, *xa)


def _ffn_up(h, g_pre, wgu, *, tm):
    T, D = h.shape
    F = wgu.shape[0] // 2

    def body(h_ref, gpre_ref, wgu_ref, gu_ref, act_ref):
        cb = _rms_fwd(h_ref[...], gpre_ref[...]).astype(BF16)
        g = _dot_nt(cb, wgu_ref[0:F, :])
        u = _dot_nt(cb, wgu_ref[F:2 * F, :])
        gu_ref[0] = g.astype(BF16)
        gu_ref[1] = u.astype(BF16)
        act_ref[...] = (g * _sigmoid(g) * u).astype(BF16)

    return pl.pallas_call(
        body, name="ffn_up_0", grid=(T // tm,),
        in_specs=[pl.BlockSpec((tm, D), lambda i: (i, 0)), _resident((1, D), lambda i: (0, 0)),
                  _resident(wgu.shape, lambda i: (0, 0))],
        out_specs=[pl.BlockSpec((2, tm, F), lambda i: (0, i, 0)), pl.BlockSpec((tm, F), lambda i: (i, 0))],
        out_shape=[jax.ShapeDtypeStruct((2, T, F), BF16), jax.ShapeDtypeStruct((T, F), BF16)],
        compiler_params=_params(dimension_semantics=("arbitrary",)),
    )(h, g_pre, wgu)


def _ffn_down(h, act, g_post, wd, *, tm):
    T, D = h.shape
    F = act.shape[1]

    def body(h_ref, act_ref, gpost_ref, wd_ref, out_ref, ff_ref):
        acc = _dot(act_ref[...], wd_ref[...])
        ff_ref[...] = acc.astype(BF16)
        out_ref[...] = h_ref[...] + _rms_fwd(acc, gpost_ref[...])

    tile = pl.BlockSpec((tm, D), lambda i: (i, 0))
    return pl.pallas_call(
        body, name="ffn_down_0", grid=(T // tm,),
        in_specs=[tile, pl.BlockSpec((tm, F), lambda i: (i, 0)), _resident((1, D), lambda i: (0, 0)),
                  _resident(wd.shape, lambda i: (0, 0))],
        out_specs=[tile, tile],
        out_shape=[jax.ShapeDtypeStruct((T, D), F32), jax.ShapeDtypeStruct((T, D), BF16)],
        compiler_params=_params(dimension_semantics=("arbitrary",)),
    )(h, act, g_post, wd)


def _ffn_bwd(dh, h, ff, gu, g_pre, g_post, wgu, wd, layer, *, tm, after=None):
    T, D = h.shape
    F = wd.shape[0]
    n_chunks = F // MXU_COLUMNS

    def body(dh_ref, h_ref, ff_ref, gu_ref, gpre_ref, gpost_ref, wgu_ref, wd_ref,
             dx_ref, dgu_ref, dff_ref, c_ref, small_ref):
        @pl.when(pl.program_id(0) == 0)
        def _():
            small_ref[...] = jnp.zeros_like(small_ref)

        dout = dh_ref[...]
        dff, dg_post = _rms_bwd(ff_ref[...].astype(F32), gpost_ref[...], dout)
        small_ref[1:2, :] += dg_post
        dffb = dff.astype(BF16)
        dff_ref[...] = dffb
        dc = jnp.zeros((tm, D), F32)
        for j in range(n_chunks + 1):
            lo, hi = j * MXU_COLUMNS, (j + 1) * MXU_COLUMNS
            if j < n_chunks:
                dact = _dot_nt(dffb, wd_ref[lo:hi, :])
            if j > 0:
                lo0 = lo - MXU_COLUMNS
                dc = dc + _dot(dgu_ref[0, :, lo0:lo], wgu_ref[lo0:lo, :]) + _dot(dgu_ref[1, :, lo0:lo], wgu_ref[F + lo0:F + lo, :])
            if j < n_chunks:
                g = gu_ref[0, :, lo:hi].astype(F32)
                u = gu_ref[1, :, lo:hi].astype(F32)
                s = _sigmoid(g)
                dgu_ref[0, :, lo:hi] = (dact * u * (s * (1.0 + g * (1.0 - s)))).astype(BF16)
                dgu_ref[1, :, lo:hi] = (dact * (g * s)).astype(BF16)
        x = h_ref[...]
        c_ref[...] = _rms_fwd(x, gpre_ref[...]).astype(BF16)
        dx, dg_pre = _rms_bwd(x, gpre_ref[...], dc)
        small_ref[0:1, :] += dg_pre
        dx_ref[...] = dout + dx

    vec = _resident((1, D), lambda i: (0, 0))
    tile = pl.BlockSpec((tm, D), lambda i: (i, 0))
    blk = pl.BlockSpec((2, tm, F), lambda i: (0, i, 0))
    fn, xa, xs = _ordered(body, 8, after)
    return pl.pallas_call(
        fn, name=f"ffn_bwd_{layer}", grid=(T // tm,),
        in_specs=[tile, tile, tile, blk, vec, vec,
                  _resident(wgu.shape, lambda i: (0, 0)), _resident(wd.shape, lambda i: (0, 0))] + xs,
        out_specs=[tile, blk, tile, tile, pl.BlockSpec((8, D), lambda i: (0, 0))],
        out_shape=[jax.ShapeDtypeStruct((T, D), F32), jax.ShapeDtypeStruct((2, T, F), BF16),
                   jax.ShapeDtypeStruct((T, D), BF16), jax.ShapeDtypeStruct((T, D), BF16),
                   jax.ShapeDtypeStruct((8, D), F32)],
        compiler_params=_params(dimension_semantics=("arbitrary",)),
    )(dh, h, ff, gu, g_pre, g_post, wgu, wd, *xa)


def _conv_fwd(h, g_pre, g_post, win, taps, wout, *, tm, after=None):
    T, D = h.shape
    nblk, cb = win.shape[0], win.shape[2]

    def body(h_ref, gpre_ref, gpost_ref, win_hbm, taps_ref, wout_ref,
             out_ref, proj_ref, y_ref, proj_scr, ext_ref, carry_ref, win_ref, win_sems):
        i = pl.program_id(0)

        @pl.when(i == 0)
        def _():
            carry_ref[...] = jnp.zeros_like(carry_ref)
            _join_blocks(win_hbm, win_ref, win_sems)

        x = h_ref[...]
        a = _rms_fwd(x, gpre_ref[...]).astype(BF16)
        proj_scr[...] = _dot(a, win_ref[...])
        proj_ref[...] = proj_scr[...].astype(BF16)
        u = proj_scr[:, D:2 * D] * proj_scr[:, 2 * D:3 * D]
        ext_ref[0:CONV_HALO, :] = carry_ref[...]
        ext_ref[CONV_HALO:, :] = u
        carry_ref[...] = u[tm - CONV_HALO:, :]
        e = ext_ref[...]
        conv = (taps_ref[2:3, :] * u + taps_ref[1:2, :] * pltpu.roll(e, 1, 0)[CONV_HALO:, :]
                + taps_ref[0:1, :] * pltpu.roll(e, 2, 0)[CONV_HALO:, :])
        z = (proj_scr[:, 0:D] * conv).astype(BF16)
        y = _dot(z, wout_ref[...])
        y_ref[...] = y.astype(BF16)
        out_ref[...] = x + _rms_fwd(y, gpost_ref[...])

    vec = _resident((1, D), lambda i: (0, 0))
    tile = pl.BlockSpec((tm, D), lambda i: (i, 0))
    fn, xa, xs = _ordered(body, 6, after)
    return pl.pallas_call(
        fn, name="conv_fwd", grid=(T // tm,),
        in_specs=[tile, vec, vec, pl.BlockSpec(memory_space=pl.ANY),
                  _resident(taps.shape, lambda i: (0, 0)), _resident(wout.shape, lambda i: (0, 0))] + xs,
        out_specs=[tile, pl.BlockSpec((tm, 3 * D), lambda i: (i, 0)), tile],
        out_shape=[jax.ShapeDtypeStruct((T, D), F32), jax.ShapeDtypeStruct((T, 3 * D), BF16),
                   jax.ShapeDtypeStruct((T, D), BF16)],
        scratch_shapes=[pltpu.VMEM((tm, 3 * D), F32), pltpu.VMEM((tm + CONV_HALO, D), F32),
                        pltpu.VMEM((CONV_HALO, D), F32), pltpu.VMEM((D, nblk * cb), BF16),
                        pltpu.SemaphoreType.DMA((nblk,))],
        compiler_params=_params(dimension_semantics=("arbitrary",)),
    )(h, g_pre, g_post, win, taps, wout, *xa)


def _conv_bwd(dh, h, y, proj, g_pre, g_post, win, taps, wout, *, tm, after=None):
    T, D = h.shape
    nblk, cb = win.shape[0], win.shape[2]
    nt = T // tm
    hb = tm // CONV_HALO
    n_ext = tm + CONV_HALO

    def body(dh_ref, h_ref, y_ref, proj_ref, halo_ref, gpre_ref, gpost_ref, win_hbm, taps_ref, wout_ref,
             dx_ref, dproj_ref, z_ref, a_ref, dy_ref, small_ref, ext_ref, ext2_ref, carry_ref, win_ref, win_sems):
        i = pl.program_id(0)
        tile = nt - 1 - i

        @pl.when(i == 0)
        def _():
            small_ref[...] = jnp.zeros_like(small_ref)
            carry_ref[...] = jnp.zeros_like(carry_ref)
            _join_blocks(win_hbm, win_ref, win_sems)

        dout = dh_ref[...]
        dy, dg_post = _rms_bwd(y_ref[...].astype(F32), gpost_ref[...], dout)
        small_ref[1:2, :] += dg_post
        dyb = dy.astype(BF16)
        dy_ref[...] = dyb
        dz = _dot_nt(dyb, wout_ref[...])
        bgate = proj_ref[:, 0:D].astype(F32)
        cgate = proj_ref[:, D:2 * D].astype(F32)
        v = proj_ref[:, 2 * D:3 * D].astype(F32)
        u = cgate * v
        uh = halo_ref[:, D:2 * D].astype(F32) * halo_ref[:, 2 * D:3 * D].astype(F32)
        ext_ref[0:CONV_HALO, :] = jnp.where(tile == 0, 0.0, uh)
        ext_ref[CONV_HALO:, :] = u
        e = ext_ref[...]
        u1 = pltpu.roll(e, 1, 0)[CONV_HALO:, :]
        u2 = pltpu.roll(e, 2, 0)[CONV_HALO:, :]
        t0, t1, t2 = taps_ref[0:1, :], taps_ref[1:2, :], taps_ref[2:3, :]
        conv = t2 * u + t1 * u1 + t0 * u2
        z_ref[...] = (bgate * conv).astype(BF16)
        dconv = dz * bgate
        small_ref[2:3, :] += jnp.sum(dconv * u2, axis=0, keepdims=True)
        small_ref[3:4, :] += jnp.sum(dconv * u1, axis=0, keepdims=True)
        small_ref[4:5, :] += jnp.sum(dconv * u, axis=0, keepdims=True)
        ext2_ref[0:tm, :] = dconv
        ext2_ref[tm:, :] = carry_ref[...]
        carry_ref[...] = dconv[0:CONV_HALO, :]
        e2 = ext2_ref[...]
        du = (t2 * dconv + t1 * pltpu.roll(e2, n_ext - 1, 0)[0:tm, :]
              + t0 * pltpu.roll(e2, n_ext - 2, 0)[0:tm, :])
        dproj_ref[:, 0:D] = (dz * conv).astype(BF16)
        dproj_ref[:, D:2 * D] = (du * v).astype(BF16)
        dproj_ref[:, 2 * D:3 * D] = (du * cgate).astype(BF16)
        da = _dot_nt(dproj_ref[...], win_ref[...])
        x = h_ref[...]
        a_ref[...] = _rms_fwd(x, gpre_ref[...]).astype(BF16)
        dx, dg_pre = _rms_bwd(x, gpre_ref[...], da)
        small_ref[0:1, :] += dg_pre
        dx_ref[...] = dout + dx

    vec = _resident((1, D), lambda i: (0, 0))
    rev = lambda i: (nt - 1 - i, 0)
    tile = pl.BlockSpec((tm, D), rev)
    wide = pl.BlockSpec((tm, 3 * D), rev)
    fn, xa, xs = _ordered(body, 10, after)
    return pl.pallas_call(
        fn, name="conv_bwd", grid=(nt,),
        in_specs=[tile, tile, tile, wide,
                  pl.BlockSpec((CONV_HALO, 3 * D), lambda i: (jnp.maximum((nt - 1 - i) * hb - 1, 0), 0)),
                  vec, vec, pl.BlockSpec(memory_space=pl.ANY),
                  _resident(taps.shape, lambda i: (0, 0)), _resident(wout.shape, lambda i: (0, 0))] + xs,
        out_specs=[tile, wide, tile, tile, tile, pl.BlockSpec((8, D), lambda i: (0, 0))],
        out_shape=[jax.ShapeDtypeStruct((T, D), F32), jax.ShapeDtypeStruct((T, 3 * D), BF16),
                   jax.ShapeDtypeStruct((T, D), BF16), jax.ShapeDtypeStruct((T, D), BF16),
                   jax.ShapeDtypeStruct((T, D), BF16), jax.ShapeDtypeStruct((8, D), F32)],
        scratch_shapes=[pltpu.VMEM((n_ext, D), F32), pltpu.VMEM((n_ext, D), F32),
                        pltpu.VMEM((CONV_HALO, D), F32), pltpu.VMEM((D, nblk * cb), BF16),
                        pltpu.SemaphoreType.DMA((nblk,))],
        compiler_params=_params(dimension_semantics=("arbitrary",)),
    )(dh, h, y, proj, proj, g_pre, g_post, win, taps, wout, *xa)


def _wgrad(name, a, b, a_spec, b_spec, block, n_blocks, *, tk, transpose=False, after=None):
    T = a.shape[-2]
    nk = T // tk
    M, N = block
    m = N_DEV // n_blocks
    R = M // m
    acc_block = block[::-1] if transpose else block

    def body(a_ref, b_ref, out_ref, acc_ref, stage_ref, recv_ref, send_sems, recv_sems):
        i, k = pl.program_id(0), pl.program_id(1)
        x, y, c = lax.axis_index("x"), lax.axis_index("y"), lax.axis_index("c")

        def sent(blk, p):
            owner = blk * m + p
            q = owner // 2
            return (owner % 2) != c, pltpu.make_async_remote_copy(
                src_ref=stage_ref.at[p * R:(p + 1) * R], dst_ref=recv_ref.at[q], send_sem=send_sems.at[q],
                recv_sem=recv_sems.at[q], device_id=(x, y, 1 - c), device_id_type=MESH)

        @pl.when(jnp.logical_and(i == 0, k == 0))
        def _():
            barrier = pltpu.get_barrier_semaphore()
            pl.semaphore_signal(barrier, inc=1, device_id=(x, y, 1 - c), device_id_type=MESH)
            pl.semaphore_wait(barrier, 1)

        @pl.when(k == 0)
        def _():
            acc_ref[...] = jnp.zeros_like(acc_ref)

        acc_ref[...] += _dot_tn(a_ref[...], b_ref[...])

        @pl.when(k == nk - 1)
        def _():
            for p in range(m):
                away, copy = sent(jnp.maximum(i - 1, 0), p)

                @pl.when(jnp.logical_and(i > 0, away))
                def _():
                    copy.wait_send()

            acc = acc_ref[...]
            stage_ref[...] = (acc.T if transpose else acc).astype(BF16)
            for p in range(m):
                away, copy = sent(i, p)

                @pl.when(away)
                def _():
                    copy.start()

                @pl.when(jnp.logical_not(away))
                def _():
                    out_ref[(i * m + p) // 2] = stage_ref[p * R:(p + 1) * R, :]

        @pl.when(jnp.logical_and(i == n_blocks - 1, k == nk - 1))
        def _():
            for p in range(m):
                away, copy = sent(i, p)

                @pl.when(away)
                def _():
                    copy.wait_send()

            for q in range(N_DEV // 2):
                pltpu.make_async_remote_copy(
                    src_ref=stage_ref.at[0:R], dst_ref=recv_ref.at[q], send_sem=send_sems.at[q],
                    recv_sem=recv_sems.at[q], device_id=(x, y, 1 - c), device_id_type=MESH).wait_recv()
                out_ref[q] = (out_ref[q].astype(F32) + recv_ref[q].astype(F32)).astype(BF16)

    fn, xa, xs = _ordered(body, 2, after)
    return pl.pallas_call(
        fn, name=name, grid=(n_blocks, nk), in_specs=[a_spec, b_spec] + xs,
        out_specs=pl.BlockSpec((N_DEV // 2, R, N), lambda i, k: (0, 0, 0)),
        out_shape=jax.ShapeDtypeStruct((N_DEV // 2, R, N), BF16),
        scratch_shapes=[pltpu.VMEM(acc_block, F32), pltpu.VMEM(block, BF16), pltpu.VMEM((N_DEV // 2, R, N), BF16),
                        pltpu.SemaphoreType.DMA((N_DEV // 2,)), pltpu.SemaphoreType.DMA((N_DEV // 2,))],
        compiler_params=_params(dimension_semantics=("arbitrary", "arbitrary"), collective_id=SIBLING_PAIR_ID),
    )(a, b, *xa)


Copy = collections.namedtuple("Copy", "mask sb src db dst sem")
Local = collections.namedtuple("Local", "sb src db dst")

HBM_SPEC = pl.BlockSpec(memory_space=pltpu.HBM)
SEM_SPEC = pl.BlockSpec(memory_space=pltpu.SEMAPHORE)
SIBLING, X_PEER, Y_PEER, DIAGONAL = 1, 4, 2, 6
OTHER_CHIPS = (X_PEER, Y_PEER, DIAGONAL)


def _whole(ref, i):
    return ref


def _lead(ref, i):
    return ref.at[i]


def _second(ref, i):
    return ref.at[:, i]


def _place():
    x, y, c = lax.axis_index("x"), lax.axis_index("y"), lax.axis_index("c")
    return (x, y, c), 4 * x + 2 * y + c


def _descriptor(cp, bufs, xyc, me, sender, send_sems, recv_sems):
    x, y, c = xyc
    flip = lambda v, bit: (1 - v) if bit else v
    return pltpu.make_async_remote_copy(
        src_ref=cp.src(bufs[cp.sb], me), dst_ref=cp.dst(bufs[cp.db], sender),
        send_sem=send_sems.at[cp.sem], recv_sem=recv_sems.at[cp.sem],
        device_id=(flip(x, cp.mask & 4), flip(y, cp.mask & 2), flip(c, cp.mask & 1)), device_id_type=MESH)


def _exchange(name, bufs, plan, local=()):
    n = len(bufs)

    def body(*refs):
        ins = refs[:n]
        send_sems, recv_sems, local_sems = refs[2 * n:]
        xyc, me = _place()
        own = [pltpu.make_async_copy(lc.src(ins[lc.sb], me), lc.dst(ins[lc.db], me), local_sems.at[i])
               for i, lc in enumerate(local)]
        sends = [_descriptor(cp, ins, xyc, me, me, send_sems, recv_sems) for cp in plan]
        for cp in own + sends:
            cp.start()
        for cp in plan:
            _descriptor(cp, ins, xyc, me, me ^ cp.mask, send_sems, recv_sems).wait_recv()
        for cp in sends:
            cp.wait_send()
        for cp in own:
            cp.wait()

    return pl.pallas_call(
        body, name=name, in_specs=[HBM_SPEC] * n, out_specs=[HBM_SPEC] * n,
        out_shape=[jax.ShapeDtypeStruct(b.shape, b.dtype) for b in bufs],
        input_output_aliases={i: i for i in range(n)},
        scratch_shapes=[pltpu.SemaphoreType.DMA((len(plan),)), pltpu.SemaphoreType.DMA((len(plan),)),
                        pltpu.SemaphoreType.DMA((max(len(local), 1),))],
    )(*bufs)


def _place_own(me, items):
    def body(me_ref, *refs):
        for src, dst in zip(refs[:len(items)], refs[len(items):]):
            dst[...] = src[...].astype(dst.dtype)

    return pl.pallas_call(
        body, name="place_own",
        grid_spec=pltpu.PrefetchScalarGridSpec(
            num_scalar_prefetch=1, grid=(1,),
            in_specs=[pl.BlockSpec(blk, functools.partial(lambda i, m, idx: idx, idx=idx)) for _, blk, idx, _, _, _, _ in items],
            out_specs=[pl.BlockSpec(oblk, functools.partial(lambda i, m, at: at(m[0]), at=at)) for *_, oblk, at in items]),
        out_shape=[jax.ShapeDtypeStruct(shape, dtype) for _, _, _, shape, dtype, _, _ in items],
        compiler_params=_params(dimension_semantics=("arbitrary",)),
    )(jnp.reshape(me, (1,)).astype(jnp.int32), *[a for a, *_ in items])


def _split_call(name, bufs, *, wait=None, wait_sems=None, start=None, local=(), after=None, token=False):
    n = len(bufs)
    n_wait = 2 if wait else 0
    n_after = 1 if after is not None else 0
    n_start = 2 if start else 0

    def body(*refs):
        ins = refs[:n]
        wsend, wrecv = refs[n:n + n_wait] if wait else (None, None)
        outs = refs[n + n_wait + n_after:]
        ssend, srecv = outs[:n_start] if start else (None, None)
        rest = outs[n_start + n:]
        xyc, me = _place()
        for cp in wait or ():
            d = _descriptor(cp, ins, xyc, me, me ^ cp.mask, wsend, wrecv)
            d.wait_send()
            d.wait_recv()
        own = [pltpu.make_async_copy(lc.src(ins[lc.sb], me), lc.dst(ins[lc.db], me), rest[-1].at[i])
               for i, lc in enumerate(local)]
        for cp in own:
            cp.start()
        for cp in start or ():
            _descriptor(cp, ins, xyc, me, me, ssend, srecv).start()
        for cp in own:
            cp.wait()
        if token:
            rest[0][...] = jnp.zeros_like(rest[0])

    args = [pltpu.with_memory_space_constraint(b, pltpu.HBM) for b in bufs]
    in_specs = [HBM_SPEC] * n
    if wait:
        args += list(wait_sems)
        in_specs += [SEM_SPEC] * 2
    if after is not None:
        args.append(after)
        in_specs.append(pl.BlockSpec(memory_space=pl.ANY))
    out_shape, out_specs = [], []
    if start:
        out_shape += [pltpu.SemaphoreType.DMA((len(start),))] * 2
        out_specs += [SEM_SPEC] * 2
    out_shape += [pltpu.HBM(b.shape, b.dtype) for b in bufs]
    out_specs += [HBM_SPEC] * n
    if token:
        out_shape.append(jax.ShapeDtypeStruct((8, 128), F32))
        out_specs.append(pl.BlockSpec(memory_space=pltpu.VMEM))
    outs = pl.pallas_call(
        body, name=name, in_specs=in_specs, out_specs=out_specs, out_shape=out_shape,
        input_output_aliases={i: n_start + i for i in range(n)},
        scratch_shapes=[pltpu.SemaphoreType.DMA((len(local),))] if local else [],
        compiler_params=pltpu.CompilerParams(has_side_effects=pltpu.SideEffectType.DATAFLOW_SIDE_EFFECTING),
    )(*args)
    sems = tuple(outs[:n_start]) if start else None
    return sems, list(outs[n_start:n_start + n]), (outs[n_start + n] if token else None)


def _adamw(w, g, m, v):
    m = ADAM_B1 * m + (1.0 - ADAM_B1) * g
    v = ADAM_B2 * v + (1.0 - ADAM_B2) * (g * g)
    m_hat = m / (1.0 - ADAM_B1 ** ADAM_STEP)
    v_hat = v / (1.0 - ADAM_B2 ** ADAM_STEP)
    delta = -ADAM_LR * (m_hat / (jnp.sqrt(v_hat) + ADAM_EPS) + ADAM_WD * w)
    return delta, m, v


def _reduce_adam(name, parts, w, m, v, *, tr, layer=None, into=None, after=None):
    L, R, C = w.shape
    S = parts[0].shape[0]
    tr = min(tr, R)
    n_l = L if layer is None else 1
    first = 0 if layer is None else layer

    def body(*refs):
        p_refs = refs[:n_l]
        w_ref, m_ref, v_ref = refs[n_l:n_l + 3]
        g_ref, d_ref, nm_ref, nv_ref = refs[-4:]
        for l in range(n_l):
            g = p_refs[l][0].astype(F32)
            for s in range(1, S):
                g = g + p_refs[l][s].astype(F32)
            g_ref[l] = g
            d_ref[l], nm_ref[l], nv_ref[l] = _adamw(w_ref[l], g, m_ref[l], v_ref[l])

    blk = pl.BlockSpec((n_l, tr, C), lambda r: (first, r, 0))
    out = jax.ShapeDtypeStruct((L, R, C), F32)
    extra = list(into or []) + ([after] if after is not None else [])
    return pl.pallas_call(
        body, name=name, grid=(R // tr,),
        in_specs=[pl.BlockSpec((S, tr, C), lambda r: (0, r, 0))] * n_l + [blk, blk, blk]
        + [pl.BlockSpec(memory_space=pl.ANY)] * len(extra),
        out_specs=[blk] * 4, out_shape=[out] * 4,
        input_output_aliases={n_l + 3 + i: i for i in range(4)} if into else {},
        compiler_params=_params(dimension_semantics=("arbitrary",)),
    )(*parts, w, m, v, *extra)


def _small_reduce(parts):
    D = parts.shape[2]
    rows = [0, 1, 8, 9, 16, 17, 24, 25, 18, 19, 20, 2, 32]

    def body(p_ref, out_ref):
        s = p_ref[0]
        for d in range(1, N_DEV):
            s = s + p_ref[d]
        out_ref[...] = jnp.zeros_like(out_ref)
        for r, src in enumerate(rows):
            out_ref[r:r + 1, :] = s[src:src + 1, :]

    return pl.pallas_call(body, name="small_reduce", out_shape=jax.ShapeDtypeStruct((16, D), F32))(parts)


def _small_adam(g_gain, g_taps, g_scale, gains, taps, scale):
    def body(gg, gt, gs, wg, mg, vg, wt, mt, vt, ws, ms, vs, *outs):
        for k, (g, w, m, v) in enumerate(((gg, wg, mg, vg), (gt, wt, mt, vt), (gs, ws, ms, vs))):
            outs[3 * k][...], outs[3 * k + 1][...], outs[3 * k + 2][...] = _adamw(w[...], g[...], m[...], v[...])

    shapes = [jax.ShapeDtypeStruct(t[0].shape, F32) for t in (gains, taps, scale) for _ in range(3)]
    return pl.pallas_call(body, name="small_adam", out_shape=shapes)(g_gain, g_taps, g_scale, *gains, *taps, *scale)


def kernel(x, norm_gains, pool_w, pool_scale, conv_in_w, conv_w, conv_out_w, ffn_gate_up_w, ffn_down_w, loss_target, m_norm_gains, m_pool_w, m_pool_scale, m_conv_in_w, m_conv_w, m_conv_out_w, m_ffn_gate_up_w, m_ffn_down_w, v_norm_gains, v_pool_w, v_pool_scale, v_conv_in_w, v_conv_w, v_conv_out_w, v_ffn_gate_up_w, v_ffn_down_w):
    T, D = x.shape[1], x.shape[2]
    tm = min(512, T)
    tm_b = min(256, T)
    tk = min(2048, T)
    n_layers = ffn_gate_up_w.shape[0]
    fb = ffn_gate_up_w.shape[2]
    fr = ffn_down_w.shape[1]
    dcol = norm_gains.shape[2]
    cb = conv_in_w.shape[2]
    gw = pool_w.shape[3]
    me = 4 * lax.axis_index("x") + 2 * lax.axis_index("y") + lax.axis_index("c")

    small_w = jnp.concatenate([norm_gains.reshape(8, dcol), jnp.pad(conv_w[0], ((0, 5), (0, 0)))], axis=0)
    every = range(1, N_DEV)
    wgu_t, m_wgu_t, v_wgu_t = (jnp.swapaxes(a, 1, 2) for a in (ffn_gate_up_w, m_ffn_gate_up_w, v_ffn_gate_up_w))
    own_lead = lambda s: lax.dynamic_update_slice(lax.empty((N_DEV,) + s.shape, s.dtype), s[None], (me,) + (0,) * s.ndim)
    lead_item = lambda a, l, dtype: (a, (None,) + a.shape[1:], (l, 0, 0), (N_DEV,) + a.shape[1:], dtype,
                                     (None,) + a.shape[1:], lambda i: (i, 0, 0))
    lands = _place_own(me, [
        (pool_w, (None,) + pool_w.shape[1:], (0, 0, 0, 0), (4, N_DEV, gw // N_DEV, gw), BF16,
         (4, None, gw // N_DEV, gw), lambda i: (0, i, 0, 0)),
        (small_w[None], (None,) + small_w.shape, (0, 0, 0), (N_DEV,) + small_w.shape, F32, (None,) + small_w.shape, lambda i: (i, 0, 0)),
        lead_item(wgu_t, 0, BF16), lead_item(ffn_down_w, 0, BF16), lead_item(conv_in_w, 0, BF16),
        lead_item(conv_out_w, 0, BF16), lead_item(wgu_t, 1, BF16), lead_item(ffn_down_w, 1, BF16)])
    n_first, n_big = 2, len(lands) - 2
    direct = ([Copy(m, 0, _second, 0, _second, m - 1) for m in every]
              + [Copy(m, 1, _lead, 1, _lead, N_DEV - 2 + m) for m in every])
    level1 = [Copy(mask, n_first + n, _lead, n_first + n, _lead, len(direct) + 4 * n + j)
              for n in range(n_big) for j, mask in enumerate((SIBLING,) + OTHER_CHIPS)]
    sems1, bufs1, _ = _split_call("gather_start", lands, start=direct + level1)
    pw_g, small_g = _split_call("gather_small_done", bufs1[:n_first], wait=direct, wait_sems=sems1)[1]
    pw = pw_g.reshape(4, gw, gw)
    small_full = jnp.swapaxes(small_g, 0, 1).reshape(16, D)
    gain = lambda l, s: small_full[4 * l + s][None, :]
    taps = small_full[8:16]

    def forward_on(name, group, after):
        k = len(group)
        landed = [Copy(cp.mask, i, cp.src, i, cp.dst, cp.sem)
                  for i, n in enumerate(group) for cp in level1 if cp.sb == n_first + n]
        onward = [Copy(SIBLING, i, (lambda ref, me, m=m: ref.at[me ^ m]), i, (lambda ref, sender, m=m: ref.at[sender ^ m]), 3 * i + j)
                  for i in range(k) for j, m in enumerate(OTHER_CHIPS)]
        sems2, bufs2, tok = _split_call(name + "_forward", [bufs1[n_first + n] for n in group], wait=landed,
                                        wait_sems=sems1, start=onward, after=after, token=True)
        return (name, onward, sems2, bufs2), tok

    def arrived(state, after=None):
        name, onward, sems2, lands2 = state
        return _split_call(name + "_done", lands2, wait=onward, wait_sems=sems2, after=after)[1]

    h0 = x[0]
    h1 = _pool_fwd(h0, pw, pool_scale, gain(0, 0), gain(0, 1), tm=tm)
    (wgu0,) = arrived(forward_on("gather_gate_up_0", [0], h1)[0])
    wgu0 = wgu0.reshape(N_DEV * fb, D)
    gu0, act0 = _ffn_up(h1, gain(0, 2), wgu0, tm=tm)
    ag_down0, tok = forward_on("gather_down_0", [1], act0)
    ag_conv, tok = forward_on("gather_conv", [2, 3], tok)
    (wd0,) = arrived(ag_down0, tok)
    wd0 = wd0.reshape(N_DEV * fr, D)
    h2, ff0 = _ffn_down(h1, act0, gain(0, 3), wd0, tm=tm)
    win_g, wout_g = arrived(ag_conv, h2)
    wout = wout_g.reshape(D, D)
    h3, proj, y = _conv_fwd(h2, gain(1, 0), gain(1, 1), win_g, taps, wout, tm=tm)
    ag_ffn1, tok = forward_on("gather_ffn1", [4, 5], h3)
    wgu1, wd1 = arrived(ag_ffn1, tok)
    wgu1, wd1 = wgu1.reshape(N_DEV * fb, D), wd1.reshape(N_DEV * fr, D)
    dh4, gu1, act1, ff1, loss_part = _ffn_fwd(h3, gain(1, 2), gain(1, 3), wgu1, wd1, 1, loss_target[0], tm=tm)

    chip = me >> 1

    def scatter_start(name, sums):
        k = len(sums)
        lands = [lax.dynamic_update_slice(lax.empty(s.shape, BF16), lax.dynamic_index_in_dim(s, chip, 0), (chip, 0, 0))
                 for s in sums]
        plan = [Copy(m, n, (lambda ref, i, m=m: ref.at[(i ^ m) >> 1]), k + n, (lambda ref, i: ref.at[i >> 1]), 3 * n + j)
                for n in range(k) for j, m in enumerate(OTHER_CHIPS)]
        sems, bufs, tok = _split_call(name + "_start", sums + lands, start=plan, token=True)
        return (name, plan, sems, bufs), tok

    def scatter_done(state, after):
        name, plan, sems, bufs = state
        return _split_call(name + "_done", bufs, wait=plan, wait_sems=sems, after=after)[1][len(bufs) // 2:]

    seq = lambda i, k: (k, 0)
    gu_pair = pl.BlockSpec((None, tk, 2 * fb), lambda i, k: (i // 2, k, i % 2))
    act_pair = pl.BlockSpec((tk, 2 * fb), lambda i, k: (k, i))
    rows = pl.BlockSpec((tk, D), seq)
    dh3, dgu1, dff1, c1, small_f1 = _ffn_bwd(dh4, h3, ff1, gu1, gain(1, 2), gain(1, 3), wgu1, wd1, 1, tm=tm_b)
    g_wgu1 = _wgrad("wgrad_gate_up_1", dgu1, c1, gu_pair, rows, (2 * fb, D), N_DEV // 2, tk=tk)
    g_wd1 = _wgrad("wgrad_down_1", act1, dff1, act_pair, rows, (2 * fb, D), N_DEV // 4, tk=tk)
    rs_ffn1, tok = scatter_start("scatter_ffn1", [g_wgu1, g_wd1])
    dh2, dproj, z, a1, dy, small_c = _conv_bwd(dh3, h2, y, proj, gain(1, 0), gain(1, 1), win_g, taps, wout, tm=tm, after=tok)
    g_win = _wgrad("wgrad_conv_in", dproj, a1, pl.BlockSpec((tk, cb), lambda i, k: (k, i)), rows, (D, cb), N_DEV,
                   tk=tk, transpose=True)
    g_wout = _wgrad("wgrad_conv_out", z, dy, rows, rows, (D, D), 1, tk=tk)
    rs_conv, tok = scatter_start("scatter_conv", [g_win, g_wout])
    dh1, dgu0, dff0, c0, small_f0 = _ffn_bwd(dh2, h1, ff0, gu0, gain(0, 2), gain(0, 3), wgu0, wd0, 0, tm=tm_b, after=tok)
    g_wgu0 = _wgrad("wgrad_gate_up_0", dgu0, c0, gu_pair, rows, (2 * fb, D), N_DEV // 2, tk=tk)
    rs_wgu0, tok = scatter_start("scatter_gate_up_0", [g_wgu0])
    g_wd0 = _wgrad("wgrad_down_0", act0, dff0, act_pair, rows, (2 * fb, D), N_DEV // 4, tk=tk, after=tok)
    rs_wd0, tok = scatter_start("scatter_down_0", [g_wd0])
    grad_x, g_pw, small_p = _pool_bwd(dh1, h0, pw, pool_scale, gain(0, 0), gain(0, 1), tm=tm, after=tok)

    loss_rows = jnp.broadcast_to(loss_part[0:1, 0:1], (8, D))
    small_part = jnp.concatenate([small_p, small_f0, small_c, small_f1, loss_rows], axis=0)
    g_pw = g_pw.reshape(4, N_DEV, gw // N_DEV, gw)
    pw_land = lax.dynamic_update_slice(lax.empty(g_pw.shape, BF16), lax.dynamic_slice_in_dim(g_pw, me, 1, 1), (0, me, 0, 0))
    last = ([Copy(m, 0, (lambda ref, i, m=m: ref.at[:, i ^ m]), 2, _second, m - 1) for m in every]
            + [Copy(m, 1, _whole, 3, _lead, N_DEV - 2 + m) for m in every])
    sems_l, bufs_l, tok = _split_call("scatter_small_start", [g_pw, small_part, pw_land, own_lead(small_part)],
                                      start=last, token=True)

    (r_wgu1, r_wd1), (r_win, r_wout) = scatter_done(rs_ffn1, tok), scatter_done(rs_conv, tok)
    o_win = _reduce_adam("adam_conv_in", [r_win], conv_in_w, m_conv_in_w, v_conv_in_w, tr=256)
    o_wout = _reduce_adam("adam_conv_out", [r_wout], conv_out_w, m_conv_out_w, v_conv_out_w, tr=128, after=o_win[0])
    o_wgu = _reduce_adam("adam_gate_up_1", [r_wgu1], wgu_t, m_wgu_t, v_wgu_t, tr=176, layer=1, after=o_wout[0])
    o_wd = _reduce_adam("adam_down_1", [r_wd1], ffn_down_w, m_ffn_down_w, v_ffn_down_w, tr=176, layer=1, after=o_wgu[0])
    r_pw, r_small = _split_call("scatter_small_done", bufs_l, wait=last, wait_sems=sems_l, after=o_wd[0])[1][2:]
    (r_wgu0,), (r_wd0,) = scatter_done(rs_wgu0, r_small), scatter_done(rs_wd0, r_small)
    o_wgu = _reduce_adam("adam_gate_up_0", [r_wgu0], wgu_t, m_wgu_t, v_wgu_t, tr=176, layer=0, into=o_wgu)
    o_wgu = [jnp.swapaxes(o, 1, 2) for o in o_wgu]
    o_wd = _reduce_adam("adam_down_0", [r_wd0], ffn_down_w, m_ffn_down_w, v_ffn_down_w, tr=176, layer=0, into=o_wd)
    o_pw = _reduce_adam("adam_pool_w", [r_pw[g] for g in range(4)], pool_w[0], m_pool_w[0], v_pool_w[0], tr=32)
    g_small = _small_reduce(r_small)
    loss = g_small[12, 0]
    g_cols = lax.dynamic_slice(g_small, (0, me * dcol), (16, dcol))
    o_small = _small_adam(
        g_cols[0:8], g_cols[8:11], g_small[11:12],
        (norm_gains.reshape(8, dcol), m_norm_gains.reshape(8, dcol), v_norm_gains.reshape(8, dcol)),
        (conv_w[0], m_conv_w[0], v_conv_w[0]), (pool_scale, m_pool_scale, v_pool_scale))
    d_gain, nm_gain, nv_gain, d_taps, nm_taps, nv_taps, d_scale, nm_scale, nv_scale = o_small

    gshape = norm_gains.shape
    per = lambda k: (
        (g_cols[0:8].reshape(gshape), d_gain.reshape(gshape), nm_gain.reshape(gshape), nv_gain.reshape(gshape))[k],
        o_pw[k][None], (g_small[11:12], d_scale, nm_scale, nv_scale)[k], o_win[k],
        (g_cols[8:11][None], d_taps[None], nm_taps[None], nv_taps[None])[k], o_wout[k], o_wgu[k], o_wd[k])
    return (loss, grad_x[None], *per(0), *per(1), *per(2), *per(3))
```

```python
import collections
import functools

import jax
import jax.numpy as jnp
from jax import lax
from jax.experimental import pallas as pl
from jax.experimental.pallas import tpu as pltpu

N_DEV = 8
RMS_EPS = 1e-6
POOL_WINDOWS = (2, 4, 8, 16)
POOL_HALO = 16
CONV_HALO = 16
ADAM_LR, ADAM_B1, ADAM_B2, ADAM_EPS, ADAM_WD, ADAM_STEP = 0.001, 0.9, 0.999, 1e-08, 0.01, 10

VMEM_LIMIT = 56 * 2**20
MXU_COLUMNS = 256
BF16 = jnp.bfloat16
F32 = jnp.float32
MESH = pl.DeviceIdType.MESH
SIBLING_PAIR_ID = 0


def _params(**kw):
    return pltpu.CompilerParams(vmem_limit_bytes=VMEM_LIMIT, **kw)


def _resident(shape, index_map):
    return pl.BlockSpec(shape, index_map, pipeline_mode=pl.Buffered(1))


def _ordered(body, n_in, after):
    if after is None:
        return functools.partial(body), [], []
    return (lambda *refs: body(*refs[:n_in], *refs[n_in + 1:])), [after], [pl.BlockSpec(memory_space=pl.ANY)]


def _rms_fwd(x, g):
    r = lax.rsqrt(jnp.mean(x * x, axis=-1, keepdims=True) + RMS_EPS)
    return x * r * g


def _rms_bwd(x, g, dy):
    r = lax.rsqrt(jnp.mean(x * x, axis=-1, keepdims=True) + RMS_EPS)
    xhat = x * r
    dg = jnp.sum(dy * xhat, axis=0, keepdims=True)
    t = dy * g
    dx = r * (t - xhat * jnp.mean(t * xhat, axis=-1, keepdims=True))
    return dx, dg


def _sigmoid(x):
    return 0.5 * jnp.tanh(0.5 * x) + 0.5


def _dot(a, b):
    return jnp.dot(a, b, preferred_element_type=F32)


def _dot_nt(a, b):
    return lax.dot_general(a, b, (((1,), (1,)), ((), ())), preferred_element_type=F32)


def _dot_tn(a, b):
    return lax.dot_general(a, b, (((0,), (0,)), ((), ())), preferred_element_type=F32)


def _join_blocks(blocks_hbm, joined_ref, sems):
    n, _, C = blocks_hbm.shape
    copies = [pltpu.make_async_copy(blocks_hbm.at[k], joined_ref.at[:, k * C:(k + 1) * C], sems.at[k]) for k in range(n)]
    for cp in copies:
        cp.start()
    for cp in copies:
        cp.wait()


def _row_inverse_counts(tile, tm):
    pos = (lax.broadcasted_iota(jnp.int32, (tm, 1), 0) + tile * tm + 1).astype(F32)
    return [1.0 / jnp.minimum(pos, float(w)) for w in POOL_WINDOWS]


def _pool_from_ext(ext, a, invs, gw):
    s = ext
    outs = []
    for g, w in enumerate(POOL_WINDOWS):
        s = s[:, (gw if g else 0):]
        s = s + pltpu.roll(s, w // 2, 0)
        outs.append(s[POOL_HALO:, :gw] * invs[g] - a[:, g * gw:(g + 1) * gw])
    return outs


def _pool_fwd(h, pw, scale, g_pre, g_post, *, tm, after=None):
    T, D = h.shape
    gw = D // len(POOL_WINDOWS)
    hb = tm // POOL_HALO

    def body(h_ref, halo_ref, pw_ref, scale_ref, gpre_ref, gpost_ref, out_ref, ext_ref):
        i = pl.program_id(0)
        x = h_ref[...]
        a = _rms_fwd(x, gpre_ref[...])
        ah = _rms_fwd(halo_ref[...], gpre_ref[...])
        ext_ref[0:POOL_HALO, :] = jnp.where(i == 0, 0.0, ah)
        ext_ref[POOL_HALO:, :] = a
        pooled = _pool_from_ext(ext_ref[...], a, _row_inverse_counts(i, tm), gw)
        mixed = jnp.concatenate([_dot(p.astype(BF16), pw_ref[g]) for g, p in enumerate(pooled)], axis=1)
        out_ref[...] = x + _rms_fwd(mixed * scale_ref[...], gpost_ref[...])

    vec = _resident((1, D), lambda i: (0, 0))
    fn, xa, xs = _ordered(body, 6, after)
    return pl.pallas_call(
        fn, name="pool_fwd", grid=(T // tm,),
        in_specs=[pl.BlockSpec((tm, D), lambda i: (i, 0)),
                  pl.BlockSpec((POOL_HALO, D), lambda i: (jnp.maximum(i * hb - 1, 0), 0)),
                  _resident(pw.shape, lambda i: (0, 0, 0)), vec, vec, vec] + xs,
        out_specs=pl.BlockSpec((tm, D), lambda i: (i, 0)),
        out_shape=jax.ShapeDtypeStruct((T, D), F32),
        scratch_shapes=[pltpu.VMEM((tm + POOL_HALO, D), F32)],
        compiler_params=_params(dimension_semantics=("arbitrary",)),
    )(h, h, pw, scale, g_pre, g_post, *xa)


def _pool_bwd(dh, h, pw, scale, g_pre, g_post, *, tm, after=None):
    T, D = h.shape
    gw = D // len(POOL_WINDOWS)
    hb = tm // POOL_HALO
    nt = T // tm
    n_ext = tm + POOL_HALO

    def body(dh_ref, h_ref, halo_ref, pw_ref, scale_ref, gpre_ref, gpost_ref,
             dx_ref, dpw_ref, small_ref, ext_ref, ext2_ref, carry_ref, dpw_acc):
        i = pl.program_id(0)
        tile = nt - 1 - i

        @pl.when(i == 0)
        def _():
            small_ref[...] = jnp.zeros_like(small_ref)
            dpw_acc[...] = jnp.zeros_like(dpw_acc)
            carry_ref[...] = jnp.zeros_like(carry_ref)

        x = h_ref[...]
        dout = dh_ref[...]
        a = _rms_fwd(x, gpre_ref[...])
        ah = _rms_fwd(halo_ref[...], gpre_ref[...])
        ext_ref[0:POOL_HALO, :] = jnp.where(tile == 0, 0.0, ah)
        ext_ref[POOL_HALO:, :] = a
        invs = _row_inverse_counts(tile, tm)
        pooled = [p.astype(BF16) for p in _pool_from_ext(ext_ref[...], a, invs, gw)]
        mixed_pre = jnp.concatenate([_dot(p, pw_ref[g]) for g, p in enumerate(pooled)], axis=1)
        scale_v = scale_ref[...]
        dmixed, dg_post = _rms_bwd(mixed_pre * scale_v, gpost_ref[...], dout)
        small_ref[1:2, :] += dg_post
        small_ref[2:3, :] += jnp.sum(dmixed * mixed_pre, axis=0, keepdims=True)
        dpre = (dmixed * scale_v).astype(BF16)
        dpooled = []
        for g in range(len(POOL_WINDOWS)):
            dp = dpre[:, g * gw:(g + 1) * gw]
            dpw_acc[g] += _dot_tn(pooled[g], dp)
            dpooled.append(_dot_nt(dp, pw_ref[g]))
        q = jnp.concatenate([d * invs[g] for g, d in enumerate(dpooled)], axis=1)
        ext2_ref[0:tm, :] = q
        ext2_ref[tm:, :] = carry_ref[...]
        carry_ref[...] = q[0:POOL_HALO, :]
        s = ext2_ref[...]
        da = []
        for g, w in enumerate(POOL_WINDOWS):
            s = s[:, (gw if g else 0):]
            s = s + pltpu.roll(s, n_ext - w // 2, 0)
            da.append(s[0:tm, :gw] - dpooled[g])
        dx, dg_pre = _rms_bwd(x, gpre_ref[...], jnp.concatenate(da, axis=1))
        small_ref[0:1, :] += dg_pre
        dx_ref[...] = dout + dx

        @pl.when(i == nt - 1)
        def _():
            dpw_ref[...] = dpw_acc[...].astype(BF16)

    vec = _resident((1, D), lambda i: (0, 0))
    rev = lambda i: (nt - 1 - i, 0)
    fn, xa, xs = _ordered(body, 7, after)
    return pl.pallas_call(
        fn, name="pool_bwd", grid=(nt,),
        in_specs=[pl.BlockSpec((tm, D), rev), pl.BlockSpec((tm, D), rev),
                  pl.BlockSpec((POOL_HALO, D), lambda i: (jnp.maximum((nt - 1 - i) * hb - 1, 0), 0)),
                  _resident(pw.shape, lambda i: (0, 0, 0)), vec, vec, vec] + xs,
        out_specs=[pl.BlockSpec((tm, D), rev),
                   pl.BlockSpec(pw.shape, lambda i: (0, 0, 0)),
                   pl.BlockSpec((8, D), lambda i: (0, 0))],
        out_shape=[jax.ShapeDtypeStruct((T, D), F32), jax.ShapeDtypeStruct(pw.shape, BF16),
                   jax.ShapeDtypeStruct((8, D), F32)],
        scratch_shapes=[pltpu.VMEM((n_ext, D), F32), pltpu.VMEM((n_ext, D), F32),
                        pltpu.VMEM((POOL_HALO, D), F32), pltpu.VMEM(pw.shape, F32)],
        compiler_params=_params(dimension_semantics=("arbitrary",)),
    )(dh, h, h, pw, scale, g_pre, g_post, *xa)


def _ffn_fwd(h, g_pre, g_post, wgu, wd, layer, target, *, tm, after=None):
    T, D = h.shape
    F = wd.shape[0]
    last = target is not None

    def body(*refs):
        if last:
            h_ref, gpre_ref, gpost_ref, wgu_ref, wd_ref, tgt_ref, out_ref, gu_ref, act_ref, ff_ref, loss_ref = refs
        else:
            h_ref, gpre_ref, gpost_ref, wgu_ref, wd_ref, out_ref, gu_ref, act_ref, ff_ref = refs
        x = h_ref[...]
        cb = _rms_fwd(x, gpre_ref[...]).astype(BF16)
        g = _dot_nt(cb, wgu_ref[0:F, :])
        u = _dot_nt(cb, wgu_ref[F:2 * F, :])
        gu_ref[0] = g.astype(BF16)
        gu_ref[1] = u.astype(BF16)
        act = (g * _sigmoid(g) * u).astype(BF16)
        act_ref[...] = act
        acc = _dot(act, wd_ref[...])
        ff_ref[...] = acc.astype(BF16)
        hout = x + _rms_fwd(acc, gpost_ref[...])
        if last:
            diff = hout - tgt_ref[...]
            out_ref[...] = diff * (1.0 / D)

            @pl.when(pl.program_id(0) == 0)
            def _():
                loss_ref[...] = jnp.zeros_like(loss_ref)

            loss_ref[...] += jnp.sum(diff * diff) * (0.5 / D)
        else:
            out_ref[...] = hout

    vec = _resident((1, D), lambda i: (0, 0))
    tile = pl.BlockSpec((tm, D), lambda i: (i, 0))
    in_specs = [tile, vec, vec, _resident(wgu.shape, lambda i: (0, 0)), _resident(wd.shape, lambda i: (0, 0))]
    out_specs = [tile, pl.BlockSpec((2, tm, F), lambda i: (0, i, 0)), pl.BlockSpec((tm, F), lambda i: (i, 0)), tile]
    out_shape = [jax.ShapeDtypeStruct((T, D), F32), jax.ShapeDtypeStruct((2, T, F), BF16),
                 jax.ShapeDtypeStruct((T, F), BF16), jax.ShapeDtypeStruct((T, D), BF16)]
    args = [h, g_pre, g_post, wgu, wd]
    if last:
        in_specs.append(tile)
        args.append(target)
        out_specs.append(pl.BlockSpec((8, 128), lambda i: (0, 0)))
        out_shape.append(jax.ShapeDtypeStruct((8, 128), F32))
    fn, xa, xs = _ordered(body, len(args), after)
    return pl.pallas_call(
        fn, name=f"ffn_fwd_{layer}", grid=(T // tm,), in_specs=in_specs + xs, out_specs=out_specs,
        out_shape=out_shape, compiler_params=_params(dimension_semantics=("arbitrary",)),
    )(*args, *xa)


def _ffn_up(h, g_pre, wgu, *, tm):
    T, D = h.shape
    F = wgu.shape[0] // 2

    def body(h_ref, gpre_ref, wgu_ref, gu_ref, act_ref):
        cb = _rms_fwd(h_ref[...], gpre_ref[...]).astype(BF16)
        g = _dot_nt(cb, wgu_ref[0:F, :])
        u = _dot_nt(cb, wgu_ref[F:2 * F, :])
        gu_ref[0] = g.astype(BF16)
        gu_ref[1] = u.astype(BF16)
        act_ref[...] = (g * _sigmoid(g) * u).astype(BF16)

    return pl.pallas_call(
        body, name="ffn_up_0", grid=(T // tm,),
        in_specs=[pl.BlockSpec((tm, D), lambda i: (i, 0)), _resident((1, D), lambda i: (0, 0)),
                  _resident(wgu.shape, lambda i: (0, 0))],
        out_specs=[pl.BlockSpec((2, tm, F), lambda i: (0, i, 0)), pl.BlockSpec((tm, F), lambda i: (i, 0))],
        out_shape=[jax.ShapeDtypeStruct((2, T, F), BF16), jax.ShapeDtypeStruct((T, F), BF16)],
        compiler_params=_params(dimension_semantics=("arbitrary",)),
    )(h, g_pre, wgu)


def _ffn_down(h, act, g_post, wd, *, tm):
    T, D = h.shape
    F = act.shape[1]

    def body(h_ref, act_ref, gpost_ref, wd_ref, out_ref, ff_ref):
        acc = _dot(act_ref[...], wd_ref[...])
        ff_ref[...] = acc.astype(BF16)
        out_ref[...] = h_ref[...] + _rms_fwd(acc, gpost_ref[...])

    tile = pl.BlockSpec((tm, D), lambda i: (i, 0))
    return pl.pallas_call(
        body, name="ffn_down_0", grid=(T // tm,),
        in_specs=[tile, pl.BlockSpec((tm, F), lambda i: (i, 0)), _resident((1, D), lambda i: (0, 0)),
                  _resident(wd.shape, lambda i: (0, 0))],
        out_specs=[tile, tile],
        out_shape=[jax.ShapeDtypeStruct((T, D), F32), jax.ShapeDtypeStruct((T, D), BF16)],
        compiler_params=_params(dimension_semantics=("arbitrary",)),
    )(h, act, g_post, wd)


def _ffn_bwd(dh, h, ff, gu, g_pre, g_post, wgu, wd, layer, *, tm, after=None):
    T, D = h.shape
    F = wd.shape[0]
    n_chunks = F // MXU_COLUMNS

    def body(dh_ref, h_ref, ff_ref, gu_ref, gpre_ref, gpost_ref, wgu_ref, wd_ref,
             dx_ref, dgu_ref, dff_ref, c_ref, small_ref):
        @pl.when(pl.program_id(0) == 0)
        def _():
            small_ref[...] = jnp.zeros_like(small_ref)

        dout = dh_ref[...]
        dff, dg_post = _rms_bwd(ff_ref[...].astype(F32), gpost_ref[...], dout)
        small_ref[1:2, :] += dg_post
        dffb = dff.astype(BF16)
        dff_ref[...] = dffb
        dc = jnp.zeros((tm, D), F32)
        for j in range(n_chunks + 1):
            lo, hi = j * MXU_COLUMNS, (j + 1) * MXU_COLUMNS
            if j < n_chunks:
                dact = _dot_nt(dffb, wd_ref[lo:hi, :])
            if j > 0:
                lo0 = lo - MXU_COLUMNS
                dc = dc + _dot(dgu_ref[0, :, lo0:lo], wgu_ref[lo0:lo, :]) + _dot(dgu_ref[1, :, lo0:lo], wgu_ref[F + lo0:F + lo, :])
            if j < n_chunks:
                g = gu_ref[0, :, lo:hi].astype(F32)
                u = gu_ref[1, :, lo:hi].astype(F32)
                s = _sigmoid(g)
                dgu_ref[0, :, lo:hi] = (dact * u * (s * (1.0 + g * (1.0 - s)))).astype(BF16)
                dgu_ref[1, :, lo:hi] = (dact * (g * s)).astype(BF16)
        x = h_ref[...]
        c_ref[...] = _rms_fwd(x, gpre_ref[...]).astype(BF16)
        dx, dg_pre = _rms_bwd(x, gpre_ref[...], dc)
        small_ref[0:1, :] += dg_pre
        dx_ref[...] = dout + dx

    vec = _resident((1, D), lambda i: (0, 0))
    tile = pl.BlockSpec((tm, D), lambda i: (i, 0))
    blk = pl.BlockSpec((2, tm, F), lambda i: (0, i, 0))
    fn, xa, xs = _ordered(body, 8, after)
    return pl.pallas_call(
        fn, name=f"ffn_bwd_{layer}", grid=(T // tm,),
        in_specs=[tile, tile, tile, blk, vec, vec,
                  _resident(wgu.shape, lambda i: (0, 0)), _resident(wd.shape, lambda i: (0, 0))] + xs,
        out_specs=[tile, blk, tile, tile, pl.BlockSpec((8, D), lambda i: (0, 0))],
        out_shape=[jax.ShapeDtypeStruct((T, D), F32), jax.ShapeDtypeStruct((2, T, F), BF16),
                   jax.ShapeDtypeStruct((T, D), BF16), jax.ShapeDtypeStruct((T, D), BF16),
                   jax.ShapeDtypeStruct((8, D), F32)],
        compiler_params=_params(dimension_semantics=("arbitrary",)),
    )(dh, h, ff, gu, g_pre, g_post, wgu, wd, *xa)


def _conv_fwd(h, g_pre, g_post, win, taps, wout, *, tm, after=None):
    T, D = h.shape
    nblk, cb = win.shape[0], win.shape[2]

    def body(h_ref, gpre_ref, gpost_ref, win_hbm, taps_ref, wout_ref,
             out_ref, proj_ref, y_ref, proj_scr, ext_ref, carry_ref, win_ref, win_sems):
        i = pl.program_id(0)

        @pl.when(i == 0)
        def _():
            carry_ref[...] = jnp.zeros_like(carry_ref)
            _join_blocks(win_hbm, win_ref, win_sems)

        x = h_ref[...]
        a = _rms_fwd(x, gpre_ref[...]).astype(BF16)
        proj_scr[...] = _dot(a, win_ref[...])
        proj_ref[...] = proj_scr[...].astype(BF16)
        u = proj_scr[:, D:2 * D] * proj_scr[:, 2 * D:3 * D]
        ext_ref[0:CONV_HALO, :] = carry_ref[...]
        ext_ref[CONV_HALO:, :] = u
        carry_ref[...] = u[tm - CONV_HALO:, :]
        e = ext_ref[...]
        conv = (taps_ref[2:3, :] * u + taps_ref[1:2, :] * pltpu.roll(e, 1, 0)[CONV_HALO:, :]
                + taps_ref[0:1, :] * pltpu.roll(e, 2, 0)[CONV_HALO:, :])
        z = (proj_scr[:, 0:D] * conv).astype(BF16)
        y = _dot(z, wout_ref[...])
        y_ref[...] = y.astype(BF16)
        out_ref[...] = x + _rms_fwd(y, gpost_ref[...])

    vec = _resident((1, D), lambda i: (0, 0))
    tile = pl.BlockSpec((tm, D), lambda i: (i, 0))
    fn, xa, xs = _ordered(body, 6, after)
    return pl.pallas_call(
        fn, name="conv_fwd", grid=(T // tm,),
        in_specs=[tile, vec, vec, pl.BlockSpec(memory_space=pl.ANY),
                  _resident(taps.shape, lambda i: (0, 0)), _resident(wout.shape, lambda i: (0, 0))] + xs,
        out_specs=[tile, pl.BlockSpec((tm, 3 * D), lambda i: (i, 0)), tile],
        out_shape=[jax.ShapeDtypeStruct((T, D), F32), jax.ShapeDtypeStruct((T, 3 * D), BF16),
                   jax.ShapeDtypeStruct((T, D), BF16)],
        scratch_shapes=[pltpu.VMEM((tm, 3 * D), F32), pltpu.VMEM((tm + CONV_HALO, D), F32),
                        pltpu.VMEM((CONV_HALO, D), F32), pltpu.VMEM((D, nblk * cb), BF16),
                        pltpu.SemaphoreType.DMA((nblk,))],
        compiler_params=_params(dimension_semantics=("arbitrary",)),
    )(h, g_pre, g_post, win, taps, wout, *xa)


def _conv_bwd(dh, h, y, proj, g_pre, g_post, win, taps, wout, *, tm, after=None):
    T, D = h.shape
    nblk, cb = win.shape[0], win.shape[2]
    nt = T // tm
    hb = tm // CONV_HALO
    n_ext = tm + CONV_HALO

    def body(dh_ref, h_ref, y_ref, proj_ref, halo_ref, gpre_ref, gpost_ref, win_hbm, taps_ref, wout_ref,
             dx_ref, dproj_ref, z_ref, a_ref, dy_ref, small_ref, ext_ref, ext2_ref, carry_ref, win_ref, win_sems):
        i = pl.program_id(0)
        tile = nt - 1 - i

        @pl.when(i == 0)
        def _():
            small_ref[...] = jnp.zeros_like(small_ref)
            carry_ref[...] = jnp.zeros_like(carry_ref)
            _join_blocks(win_hbm, win_ref, win_sems)

        dout = dh_ref[...]
        dy, dg_post = _rms_bwd(y_ref[...].astype(F32), gpost_ref[...], dout)
        small_ref[1:2, :] += dg_post
        dyb = dy.astype(BF16)
        dy_ref[...] = dyb
        dz = _dot_nt(dyb, wout_ref[...])
        bgate = proj_ref[:, 0:D].astype(F32)
        cgate = proj_ref[:, D:2 * D].astype(F32)
        v = proj_ref[:, 2 * D:3 * D].astype(F32)
        u = cgate * v
        uh = halo_ref[:, D:2 * D].astype(F32) * halo_ref[:, 2 * D:3 * D].astype(F32)
        ext_ref[0:CONV_HALO, :] = jnp.where(tile == 0, 0.0, uh)
        ext_ref[CONV_HALO:, :] = u
        e = ext_ref[...]
        u1 = pltpu.roll(e, 1, 0)[CONV_HALO:, :]
        u2 = pltpu.roll(e, 2, 0)[CONV_HALO:, :]
        t0, t1, t2 = taps_ref[0:1, :], taps_ref[1:2, :], taps_ref[2:3, :]
        conv = t2 * u + t1 * u1 + t0 * u2
        z_ref[...] = (bgate * conv).astype(BF16)
        dconv = dz * bgate
        small_ref[2:3, :] += jnp.sum(dconv * u2, axis=0, keepdims=True)
        small_ref[3:4, :] += jnp.sum(dconv * u1, axis=0, keepdims=True)
        small_ref[4:5, :] += jnp.sum(dconv * u, axis=0, keepdims=True)
        ext2_ref[0:tm, :] = dconv
        ext2_ref[tm:, :] = carry_ref[...]
        carry_ref[...] = dconv[0:CONV_HALO, :]
        e2 = ext2_ref[...]
        du = (t2 * dconv + t1 * pltpu.roll(e2, n_ext - 1, 0)[0:tm, :]
              + t0 * pltpu.roll(e2, n_ext - 2, 0)[0:tm, :])
        dproj_ref[:, 0:D] = (dz * conv).astype(BF16)
        dproj_ref[:, D:2 * D] = (du * v).astype(BF16)
        dproj_ref[:, 2 * D:3 * D] = (du * cgate).astype(BF16)
        da = _dot_nt(dproj_ref[...], win_ref[...])
        x = h_ref[...]
        a_ref[...] = _rms_fwd(x, gpre_ref[...]).astype(BF16)
        dx, dg_pre = _rms_bwd(x, gpre_ref[...], da)
        small_ref[0:1, :] += dg_pre
        dx_ref[...] = dout + dx

    vec = _resident((1, D), lambda i: (0, 0))
    rev = lambda i: (nt - 1 - i, 0)
    tile = pl.BlockSpec((tm, D), rev)
    wide = pl.BlockSpec((tm, 3 * D), rev)
    fn, xa, xs = _ordered(body, 10, after)
    return pl.pallas_call(
        fn, name="conv_bwd", grid=(nt,),
        in_specs=[tile, tile, tile, wide,
                  pl.BlockSpec((CONV_HALO, 3 * D), lambda i: (jnp.maximum((nt - 1 - i) * hb - 1, 0), 0)),
                  vec, vec, pl.BlockSpec(memory_space=pl.ANY),
                  _resident(taps.shape, lambda i: (0, 0)), _resident(wout.shape, lambda i: (0, 0))] + xs,
        out_specs=[tile, wide, tile, tile, tile, pl.BlockSpec((8, D), lambda i: (0, 0))],
        out_shape=[jax.ShapeDtypeStruct((T, D), F32), jax.ShapeDtypeStruct((T, 3 * D), BF16),
                   jax.ShapeDtypeStruct((T, D), BF16), jax.ShapeDtypeStruct((T, D), BF16),
                   jax.ShapeDtypeStruct((T, D), BF16), jax.ShapeDtypeStruct((8, D), F32)],
        scratch_shapes=[pltpu.VMEM((n_ext, D), F32), pltpu.VMEM((n_ext, D), F32),
                        pltpu.VMEM((CONV_HALO, D), F32), pltpu.VMEM((D, nblk * cb), BF16),
                        pltpu.SemaphoreType.DMA((nblk,))],
        compiler_params=_params(dimension_semantics=("arbitrary",)),
    )(dh, h, y, proj, proj, g_pre, g_post, win, taps, wout, *xa)


def _wgrad(name, a, b, a_spec, b_spec, block, n_blocks, *, tk, after=None):
    T = a.shape[-2]
    nk = T // tk
    M, N = block
    m = N_DEV // n_blocks
    R = M // m

    def body(a_ref, b_ref, out_ref, acc_ref, stage_ref, recv_ref, send_sems, recv_sems):
        i, k = pl.program_id(0), pl.program_id(1)
        x, y, c = lax.axis_index("x"), lax.axis_index("y"), lax.axis_index("c")

        def sent(blk, p):
            owner = blk * m + p
            q = owner // 2
            return (owner % 2) != c, pltpu.make_async_remote_copy(
                src_ref=stage_ref.at[p * R:(p + 1) * R], dst_ref=recv_ref.at[q], send_sem=send_sems.at[q],
                recv_sem=recv_sems.at[q], device_id=(x, y, 1 - c), device_id_type=MESH)

        @pl.when(jnp.logical_and(i == 0, k == 0))
        def _():
            barrier = pltpu.get_barrier_semaphore()
            pl.semaphore_signal(barrier, inc=1, device_id=(x, y, 1 - c), device_id_type=MESH)
            pl.semaphore_wait(barrier, 1)

        @pl.when(k == 0)
        def _():
            acc_ref[...] = jnp.zeros_like(acc_ref)

        acc_ref[...] += _dot_tn(a_ref[...], b_ref[...])

        @pl.when(k == nk - 1)
        def _():
            for p in range(m):
                away, copy = sent(jnp.maximum(i - 1, 0), p)

                @pl.when(jnp.logical_and(i > 0, away))
                def _():
                    copy.wait_send()

            acc = acc_ref[...]
            stage_ref[...] = acc.astype(BF16)
            for p in range(m):
                away, copy = sent(i, p)

                @pl.when(away)
                def _():
                    copy.start()

                @pl.when(jnp.logical_not(away))
                def _():
                    out_ref[(i * m + p) // 2] = stage_ref[p * R:(p + 1) * R, :]

        @pl.when(jnp.logical_and(i == n_blocks - 1, k == nk - 1))
        def _():
            for p in range(m):
                away, copy = sent(i, p)

                @pl.when(away)
                def _():
                    copy.wait_send()

            for q in range(N_DEV // 2):
                pltpu.make_async_remote_copy(
                    src_ref=stage_ref.at[0:R], dst_ref=recv_ref.at[q], send_sem=send_sems.at[q],
                    recv_sem=recv_sems.at[q], device_id=(x, y, 1 - c), device_id_type=MESH).wait_recv()
                out_ref[q] = (out_ref[q].astype(F32) + recv_ref[q].astype(F32)).astype(BF16)

    fn, xa, xs = _ordered(body, 2, after)
    return pl.pallas_call(
        fn, name=name, grid=(n_blocks, nk), in_specs=[a_spec, b_spec] + xs,
        out_specs=pl.BlockSpec((N_DEV // 2, R, N), lambda i, k: (0, 0, 0)),
        out_shape=jax.ShapeDtypeStruct((N_DEV // 2, R, N), BF16),
        scratch_shapes=[pltpu.VMEM(block, F32), pltpu.VMEM(block, BF16), pltpu.VMEM((N_DEV // 2, R, N), BF16),
                        pltpu.SemaphoreType.DMA((N_DEV // 2,)), pltpu.SemaphoreType.DMA((N_DEV // 2,))],
        compiler_params=_params(dimension_semantics=("arbitrary", "arbitrary"), collective_id=SIBLING_PAIR_ID),
    )(a, b, *xa)


Copy = collections.namedtuple("Copy", "mask sb src db dst sem")
Local = collections.namedtuple("Local", "sb src db dst")

HBM_SPEC = pl.BlockSpec(memory_space=pltpu.HBM)
SEM_SPEC = pl.BlockSpec(memory_space=pltpu.SEMAPHORE)
SIBLING, X_PEER, Y_PEER, DIAGONAL = 1, 4, 2, 6
OTHER_CHIPS = (X_PEER, Y_PEER, DIAGONAL)


def _whole(ref, i):
    return ref


def _lead(ref, i):
    return ref.at[i]


def _second(ref, i):
    return ref.at[:, i]


def _place():
    x, y, c = lax.axis_index("x"), lax.axis_index("y"), lax.axis_index("c")
    return (x, y, c), 4 * x + 2 * y + c


def _descriptor(cp, bufs, xyc, me, sender, send_sems, recv_sems):
    x, y, c = xyc
    flip = lambda v, bit: (1 - v) if bit else v
    return pltpu.make_async_remote_copy(
        src_ref=cp.src(bufs[cp.sb], me), dst_ref=cp.dst(bufs[cp.db], sender),
        send_sem=send_sems.at[cp.sem], recv_sem=recv_sems.at[cp.sem],
        device_id=(flip(x, cp.mask & 4), flip(y, cp.mask & 2), flip(c, cp.mask & 1)), device_id_type=MESH)


def _exchange(name, bufs, plan, local=()):
    n = len(bufs)

    def body(*refs):
        ins = refs[:n]
        send_sems, recv_sems, local_sems = refs[2 * n:]
        xyc, me = _place()
        own = [pltpu.make_async_copy(lc.src(ins[lc.sb], me), lc.dst(ins[lc.db], me), local_sems.at[i])
               for i, lc in enumerate(local)]
        sends = [_descriptor(cp, ins, xyc, me, me, send_sems, recv_sems) for cp in plan]
        for cp in own + sends:
            cp.start()
        for cp in plan:
            _descriptor(cp, ins, xyc, me, me ^ cp.mask, send_sems, recv_sems).wait_recv()
        for cp in sends:
            cp.wait_send()
        for cp in own:
            cp.wait()

    return pl.pallas_call(
        body, name=name, in_specs=[HBM_SPEC] * n, out_specs=[HBM_SPEC] * n,
        out_shape=[jax.ShapeDtypeStruct(b.shape, b.dtype) for b in bufs],
        input_output_aliases={i: i for i in range(n)},
        scratch_shapes=[pltpu.SemaphoreType.DMA((len(plan),)), pltpu.SemaphoreType.DMA((len(plan),)),
                        pltpu.SemaphoreType.DMA((max(len(local), 1),))],
    )(*bufs)


def _place_own(me, items):
    def body(me_ref, *refs):
        for src, dst in zip(refs[:len(items)], refs[len(items):]):
            dst[...] = src[...].astype(dst.dtype)

    return pl.pallas_call(
        body, name="place_own",
        grid_spec=pltpu.PrefetchScalarGridSpec(
            num_scalar_prefetch=1, grid=(1,),
            in_specs=[pl.BlockSpec(blk, functools.partial(lambda i, m, idx: idx, idx=idx)) for _, blk, idx, _, _, _, _ in items],
            out_specs=[pl.BlockSpec(oblk, functools.partial(lambda i, m, at: at(m[0]), at=at)) for *_, oblk, at in items]),
        out_shape=[jax.ShapeDtypeStruct(shape, dtype) for _, _, _, shape, dtype, _, _ in items],
        compiler_params=_params(dimension_semantics=("arbitrary",)),
    )(jnp.reshape(me, (1,)).astype(jnp.int32), *[a for a, *_ in items])


def _split_call(name, bufs, *, wait=None, wait_sems=None, start=None, local=(), after=None, token=False):
    n = len(bufs)
    n_wait = 2 if wait else 0
    n_after = 1 if after is not None else 0
    n_start = 2 if start else 0

    def body(*refs):
        ins = refs[:n]
        wsend, wrecv = refs[n:n + n_wait] if wait else (None, None)
        outs = refs[n + n_wait + n_after:]
        ssend, srecv = outs[:n_start] if start else (None, None)
        rest = outs[n_start + n:]
        xyc, me = _place()
        for cp in wait or ():
            d = _descriptor(cp, ins, xyc, me, me ^ cp.mask, wsend, wrecv)
            d.wait_send()
            d.wait_recv()
        own = [pltpu.make_async_copy(lc.src(ins[lc.sb], me), lc.dst(ins[lc.db], me), rest[-1].at[i])
               for i, lc in enumerate(local)]
        for cp in own:
            cp.start()
        for cp in start or ():
            _descriptor(cp, ins, xyc, me, me, ssend, srecv).start()
        for cp in own:
            cp.wait()
        if token:
            rest[0][...] = jnp.zeros_like(rest[0])

    args = [pltpu.with_memory_space_constraint(b, pltpu.HBM) for b in bufs]
    in_specs = [HBM_SPEC] * n
    if wait:
        args += list(wait_sems)
        in_specs += [SEM_SPEC] * 2
    if after is not None:
        args.append(after)
        in_specs.append(pl.BlockSpec(memory_space=pl.ANY))
    out_shape, out_specs = [], []
    if start:
        out_shape += [pltpu.SemaphoreType.DMA((len(start),))] * 2
        out_specs += [SEM_SPEC] * 2
    out_shape += [pltpu.HBM(b.shape, b.dtype) for b in bufs]
    out_specs += [HBM_SPEC] * n
    if token:
        out_shape.append(jax.ShapeDtypeStruct((8, 128), F32))
        out_specs.append(pl.BlockSpec(memory_space=pltpu.VMEM))
    outs = pl.pallas_call(
        body, name=name, in_specs=in_specs, out_specs=out_specs, out_shape=out_shape,
        input_output_aliases={i: n_start + i for i in range(n)},
        scratch_shapes=[pltpu.SemaphoreType.DMA((len(local),))] if local else [],
        compiler_params=pltpu.CompilerParams(has_side_effects=pltpu.SideEffectType.DATAFLOW_SIDE_EFFECTING),
    )(*args)
    sems = tuple(outs[:n_start]) if start else None
    return sems, list(outs[n_start:n_start + n]), (outs[n_start + n] if token else None)


def _adamw(w, g, m, v):
    m = ADAM_B1 * m + (1.0 - ADAM_B1) * g
    v = ADAM_B2 * v + (1.0 - ADAM_B2) * (g * g)
    m_hat = m / (1.0 - ADAM_B1 ** ADAM_STEP)
    v_hat = v / (1.0 - ADAM_B2 ** ADAM_STEP)
    delta = -ADAM_LR * (m_hat / (jnp.sqrt(v_hat) + ADAM_EPS) + ADAM_WD * w)
    return delta, m, v


def _reduce_adam(name, parts, w, m, v, *, tr, layer=None, into=None, after=None, transposed=False):
    L, R, C = w.shape
    S = parts[0].shape[0]
    tr = min(tr, R)
    n_l = L if layer is None else 1
    first = 0 if layer is None else layer

    def body(*refs):
        p_refs = refs[:n_l]
        w_ref, m_ref, v_ref = refs[n_l:n_l + 3]
        g_ref, d_ref, nm_ref, nv_ref = refs[-4:]
        for l in range(n_l):
            g = p_refs[l][0].astype(F32)
            for s in range(1, S):
                g = g + p_refs[l][s].astype(F32)
            g = g.T if transposed else g
            g_ref[l] = g
            d_ref[l], nm_ref[l], nv_ref[l] = _adamw(w_ref[l], g, m_ref[l], v_ref[l])

    blk = pl.BlockSpec((n_l, tr, C), lambda r: (first, r, 0))
    out = jax.ShapeDtypeStruct((L, R, C), F32)
    extra = list(into or []) + ([after] if after is not None else [])
    return pl.pallas_call(
        body, name=name, grid=(R // tr,),
        in_specs=[pl.BlockSpec((S, C, tr), lambda r: (0, 0, r)) if transposed else pl.BlockSpec((S, tr, C), lambda r: (0, r, 0))] * n_l
        + [blk, blk, blk]
        + [pl.BlockSpec(memory_space=pl.ANY)] * len(extra),
        out_specs=[blk] * 4, out_shape=[out] * 4,
        input_output_aliases={n_l + 3 + i: i for i in range(4)} if into else {},
        compiler_params=_params(dimension_semantics=("arbitrary",)),
    )(*parts, w, m, v, *extra)


def _small_reduce(parts):
    D = parts.shape[2]
    rows = [0, 1, 8, 9, 16, 17, 24, 25, 18, 19, 20, 2, 32]

    def body(p_ref, out_ref):
        s = p_ref[0]
        for d in range(1, N_DEV):
            s = s + p_ref[d]
        out_ref[...] = jnp.zeros_like(out_ref)
        for r, src in enumerate(rows):
            out_ref[r:r + 1, :] = s[src:src + 1, :]

    return pl.pallas_call(body, name="small_reduce", out_shape=jax.ShapeDtypeStruct((16, D), F32))(parts)


def _small_adam(g_gain, g_taps, g_scale, gains, taps, scale):
    def body(gg, gt, gs, wg, mg, vg, wt, mt, vt, ws, ms, vs, *outs):
        for k, (g, w, m, v) in enumerate(((gg, wg, mg, vg), (gt, wt, mt, vt), (gs, ws, ms, vs))):
            outs[3 * k][...], outs[3 * k + 1][...], outs[3 * k + 2][...] = _adamw(w[...], g[...], m[...], v[...])

    shapes = [jax.ShapeDtypeStruct(t[0].shape, F32) for t in (gains, taps, scale) for _ in range(3)]
    return pl.pallas_call(body, name="small_adam", out_shape=shapes)(g_gain, g_taps, g_scale, *gains, *taps, *scale)


def kernel(x, norm_gains, pool_w, pool_scale, conv_in_w, conv_w, conv_out_w, ffn_gate_up_w, ffn_down_w, loss_target, m_norm_gains, m_pool_w, m_pool_scale, m_conv_in_w, m_conv_w, m_conv_out_w, m_ffn_gate_up_w, m_ffn_down_w, v_norm_gains, v_pool_w, v_pool_scale, v_conv_in_w, v_conv_w, v_conv_out_w, v_ffn_gate_up_w, v_ffn_down_w):
    T, D = x.shape[1], x.shape[2]
    tm = min(512, T)
    tm_b = min(256, T)
    tk = min(2048, T)
    n_layers = ffn_gate_up_w.shape[0]
    fb = ffn_gate_up_w.shape[2]
    fr = ffn_down_w.shape[1]
    dcol = norm_gains.shape[2]
    cb = conv_in_w.shape[2]
    gw = pool_w.shape[3]
    me = 4 * lax.axis_index("x") + 2 * lax.axis_index("y") + lax.axis_index("c")

    small_w = jnp.concatenate([norm_gains.reshape(8, dcol), jnp.pad(conv_w[0], ((0, 5), (0, 0)))], axis=0)
    every = range(1, N_DEV)
    wgu_t, m_wgu_t, v_wgu_t = (jnp.swapaxes(a, 1, 2) for a in (ffn_gate_up_w, m_ffn_gate_up_w, v_ffn_gate_up_w))
    own_lead = lambda s: lax.dynamic_update_slice(lax.empty((N_DEV,) + s.shape, s.dtype), s[None], (me,) + (0,) * s.ndim)
    lead_item = lambda a, l, dtype: (a, (None,) + a.shape[1:], (l, 0, 0), (N_DEV,) + a.shape[1:], dtype,
                                     (None,) + a.shape[1:], lambda i: (i, 0, 0))
    lands = _place_own(me, [
        (pool_w, (None,) + pool_w.shape[1:], (0, 0, 0, 0), (4, N_DEV, gw // N_DEV, gw), BF16,
         (4, None, gw // N_DEV, gw), lambda i: (0, i, 0, 0)),
        (small_w[None], (None,) + small_w.shape, (0, 0, 0), (N_DEV,) + small_w.shape, F32, (None,) + small_w.shape, lambda i: (i, 0, 0)),
        lead_item(wgu_t, 0, BF16), lead_item(ffn_down_w, 0, BF16), lead_item(conv_in_w, 0, BF16),
        lead_item(conv_out_w, 0, BF16), lead_item(wgu_t, 1, BF16), lead_item(ffn_down_w, 1, BF16)])
    n_first, n_big = 2, len(lands) - 2
    direct = ([Copy(m, 0, _second, 0, _second, m - 1) for m in every]
              + [Copy(m, 1, _lead, 1, _lead, N_DEV - 2 + m) for m in every])
    level1 = [Copy(mask, n_first + n, _lead, n_first + n, _lead, len(direct) + 4 * n + j)
              for n in range(n_big) for j, mask in enumerate((SIBLING,) + OTHER_CHIPS)]
    sems1, bufs1, _ = _split_call("gather_start", lands, start=direct + level1)
    pw_g, small_g = _split_call("gather_small_done", bufs1[:n_first], wait=direct, wait_sems=sems1)[1]
    pw = pw_g.reshape(4, gw, gw)
    small_full = jnp.swapaxes(small_g, 0, 1).reshape(16, D)
    gain = lambda l, s: small_full[4 * l + s][None, :]
    taps = small_full[8:16]

    def forward_on(name, group, after):
        k = len(group)
        landed = [Copy(cp.mask, i, cp.src, i, cp.dst, cp.sem)
                  for i, n in enumerate(group) for cp in level1 if cp.sb == n_first + n]
        onward = [Copy(SIBLING, i, (lambda ref, me, m=m: ref.at[me ^ m]), i, (lambda ref, sender, m=m: ref.at[sender ^ m]), 3 * i + j)
                  for i in range(k) for j, m in enumerate(OTHER_CHIPS)]
        sems2, bufs2, tok = _split_call(name + "_forward", [bufs1[n_first + n] for n in group], wait=landed,
                                        wait_sems=sems1, start=onward, after=after, token=True)
        return (name, onward, sems2, bufs2), tok

    def arrived(state, after=None):
        name, onward, sems2, lands2 = state
        return _split_call(name + "_done", lands2, wait=onward, wait_sems=sems2, after=after)[1]

    h0 = x[0]
    h1 = _pool_fwd(h0, pw, pool_scale, gain(0, 0), gain(0, 1), tm=tm)
    (wgu0,) = arrived(forward_on("gather_gate_up_0", [0], h1)[0])
    wgu0 = wgu0.reshape(N_DEV * fb, D)
    gu0, act0 = _ffn_up(h1, gain(0, 2), wgu0, tm=tm)
    ag_down0, tok = forward_on("gather_down_0", [1], act0)
    ag_conv, tok = forward_on("gather_conv", [2, 3], tok)
    (wd0,) = arrived(ag_down0, tok)
    wd0 = wd0.reshape(N_DEV * fr, D)
    h2, ff0 = _ffn_down(h1, act0, gain(0, 3), wd0, tm=tm)
    win_g, wout_g = arrived(ag_conv, h2)
    wout = wout_g.reshape(D, D)
    h3, proj, y = _conv_fwd(h2, gain(1, 0), gain(1, 1), win_g, taps, wout, tm=tm)
    ag_ffn1, tok = forward_on("gather_ffn1", [4, 5], h3)
    wgu1, wd1 = arrived(ag_ffn1, tok)
    wgu1, wd1 = wgu1.reshape(N_DEV * fb, D), wd1.reshape(N_DEV * fr, D)
    dh4, gu1, act1, ff1, loss_part = _ffn_fwd(h3, gain(1, 2), gain(1, 3), wgu1, wd1, 1, loss_target[0], tm=tm)

    chip = me >> 1

    def scatter_start(name, sums):
        k = len(sums)
        lands = [lax.dynamic_update_slice(lax.empty(s.shape, BF16), lax.dynamic_index_in_dim(s, chip, 0), (chip, 0, 0))
                 for s in sums]
        plan = [Copy(m, n, (lambda ref, i, m=m: ref.at[(i ^ m) >> 1]), k + n, (lambda ref, i: ref.at[i >> 1]), 3 * n + j)
                for n in range(k) for j, m in enumerate(OTHER_CHIPS)]
        sems, bufs, tok = _split_call(name + "_start", sums + lands, start=plan, token=True)
        return (name, plan, sems, bufs), tok

    def scatter_done(state, after):
        name, plan, sems, bufs = state
        return _split_call(name + "_done", bufs, wait=plan, wait_sems=sems, after=after)[1][len(bufs) // 2:]

    seq = lambda i, k: (k, 0)
    gu_pair = pl.BlockSpec((None, tk, 2 * fb), lambda i, k: (i // 2, k, i % 2))
    act_pair = pl.BlockSpec((tk, 2 * fb), lambda i, k: (k, i))
    rows = pl.BlockSpec((tk, D), seq)
    dh3, dgu1, dff1, c1, small_f1 = _ffn_bwd(dh4, h3, ff1, gu1, gain(1, 2), gain(1, 3), wgu1, wd1, 1, tm=tm_b)
    g_wgu1 = _wgrad("wgrad_gate_up_1", dgu1, c1, gu_pair, rows, (2 * fb, D), N_DEV // 2, tk=tk)
    g_wd1 = _wgrad("wgrad_down_1", act1, dff1, act_pair, rows, (2 * fb, D), N_DEV // 4, tk=tk)
    rs_ffn1, tok = scatter_start("scatter_ffn1", [g_wgu1, g_wd1])
    dh2, dproj, z, a1, dy, small_c = _conv_bwd(dh3, h2, y, proj, gain(1, 0), gain(1, 1), win_g, taps, wout, tm=tm, after=tok)
    g_win = _wgrad("wgrad_conv_in", dproj, a1, pl.BlockSpec((tk, 2 * cb), lambda i, k: (k, i)), rows, (2 * cb, D), N_DEV // 2,
                   tk=tk)
    g_wout = _wgrad("wgrad_conv_out", z, dy, rows, rows, (D, D), 1, tk=tk)
    rs_conv, tok = scatter_start("scatter_conv", [g_win, g_wout])
    dh1, dgu0, dff0, c0, small_f0 = _ffn_bwd(dh2, h1, ff0, gu0, gain(0, 2), gain(0, 3), wgu0, wd0, 0, tm=tm_b, after=tok)
    g_wgu0 = _wgrad("wgrad_gate_up_0", dgu0, c0, gu_pair, rows, (2 * fb, D), N_DEV // 2, tk=tk)
    rs_wgu0, tok = scatter_start("scatter_gate_up_0", [g_wgu0])
    g_wd0 = _wgrad("wgrad_down_0", act0, dff0, act_pair, rows, (2 * fb, D), N_DEV // 4, tk=tk, after=tok)
    rs_wd0, tok = scatter_start("scatter_down_0", [g_wd0])
    grad_x, g_pw, small_p = _pool_bwd(dh1, h0, pw, pool_scale, gain(0, 0), gain(0, 1), tm=tm, after=tok)

    loss_rows = jnp.broadcast_to(loss_part[0:1, 0:1], (8, D))
    small_part = jnp.concatenate([small_p, small_f0, small_c, small_f1, loss_rows], axis=0)
    g_pw = g_pw.reshape(4, N_DEV, gw // N_DEV, gw)
    pw_land = lax.dynamic_update_slice(lax.empty(g_pw.shape, BF16), lax.dynamic_slice_in_dim(g_pw, me, 1, 1), (0, me, 0, 0))
    last = ([Copy(m, 0, (lambda ref, i, m=m: ref.at[:, i ^ m]), 2, _second, m - 1) for m in every]
            + [Copy(m, 1, _whole, 3, _lead, N_DEV - 2 + m) for m in every])
    sems_l, bufs_l, tok = _split_call("scatter_small_start", [g_pw, small_part, pw_land, own_lead(small_part)],
                                      start=last, token=True)

    (r_wgu1, r_wd1), (r_win, r_wout) = scatter_done(rs_ffn1, tok), scatter_done(rs_conv, tok)
    o_win = _reduce_adam("adam_conv_in", [r_win], conv_in_w, m_conv_in_w, v_conv_in_w, tr=256, transposed=True)
    o_wout = _reduce_adam("adam_conv_out", [r_wout], conv_out_w, m_conv_out_w, v_conv_out_w, tr=128, after=o_win[0])
    o_wgu = _reduce_adam("adam_gate_up_1", [r_wgu1], wgu_t, m_wgu_t, v_wgu_t, tr=176, layer=1, after=o_wout[0])
    o_wd = _reduce_adam("adam_down_1", [r_wd1], ffn_down_w, m_ffn_down_w, v_ffn_down_w, tr=176, layer=1, after=o_wgu[0])
    r_pw, r_small = _split_call("scatter_small_done", bufs_l, wait=last, wait_sems=sems_l, after=o_wd[0])[1][2:]
    (r_wgu0,), (r_wd0,) = scatter_done(rs_wgu0, r_small), scatter_done(rs_wd0, r_small)
    o_wgu = _reduce_adam("adam_gate_up_0", [r_wgu0], wgu_t, m_wgu_t, v_wgu_t, tr=176, layer=0, into=o_wgu)
    o_wgu = [jnp.swapaxes(o, 1, 2) for o in o_wgu]
    o_wd = _reduce_adam("adam_down_0", [r_wd0], ffn_down_w, m_ffn_down_w, v_ffn_down_w, tr=176, layer=0, into=o_wd)
    o_pw = _reduce_adam("adam_pool_w", [r_pw[g] for g in range(4)], pool_w[0], m_pool_w[0], v_pool_w[0], tr=32)
    g_small = _small_reduce(r_small)
    loss = g_small[12, 0]
    g_cols = lax.dynamic_slice(g_small, (0, me * dcol), (16, dcol))
    o_small = _small_adam(
        g_cols[0:8], g_cols[8:11], g_small[11:12],
        (norm_gains.reshape(8, dcol), m_norm_gains.reshape(8, dcol), v_norm_gains.reshape(8, dcol)),
        (conv_w[0], m_conv_w[0], v_conv_w[0]), (pool_scale, m_pool_scale, v_pool_scale))
    d_gain, nm_gain, nv_gain, d_taps, nm_taps, nv_taps, d_scale, nm_scale, nv_scale = o_small

    gshape = norm_gains.shape
    per = lambda k: (
        (g_cols[0:8].reshape(gshape), d_gain.reshape(gshape), nm_gain.reshape(gshape), nv_gain.reshape(gshape))[k],
        o_pw[k][None], (g_small[11:12], d_scale, nm_scale, nv_scale)[k], o_win[k],
        (g_cols[8:11][None], d_taps[None], nm_taps[None], nv_taps[None])[k], o_wout[k], o_wgu[k], o_wd[k])
    return (loss, grad_x[None], *per(0), *per(1), *per(2), *per(3))
```

```python
import collections
import functools

import jax
import jax.numpy as jnp
from jax import lax
from jax.experimental import pallas as pl
from jax.experimental.pallas import tpu as pltpu

N_DEV = 8
RMS_EPS = 1e-6
POOL_WINDOWS = (2, 4, 8, 16)
POOL_HALO = 16
CONV_HALO = 16
ADAM_LR, ADAM_B1, ADAM_B2, ADAM_EPS, ADAM_WD, ADAM_STEP = 0.001, 0.9, 0.999, 1e-08, 0.01, 10

VMEM_LIMIT = 56 * 2**20
MXU_COLUMNS = 256
BF16 = jnp.bfloat16
F32 = jnp.float32
MESH = pl.DeviceIdType.MESH
SIBLING_PAIR_ID = 0


def _params(**kw):
    return pltpu.CompilerParams(vmem_limit_bytes=VMEM_LIMIT, **kw)


def _resident(shape, index_map):
    return pl.BlockSpec(shape, index_map, pipeline_mode=pl.Buffered(1))


def _ordered(body, n_in, after):
    if after is None:
        return functools.partial(body), [], []
    return (lambda *refs: body(*refs[:n_in], *refs[n_in + 1:])), [after], [pl.BlockSpec(memory_space=pl.ANY)]


def _rms_fwd(x, g):
    r = lax.rsqrt(jnp.mean(x * x, axis=-1, keepdims=True) + RMS_EPS)
    return x * r * g


def _rms_bwd(x, g, dy):
    r = lax.rsqrt(jnp.mean(x * x, axis=-1, keepdims=True) + RMS_EPS)
    xhat = x * r
    dg = jnp.sum(dy * xhat, axis=0, keepdims=True)
    t = dy * g
    dx = r * (t - xhat * jnp.mean(t * xhat, axis=-1, keepdims=True))
    return dx, dg


def _sigmoid(x):
    return 0.5 * jnp.tanh(0.5 * x) + 0.5


def _dot(a, b):
    return jnp.dot(a, b, preferred_element_type=F32)


def _dot_nt(a, b):
    return lax.dot_general(a, b, (((1,), (1,)), ((), ())), preferred_element_type=F32)


def _dot_tn(a, b):
    return lax.dot_general(a, b, (((0,), (0,)), ((), ())), preferred_element_type=F32)


def _join_blocks(blocks_hbm, joined_ref, sems):
    n, _, C = blocks_hbm.shape
    copies = [pltpu.make_async_copy(blocks_hbm.at[k], joined_ref.at[:, k * C:(k + 1) * C], sems.at[k]) for k in range(n)]
    for cp in copies:
        cp.start()
    for cp in copies:
        cp.wait()


def _row_inverse_counts(tile, tm):
    pos = (lax.broadcasted_iota(jnp.int32, (tm, 1), 0) + tile * tm + 1).astype(F32)
    return [1.0 / jnp.minimum(pos, float(w)) for w in POOL_WINDOWS]


def _pool_from_ext(ext, a, invs, gw):
    s = ext
    outs = []
    for g, w in enumerate(POOL_WINDOWS):
        s = s[:, (gw if g else 0):]
        s = s + pltpu.roll(s, w // 2, 0)
        outs.append(s[POOL_HALO:, :gw] * invs[g] - a[:, g * gw:(g + 1) * gw])
    return outs


def _pool_fwd(h, pw, scale, g_pre, g_post, *, tm, after=None):
    T, D = h.shape
    gw = D // len(POOL_WINDOWS)
    hb = tm // POOL_HALO

    def body(h_ref, halo_ref, pw_ref, scale_ref, gpre_ref, gpost_ref, out_ref, pooled_ref, mixed_ref, ext_ref):
        i = pl.program_id(0)
        x = h_ref[...]
        a = _rms_fwd(x, gpre_ref[...])
        ah = _rms_fwd(halo_ref[...], gpre_ref[...])
        ext_ref[0:POOL_HALO, :] = jnp.where(i == 0, 0.0, ah)
        ext_ref[POOL_HALO:, :] = a
        pooled = [p.astype(BF16) for p in _pool_from_ext(ext_ref[...], a, _row_inverse_counts(i, tm), gw)]
        pooled_ref[...] = jnp.concatenate(pooled, axis=1)
        mixed = jnp.concatenate([_dot(p, pw_ref[g]) for g, p in enumerate(pooled)], axis=1)
        mixed_ref[...] = mixed.astype(BF16)
        out_ref[...] = x + _rms_fwd(mixed * scale_ref[...], gpost_ref[...])

    vec = _resident((1, D), lambda i: (0, 0))
    fn, xa, xs = _ordered(body, 6, after)
    return pl.pallas_call(
        fn, name="pool_fwd", grid=(T // tm,),
        in_specs=[pl.BlockSpec((tm, D), lambda i: (i, 0)),
                  pl.BlockSpec((POOL_HALO, D), lambda i: (jnp.maximum(i * hb - 1, 0), 0)),
                  _resident(pw.shape, lambda i: (0, 0, 0)), vec, vec, vec] + xs,
        out_specs=[pl.BlockSpec((tm, D), lambda i: (i, 0))] * 3,
        out_shape=[jax.ShapeDtypeStruct((T, D), F32), jax.ShapeDtypeStruct((T, D), BF16), jax.ShapeDtypeStruct((T, D), BF16)],
        scratch_shapes=[pltpu.VMEM((tm + POOL_HALO, D), F32)],
        compiler_params=_params(dimension_semantics=("arbitrary",)),
    )(h, h, pw, scale, g_pre, g_post, *xa)


def _pool_bwd(dh, h, pooled, mixed_pre, pw, scale, g_pre, g_post, *, tm, after=None):
    T, D = h.shape
    gw = D // len(POOL_WINDOWS)
    nt = T // tm
    n_ext = tm + POOL_HALO

    def body(dh_ref, h_ref, pooled_ref, mixed_ref, pw_ref, scale_ref, gpre_ref, gpost_ref,
             dx_ref, dpw_ref, small_ref, ext2_ref, carry_ref, dpw_acc):
        i = pl.program_id(0)
        tile = nt - 1 - i

        @pl.when(i == 0)
        def _():
            small_ref[...] = jnp.zeros_like(small_ref)
            dpw_acc[...] = jnp.zeros_like(dpw_acc)
            carry_ref[...] = jnp.zeros_like(carry_ref)

        x = h_ref[...]
        dout = dh_ref[...]
        invs = _row_inverse_counts(tile, tm)
        pooled = [pooled_ref[:, g * gw:(g + 1) * gw] for g in range(len(POOL_WINDOWS))]
        mixed_pre = mixed_ref[...].astype(F32)
        scale_v = scale_ref[...]
        dmixed, dg_post = _rms_bwd(mixed_pre * scale_v, gpost_ref[...], dout)
        small_ref[1:2, :] += dg_post
        small_ref[2:3, :] += jnp.sum(dmixed * mixed_pre, axis=0, keepdims=True)
        dpre = (dmixed * scale_v).astype(BF16)
        dpooled = []
        for g in range(len(POOL_WINDOWS)):
            dp = dpre[:, g * gw:(g + 1) * gw]
            dpw_acc[g] += _dot_tn(pooled[g], dp)
            dpooled.append(_dot_nt(dp, pw_ref[g]))
        q = jnp.concatenate([d * invs[g] for g, d in enumerate(dpooled)], axis=1)
        ext2_ref[0:tm, :] = q
        ext2_ref[tm:, :] = carry_ref[...]
        carry_ref[...] = q[0:POOL_HALO, :]
        s = ext2_ref[...]
        da = []
        for g, w in enumerate(POOL_WINDOWS):
            s = s[:, (gw if g else 0):]
            s = s + pltpu.roll(s, n_ext - w // 2, 0)
            da.append(s[0:tm, :gw] - dpooled[g])
        dx, dg_pre = _rms_bwd(x, gpre_ref[...], jnp.concatenate(da, axis=1))
        small_ref[0:1, :] += dg_pre
        dx_ref[...] = dout + dx

        @pl.when(i == nt - 1)
        def _():
            dpw_ref[...] = dpw_acc[...].astype(BF16)

    vec = _resident((1, D), lambda i: (0, 0))
    rev = lambda i: (nt - 1 - i, 0)
    fn, xa, xs = _ordered(body, 8, after)
    tile = pl.BlockSpec((tm, D), rev)
    return pl.pallas_call(
        fn, name="pool_bwd", grid=(nt,),
        in_specs=[tile, tile, tile, tile, _resident(pw.shape, lambda i: (0, 0, 0)), vec, vec, vec] + xs,
        out_specs=[tile, pl.BlockSpec(pw.shape, lambda i: (0, 0, 0)), pl.BlockSpec((8, D), lambda i: (0, 0))],
        out_shape=[jax.ShapeDtypeStruct((T, D), F32), jax.ShapeDtypeStruct(pw.shape, BF16),
                   jax.ShapeDtypeStruct((8, D), F32)],
        scratch_shapes=[pltpu.VMEM((n_ext, D), F32), pltpu.VMEM((POOL_HALO, D), F32), pltpu.VMEM(pw.shape, F32)],
        compiler_params=_params(dimension_semantics=("arbitrary",)),
    )(dh, h, pooled, mixed_pre, pw, scale, g_pre, g_post, *xa)


def _ffn_fwd(h, g_pre, g_post, wgu, wd, layer, target, *, tm, after=None):
    T, D = h.shape
    F = wd.shape[0]
    last = target is not None

    def body(*refs):
        if last:
            h_ref, gpre_ref, gpost_ref, wgu_ref, wd_ref, tgt_ref, out_ref, gu_ref, act_ref, ff_ref, loss_ref = refs
        else:
            h_ref, gpre_ref, gpost_ref, wgu_ref, wd_ref, out_ref, gu_ref, act_ref, ff_ref = refs
        x = h_ref[...]
        cb = _rms_fwd(x, gpre_ref[...]).astype(BF16)
        g = _dot_nt(cb, wgu_ref[0:F, :])
        u = _dot_nt(cb, wgu_ref[F:2 * F, :])
        gu_ref[0] = g.astype(BF16)
        gu_ref[1] = u.astype(BF16)
        act = (g * _sigmoid(g) * u).astype(BF16)
        act_ref[...] = act
        acc = _dot(act, wd_ref[...])
        ff_ref[...] = acc.astype(BF16)
        hout = x + _rms_fwd(acc, gpost_ref[...])
        if last:
            diff = hout - tgt_ref[...]
            out_ref[...] = diff * (1.0 / D)

            @pl.when(pl.program_id(0) == 0)
            def _():
                loss_ref[...] = jnp.zeros_like(loss_ref)

            loss_ref[...] += jnp.sum(diff * diff) * (0.5 / D)
        else:
            out_ref[...] = hout

    vec = _resident((1, D), lambda i: (0, 0))
    tile = pl.BlockSpec((tm, D), lambda i: (i, 0))
    in_specs = [tile, vec, vec, _resident(wgu.shape, lambda i: (0, 0)), _resident(wd.shape, lambda i: (0, 0))]
    out_specs = [tile, pl.BlockSpec((2, tm, F), lambda i: (0, i, 0)), pl.BlockSpec((tm, F), lambda i: (i, 0)), tile]
    out_shape = [jax.ShapeDtypeStruct((T, D), F32), jax.ShapeDtypeStruct((2, T, F), BF16),
                 jax.ShapeDtypeStruct((T, F), BF16), jax.ShapeDtypeStruct((T, D), BF16)]
    args = [h, g_pre, g_post, wgu, wd]
    if last:
        in_specs.append(tile)
        args.append(target)
        out_specs.append(pl.BlockSpec((8, 128), lambda i: (0, 0)))
        out_shape.append(jax.ShapeDtypeStruct((8, 128), F32))
    fn, xa, xs = _ordered(body, len(args), after)
    return pl.pallas_call(
        fn, name=f"ffn_fwd_{layer}", grid=(T // tm,), in_specs=in_specs + xs, out_specs=out_specs,
        out_shape=out_shape, compiler_params=_params(dimension_semantics=("arbitrary",)),
    )(*args, *xa)


def _ffn_up(h, g_pre, wgu, *, tm):
    T, D = h.shape
    F = wgu.shape[0] // 2

    def body(h_ref, gpre_ref, wgu_ref, gu_ref, act_ref):
        cb = _rms_fwd(h_ref[...], gpre_ref[...]).astype(BF16)
        g = _dot_nt(cb, wgu_ref[0:F, :])
        u = _dot_nt(cb, wgu_ref[F:2 * F, :])
        gu_ref[0] = g.astype(BF16)
        gu_ref[1] = u.astype(BF16)
        act_ref[...] = (g * _sigmoid(g) * u).astype(BF16)

    return pl.pallas_call(
        body, name="ffn_up_0", grid=(T // tm,),
        in_specs=[pl.BlockSpec((tm, D), lambda i: (i, 0)), _resident((1, D), lambda i: (0, 0)),
                  _resident(wgu.shape, lambda i: (0, 0))],
        out_specs=[pl.BlockSpec((2, tm, F), lambda i: (0, i, 0)), pl.BlockSpec((tm, F), lambda i: (i, 0))],
        out_shape=[jax.ShapeDtypeStruct((2, T, F), BF16), jax.ShapeDtypeStruct((T, F), BF16)],
        compiler_params=_params(dimension_semantics=("arbitrary",)),
    )(h, g_pre, wgu)


def _ffn_down(h, act, g_post, wd, *, tm):
    T, D = h.shape
    F = act.shape[1]

    def body(h_ref, act_ref, gpost_ref, wd_ref, out_ref, ff_ref):
        acc = _dot(act_ref[...], wd_ref[...])
        ff_ref[...] = acc.astype(BF16)
        out_ref[...] = h_ref[...] + _rms_fwd(acc, gpost_ref[...])

    tile = pl.BlockSpec((tm, D), lambda i: (i, 0))
    return pl.pallas_call(
        body, name="ffn_down_0", grid=(T // tm,),
        in_specs=[tile, pl.BlockSpec((tm, F), lambda i: (i, 0)), _resident((1, D), lambda i: (0, 0)),
                  _resident(wd.shape, lambda i: (0, 0))],
        out_specs=[tile, tile],
        out_shape=[jax.ShapeDtypeStruct((T, D), F32), jax.ShapeDtypeStruct((T, D), BF16)],
        compiler_params=_params(dimension_semantics=("arbitrary",)),
    )(h, act, g_post, wd)


def _ffn_bwd(dh, h, ff, gu, g_pre, g_post, wgu, wd, layer, *, tm, after=None):
    T, D = h.shape
    F = wd.shape[0]
    n_chunks = F // MXU_COLUMNS

    def body(dh_ref, h_ref, ff_ref, gu_ref, gpre_ref, gpost_ref, wgu_ref, wd_ref,
             dx_ref, dgu_ref, dff_ref, c_ref, small_ref):
        @pl.when(pl.program_id(0) == 0)
        def _():
            small_ref[...] = jnp.zeros_like(small_ref)

        dout = dh_ref[...]
        dff, dg_post = _rms_bwd(ff_ref[...].astype(F32), gpost_ref[...], dout)
        small_ref[1:2, :] += dg_post
        dffb = dff.astype(BF16)
        dff_ref[...] = dffb
        dc = jnp.zeros((tm, D), F32)
        for j in range(n_chunks + 1):
            lo, hi = j * MXU_COLUMNS, (j + 1) * MXU_COLUMNS
            if j < n_chunks:
                dact = _dot_nt(dffb, wd_ref[lo:hi, :])
            if j > 0:
                lo0 = lo - MXU_COLUMNS
                dc = dc + _dot(dgu_ref[0, :, lo0:lo], wgu_ref[lo0:lo, :]) + _dot(dgu_ref[1, :, lo0:lo], wgu_ref[F + lo0:F + lo, :])
            if j < n_chunks:
                g = gu_ref[0, :, lo:hi].astype(F32)
                u = gu_ref[1, :, lo:hi].astype(F32)
                s = _sigmoid(g)
                dgu_ref[0, :, lo:hi] = (dact * u * (s * (1.0 + g * (1.0 - s)))).astype(BF16)
                dgu_ref[1, :, lo:hi] = (dact * (g * s)).astype(BF16)
        x = h_ref[...]
        c_ref[...] = _rms_fwd(x, gpre_ref[...]).astype(BF16)
        dx, dg_pre = _rms_bwd(x, gpre_ref[...], dc)
        small_ref[0:1, :] += dg_pre
        dx_ref[...] = dout + dx

    vec = _resident((1, D), lambda i: (0, 0))
    tile = pl.BlockSpec((tm, D), lambda i: (i, 0))
    blk = pl.BlockSpec((2, tm, F), lambda i: (0, i, 0))
    fn, xa, xs = _ordered(body, 8, after)
    return pl.pallas_call(
        fn, name=f"ffn_bwd_{layer}", grid=(T // tm,),
        in_specs=[tile, tile, tile, blk, vec, vec,
                  _resident(wgu.shape, lambda i: (0, 0)), _resident(wd.shape, lambda i: (0, 0))] + xs,
        out_specs=[tile, blk, tile, tile, pl.BlockSpec((8, D), lambda i: (0, 0))],
        out_shape=[jax.ShapeDtypeStruct((T, D), F32), jax.ShapeDtypeStruct((2, T, F), BF16),
                   jax.ShapeDtypeStruct((T, D), BF16), jax.ShapeDtypeStruct((T, D), BF16),
                   jax.ShapeDtypeStruct((8, D), F32)],
        compiler_params=_params(dimension_semantics=("arbitrary",)),
    )(dh, h, ff, gu, g_pre, g_post, wgu, wd, *xa)


def _conv_fwd(h, g_pre, g_post, win, taps, wout, *, tm, after=None):
    T, D = h.shape
    nblk, cb = win.shape[0], win.shape[2]

    def body(h_ref, gpre_ref, gpost_ref, win_hbm, taps_ref, wout_ref,
             out_ref, proj_ref, y_ref, conv_ref, z_ref, proj_scr, ext_ref, carry_ref, win_ref, win_sems):
        i = pl.program_id(0)

        @pl.when(i == 0)
        def _():
            carry_ref[...] = jnp.zeros_like(carry_ref)
            _join_blocks(win_hbm, win_ref, win_sems)

        x = h_ref[...]
        a = _rms_fwd(x, gpre_ref[...]).astype(BF16)
        proj_scr[...] = _dot(a, win_ref[...])
        proj_ref[...] = proj_scr[...].astype(BF16)
        u = proj_scr[:, D:2 * D] * proj_scr[:, 2 * D:3 * D]
        ext_ref[0:CONV_HALO, :] = carry_ref[...]
        ext_ref[CONV_HALO:, :] = u
        carry_ref[...] = u[tm - CONV_HALO:, :]
        e = ext_ref[...]
        conv = (taps_ref[2:3, :] * u + taps_ref[1:2, :] * pltpu.roll(e, 1, 0)[CONV_HALO:, :]
                + taps_ref[0:1, :] * pltpu.roll(e, 2, 0)[CONV_HALO:, :])
        conv_ref[...] = conv.astype(BF16)
        z = (proj_scr[:, 0:D] * conv).astype(BF16)
        z_ref[...] = z
        y = _dot(z, wout_ref[...])
        y_ref[...] = y.astype(BF16)
        out_ref[...] = x + _rms_fwd(y, gpost_ref[...])

    vec = _resident((1, D), lambda i: (0, 0))
    tile = pl.BlockSpec((tm, D), lambda i: (i, 0))
    fn, xa, xs = _ordered(body, 6, after)
    return pl.pallas_call(
        fn, name="conv_fwd", grid=(T // tm,),
        in_specs=[tile, vec, vec, pl.BlockSpec(memory_space=pl.ANY),
                  _resident(taps.shape, lambda i: (0, 0)), _resident(wout.shape, lambda i: (0, 0))] + xs,
        out_specs=[tile, pl.BlockSpec((tm, 3 * D), lambda i: (i, 0)), tile, tile, tile],
        out_shape=[jax.ShapeDtypeStruct((T, D), F32), jax.ShapeDtypeStruct((T, 3 * D), BF16),
                   jax.ShapeDtypeStruct((T, D), BF16), jax.ShapeDtypeStruct((T, D), BF16),
                   jax.ShapeDtypeStruct((T, D), BF16)],
        scratch_shapes=[pltpu.VMEM((tm, 3 * D), F32), pltpu.VMEM((tm + CONV_HALO, D), F32),
                        pltpu.VMEM((CONV_HALO, D), F32), pltpu.VMEM((D, nblk * cb), BF16),
                        pltpu.SemaphoreType.DMA((nblk,))],
        compiler_params=_params(dimension_semantics=("arbitrary",)),
    )(h, g_pre, g_post, win, taps, wout, *xa)


def _conv_bwd(dh, h, y, proj, conv, g_pre, g_post, win, taps, wout, *, tm, after=None):
    T, D = h.shape
    nblk, cb = win.shape[0], win.shape[2]
    nt = T // tm
    hb = tm // CONV_HALO
    n_ext = tm + CONV_HALO

    def body(dh_ref, h_ref, y_ref, proj_ref, halo_ref, conv_ref, gpre_ref, gpost_ref, win_hbm, taps_ref, wout_ref,
             dx_ref, dproj_ref, a_ref, dy_ref, small_ref, ext_ref, ext2_ref, carry_ref, win_ref, win_sems):
        i = pl.program_id(0)
        tile = nt - 1 - i

        @pl.when(i == 0)
        def _():
            small_ref[...] = jnp.zeros_like(small_ref)
            carry_ref[...] = jnp.zeros_like(carry_ref)
            _join_blocks(win_hbm, win_ref, win_sems)

        dout = dh_ref[...]
        dy, dg_post = _rms_bwd(y_ref[...].astype(F32), gpost_ref[...], dout)
        small_ref[1:2, :] += dg_post
        dyb = dy.astype(BF16)
        dy_ref[...] = dyb
        dz = _dot_nt(dyb, wout_ref[...])
        bgate = proj_ref[:, 0:D].astype(F32)
        cgate = proj_ref[:, D:2 * D].astype(F32)
        v = proj_ref[:, 2 * D:3 * D].astype(F32)
        u = cgate * v
        uh = halo_ref[:, D:2 * D].astype(F32) * halo_ref[:, 2 * D:3 * D].astype(F32)
        ext_ref[0:CONV_HALO, :] = jnp.where(tile == 0, 0.0, uh)
        ext_ref[CONV_HALO:, :] = u
        e = ext_ref[...]
        u1 = pltpu.roll(e, 1, 0)[CONV_HALO:, :]
        u2 = pltpu.roll(e, 2, 0)[CONV_HALO:, :]
        t0, t1, t2 = taps_ref[0:1, :], taps_ref[1:2, :], taps_ref[2:3, :]
        dconv = dz * bgate
        small_ref[2:3, :] += jnp.sum(dconv * u2, axis=0, keepdims=True)
        small_ref[3:4, :] += jnp.sum(dconv * u1, axis=0, keepdims=True)
        small_ref[4:5, :] += jnp.sum(dconv * u, axis=0, keepdims=True)
        ext2_ref[0:tm, :] = dconv
        ext2_ref[tm:, :] = carry_ref[...]
        carry_ref[...] = dconv[0:CONV_HALO, :]
        e2 = ext2_ref[...]
        du = (t2 * dconv + t1 * pltpu.roll(e2, n_ext - 1, 0)[0:tm, :]
              + t0 * pltpu.roll(e2, n_ext - 2, 0)[0:tm, :])
        dproj_ref[:, 0:D] = (dz * conv_ref[...].astype(F32)).astype(BF16)
        dproj_ref[:, D:2 * D] = (du * v).astype(BF16)
        dproj_ref[:, 2 * D:3 * D] = (du * cgate).astype(BF16)
        da = _dot_nt(dproj_ref[...], win_ref[...])
        x = h_ref[...]
        a_ref[...] = _rms_fwd(x, gpre_ref[...]).astype(BF16)
        dx, dg_pre = _rms_bwd(x, gpre_ref[...], da)
        small_ref[0:1, :] += dg_pre
        dx_ref[...] = dout + dx

    vec = _resident((1, D), lambda i: (0, 0))
    rev = lambda i: (nt - 1 - i, 0)
    tile = pl.BlockSpec((tm, D), rev)
    wide = pl.BlockSpec((tm, 3 * D), rev)
    fn, xa, xs = _ordered(body, 11, after)
    return pl.pallas_call(
        fn, name="conv_bwd", grid=(nt,),
        in_specs=[tile, tile, tile, wide,
                  pl.BlockSpec((CONV_HALO, 3 * D), lambda i: (jnp.maximum((nt - 1 - i) * hb - 1, 0), 0)),
                  tile, vec, vec, pl.BlockSpec(memory_space=pl.ANY),
                  _resident(taps.shape, lambda i: (0, 0)), _resident(wout.shape, lambda i: (0, 0))] + xs,
        out_specs=[tile, wide, tile, tile, pl.BlockSpec((8, D), lambda i: (0, 0))],
        out_shape=[jax.ShapeDtypeStruct((T, D), F32), jax.ShapeDtypeStruct((T, 3 * D), BF16),
                   jax.ShapeDtypeStruct((T, D), BF16), jax.ShapeDtypeStruct((T, D), BF16),
                   jax.ShapeDtypeStruct((8, D), F32)],
        scratch_shapes=[pltpu.VMEM((n_ext, D), F32), pltpu.VMEM((n_ext, D), F32),
                        pltpu.VMEM((CONV_HALO, D), F32), pltpu.VMEM((D, nblk * cb), BF16),
                        pltpu.SemaphoreType.DMA((nblk,))],
        compiler_params=_params(dimension_semantics=("arbitrary",)),
    )(dh, h, y, proj, proj, conv, g_pre, g_post, win, taps, wout, *xa)


def _wgrad(name, a, b, a_spec, b_spec, block, n_blocks, *, tk, after=None):
    T = a.shape[-2]
    nk = T // tk
    M, N = block
    m = N_DEV // n_blocks
    R = M // m

    def body(a_ref, b_ref, out_ref, acc_ref, stage_ref, recv_ref, send_sems, recv_sems):
        i, k = pl.program_id(0), pl.program_id(1)
        x, y, c = lax.axis_index("x"), lax.axis_index("y"), lax.axis_index("c")

        def sent(blk, p):
            owner = blk * m + p
            q = owner // 2
            return (owner % 2) != c, pltpu.make_async_remote_copy(
                src_ref=stage_ref.at[p * R:(p + 1) * R], dst_ref=recv_ref.at[q], send_sem=send_sems.at[q],
                recv_sem=recv_sems.at[q], device_id=(x, y, 1 - c), device_id_type=MESH)

        @pl.when(jnp.logical_and(i == 0, k == 0))
        def _():
            barrier = pltpu.get_barrier_semaphore()
            pl.semaphore_signal(barrier, inc=1, device_id=(x, y, 1 - c), device_id_type=MESH)
            pl.semaphore_wait(barrier, 1)

        @pl.when(k == 0)
        def _():
            acc_ref[...] = jnp.zeros_like(acc_ref)

        acc_ref[...] += _dot_tn(a_ref[...], b_ref[...])

        @pl.when(k == nk - 1)
        def _():
            for p in range(m):
                away, copy = sent(jnp.maximum(i - 1, 0), p)

                @pl.when(jnp.logical_and(i > 0, away))
                def _():
                    copy.wait_send()

            acc = acc_ref[...]
            stage_ref[...] = acc.astype(BF16)
            for p in range(m):
                away, copy = sent(i, p)

                @pl.when(away)
                def _():
                    copy.start()

                @pl.when(jnp.logical_not(away))
                def _():
                    out_ref[(i * m + p) // 2] = stage_ref[p * R:(p + 1) * R, :]

        @pl.when(jnp.logical_and(i == n_blocks - 1, k == nk - 1))
        def _():
            for p in range(m):
                away, copy = sent(i, p)

                @pl.when(away)
                def _():
                    copy.wait_send()

            for q in range(N_DEV // 2):
                pltpu.make_async_remote_copy(
                    src_ref=stage_ref.at[0:R], dst_ref=recv_ref.at[q], send_sem=send_sems.at[q],
                    recv_sem=recv_sems.at[q], device_id=(x, y, 1 - c), device_id_type=MESH).wait_recv()
                out_ref[q] = (out_ref[q].astype(F32) + recv_ref[q].astype(F32)).astype(BF16)

    fn, xa, xs = _ordered(body, 2, after)
    return pl.pallas_call(
        fn, name=name, grid=(n_blocks, nk), in_specs=[a_spec, b_spec] + xs,
        out_specs=pl.BlockSpec((N_DEV // 2, R, N), lambda i, k: (0, 0, 0)),
        out_shape=jax.ShapeDtypeStruct((N_DEV // 2, R, N), BF16),
        scratch_shapes=[pltpu.VMEM(block, F32), pltpu.VMEM(block, BF16), pltpu.VMEM((N_DEV // 2, R, N), BF16),
                        pltpu.SemaphoreType.DMA((N_DEV // 2,)), pltpu.SemaphoreType.DMA((N_DEV // 2,))],
        compiler_params=_params(dimension_semantics=("arbitrary", "arbitrary"), collective_id=SIBLING_PAIR_ID),
    )(a, b, *xa)


Copy = collections.namedtuple("Copy", "mask sb src db dst sem")
Local = collections.namedtuple("Local", "sb src db dst")

HBM_SPEC = pl.BlockSpec(memory_space=pltpu.HBM)
SEM_SPEC = pl.BlockSpec(memory_space=pltpu.SEMAPHORE)
SIBLING, X_PEER, Y_PEER, DIAGONAL = 1, 4, 2, 6
OTHER_CHIPS = (X_PEER, Y_PEER, DIAGONAL)


def _whole(ref, i):
    return ref


def _lead(ref, i):
    return ref.at[i]


def _second(ref, i):
    return ref.at[:, i]


def _place():
    x, y, c = lax.axis_index("x"), lax.axis_index("y"), lax.axis_index("c")
    return (x, y, c), 4 * x + 2 * y + c


def _descriptor(cp, bufs, xyc, me, sender, send_sems, recv_sems):
    x, y, c = xyc
    flip = lambda v, bit: (1 - v) if bit else v
    return pltpu.make_async_remote_copy(
        src_ref=cp.src(bufs[cp.sb], me), dst_ref=cp.dst(bufs[cp.db], sender),
        send_sem=send_sems.at[cp.sem], recv_sem=recv_sems.at[cp.sem],
        device_id=(flip(x, cp.mask & 4), flip(y, cp.mask & 2), flip(c, cp.mask & 1)), device_id_type=MESH)


def _exchange(name, bufs, plan, local=()):
    n = len(bufs)

    def body(*refs):
        ins = refs[:n]
        send_sems, recv_sems, local_sems = refs[2 * n:]
        xyc, me = _place()
        own = [pltpu.make_async_copy(lc.src(ins[lc.sb], me), lc.dst(ins[lc.db], me), local_sems.at[i])
               for i, lc in enumerate(local)]
        sends = [_descriptor(cp, ins, xyc, me, me, send_sems, recv_sems) for cp in plan]
        for cp in own + sends:
            cp.start()
        for cp in plan:
            _descriptor(cp, ins, xyc, me, me ^ cp.mask, send_sems, recv_sems).wait_recv()
        for cp in sends:
            cp.wait_send()
        for cp in own:
            cp.wait()

    return pl.pallas_call(
        body, name=name, in_specs=[HBM_SPEC] * n, out_specs=[HBM_SPEC] * n,
        out_shape=[jax.ShapeDtypeStruct(b.shape, b.dtype) for b in bufs],
        input_output_aliases={i: i for i in range(n)},
        scratch_shapes=[pltpu.SemaphoreType.DMA((len(plan),)), pltpu.SemaphoreType.DMA((len(plan),)),
                        pltpu.SemaphoreType.DMA((max(len(local), 1),))],
    )(*bufs)


def _place_own(me, items):
    def body(me_ref, *refs):
        for src, dst in zip(refs[:len(items)], refs[len(items):]):
            dst[...] = src[...].astype(dst.dtype)

    return pl.pallas_call(
        body, name="place_own",
        grid_spec=pltpu.PrefetchScalarGridSpec(
            num_scalar_prefetch=1, grid=(1,),
            in_specs=[pl.BlockSpec(blk, functools.partial(lambda i, m, idx: idx, idx=idx)) for _, blk, idx, _, _, _, _ in items],
            out_specs=[pl.BlockSpec(oblk, functools.partial(lambda i, m, at: at(m[0]), at=at)) for *_, oblk, at in items]),
        out_shape=[jax.ShapeDtypeStruct(shape, dtype) for _, _, _, shape, dtype, _, _ in items],
        compiler_params=_params(dimension_semantics=("arbitrary",)),
    )(jnp.reshape(me, (1,)).astype(jnp.int32), *[a for a, *_ in items])


def _split_call(name, bufs, *, wait=None, wait_sems=None, start=None, local=(), after=None, token=False):
    n = len(bufs)
    n_wait = 2 if wait else 0
    n_after = 1 if after is not None else 0
    n_start = 2 if start else 0

    def body(*refs):
        ins = refs[:n]
        wsend, wrecv = refs[n:n + n_wait] if wait else (None, None)
        outs = refs[n + n_wait + n_after:]
        ssend, srecv = outs[:n_start] if start else (None, None)
        rest = outs[n_start + n:]
        xyc, me = _place()
        for cp in wait or ():
            d = _descriptor(cp, ins, xyc, me, me ^ cp.mask, wsend, wrecv)
            d.wait_send()
            d.wait_recv()
        own = [pltpu.make_async_copy(lc.src(ins[lc.sb], me), lc.dst(ins[lc.db], me), rest[-1].at[i])
               for i, lc in enumerate(local)]
        for cp in own:
            cp.start()
        for cp in start or ():
            _descriptor(cp, ins, xyc, me, me, ssend, srecv).start()
        for cp in own:
            cp.wait()
        if token:
            rest[0][...] = jnp.zeros_like(rest[0])

    args = [pltpu.with_memory_space_constraint(b, pltpu.HBM) for b in bufs]
    in_specs = [HBM_SPEC] * n
    if wait:
        args += list(wait_sems)
        in_specs += [SEM_SPEC] * 2
    if after is not None:
        args.append(after)
        in_specs.append(pl.BlockSpec(memory_space=pl.ANY))
    out_shape, out_specs = [], []
    if start:
        out_shape += [pltpu.SemaphoreType.DMA((len(start),))] * 2
        out_specs += [SEM_SPEC] * 2
    out_shape += [pltpu.HBM(b.shape, b.dtype) for b in bufs]
    out_specs += [HBM_SPEC] * n
    if token:
        out_shape.append(jax.ShapeDtypeStruct((8, 128), F32))
        out_specs.append(pl.BlockSpec(memory_space=pltpu.VMEM))
    outs = pl.pallas_call(
        body, name=name, in_specs=in_specs, out_specs=out_specs, out_shape=out_shape,
        input_output_aliases={i: n_start + i for i in range(n)},
        scratch_shapes=[pltpu.SemaphoreType.DMA((len(local),))] if local else [],
        compiler_params=pltpu.CompilerParams(has_side_effects=pltpu.SideEffectType.DATAFLOW_SIDE_EFFECTING),
    )(*args)
    sems = tuple(outs[:n_start]) if start else None
    return sems, list(outs[n_start:n_start + n]), (outs[n_start + n] if token else None)


def _adamw(w, g, m, v):
    m = ADAM_B1 * m + (1.0 - ADAM_B1) * g
    v = ADAM_B2 * v + (1.0 - ADAM_B2) * (g * g)
    m_hat = m / (1.0 - ADAM_B1 ** ADAM_STEP)
    v_hat = v / (1.0 - ADAM_B2 ** ADAM_STEP)
    delta = -ADAM_LR * (m_hat / (jnp.sqrt(v_hat) + ADAM_EPS) + ADAM_WD * w)
    return delta, m, v


def _reduce_adam(name, parts, w, m, v, *, tr, layer=None, into=None, after=None, transposed=False):
    L, R, C = w.shape
    S = parts[0].shape[0]
    tr = min(tr, R)
    n_l = L if layer is None else 1
    first = 0 if layer is None else layer

    def body(*refs):
        p_refs = refs[:n_l]
        w_ref, m_ref, v_ref = refs[n_l:n_l + 3]
        g_ref, d_ref, nm_ref, nv_ref = refs[-4:]
        for l in range(n_l):
            g = p_refs[l][0].astype(F32)
            for s in range(1, S):
                g = g + p_refs[l][s].astype(F32)
            g = g.T if transposed else g
            g_ref[l] = g
            d_ref[l], nm_ref[l], nv_ref[l] = _adamw(w_ref[l], g, m_ref[l], v_ref[l])

    blk = pl.BlockSpec((n_l, tr, C), lambda r: (first, r, 0))
    out = jax.ShapeDtypeStruct((L, R, C), F32)
    extra = list(into or []) + ([after] if after is not None else [])
    return pl.pallas_call(
        body, name=name, grid=(R // tr,),
        in_specs=[pl.BlockSpec((S, C, tr), lambda r: (0, 0, r)) if transposed else pl.BlockSpec((S, tr, C), lambda r: (0, r, 0))] * n_l
        + [blk, blk, blk]
        + [pl.BlockSpec(memory_space=pl.ANY)] * len(extra),
        out_specs=[blk] * 4, out_shape=[out] * 4,
        input_output_aliases={n_l + 3 + i: i for i in range(4)} if into else {},
        compiler_params=_params(dimension_semantics=("arbitrary",)),
    )(*parts, w, m, v, *extra)


def _small_reduce(parts):
    D = parts.shape[2]
    rows = [0, 1, 8, 9, 16, 17, 24, 25, 18, 19, 20, 2, 32]

    def body(p_ref, out_ref):
        s = p_ref[0]
        for d in range(1, N_DEV):
            s = s + p_ref[d]
        out_ref[...] = jnp.zeros_like(out_ref)
        for r, src in enumerate(rows):
            out_ref[r:r + 1, :] = s[src:src + 1, :]

    return pl.pallas_call(body, name="small_reduce", out_shape=jax.ShapeDtypeStruct((16, D), F32))(parts)


def _small_adam(g_gain, g_taps, g_scale, gains, taps, scale):
    def body(gg, gt, gs, wg, mg, vg, wt, mt, vt, ws, ms, vs, *outs):
        for k, (g, w, m, v) in enumerate(((gg, wg, mg, vg), (gt, wt, mt, vt), (gs, ws, ms, vs))):
            outs[3 * k][...], outs[3 * k + 1][...], outs[3 * k + 2][...] = _adamw(w[...], g[...], m[...], v[...])

    shapes = [jax.ShapeDtypeStruct(t[0].shape, F32) for t in (gains, taps, scale) for _ in range(3)]
    return pl.pallas_call(body, name="small_adam", out_shape=shapes)(g_gain, g_taps, g_scale, *gains, *taps, *scale)


def kernel(x, norm_gains, pool_w, pool_scale, conv_in_w, conv_w, conv_out_w, ffn_gate_up_w, ffn_down_w, loss_target, m_norm_gains, m_pool_w, m_pool_scale, m_conv_in_w, m_conv_w, m_conv_out_w, m_ffn_gate_up_w, m_ffn_down_w, v_norm_gains, v_pool_w, v_pool_scale, v_conv_in_w, v_conv_w, v_conv_out_w, v_ffn_gate_up_w, v_ffn_down_w):
    T, D = x.shape[1], x.shape[2]
    tm = min(512, T)
    tm_b = min(256, T)
    tk = min(2048, T)
    n_layers = ffn_gate_up_w.shape[0]
    fb = ffn_gate_up_w.shape[2]
    fr = ffn_down_w.shape[1]
    dcol = norm_gains.shape[2]
    cb = conv_in_w.shape[2]
    gw = pool_w.shape[3]
    me = 4 * lax.axis_index("x") + 2 * lax.axis_index("y") + lax.axis_index("c")

    small_w = jnp.concatenate([norm_gains.reshape(8, dcol), jnp.pad(conv_w[0], ((0, 5), (0, 0)))], axis=0)
    every = range(1, N_DEV)
    wgu_t, m_wgu_t, v_wgu_t = (jnp.swapaxes(a, 1, 2) for a in (ffn_gate_up_w, m_ffn_gate_up_w, v_ffn_gate_up_w))
    own_lead = lambda s: lax.dynamic_update_slice(lax.empty((N_DEV,) + s.shape, s.dtype), s[None], (me,) + (0,) * s.ndim)
    lead_item = lambda a, l, dtype: (a, (None,) + a.shape[1:], (l, 0, 0), (N_DEV,) + a.shape[1:], dtype,
                                     (None,) + a.shape[1:], lambda i: (i, 0, 0))
    lands = _place_own(me, [
        (pool_w, (None,) + pool_w.shape[1:], (0, 0, 0, 0), (4, N_DEV, gw // N_DEV, gw), BF16,
         (4, None, gw // N_DEV, gw), lambda i: (0, i, 0, 0)),
        (small_w[None], (None,) + small_w.shape, (0, 0, 0), (N_DEV,) + small_w.shape, F32, (None,) + small_w.shape, lambda i: (i, 0, 0)),
        lead_item(wgu_t, 0, BF16), lead_item(ffn_down_w, 0, BF16), lead_item(conv_in_w, 0, BF16),
        lead_item(conv_out_w, 0, BF16), lead_item(wgu_t, 1, BF16), lead_item(ffn_down_w, 1, BF16)])
    n_first, n_big = 2, len(lands) - 2
    direct = ([Copy(m, 0, _second, 0, _second, m - 1) for m in every]
              + [Copy(m, 1, _lead, 1, _lead, N_DEV - 2 + m) for m in every])
    level1 = [Copy(mask, n_first + n, _lead, n_first + n, _lead, len(direct) + 4 * n + j)
              for n in range(n_big) for j, mask in enumerate((SIBLING,) + OTHER_CHIPS)]
    sems1, bufs1, _ = _split_call("gather_start", lands, start=direct + level1)
    pw_g, small_g = _split_call("gather_small_done", bufs1[:n_first], wait=direct, wait_sems=sems1)[1]
    pw = pw_g.reshape(4, gw, gw)
    small_full = jnp.swapaxes(small_g, 0, 1).reshape(16, D)
    gain = lambda l, s: small_full[4 * l + s][None, :]
    taps = small_full[8:16]

    def forward_on(name, group, after):
        k = len(group)
        landed = [Copy(cp.mask, i, cp.src, i, cp.dst, cp.sem)
                  for i, n in enumerate(group) for cp in level1 if cp.sb == n_first + n]
        onward = [Copy(SIBLING, i, (lambda ref, me, m=m: ref.at[me ^ m]), i, (lambda ref, sender, m=m: ref.at[sender ^ m]), 3 * i + j)
                  for i in range(k) for j, m in enumerate(OTHER_CHIPS)]
        sems2, bufs2, tok = _split_call(name + "_forward", [bufs1[n_first + n] for n in group], wait=landed,
                                        wait_sems=sems1, start=onward, after=after, token=True)
        return (name, onward, sems2, bufs2), tok

    def arrived(state, after=None):
        name, onward, sems2, lands2 = state
        return _split_call(name + "_done", lands2, wait=onward, wait_sems=sems2, after=after)[1]

    h0 = x[0]
    h1, pooled, mixed_pre = _pool_fwd(h0, pw, pool_scale, gain(0, 0), gain(0, 1), tm=tm)
    (wgu0,) = arrived(forward_on("gather_gate_up_0", [0], h1)[0])
    wgu0 = wgu0.reshape(N_DEV * fb, D)
    gu0, act0 = _ffn_up(h1, gain(0, 2), wgu0, tm=tm)
    ag_down0, tok = forward_on("gather_down_0", [1], act0)
    ag_conv, tok = forward_on("gather_conv", [2, 3], tok)
    (wd0,) = arrived(ag_down0, tok)
    wd0 = wd0.reshape(N_DEV * fr, D)
    h2, ff0 = _ffn_down(h1, act0, gain(0, 3), wd0, tm=tm)
    win_g, wout_g = arrived(ag_conv, h2)
    wout = wout_g.reshape(D, D)
    h3, proj, y, conv, z = _conv_fwd(h2, gain(1, 0), gain(1, 1), win_g, taps, wout, tm=tm)
    ag_ffn1, tok = forward_on("gather_ffn1", [4, 5], h3)
    wgu1, wd1 = arrived(ag_ffn1, tok)
    wgu1, wd1 = wgu1.reshape(N_DEV * fb, D), wd1.reshape(N_DEV * fr, D)
    dh4, gu1, act1, ff1, loss_part = _ffn_fwd(h3, gain(1, 2), gain(1, 3), wgu1, wd1, 1, loss_target[0], tm=tm)

    chip = me >> 1

    def scatter_start(name, sums):
        k = len(sums)
        lands = [lax.dynamic_update_slice(lax.empty(s.shape, BF16), lax.dynamic_index_in_dim(s, chip, 0), (chip, 0, 0))
                 for s in sums]
        plan = [Copy(m, n, (lambda ref, i, m=m: ref.at[(i ^ m) >> 1]), k + n, (lambda ref, i: ref.at[i >> 1]), 3 * n + j)
                for n in range(k) for j, m in enumerate(OTHER_CHIPS)]
        sems, bufs, tok = _split_call(name + "_start", sums + lands, start=plan, token=True)
        return (name, plan, sems, bufs), tok

    def scatter_done(state, after):
        name, plan, sems, bufs = state
        return _split_call(name + "_done", bufs, wait=plan, wait_sems=sems, after=after)[1][len(bufs) // 2:]

    seq = lambda i, k: (k, 0)
    gu_pair = pl.BlockSpec((None, tk, 2 * fb), lambda i, k: (i // 2, k, i % 2))
    act_pair = pl.BlockSpec((tk, 2 * fb), lambda i, k: (k, i))
    rows = pl.BlockSpec((tk, D), seq)
    dh3, dgu1, dff1, c1, small_f1 = _ffn_bwd(dh4, h3, ff1, gu1, gain(1, 2), gain(1, 3), wgu1, wd1, 1, tm=tm_b)
    g_wgu1 = _wgrad("wgrad_gate_up_1", dgu1, c1, gu_pair, rows, (2 * fb, D), N_DEV // 2, tk=tk)
    g_wd1 = _wgrad("wgrad_down_1", act1, dff1, act_pair, rows, (2 * fb, D), N_DEV // 4, tk=tk)
    rs_ffn1, tok = scatter_start("scatter_ffn1", [g_wgu1, g_wd1])
    dh2, dproj, a1, dy, small_c = _conv_bwd(dh3, h2, y, proj, conv, gain(1, 0), gain(1, 1), win_g, taps, wout, tm=tm, after=tok)
    g_win = _wgrad("wgrad_conv_in", dproj, a1, pl.BlockSpec((tk, 2 * cb), lambda i, k: (k, i)), rows, (2 * cb, D), N_DEV // 2,
                   tk=tk)
    g_wout = _wgrad("wgrad_conv_out", z, dy, rows, rows, (D, D), 1, tk=tk)
    rs_conv, tok = scatter_start("scatter_conv", [g_win, g_wout])
    dh1, dgu0, dff0, c0, small_f0 = _ffn_bwd(dh2, h1, ff0, gu0, gain(0, 2), gain(0, 3), wgu0, wd0, 0, tm=tm_b, after=tok)
    g_wgu0 = _wgrad("wgrad_gate_up_0", dgu0, c0, gu_pair, rows, (2 * fb, D), N_DEV // 2, tk=tk)
    rs_wgu0, tok = scatter_start("scatter_gate_up_0", [g_wgu0])
    g_wd0 = _wgrad("wgrad_down_0", act0, dff0, act_pair, rows, (2 * fb, D), N_DEV // 4, tk=tk, after=tok)
    rs_wd0, tok = scatter_start("scatter_down_0", [g_wd0])
    grad_x, g_pw, small_p = _pool_bwd(dh1, h0, pooled, mixed_pre, pw, pool_scale, gain(0, 0), gain(0, 1), tm=tm, after=tok)

    loss_rows = jnp.broadcast_to(loss_part[0:1, 0:1], (8, D))
    small_part = jnp.concatenate([small_p, small_f0, small_c, small_f1, loss_rows], axis=0)
    g_pw = g_pw.reshape(4, N_DEV, gw // N_DEV, gw)
    pw_land = lax.dynamic_update_slice(lax.empty(g_pw.shape, BF16), lax.dynamic_slice_in_dim(g_pw, me, 1, 1), (0, me, 0, 0))
    last = ([Copy(m, 0, (lambda ref, i, m=m: ref.at[:, i ^ m]), 2, _second, m - 1) for m in every]
            + [Copy(m, 1, _whole, 3, _lead, N_DEV - 2 + m) for m in every])
    sems_l, bufs_l, tok = _split_call("scatter_small_start", [g_pw, small_part, pw_land, own_lead(small_part)],
                                      start=last, token=True)

    (r_wgu1, r_wd1), (r_win, r_wout) = scatter_done(rs_ffn1, tok), scatter_done(rs_conv, tok)
    o_win = _reduce_adam("adam_conv_in", [r_win], conv_in_w, m_conv_in_w, v_conv_in_w, tr=256, transposed=True)
    o_wout = _reduce_adam("adam_conv_out", [r_wout], conv_out_w, m_conv_out_w, v_conv_out_w, tr=128, after=o_win[0])
    o_wgu = _reduce_adam("adam_gate_up_1", [r_wgu1], wgu_t, m_wgu_t, v_wgu_t, tr=176, layer=1, after=o_wout[0])
    o_wd = _reduce_adam("adam_down_1", [r_wd1], ffn_down_w, m_ffn_down_w, v_ffn_down_w, tr=176, layer=1, after=o_wgu[0])
    r_pw, r_small = _split_call("scatter_small_done", bufs_l, wait=last, wait_sems=sems_l, after=o_wd[0])[1][2:]
    (r_wgu0,), (r_wd0,) = scatter_done(rs_wgu0, r_small), scatter_done(rs_wd0, r_small)
    o_wgu = _reduce_adam("adam_gate_up_0", [r_wgu0], wgu_t, m_wgu_t, v_wgu_t, tr=176, layer=0, into=o_wgu)
    o_wgu = [jnp.swapaxes(o, 1, 2) for o in o_wgu]
    o_wd = _reduce_adam("adam_down_0", [r_wd0], ffn_down_w, m_ffn_down_w, v_ffn_down_w, tr=176, layer=0, into=o_wd)
    o_pw = _reduce_adam("adam_pool_w", [r_pw[g] for g in range(4)], pool_w[0], m_pool_w[0], v_pool_w[0], tr=32)
    g_small = _small_reduce(r_small)
    loss = g_small[12, 0]
    g_cols = lax.dynamic_slice(g_small, (0, me * dcol), (16, dcol))
    o_small = _small_adam(
        g_cols[0:8], g_cols[8:11], g_small[11:12],
        (norm_gains.reshape(8, dcol), m_norm_gains.reshape(8, dcol), v_norm_gains.reshape(8, dcol)),
        (conv_w[0], m_conv_w[0], v_conv_w[0]), (pool_scale, m_pool_scale, v_pool_scale))
    d_gain, nm_gain, nv_gain, d_taps, nm_taps, nv_taps, d_scale, nm_scale, nv_scale = o_small

    gshape = norm_gains.shape
    per = lambda k: (
        (g_cols[0:8].reshape(gshape), d_gain.reshape(gshape), nm_gain.reshape(gshape), nv_gain.reshape(gshape))[k],
        o_pw[k][None], (g_small[11:12], d_scale, nm_scale, nv_scale)[k], o_win[k],
        (g_cols[8:11][None], d_taps[None], nm_taps[None], nv_taps[None])[k], o_wout[k], o_wgu[k], o_wd[k])
    return (loss, grad_x[None], *per(0), *per(1), *per(2), *per(3))
```

```python
import collections
import functools

import jax
import jax.numpy as jnp
from jax import lax
from jax.experimental import pallas as pl
from jax.experimental.pallas import tpu as pltpu

N_DEV = 8
RMS_EPS = 1e-6
POOL_WINDOWS = (2, 4, 8, 16)
POOL_HALO = 16
CONV_HALO = 16
ADAM_LR, ADAM_B1, ADAM_B2, ADAM_EPS, ADAM_WD, ADAM_STEP = 0.001, 0.9, 0.999, 1e-08, 0.01, 10

VMEM_LIMIT = 56 * 2**20
MXU_COLUMNS = 256
BF16 = jnp.bfloat16
F32 = jnp.float32
MESH = pl.DeviceIdType.MESH
SIBLING_PAIR_ID = 0


def _params(**kw):
    return pltpu.CompilerParams(vmem_limit_bytes=VMEM_LIMIT, **kw)


def _resident(shape, index_map):
    return pl.BlockSpec(shape, index_map, pipeline_mode=pl.Buffered(1))


def _ordered(body, n_in, after):
    if after is None:
        return functools.partial(body), [], []
    return (lambda *refs: body(*refs[:n_in], *refs[n_in + 1:])), [after], [pl.BlockSpec(memory_space=pl.ANY)]


def _rms_fwd(x, g):
    r = lax.rsqrt(jnp.mean(x * x, axis=-1, keepdims=True) + RMS_EPS)
    return x * r * g


def _rms_bwd(x, g, dy):
    r = lax.rsqrt(jnp.mean(x * x, axis=-1, keepdims=True) + RMS_EPS)
    xhat = x * r
    dg = jnp.sum(dy * xhat, axis=0, keepdims=True)
    t = dy * g
    dx = r * (t - xhat * jnp.mean(t * xhat, axis=-1, keepdims=True))
    return dx, dg


def _sigmoid(x):
    return 0.5 * jnp.tanh(0.5 * x) + 0.5


def _dot(a, b):
    return jnp.dot(a, b, preferred_element_type=F32)


def _dot_nt(a, b):
    return lax.dot_general(a, b, (((1,), (1,)), ((), ())), preferred_element_type=F32)


def _dot_tn(a, b):
    return lax.dot_general(a, b, (((0,), (0,)), ((), ())), preferred_element_type=F32)


def _join_blocks(blocks_hbm, joined_ref, sems):
    n, _, C = blocks_hbm.shape
    copies = [pltpu.make_async_copy(blocks_hbm.at[k], joined_ref.at[:, k * C:(k + 1) * C], sems.at[k]) for k in range(n)]
    for cp in copies:
        cp.start()
    for cp in copies:
        cp.wait()


def _row_inverse_counts(tile, tm):
    pos = (lax.broadcasted_iota(jnp.int32, (tm, 1), 0) + tile * tm + 1).astype(F32)
    return [1.0 / jnp.minimum(pos, float(w)) for w in POOL_WINDOWS]


def _pool_from_ext(ext, a, invs, gw):
    s = ext
    outs = []
    for g, w in enumerate(POOL_WINDOWS):
        s = s[:, (gw if g else 0):]
        s = s + pltpu.roll(s, w // 2, 0)
        outs.append(s[POOL_HALO:, :gw] * invs[g] - a[:, g * gw:(g + 1) * gw])
    return outs


def _pool_fwd(h, pw, scale, g_pre, g_post, *, tm, after=None):
    T, D = h.shape
    gw = D // len(POOL_WINDOWS)
    hb = tm // POOL_HALO

    def body(h_ref, halo_ref, pw_ref, scale_ref, gpre_ref, gpost_ref, out_ref, pooled_ref, mixed_ref, ext_ref):
        i = pl.program_id(0)
        x = h_ref[...]
        a = _rms_fwd(x, gpre_ref[...])
        ah = _rms_fwd(halo_ref[...], gpre_ref[...])
        ext_ref[0:POOL_HALO, :] = jnp.where(i == 0, 0.0, ah)
        ext_ref[POOL_HALO:, :] = a
        pooled = [p.astype(BF16) for p in _pool_from_ext(ext_ref[...], a, _row_inverse_counts(i, tm), gw)]
        pooled_ref[...] = jnp.concatenate(pooled, axis=1)
        mixed = jnp.concatenate([_dot(p, pw_ref[g]) for g, p in enumerate(pooled)], axis=1)
        mixed_ref[...] = mixed.astype(BF16)
        out_ref[...] = x + _rms_fwd(mixed * scale_ref[...], gpost_ref[...])

    vec = _resident((1, D), lambda i: (0, 0))
    fn, xa, xs = _ordered(body, 6, after)
    return pl.pallas_call(
        fn, name="pool_fwd", grid=(T // tm,),
        in_specs=[pl.BlockSpec((tm, D), lambda i: (i, 0)),
                  pl.BlockSpec((POOL_HALO, D), lambda i: (jnp.maximum(i * hb - 1, 0), 0)),
                  _resident(pw.shape, lambda i: (0, 0, 0)), vec, vec, vec] + xs,
        out_specs=[pl.BlockSpec((tm, D), lambda i: (i, 0))] * 3,
        out_shape=[jax.ShapeDtypeStruct((T, D), F32), jax.ShapeDtypeStruct((T, D), BF16), jax.ShapeDtypeStruct((T, D), BF16)],
        scratch_shapes=[pltpu.VMEM((tm + POOL_HALO, D), F32)],
        compiler_params=_params(dimension_semantics=("arbitrary",)),
    )(h, h, pw, scale, g_pre, g_post, *xa)


def _pool_bwd(dh, h, pooled, mixed_pre, pw, scale, g_pre, g_post, *, tm, after=None):
    T, D = h.shape
    gw = D // len(POOL_WINDOWS)
    nt = T // tm
    n_ext = tm + POOL_HALO

    def body(dh_ref, h_ref, pooled_ref, mixed_ref, pw_ref, scale_ref, gpre_ref, gpost_ref,
             dx_ref, dpw_ref, small_ref, ext2_ref, carry_ref, dpw_acc):
        i = pl.program_id(0)
        tile = nt - 1 - i

        @pl.when(i == 0)
        def _():
            small_ref[...] = jnp.zeros_like(small_ref)
            dpw_acc[...] = jnp.zeros_like(dpw_acc)
            carry_ref[...] = jnp.zeros_like(carry_ref)

        x = h_ref[...]
        dout = dh_ref[...]
        invs = _row_inverse_counts(tile, tm)
        pooled = [pooled_ref[:, g * gw:(g + 1) * gw] for g in range(len(POOL_WINDOWS))]
        mixed_pre = mixed_ref[...].astype(F32)
        scale_v = scale_ref[...]
        dmixed, dg_post = _rms_bwd(mixed_pre * scale_v, gpost_ref[...], dout)
        small_ref[1:2, :] += dg_post
        small_ref[2:3, :] += jnp.sum(dmixed * mixed_pre, axis=0, keepdims=True)
        dpre = (dmixed * scale_v).astype(BF16)
        dpooled = []
        for g in range(len(POOL_WINDOWS)):
            dp = dpre[:, g * gw:(g + 1) * gw]
            dpw_acc[g] += _dot_tn(pooled[g], dp)
            dpooled.append(_dot_nt(dp, pw_ref[g]))
        q = jnp.concatenate([d * invs[g] for g, d in enumerate(dpooled)], axis=1)
        ext2_ref[0:tm, :] = q
        ext2_ref[tm:, :] = carry_ref[...]
        carry_ref[...] = q[0:POOL_HALO, :]
        s = ext2_ref[...]
        da = []
        for g, w in enumerate(POOL_WINDOWS):
            s = s[:, (gw if g else 0):]
            s = s + pltpu.roll(s, n_ext - w // 2, 0)
            da.append(s[0:tm, :gw] - dpooled[g])
        dx, dg_pre = _rms_bwd(x, gpre_ref[...], jnp.concatenate(da, axis=1))
        small_ref[0:1, :] += dg_pre
        dx_ref[...] = dout + dx

        @pl.when(i == nt - 1)
        def _():
            dpw_ref[...] = dpw_acc[...].astype(BF16)

    vec = _resident((1, D), lambda i: (0, 0))
    rev = lambda i: (nt - 1 - i, 0)
    fn, xa, xs = _ordered(body, 8, after)
    tile = pl.BlockSpec((tm, D), rev)
    return pl.pallas_call(
        fn, name="pool_bwd", grid=(nt,),
        in_specs=[tile, tile, tile, tile, _resident(pw.shape, lambda i: (0, 0, 0)), vec, vec, vec] + xs,
        out_specs=[tile, pl.BlockSpec(pw.shape, lambda i: (0, 0, 0)), pl.BlockSpec((8, D), lambda i: (0, 0))],
        out_shape=[jax.ShapeDtypeStruct((T, D), F32), jax.ShapeDtypeStruct(pw.shape, BF16),
                   jax.ShapeDtypeStruct((8, D), F32)],
        scratch_shapes=[pltpu.VMEM((n_ext, D), F32), pltpu.VMEM((POOL_HALO, D), F32), pltpu.VMEM(pw.shape, F32)],
        compiler_params=_params(dimension_semantics=("arbitrary",)),
    )(dh, h, pooled, mixed_pre, pw, scale, g_pre, g_post, *xa)


def _ffn_fwd(h, g_pre, g_post, wgu, wd, layer, target, *, tm, after=None):
    T, D = h.shape
    F = wd.shape[0]
    last = target is not None

    def body(*refs):
        if last:
            h_ref, gpre_ref, gpost_ref, wgu_ref, wd_ref, tgt_ref, out_ref, gu_ref, act_ref, ff_ref, loss_ref = refs
        else:
            h_ref, gpre_ref, gpost_ref, wgu_ref, wd_ref, out_ref, gu_ref, act_ref, ff_ref = refs
        x = h_ref[...]
        cb = _rms_fwd(x, gpre_ref[...]).astype(BF16)
        g = _dot_nt(cb, wgu_ref[0:F, :])
        u = _dot_nt(cb, wgu_ref[F:2 * F, :])
        gu_ref[0] = g.astype(BF16)
        gu_ref[1] = u.astype(BF16)
        act = (g * _sigmoid(g) * u).astype(BF16)
        act_ref[...] = act
        acc = _dot(act, wd_ref[...])
        ff_ref[...] = acc.astype(BF16)
        hout = x + _rms_fwd(acc, gpost_ref[...])
        if last:
            diff = hout - tgt_ref[...]
            out_ref[...] = diff * (1.0 / D)

            @pl.when(pl.program_id(0) == 0)
            def _():
                loss_ref[...] = jnp.zeros_like(loss_ref)

            loss_ref[...] += jnp.sum(diff * diff) * (0.5 / D)
        else:
            out_ref[...] = hout

    vec = _resident((1, D), lambda i: (0, 0))
    tile = pl.BlockSpec((tm, D), lambda i: (i, 0))
    in_specs = [tile, vec, vec, _resident(wgu.shape, lambda i: (0, 0)), _resident(wd.shape, lambda i: (0, 0))]
    out_specs = [tile, pl.BlockSpec((2, tm, F), lambda i: (0, i, 0)), pl.BlockSpec((tm, F), lambda i: (i, 0)), tile]
    out_shape = [jax.ShapeDtypeStruct((T, D), F32), jax.ShapeDtypeStruct((2, T, F), BF16),
                 jax.ShapeDtypeStruct((T, F), BF16), jax.ShapeDtypeStruct((T, D), BF16)]
    args = [h, g_pre, g_post, wgu, wd]
    if last:
        in_specs.append(tile)
        args.append(target)
        out_specs.append(pl.BlockSpec((8, 128), lambda i: (0, 0)))
        out_shape.append(jax.ShapeDtypeStruct((8, 128), F32))
    fn, xa, xs = _ordered(body, len(args), after)
    return pl.pallas_call(
        fn, name=f"ffn_fwd_{layer}", grid=(T // tm,), in_specs=in_specs + xs, out_specs=out_specs,
        out_shape=out_shape, compiler_params=_params(dimension_semantics=("arbitrary",)),
    )(*args, *xa)


def _ffn_up(h, g_pre, wgu, *, tm):
    T, D = h.shape
    F = wgu.shape[0] // 2

    def body(h_ref, gpre_ref, wgu_ref, gu_ref, act_ref):
        cb = _rms_fwd(h_ref[...], gpre_ref[...]).astype(BF16)
        g = _dot_nt(cb, wgu_ref[0:F, :])
        u = _dot_nt(cb, wgu_ref[F:2 * F, :])
        gu_ref[0] = g.astype(BF16)
        gu_ref[1] = u.astype(BF16)
        act_ref[...] = (g * _sigmoid(g) * u).astype(BF16)

    return pl.pallas_call(
        body, name="ffn_up_0", grid=(T // tm,),
        in_specs=[pl.BlockSpec((tm, D), lambda i: (i, 0)), _resident((1, D), lambda i: (0, 0)),
                  _resident(wgu.shape, lambda i: (0, 0))],
        out_specs=[pl.BlockSpec((2, tm, F), lambda i: (0, i, 0)), pl.BlockSpec((tm, F), lambda i: (i, 0))],
        out_shape=[jax.ShapeDtypeStruct((2, T, F), BF16), jax.ShapeDtypeStruct((T, F), BF16)],
        compiler_params=_params(dimension_semantics=("arbitrary",)),
    )(h, g_pre, wgu)


def _ffn_down(h, act, g_post, wd, *, tm):
    T, D = h.shape
    F = act.shape[1]

    def body(h_ref, act_ref, gpost_ref, wd_ref, out_ref, ff_ref):
        acc = _dot(act_ref[...], wd_ref[...])
        ff_ref[...] = acc.astype(BF16)
        out_ref[...] = h_ref[...] + _rms_fwd(acc, gpost_ref[...])

    tile = pl.BlockSpec((tm, D), lambda i: (i, 0))
    return pl.pallas_call(
        body, name="ffn_down_0", grid=(T // tm,),
        in_specs=[tile, pl.BlockSpec((tm, F), lambda i: (i, 0)), _resident((1, D), lambda i: (0, 0)),
                  _resident(wd.shape, lambda i: (0, 0))],
        out_specs=[tile, tile],
        out_shape=[jax.ShapeDtypeStruct((T, D), F32), jax.ShapeDtypeStruct((T, D), BF16)],
        compiler_params=_params(dimension_semantics=("arbitrary",)),
    )(h, act, g_post, wd)


def _ffn_bwd(dh, h, ff, gu, g_pre, g_post, wgu, wd, layer, *, tm, after=None):
    T, D = h.shape
    F = wd.shape[0]
    n_chunks = F // MXU_COLUMNS

    def body(dh_ref, h_ref, ff_ref, gu_ref, gpre_ref, gpost_ref, wgu_ref, wd_ref,
             dx_ref, dgu_ref, dff_ref, c_ref, small_ref):
        @pl.when(pl.program_id(0) == 0)
        def _():
            small_ref[...] = jnp.zeros_like(small_ref)

        dout = dh_ref[...]
        dff, dg_post = _rms_bwd(ff_ref[...].astype(F32), gpost_ref[...], dout)
        small_ref[1:2, :] += dg_post
        dffb = dff.astype(BF16)
        dff_ref[...] = dffb
        dc = jnp.zeros((tm, D), F32)
        for j in range(n_chunks + 1):
            lo, hi = j * MXU_COLUMNS, (j + 1) * MXU_COLUMNS
            if j < n_chunks:
                dact = _dot_nt(dffb, wd_ref[lo:hi, :])
            if j > 0:
                lo0 = lo - MXU_COLUMNS
                dc = dc + _dot(dgu_ref[0, :, lo0:lo], wgu_ref[lo0:lo, :]) + _dot(dgu_ref[1, :, lo0:lo], wgu_ref[F + lo0:F + lo, :])
            if j < n_chunks:
                g = gu_ref[0, :, lo:hi].astype(F32)
                u = gu_ref[1, :, lo:hi].astype(F32)
                s = _sigmoid(g)
                dgu_ref[0, :, lo:hi] = (dact * u * (s * (1.0 + g * (1.0 - s)))).astype(BF16)
                dgu_ref[1, :, lo:hi] = (dact * (g * s)).astype(BF16)
        x = h_ref[...]
        c_ref[...] = _rms_fwd(x, gpre_ref[...]).astype(BF16)
        dx, dg_pre = _rms_bwd(x, gpre_ref[...], dc)
        small_ref[0:1, :] += dg_pre
        dx_ref[...] = dout + dx

    vec = _resident((1, D), lambda i: (0, 0))
    tile = pl.BlockSpec((tm, D), lambda i: (i, 0))
    blk = pl.BlockSpec((2, tm, F), lambda i: (0, i, 0))
    fn, xa, xs = _ordered(body, 8, after)
    return pl.pallas_call(
        fn, name=f"ffn_bwd_{layer}", grid=(T // tm,),
        in_specs=[tile, tile, tile, blk, vec, vec,
                  _resident(wgu.shape, lambda i: (0, 0)), _resident(wd.shape, lambda i: (0, 0))] + xs,
        out_specs=[tile, blk, tile, tile, pl.BlockSpec((8, D), lambda i: (0, 0))],
        out_shape=[jax.ShapeDtypeStruct((T, D), F32), jax.ShapeDtypeStruct((2, T, F), BF16),
                   jax.ShapeDtypeStruct((T, D), BF16), jax.ShapeDtypeStruct((T, D), BF16),
                   jax.ShapeDtypeStruct((8, D), F32)],
        compiler_params=_params(dimension_semantics=("arbitrary",)),
    )(dh, h, ff, gu, g_pre, g_post, wgu, wd, *xa)


def _conv_fwd(h, g_pre, g_post, win, taps, wout, *, tm, after=None):
    T, D = h.shape
    nblk, cb = win.shape[0], win.shape[2]

    def body(h_ref, gpre_ref, gpost_ref, win_hbm, taps_ref, wout_ref,
             out_ref, proj_ref, y_ref, conv_ref, z_ref, proj_scr, ext_ref, carry_ref, win_ref, win_sems):
        i = pl.program_id(0)

        @pl.when(i == 0)
        def _():
            carry_ref[...] = jnp.zeros_like(carry_ref)
            _join_blocks(win_hbm, win_ref, win_sems)

        x = h_ref[...]
        a = _rms_fwd(x, gpre_ref[...]).astype(BF16)
        proj_scr[...] = _dot(a, win_ref[...])
        proj_ref[...] = proj_scr[...].astype(BF16)
        u = proj_scr[:, D:2 * D] * proj_scr[:, 2 * D:3 * D]
        ext_ref[0:CONV_HALO, :] = carry_ref[...]
        ext_ref[CONV_HALO:, :] = u
        carry_ref[...] = u[tm - CONV_HALO:, :]
        e = ext_ref[...]
        conv = (taps_ref[2:3, :] * u + taps_ref[1:2, :] * pltpu.roll(e, 1, 0)[CONV_HALO:, :]
                + taps_ref[0:1, :] * pltpu.roll(e, 2, 0)[CONV_HALO:, :])
        conv_ref[...] = conv.astype(BF16)
        z = (proj_scr[:, 0:D] * conv).astype(BF16)
        z_ref[...] = z
        y = _dot(z, wout_ref[...])
        y_ref[...] = y.astype(BF16)
        out_ref[...] = x + _rms_fwd(y, gpost_ref[...])

    vec = _resident((1, D), lambda i: (0, 0))
    tile = pl.BlockSpec((tm, D), lambda i: (i, 0))
    fn, xa, xs = _ordered(body, 6, after)
    return pl.pallas_call(
        fn, name="conv_fwd", grid=(T // tm,),
        in_specs=[tile, vec, vec, pl.BlockSpec(memory_space=pl.ANY),
                  _resident(taps.shape, lambda i: (0, 0)), _resident(wout.shape, lambda i: (0, 0))] + xs,
        out_specs=[tile, pl.BlockSpec((tm, 3 * D), lambda i: (i, 0)), tile, tile, tile],
        out_shape=[jax.ShapeDtypeStruct((T, D), F32), jax.ShapeDtypeStruct((T, 3 * D), BF16),
                   jax.ShapeDtypeStruct((T, D), BF16), jax.ShapeDtypeStruct((T, D), BF16),
                   jax.ShapeDtypeStruct((T, D), BF16)],
        scratch_shapes=[pltpu.VMEM((tm, 3 * D), F32), pltpu.VMEM((tm + CONV_HALO, D), F32),
                        pltpu.VMEM((CONV_HALO, D), F32), pltpu.VMEM((D, nblk * cb), BF16),
                        pltpu.SemaphoreType.DMA((nblk,))],
        compiler_params=_params(dimension_semantics=("arbitrary",)),
    )(h, g_pre, g_post, win, taps, wout, *xa)


def _conv_bwd(dh, h, y, proj, conv, g_pre, g_post, win, taps, wout, *, tm, after=None):
    T, D = h.shape
    nblk, cb = win.shape[0], win.shape[2]
    nt = T // tm
    hb = tm // CONV_HALO
    n_ext = tm + CONV_HALO

    def body(dh_ref, h_ref, y_ref, proj_ref, halo_ref, conv_ref, gpre_ref, gpost_ref, win_hbm, taps_ref, wout_ref,
             dx_ref, dproj_ref, a_ref, dy_ref, small_ref, ext_ref, ext2_ref, carry_ref, win_ref, win_sems):
        i = pl.program_id(0)
        tile = nt - 1 - i

        @pl.when(i == 0)
        def _():
            small_ref[...] = jnp.zeros_like(small_ref)
            carry_ref[...] = jnp.zeros_like(carry_ref)
            _join_blocks(win_hbm, win_ref, win_sems)

        dout = dh_ref[...]
        dy, dg_post = _rms_bwd(y_ref[...].astype(F32), gpost_ref[...], dout)
        small_ref[1:2, :] += dg_post
        dyb = dy.astype(BF16)
        dy_ref[...] = dyb
        dz = _dot_nt(dyb, wout_ref[...])
        bgate = proj_ref[:, 0:D].astype(F32)
        cgate = proj_ref[:, D:2 * D].astype(F32)
        v = proj_ref[:, 2 * D:3 * D].astype(F32)
        u = cgate * v
        uh = halo_ref[:, D:2 * D].astype(F32) * halo_ref[:, 2 * D:3 * D].astype(F32)
        ext_ref[0:CONV_HALO, :] = jnp.where(tile == 0, 0.0, uh)
        ext_ref[CONV_HALO:, :] = u
        e = ext_ref[...]
        u1 = pltpu.roll(e, 1, 0)[CONV_HALO:, :]
        u2 = pltpu.roll(e, 2, 0)[CONV_HALO:, :]
        t0, t1, t2 = taps_ref[0:1, :], taps_ref[1:2, :], taps_ref[2:3, :]
        dconv = dz * bgate
        small_ref[2:3, :] += jnp.sum(dconv * u2, axis=0, keepdims=True)
        small_ref[3:4, :] += jnp.sum(dconv * u1, axis=0, keepdims=True)
        small_ref[4:5, :] += jnp.sum(dconv * u, axis=0, keepdims=True)
        ext2_ref[0:tm, :] = dconv
        ext2_ref[tm:, :] = carry_ref[...]
        carry_ref[...] = dconv[0:CONV_HALO, :]
        e2 = ext2_ref[...]
        du = (t2 * dconv + t1 * pltpu.roll(e2, n_ext - 1, 0)[0:tm, :]
              + t0 * pltpu.roll(e2, n_ext - 2, 0)[0:tm, :])
        dproj_ref[:, 0:D] = (dz * conv_ref[...].astype(F32)).astype(BF16)
        dproj_ref[:, D:2 * D] = (du * v).astype(BF16)
        dproj_ref[:, 2 * D:3 * D] = (du * cgate).astype(BF16)
        da = _dot_nt(dproj_ref[...], win_ref[...])
        x = h_ref[...]
        a_ref[...] = _rms_fwd(x, gpre_ref[...]).astype(BF16)
        dx, dg_pre = _rms_bwd(x, gpre_ref[...], da)
        small_ref[0:1, :] += dg_pre
        dx_ref[...] = dout + dx

    vec = _resident((1, D), lambda i: (0, 0))
    rev = lambda i: (nt - 1 - i, 0)
    tile = pl.BlockSpec((tm, D), rev)
    wide = pl.BlockSpec((tm, 3 * D), rev)
    fn, xa, xs = _ordered(body, 11, after)
    return pl.pallas_call(
        fn, name="conv_bwd", grid=(nt,),
        in_specs=[tile, tile, tile, wide,
                  pl.BlockSpec((CONV_HALO, 3 * D), lambda i: (jnp.maximum((nt - 1 - i) * hb - 1, 0), 0)),
                  tile, vec, vec, pl.BlockSpec(memory_space=pl.ANY),
                  _resident(taps.shape, lambda i: (0, 0)), _resident(wout.shape, lambda i: (0, 0))] + xs,
        out_specs=[tile, wide, tile, tile, pl.BlockSpec((8, D), lambda i: (0, 0))],
        out_shape=[jax.ShapeDtypeStruct((T, D), F32), jax.ShapeDtypeStruct((T, 3 * D), BF16),
                   jax.ShapeDtypeStruct((T, D), BF16), jax.ShapeDtypeStruct((T, D), BF16),
                   jax.ShapeDtypeStruct((8, D), F32)],
        scratch_shapes=[pltpu.VMEM((n_ext, D), F32), pltpu.VMEM((n_ext, D), F32),
                        pltpu.VMEM((CONV_HALO, D), F32), pltpu.VMEM((D, nblk * cb), BF16),
                        pltpu.SemaphoreType.DMA((nblk,))],
        compiler_params=_params(dimension_semantics=("arbitrary",)),
    )(dh, h, y, proj, proj, conv, g_pre, g_post, win, taps, wout, *xa)


def _wgrad(name, a, b, a_spec, b_spec, block, n_blocks, *, tk, after=None):
    T = a.shape[-2]
    nk = T // tk
    M, N = block
    m = N_DEV // n_blocks
    R = M // m

    def body(a_ref, b_ref, out_ref, land_hbm, acc_ref, stage_ref, recv_ref, send_sems, recv_sems, land_sem):
        i, k = pl.program_id(0), pl.program_id(1)
        x, y, c = lax.axis_index("x"), lax.axis_index("y"), lax.axis_index("c")

        def sent(blk, p):
            owner = blk * m + p
            q = owner // 2
            return (owner % 2) != c, pltpu.make_async_remote_copy(
                src_ref=stage_ref.at[p * R:(p + 1) * R], dst_ref=recv_ref.at[q], send_sem=send_sems.at[q],
                recv_sem=recv_sems.at[q], device_id=(x, y, 1 - c), device_id_type=MESH)

        @pl.when(jnp.logical_and(i == 0, k == 0))
        def _():
            barrier = pltpu.get_barrier_semaphore()
            pl.semaphore_signal(barrier, inc=1, device_id=(x, y, 1 - c), device_id_type=MESH)
            pl.semaphore_wait(barrier, 1)

        @pl.when(k == 0)
        def _():
            acc_ref[...] = jnp.zeros_like(acc_ref)

        acc_ref[...] += _dot_tn(a_ref[...], b_ref[...])

        @pl.when(k == nk - 1)
        def _():
            for p in range(m):
                away, copy = sent(jnp.maximum(i - 1, 0), p)

                @pl.when(jnp.logical_and(i > 0, away))
                def _():
                    copy.wait_send()

            acc = acc_ref[...]
            stage_ref[...] = acc.astype(BF16)
            for p in range(m):
                away, copy = sent(i, p)

                @pl.when(away)
                def _():
                    copy.start()

                @pl.when(jnp.logical_not(away))
                def _():
                    out_ref[(i * m + p) // 2] = stage_ref[p * R:(p + 1) * R, :]

        @pl.when(jnp.logical_and(i == n_blocks - 1, k == nk - 1))
        def _():
            for p in range(m):
                away, copy = sent(i, p)

                @pl.when(away)
                def _():
                    copy.wait_send()

            for q in range(N_DEV // 2):
                pltpu.make_async_remote_copy(
                    src_ref=stage_ref.at[0:R], dst_ref=recv_ref.at[q], send_sem=send_sems.at[q],
                    recv_sem=recv_sems.at[q], device_id=(x, y, 1 - c), device_id_type=MESH).wait_recv()
                out_ref[q] = (out_ref[q].astype(F32) + recv_ref[q].astype(F32)).astype(BF16)
            mine = pltpu.make_async_copy(out_ref.at[2 * x + y], land_hbm.at[2 * x + y], land_sem)
            mine.start()
            mine.wait()

    fn, xa, xs = _ordered(body, 2, after)
    sums = jax.ShapeDtypeStruct((N_DEV // 2, R, N), BF16)
    return pl.pallas_call(
        fn, name=name, grid=(n_blocks, nk), in_specs=[a_spec, b_spec] + xs,
        out_specs=[pl.BlockSpec((N_DEV // 2, R, N), lambda i, k: (0, 0, 0)), pl.BlockSpec(memory_space=pl.ANY)],
        out_shape=[sums, sums],
        scratch_shapes=[pltpu.VMEM(block, F32), pltpu.VMEM(block, BF16), pltpu.VMEM((N_DEV // 2, R, N), BF16),
                        pltpu.SemaphoreType.DMA((N_DEV // 2,)), pltpu.SemaphoreType.DMA((N_DEV // 2,)),
                        pltpu.SemaphoreType.DMA],
        compiler_params=_params(dimension_semantics=("arbitrary", "arbitrary"), collective_id=SIBLING_PAIR_ID),
    )(a, b, *xa)


Copy = collections.namedtuple("Copy", "mask sb src db dst sem")
Local = collections.namedtuple("Local", "sb src db dst")

HBM_SPEC = pl.BlockSpec(memory_space=pltpu.HBM)
SEM_SPEC = pl.BlockSpec(memory_space=pltpu.SEMAPHORE)
SIBLING, X_PEER, Y_PEER, DIAGONAL = 1, 4, 2, 6
OTHER_CHIPS = (X_PEER, Y_PEER, DIAGONAL)


def _whole(ref, i):
    return ref


def _lead(ref, i):
    return ref.at[i]


def _second(ref, i):
    return ref.at[:, i]


def _place():
    x, y, c = lax.axis_index("x"), lax.axis_index("y"), lax.axis_index("c")
    return (x, y, c), 4 * x + 2 * y + c


def _descriptor(cp, bufs, xyc, me, sender, send_sems, recv_sems):
    x, y, c = xyc
    flip = lambda v, bit: (1 - v) if bit else v
    return pltpu.make_async_remote_copy(
        src_ref=cp.src(bufs[cp.sb], me), dst_ref=cp.dst(bufs[cp.db], sender),
        send_sem=send_sems.at[cp.sem], recv_sem=recv_sems.at[cp.sem],
        device_id=(flip(x, cp.mask & 4), flip(y, cp.mask & 2), flip(c, cp.mask & 1)), device_id_type=MESH)


def _exchange(name, bufs, plan, local=()):
    n = len(bufs)

    def body(*refs):
        ins = refs[:n]
        send_sems, recv_sems, local_sems = refs[2 * n:]
        xyc, me = _place()
        own = [pltpu.make_async_copy(lc.src(ins[lc.sb], me), lc.dst(ins[lc.db], me), local_sems.at[i])
               for i, lc in enumerate(local)]
        sends = [_descriptor(cp, ins, xyc, me, me, send_sems, recv_sems) for cp in plan]
        for cp in own + sends:
            cp.start()
        for cp in plan:
            _descriptor(cp, ins, xyc, me, me ^ cp.mask, send_sems, recv_sems).wait_recv()
        for cp in sends:
            cp.wait_send()
        for cp in own:
            cp.wait()

    return pl.pallas_call(
        body, name=name, in_specs=[HBM_SPEC] * n, out_specs=[HBM_SPEC] * n,
        out_shape=[jax.ShapeDtypeStruct(b.shape, b.dtype) for b in bufs],
        input_output_aliases={i: i for i in range(n)},
        scratch_shapes=[pltpu.SemaphoreType.DMA((len(plan),)), pltpu.SemaphoreType.DMA((len(plan),)),
                        pltpu.SemaphoreType.DMA((max(len(local), 1),))],
    )(*bufs)


def _place_own(me, items):
    def body(me_ref, *refs):
        for src, dst in zip(refs[:len(items)], refs[len(items):]):
            dst[...] = src[...].astype(dst.dtype)

    return pl.pallas_call(
        body, name="place_own",
        grid_spec=pltpu.PrefetchScalarGridSpec(
            num_scalar_prefetch=1, grid=(1,),
            in_specs=[pl.BlockSpec(blk, functools.partial(lambda i, m, idx: idx, idx=idx)) for _, blk, idx, _, _, _, _ in items],
            out_specs=[pl.BlockSpec(oblk, functools.partial(lambda i, m, at: at(m[0]), at=at)) for *_, oblk, at in items]),
        out_shape=[jax.ShapeDtypeStruct(shape, dtype) for _, _, _, shape, dtype, _, _ in items],
        compiler_params=_params(dimension_semantics=("arbitrary",)),
    )(jnp.reshape(me, (1,)).astype(jnp.int32), *[a for a, *_ in items])


def _split_call(name, bufs, *, wait=None, wait_sems=None, start=None, local=(), after=None, token=False):
    n = len(bufs)
    n_wait = 2 if wait else 0
    n_after = 1 if after is not None else 0
    n_start = 2 if start else 0

    def body(*refs):
        ins = refs[:n]
        wsend, wrecv = refs[n:n + n_wait] if wait else (None, None)
        outs = refs[n + n_wait + n_after:]
        ssend, srecv = outs[:n_start] if start else (None, None)
        rest = outs[n_start + n:]
        xyc, me = _place()
        for cp in wait or ():
            d = _descriptor(cp, ins, xyc, me, me ^ cp.mask, wsend, wrecv)
            d.wait_send()
            d.wait_recv()
        own = [pltpu.make_async_copy(lc.src(ins[lc.sb], me), lc.dst(ins[lc.db], me), rest[-1].at[i])
               for i, lc in enumerate(local)]
        for cp in own:
            cp.start()
        for cp in start or ():
            _descriptor(cp, ins, xyc, me, me, ssend, srecv).start()
        for cp in own:
            cp.wait()
        if token:
            rest[0][...] = jnp.zeros_like(rest[0])

    args = [pltpu.with_memory_space_constraint(b, pltpu.HBM) for b in bufs]
    in_specs = [HBM_SPEC] * n
    if wait:
        args += list(wait_sems)
        in_specs += [SEM_SPEC] * 2
    if after is not None:
        args.append(after)
        in_specs.append(pl.BlockSpec(memory_space=pl.ANY))
    out_shape, out_specs = [], []
    if start:
        out_shape += [pltpu.SemaphoreType.DMA((len(start),))] * 2
        out_specs += [SEM_SPEC] * 2
    out_shape += [pltpu.HBM(b.shape, b.dtype) for b in bufs]
    out_specs += [HBM_SPEC] * n
    if token:
        out_shape.append(jax.ShapeDtypeStruct((8, 128), F32))
        out_specs.append(pl.BlockSpec(memory_space=pltpu.VMEM))
    outs = pl.pallas_call(
        body, name=name, in_specs=in_specs, out_specs=out_specs, out_shape=out_shape,
        input_output_aliases={i: n_start + i for i in range(n)},
        scratch_shapes=[pltpu.SemaphoreType.DMA((len(local),))] if local else [],
        compiler_params=pltpu.CompilerParams(has_side_effects=pltpu.SideEffectType.DATAFLOW_SIDE_EFFECTING),
    )(*args)
    sems = tuple(outs[:n_start]) if start else None
    return sems, list(outs[n_start:n_start + n]), (outs[n_start + n] if token else None)


def _adamw(w, g, m, v):
    m = ADAM_B1 * m + (1.0 - ADAM_B1) * g
    v = ADAM_B2 * v + (1.0 - ADAM_B2) * (g * g)
    m_hat = m / (1.0 - ADAM_B1 ** ADAM_STEP)
    v_hat = v / (1.0 - ADAM_B2 ** ADAM_STEP)
    delta = -ADAM_LR * (m_hat / (jnp.sqrt(v_hat) + ADAM_EPS) + ADAM_WD * w)
    return delta, m, v


def _reduce_adam(name, parts, w, m, v, *, tr, layer=None, into=None, after=None, transposed=False):
    L, R, C = w.shape
    S = parts[0].shape[0]
    tr = min(tr, R)
    n_l = L if layer is None else 1
    first = 0 if layer is None else layer

    def body(*refs):
        p_refs = refs[:n_l]
        w_ref, m_ref, v_ref = refs[n_l:n_l + 3]
        g_ref, d_ref, nm_ref, nv_ref = refs[-4:]
        for l in range(n_l):
            g = p_refs[l][0].astype(F32)
            for s in range(1, S):
                g = g + p_refs[l][s].astype(F32)
            g = g.T if transposed else g
            g_ref[l] = g
            d_ref[l], nm_ref[l], nv_ref[l] = _adamw(w_ref[l], g, m_ref[l], v_ref[l])

    blk = pl.BlockSpec((n_l, tr, C), lambda r: (first, r, 0))
    out = jax.ShapeDtypeStruct((L, R, C), F32)
    extra = list(into or []) + ([after] if after is not None else [])
    return pl.pallas_call(
        body, name=name, grid=(R // tr,),
        in_specs=[pl.BlockSpec((S, C, tr), lambda r: (0, 0, r)) if transposed else pl.BlockSpec((S, tr, C), lambda r: (0, r, 0))] * n_l
        + [blk, blk, blk]
        + [pl.BlockSpec(memory_space=pl.ANY)] * len(extra),
        out_specs=[blk] * 4, out_shape=[out] * 4,
        input_output_aliases={n_l + 3 + i: i for i in range(4)} if into else {},
        compiler_params=_params(dimension_semantics=("arbitrary",)),
    )(*parts, w, m, v, *extra)


def _small_reduce(parts):
    D = parts.shape[2]
    rows = [0, 1, 8, 9, 16, 17, 24, 25, 18, 19, 20, 2, 32]

    def body(p_ref, out_ref):
        s = p_ref[0]
        for d in range(1, N_DEV):
            s = s + p_ref[d]
        out_ref[...] = jnp.zeros_like(out_ref)
        for r, src in enumerate(rows):
            out_ref[r:r + 1, :] = s[src:src + 1, :]

    return pl.pallas_call(body, name="small_reduce", out_shape=jax.ShapeDtypeStruct((16, D), F32))(parts)


def _small_adam(g_gain, g_taps, g_scale, gains, taps, scale):
    def body(gg, gt, gs, wg, mg, vg, wt, mt, vt, ws, ms, vs, *outs):
        for k, (g, w, m, v) in enumerate(((gg, wg, mg, vg), (gt, wt, mt, vt), (gs, ws, ms, vs))):
            outs[3 * k][...], outs[3 * k + 1][...], outs[3 * k + 2][...] = _adamw(w[...], g[...], m[...], v[...])

    shapes = [jax.ShapeDtypeStruct(t[0].shape, F32) for t in (gains, taps, scale) for _ in range(3)]
    return pl.pallas_call(body, name="small_adam", out_shape=shapes)(g_gain, g_taps, g_scale, *gains, *taps, *scale)


def kernel(x, norm_gains, pool_w, pool_scale, conv_in_w, conv_w, conv_out_w, ffn_gate_up_w, ffn_down_w, loss_target, m_norm_gains, m_pool_w, m_pool_scale, m_conv_in_w, m_conv_w, m_conv_out_w, m_ffn_gate_up_w, m_ffn_down_w, v_norm_gains, v_pool_w, v_pool_scale, v_conv_in_w, v_conv_w, v_conv_out_w, v_ffn_gate_up_w, v_ffn_down_w):
    T, D = x.shape[1], x.shape[2]
    tm = min(512, T)
    tm_b = min(256, T)
    tk = min(2048, T)
    n_layers = ffn_gate_up_w.shape[0]
    fb = ffn_gate_up_w.shape[2]
    fr = ffn_down_w.shape[1]
    dcol = norm_gains.shape[2]
    cb = conv_in_w.shape[2]
    gw = pool_w.shape[3]
    me = 4 * lax.axis_index("x") + 2 * lax.axis_index("y") + lax.axis_index("c")

    small_w = jnp.concatenate([norm_gains.reshape(8, dcol), jnp.pad(conv_w[0], ((0, 5), (0, 0)))], axis=0)
    every = range(1, N_DEV)
    wgu_t, m_wgu_t, v_wgu_t = (jnp.swapaxes(a, 1, 2) for a in (ffn_gate_up_w, m_ffn_gate_up_w, v_ffn_gate_up_w))
    own_lead = lambda s: lax.dynamic_update_slice(lax.empty((N_DEV,) + s.shape, s.dtype), s[None], (me,) + (0,) * s.ndim)
    lead_item = lambda a, l, dtype: (a, (None,) + a.shape[1:], (l, 0, 0), (N_DEV,) + a.shape[1:], dtype,
                                     (None,) + a.shape[1:], lambda i: (i, 0, 0))
    lands = _place_own(me, [
        (pool_w, (None,) + pool_w.shape[1:], (0, 0, 0, 0), (4, N_DEV, gw // N_DEV, gw), BF16,
         (4, None, gw // N_DEV, gw), lambda i: (0, i, 0, 0)),
        (small_w[None], (None,) + small_w.shape, (0, 0, 0), (N_DEV,) + small_w.shape, F32, (None,) + small_w.shape, lambda i: (i, 0, 0)),
        lead_item(wgu_t, 0, BF16), lead_item(ffn_down_w, 0, BF16), lead_item(conv_in_w, 0, BF16),
        lead_item(conv_out_w, 0, BF16), lead_item(wgu_t, 1, BF16), lead_item(ffn_down_w, 1, BF16)])
    n_first, n_big = 2, len(lands) - 2
    direct = ([Copy(m, 0, _second, 0, _second, m - 1) for m in every]
              + [Copy(m, 1, _lead, 1, _lead, N_DEV - 2 + m) for m in every])
    level1 = [Copy(mask, n_first + n, _lead, n_first + n, _lead, len(direct) + 4 * n + j)
              for n in range(n_big) for j, mask in enumerate((SIBLING,) + OTHER_CHIPS)]
    sems1, bufs1, _ = _split_call("gather_start", lands, start=direct + level1)
    pw_g, small_g = _split_call("gather_small_done", bufs1[:n_first], wait=direct, wait_sems=sems1)[1]
    pw = pw_g.reshape(4, gw, gw)
    small_full = jnp.swapaxes(small_g, 0, 1).reshape(16, D)
    gain = lambda l, s: small_full[4 * l + s][None, :]
    taps = small_full[8:16]

    def forward_on(name, group, after):
        k = len(group)
        landed = [Copy(cp.mask, i, cp.src, i, cp.dst, cp.sem)
                  for i, n in enumerate(group) for cp in level1 if cp.sb == n_first + n]
        onward = [Copy(SIBLING, i, (lambda ref, me, m=m: ref.at[me ^ m]), i, (lambda ref, sender, m=m: ref.at[sender ^ m]), 3 * i + j)
                  for i in range(k) for j, m in enumerate(OTHER_CHIPS)]
        sems2, bufs2, tok = _split_call(name + "_forward", [bufs1[n_first + n] for n in group], wait=landed,
                                        wait_sems=sems1, start=onward, after=after, token=True)
        return (name, onward, sems2, bufs2), tok

    def arrived(state, after=None):
        name, onward, sems2, lands2 = state
        return _split_call(name + "_done", lands2, wait=onward, wait_sems=sems2, after=after)[1]

    h0 = x[0]
    h1, pooled, mixed_pre = _pool_fwd(h0, pw, pool_scale, gain(0, 0), gain(0, 1), tm=tm)
    (wgu0,) = arrived(forward_on("gather_gate_up_0", [0], h1)[0])
    wgu0 = wgu0.reshape(N_DEV * fb, D)
    gu0, act0 = _ffn_up(h1, gain(0, 2), wgu0, tm=tm)
    ag_down0, tok = forward_on("gather_down_0", [1], act0)
    ag_conv, tok = forward_on("gather_conv", [2, 3], tok)
    (wd0,) = arrived(ag_down0, tok)
    wd0 = wd0.reshape(N_DEV * fr, D)
    h2, ff0 = _ffn_down(h1, act0, gain(0, 3), wd0, tm=tm)
    win_g, wout_g = arrived(ag_conv, h2)
    wout = wout_g.reshape(D, D)
    h3, proj, y, conv, z = _conv_fwd(h2, gain(1, 0), gain(1, 1), win_g, taps, wout, tm=tm)
    ag_ffn1, tok = forward_on("gather_ffn1", [4, 5], h3)
    wgu1, wd1 = arrived(ag_ffn1, tok)
    wgu1, wd1 = wgu1.reshape(N_DEV * fb, D), wd1.reshape(N_DEV * fr, D)
    dh4, gu1, act1, ff1, loss_part = _ffn_fwd(h3, gain(1, 2), gain(1, 3), wgu1, wd1, 1, loss_target[0], tm=tm)


    def scatter_start(name, pairs):
        k = len(pairs)
        sums, lands = [p[0] for p in pairs], [p[1] for p in pairs]
        plan = [Copy(m, n, (lambda ref, i, m=m: ref.at[(i ^ m) >> 1]), k + n, (lambda ref, i: ref.at[i >> 1]), 3 * n + j)
                for n in range(k) for j, m in enumerate(OTHER_CHIPS)]
        sems, bufs, tok = _split_call(name + "_start", sums + lands, start=plan, token=True)
        return (name, plan, sems, bufs), tok

    def scatter_done(state, after):
        name, plan, sems, bufs = state
        return _split_call(name + "_done", bufs, wait=plan, wait_sems=sems, after=after)[1][len(bufs) // 2:]

    seq = lambda i, k: (k, 0)
    gu_pair = pl.BlockSpec((None, tk, 2 * fb), lambda i, k: (i // 2, k, i % 2))
    act_pair = pl.BlockSpec((tk, 2 * fb), lambda i, k: (k, i))
    rows = pl.BlockSpec((tk, D), seq)
    dh3, dgu1, dff1, c1, small_f1 = _ffn_bwd(dh4, h3, ff1, gu1, gain(1, 2), gain(1, 3), wgu1, wd1, 1, tm=tm_b)
    g_wgu1 = _wgrad("wgrad_gate_up_1", dgu1, c1, gu_pair, rows, (2 * fb, D), N_DEV // 2, tk=tk)
    g_wd1 = _wgrad("wgrad_down_1", act1, dff1, act_pair, rows, (2 * fb, D), N_DEV // 4, tk=tk)
    rs_ffn1, tok = scatter_start("scatter_ffn1", [g_wgu1, g_wd1])
    dh2, dproj, a1, dy, small_c = _conv_bwd(dh3, h2, y, proj, conv, gain(1, 0), gain(1, 1), win_g, taps, wout, tm=tm, after=tok)
    g_win = _wgrad("wgrad_conv_in", dproj, a1, pl.BlockSpec((tk, 2 * cb), lambda i, k: (k, i)), rows, (2 * cb, D), N_DEV // 2,
                   tk=tk)
    g_wout = _wgrad("wgrad_conv_out", z, dy, rows, rows, (D, D), 1, tk=tk)
    rs_conv, tok = scatter_start("scatter_conv", [g_win, g_wout])
    dh1, dgu0, dff0, c0, small_f0 = _ffn_bwd(dh2, h1, ff0, gu0, gain(0, 2), gain(0, 3), wgu0, wd0, 0, tm=tm_b, after=tok)
    g_wgu0 = _wgrad("wgrad_gate_up_0", dgu0, c0, gu_pair, rows, (2 * fb, D), N_DEV // 2, tk=tk)
    rs_wgu0, tok = scatter_start("scatter_gate_up_0", [g_wgu0])
    g_wd0 = _wgrad("wgrad_down_0", act0, dff0, act_pair, rows, (2 * fb, D), N_DEV // 4, tk=tk, after=tok)
    rs_wd0, tok = scatter_start("scatter_down_0", [g_wd0])
    grad_x, g_pw, small_p = _pool_bwd(dh1, h0, pooled, mixed_pre, pw, pool_scale, gain(0, 0), gain(0, 1), tm=tm, after=tok)

    loss_rows = jnp.broadcast_to(loss_part[0:1, 0:1], (8, D))
    small_part = jnp.concatenate([small_p, small_f0, small_c, small_f1, loss_rows], axis=0)
    g_pw = g_pw.reshape(4, N_DEV, gw // N_DEV, gw)
    pw_land = lax.dynamic_update_slice(lax.empty(g_pw.shape, BF16), lax.dynamic_slice_in_dim(g_pw, me, 1, 1), (0, me, 0, 0))
    last = ([Copy(m, 0, (lambda ref, i, m=m: ref.at[:, i ^ m]), 2, _second, m - 1) for m in every]
            + [Copy(m, 1, _whole, 3, _lead, N_DEV - 2 + m) for m in every])
    sems_l, bufs_l, tok = _split_call("scatter_small_start", [g_pw, small_part, pw_land, own_lead(small_part)],
                                      start=last, token=True)

    (r_wgu1, r_wd1), (r_win, r_wout) = scatter_done(rs_ffn1, tok), scatter_done(rs_conv, tok)
    o_win = _reduce_adam("adam_conv_in", [r_win], conv_in_w, m_conv_in_w, v_conv_in_w, tr=256, transposed=True)
    o_wout = _reduce_adam("adam_conv_out", [r_wout], conv_out_w, m_conv_out_w, v_conv_out_w, tr=128, after=o_win[0])
    o_wgu = _reduce_adam("adam_gate_up_1", [r_wgu1], wgu_t, m_wgu_t, v_wgu_t, tr=176, layer=1, after=o_wout[0])
    o_wd = _reduce_adam("adam_down_1", [r_wd1], ffn_down_w, m_ffn_down_w, v_ffn_down_w, tr=176, layer=1, after=o_wgu[0])
    r_pw, r_small = _split_call("scatter_small_done", bufs_l, wait=last, wait_sems=sems_l, after=o_wd[0])[1][2:]
    (r_wgu0,), (r_wd0,) = scatter_done(rs_wgu0, r_small), scatter_done(rs_wd0, r_small)
    o_wgu = _reduce_adam("adam_gate_up_0", [r_wgu0], wgu_t, m_wgu_t, v_wgu_t, tr=176, layer=0, into=o_wgu)
    o_wgu = [jnp.swapaxes(o, 1, 2) for o in o_wgu]
    o_wd = _reduce_adam("adam_down_0", [r_wd0], ffn_down_w, m_ffn_down_w, v_ffn_down_w, tr=176, layer=0, into=o_wd)
    o_pw = _reduce_adam("adam_pool_w", [r_pw[g] for g in range(4)], pool_w[0], m_pool_w[0], v_pool_w[0], tr=32)
    g_small = _small_reduce(r_small)
    loss = g_small[12, 0]
    g_cols = lax.dynamic_slice(g_small, (0, me * dcol), (16, dcol))
    o_small = _small_adam(
        g_cols[0:8], g_cols[8:11], g_small[11:12],
        (norm_gains.reshape(8, dcol), m_norm_gains.reshape(8, dcol), v_norm_gains.reshape(8, dcol)),
        (conv_w[0], m_conv_w[0], v_conv_w[0]), (pool_scale, m_pool_scale, v_pool_scale))
    d_gain, nm_gain, nv_gain, d_taps, nm_taps, nv_taps, d_scale, nm_scale, nv_scale = o_small

    gshape = norm_gains.shape
    per = lambda k: (
        (g_cols[0:8].reshape(gshape), d_gain.reshape(gshape), nm_gain.reshape(gshape), nv_gain.reshape(gshape))[k],
        o_pw[k][None], (g_small[11:12], d_scale, nm_scale, nv_scale)[k], o_win[k],
        (g_cols[8:11][None], d_taps[None], nm_taps[None], nv_taps[None])[k], o_wout[k], o_wgu[k], o_wd[k])
    return (loss, grad_x[None], *per(0), *per(1), *per(2), *per(3))
```

```python
import collections
import functools

import jax
import jax.numpy as jnp
from jax import lax
from jax.experimental import pallas as pl
from jax.experimental.pallas import tpu as pltpu

N_DEV = 8
RMS_EPS = 1e-6
POOL_WINDOWS = (2, 4, 8, 16)
POOL_HALO = 16
CONV_HALO = 16
ADAM_LR, ADAM_B1, ADAM_B2, ADAM_EPS, ADAM_WD, ADAM_STEP = 0.001, 0.9, 0.999, 1e-08, 0.01, 10

VMEM_LIMIT = 56 * 2**20
MXU_COLUMNS = 256
BF16 = jnp.bfloat16
F32 = jnp.float32
MESH = pl.DeviceIdType.MESH
SIBLING_PAIR_ID = 0


def _params(**kw):
    return pltpu.CompilerParams(vmem_limit_bytes=VMEM_LIMIT, **kw)


def _resident(shape, index_map):
    return pl.BlockSpec(shape, index_map, pipeline_mode=pl.Buffered(1))


def _ordered(body, n_in, after):
    if after is None:
        return functools.partial(body), [], []
    return (lambda *refs: body(*refs[:n_in], *refs[n_in + 1:])), [after], [pl.BlockSpec(memory_space=pl.ANY)]


def _rms_fwd(x, g):
    r = lax.rsqrt(jnp.mean(x * x, axis=-1, keepdims=True) + RMS_EPS)
    return x * r * g


def _rms_bwd(x, g, dy):
    r = lax.rsqrt(jnp.mean(x * x, axis=-1, keepdims=True) + RMS_EPS)
    xhat = x * r
    dg = jnp.sum(dy * xhat, axis=0, keepdims=True)
    t = dy * g
    dx = r * (t - xhat * jnp.mean(t * xhat, axis=-1, keepdims=True))
    return dx, dg


def _sigmoid(x):
    return 0.5 * jnp.tanh(0.5 * x) + 0.5


def _dot(a, b):
    return jnp.dot(a, b, preferred_element_type=F32)


def _dot_nt(a, b):
    return lax.dot_general(a, b, (((1,), (1,)), ((), ())), preferred_element_type=F32)


def _dot_tn(a, b):
    return lax.dot_general(a, b, (((0,), (0,)), ((), ())), preferred_element_type=F32)


def _join_blocks(blocks_hbm, joined_ref, sems):
    n, _, C = blocks_hbm.shape
    copies = [pltpu.make_async_copy(blocks_hbm.at[k], joined_ref.at[:, k * C:(k + 1) * C], sems.at[k]) for k in range(n)]
    for cp in copies:
        cp.start()
    for cp in copies:
        cp.wait()


def _row_inverse_counts(tile, tm):
    pos = (lax.broadcasted_iota(jnp.int32, (tm, 1), 0) + tile * tm + 1).astype(F32)
    return [1.0 / jnp.minimum(pos, float(w)) for w in POOL_WINDOWS]


def _pool_from_ext(ext, a, invs, gw):
    s = ext
    outs = []
    for g, w in enumerate(POOL_WINDOWS):
        s = s[:, (gw if g else 0):]
        s = s + pltpu.roll(s, w // 2, 0)
        outs.append(s[POOL_HALO:, :gw] * invs[g] - a[:, g * gw:(g + 1) * gw])
    return outs


def _pool_fwd(h, pw, scale, g_pre, g_post, *, tm, after=None):
    T, D = h.shape
    gw = D // len(POOL_WINDOWS)
    hb = tm // POOL_HALO

    def body(h_ref, halo_ref, pw_ref, scale_ref, gpre_ref, gpost_ref, out_ref, pooled_ref, mixed_ref, ext_ref):
        i = pl.program_id(0)
        x = h_ref[...]
        a = _rms_fwd(x, gpre_ref[...])
        ah = _rms_fwd(halo_ref[...], gpre_ref[...])
        ext_ref[0:POOL_HALO, :] = jnp.where(i == 0, 0.0, ah)
        ext_ref[POOL_HALO:, :] = a
        pooled = [p.astype(BF16) for p in _pool_from_ext(ext_ref[...], a, _row_inverse_counts(i, tm), gw)]
        pooled_ref[...] = jnp.concatenate(pooled, axis=1)
        mixed = jnp.concatenate([_dot(p, pw_ref[g]) for g, p in enumerate(pooled)], axis=1)
        mixed_ref[...] = mixed.astype(BF16)
        out_ref[...] = x + _rms_fwd(mixed * scale_ref[...], gpost_ref[...])

    vec = _resident((1, D), lambda i: (0, 0))
    fn, xa, xs = _ordered(body, 6, after)
    return pl.pallas_call(
        fn, name="pool_fwd", grid=(T // tm,),
        in_specs=[pl.BlockSpec((tm, D), lambda i: (i, 0)),
                  pl.BlockSpec((POOL_HALO, D), lambda i: (jnp.maximum(i * hb - 1, 0), 0)),
                  _resident(pw.shape, lambda i: (0, 0, 0)), vec, vec, vec] + xs,
        out_specs=[pl.BlockSpec((tm, D), lambda i: (i, 0))] * 3,
        out_shape=[jax.ShapeDtypeStruct((T, D), F32), jax.ShapeDtypeStruct((T, D), BF16), jax.ShapeDtypeStruct((T, D), BF16)],
        scratch_shapes=[pltpu.VMEM((tm + POOL_HALO, D), F32)],
        compiler_params=_params(dimension_semantics=("arbitrary",)),
    )(h, h, pw, scale, g_pre, g_post, *xa)


def _pool_bwd(dh, h, pooled, mixed_pre, pw, scale, g_pre, g_post, *, tm, after=None):
    T, D = h.shape
    gw = D // len(POOL_WINDOWS)
    nt = T // tm
    n_ext = tm + POOL_HALO

    def body(dh_ref, h_ref, pooled_ref, mixed_ref, pw_ref, scale_ref, gpre_ref, gpost_ref,
             dx_ref, dpw_ref, small_ref, ext2_ref, carry_ref, dpw_acc):
        i = pl.program_id(0)
        tile = nt - 1 - i

        @pl.when(i == 0)
        def _():
            small_ref[...] = jnp.zeros_like(small_ref)
            dpw_acc[...] = jnp.zeros_like(dpw_acc)
            carry_ref[...] = jnp.zeros_like(carry_ref)

        x = h_ref[...]
        dout = dh_ref[...]
        invs = _row_inverse_counts(tile, tm)
        pooled = [pooled_ref[:, g * gw:(g + 1) * gw] for g in range(len(POOL_WINDOWS))]
        mixed_pre = mixed_ref[...].astype(F32)
        scale_v = scale_ref[...]
        dmixed, dg_post = _rms_bwd(mixed_pre * scale_v, gpost_ref[...], dout)
        small_ref[1:2, :] += dg_post
        small_ref[2:3, :] += jnp.sum(dmixed * mixed_pre, axis=0, keepdims=True)
        dpre = (dmixed * scale_v).astype(BF16)
        dpooled = []
        for g in range(len(POOL_WINDOWS)):
            dp = dpre[:, g * gw:(g + 1) * gw]
            dpw_acc[g] += _dot_tn(pooled[g], dp)
            dpooled.append(_dot_nt(dp, pw_ref[g]))
        q = jnp.concatenate([d * invs[g] for g, d in enumerate(dpooled)], axis=1)
        ext2_ref[0:tm, :] = q
        ext2_ref[tm:, :] = carry_ref[...]
        carry_ref[...] = q[0:POOL_HALO, :]
        s = ext2_ref[...]
        da = []
        for g, w in enumerate(POOL_WINDOWS):
            s = s[:, (gw if g else 0):]
            s = s + pltpu.roll(s, n_ext - w // 2, 0)
            da.append(s[0:tm, :gw] - dpooled[g])
        dx, dg_pre = _rms_bwd(x, gpre_ref[...], jnp.concatenate(da, axis=1))
        small_ref[0:1, :] += dg_pre
        dx_ref[...] = dout + dx

        @pl.when(i == nt - 1)
        def _():
            dpw_ref[...] = dpw_acc[...].astype(BF16)

    vec = _resident((1, D), lambda i: (0, 0))
    rev = lambda i: (nt - 1 - i, 0)
    fn, xa, xs = _ordered(body, 8, after)
    tile = pl.BlockSpec((tm, D), rev)
    return pl.pallas_call(
        fn, name="pool_bwd", grid=(nt,),
        in_specs=[tile, tile, tile, tile, _resident(pw.shape, lambda i: (0, 0, 0)), vec, vec, vec] + xs,
        out_specs=[tile, pl.BlockSpec(pw.shape, lambda i: (0, 0, 0)), pl.BlockSpec((8, D), lambda i: (0, 0))],
        out_shape=[jax.ShapeDtypeStruct((T, D), F32), jax.ShapeDtypeStruct(pw.shape, BF16),
                   jax.ShapeDtypeStruct((8, D), F32)],
        scratch_shapes=[pltpu.VMEM((n_ext, D), F32), pltpu.VMEM((POOL_HALO, D), F32), pltpu.VMEM(pw.shape, F32)],
        compiler_params=_params(dimension_semantics=("arbitrary",)),
    )(dh, h, pooled, mixed_pre, pw, scale, g_pre, g_post, *xa)


def _ffn_fwd(h, g_pre, g_post, wgu, wd, layer, target, *, tm, after=None):
    T, D = h.shape
    F = wd.shape[0]
    last = target is not None

    def body(*refs):
        if last:
            (h_ref, gpre_ref, gpost_ref, wgu_ref, wd_ref, tgt_ref,
             out_ref, gu_ref, act_ref, ff_ref, loss_ref, dff_ref, post_ref) = refs
        else:
            h_ref, gpre_ref, gpost_ref, wgu_ref, wd_ref, out_ref, gu_ref, act_ref, ff_ref = refs
        x = h_ref[...]
        cb = _rms_fwd(x, gpre_ref[...]).astype(BF16)
        g = _dot_nt(cb, wgu_ref[0:F, :])
        u = _dot_nt(cb, wgu_ref[F:2 * F, :])
        gu_ref[0] = g.astype(BF16)
        gu_ref[1] = u.astype(BF16)
        act = (g * _sigmoid(g) * u).astype(BF16)
        act_ref[...] = act
        acc = _dot(act, wd_ref[...])
        ff_ref[...] = acc.astype(BF16)
        hout = x + _rms_fwd(acc, gpost_ref[...])
        if last:
            diff = hout - tgt_ref[...]
            dout = diff * (1.0 / D)
            out_ref[...] = dout

            @pl.when(pl.program_id(0) == 0)
            def _():
                loss_ref[...] = jnp.zeros_like(loss_ref)
                post_ref[...] = jnp.zeros_like(post_ref)

            loss_ref[...] += jnp.sum(diff * diff) * (0.5 / D)
            dff, dg_post = _rms_bwd(acc, gpost_ref[...], dout)
            dff_ref[...] = dff.astype(BF16)
            post_ref[1:2, :] += dg_post
        else:
            out_ref[...] = hout

    vec = _resident((1, D), lambda i: (0, 0))
    tile = pl.BlockSpec((tm, D), lambda i: (i, 0))
    in_specs = [tile, vec, vec, _resident(wgu.shape, lambda i: (0, 0)), _resident(wd.shape, lambda i: (0, 0))]
    out_specs = [tile, pl.BlockSpec((2, tm, F), lambda i: (0, i, 0)), pl.BlockSpec((tm, F), lambda i: (i, 0)), tile]
    out_shape = [jax.ShapeDtypeStruct((T, D), F32), jax.ShapeDtypeStruct((2, T, F), BF16),
                 jax.ShapeDtypeStruct((T, F), BF16), jax.ShapeDtypeStruct((T, D), BF16)]
    args = [h, g_pre, g_post, wgu, wd]
    if last:
        in_specs.append(tile)
        args.append(target)
        out_specs += [pl.BlockSpec((8, 128), lambda i: (0, 0)), tile, pl.BlockSpec((8, D), lambda i: (0, 0))]
        out_shape += [jax.ShapeDtypeStruct((8, 128), F32), jax.ShapeDtypeStruct((T, D), BF16),
                      jax.ShapeDtypeStruct((8, D), F32)]
    fn, xa, xs = _ordered(body, len(args), after)
    return pl.pallas_call(
        fn, name=f"ffn_fwd_{layer}", grid=(T // tm,), in_specs=in_specs + xs, out_specs=out_specs,
        out_shape=out_shape, compiler_params=_params(dimension_semantics=("arbitrary",)),
    )(*args, *xa)


def _ffn_up(h, g_pre, wgu, *, tm):
    T, D = h.shape
    F = wgu.shape[0] // 2

    def body(h_ref, gpre_ref, wgu_ref, gu_ref, act_ref):
        cb = _rms_fwd(h_ref[...], gpre_ref[...]).astype(BF16)
        g = _dot_nt(cb, wgu_ref[0:F, :])
        u = _dot_nt(cb, wgu_ref[F:2 * F, :])
        gu_ref[0] = g.astype(BF16)
        gu_ref[1] = u.astype(BF16)
        act_ref[...] = (g * _sigmoid(g) * u).astype(BF16)

    return pl.pallas_call(
        body, name="ffn_up_0", grid=(T // tm,),
        in_specs=[pl.BlockSpec((tm, D), lambda i: (i, 0)), _resident((1, D), lambda i: (0, 0)),
                  _resident(wgu.shape, lambda i: (0, 0))],
        out_specs=[pl.BlockSpec((2, tm, F), lambda i: (0, i, 0)), pl.BlockSpec((tm, F), lambda i: (i, 0))],
        out_shape=[jax.ShapeDtypeStruct((2, T, F), BF16), jax.ShapeDtypeStruct((T, F), BF16)],
        compiler_params=_params(dimension_semantics=("arbitrary",)),
    )(h, g_pre, wgu)


def _ffn_down(h, act, g_post, wd, *, tm):
    T, D = h.shape
    F = act.shape[1]

    def body(h_ref, act_ref, gpost_ref, wd_ref, out_ref, ff_ref):
        acc = _dot(act_ref[...], wd_ref[...])
        ff_ref[...] = acc.astype(BF16)
        out_ref[...] = h_ref[...] + _rms_fwd(acc, gpost_ref[...])

    tile = pl.BlockSpec((tm, D), lambda i: (i, 0))
    return pl.pallas_call(
        body, name="ffn_down_0", grid=(T // tm,),
        in_specs=[tile, pl.BlockSpec((tm, F), lambda i: (i, 0)), _resident((1, D), lambda i: (0, 0)),
                  _resident(wd.shape, lambda i: (0, 0))],
        out_specs=[tile, tile],
        out_shape=[jax.ShapeDtypeStruct((T, D), F32), jax.ShapeDtypeStruct((T, D), BF16)],
        compiler_params=_params(dimension_semantics=("arbitrary",)),
    )(h, act, g_post, wd)


def _ffn_bwd(dh, h, ff, gu, g_pre, g_post, wgu, wd, layer, *, tm, post=None, after=None):
    T, D = h.shape
    F = wd.shape[0]
    n_chunks = F // MXU_COLUMNS

    def body(dh_ref, h_ref, ff_ref, gu_ref, gpre_ref, gpost_ref, post_ref, wgu_ref, wd_ref,
             dx_ref, dgu_ref, dff_ref, c_ref, small_ref):
        @pl.when(pl.program_id(0) == 0)
        def _():
            small_ref[...] = post_ref[...]

        dout = dh_ref[...]
        if post is None:
            dff, dg_post = _rms_bwd(ff_ref[...].astype(F32), gpost_ref[...], dout)
            small_ref[1:2, :] += dg_post
            dffb = dff.astype(BF16)
        else:
            dffb = ff_ref[...]
        dff_ref[...] = dffb
        dc = jnp.zeros((tm, D), F32)
        for j in range(n_chunks + 1):
            lo, hi = j * MXU_COLUMNS, (j + 1) * MXU_COLUMNS
            if j < n_chunks:
                dact = _dot_nt(dffb, wd_ref[lo:hi, :])
            if j > 0:
                lo0 = lo - MXU_COLUMNS
                dc = dc + _dot(dgu_ref[0, :, lo0:lo], wgu_ref[lo0:lo, :]) + _dot(dgu_ref[1, :, lo0:lo], wgu_ref[F + lo0:F + lo, :])
            if j < n_chunks:
                g = gu_ref[0, :, lo:hi].astype(F32)
                u = gu_ref[1, :, lo:hi].astype(F32)
                s = _sigmoid(g)
                dgu_ref[0, :, lo:hi] = (dact * u * (s * (1.0 + g * (1.0 - s)))).astype(BF16)
                dgu_ref[1, :, lo:hi] = (dact * (g * s)).astype(BF16)
        x = h_ref[...]
        c_ref[...] = _rms_fwd(x, gpre_ref[...]).astype(BF16)
        dx, dg_pre = _rms_bwd(x, gpre_ref[...], dc)
        small_ref[0:1, :] += dg_pre
        dx_ref[...] = dout + dx

    vec = _resident((1, D), lambda i: (0, 0))
    tile = pl.BlockSpec((tm, D), lambda i: (i, 0))
    blk = pl.BlockSpec((2, tm, F), lambda i: (0, i, 0))
    fn, xa, xs = _ordered(body, 9, after)
    return pl.pallas_call(
        fn, name=f"ffn_bwd_{layer}", grid=(T // tm,),
        in_specs=[tile, tile, tile, blk, vec, vec, _resident((8, D), lambda i: (0, 0)),
                  _resident(wgu.shape, lambda i: (0, 0)), _resident(wd.shape, lambda i: (0, 0))] + xs,
        out_specs=[tile, blk, tile, tile, pl.BlockSpec((8, D), lambda i: (0, 0))],
        out_shape=[jax.ShapeDtypeStruct((T, D), F32), jax.ShapeDtypeStruct((2, T, F), BF16),
                   jax.ShapeDtypeStruct((T, D), BF16), jax.ShapeDtypeStruct((T, D), BF16),
                   jax.ShapeDtypeStruct((8, D), F32)],
        compiler_params=_params(dimension_semantics=("arbitrary",)),
    )(dh, h, ff, gu, g_pre, g_post, jnp.zeros((8, D), F32) if post is None else post, wgu, wd, *xa)


def _conv_fwd(h, g_pre, g_post, win, taps, wout, *, tm, after=None):
    T, D = h.shape
    nblk, cb = win.shape[0], win.shape[2]

    def body(h_ref, gpre_ref, gpost_ref, win_hbm, taps_ref, wout_ref,
             out_ref, proj_ref, y_ref, conv_ref, z_ref, proj_scr, ext_ref, carry_ref, win_ref, win_sems):
        i = pl.program_id(0)

        @pl.when(i == 0)
        def _():
            carry_ref[...] = jnp.zeros_like(carry_ref)
            _join_blocks(win_hbm, win_ref, win_sems)

        x = h_ref[...]
        a = _rms_fwd(x, gpre_ref[...]).astype(BF16)
        proj_scr[...] = _dot(a, win_ref[...])
        proj_ref[...] = proj_scr[...].astype(BF16)
        u = proj_scr[:, D:2 * D] * proj_scr[:, 2 * D:3 * D]
        ext_ref[0:CONV_HALO, :] = carry_ref[...]
        ext_ref[CONV_HALO:, :] = u
        carry_ref[...] = u[tm - CONV_HALO:, :]
        e = ext_ref[...]
        conv = (taps_ref[2:3, :] * u + taps_ref[1:2, :] * pltpu.roll(e, 1, 0)[CONV_HALO:, :]
                + taps_ref[0:1, :] * pltpu.roll(e, 2, 0)[CONV_HALO:, :])
        conv_ref[...] = conv.astype(BF16)
        z = (proj_scr[:, 0:D] * conv).astype(BF16)
        z_ref[...] = z
        y = _dot(z, wout_ref[...])
        y_ref[...] = y.astype(BF16)
        out_ref[...] = x + _rms_fwd(y, gpost_ref[...])

    vec = _resident((1, D), lambda i: (0, 0))
    tile = pl.BlockSpec((tm, D), lambda i: (i, 0))
    fn, xa, xs = _ordered(body, 6, after)
    return pl.pallas_call(
        fn, name="conv_fwd", grid=(T // tm,),
        in_specs=[tile, vec, vec, pl.BlockSpec(memory_space=pl.ANY),
                  _resident(taps.shape, lambda i: (0, 0)), _resident(wout.shape, lambda i: (0, 0))] + xs,
        out_specs=[tile, pl.BlockSpec((tm, 3 * D), lambda i: (i, 0)), tile, tile, tile],
        out_shape=[jax.ShapeDtypeStruct((T, D), F32), jax.ShapeDtypeStruct((T, 3 * D), BF16),
                   jax.ShapeDtypeStruct((T, D), BF16), jax.ShapeDtypeStruct((T, D), BF16),
                   jax.ShapeDtypeStruct((T, D), BF16)],
        scratch_shapes=[pltpu.VMEM((tm, 3 * D), F32), pltpu.VMEM((tm + CONV_HALO, D), F32),
                        pltpu.VMEM((CONV_HALO, D), F32), pltpu.VMEM((D, nblk * cb), BF16),
                        pltpu.SemaphoreType.DMA((nblk,))],
        compiler_params=_params(dimension_semantics=("arbitrary",)),
    )(h, g_pre, g_post, win, taps, wout, *xa)


def _conv_bwd(dh, h, y, proj, conv, g_pre, g_post, win, taps, wout, *, tm, after=None):
    T, D = h.shape
    nblk, cb = win.shape[0], win.shape[2]
    nt = T // tm
    hb = tm // CONV_HALO
    n_ext = tm + CONV_HALO

    def body(dh_ref, h_ref, y_ref, proj_ref, halo_ref, conv_ref, gpre_ref, gpost_ref, win_hbm, taps_ref, wout_ref,
             dx_ref, dproj_ref, a_ref, dy_ref, small_ref, ext_ref, ext2_ref, carry_ref, win_ref, win_sems):
        i = pl.program_id(0)
        tile = nt - 1 - i

        @pl.when(i == 0)
        def _():
            small_ref[...] = jnp.zeros_like(small_ref)
            carry_ref[...] = jnp.zeros_like(carry_ref)
            _join_blocks(win_hbm, win_ref, win_sems)

        dout = dh_ref[...]
        dy, dg_post = _rms_bwd(y_ref[...].astype(F32), gpost_ref[...], dout)
        small_ref[1:2, :] += dg_post
        dyb = dy.astype(BF16)
        dy_ref[...] = dyb
        dz = _dot_nt(dyb, wout_ref[...])
        bgate = proj_ref[:, 0:D].astype(F32)
        cgate = proj_ref[:, D:2 * D].astype(F32)
        v = proj_ref[:, 2 * D:3 * D].astype(F32)
        u = cgate * v
        uh = halo_ref[:, D:2 * D].astype(F32) * halo_ref[:, 2 * D:3 * D].astype(F32)
        ext_ref[0:CONV_HALO, :] = jnp.where(tile == 0, 0.0, uh)
        ext_ref[CONV_HALO:, :] = u
        e = ext_ref[...]
        u1 = pltpu.roll(e, 1, 0)[CONV_HALO:, :]
        u2 = pltpu.roll(e, 2, 0)[CONV_HALO:, :]
        t0, t1, t2 = taps_ref[0:1, :], taps_ref[1:2, :], taps_ref[2:3, :]
        dconv = dz * bgate
        small_ref[2:3, :] += jnp.sum(dconv * u2, axis=0, keepdims=True)
        small_ref[3:4, :] += jnp.sum(dconv * u1, axis=0, keepdims=True)
        small_ref[4:5, :] += jnp.sum(dconv * u, axis=0, keepdims=True)
        ext2_ref[0:tm, :] = dconv
        ext2_ref[tm:, :] = carry_ref[...]
        carry_ref[...] = dconv[0:CONV_HALO, :]
        e2 = ext2_ref[...]
        du = (t2 * dconv + t1 * pltpu.roll(e2, n_ext - 1, 0)[0:tm, :]
              + t0 * pltpu.roll(e2, n_ext - 2, 0)[0:tm, :])
        dproj_ref[:, 0:D] = (dz * conv_ref[...].astype(F32)).astype(BF16)
        dproj_ref[:, D:2 * D] = (du * v).astype(BF16)
        dproj_ref[:, 2 * D:3 * D] = (du * cgate).astype(BF16)
        da = _dot_nt(dproj_ref[...], win_ref[...])
        x = h_ref[...]
        a_ref[...] = _rms_fwd(x, gpre_ref[...]).astype(BF16)
        dx, dg_pre = _rms_bwd(x, gpre_ref[...], da)
        small_ref[0:1, :] += dg_pre
        dx_ref[...] = dout + dx

    vec = _resident((1, D), lambda i: (0, 0))
    rev = lambda i: (nt - 1 - i, 0)
    tile = pl.BlockSpec((tm, D), rev)
    wide = pl.BlockSpec((tm, 3 * D), rev)
    fn, xa, xs = _ordered(body, 11, after)
    return pl.pallas_call(
        fn, name="conv_bwd", grid=(nt,),
        in_specs=[tile, tile, tile, wide,
                  pl.BlockSpec((CONV_HALO, 3 * D), lambda i: (jnp.maximum((nt - 1 - i) * hb - 1, 0), 0)),
                  tile, vec, vec, pl.BlockSpec(memory_space=pl.ANY),
                  _resident(taps.shape, lambda i: (0, 0)), _resident(wout.shape, lambda i: (0, 0))] + xs,
        out_specs=[tile, wide, tile, tile, pl.BlockSpec((8, D), lambda i: (0, 0))],
        out_shape=[jax.ShapeDtypeStruct((T, D), F32), jax.ShapeDtypeStruct((T, 3 * D), BF16),
                   jax.ShapeDtypeStruct((T, D), BF16), jax.ShapeDtypeStruct((T, D), BF16),
                   jax.ShapeDtypeStruct((8, D), F32)],
        scratch_shapes=[pltpu.VMEM((n_ext, D), F32), pltpu.VMEM((n_ext, D), F32),
                        pltpu.VMEM((CONV_HALO, D), F32), pltpu.VMEM((D, nblk * cb), BF16),
                        pltpu.SemaphoreType.DMA((nblk,))],
        compiler_params=_params(dimension_semantics=("arbitrary",)),
    )(dh, h, y, proj, proj, conv, g_pre, g_post, win, taps, wout, *xa)


def _wgrad(name, a, b, a_spec, b_spec, block, n_blocks, *, tk, after=None):
    T = a.shape[-2]
    nk = T // tk
    M, N = block
    m = N_DEV // n_blocks
    R = M // m

    def body(a_ref, b_ref, out_ref, land_hbm, acc_ref, stage_ref, recv_ref, send_sems, recv_sems, land_sem):
        i, k = pl.program_id(0), pl.program_id(1)
        x, y, c = lax.axis_index("x"), lax.axis_index("y"), lax.axis_index("c")

        def sent(blk, p):
            owner = blk * m + p
            q = owner // 2
            return (owner % 2) != c, pltpu.make_async_remote_copy(
                src_ref=stage_ref.at[p * R:(p + 1) * R], dst_ref=recv_ref.at[q], send_sem=send_sems.at[q],
                recv_sem=recv_sems.at[q], device_id=(x, y, 1 - c), device_id_type=MESH)

        @pl.when(jnp.logical_and(i == 0, k == 0))
        def _():
            barrier = pltpu.get_barrier_semaphore()
            pl.semaphore_signal(barrier, inc=1, device_id=(x, y, 1 - c), device_id_type=MESH)
            pl.semaphore_wait(barrier, 1)

        @pl.when(k == 0)
        def _():
            acc_ref[...] = jnp.zeros_like(acc_ref)

        acc_ref[...] += _dot_tn(a_ref[...], b_ref[...])

        @pl.when(k == nk - 1)
        def _():
            for p in range(m):
                away, copy = sent(jnp.maximum(i - 1, 0), p)

                @pl.when(jnp.logical_and(i > 0, away))
                def _():
                    copy.wait_send()

            acc = acc_ref[...]
            stage_ref[...] = acc.astype(BF16)
            for p in range(m):
                away, copy = sent(i, p)

                @pl.when(away)
                def _():
                    copy.start()

                @pl.when(jnp.logical_not(away))
                def _():
                    out_ref[(i * m + p) // 2] = stage_ref[p * R:(p + 1) * R, :]

        @pl.when(jnp.logical_and(i == n_blocks - 1, k == nk - 1))
        def _():
            for p in range(m):
                away, copy = sent(i, p)

                @pl.when(away)
                def _():
                    copy.wait_send()

            for q in range(N_DEV // 2):
                pltpu.make_async_remote_copy(
                    src_ref=stage_ref.at[0:R], dst_ref=recv_ref.at[q], send_sem=send_sems.at[q],
                    recv_sem=recv_sems.at[q], device_id=(x, y, 1 - c), device_id_type=MESH).wait_recv()
                out_ref[q] = (out_ref[q].astype(F32) + recv_ref[q].astype(F32)).astype(BF16)
            mine = pltpu.make_async_copy(out_ref.at[2 * x + y], land_hbm.at[2 * x + y], land_sem)
            mine.start()
            mine.wait()

    fn, xa, xs = _ordered(body, 2, after)
    sums = jax.ShapeDtypeStruct((N_DEV // 2, R, N), BF16)
    return pl.pallas_call(
        fn, name=name, grid=(n_blocks, nk), in_specs=[a_spec, b_spec] + xs,
        out_specs=[pl.BlockSpec((N_DEV // 2, R, N), lambda i, k: (0, 0, 0)), pl.BlockSpec(memory_space=pl.ANY)],
        out_shape=[sums, sums],
        scratch_shapes=[pltpu.VMEM(block, F32), pltpu.VMEM(block, BF16), pltpu.VMEM((N_DEV // 2, R, N), BF16),
                        pltpu.SemaphoreType.DMA((N_DEV // 2,)), pltpu.SemaphoreType.DMA((N_DEV // 2,)),
                        pltpu.SemaphoreType.DMA],
        compiler_params=_params(dimension_semantics=("arbitrary", "arbitrary"), collective_id=SIBLING_PAIR_ID),
    )(a, b, *xa)


Copy = collections.namedtuple("Copy", "mask sb src db dst sem")
Local = collections.namedtuple("Local", "sb src db dst")

HBM_SPEC = pl.BlockSpec(memory_space=pltpu.HBM)
SEM_SPEC = pl.BlockSpec(memory_space=pltpu.SEMAPHORE)
SIBLING, X_PEER, Y_PEER, DIAGONAL = 1, 4, 2, 6
OTHER_CHIPS = (X_PEER, Y_PEER, DIAGONAL)


def _whole(ref, i):
    return ref


def _lead(ref, i):
    return ref.at[i]


def _second(ref, i):
    return ref.at[:, i]


def _place():
    x, y, c = lax.axis_index("x"), lax.axis_index("y"), lax.axis_index("c")
    return (x, y, c), 4 * x + 2 * y + c


def _descriptor(cp, bufs, xyc, me, sender, send_sems, recv_sems):
    x, y, c = xyc
    flip = lambda v, bit: (1 - v) if bit else v
    return pltpu.make_async_remote_copy(
        src_ref=cp.src(bufs[cp.sb], me), dst_ref=cp.dst(bufs[cp.db], sender),
        send_sem=send_sems.at[cp.sem], recv_sem=recv_sems.at[cp.sem],
        device_id=(flip(x, cp.mask & 4), flip(y, cp.mask & 2), flip(c, cp.mask & 1)), device_id_type=MESH)


def _exchange(name, bufs, plan, local=()):
    n = len(bufs)

    def body(*refs):
        ins = refs[:n]
        send_sems, recv_sems, local_sems = refs[2 * n:]
        xyc, me = _place()
        own = [pltpu.make_async_copy(lc.src(ins[lc.sb], me), lc.dst(ins[lc.db], me), local_sems.at[i])
               for i, lc in enumerate(local)]
        sends = [_descriptor(cp, ins, xyc, me, me, send_sems, recv_sems) for cp in plan]
        for cp in own + sends:
            cp.start()
        for cp in plan:
            _descriptor(cp, ins, xyc, me, me ^ cp.mask, send_sems, recv_sems).wait_recv()
        for cp in sends:
            cp.wait_send()
        for cp in own:
            cp.wait()

    return pl.pallas_call(
        body, name=name, in_specs=[HBM_SPEC] * n, out_specs=[HBM_SPEC] * n,
        out_shape=[jax.ShapeDtypeStruct(b.shape, b.dtype) for b in bufs],
        input_output_aliases={i: i for i in range(n)},
        scratch_shapes=[pltpu.SemaphoreType.DMA((len(plan),)), pltpu.SemaphoreType.DMA((len(plan),)),
                        pltpu.SemaphoreType.DMA((max(len(local), 1),))],
    )(*bufs)


def _place_own(me, items):
    def body(me_ref, *refs):
        for src, dst in zip(refs[:len(items)], refs[len(items):]):
            dst[...] = src[...].astype(dst.dtype)

    return pl.pallas_call(
        body, name="place_own",
        grid_spec=pltpu.PrefetchScalarGridSpec(
            num_scalar_prefetch=1, grid=(1,),
            in_specs=[pl.BlockSpec(blk, functools.partial(lambda i, m, idx: idx, idx=idx)) for _, blk, idx, _, _, _, _ in items],
            out_specs=[pl.BlockSpec(oblk, functools.partial(lambda i, m, at: at(m[0]), at=at)) for *_, oblk, at in items]),
        out_shape=[jax.ShapeDtypeStruct(shape, dtype) for _, _, _, shape, dtype, _, _ in items],
        compiler_params=_params(dimension_semantics=("arbitrary",)),
    )(jnp.reshape(me, (1,)).astype(jnp.int32), *[a for a, *_ in items])


def _split_call(name, bufs, *, wait=None, wait_sems=None, start=None, local=(), after=None, token=False):
    n = len(bufs)
    n_wait = 2 if wait else 0
    n_after = 1 if after is not None else 0
    n_start = 2 if start else 0

    def body(*refs):
        ins = refs[:n]
        wsend, wrecv = refs[n:n + n_wait] if wait else (None, None)
        outs = refs[n + n_wait + n_after:]
        ssend, srecv = outs[:n_start] if start else (None, None)
        rest = outs[n_start + n:]
        xyc, me = _place()
        for cp in wait or ():
            d = _descriptor(cp, ins, xyc, me, me ^ cp.mask, wsend, wrecv)
            d.wait_send()
            d.wait_recv()
        own = [pltpu.make_async_copy(lc.src(ins[lc.sb], me), lc.dst(ins[lc.db], me), rest[-1].at[i])
               for i, lc in enumerate(local)]
        for cp in own:
            cp.start()
        for cp in start or ():
            _descriptor(cp, ins, xyc, me, me, ssend, srecv).start()
        for cp in own:
            cp.wait()
        if token:
            rest[0][...] = jnp.zeros_like(rest[0])

    args = [pltpu.with_memory_space_constraint(b, pltpu.HBM) for b in bufs]
    in_specs = [HBM_SPEC] * n
    if wait:
        args += list(wait_sems)
        in_specs += [SEM_SPEC] * 2
    if after is not None:
        args.append(after)
        in_specs.append(pl.BlockSpec(memory_space=pl.ANY))
    out_shape, out_specs = [], []
    if start:
        out_shape += [pltpu.SemaphoreType.DMA((len(start),))] * 2
        out_specs += [SEM_SPEC] * 2
    out_shape += [pltpu.HBM(b.shape, b.dtype) for b in bufs]
    out_specs += [HBM_SPEC] * n
    if token:
        out_shape.append(jax.ShapeDtypeStruct((8, 128), F32))
        out_specs.append(pl.BlockSpec(memory_space=pltpu.VMEM))
    outs = pl.pallas_call(
        body, name=name, in_specs=in_specs, out_specs=out_specs, out_shape=out_shape,
        input_output_aliases={i: n_start + i for i in range(n)},
        scratch_shapes=[pltpu.SemaphoreType.DMA((len(local),))] if local else [],
        compiler_params=pltpu.CompilerParams(has_side_effects=pltpu.SideEffectType.DATAFLOW_SIDE_EFFECTING),
    )(*args)
    sems = tuple(outs[:n_start]) if start else None
    return sems, list(outs[n_start:n_start + n]), (outs[n_start + n] if token else None)


def _adamw(w, g, m, v):
    m = ADAM_B1 * m + (1.0 - ADAM_B1) * g
    v = ADAM_B2 * v + (1.0 - ADAM_B2) * (g * g)
    m_hat = m / (1.0 - ADAM_B1 ** ADAM_STEP)
    v_hat = v / (1.0 - ADAM_B2 ** ADAM_STEP)
    delta = -ADAM_LR * (m_hat / (jnp.sqrt(v_hat) + ADAM_EPS) + ADAM_WD * w)
    return delta, m, v


def _reduce_adam(name, parts, w, m, v, *, tr, layer=None, into=None, after=None, transposed=False):
    L, R, C = w.shape
    S = parts[0].shape[0]
    tr = min(tr, R)
    n_l = L if layer is None else 1
    first = 0 if layer is None else layer

    def body(*refs):
        p_refs = refs[:n_l]
        w_ref, m_ref, v_ref = refs[n_l:n_l + 3]
        g_ref, d_ref, nm_ref, nv_ref = refs[-4:]
        for l in range(n_l):
            g = p_refs[l][0].astype(F32)
            for s in range(1, S):
                g = g + p_refs[l][s].astype(F32)
            g = g.T if transposed else g
            g_ref[l] = g
            d_ref[l], nm_ref[l], nv_ref[l] = _adamw(w_ref[l], g, m_ref[l], v_ref[l])

    blk = pl.BlockSpec((n_l, tr, C), lambda r: (first, r, 0))
    out = jax.ShapeDtypeStruct((L, R, C), F32)
    extra = list(into or []) + ([after] if after is not None else [])
    return pl.pallas_call(
        body, name=name, grid=(R // tr,),
        in_specs=[pl.BlockSpec((S, C, tr), lambda r: (0, 0, r)) if transposed else pl.BlockSpec((S, tr, C), lambda r: (0, r, 0))] * n_l
        + [blk, blk, blk]
        + [pl.BlockSpec(memory_space=pl.ANY)] * len(extra),
        out_specs=[blk] * 4, out_shape=[out] * 4,
        input_output_aliases={n_l + 3 + i: i for i in range(4)} if into else {},
        compiler_params=_params(dimension_semantics=("arbitrary",)),
    )(*parts, w, m, v, *extra)


def _small_update(me, parts, gains, taps, scale):
    dcol = gains[0].shape[1]
    gain_rows, tap_rows, scale_row, loss_row = [0, 1, 8, 9, 16, 17, 24, 25], [18, 19, 20], 2, 32

    def body(me_ref, full_ref, col_ref, wg, mg, vg, wt, mt, vt, ws, ms, vs, *outs):
        col = col_ref[0]
        g_scale = full_ref[0, scale_row:scale_row + 1, :]
        loss = full_ref[0, loss_row:loss_row + 1, 0:128]
        for d in range(1, N_DEV):
            col = col + col_ref[d]
            g_scale = g_scale + full_ref[d, scale_row:scale_row + 1, :]
            loss = loss + full_ref[d, loss_row:loss_row + 1, 0:128]
        for r, src in enumerate(gain_rows):
            outs[0][r:r + 1, :] = col[src:src + 1, :]
        for r, src in enumerate(tap_rows):
            outs[4][r:r + 1, :] = col[src:src + 1, :]
        outs[8][...] = g_scale
        for k, (w, m, v) in enumerate(((wg, mg, vg), (wt, mt, vt), (ws, ms, vs))):
            outs[4 * k + 1][...], outs[4 * k + 2][...], outs[4 * k + 3][...] = _adamw(w[...], outs[4 * k][...], m[...], v[...])
        outs[12][...] = jnp.broadcast_to(loss, outs[12].shape)

    whole = lambda a: pl.BlockSpec(a.shape, lambda i, m: (0,) * a.ndim)
    small = [a for t in (gains, taps, scale) for a in t]
    shapes = [jax.ShapeDtypeStruct(t[0].shape, F32) for t in (gains, taps, scale) for _ in range(4)]
    shapes.append(jax.ShapeDtypeStruct((8, 128), F32))
    return pl.pallas_call(
        body, name="small_update",
        grid_spec=pltpu.PrefetchScalarGridSpec(
            num_scalar_prefetch=1, grid=(1,),
            in_specs=[whole(parts), pl.BlockSpec(parts.shape[:2] + (dcol,), lambda i, m: (0, 0, m[0]))] + [whole(a) for a in small],
            out_specs=[pl.BlockSpec(sh.shape, functools.partial(lambda i, m, n: (0,) * n, n=len(sh.shape))) for sh in shapes]),
        out_shape=shapes, compiler_params=_params(dimension_semantics=("arbitrary",)),
    )(jnp.reshape(me, (1,)).astype(jnp.int32), parts, parts, *small)


def kernel(x, norm_gains, pool_w, pool_scale, conv_in_w, conv_w, conv_out_w, ffn_gate_up_w, ffn_down_w, loss_target, m_norm_gains, m_pool_w, m_pool_scale, m_conv_in_w, m_conv_w, m_conv_out_w, m_ffn_gate_up_w, m_ffn_down_w, v_norm_gains, v_pool_w, v_pool_scale, v_conv_in_w, v_conv_w, v_conv_out_w, v_ffn_gate_up_w, v_ffn_down_w):
    T, D = x.shape[1], x.shape[2]
    tm = min(512, T)
    tm_b = min(256, T)
    tk = min(2048, T)
    n_layers = ffn_gate_up_w.shape[0]
    fb = ffn_gate_up_w.shape[2]
    fr = ffn_down_w.shape[1]
    dcol = norm_gains.shape[2]
    cb = conv_in_w.shape[2]
    gw = pool_w.shape[3]
    me = 4 * lax.axis_index("x") + 2 * lax.axis_index("y") + lax.axis_index("c")

    small_w = jnp.concatenate([norm_gains.reshape(8, dcol), jnp.pad(conv_w[0], ((0, 5), (0, 0)))], axis=0)
    every = range(1, N_DEV)
    wgu_t, m_wgu_t, v_wgu_t = (jnp.swapaxes(a, 1, 2) for a in (ffn_gate_up_w, m_ffn_gate_up_w, v_ffn_gate_up_w))
    own_lead = lambda s: lax.dynamic_update_slice(lax.empty((N_DEV,) + s.shape, s.dtype), s[None], (me,) + (0,) * s.ndim)
    lead_item = lambda a, l, dtype: (a, (None,) + a.shape[1:], (l, 0, 0), (N_DEV,) + a.shape[1:], dtype,
                                     (None,) + a.shape[1:], lambda i: (i, 0, 0))
    lands = _place_own(me, [
        (pool_w, (None,) + pool_w.shape[1:], (0, 0, 0, 0), (4, N_DEV, gw // N_DEV, gw), BF16,
         (4, None, gw // N_DEV, gw), lambda i: (0, i, 0, 0)),
        (small_w[None], (None,) + small_w.shape, (0, 0, 0), (N_DEV,) + small_w.shape, F32, (None,) + small_w.shape, lambda i: (i, 0, 0)),
        lead_item(wgu_t, 0, BF16), lead_item(ffn_down_w, 0, BF16), lead_item(conv_in_w, 0, BF16),
        lead_item(conv_out_w, 0, BF16), lead_item(wgu_t, 1, BF16), lead_item(ffn_down_w, 1, BF16)])
    n_first, n_big = 2, len(lands) - 2
    direct = ([Copy(m, 0, _second, 0, _second, m - 1) for m in every]
              + [Copy(m, 1, _lead, 1, _lead, N_DEV - 2 + m) for m in every])
    level1 = [Copy(mask, n_first + n, _lead, n_first + n, _lead, len(direct) + 4 * n + j)
              for n in range(n_big) for j, mask in enumerate((SIBLING,) + OTHER_CHIPS)]
    sems1, bufs1, _ = _split_call("gather_start", lands, start=direct + level1)
    pw_g, small_g = _split_call("gather_small_done", bufs1[:n_first], wait=direct, wait_sems=sems1)[1]
    pw = pw_g.reshape(4, gw, gw)
    small_full = jnp.swapaxes(small_g, 0, 1).reshape(16, D)
    gain = lambda l, s: small_full[4 * l + s][None, :]
    taps = small_full[8:16]

    def forward_on(name, group, after):
        k = len(group)
        landed = [Copy(cp.mask, i, cp.src, i, cp.dst, cp.sem)
                  for i, n in enumerate(group) for cp in level1 if cp.sb == n_first + n]
        onward = [Copy(SIBLING, i, (lambda ref, me, m=m: ref.at[me ^ m]), i, (lambda ref, sender, m=m: ref.at[sender ^ m]), 3 * i + j)
                  for i in range(k) for j, m in enumerate(OTHER_CHIPS)]
        sems2, bufs2, tok = _split_call(name + "_forward", [bufs1[n_first + n] for n in group], wait=landed,
                                        wait_sems=sems1, start=onward, after=after, token=True)
        return (name, onward, sems2, bufs2), tok

    def arrived(state, after=None):
        name, onward, sems2, lands2 = state
        return _split_call(name + "_done", lands2, wait=onward, wait_sems=sems2, after=after)[1]

    h0 = x[0]
    h1, pooled, mixed_pre = _pool_fwd(h0, pw, pool_scale, gain(0, 0), gain(0, 1), tm=tm)
    (wgu0,) = arrived(forward_on("gather_gate_up_0", [0], h1)[0])
    wgu0 = wgu0.reshape(N_DEV * fb, D)
    gu0, act0 = _ffn_up(h1, gain(0, 2), wgu0, tm=tm)
    ag_down0, tok = forward_on("gather_down_0", [1], act0)
    ag_conv, tok = forward_on("gather_conv", [2, 3], tok)
    (wd0,) = arrived(ag_down0, tok)
    wd0 = wd0.reshape(N_DEV * fr, D)
    h2, ff0 = _ffn_down(h1, act0, gain(0, 3), wd0, tm=tm)
    win_g, wout_g = arrived(ag_conv, h2)
    wout = wout_g.reshape(D, D)
    h3, proj, y, conv, z = _conv_fwd(h2, gain(1, 0), gain(1, 1), win_g, taps, wout, tm=tm)
    ag_ffn1, tok = forward_on("gather_ffn1", [4, 5], h3)
    wgu1, wd1 = arrived(ag_ffn1, tok)
    wgu1, wd1 = wgu1.reshape(N_DEV * fb, D), wd1.reshape(N_DEV * fr, D)
    dh4, gu1, act1, _, loss_part, dff1, post1 = _ffn_fwd(h3, gain(1, 2), gain(1, 3), wgu1, wd1, 1, loss_target[0], tm=tm)


    def scatter_start(name, pairs):
        k = len(pairs)
        sums, lands = [p[0] for p in pairs], [p[1] for p in pairs]
        plan = [Copy(m, n, (lambda ref, i, m=m: ref.at[(i ^ m) >> 1]), k + n, (lambda ref, i: ref.at[i >> 1]), 3 * n + j)
                for n in range(k) for j, m in enumerate(OTHER_CHIPS)]
        sems, bufs, tok = _split_call(name + "_start", sums + lands, start=plan, token=True)
        return (name, plan, sems, bufs), tok

    def scatter_done(state, after):
        name, plan, sems, bufs = state
        return _split_call(name + "_done", bufs, wait=plan, wait_sems=sems, after=after)[1][len(bufs) // 2:]

    seq = lambda i, k: (k, 0)
    gu_pair = pl.BlockSpec((None, tk, 2 * fb), lambda i, k: (i // 2, k, i % 2))
    act_pair = pl.BlockSpec((tk, 2 * fb), lambda i, k: (k, i))
    rows = pl.BlockSpec((tk, D), seq)
    dh3, dgu1, dff1, c1, small_f1 = _ffn_bwd(dh4, h3, dff1, gu1, gain(1, 2), gain(1, 3), wgu1, wd1, 1, tm=tm_b, post=post1)
    g_wgu1 = _wgrad("wgrad_gate_up_1", dgu1, c1, gu_pair, rows, (2 * fb, D), N_DEV // 2, tk=tk)
    g_wd1 = _wgrad("wgrad_down_1", act1, dff1, act_pair, rows, (2 * fb, D), N_DEV // 4, tk=tk)
    rs_ffn1, tok = scatter_start("scatter_ffn1", [g_wgu1, g_wd1])
    dh2, dproj, a1, dy, small_c = _conv_bwd(dh3, h2, y, proj, conv, gain(1, 0), gain(1, 1), win_g, taps, wout, tm=tm, after=tok)
    g_win = _wgrad("wgrad_conv_in", dproj, a1, pl.BlockSpec((tk, 2 * cb), lambda i, k: (k, i)), rows, (2 * cb, D), N_DEV // 2,
                   tk=tk)
    g_wout = _wgrad("wgrad_conv_out", z, dy, rows, rows, (D, D), 1, tk=tk)
    rs_conv, tok = scatter_start("scatter_conv", [g_win, g_wout])
    dh1, dgu0, dff0, c0, small_f0 = _ffn_bwd(dh2, h1, ff0, gu0, gain(0, 2), gain(0, 3), wgu0, wd0, 0, tm=tm_b, after=tok)
    g_wgu0 = _wgrad("wgrad_gate_up_0", dgu0, c0, gu_pair, rows, (2 * fb, D), N_DEV // 2, tk=tk)
    rs_wgu0, tok = scatter_start("scatter_gate_up_0", [g_wgu0])
    g_wd0 = _wgrad("wgrad_down_0", act0, dff0, act_pair, rows, (2 * fb, D), N_DEV // 4, tk=tk, after=tok)
    rs_wd0, tok = scatter_start("scatter_down_0", [g_wd0])
    grad_x, g_pw, small_p = _pool_bwd(dh1, h0, pooled, mixed_pre, pw, pool_scale, gain(0, 0), gain(0, 1), tm=tm, after=tok)

    loss_rows = jnp.broadcast_to(loss_part[0:1, 0:1], (8, D))
    small_part = jnp.concatenate([small_p, small_f0, small_c, small_f1, loss_rows], axis=0)
    g_pw = g_pw.reshape(4, N_DEV, gw // N_DEV, gw)
    pw_land = lax.dynamic_update_slice(lax.empty(g_pw.shape, BF16), lax.dynamic_slice_in_dim(g_pw, me, 1, 1), (0, me, 0, 0))
    last = ([Copy(m, 0, (lambda ref, i, m=m: ref.at[:, i ^ m]), 2, _second, m - 1) for m in every]
            + [Copy(m, 1, _whole, 3, _lead, N_DEV - 2 + m) for m in every])
    sems_l, bufs_l, tok = _split_call("scatter_small_start", [g_pw, small_part, pw_land, own_lead(small_part)],
                                      start=last, token=True)

    (r_wgu1, r_wd1), (r_win, r_wout) = scatter_done(rs_ffn1, tok), scatter_done(rs_conv, tok)
    o_win = _reduce_adam("adam_conv_in", [r_win], conv_in_w, m_conv_in_w, v_conv_in_w, tr=256, transposed=True)
    o_wout = _reduce_adam("adam_conv_out", [r_wout], conv_out_w, m_conv_out_w, v_conv_out_w, tr=128, after=o_win[0])
    o_wgu = _reduce_adam("adam_gate_up_1", [r_wgu1], wgu_t, m_wgu_t, v_wgu_t, tr=176, layer=1, after=o_wout[0])
    o_wd = _reduce_adam("adam_down_1", [r_wd1], ffn_down_w, m_ffn_down_w, v_ffn_down_w, tr=176, layer=1, after=o_wgu[0])
    r_pw, r_small = _split_call("scatter_small_done", bufs_l, wait=last, wait_sems=sems_l, after=o_wd[0])[1][2:]
    (r_wgu0,), (r_wd0,) = scatter_done(rs_wgu0, r_small), scatter_done(rs_wd0, r_small)
    o_wgu = _reduce_adam("adam_gate_up_0", [r_wgu0], wgu_t, m_wgu_t, v_wgu_t, tr=176, layer=0, into=o_wgu)
    o_wgu = [jnp.swapaxes(o, 1, 2) for o in o_wgu]
    o_wd = _reduce_adam("adam_down_0", [r_wd0], ffn_down_w, m_ffn_down_w, v_ffn_down_w, tr=176, layer=0, into=o_wd)
    o_pw = _reduce_adam("adam_pool_w", [r_pw[g] for g in range(4)], pool_w[0], m_pool_w[0], v_pool_w[0], tr=32)
    o_small = _small_update(
        me, r_small,
        (norm_gains.reshape(8, dcol), m_norm_gains.reshape(8, dcol), v_norm_gains.reshape(8, dcol)),
        (conv_w[0], m_conv_w[0], v_conv_w[0]), (pool_scale, m_pool_scale, v_pool_scale))
    o_gain, o_taps, o_scale, loss = o_small[0:4], o_small[4:8], o_small[8:12], o_small[12][0, 0]

    gshape = norm_gains.shape
    per = lambda k: (o_gain[k].reshape(gshape), o_pw[k][None], o_scale[k], o_win[k], o_taps[k][None],
                     o_wout[k], o_wgu[k], o_wd[k])
    return (loss, grad_x[None], *per(0), *per(1), *per(2), *per(3))
```

```python
import collections
import functools

import jax
import jax.numpy as jnp
from jax import lax
from jax.experimental import pallas as pl
from jax.experimental.pallas import tpu as pltpu

N_DEV = 8
RMS_EPS = 1e-6
POOL_WINDOWS = (2, 4, 8, 16)
POOL_HALO = 16
CONV_HALO = 16
ADAM_LR, ADAM_B1, ADAM_B2, ADAM_EPS, ADAM_WD, ADAM_STEP = 0.001, 0.9, 0.999, 1e-08, 0.01, 10

VMEM_LIMIT = 56 * 2**20
MXU_COLUMNS = 256
BF16 = jnp.bfloat16
F32 = jnp.float32
MESH = pl.DeviceIdType.MESH
SIBLING_PAIR_ID = 0


def _params(**kw):
    return pltpu.CompilerParams(vmem_limit_bytes=VMEM_LIMIT, **kw)


def _resident(shape, index_map):
    return pl.BlockSpec(shape, index_map, pipeline_mode=pl.Buffered(1))


def _ordered(body, n_in, after):
    if after is None:
        return functools.partial(body), [], []
    return (lambda *refs: body(*refs[:n_in], *refs[n_in + 1:])), [after], [pl.BlockSpec(memory_space=pl.ANY)]


def _rms_fwd(x, g):
    r = lax.rsqrt(jnp.mean(x * x, axis=-1, keepdims=True) + RMS_EPS)
    return x * r * g


def _rms_bwd(x, g, dy):
    r = lax.rsqrt(jnp.mean(x * x, axis=-1, keepdims=True) + RMS_EPS)
    xhat = x * r
    dg = jnp.sum(dy * xhat, axis=0, keepdims=True)
    t = dy * g
    dx = r * (t - xhat * jnp.mean(t * xhat, axis=-1, keepdims=True))
    return dx, dg


def _sigmoid(x):
    return 0.5 * jnp.tanh(0.5 * x) + 0.5


def _dot(a, b):
    return jnp.dot(a, b, preferred_element_type=F32)


def _dot_nt(a, b):
    return lax.dot_general(a, b, (((1,), (1,)), ((), ())), preferred_element_type=F32)


def _dot_tn(a, b):
    return lax.dot_general(a, b, (((0,), (0,)), ((), ())), preferred_element_type=F32)


def _join_blocks(blocks_hbm, joined_ref, sems):
    n, _, C = blocks_hbm.shape
    copies = [pltpu.make_async_copy(blocks_hbm.at[k], joined_ref.at[:, k * C:(k + 1) * C], sems.at[k]) for k in range(n)]
    for cp in copies:
        cp.start()
    for cp in copies:
        cp.wait()


def _row_inverse_counts(tile, tm):
    pos = (lax.broadcasted_iota(jnp.int32, (tm, 1), 0) + tile * tm + 1).astype(F32)
    return [1.0 / jnp.minimum(pos, float(w)) for w in POOL_WINDOWS]


def _pool_from_ext(ext, a, invs, gw):
    s = ext
    outs = []
    for g, w in enumerate(POOL_WINDOWS):
        s = s[:, (gw if g else 0):]
        s = s + pltpu.roll(s, w // 2, 0)
        outs.append(s[POOL_HALO:, :gw] * invs[g] - a[:, g * gw:(g + 1) * gw])
    return outs


def _pool_fwd(h, pw, scale, g_pre, g_post, *, tm, after=None):
    T, D = h.shape
    gw = D // len(POOL_WINDOWS)
    hb = tm // POOL_HALO

    def body(h_ref, halo_ref, pw_ref, scale_ref, gpre_ref, gpost_ref, out_ref, pooled_ref, mixed_ref, ext_ref):
        i = pl.program_id(0)
        x = h_ref[...]
        a = _rms_fwd(x, gpre_ref[...])
        ah = _rms_fwd(halo_ref[...], gpre_ref[...])
        ext_ref[0:POOL_HALO, :] = jnp.where(i == 0, 0.0, ah)
        ext_ref[POOL_HALO:, :] = a
        pooled = [p.astype(BF16) for p in _pool_from_ext(ext_ref[...], a, _row_inverse_counts(i, tm), gw)]
        pooled_ref[...] = jnp.concatenate(pooled, axis=1)
        mixed = jnp.concatenate([_dot(p, pw_ref[g]) for g, p in enumerate(pooled)], axis=1)
        mixed_ref[...] = mixed.astype(BF16)
        out_ref[...] = x + _rms_fwd(mixed * scale_ref[...], gpost_ref[...])

    vec = _resident((1, D), lambda i: (0, 0))
    fn, xa, xs = _ordered(body, 6, after)
    return pl.pallas_call(
        fn, name="pool_fwd", grid=(T // tm,),
        in_specs=[pl.BlockSpec((tm, D), lambda i: (i, 0)),
                  pl.BlockSpec((POOL_HALO, D), lambda i: (jnp.maximum(i * hb - 1, 0), 0)),
                  _resident(pw.shape, lambda i: (0, 0, 0)), vec, vec, vec] + xs,
        out_specs=[pl.BlockSpec((tm, D), lambda i: (i, 0))] * 3,
        out_shape=[jax.ShapeDtypeStruct((T, D), F32), jax.ShapeDtypeStruct((T, D), BF16), jax.ShapeDtypeStruct((T, D), BF16)],
        scratch_shapes=[pltpu.VMEM((tm + POOL_HALO, D), F32)],
        compiler_params=_params(dimension_semantics=("arbitrary",)),
    )(h, h, pw, scale, g_pre, g_post, *xa)


def _pool_bwd(dh, h, pooled, mixed_pre, pw, scale, g_pre, g_post, *, tm, after=None):
    T, D = h.shape
    gw = D // len(POOL_WINDOWS)
    nt = T // tm
    n_ext = tm + POOL_HALO

    def body(dh_ref, h_ref, pooled_ref, mixed_ref, pw_ref, scale_ref, gpre_ref, gpost_ref,
             dx_ref, dpw_ref, small_ref, ext2_ref, carry_ref, dpw_acc):
        i = pl.program_id(0)
        tile = nt - 1 - i

        @pl.when(i == 0)
        def _():
            small_ref[...] = jnp.zeros_like(small_ref)
            dpw_acc[...] = jnp.zeros_like(dpw_acc)
            carry_ref[...] = jnp.zeros_like(carry_ref)

        x = h_ref[...]
        dout = dh_ref[...]
        invs = _row_inverse_counts(tile, tm)
        pooled = [pooled_ref[:, g * gw:(g + 1) * gw] for g in range(len(POOL_WINDOWS))]
        mixed_pre = mixed_ref[...].astype(F32)
        scale_v = scale_ref[...]
        dmixed, dg_post = _rms_bwd(mixed_pre * scale_v, gpost_ref[...], dout)
        small_ref[1:2, :] += dg_post
        small_ref[2:3, :] += jnp.sum(dmixed * mixed_pre, axis=0, keepdims=True)
        dpre = (dmixed * scale_v).astype(BF16)
        dpooled = []
        for g in range(len(POOL_WINDOWS)):
            dp = dpre[:, g * gw:(g + 1) * gw]
            dpw_acc[g] += _dot_tn(pooled[g], dp)
            dpooled.append(_dot_nt(dp, pw_ref[g]))
        q = jnp.concatenate([d * invs[g] for g, d in enumerate(dpooled)], axis=1)
        ext2_ref[0:tm, :] = q
        ext2_ref[tm:, :] = carry_ref[...]
        carry_ref[...] = q[0:POOL_HALO, :]
        s = ext2_ref[...]
        da = []
        for g, w in enumerate(POOL_WINDOWS):
            s = s[:, (gw if g else 0):]
            s = s + pltpu.roll(s, n_ext - w // 2, 0)
            da.append(s[0:tm, :gw] - dpooled[g])
        dx, dg_pre = _rms_bwd(x, gpre_ref[...], jnp.concatenate(da, axis=1))
        small_ref[0:1, :] += dg_pre
        dx_ref[...] = dout + dx

        @pl.when(i == nt - 1)
        def _():
            dpw_ref[...] = dpw_acc[...].astype(BF16)

    vec = _resident((1, D), lambda i: (0, 0))
    rev = lambda i: (nt - 1 - i, 0)
    fn, xa, xs = _ordered(body, 8, after)
    tile = pl.BlockSpec((tm, D), rev)
    return pl.pallas_call(
        fn, name="pool_bwd", grid=(nt,),
        in_specs=[tile, tile, tile, tile, _resident(pw.shape, lambda i: (0, 0, 0)), vec, vec, vec] + xs,
        out_specs=[tile, pl.BlockSpec(pw.shape, lambda i: (0, 0, 0)), pl.BlockSpec((8, D), lambda i: (0, 0))],
        out_shape=[jax.ShapeDtypeStruct((T, D), F32), jax.ShapeDtypeStruct(pw.shape, BF16),
                   jax.ShapeDtypeStruct((8, D), F32)],
        scratch_shapes=[pltpu.VMEM((n_ext, D), F32), pltpu.VMEM((POOL_HALO, D), F32), pltpu.VMEM(pw.shape, F32)],
        compiler_params=_params(dimension_semantics=("arbitrary",)),
    )(dh, h, pooled, mixed_pre, pw, scale, g_pre, g_post, *xa)


def _ffn_fwd(h, g_pre, g_post, wgu, wd, layer, target, *, tm, after=None):
    T, D = h.shape
    F = wd.shape[0]
    last = target is not None

    def body(*refs):
        if last:
            h_ref, gpre_ref, gpost_ref, wgu_ref, wd_ref, tgt_ref, out_ref, gu_ref, act_ref, ff_ref, loss_ref = refs
        else:
            h_ref, gpre_ref, gpost_ref, wgu_ref, wd_ref, out_ref, gu_ref, act_ref, ff_ref = refs
        x = h_ref[...]
        cb = _rms_fwd(x, gpre_ref[...]).astype(BF16)
        g = _dot_nt(cb, wgu_ref[0:F, :])
        u = _dot_nt(cb, wgu_ref[F:2 * F, :])
        gu_ref[0] = g.astype(BF16)
        gu_ref[1] = u.astype(BF16)
        act = (g * _sigmoid(g) * u).astype(BF16)
        act_ref[...] = act
        acc = _dot(act, wd_ref[...])
        ff_ref[...] = acc.astype(BF16)
        hout = x + _rms_fwd(acc, gpost_ref[...])
        if last:
            diff = hout - tgt_ref[...]
            out_ref[...] = diff * (1.0 / D)

            @pl.when(pl.program_id(0) == 0)
            def _():
                loss_ref[...] = jnp.zeros_like(loss_ref)

            loss_ref[...] += jnp.sum(diff * diff) * (0.5 / D)
        else:
            out_ref[...] = hout

    vec = _resident((1, D), lambda i: (0, 0))
    tile = pl.BlockSpec((tm, D), lambda i: (i, 0))
    in_specs = [tile, vec, vec, _resident(wgu.shape, lambda i: (0, 0)), _resident(wd.shape, lambda i: (0, 0))]
    out_specs = [tile, pl.BlockSpec((2, tm, F), lambda i: (0, i, 0)), pl.BlockSpec((tm, F), lambda i: (i, 0)), tile]
    out_shape = [jax.ShapeDtypeStruct((T, D), F32), jax.ShapeDtypeStruct((2, T, F), BF16),
                 jax.ShapeDtypeStruct((T, F), BF16), jax.ShapeDtypeStruct((T, D), BF16)]
    args = [h, g_pre, g_post, wgu, wd]
    if last:
        in_specs.append(tile)
        args.append(target)
        out_specs.append(pl.BlockSpec((8, 128), lambda i: (0, 0)))
        out_shape.append(jax.ShapeDtypeStruct((8, 128), F32))
    fn, xa, xs = _ordered(body, len(args), after)
    return pl.pallas_call(
        fn, name=f"ffn_fwd_{layer}", grid=(T // tm,), in_specs=in_specs + xs, out_specs=out_specs,
        out_shape=out_shape, compiler_params=_params(dimension_semantics=("arbitrary",)),
    )(*args, *xa)


def _ffn_up(h, g_pre, wgu, *, tm):
    T, D = h.shape
    F = wgu.shape[0] // 2

    def body(h_ref, gpre_ref, wgu_ref, gu_ref, act_ref):
        cb = _rms_fwd(h_ref[...], gpre_ref[...]).astype(BF16)
        g = _dot_nt(cb, wgu_ref[0:F, :])
        u = _dot_nt(cb, wgu_ref[F:2 * F, :])
        gu_ref[0] = g.astype(BF16)
        gu_ref[1] = u.astype(BF16)
        act_ref[...] = (g * _sigmoid(g) * u).astype(BF16)

    return pl.pallas_call(
        body, name="ffn_up_0", grid=(T // tm,),
        in_specs=[pl.BlockSpec((tm, D), lambda i: (i, 0)), _resident((1, D), lambda i: (0, 0)),
                  _resident(wgu.shape, lambda i: (0, 0))],
        out_specs=[pl.BlockSpec((2, tm, F), lambda i: (0, i, 0)), pl.BlockSpec((tm, F), lambda i: (i, 0))],
        out_shape=[jax.ShapeDtypeStruct((2, T, F), BF16), jax.ShapeDtypeStruct((T, F), BF16)],
        compiler_params=_params(dimension_semantics=("arbitrary",)),
    )(h, g_pre, wgu)


def _ffn_down(h, act, g_post, wd, *, tm):
    T, D = h.shape
    F = act.shape[1]

    def body(h_ref, act_ref, gpost_ref, wd_ref, out_ref, ff_ref):
        acc = _dot(act_ref[...], wd_ref[...])
        ff_ref[...] = acc.astype(BF16)
        out_ref[...] = h_ref[...] + _rms_fwd(acc, gpost_ref[...])

    tile = pl.BlockSpec((tm, D), lambda i: (i, 0))
    return pl.pallas_call(
        body, name="ffn_down_0", grid=(T // tm,),
        in_specs=[tile, pl.BlockSpec((tm, F), lambda i: (i, 0)), _resident((1, D), lambda i: (0, 0)),
                  _resident(wd.shape, lambda i: (0, 0))],
        out_specs=[tile, tile],
        out_shape=[jax.ShapeDtypeStruct((T, D), F32), jax.ShapeDtypeStruct((T, D), BF16)],
        compiler_params=_params(dimension_semantics=("arbitrary",)),
    )(h, act, g_post, wd)


def _ffn_bwd(dh, h, ff, gu, g_pre, g_post, wgu, wd, layer, *, tm, after=None):
    T, D = h.shape
    F = wd.shape[0]
    n_chunks = F // MXU_COLUMNS

    def body(dh_ref, h_ref, ff_ref, gu_ref, gpre_ref, gpost_ref, wgu_ref, wd_ref,
             dx_ref, dgu_ref, dff_ref, c_ref, small_ref):
        @pl.when(pl.program_id(0) == 0)
        def _():
            small_ref[...] = jnp.zeros_like(small_ref)

        dout = dh_ref[...]
        dff, dg_post = _rms_bwd(ff_ref[...].astype(F32), gpost_ref[...], dout)
        small_ref[1:2, :] += dg_post
        dffb = dff.astype(BF16)
        dff_ref[...] = dffb
        dc = jnp.zeros((tm, D), F32)
        for j in range(n_chunks + 1):
            lo, hi = j * MXU_COLUMNS, (j + 1) * MXU_COLUMNS
            if j < n_chunks:
                dact = _dot_nt(dffb, wd_ref[lo:hi, :])
            if j > 0:
                lo0 = lo - MXU_COLUMNS
                dc = dc + _dot(dgu_ref[0, :, lo0:lo], wgu_ref[lo0:lo, :]) + _dot(dgu_ref[1, :, lo0:lo], wgu_ref[F + lo0:F + lo, :])
            if j < n_chunks:
                g = gu_ref[0, :, lo:hi].astype(F32)
                u = gu_ref[1, :, lo:hi].astype(F32)
                s = _sigmoid(g)
                dgu_ref[0, :, lo:hi] = (dact * u * (s * (1.0 + g * (1.0 - s)))).astype(BF16)
                dgu_ref[1, :, lo:hi] = (dact * (g * s)).astype(BF16)
        x = h_ref[...]
        c_ref[...] = _rms_fwd(x, gpre_ref[...]).astype(BF16)
        dx, dg_pre = _rms_bwd(x, gpre_ref[...], dc)
        small_ref[0:1, :] += dg_pre
        dx_ref[...] = dout + dx

    vec = _resident((1, D), lambda i: (0, 0))
    tile = pl.BlockSpec((tm, D), lambda i: (i, 0))
    blk = pl.BlockSpec((2, tm, F), lambda i: (0, i, 0))
    fn, xa, xs = _ordered(body, 8, after)
    return pl.pallas_call(
        fn, name=f"ffn_bwd_{layer}", grid=(T // tm,),
        in_specs=[tile, tile, tile, blk, vec, vec,
                  _resident(wgu.shape, lambda i: (0, 0)), _resident(wd.shape, lambda i: (0, 0))] + xs,
        out_specs=[tile, blk, tile, tile, pl.BlockSpec((8, D), lambda i: (0, 0))],
        out_shape=[jax.ShapeDtypeStruct((T, D), F32), jax.ShapeDtypeStruct((2, T, F), BF16),
                   jax.ShapeDtypeStruct((T, D), BF16), jax.ShapeDtypeStruct((T, D), BF16),
                   jax.ShapeDtypeStruct((8, D), F32)],
        compiler_params=_params(dimension_semantics=("arbitrary",)),
    )(dh, h, ff, gu, g_pre, g_post, wgu, wd, *xa)


def _conv_fwd(h, g_pre, g_post, win, taps, wout, *, tm, after=None):
    T, D = h.shape
    nblk, cb = win.shape[0], win.shape[2]

    def body(h_ref, gpre_ref, gpost_ref, win_hbm, taps_ref, wout_ref,
             out_ref, proj_ref, y_ref, conv_ref, z_ref, proj_scr, ext_ref, carry_ref, win_ref, win_sems):
        i = pl.program_id(0)

        @pl.when(i == 0)
        def _():
            carry_ref[...] = jnp.zeros_like(carry_ref)
            _join_blocks(win_hbm, win_ref, win_sems)

        x = h_ref[...]
        a = _rms_fwd(x, gpre_ref[...]).astype(BF16)
        proj_scr[...] = _dot(a, win_ref[...])
        proj_ref[...] = proj_scr[...].astype(BF16)
        u = proj_scr[:, D:2 * D] * proj_scr[:, 2 * D:3 * D]
        ext_ref[0:CONV_HALO, :] = carry_ref[...]
        ext_ref[CONV_HALO:, :] = u
        carry_ref[...] = u[tm - CONV_HALO:, :]
        e = ext_ref[...]
        conv = (taps_ref[2:3, :] * u + taps_ref[1:2, :] * pltpu.roll(e, 1, 0)[CONV_HALO:, :]
                + taps_ref[0:1, :] * pltpu.roll(e, 2, 0)[CONV_HALO:, :])
        conv_ref[...] = conv.astype(BF16)
        z = (proj_scr[:, 0:D] * conv).astype(BF16)
        z_ref[...] = z
        y = _dot(z, wout_ref[...])
        y_ref[...] = y.astype(BF16)
        out_ref[...] = x + _rms_fwd(y, gpost_ref[...])

    vec = _resident((1, D), lambda i: (0, 0))
    tile = pl.BlockSpec((tm, D), lambda i: (i, 0))
    fn, xa, xs = _ordered(body, 6, after)
    return pl.pallas_call(
        fn, name="conv_fwd", grid=(T // tm,),
        in_specs=[tile, vec, vec, pl.BlockSpec(memory_space=pl.ANY),
                  _resident(taps.shape, lambda i: (0, 0)), _resident(wout.shape, lambda i: (0, 0))] + xs,
        out_specs=[tile, pl.BlockSpec((tm, 3 * D), lambda i: (i, 0)), tile, tile, tile],
        out_shape=[jax.ShapeDtypeStruct((T, D), F32), jax.ShapeDtypeStruct((T, 3 * D), BF16),
                   jax.ShapeDtypeStruct((T, D), BF16), jax.ShapeDtypeStruct((T, D), BF16),
                   jax.ShapeDtypeStruct((T, D), BF16)],
        scratch_shapes=[pltpu.VMEM((tm, 3 * D), F32), pltpu.VMEM((tm + CONV_HALO, D), F32),
                        pltpu.VMEM((CONV_HALO, D), F32), pltpu.VMEM((D, nblk * cb), BF16),
                        pltpu.SemaphoreType.DMA((nblk,))],
        compiler_params=_params(dimension_semantics=("arbitrary",)),
    )(h, g_pre, g_post, win, taps, wout, *xa)


def _conv_bwd(dh, h, y, proj, conv, g_pre, g_post, win, taps, wout, *, tm, after=None):
    T, D = h.shape
    nblk, cb = win.shape[0], win.shape[2]
    nt = T // tm
    hb = tm // CONV_HALO
    n_ext = tm + CONV_HALO

    def body(dh_ref, h_ref, y_ref, proj_ref, halo_ref, conv_ref, gpre_ref, gpost_ref, win_hbm, taps_ref, wout_ref,
             dx_ref, dproj_ref, a_ref, dy_ref, small_ref, ext_ref, ext2_ref, carry_ref, win_ref, win_sems):
        i = pl.program_id(0)
        tile = nt - 1 - i

        @pl.when(i == 0)
        def _():
            small_ref[...] = jnp.zeros_like(small_ref)
            carry_ref[...] = jnp.zeros_like(carry_ref)
            _join_blocks(win_hbm, win_ref, win_sems)

        dout = dh_ref[...]
        dy, dg_post = _rms_bwd(y_ref[...].astype(F32), gpost_ref[...], dout)
        small_ref[1:2, :] += dg_post
        dyb = dy.astype(BF16)
        dy_ref[...] = dyb
        dz = _dot_nt(dyb, wout_ref[...])
        bgate = proj_ref[:, 0:D].astype(F32)
        cgate = proj_ref[:, D:2 * D].astype(F32)
        v = proj_ref[:, 2 * D:3 * D].astype(F32)
        u = cgate * v
        uh = halo_ref[:, D:2 * D].astype(F32) * halo_ref[:, 2 * D:3 * D].astype(F32)
        ext_ref[0:CONV_HALO, :] = jnp.where(tile == 0, 0.0, uh)
        ext_ref[CONV_HALO:, :] = u
        e = ext_ref[...]
        u1 = pltpu.roll(e, 1, 0)[CONV_HALO:, :]
        u2 = pltpu.roll(e, 2, 0)[CONV_HALO:, :]
        t0, t1, t2 = taps_ref[0:1, :], taps_ref[1:2, :], taps_ref[2:3, :]
        dconv = dz * bgate
        small_ref[2:3, :] += jnp.sum(dconv * u2, axis=0, keepdims=True)
        small_ref[3:4, :] += jnp.sum(dconv * u1, axis=0, keepdims=True)
        small_ref[4:5, :] += jnp.sum(dconv * u, axis=0, keepdims=True)
        ext2_ref[0:tm, :] = dconv
        ext2_ref[tm:, :] = carry_ref[...]
        carry_ref[...] = dconv[0:CONV_HALO, :]
        e2 = ext2_ref[...]
        du = (t2 * dconv + t1 * pltpu.roll(e2, n_ext - 1, 0)[0:tm, :]
              + t0 * pltpu.roll(e2, n_ext - 2, 0)[0:tm, :])
        dproj_ref[:, 0:D] = (dz * conv_ref[...].astype(F32)).astype(BF16)
        dproj_ref[:, D:2 * D] = (du * v).astype(BF16)
        dproj_ref[:, 2 * D:3 * D] = (du * cgate).astype(BF16)
        da = _dot_nt(dproj_ref[...], win_ref[...])
        x = h_ref[...]
        a_ref[...] = _rms_fwd(x, gpre_ref[...]).astype(BF16)
        dx, dg_pre = _rms_bwd(x, gpre_ref[...], da)
        small_ref[0:1, :] += dg_pre
        dx_ref[...] = dout + dx

    vec = _resident((1, D), lambda i: (0, 0))
    rev = lambda i: (nt - 1 - i, 0)
    tile = pl.BlockSpec((tm, D), rev)
    wide = pl.BlockSpec((tm, 3 * D), rev)
    fn, xa, xs = _ordered(body, 11, after)
    return pl.pallas_call(
        fn, name="conv_bwd", grid=(nt,),
        in_specs=[tile, tile, tile, wide,
                  pl.BlockSpec((CONV_HALO, 3 * D), lambda i: (jnp.maximum((nt - 1 - i) * hb - 1, 0), 0)),
                  tile, vec, vec, pl.BlockSpec(memory_space=pl.ANY),
                  _resident(taps.shape, lambda i: (0, 0)), _resident(wout.shape, lambda i: (0, 0))] + xs,
        out_specs=[tile, wide, tile, tile, pl.BlockSpec((8, D), lambda i: (0, 0))],
        out_shape=[jax.ShapeDtypeStruct((T, D), F32), jax.ShapeDtypeStruct((T, 3 * D), BF16),
                   jax.ShapeDtypeStruct((T, D), BF16), jax.ShapeDtypeStruct((T, D), BF16),
                   jax.ShapeDtypeStruct((8, D), F32)],
        scratch_shapes=[pltpu.VMEM((n_ext, D), F32), pltpu.VMEM((n_ext, D), F32),
                        pltpu.VMEM((CONV_HALO, D), F32), pltpu.VMEM((D, nblk * cb), BF16),
                        pltpu.SemaphoreType.DMA((nblk,))],
        compiler_params=_params(dimension_semantics=("arbitrary",)),
    )(dh, h, y, proj, proj, conv, g_pre, g_post, win, taps, wout, *xa)


def _wgrad(name, a, b, a_spec, b_spec, block, n_blocks, *, tk, after=None):
    T = a.shape[-2]
    nk = T // tk
    M, N = block
    m = N_DEV // n_blocks
    R = M // m

    def body(a_ref, b_ref, out_ref, land_hbm, acc_ref, stage_ref, recv_ref, send_sems, recv_sems, land_sem):
        i, k = pl.program_id(0), pl.program_id(1)
        x, y, c = lax.axis_index("x"), lax.axis_index("y"), lax.axis_index("c")

        def sent(blk, p):
            owner = blk * m + p
            q = owner // 2
            return (owner % 2) != c, pltpu.make_async_remote_copy(
                src_ref=stage_ref.at[p * R:(p + 1) * R], dst_ref=recv_ref.at[q], send_sem=send_sems.at[q],
                recv_sem=recv_sems.at[q], device_id=(x, y, 1 - c), device_id_type=MESH)

        @pl.when(jnp.logical_and(i == 0, k == 0))
        def _():
            barrier = pltpu.get_barrier_semaphore()
            pl.semaphore_signal(barrier, inc=1, device_id=(x, y, 1 - c), device_id_type=MESH)
            pl.semaphore_wait(barrier, 1)

        @pl.when(k == 0)
        def _():
            acc_ref[...] = jnp.zeros_like(acc_ref)

        acc_ref[...] += _dot_tn(a_ref[...], b_ref[...])

        @pl.when(k == nk - 1)
        def _():
            for p in range(m):
                away, copy = sent(jnp.maximum(i - 1, 0), p)

                @pl.when(jnp.logical_and(i > 0, away))
                def _():
                    copy.wait_send()

            acc = acc_ref[...]
            stage_ref[...] = acc.astype(BF16)
            for p in range(m):
                away, copy = sent(i, p)

                @pl.when(away)
                def _():
                    copy.start()

                @pl.when(jnp.logical_not(away))
                def _():
                    out_ref[(i * m + p) // 2] = stage_ref[p * R:(p + 1) * R, :]

        @pl.when(jnp.logical_and(i == n_blocks - 1, k == nk - 1))
        def _():
            for p in range(m):
                away, copy = sent(i, p)

                @pl.when(away)
                def _():
                    copy.wait_send()

            for q in range(N_DEV // 2):
                pltpu.make_async_remote_copy(
                    src_ref=stage_ref.at[0:R], dst_ref=recv_ref.at[q], send_sem=send_sems.at[q],
                    recv_sem=recv_sems.at[q], device_id=(x, y, 1 - c), device_id_type=MESH).wait_recv()
                out_ref[q] = (out_ref[q].astype(F32) + recv_ref[q].astype(F32)).astype(BF16)
            mine = pltpu.make_async_copy(out_ref.at[2 * x + y], land_hbm.at[2 * x + y], land_sem)
            mine.start()
            mine.wait()

    fn, xa, xs = _ordered(body, 2, after)
    sums = jax.ShapeDtypeStruct((N_DEV // 2, R, N), BF16)
    return pl.pallas_call(
        fn, name=name, grid=(n_blocks, nk), in_specs=[a_spec, b_spec] + xs,
        out_specs=[pl.BlockSpec((N_DEV // 2, R, N), lambda i, k: (0, 0, 0)), pl.BlockSpec(memory_space=pl.ANY)],
        out_shape=[sums, sums],
        scratch_shapes=[pltpu.VMEM(block, F32), pltpu.VMEM(block, BF16), pltpu.VMEM((N_DEV // 2, R, N), BF16),
                        pltpu.SemaphoreType.DMA((N_DEV // 2,)), pltpu.SemaphoreType.DMA((N_DEV // 2,)),
                        pltpu.SemaphoreType.DMA],
        compiler_params=_params(dimension_semantics=("arbitrary", "arbitrary"), collective_id=SIBLING_PAIR_ID),
    )(a, b, *xa)


Copy = collections.namedtuple("Copy", "mask sb src db dst sem")
Local = collections.namedtuple("Local", "sb src db dst")

HBM_SPEC = pl.BlockSpec(memory_space=pltpu.HBM)
SEM_SPEC = pl.BlockSpec(memory_space=pltpu.SEMAPHORE)
SIBLING, X_PEER, Y_PEER, DIAGONAL = 1, 4, 2, 6
OTHER_CHIPS = (X_PEER, Y_PEER, DIAGONAL)


def _whole(ref, i):
    return ref


def _lead(ref, i):
    return ref.at[i]


def _second(ref, i):
    return ref.at[:, i]


def _place():
    x, y, c = lax.axis_index("x"), lax.axis_index("y"), lax.axis_index("c")
    return (x, y, c), 4 * x + 2 * y + c


def _descriptor(cp, bufs, xyc, me, sender, send_sems, recv_sems):
    x, y, c = xyc
    flip = lambda v, bit: (1 - v) if bit else v
    return pltpu.make_async_remote_copy(
        src_ref=cp.src(bufs[cp.sb], me), dst_ref=cp.dst(bufs[cp.db], sender),
        send_sem=send_sems.at[cp.sem], recv_sem=recv_sems.at[cp.sem],
        device_id=(flip(x, cp.mask & 4), flip(y, cp.mask & 2), flip(c, cp.mask & 1)), device_id_type=MESH)


def _exchange(name, bufs, plan, local=()):
    n = len(bufs)

    def body(*refs):
        ins = refs[:n]
        send_sems, recv_sems, local_sems = refs[2 * n:]
        xyc, me = _place()
        own = [pltpu.make_async_copy(lc.src(ins[lc.sb], me), lc.dst(ins[lc.db], me), local_sems.at[i])
               for i, lc in enumerate(local)]
        sends = [_descriptor(cp, ins, xyc, me, me, send_sems, recv_sems) for cp in plan]
        for cp in own + sends:
            cp.start()
        for cp in plan:
            _descriptor(cp, ins, xyc, me, me ^ cp.mask, send_sems, recv_sems).wait_recv()
        for cp in sends:
            cp.wait_send()
        for cp in own:
            cp.wait()

    return pl.pallas_call(
        body, name=name, in_specs=[HBM_SPEC] * n, out_specs=[HBM_SPEC] * n,
        out_shape=[jax.ShapeDtypeStruct(b.shape, b.dtype) for b in bufs],
        input_output_aliases={i: i for i in range(n)},
        scratch_shapes=[pltpu.SemaphoreType.DMA((len(plan),)), pltpu.SemaphoreType.DMA((len(plan),)),
                        pltpu.SemaphoreType.DMA((max(len(local), 1),))],
    )(*bufs)


def _place_own(me, items):
    def body(me_ref, *refs):
        for src, dst in zip(refs[:len(items)], refs[len(items):]):
            dst[...] = src[...].astype(dst.dtype)

    return pl.pallas_call(
        body, name="place_own",
        grid_spec=pltpu.PrefetchScalarGridSpec(
            num_scalar_prefetch=1, grid=(1,),
            in_specs=[pl.BlockSpec(blk, functools.partial(lambda i, m, idx: idx, idx=idx)) for _, blk, idx, _, _, _, _ in items],
            out_specs=[pl.BlockSpec(oblk, functools.partial(lambda i, m, at: at(m[0]), at=at)) for *_, oblk, at in items]),
        out_shape=[jax.ShapeDtypeStruct(shape, dtype) for _, _, _, shape, dtype, _, _ in items],
        compiler_params=_params(dimension_semantics=("arbitrary",)),
    )(jnp.reshape(me, (1,)).astype(jnp.int32), *[a for a, *_ in items])


def _split_call(name, bufs, *, wait=None, wait_sems=None, start=None, local=(), after=None, token=False):
    n = len(bufs)
    n_wait = 2 if wait else 0
    n_after = 1 if after is not None else 0
    n_start = 2 if start else 0

    def body(*refs):
        ins = refs[:n]
        wsend, wrecv = refs[n:n + n_wait] if wait else (None, None)
        outs = refs[n + n_wait + n_after:]
        ssend, srecv = outs[:n_start] if start else (None, None)
        rest = outs[n_start + n:]
        xyc, me = _place()
        for cp in wait or ():
            d = _descriptor(cp, ins, xyc, me, me ^ cp.mask, wsend, wrecv)
            d.wait_send()
            d.wait_recv()
        own = [pltpu.make_async_copy(lc.src(ins[lc.sb], me), lc.dst(ins[lc.db], me), rest[-1].at[i])
               for i, lc in enumerate(local)]
        for cp in own:
            cp.start()
        for cp in start or ():
            _descriptor(cp, ins, xyc, me, me, ssend, srecv).start()
        for cp in own:
            cp.wait()
        if token:
            rest[0][...] = jnp.zeros_like(rest[0])

    args = [pltpu.with_memory_space_constraint(b, pltpu.HBM) for b in bufs]
    in_specs = [HBM_SPEC] * n
    if wait:
        args += list(wait_sems)
        in_specs += [SEM_SPEC] * 2
    if after is not None:
        args.append(after)
        in_specs.append(pl.BlockSpec(memory_space=pl.ANY))
    out_shape, out_specs = [], []
    if start:
        out_shape += [pltpu.SemaphoreType.DMA((len(start),))] * 2
        out_specs += [SEM_SPEC] * 2
    out_shape += [pltpu.HBM(b.shape, b.dtype) for b in bufs]
    out_specs += [HBM_SPEC] * n
    if token:
        out_shape.append(jax.ShapeDtypeStruct((8, 128), F32))
        out_specs.append(pl.BlockSpec(memory_space=pltpu.VMEM))
    outs = pl.pallas_call(
        body, name=name, in_specs=in_specs, out_specs=out_specs, out_shape=out_shape,
        input_output_aliases={i: n_start + i for i in range(n)},
        scratch_shapes=[pltpu.SemaphoreType.DMA((len(local),))] if local else [],
        compiler_params=pltpu.CompilerParams(has_side_effects=pltpu.SideEffectType.DATAFLOW_SIDE_EFFECTING),
    )(*args)
    sems = tuple(outs[:n_start]) if start else None
    return sems, list(outs[n_start:n_start + n]), (outs[n_start + n] if token else None)


def _adamw(w, g, m, v):
    m = ADAM_B1 * m + (1.0 - ADAM_B1) * g
    v = ADAM_B2 * v + (1.0 - ADAM_B2) * (g * g)
    m_hat = m / (1.0 - ADAM_B1 ** ADAM_STEP)
    v_hat = v / (1.0 - ADAM_B2 ** ADAM_STEP)
    delta = -ADAM_LR * (m_hat / (jnp.sqrt(v_hat) + ADAM_EPS) + ADAM_WD * w)
    return delta, m, v


def _reduce_adam(name, parts, w, m, v, *, tr, layer=None, into=None, after=None, transposed=False):
    L, R, C = w.shape
    S = parts[0].shape[0]
    tr = min(tr, R)
    n_l = L if layer is None else 1
    first = 0 if layer is None else layer

    def body(*refs):
        p_refs = refs[:n_l]
        w_ref, m_ref, v_ref = refs[n_l:n_l + 3]
        g_ref, d_ref, nm_ref, nv_ref = refs[-4:]
        for l in range(n_l):
            g = p_refs[l][0].astype(F32)
            for s in range(1, S):
                g = g + p_refs[l][s].astype(F32)
            g = g.T if transposed else g
            g_ref[l] = g
            d_ref[l], nm_ref[l], nv_ref[l] = _adamw(w_ref[l], g, m_ref[l], v_ref[l])

    blk = pl.BlockSpec((n_l, tr, C), lambda r: (first, r, 0))
    out = jax.ShapeDtypeStruct((L, R, C), F32)
    extra = list(into or []) + ([after] if after is not None else [])
    return pl.pallas_call(
        body, name=name, grid=(R // tr,),
        in_specs=[pl.BlockSpec((S, C, tr), lambda r: (0, 0, r)) if transposed else pl.BlockSpec((S, tr, C), lambda r: (0, r, 0))] * n_l
        + [blk, blk, blk]
        + [pl.BlockSpec(memory_space=pl.ANY)] * len(extra),
        out_specs=[blk] * 4, out_shape=[out] * 4,
        input_output_aliases={n_l + 3 + i: i for i in range(4)} if into else {},
        compiler_params=_params(dimension_semantics=("arbitrary",)),
    )(*parts, w, m, v, *extra)


def _small_update(me, parts, gains, taps, scale):
    dcol = gains[0].shape[1]
    gain_rows, tap_rows, scale_row, loss_row = [0, 1, 8, 9, 16, 17, 24, 25], [18, 19, 20], 2, 32

    def body(me_ref, full_ref, col_ref, wg, mg, vg, wt, mt, vt, ws, ms, vs, *outs):
        col = col_ref[0]
        g_scale = full_ref[0, scale_row:scale_row + 1, :]
        loss = full_ref[0, loss_row:loss_row + 1, 0:128]
        for d in range(1, N_DEV):
            col = col + col_ref[d]
            g_scale = g_scale + full_ref[d, scale_row:scale_row + 1, :]
            loss = loss + full_ref[d, loss_row:loss_row + 1, 0:128]
        for r, src in enumerate(gain_rows):
            outs[0][r:r + 1, :] = col[src:src + 1, :]
        for r, src in enumerate(tap_rows):
            outs[4][r] = col[src:src + 1, :]
        outs[8][...] = g_scale
        for k, (w, m, v) in enumerate(((wg, mg, vg), (wt, mt, vt), (ws, ms, vs))):
            outs[4 * k + 1][...], outs[4 * k + 2][...], outs[4 * k + 3][...] = _adamw(w[...], outs[4 * k][...], m[...], v[...])
        outs[12][...] = jnp.broadcast_to(loss, outs[12].shape)

    whole = lambda a: pl.BlockSpec(a.shape, lambda i, m: (0,) * a.ndim)
    small = [a for t in (gains, taps, scale) for a in t]
    shapes = [jax.ShapeDtypeStruct(t[0].shape, F32) for t in (gains, taps, scale) for _ in range(4)]
    shapes.append(jax.ShapeDtypeStruct((8, 128), F32))
    return pl.pallas_call(
        body, name="small_update",
        grid_spec=pltpu.PrefetchScalarGridSpec(
            num_scalar_prefetch=1, grid=(1,),
            in_specs=[whole(parts), pl.BlockSpec(parts.shape[:2] + (dcol,), lambda i, m: (0, 0, m[0]))] + [whole(a) for a in small],
            out_specs=[pl.BlockSpec(sh.shape, functools.partial(lambda i, m, n: (0,) * n, n=len(sh.shape))) for sh in shapes]),
        out_shape=shapes, compiler_params=_params(dimension_semantics=("arbitrary",)),
    )(jnp.reshape(me, (1,)).astype(jnp.int32), parts, parts, *small)


def kernel(x, norm_gains, pool_w, pool_scale, conv_in_w, conv_w, conv_out_w, ffn_gate_up_w, ffn_down_w, loss_target, m_norm_gains, m_pool_w, m_pool_scale, m_conv_in_w, m_conv_w, m_conv_out_w, m_ffn_gate_up_w, m_ffn_down_w, v_norm_gains, v_pool_w, v_pool_scale, v_conv_in_w, v_conv_w, v_conv_out_w, v_ffn_gate_up_w, v_ffn_down_w):
    T, D = x.shape[1], x.shape[2]
    tm = min(512, T)
    tm_b = min(256, T)
    tk = min(2048, T)
    n_layers = ffn_gate_up_w.shape[0]
    fb = ffn_gate_up_w.shape[2]
    fr = ffn_down_w.shape[1]
    dcol = norm_gains.shape[2]
    cb = conv_in_w.shape[2]
    gw = pool_w.shape[3]
    me = 4 * lax.axis_index("x") + 2 * lax.axis_index("y") + lax.axis_index("c")

    small_w = jnp.concatenate([norm_gains.reshape(8, dcol), jnp.pad(conv_w[0], ((0, 5), (0, 0)))], axis=0)
    every = range(1, N_DEV)
    wgu_t, m_wgu_t, v_wgu_t = (jnp.swapaxes(a, 1, 2) for a in (ffn_gate_up_w, m_ffn_gate_up_w, v_ffn_gate_up_w))
    own_lead = lambda s: lax.dynamic_update_slice(lax.empty((N_DEV,) + s.shape, s.dtype), s[None], (me,) + (0,) * s.ndim)
    lead_item = lambda a, l, dtype: (a, (None,) + a.shape[1:], (l, 0, 0), (N_DEV,) + a.shape[1:], dtype,
                                     (None,) + a.shape[1:], lambda i: (i, 0, 0))
    lands = _place_own(me, [
        (pool_w, (None,) + pool_w.shape[1:], (0, 0, 0, 0), (4, N_DEV, gw // N_DEV, gw), BF16,
         (4, None, gw // N_DEV, gw), lambda i: (0, i, 0, 0)),
        (small_w[None], (None,) + small_w.shape, (0, 0, 0), (N_DEV,) + small_w.shape, F32, (None,) + small_w.shape, lambda i: (i, 0, 0)),
        lead_item(wgu_t, 0, BF16), lead_item(ffn_down_w, 0, BF16), lead_item(conv_in_w, 0, BF16),
        lead_item(conv_out_w, 0, BF16), lead_item(wgu_t, 1, BF16), lead_item(ffn_down_w, 1, BF16)])
    n_first, n_big = 2, len(lands) - 2
    direct = ([Copy(m, 0, _second, 0, _second, m - 1) for m in every]
              + [Copy(m, 1, _lead, 1, _lead, N_DEV - 2 + m) for m in every])
    level1 = [Copy(mask, n_first + n, _lead, n_first + n, _lead, len(direct) + 4 * n + j)
              for n in range(n_big) for j, mask in enumerate((SIBLING,) + OTHER_CHIPS)]
    sems1, bufs1, _ = _split_call("gather_start", lands, start=direct + level1)
    pw_g, small_g = _split_call("gather_small_done", bufs1[:n_first], wait=direct, wait_sems=sems1)[1]
    pw = pw_g.reshape(4, gw, gw)
    small_full = jnp.swapaxes(small_g, 0, 1).reshape(16, D)
    gain = lambda l, s: small_full[4 * l + s][None, :]
    taps = small_full[8:16]

    def forward_on(name, group, after):
        k = len(group)
        landed = [Copy(cp.mask, i, cp.src, i, cp.dst, cp.sem)
                  for i, n in enumerate(group) for cp in level1 if cp.sb == n_first + n]
        onward = [Copy(SIBLING, i, (lambda ref, me, m=m: ref.at[me ^ m]), i, (lambda ref, sender, m=m: ref.at[sender ^ m]), 3 * i + j)
                  for i in range(k) for j, m in enumerate(OTHER_CHIPS)]
        sems2, bufs2, tok = _split_call(name + "_forward", [bufs1[n_first + n] for n in group], wait=landed,
                                        wait_sems=sems1, start=onward, after=after, token=True)
        return (name, onward, sems2, bufs2), tok

    def arrived(state, after=None):
        name, onward, sems2, lands2 = state
        return _split_call(name + "_done", lands2, wait=onward, wait_sems=sems2, after=after)[1]

    h0 = x[0]
    h1, pooled, mixed_pre = _pool_fwd(h0, pw, pool_scale, gain(0, 0), gain(0, 1), tm=tm)
    (wgu0,) = arrived(forward_on("gather_gate_up_0", [0], h1)[0])
    wgu0 = wgu0.reshape(N_DEV * fb, D)
    gu0, act0 = _ffn_up(h1, gain(0, 2), wgu0, tm=tm)
    ag_down0, tok = forward_on("gather_down_0", [1], act0)
    ag_conv, tok = forward_on("gather_conv", [2, 3], tok)
    (wd0,) = arrived(ag_down0, tok)
    wd0 = wd0.reshape(N_DEV * fr, D)
    h2, ff0 = _ffn_down(h1, act0, gain(0, 3), wd0, tm=tm)
    win_g, wout_g = arrived(ag_conv, h2)
    wout = wout_g.reshape(D, D)
    h3, proj, y, conv, z = _conv_fwd(h2, gain(1, 0), gain(1, 1), win_g, taps, wout, tm=tm)
    ag_ffn1, tok = forward_on("gather_ffn1", [4, 5], h3)
    wgu1, wd1 = arrived(ag_ffn1, tok)
    wgu1, wd1 = wgu1.reshape(N_DEV * fb, D), wd1.reshape(N_DEV * fr, D)
    dh4, gu1, act1, ff1, loss_part = _ffn_fwd(h3, gain(1, 2), gain(1, 3), wgu1, wd1, 1, loss_target[0], tm=tm)


    def scatter_start(name, pairs):
        k = len(pairs)
        sums, lands = [p[0] for p in pairs], [p[1] for p in pairs]
        plan = [Copy(m, n, (lambda ref, i, m=m: ref.at[(i ^ m) >> 1]), k + n, (lambda ref, i: ref.at[i >> 1]), 3 * n + j)
                for n in range(k) for j, m in enumerate(OTHER_CHIPS)]
        sems, bufs, tok = _split_call(name + "_start", sums + lands, start=plan, token=True)
        return (name, plan, sems, bufs), tok

    def scatter_done(state, after):
        name, plan, sems, bufs = state
        return _split_call(name + "_done", bufs, wait=plan, wait_sems=sems, after=after)[1][len(bufs) // 2:]

    seq = lambda i, k: (k, 0)
    gu_pair = pl.BlockSpec((None, tk, 2 * fb), lambda i, k: (i // 2, k, i % 2))
    act_pair = pl.BlockSpec((tk, 2 * fb), lambda i, k: (k, i))
    rows = pl.BlockSpec((tk, D), seq)
    dh3, dgu1, dff1, c1, small_f1 = _ffn_bwd(dh4, h3, ff1, gu1, gain(1, 2), gain(1, 3), wgu1, wd1, 1, tm=tm_b)
    g_wgu1 = _wgrad("wgrad_gate_up_1", dgu1, c1, gu_pair, rows, (2 * fb, D), N_DEV // 2, tk=tk)
    g_wd1 = _wgrad("wgrad_down_1", act1, dff1, act_pair, rows, (2 * fb, D), N_DEV // 4, tk=tk)
    rs_ffn1, tok = scatter_start("scatter_ffn1", [g_wgu1, g_wd1])
    dh2, dproj, a1, dy, small_c = _conv_bwd(dh3, h2, y, proj, conv, gain(1, 0), gain(1, 1), win_g, taps, wout, tm=tm, after=tok)
    g_win = _wgrad("wgrad_conv_in", dproj, a1, pl.BlockSpec((tk, 2 * cb), lambda i, k: (k, i)), rows, (2 * cb, D), N_DEV // 2,
                   tk=tk)
    g_wout = _wgrad("wgrad_conv_out", z, dy, rows, rows, (D, D), 1, tk=tk)
    rs_conv, tok = scatter_start("scatter_conv", [g_win, g_wout])
    dh1, dgu0, dff0, c0, small_f0 = _ffn_bwd(dh2, h1, ff0, gu0, gain(0, 2), gain(0, 3), wgu0, wd0, 0, tm=tm_b, after=tok)
    g_wgu0 = _wgrad("wgrad_gate_up_0", dgu0, c0, gu_pair, rows, (2 * fb, D), N_DEV // 2, tk=tk)
    rs_wgu0, tok = scatter_start("scatter_gate_up_0", [g_wgu0])
    g_wd0 = _wgrad("wgrad_down_0", act0, dff0, act_pair, rows, (2 * fb, D), N_DEV // 4, tk=tk, after=tok)
    rs_wd0, tok = scatter_start("scatter_down_0", [g_wd0])
    grad_x, g_pw, small_p = _pool_bwd(dh1, h0, pooled, mixed_pre, pw, pool_scale, gain(0, 0), gain(0, 1), tm=tm, after=tok)

    loss_rows = jnp.broadcast_to(loss_part[0:1, 0:1], (8, D))
    small_part = jnp.concatenate([small_p, small_f0, small_c, small_f1, loss_rows], axis=0)
    g_pw = g_pw.reshape(4, N_DEV, gw // N_DEV, gw)
    pw_land = lax.dynamic_update_slice(lax.empty(g_pw.shape, BF16), lax.dynamic_slice_in_dim(g_pw, me, 1, 1), (0, me, 0, 0))
    last = ([Copy(m, 0, (lambda ref, i, m=m: ref.at[:, i ^ m]), 2, _second, m - 1) for m in every]
            + [Copy(m, 1, _whole, 3, _lead, N_DEV - 2 + m) for m in every])
    sems_l, bufs_l, tok = _split_call("scatter_small_start", [g_pw, small_part, pw_land, own_lead(small_part)],
                                      start=last, token=True)

    (r_wgu1, r_wd1), (r_win, r_wout) = scatter_done(rs_ffn1, tok), scatter_done(rs_conv, tok)
    o_win = _reduce_adam("adam_conv_in", [r_win], conv_in_w, m_conv_in_w, v_conv_in_w, tr=256, transposed=True)
    o_wout = _reduce_adam("adam_conv_out", [r_wout], conv_out_w, m_conv_out_w, v_conv_out_w, tr=128, after=o_win[0])
    o_wgu = _reduce_adam("adam_gate_up_1", [r_wgu1], wgu_t, m_wgu_t, v_wgu_t, tr=176, layer=1, after=o_wout[0])
    o_wd = _reduce_adam("adam_down_1", [r_wd1], ffn_down_w, m_ffn_down_w, v_ffn_down_w, tr=176, layer=1, after=o_wgu[0])
    r_pw, r_small = _split_call("scatter_small_done", bufs_l, wait=last, wait_sems=sems_l, after=o_wd[0])[1][2:]
    (r_wgu0,), (r_wd0,) = scatter_done(rs_wgu0, r_small), scatter_done(rs_wd0, r_small)
    o_wgu = _reduce_adam("adam_gate_up_0", [r_wgu0], wgu_t, m_wgu_t, v_wgu_t, tr=176, layer=0, into=o_wgu)
    o_wgu = [jnp.swapaxes(o, 1, 2) for o in o_wgu]
    o_wd = _reduce_adam("adam_down_0", [r_wd0], ffn_down_w, m_ffn_down_w, v_ffn_down_w, tr=176, layer=0, into=o_wd)
    o_pw = _reduce_adam("adam_pool_w", [r_pw[g] for g in range(4)], pool_w[0], m_pool_w[0], v_pool_w[0], tr=32)
    o_small = _small_update(
        me, r_small,
        (norm_gains.reshape(8, dcol), m_norm_gains.reshape(8, dcol), v_norm_gains.reshape(8, dcol)),
        tuple(jnp.swapaxes(a, 0, 1) for a in (conv_w, m_conv_w, v_conv_w)), (pool_scale, m_pool_scale, v_pool_scale))
    o_gain, o_taps, o_scale, loss = o_small[0:4], o_small[4:8], o_small[8:12], o_small[12][0, 0]

    gshape = norm_gains.shape
    per = lambda k: (o_gain[k].reshape(gshape), o_pw[k][None], o_scale[k], o_win[k], jnp.swapaxes(o_taps[k], 0, 1),
                     o_wout[k], o_wgu[k], o_wd[k])
    return (loss, grad_x[None], *per(0), *per(1), *per(2), *per(3))
```

```python
import collections
import functools

import jax
import jax.numpy as jnp
from jax import lax
from jax.experimental import pallas as pl
from jax.experimental.pallas import tpu as pltpu

N_DEV = 8
RMS_EPS = 1e-6
POOL_WINDOWS = (2, 4, 8, 16)
POOL_HALO = 16
CONV_HALO = 16
ADAM_LR, ADAM_B1, ADAM_B2, ADAM_EPS, ADAM_WD, ADAM_STEP = 0.001, 0.9, 0.999, 1e-08, 0.01, 10

VMEM_LIMIT = 56 * 2**20
MXU_COLUMNS = 256
BF16 = jnp.bfloat16
F32 = jnp.float32
MESH = pl.DeviceIdType.MESH
SIBLING_PAIR_ID = 0


def _params(**kw):
    return pltpu.CompilerParams(vmem_limit_bytes=VMEM_LIMIT, **kw)


def _resident(shape, index_map):
    return pl.BlockSpec(shape, index_map, pipeline_mode=pl.Buffered(1))


def _ordered(body, n_in, after):
    if after is None:
        return functools.partial(body), [], []
    return (lambda *refs: body(*refs[:n_in], *refs[n_in + 1:])), [after], [pl.BlockSpec(memory_space=pl.ANY)]


def _rms_fwd(x, g):
    r = lax.rsqrt(jnp.mean(x * x, axis=-1, keepdims=True) + RMS_EPS)
    return x * r * g


def _rms_bwd(x, g, dy):
    r = lax.rsqrt(jnp.mean(x * x, axis=-1, keepdims=True) + RMS_EPS)
    xhat = x * r
    dg = jnp.sum(dy * xhat, axis=0, keepdims=True)
    t = dy * g
    dx = r * (t - xhat * jnp.mean(t * xhat, axis=-1, keepdims=True))
    return dx, dg


def _sigmoid(x):
    return 0.5 * jnp.tanh(0.5 * x) + 0.5


def _dot(a, b):
    return jnp.dot(a, b, preferred_element_type=F32)


def _dot_nt(a, b):
    return lax.dot_general(a, b, (((1,), (1,)), ((), ())), preferred_element_type=F32)


def _dot_tn(a, b):
    return lax.dot_general(a, b, (((0,), (0,)), ((), ())), preferred_element_type=F32)


def _join_blocks(blocks_hbm, joined_ref, sems):
    n, _, C = blocks_hbm.shape
    copies = [pltpu.make_async_copy(blocks_hbm.at[k], joined_ref.at[:, k * C:(k + 1) * C], sems.at[k]) for k in range(n)]
    for cp in copies:
        cp.start()
    for cp in copies:
        cp.wait()


def _row_inverse_counts(tile, tm):
    pos = (lax.broadcasted_iota(jnp.int32, (tm, 1), 0) + tile * tm + 1).astype(F32)
    return [1.0 / jnp.minimum(pos, float(w)) for w in POOL_WINDOWS]


def _pool_from_ext(ext, a, invs, gw):
    s = ext
    outs = []
    for g, w in enumerate(POOL_WINDOWS):
        s = s[:, (gw if g else 0):]
        s = s + pltpu.roll(s, w // 2, 0)
        outs.append(s[POOL_HALO:, :gw] * invs[g] - a[:, g * gw:(g + 1) * gw])
    return outs


def _pool_fwd(h, pw, scale, g_pre, g_post, *, tm, after=None):
    T, D = h.shape
    gw = D // len(POOL_WINDOWS)
    hb = tm // POOL_HALO

    def body(h_ref, halo_ref, pw_ref, scale_ref, gpre_ref, gpost_ref, out_ref, pooled_ref, mixed_ref, ext_ref):
        i = pl.program_id(0)
        x = h_ref[...]
        a = _rms_fwd(x, gpre_ref[...])
        ah = _rms_fwd(halo_ref[...], gpre_ref[...])
        ext_ref[0:POOL_HALO, :] = jnp.where(i == 0, 0.0, ah)
        ext_ref[POOL_HALO:, :] = a
        pooled = [p.astype(BF16) for p in _pool_from_ext(ext_ref[...], a, _row_inverse_counts(i, tm), gw)]
        pooled_ref[...] = jnp.concatenate(pooled, axis=1)
        mixed = jnp.concatenate([_dot(p, pw_ref[g]) for g, p in enumerate(pooled)], axis=1)
        mixed_ref[...] = mixed.astype(BF16)
        out_ref[...] = x + _rms_fwd(mixed * scale_ref[...], gpost_ref[...])

    vec = _resident((1, D), lambda i: (0, 0))
    fn, xa, xs = _ordered(body, 6, after)
    return pl.pallas_call(
        fn, name="pool_fwd", grid=(T // tm,),
        in_specs=[pl.BlockSpec((tm, D), lambda i: (i, 0)),
                  pl.BlockSpec((POOL_HALO, D), lambda i: (jnp.maximum(i * hb - 1, 0), 0)),
                  _resident(pw.shape, lambda i: (0, 0, 0)), vec, vec, vec] + xs,
        out_specs=[pl.BlockSpec((tm, D), lambda i: (i, 0))] * 3,
        out_shape=[jax.ShapeDtypeStruct((T, D), F32), jax.ShapeDtypeStruct((T, D), BF16), jax.ShapeDtypeStruct((T, D), BF16)],
        scratch_shapes=[pltpu.VMEM((tm + POOL_HALO, D), F32)],
        compiler_params=_params(dimension_semantics=("arbitrary",)),
    )(h, h, pw, scale, g_pre, g_post, *xa)


def _pool_bwd(dh, h, pooled, mixed_pre, pw, scale, g_pre, g_post, *, tm, after=None):
    T, D = h.shape
    gw = D // len(POOL_WINDOWS)
    nt = T // tm
    n_ext = tm + POOL_HALO

    def body(dh_ref, h_ref, pooled_ref, mixed_ref, pw_ref, scale_ref, gpre_ref, gpost_ref,
             dx_ref, dpw_ref, small_ref, ext2_ref, carry_ref, dpw_acc):
        i = pl.program_id(0)
        tile = nt - 1 - i

        @pl.when(i == 0)
        def _():
            small_ref[...] = jnp.zeros_like(small_ref)
            dpw_acc[...] = jnp.zeros_like(dpw_acc)
            carry_ref[...] = jnp.zeros_like(carry_ref)

        x = h_ref[...]
        dout = dh_ref[...]
        invs = _row_inverse_counts(tile, tm)
        pooled = [pooled_ref[:, g * gw:(g + 1) * gw] for g in range(len(POOL_WINDOWS))]
        mixed_pre = mixed_ref[...].astype(F32)
        scale_v = scale_ref[...]
        dmixed, dg_post = _rms_bwd(mixed_pre * scale_v, gpost_ref[...], dout)
        small_ref[1:2, :] += dg_post
        small_ref[2:3, :] += jnp.sum(dmixed * mixed_pre, axis=0, keepdims=True)
        dpre = (dmixed * scale_v).astype(BF16)
        dpooled = []
        for g in range(len(POOL_WINDOWS)):
            dp = dpre[:, g * gw:(g + 1) * gw]
            dpw_acc[g] += _dot_tn(pooled[g], dp)
            dpooled.append(_dot_nt(dp, pw_ref[g]))
        q = jnp.concatenate([d * invs[g] for g, d in enumerate(dpooled)], axis=1)
        ext2_ref[0:tm, :] = q
        ext2_ref[tm:, :] = carry_ref[...]
        carry_ref[...] = q[0:POOL_HALO, :]
        s = ext2_ref[...]
        da = []
        for g, w in enumerate(POOL_WINDOWS):
            s = s[:, (gw if g else 0):]
            s = s + pltpu.roll(s, n_ext - w // 2, 0)
            da.append(s[0:tm, :gw] - dpooled[g])
        dx, dg_pre = _rms_bwd(x, gpre_ref[...], jnp.concatenate(da, axis=1))
        small_ref[0:1, :] += dg_pre
        dx_ref[...] = dout + dx

        @pl.when(i == nt - 1)
        def _():
            dpw_ref[...] = dpw_acc[...].astype(BF16)

    vec = _resident((1, D), lambda i: (0, 0))
    rev = lambda i: (nt - 1 - i, 0)
    fn, xa, xs = _ordered(body, 8, after)
    tile = pl.BlockSpec((tm, D), rev)
    return pl.pallas_call(
        fn, name="pool_bwd", grid=(nt,),
        in_specs=[tile, tile, tile, tile, _resident(pw.shape, lambda i: (0, 0, 0)), vec, vec, vec] + xs,
        out_specs=[tile, pl.BlockSpec(pw.shape, lambda i: (0, 0, 0)), pl.BlockSpec((8, D), lambda i: (0, 0))],
        out_shape=[jax.ShapeDtypeStruct((T, D), F32), jax.ShapeDtypeStruct(pw.shape, BF16),
                   jax.ShapeDtypeStruct((8, D), F32)],
        scratch_shapes=[pltpu.VMEM((n_ext, D), F32), pltpu.VMEM((POOL_HALO, D), F32), pltpu.VMEM(pw.shape, F32)],
        compiler_params=_params(dimension_semantics=("arbitrary",)),
    )(dh, h, pooled, mixed_pre, pw, scale, g_pre, g_post, *xa)


def _ffn_fwd(h, g_pre, g_post, wgu, wd, layer, target, *, tm, after=None):
    T, D = h.shape
    F = wd.shape[0]
    last = target is not None

    def body(*refs):
        if last:
            h_ref, gpre_ref, gpost_ref, wgu_ref, wd_ref, tgt_ref, out_ref, gu_ref, act_ref, ff_ref, loss_ref = refs
        else:
            h_ref, gpre_ref, gpost_ref, wgu_ref, wd_ref, out_ref, gu_ref, act_ref, ff_ref = refs
        x = h_ref[...]
        cb = _rms_fwd(x, gpre_ref[...]).astype(BF16)
        g = _dot_nt(cb, wgu_ref[0:F, :])
        u = _dot_nt(cb, wgu_ref[F:2 * F, :])
        gu_ref[0] = g.astype(BF16)
        gu_ref[1] = u.astype(BF16)
        act = (g * _sigmoid(g) * u).astype(BF16)
        act_ref[...] = act
        acc = _dot(act, wd_ref[...])
        ff_ref[...] = acc.astype(BF16)
        hout = x + _rms_fwd(acc, gpost_ref[...])
        if last:
            diff = hout - tgt_ref[...]
            out_ref[...] = diff * (1.0 / D)

            @pl.when(pl.program_id(0) == 0)
            def _():
                loss_ref[...] = jnp.zeros_like(loss_ref)

            loss_ref[...] += jnp.sum(diff * diff) * (0.5 / D)
        else:
            out_ref[...] = hout

    vec = _resident((1, D), lambda i: (0, 0))
    tile = pl.BlockSpec((tm, D), lambda i: (i, 0))
    in_specs = [tile, vec, vec, _resident(wgu.shape, lambda i: (0, 0)), _resident(wd.shape, lambda i: (0, 0))]
    out_specs = [tile, pl.BlockSpec((2, tm, F), lambda i: (0, i, 0)), pl.BlockSpec((tm, F), lambda i: (i, 0)), tile]
    out_shape = [jax.ShapeDtypeStruct((T, D), F32), jax.ShapeDtypeStruct((2, T, F), BF16),
                 jax.ShapeDtypeStruct((T, F), BF16), jax.ShapeDtypeStruct((T, D), BF16)]
    args = [h, g_pre, g_post, wgu, wd]
    if last:
        in_specs.append(tile)
        args.append(target)
        out_specs.append(pl.BlockSpec((8, 128), lambda i: (0, 0)))
        out_shape.append(jax.ShapeDtypeStruct((8, 128), F32))
    fn, xa, xs = _ordered(body, len(args), after)
    return pl.pallas_call(
        fn, name=f"ffn_fwd_{layer}", grid=(T // tm,), in_specs=in_specs + xs, out_specs=out_specs,
        out_shape=out_shape, compiler_params=_params(dimension_semantics=("arbitrary",)),
    )(*args, *xa)


def _ffn_up(h, g_pre, wgu, *, tm):
    T, D = h.shape
    F = wgu.shape[0] // 2

    def body(h_ref, gpre_ref, wgu_ref, gu_ref, act_ref):
        cb = _rms_fwd(h_ref[...], gpre_ref[...]).astype(BF16)
        g = _dot_nt(cb, wgu_ref[0:F, :])
        u = _dot_nt(cb, wgu_ref[F:2 * F, :])
        gu_ref[0] = g.astype(BF16)
        gu_ref[1] = u.astype(BF16)
        act_ref[...] = (g * _sigmoid(g) * u).astype(BF16)

    return pl.pallas_call(
        body, name="ffn_up_0", grid=(T // tm,),
        in_specs=[pl.BlockSpec((tm, D), lambda i: (i, 0)), _resident((1, D), lambda i: (0, 0)),
                  _resident(wgu.shape, lambda i: (0, 0))],
        out_specs=[pl.BlockSpec((2, tm, F), lambda i: (0, i, 0)), pl.BlockSpec((tm, F), lambda i: (i, 0))],
        out_shape=[jax.ShapeDtypeStruct((2, T, F), BF16), jax.ShapeDtypeStruct((T, F), BF16)],
        compiler_params=_params(dimension_semantics=("arbitrary",)),
    )(h, g_pre, wgu)


def _ffn_down(h, act, g_post, wd, *, tm):
    T, D = h.shape
    F = act.shape[1]

    def body(h_ref, act_ref, gpost_ref, wd_ref, out_ref, ff_ref):
        acc = _dot(act_ref[...], wd_ref[...])
        ff_ref[...] = acc.astype(BF16)
        out_ref[...] = h_ref[...] + _rms_fwd(acc, gpost_ref[...])

    tile = pl.BlockSpec((tm, D), lambda i: (i, 0))
    return pl.pallas_call(
        body, name="ffn_down_0", grid=(T // tm,),
        in_specs=[tile, pl.BlockSpec((tm, F), lambda i: (i, 0)), _resident((1, D), lambda i: (0, 0)),
                  _resident(wd.shape, lambda i: (0, 0))],
        out_specs=[tile, tile],
        out_shape=[jax.ShapeDtypeStruct((T, D), F32), jax.ShapeDtypeStruct((T, D), BF16)],
        compiler_params=_params(dimension_semantics=("arbitrary",)),
    )(h, act, g_post, wd)


def _ffn_bwd(dh, h, ff, gu, g_pre, g_post, wgu, wd, layer, *, tm, after=None):
    T, D = h.shape
    F = wd.shape[0]
    n_chunks = F // MXU_COLUMNS

    def body(dh_ref, h_ref, ff_ref, gu_ref, gpre_ref, gpost_ref, wgu_ref, wd_ref,
             dx_ref, dgu_ref, dff_ref, c_ref, small_ref):
        @pl.when(pl.program_id(0) == 0)
        def _():
            small_ref[...] = jnp.zeros_like(small_ref)

        dout = dh_ref[...]
        dff, dg_post = _rms_bwd(ff_ref[...].astype(F32), gpost_ref[...], dout)
        small_ref[1:2, :] += dg_post
        dffb = dff.astype(BF16)
        dff_ref[...] = dffb
        dc = jnp.zeros((tm, D), F32)
        for j in range(n_chunks + 1):
            lo, hi = j * MXU_COLUMNS, (j + 1) * MXU_COLUMNS
            if j < n_chunks:
                dact = _dot_nt(dffb, wd_ref[lo:hi, :])
            if j > 0:
                lo0 = lo - MXU_COLUMNS
                dc = dc + _dot(dgu_ref[0, :, lo0:lo], wgu_ref[lo0:lo, :]) + _dot(dgu_ref[1, :, lo0:lo], wgu_ref[F + lo0:F + lo, :])
            if j < n_chunks:
                g = gu_ref[0, :, lo:hi].astype(F32)
                u = gu_ref[1, :, lo:hi].astype(F32)
                s = _sigmoid(g)
                dgu_ref[0, :, lo:hi] = (dact * u * (s * (1.0 + g * (1.0 - s)))).astype(BF16)
                dgu_ref[1, :, lo:hi] = (dact * (g * s)).astype(BF16)
        x = h_ref[...]
        c_ref[...] = _rms_fwd(x, gpre_ref[...]).astype(BF16)
        dx, dg_pre = _rms_bwd(x, gpre_ref[...], dc)
        small_ref[0:1, :] += dg_pre
        dx_ref[...] = dout + dx

    vec = _resident((1, D), lambda i: (0, 0))
    tile = pl.BlockSpec((tm, D), lambda i: (i, 0))
    blk = pl.BlockSpec((2, tm, F), lambda i: (0, i, 0))
    fn, xa, xs = _ordered(body, 8, after)
    return pl.pallas_call(
        fn, name=f"ffn_bwd_{layer}", grid=(T // tm,),
        in_specs=[tile, tile, tile, blk, vec, vec,
                  _resident(wgu.shape, lambda i: (0, 0)), _resident(wd.shape, lambda i: (0, 0))] + xs,
        out_specs=[tile, blk, tile, tile, pl.BlockSpec((8, D), lambda i: (0, 0))],
        out_shape=[jax.ShapeDtypeStruct((T, D), F32), jax.ShapeDtypeStruct((2, T, F), BF16),
                   jax.ShapeDtypeStruct((T, D), BF16), jax.ShapeDtypeStruct((T, D), BF16),
                   jax.ShapeDtypeStruct((8, D), F32)],
        compiler_params=_params(dimension_semantics=("arbitrary",)),
    )(dh, h, ff, gu, g_pre, g_post, wgu, wd, *xa)


def _conv_fwd(h, g_pre, g_post, win, taps, wout, *, tm, after=None):
    T, D = h.shape
    nblk, cb = win.shape[0], win.shape[2]

    def body(h_ref, gpre_ref, gpost_ref, win_hbm, taps_ref, wout_ref,
             out_ref, proj_ref, y_ref, conv_ref, z_ref, proj_scr, ext_ref, carry_ref, win_ref, win_sems):
        i = pl.program_id(0)

        @pl.when(i == 0)
        def _():
            carry_ref[...] = jnp.zeros_like(carry_ref)
            _join_blocks(win_hbm, win_ref, win_sems)

        x = h_ref[...]
        a = _rms_fwd(x, gpre_ref[...]).astype(BF16)
        proj_scr[...] = _dot(a, win_ref[...])
        proj_ref[...] = proj_scr[...].astype(BF16)
        u = proj_scr[:, D:2 * D] * proj_scr[:, 2 * D:3 * D]
        ext_ref[0:CONV_HALO, :] = carry_ref[...]
        ext_ref[CONV_HALO:, :] = u
        carry_ref[...] = u[tm - CONV_HALO:, :]
        e = ext_ref[...]
        conv = (taps_ref[2:3, :] * u + taps_ref[1:2, :] * pltpu.roll(e, 1, 0)[CONV_HALO:, :]
                + taps_ref[0:1, :] * pltpu.roll(e, 2, 0)[CONV_HALO:, :])
        conv_ref[...] = conv.astype(BF16)
        z = (proj_scr[:, 0:D] * conv).astype(BF16)
        z_ref[...] = z
        y = _dot(z, wout_ref[...])
        y_ref[...] = y.astype(BF16)
        out_ref[...] = x + _rms_fwd(y, gpost_ref[...])

    vec = _resident((1, D), lambda i: (0, 0))
    tile = pl.BlockSpec((tm, D), lambda i: (i, 0))
    fn, xa, xs = _ordered(body, 6, after)
    return pl.pallas_call(
        fn, name="conv_fwd", grid=(T // tm,),
        in_specs=[tile, vec, vec, pl.BlockSpec(memory_space=pl.ANY),
                  _resident(taps.shape, lambda i: (0, 0)), _resident(wout.shape, lambda i: (0, 0))] + xs,
        out_specs=[tile, pl.BlockSpec((tm, 3 * D), lambda i: (i, 0)), tile, tile, tile],
        out_shape=[jax.ShapeDtypeStruct((T, D), F32), jax.ShapeDtypeStruct((T, 3 * D), BF16),
                   jax.ShapeDtypeStruct((T, D), BF16), jax.ShapeDtypeStruct((T, D), BF16),
                   jax.ShapeDtypeStruct((T, D), BF16)],
        scratch_shapes=[pltpu.VMEM((tm, 3 * D), F32), pltpu.VMEM((tm + CONV_HALO, D), F32),
                        pltpu.VMEM((CONV_HALO, D), F32), pltpu.VMEM((D, nblk * cb), BF16),
                        pltpu.SemaphoreType.DMA((nblk,))],
        compiler_params=_params(dimension_semantics=("arbitrary",)),
    )(h, g_pre, g_post, win, taps, wout, *xa)


def _conv_bwd(dh, h, y, proj, conv, g_pre, g_post, win, taps, wout, *, tm, after=None):
    T, D = h.shape
    nblk, cb = win.shape[0], win.shape[2]
    nt = T // tm
    n_ext = tm + CONV_HALO

    def body(dh_ref, h_ref, y_ref, proj_ref, conv_ref, gpre_ref, gpost_ref, win_hbm, taps_ref, wout_ref,
             dx_ref, dproj_ref, a_ref, dy_ref, small_ref, ext2_ref, carry_ref, win_ref, win_sems):
        i = pl.program_id(0)

        @pl.when(i == 0)
        def _():
            small_ref[...] = jnp.zeros_like(small_ref)
            carry_ref[...] = jnp.zeros_like(carry_ref)
            _join_blocks(win_hbm, win_ref, win_sems)

        dout = dh_ref[...]
        dy, dg_post = _rms_bwd(y_ref[...].astype(F32), gpost_ref[...], dout)
        small_ref[1:2, :] += dg_post
        dyb = dy.astype(BF16)
        dy_ref[...] = dyb
        dz = _dot_nt(dyb, wout_ref[...])
        bgate = proj_ref[:, 0:D].astype(F32)
        cgate = proj_ref[:, D:2 * D].astype(F32)
        v = proj_ref[:, 2 * D:3 * D].astype(F32)
        u = cgate * v
        t0, t1, t2 = taps_ref[0:1, :], taps_ref[1:2, :], taps_ref[2:3, :]
        dconv = dz * bgate
        ext2_ref[0:tm, :] = dconv
        ext2_ref[tm:, :] = carry_ref[...]
        carry_ref[...] = dconv[0:CONV_HALO, :]
        e2 = ext2_ref[...]
        ahead1 = pltpu.roll(e2, n_ext - 1, 0)[0:tm, :]
        ahead2 = pltpu.roll(e2, n_ext - 2, 0)[0:tm, :]
        small_ref[2:3, :] += jnp.sum(ahead2 * u, axis=0, keepdims=True)
        small_ref[3:4, :] += jnp.sum(ahead1 * u, axis=0, keepdims=True)
        small_ref[4:5, :] += jnp.sum(dconv * u, axis=0, keepdims=True)
        du = t2 * dconv + t1 * ahead1 + t0 * ahead2
        dproj_ref[:, 0:D] = (dz * conv_ref[...].astype(F32)).astype(BF16)
        dproj_ref[:, D:2 * D] = (du * v).astype(BF16)
        dproj_ref[:, 2 * D:3 * D] = (du * cgate).astype(BF16)
        da = _dot_nt(dproj_ref[...], win_ref[...])
        x = h_ref[...]
        a_ref[...] = _rms_fwd(x, gpre_ref[...]).astype(BF16)
        dx, dg_pre = _rms_bwd(x, gpre_ref[...], da)
        small_ref[0:1, :] += dg_pre
        dx_ref[...] = dout + dx

    vec = _resident((1, D), lambda i: (0, 0))
    rev = lambda i: (nt - 1 - i, 0)
    tile = pl.BlockSpec((tm, D), rev)
    wide = pl.BlockSpec((tm, 3 * D), rev)
    fn, xa, xs = _ordered(body, 10, after)
    return pl.pallas_call(
        fn, name="conv_bwd", grid=(nt,),
        in_specs=[tile, tile, tile, wide, tile, vec, vec, pl.BlockSpec(memory_space=pl.ANY),
                  _resident(taps.shape, lambda i: (0, 0)), _resident(wout.shape, lambda i: (0, 0))] + xs,
        out_specs=[tile, wide, tile, tile, pl.BlockSpec((8, D), lambda i: (0, 0))],
        out_shape=[jax.ShapeDtypeStruct((T, D), F32), jax.ShapeDtypeStruct((T, 3 * D), BF16),
                   jax.ShapeDtypeStruct((T, D), BF16), jax.ShapeDtypeStruct((T, D), BF16),
                   jax.ShapeDtypeStruct((8, D), F32)],
        scratch_shapes=[pltpu.VMEM((n_ext, D), F32), pltpu.VMEM((CONV_HALO, D), F32),
                        pltpu.VMEM((D, nblk * cb), BF16), pltpu.SemaphoreType.DMA((nblk,))],
        compiler_params=_params(dimension_semantics=("arbitrary",)),
    )(dh, h, y, proj, conv, g_pre, g_post, win, taps, wout, *xa)


def _wgrad(name, a, b, a_spec, b_spec, block, n_blocks, *, tk, after=None):
    T = a.shape[-2]
    nk = T // tk
    M, N = block
    m = N_DEV // n_blocks
    R = M // m

    def body(a_ref, b_ref, out_ref, land_hbm, acc_ref, stage_ref, recv_ref, send_sems, recv_sems, land_sem):
        i, k = pl.program_id(0), pl.program_id(1)
        x, y, c = lax.axis_index("x"), lax.axis_index("y"), lax.axis_index("c")

        def sent(blk, p):
            owner = blk * m + p
            q = owner // 2
            return (owner % 2) != c, pltpu.make_async_remote_copy(
                src_ref=stage_ref.at[p * R:(p + 1) * R], dst_ref=recv_ref.at[q], send_sem=send_sems.at[q],
                recv_sem=recv_sems.at[q], device_id=(x, y, 1 - c), device_id_type=MESH)

        @pl.when(jnp.logical_and(i == 0, k == 0))
        def _():
            barrier = pltpu.get_barrier_semaphore()
            pl.semaphore_signal(barrier, inc=1, device_id=(x, y, 1 - c), device_id_type=MESH)
            pl.semaphore_wait(barrier, 1)

        @pl.when(k == 0)
        def _():
            acc_ref[...] = jnp.zeros_like(acc_ref)

        acc_ref[...] += _dot_tn(a_ref[...], b_ref[...])

        @pl.when(k == nk - 1)
        def _():
            for p in range(m):
                away, copy = sent(jnp.maximum(i - 1, 0), p)

                @pl.when(jnp.logical_and(i > 0, away))
                def _():
                    copy.wait_send()

            acc = acc_ref[...]
            stage_ref[...] = acc.astype(BF16)
            for p in range(m):
                away, copy = sent(i, p)

                @pl.when(away)
                def _():
                    copy.start()

                @pl.when(jnp.logical_not(away))
                def _():
                    out_ref[(i * m + p) // 2] = stage_ref[p * R:(p + 1) * R, :]

        @pl.when(jnp.logical_and(i == n_blocks - 1, k == nk - 1))
        def _():
            for p in range(m):
                away, copy = sent(i, p)

                @pl.when(away)
                def _():
                    copy.wait_send()

            for q in range(N_DEV // 2):
                pltpu.make_async_remote_copy(
                    src_ref=stage_ref.at[0:R], dst_ref=recv_ref.at[q], send_sem=send_sems.at[q],
                    recv_sem=recv_sems.at[q], device_id=(x, y, 1 - c), device_id_type=MESH).wait_recv()
                out_ref[q] = (out_ref[q].astype(F32) + recv_ref[q].astype(F32)).astype(BF16)
            mine = pltpu.make_async_copy(out_ref.at[2 * x + y], land_hbm.at[2 * x + y], land_sem)
            mine.start()
            mine.wait()

    fn, xa, xs = _ordered(body, 2, after)
    sums = jax.ShapeDtypeStruct((N_DEV // 2, R, N), BF16)
    return pl.pallas_call(
        fn, name=name, grid=(n_blocks, nk), in_specs=[a_spec, b_spec] + xs,
        out_specs=[pl.BlockSpec((N_DEV // 2, R, N), lambda i, k: (0, 0, 0)), pl.BlockSpec(memory_space=pl.ANY)],
        out_shape=[sums, sums],
        scratch_shapes=[pltpu.VMEM(block, F32), pltpu.VMEM(block, BF16), pltpu.VMEM((N_DEV // 2, R, N), BF16),
                        pltpu.SemaphoreType.DMA((N_DEV // 2,)), pltpu.SemaphoreType.DMA((N_DEV // 2,)),
                        pltpu.SemaphoreType.DMA],
        compiler_params=_params(dimension_semantics=("arbitrary", "arbitrary"), collective_id=SIBLING_PAIR_ID),
    )(a, b, *xa)


Copy = collections.namedtuple("Copy", "mask sb src db dst sem")
Local = collections.namedtuple("Local", "sb src db dst")

HBM_SPEC = pl.BlockSpec(memory_space=pltpu.HBM)
SEM_SPEC = pl.BlockSpec(memory_space=pltpu.SEMAPHORE)
SIBLING, X_PEER, Y_PEER, DIAGONAL = 1, 4, 2, 6
OTHER_CHIPS = (X_PEER, Y_PEER, DIAGONAL)


def _whole(ref, i):
    return ref


def _lead(ref, i):
    return ref.at[i]


def _second(ref, i):
    return ref.at[:, i]


def _place():
    x, y, c = lax.axis_index("x"), lax.axis_index("y"), lax.axis_index("c")
    return (x, y, c), 4 * x + 2 * y + c


def _descriptor(cp, bufs, xyc, me, sender, send_sems, recv_sems):
    x, y, c = xyc
    flip = lambda v, bit: (1 - v) if bit else v
    return pltpu.make_async_remote_copy(
        src_ref=cp.src(bufs[cp.sb], me), dst_ref=cp.dst(bufs[cp.db], sender),
        send_sem=send_sems.at[cp.sem], recv_sem=recv_sems.at[cp.sem],
        device_id=(flip(x, cp.mask & 4), flip(y, cp.mask & 2), flip(c, cp.mask & 1)), device_id_type=MESH)


def _exchange(name, bufs, plan, local=()):
    n = len(bufs)

    def body(*refs):
        ins = refs[:n]
        send_sems, recv_sems, local_sems = refs[2 * n:]
        xyc, me = _place()
        own = [pltpu.make_async_copy(lc.src(ins[lc.sb], me), lc.dst(ins[lc.db], me), local_sems.at[i])
               for i, lc in enumerate(local)]
        sends = [_descriptor(cp, ins, xyc, me, me, send_sems, recv_sems) for cp in plan]
        for cp in own + sends:
            cp.start()
        for cp in plan:
            _descriptor(cp, ins, xyc, me, me ^ cp.mask, send_sems, recv_sems).wait_recv()
        for cp in sends:
            cp.wait_send()
        for cp in own:
            cp.wait()

    return pl.pallas_call(
        body, name=name, in_specs=[HBM_SPEC] * n, out_specs=[HBM_SPEC] * n,
        out_shape=[jax.ShapeDtypeStruct(b.shape, b.dtype) for b in bufs],
        input_output_aliases={i: i for i in range(n)},
        scratch_shapes=[pltpu.SemaphoreType.DMA((len(plan),)), pltpu.SemaphoreType.DMA((len(plan),)),
                        pltpu.SemaphoreType.DMA((max(len(local), 1),))],
    )(*bufs)


def _place_own(me, items):
    def body(me_ref, *refs):
        for src, dst in zip(refs[:len(items)], refs[len(items):]):
            dst[...] = src[...].astype(dst.dtype)

    return pl.pallas_call(
        body, name="place_own",
        grid_spec=pltpu.PrefetchScalarGridSpec(
            num_scalar_prefetch=1, grid=(1,),
            in_specs=[pl.BlockSpec(blk, functools.partial(lambda i, m, idx: idx, idx=idx)) for _, blk, idx, _, _, _, _ in items],
            out_specs=[pl.BlockSpec(oblk, functools.partial(lambda i, m, at: at(m[0]), at=at)) for *_, oblk, at in items]),
        out_shape=[jax.ShapeDtypeStruct(shape, dtype) for _, _, _, shape, dtype, _, _ in items],
        compiler_params=_params(dimension_semantics=("arbitrary",)),
    )(jnp.reshape(me, (1,)).astype(jnp.int32), *[a for a, *_ in items])


def _split_call(name, bufs, *, wait=None, wait_sems=None, start=None, local=(), after=None, token=False):
    n = len(bufs)
    n_wait = 2 if wait else 0
    n_after = 1 if after is not None else 0
    n_start = 2 if start else 0

    def body(*refs):
        ins = refs[:n]
        wsend, wrecv = refs[n:n + n_wait] if wait else (None, None)
        outs = refs[n + n_wait + n_after:]
        ssend, srecv = outs[:n_start] if start else (None, None)
        rest = outs[n_start + n:]
        xyc, me = _place()
        for cp in wait or ():
            d = _descriptor(cp, ins, xyc, me, me ^ cp.mask, wsend, wrecv)
            d.wait_send()
            d.wait_recv()
        own = [pltpu.make_async_copy(lc.src(ins[lc.sb], me), lc.dst(ins[lc.db], me), rest[-1].at[i])
               for i, lc in enumerate(local)]
        for cp in own:
            cp.start()
        for cp in start or ():
            _descriptor(cp, ins, xyc, me, me, ssend, srecv).start()
        for cp in own:
            cp.wait()
        if token:
            rest[0][...] = jnp.zeros_like(rest[0])

    args = [pltpu.with_memory_space_constraint(b, pltpu.HBM) for b in bufs]
    in_specs = [HBM_SPEC] * n
    if wait:
        args += list(wait_sems)
        in_specs += [SEM_SPEC] * 2
    if after is not None:
        args.append(after)
        in_specs.append(pl.BlockSpec(memory_space=pl.ANY))
    out_shape, out_specs = [], []
    if start:
        out_shape += [pltpu.SemaphoreType.DMA((len(start),))] * 2
        out_specs += [SEM_SPEC] * 2
    out_shape += [pltpu.HBM(b.shape, b.dtype) for b in bufs]
    out_specs += [HBM_SPEC] * n
    if token:
        out_shape.append(jax.ShapeDtypeStruct((8, 128), F32))
        out_specs.append(pl.BlockSpec(memory_space=pltpu.VMEM))
    outs = pl.pallas_call(
        body, name=name, in_specs=in_specs, out_specs=out_specs, out_shape=out_shape,
        input_output_aliases={i: n_start + i for i in range(n)},
        scratch_shapes=[pltpu.SemaphoreType.DMA((len(local),))] if local else [],
        compiler_params=pltpu.CompilerParams(has_side_effects=pltpu.SideEffectType.DATAFLOW_SIDE_EFFECTING),
    )(*args)
    sems = tuple(outs[:n_start]) if start else None
    return sems, list(outs[n_start:n_start + n]), (outs[n_start + n] if token else None)


def _adamw(w, g, m, v):
    m = ADAM_B1 * m + (1.0 - ADAM_B1) * g
    v = ADAM_B2 * v + (1.0 - ADAM_B2) * (g * g)
    m_hat = m / (1.0 - ADAM_B1 ** ADAM_STEP)
    v_hat = v / (1.0 - ADAM_B2 ** ADAM_STEP)
    delta = -ADAM_LR * (m_hat / (jnp.sqrt(v_hat) + ADAM_EPS) + ADAM_WD * w)
    return delta, m, v


def _reduce_adam(name, parts, w, m, v, *, tr, layer=None, into=None, after=None, transposed=False):
    L, R, C = w.shape
    S = parts[0].shape[0]
    tr = min(tr, R)
    n_l = L if layer is None else 1
    first = 0 if layer is None else layer

    def body(*refs):
        p_refs = refs[:n_l]
        w_ref, m_ref, v_ref = refs[n_l:n_l + 3]
        g_ref, d_ref, nm_ref, nv_ref = refs[-4:]
        for l in range(n_l):
            g = p_refs[l][0].astype(F32)
            for s in range(1, S):
                g = g + p_refs[l][s].astype(F32)
            g = g.T if transposed else g
            g_ref[l] = g
            d_ref[l], nm_ref[l], nv_ref[l] = _adamw(w_ref[l], g, m_ref[l], v_ref[l])

    blk = pl.BlockSpec((n_l, tr, C), lambda r: (first, r, 0))
    out = jax.ShapeDtypeStruct((L, R, C), F32)
    extra = list(into or []) + ([after] if after is not None else [])
    return pl.pallas_call(
        body, name=name, grid=(R // tr,),
        in_specs=[pl.BlockSpec((S, C, tr), lambda r: (0, 0, r)) if transposed else pl.BlockSpec((S, tr, C), lambda r: (0, r, 0))] * n_l
        + [blk, blk, blk]
        + [pl.BlockSpec(memory_space=pl.ANY)] * len(extra),
        out_specs=[blk] * 4, out_shape=[out] * 4,
        input_output_aliases={n_l + 3 + i: i for i in range(4)} if into else {},
        compiler_params=_params(dimension_semantics=("arbitrary",)),
    )(*parts, w, m, v, *extra)


def _small_update(me, parts, gains, taps, scale):
    dcol = gains[0].shape[1]
    gain_rows, tap_rows, scale_row, loss_row = [0, 1, 8, 9, 16, 17, 24, 25], [18, 19, 20], 2, 32

    def body(me_ref, full_ref, col_ref, wg, mg, vg, wt, mt, vt, ws, ms, vs, *outs):
        col = col_ref[0]
        g_scale = full_ref[0, scale_row:scale_row + 1, :]
        loss = full_ref[0, loss_row:loss_row + 1, 0:128]
        for d in range(1, N_DEV):
            col = col + col_ref[d]
            g_scale = g_scale + full_ref[d, scale_row:scale_row + 1, :]
            loss = loss + full_ref[d, loss_row:loss_row + 1, 0:128]
        for r, src in enumerate(gain_rows):
            outs[0][r:r + 1, :] = col[src:src + 1, :]
        for r, src in enumerate(tap_rows):
            outs[4][r] = col[src:src + 1, :]
        outs[8][...] = g_scale
        for k, (w, m, v) in enumerate(((wg, mg, vg), (wt, mt, vt), (ws, ms, vs))):
            outs[4 * k + 1][...], outs[4 * k + 2][...], outs[4 * k + 3][...] = _adamw(w[...], outs[4 * k][...], m[...], v[...])
        outs[12][...] = jnp.broadcast_to(loss, outs[12].shape)

    whole = lambda a: pl.BlockSpec(a.shape, lambda i, m: (0,) * a.ndim)
    small = [a for t in (gains, taps, scale) for a in t]
    shapes = [jax.ShapeDtypeStruct(t[0].shape, F32) for t in (gains, taps, scale) for _ in range(4)]
    shapes.append(jax.ShapeDtypeStruct((8, 128), F32))
    return pl.pallas_call(
        body, name="small_update",
        grid_spec=pltpu.PrefetchScalarGridSpec(
            num_scalar_prefetch=1, grid=(1,),
            in_specs=[whole(parts), pl.BlockSpec(parts.shape[:2] + (dcol,), lambda i, m: (0, 0, m[0]))] + [whole(a) for a in small],
            out_specs=[pl.BlockSpec(sh.shape, functools.partial(lambda i, m, n: (0,) * n, n=len(sh.shape))) for sh in shapes]),
        out_shape=shapes, compiler_params=_params(dimension_semantics=("arbitrary",)),
    )(jnp.reshape(me, (1,)).astype(jnp.int32), parts, parts, *small)


def kernel(x, norm_gains, pool_w, pool_scale, conv_in_w, conv_w, conv_out_w, ffn_gate_up_w, ffn_down_w, loss_target, m_norm_gains, m_pool_w, m_pool_scale, m_conv_in_w, m_conv_w, m_conv_out_w, m_ffn_gate_up_w, m_ffn_down_w, v_norm_gains, v_pool_w, v_pool_scale, v_conv_in_w, v_conv_w, v_conv_out_w, v_ffn_gate_up_w, v_ffn_down_w):
    T, D = x.shape[1], x.shape[2]
    tm = min(512, T)
    tm_b = min(256, T)
    tk = min(2048, T)
    n_layers = ffn_gate_up_w.shape[0]
    fb = ffn_gate_up_w.shape[2]
    fr = ffn_down_w.shape[1]
    dcol = norm_gains.shape[2]
    cb = conv_in_w.shape[2]
    gw = pool_w.shape[3]
    me = 4 * lax.axis_index("x") + 2 * lax.axis_index("y") + lax.axis_index("c")

    small_w = jnp.concatenate([norm_gains.reshape(8, dcol), jnp.pad(conv_w[0], ((0, 5), (0, 0)))], axis=0)
    every = range(1, N_DEV)
    wgu_t, m_wgu_t, v_wgu_t = (jnp.swapaxes(a, 1, 2) for a in (ffn_gate_up_w, m_ffn_gate_up_w, v_ffn_gate_up_w))
    own_lead = lambda s: lax.dynamic_update_slice(lax.empty((N_DEV,) + s.shape, s.dtype), s[None], (me,) + (0,) * s.ndim)
    lead_item = lambda a, l, dtype: (a, (None,) + a.shape[1:], (l, 0, 0), (N_DEV,) + a.shape[1:], dtype,
                                     (None,) + a.shape[1:], lambda i: (i, 0, 0))
    lands = _place_own(me, [
        (pool_w, (None,) + pool_w.shape[1:], (0, 0, 0, 0), (4, N_DEV, gw // N_DEV, gw), BF16,
         (4, None, gw // N_DEV, gw), lambda i: (0, i, 0, 0)),
        (small_w[None], (None,) + small_w.shape, (0, 0, 0), (N_DEV,) + small_w.shape, F32, (None,) + small_w.shape, lambda i: (i, 0, 0)),
        lead_item(wgu_t, 0, BF16), lead_item(ffn_down_w, 0, BF16), lead_item(conv_in_w, 0, BF16),
        lead_item(conv_out_w, 0, BF16), lead_item(wgu_t, 1, BF16), lead_item(ffn_down_w, 1, BF16)])
    n_first, n_big = 2, len(lands) - 2
    direct = ([Copy(m, 0, _second, 0, _second, m - 1) for m in every]
              + [Copy(m, 1, _lead, 1, _lead, N_DEV - 2 + m) for m in every])
    level1 = [Copy(mask, n_first + n, _lead, n_first + n, _lead, len(direct) + 4 * n + j)
              for n in range(n_big) for j, mask in enumerate((SIBLING,) + OTHER_CHIPS)]
    sems1, bufs1, _ = _split_call("gather_start", lands, start=direct + level1)
    pw_g, small_g = _split_call("gather_small_done", bufs1[:n_first], wait=direct, wait_sems=sems1)[1]
    pw = pw_g.reshape(4, gw, gw)
    small_full = jnp.swapaxes(small_g, 0, 1).reshape(16, D)
    gain = lambda l, s: small_full[4 * l + s][None, :]
    taps = small_full[8:16]

    def forward_on(name, group, after):
        k = len(group)
        landed = [Copy(cp.mask, i, cp.src, i, cp.dst, cp.sem)
                  for i, n in enumerate(group) for cp in level1 if cp.sb == n_first + n]
        onward = [Copy(SIBLING, i, (lambda ref, me, m=m: ref.at[me ^ m]), i, (lambda ref, sender, m=m: ref.at[sender ^ m]), 3 * i + j)
                  for i in range(k) for j, m in enumerate(OTHER_CHIPS)]
        sems2, bufs2, tok = _split_call(name + "_forward", [bufs1[n_first + n] for n in group], wait=landed,
                                        wait_sems=sems1, start=onward, after=after, token=True)
        return (name, onward, sems2, bufs2), tok

    def arrived(state, after=None):
        name, onward, sems2, lands2 = state
        return _split_call(name + "_done", lands2, wait=onward, wait_sems=sems2, after=after)[1]

    h0 = x[0]
    h1, pooled, mixed_pre = _pool_fwd(h0, pw, pool_scale, gain(0, 0), gain(0, 1), tm=tm)
    (wgu0,) = arrived(forward_on("gather_gate_up_0", [0], h1)[0])
    wgu0 = wgu0.reshape(N_DEV * fb, D)
    gu0, act0 = _ffn_up(h1, gain(0, 2), wgu0, tm=tm)
    ag_down0, tok = forward_on("gather_down_0", [1], act0)
    ag_conv, tok = forward_on("gather_conv", [2, 3], tok)
    (wd0,) = arrived(ag_down0, tok)
    wd0 = wd0.reshape(N_DEV * fr, D)
    h2, ff0 = _ffn_down(h1, act0, gain(0, 3), wd0, tm=tm)
    win_g, wout_g = arrived(ag_conv, h2)
    wout = wout_g.reshape(D, D)
    h3, proj, y, conv, z = _conv_fwd(h2, gain(1, 0), gain(1, 1), win_g, taps, wout, tm=tm)
    ag_ffn1, tok = forward_on("gather_ffn1", [4, 5], h3)
    wgu1, wd1 = arrived(ag_ffn1, tok)
    wgu1, wd1 = wgu1.reshape(N_DEV * fb, D), wd1.reshape(N_DEV * fr, D)
    dh4, gu1, act1, ff1, loss_part = _ffn_fwd(h3, gain(1, 2), gain(1, 3), wgu1, wd1, 1, loss_target[0], tm=tm)


    def scatter_start(name, pairs):
        k = len(pairs)
        sums, lands = [p[0] for p in pairs], [p[1] for p in pairs]
        plan = [Copy(m, n, (lambda ref, i, m=m: ref.at[(i ^ m) >> 1]), k + n, (lambda ref, i: ref.at[i >> 1]), 3 * n + j)
                for n in range(k) for j, m in enumerate(OTHER_CHIPS)]
        sems, bufs, tok = _split_call(name + "_start", sums + lands, start=plan, token=True)
        return (name, plan, sems, bufs), tok

    def scatter_done(state, after):
        name, plan, sems, bufs = state
        return _split_call(name + "_done", bufs, wait=plan, wait_sems=sems, after=after)[1][len(bufs) // 2:]

    seq = lambda i, k: (k, 0)
    gu_pair = pl.BlockSpec((None, tk, 2 * fb), lambda i, k: (i // 2, k, i % 2))
    act_pair = pl.BlockSpec((tk, 2 * fb), lambda i, k: (k, i))
    rows = pl.BlockSpec((tk, D), seq)
    dh3, dgu1, dff1, c1, small_f1 = _ffn_bwd(dh4, h3, ff1, gu1, gain(1, 2), gain(1, 3), wgu1, wd1, 1, tm=tm_b)
    g_wgu1 = _wgrad("wgrad_gate_up_1", dgu1, c1, gu_pair, rows, (2 * fb, D), N_DEV // 2, tk=tk)
    g_wd1 = _wgrad("wgrad_down_1", act1, dff1, act_pair, rows, (2 * fb, D), N_DEV // 4, tk=tk)
    rs_ffn1, tok = scatter_start("scatter_ffn1", [g_wgu1, g_wd1])
    dh2, dproj, a1, dy, small_c = _conv_bwd(dh3, h2, y, proj, conv, gain(1, 0), gain(1, 1), win_g, taps, wout, tm=tm, after=tok)
    g_win = _wgrad("wgrad_conv_in", dproj, a1, pl.BlockSpec((tk, 2 * cb), lambda i, k: (k, i)), rows, (2 * cb, D), N_DEV // 2,
                   tk=tk)
    g_wout = _wgrad("wgrad_conv_out", z, dy, rows, rows, (D, D), 1, tk=tk)
    rs_conv, tok = scatter_start("scatter_conv", [g_win, g_wout])
    dh1, dgu0, dff0, c0, small_f0 = _ffn_bwd(dh2, h1, ff0, gu0, gain(0, 2), gain(0, 3), wgu0, wd0, 0, tm=tm_b, after=tok)
    g_wgu0 = _wgrad("wgrad_gate_up_0", dgu0, c0, gu_pair, rows, (2 * fb, D), N_DEV // 2, tk=tk)
    rs_wgu0, tok = scatter_start("scatter_gate_up_0", [g_wgu0])
    g_wd0 = _wgrad("wgrad_down_0", act0, dff0, act_pair, rows, (2 * fb, D), N_DEV // 4, tk=tk, after=tok)
    rs_wd0, tok = scatter_start("scatter_down_0", [g_wd0])
    grad_x, g_pw, small_p = _pool_bwd(dh1, h0, pooled, mixed_pre, pw, pool_scale, gain(0, 0), gain(0, 1), tm=tm, after=tok)

    loss_rows = jnp.broadcast_to(loss_part[0:1, 0:1], (8, D))
    small_part = jnp.concatenate([small_p, small_f0, small_c, small_f1, loss_rows], axis=0)
    g_pw = g_pw.reshape(4, N_DEV, gw // N_DEV, gw)
    pw_land = lax.dynamic_update_slice(lax.empty(g_pw.shape, BF16), lax.dynamic_slice_in_dim(g_pw, me, 1, 1), (0, me, 0, 0))
    last = ([Copy(m, 0, (lambda ref, i, m=m: ref.at[:, i ^ m]), 2, _second, m - 1) for m in every]
            + [Copy(m, 1, _whole, 3, _lead, N_DEV - 2 + m) for m in every])
    sems_l, bufs_l, tok = _split_call("scatter_small_start", [g_pw, small_part, pw_land, own_lead(small_part)],
                                      start=last, token=True)

    (r_wgu1, r_wd1), (r_win, r_wout) = scatter_done(rs_ffn1, tok), scatter_done(rs_conv, tok)
    o_win = _reduce_adam("adam_conv_in", [r_win], conv_in_w, m_conv_in_w, v_conv_in_w, tr=256, transposed=True)
    o_wout = _reduce_adam("adam_conv_out", [r_wout], conv_out_w, m_conv_out_w, v_conv_out_w, tr=128, after=o_win[0])
    o_wgu = _reduce_adam("adam_gate_up_1", [r_wgu1], wgu_t, m_wgu_t, v_wgu_t, tr=176, layer=1, after=o_wout[0])
    o_wd = _reduce_adam("adam_down_1", [r_wd1], ffn_down_w, m_ffn_down_w, v_ffn_down_w, tr=176, layer=1, after=o_wgu[0])
    r_pw, r_small = _split_call("scatter_small_done", bufs_l, wait=last, wait_sems=sems_l, after=o_wd[0])[1][2:]
    (r_wgu0,), (r_wd0,) = scatter_done(rs_wgu0, r_small), scatter_done(rs_wd0, r_small)
    o_wgu = _reduce_adam("adam_gate_up_0", [r_wgu0], wgu_t, m_wgu_t, v_wgu_t, tr=176, layer=0, into=o_wgu)
    o_wgu = [jnp.swapaxes(o, 1, 2) for o in o_wgu]
    o_wd = _reduce_adam("adam_down_0", [r_wd0], ffn_down_w, m_ffn_down_w, v_ffn_down_w, tr=176, layer=0, into=o_wd)
    o_pw = _reduce_adam("adam_pool_w", [r_pw[g] for g in range(4)], pool_w[0], m_pool_w[0], v_pool_w[0], tr=32)
    o_small = _small_update(
        me, r_small,
        (norm_gains.reshape(8, dcol), m_norm_gains.reshape(8, dcol), v_norm_gains.reshape(8, dcol)),
        tuple(jnp.swapaxes(a, 0, 1) for a in (conv_w, m_conv_w, v_conv_w)), (pool_scale, m_pool_scale, v_pool_scale))
    o_gain, o_taps, o_scale, loss = o_small[0:4], o_small[4:8], o_small[8:12], o_small[12][0, 0]

    gshape = norm_gains.shape
    per = lambda k: (o_gain[k].reshape(gshape), o_pw[k][None], o_scale[k], o_win[k], jnp.swapaxes(o_taps[k], 0, 1),
                     o_wout[k], o_wgu[k], o_wd[k])
    return (loss, grad_x[None], *per(0), *per(1), *per(2), *per(3))
```

```python
import collections
import functools

import jax
import jax.numpy as jnp
from jax import lax
from jax.experimental import pallas as pl
from jax.experimental.pallas import tpu as pltpu

N_DEV = 8
RMS_EPS = 1e-6
POOL_WINDOWS = (2, 4, 8, 16)
POOL_HALO = 16
CONV_HALO = 16
ADAM_LR, ADAM_B1, ADAM_B2, ADAM_EPS, ADAM_WD, ADAM_STEP = 0.001, 0.9, 0.999, 1e-08, 0.01, 10

VMEM_LIMIT = 56 * 2**20
MXU_COLUMNS = 256
BF16 = jnp.bfloat16
F32 = jnp.float32
MESH = pl.DeviceIdType.MESH
SIBLING_PAIR_ID = 0


def _params(**kw):
    return pltpu.CompilerParams(vmem_limit_bytes=VMEM_LIMIT, **kw)


def _resident(shape, index_map):
    return pl.BlockSpec(shape, index_map, pipeline_mode=pl.Buffered(1))


def _ordered(body, n_in, after):
    if after is None:
        return functools.partial(body), [], []
    return (lambda *refs: body(*refs[:n_in], *refs[n_in + 1:])), [after], [pl.BlockSpec(memory_space=pl.ANY)]


def _rms_fwd(x, g):
    r = lax.rsqrt(jnp.mean(x * x, axis=-1, keepdims=True) + RMS_EPS)
    return x * r * g


def _rms_bwd(x, g, dy):
    r = lax.rsqrt(jnp.mean(x * x, axis=-1, keepdims=True) + RMS_EPS)
    xhat = x * r
    dg = jnp.sum(dy * xhat, axis=0, keepdims=True)
    t = dy * g
    dx = r * (t - xhat * jnp.mean(t * xhat, axis=-1, keepdims=True))
    return dx, dg


def _sigmoid(x):
    return 0.5 * jnp.tanh(0.5 * x) + 0.5


def _dot(a, b):
    return jnp.dot(a, b, preferred_element_type=F32)


def _dot_nt(a, b):
    return lax.dot_general(a, b, (((1,), (1,)), ((), ())), preferred_element_type=F32)


def _dot_tn(a, b):
    return lax.dot_general(a, b, (((0,), (0,)), ((), ())), preferred_element_type=F32)


def _join_blocks(blocks_hbm, joined_ref, sems):
    n, _, C = blocks_hbm.shape
    copies = [pltpu.make_async_copy(blocks_hbm.at[k], joined_ref.at[:, k * C:(k + 1) * C], sems.at[k]) for k in range(n)]
    for cp in copies:
        cp.start()
    for cp in copies:
        cp.wait()


def _row_inverse_counts(tile, tm):
    pos = (lax.broadcasted_iota(jnp.int32, (tm, 1), 0) + tile * tm + 1).astype(F32)
    return [1.0 / jnp.minimum(pos, float(w)) for w in POOL_WINDOWS]


def _pool_from_ext(ext, a, invs, gw):
    s = ext
    outs = []
    for g, w in enumerate(POOL_WINDOWS):
        s = s[:, (gw if g else 0):]
        s = s + pltpu.roll(s, w // 2, 0)
        outs.append(s[POOL_HALO:, :gw] * invs[g] - a[:, g * gw:(g + 1) * gw])
    return outs


def _pool_fwd(h, pw, scale, g_pre, g_post, *, tm, after=None):
    T, D = h.shape
    gw = D // len(POOL_WINDOWS)
    hb = tm // POOL_HALO

    def body(h_ref, halo_ref, pw_ref, scale_ref, gpre_ref, gpost_ref, out_ref, pooled_ref, mixed_ref, ext_ref):
        i = pl.program_id(0)
        x = h_ref[...]
        a = _rms_fwd(x, gpre_ref[...])
        ah = _rms_fwd(halo_ref[...], gpre_ref[...])
        ext_ref[0:POOL_HALO, :] = jnp.where(i == 0, 0.0, ah)
        ext_ref[POOL_HALO:, :] = a
        pooled = [p.astype(BF16) for p in _pool_from_ext(ext_ref[...], a, _row_inverse_counts(i, tm), gw)]
        pooled_ref[...] = jnp.concatenate(pooled, axis=1)
        mixed = jnp.concatenate([_dot(p, pw_ref[g]) for g, p in enumerate(pooled)], axis=1)
        mixed_ref[...] = mixed.astype(BF16)
        out_ref[...] = x + _rms_fwd(mixed * scale_ref[...], gpost_ref[...])

    vec = _resident((1, D), lambda i: (0, 0))
    fn, xa, xs = _ordered(body, 6, after)
    return pl.pallas_call(
        fn, name="pool_fwd", grid=(T // tm,),
        in_specs=[pl.BlockSpec((tm, D), lambda i: (i, 0)),
                  pl.BlockSpec((POOL_HALO, D), lambda i: (jnp.maximum(i * hb - 1, 0), 0)),
                  _resident(pw.shape, lambda i: (0, 0, 0)), vec, vec, vec] + xs,
        out_specs=[pl.BlockSpec((tm, D), lambda i: (i, 0))] * 3,
        out_shape=[jax.ShapeDtypeStruct((T, D), F32), jax.ShapeDtypeStruct((T, D), BF16), jax.ShapeDtypeStruct((T, D), BF16)],
        scratch_shapes=[pltpu.VMEM((tm + POOL_HALO, D), F32)],
        compiler_params=_params(dimension_semantics=("arbitrary",)),
    )(h, h, pw, scale, g_pre, g_post, *xa)


def _pool_bwd(dh, h, pooled, mixed_pre, pw, scale, g_pre, g_post, *, tm, after=None):
    T, D = h.shape
    gw = D // len(POOL_WINDOWS)
    nt = T // tm
    n_ext = tm + POOL_HALO

    def body(dh_ref, h_ref, pooled_ref, mixed_ref, pw_ref, scale_ref, gpre_ref, gpost_ref,
             dx_ref, dpw_ref, small_ref, ext2_ref, carry_ref, dpw_acc):
        i = pl.program_id(0)
        tile = nt - 1 - i

        @pl.when(i == 0)
        def _():
            small_ref[...] = jnp.zeros_like(small_ref)
            dpw_acc[...] = jnp.zeros_like(dpw_acc)
            carry_ref[...] = jnp.zeros_like(carry_ref)

        x = h_ref[...]
        dout = dh_ref[...]
        invs = _row_inverse_counts(tile, tm)
        pooled = [pooled_ref[:, g * gw:(g + 1) * gw] for g in range(len(POOL_WINDOWS))]
        mixed_pre = mixed_ref[...].astype(F32)
        scale_v = scale_ref[...]
        dmixed, dg_post = _rms_bwd(mixed_pre * scale_v, gpost_ref[...], dout)
        small_ref[1:2, :] += dg_post
        small_ref[2:3, :] += jnp.sum(dmixed * mixed_pre, axis=0, keepdims=True)
        dpre = (dmixed * scale_v).astype(BF16)
        dpooled = []
        for g in range(len(POOL_WINDOWS)):
            dp = dpre[:, g * gw:(g + 1) * gw]
            dpw_acc[g] += _dot_tn(pooled[g], dp)
            dpooled.append(_dot_nt(dp, pw_ref[g]))
        q = jnp.concatenate([d * invs[g] for g, d in enumerate(dpooled)], axis=1)
        ext2_ref[0:tm, :] = q
        ext2_ref[tm:, :] = carry_ref[...]
        carry_ref[...] = q[0:POOL_HALO, :]
        s = ext2_ref[...]
        da = []
        for g, w in enumerate(POOL_WINDOWS):
            s = s[:, (gw if g else 0):]
            s = s + pltpu.roll(s, n_ext - w // 2, 0)
            da.append(s[0:tm, :gw] - dpooled[g])
        dx, dg_pre = _rms_bwd(x, gpre_ref[...], jnp.concatenate(da, axis=1))
        small_ref[0:1, :] += dg_pre
        dx_ref[...] = dout + dx

        @pl.when(i == nt - 1)
        def _():
            dpw_ref[...] = dpw_acc[...].astype(BF16)

    vec = _resident((1, D), lambda i: (0, 0))
    rev = lambda i: (nt - 1 - i, 0)
    fn, xa, xs = _ordered(body, 8, after)
    tile = pl.BlockSpec((tm, D), rev)
    return pl.pallas_call(
        fn, name="pool_bwd", grid=(nt,),
        in_specs=[tile, tile, tile, tile, _resident(pw.shape, lambda i: (0, 0, 0)), vec, vec, vec] + xs,
        out_specs=[tile, pl.BlockSpec(pw.shape, lambda i: (0, 0, 0)), pl.BlockSpec((8, D), lambda i: (0, 0))],
        out_shape=[jax.ShapeDtypeStruct((T, D), F32), jax.ShapeDtypeStruct(pw.shape, BF16),
                   jax.ShapeDtypeStruct((8, D), F32)],
        scratch_shapes=[pltpu.VMEM((n_ext, D), F32), pltpu.VMEM((POOL_HALO, D), F32), pltpu.VMEM(pw.shape, F32)],
        compiler_params=_params(dimension_semantics=("arbitrary",)),
    )(dh, h, pooled, mixed_pre, pw, scale, g_pre, g_post, *xa)


def _ffn_fwd(h, g_pre, g_post, wgu, wd, layer, target, *, tm, after=None):
    T, D = h.shape
    F = wd.shape[0]
    last = target is not None

    def body(*refs):
        if last:
            h_ref, gpre_ref, gpost_ref, wgu_ref, wd_ref, tgt_ref, out_ref, gu_ref, act_ref, ff_ref, loss_ref = refs
        else:
            h_ref, gpre_ref, gpost_ref, wgu_ref, wd_ref, out_ref, gu_ref, act_ref, ff_ref = refs
        x = h_ref[...]
        cb = _rms_fwd(x, gpre_ref[...]).astype(BF16)
        g = _dot_nt(cb, wgu_ref[0:F, :])
        u = _dot_nt(cb, wgu_ref[F:2 * F, :])
        gu_ref[0] = g.astype(BF16)
        gu_ref[1] = u.astype(BF16)
        act = (g * _sigmoid(g) * u).astype(BF16)
        act_ref[...] = act
        acc = _dot(act, wd_ref[...])
        ff_ref[...] = acc.astype(BF16)
        hout = x + _rms_fwd(acc, gpost_ref[...])
        if last:
            diff = hout - tgt_ref[...]
            out_ref[...] = diff * (1.0 / D)

            @pl.when(pl.program_id(0) == 0)
            def _():
                loss_ref[...] = jnp.zeros_like(loss_ref)

            loss_ref[...] += jnp.sum(diff * diff) * (0.5 / D)
        else:
            out_ref[...] = hout

    vec = _resident((1, D), lambda i: (0, 0))
    tile = pl.BlockSpec((tm, D), lambda i: (i, 0))
    in_specs = [tile, vec, vec, _resident(wgu.shape, lambda i: (0, 0)), _resident(wd.shape, lambda i: (0, 0))]
    out_specs = [tile, pl.BlockSpec((2, tm, F), lambda i: (0, i, 0)), pl.BlockSpec((tm, F), lambda i: (i, 0)), tile]
    out_shape = [jax.ShapeDtypeStruct((T, D), F32), jax.ShapeDtypeStruct((2, T, F), BF16),
                 jax.ShapeDtypeStruct((T, F), BF16), jax.ShapeDtypeStruct((T, D), BF16)]
    args = [h, g_pre, g_post, wgu, wd]
    if last:
        in_specs.append(tile)
        args.append(target)
        out_specs.append(pl.BlockSpec((8, 128), lambda i: (0, 0)))
        out_shape.append(jax.ShapeDtypeStruct((8, 128), F32))
    fn, xa, xs = _ordered(body, len(args), after)
    return pl.pallas_call(
        fn, name=f"ffn_fwd_{layer}", grid=(T // tm,), in_specs=in_specs + xs, out_specs=out_specs,
        out_shape=out_shape, compiler_params=_params(dimension_semantics=("arbitrary",)),
    )(*args, *xa)


def _ffn_up(h, g_pre, wgu, *, tm):
    T, D = h.shape
    F = wgu.shape[0] // 2

    def body(h_ref, gpre_ref, wgu_ref, gu_ref, act_ref):
        cb = _rms_fwd(h_ref[...], gpre_ref[...]).astype(BF16)
        g = _dot_nt(cb, wgu_ref[0:F, :])
        u = _dot_nt(cb, wgu_ref[F:2 * F, :])
        gu_ref[0] = g.astype(BF16)
        gu_ref[1] = u.astype(BF16)
        act_ref[...] = (g * _sigmoid(g) * u).astype(BF16)

    return pl.pallas_call(
        body, name="ffn_up_0", grid=(T // tm,),
        in_specs=[pl.BlockSpec((tm, D), lambda i: (i, 0)), _resident((1, D), lambda i: (0, 0)),
                  _resident(wgu.shape, lambda i: (0, 0))],
        out_specs=[pl.BlockSpec((2, tm, F), lambda i: (0, i, 0)), pl.BlockSpec((tm, F), lambda i: (i, 0))],
        out_shape=[jax.ShapeDtypeStruct((2, T, F), BF16), jax.ShapeDtypeStruct((T, F), BF16)],
        compiler_params=_params(dimension_semantics=("arbitrary",)),
    )(h, g_pre, wgu)


def _ffn_down(h, act, g_post, wd, *, tm):
    T, D = h.shape
    F = act.shape[1]

    def body(h_ref, act_ref, gpost_ref, wd_ref, out_ref, ff_ref):
        acc = _dot(act_ref[...], wd_ref[...])
        ff_ref[...] = acc.astype(BF16)
        out_ref[...] = h_ref[...] + _rms_fwd(acc, gpost_ref[...])

    tile = pl.BlockSpec((tm, D), lambda i: (i, 0))
    return pl.pallas_call(
        body, name="ffn_down_0", grid=(T // tm,),
        in_specs=[tile, pl.BlockSpec((tm, F), lambda i: (i, 0)), _resident((1, D), lambda i: (0, 0)),
                  _resident(wd.shape, lambda i: (0, 0))],
        out_specs=[tile, tile],
        out_shape=[jax.ShapeDtypeStruct((T, D), F32), jax.ShapeDtypeStruct((T, D), BF16)],
        compiler_params=_params(dimension_semantics=("arbitrary",)),
    )(h, act, g_post, wd)


def _ffn_bwd(dh, h, ff, gu, g_pre, g_post, wgu, wd, layer, *, tm, after=None):
    T, D = h.shape
    F = wd.shape[0]
    n_chunks = F // MXU_COLUMNS

    def body(dh_ref, h_ref, ff_ref, gu_ref, gpre_ref, gpost_ref, wgu_ref, wd_ref,
             dx_ref, dgu_ref, dff_ref, c_ref, small_ref):
        @pl.when(pl.program_id(0) == 0)
        def _():
            small_ref[...] = jnp.zeros_like(small_ref)

        dout = dh_ref[...]
        dff, dg_post = _rms_bwd(ff_ref[...].astype(F32), gpost_ref[...], dout)
        small_ref[1:2, :] += dg_post
        dffb = dff.astype(BF16)
        dff_ref[...] = dffb
        dc = jnp.zeros((tm, D), F32)
        for j in range(n_chunks + 1):
            lo, hi = j * MXU_COLUMNS, (j + 1) * MXU_COLUMNS
            if j < n_chunks:
                dact = _dot_nt(dffb, wd_ref[lo:hi, :])
            if j > 0:
                lo0 = lo - MXU_COLUMNS
                dc = dc + _dot(dgu_ref[0, :, lo0:lo], wgu_ref[lo0:lo, :]) + _dot(dgu_ref[1, :, lo0:lo], wgu_ref[F + lo0:F + lo, :])
            if j < n_chunks:
                g = gu_ref[0, :, lo:hi].astype(F32)
                u = gu_ref[1, :, lo:hi].astype(F32)
                s = _sigmoid(g)
                dgu_ref[0, :, lo:hi] = (dact * u * (s * (1.0 + g * (1.0 - s)))).astype(BF16)
                dgu_ref[1, :, lo:hi] = (dact * (g * s)).astype(BF16)
        x = h_ref[...]
        c_ref[...] = _rms_fwd(x, gpre_ref[...]).astype(BF16)
        dx, dg_pre = _rms_bwd(x, gpre_ref[...], dc)
        small_ref[0:1, :] += dg_pre
        dx_ref[...] = dout + dx

    vec = _resident((1, D), lambda i: (0, 0))
    tile = pl.BlockSpec((tm, D), lambda i: (i, 0))
    blk = pl.BlockSpec((2, tm, F), lambda i: (0, i, 0))
    fn, xa, xs = _ordered(body, 8, after)
    return pl.pallas_call(
        fn, name=f"ffn_bwd_{layer}", grid=(T // tm,),
        in_specs=[tile, tile, tile, blk, vec, vec,
                  _resident(wgu.shape, lambda i: (0, 0)), _resident(wd.shape, lambda i: (0, 0))] + xs,
        out_specs=[tile, blk, tile, tile, pl.BlockSpec((8, D), lambda i: (0, 0))],
        out_shape=[jax.ShapeDtypeStruct((T, D), F32), jax.ShapeDtypeStruct((2, T, F), BF16),
                   jax.ShapeDtypeStruct((T, D), BF16), jax.ShapeDtypeStruct((T, D), BF16),
                   jax.ShapeDtypeStruct((8, D), F32)],
        compiler_params=_params(dimension_semantics=("arbitrary",)),
    )(dh, h, ff, gu, g_pre, g_post, wgu, wd, *xa)


def _conv_fwd(h, g_pre, g_post, win, taps, wout, *, tm, after=None):
    T, D = h.shape
    nblk, cb = win.shape[0], win.shape[2]

    def body(h_ref, gpre_ref, gpost_ref, win_hbm, taps_ref, wout_ref,
             out_ref, proj_ref, y_ref, conv_ref, z_ref, proj_scr, ext_ref, carry_ref, win_ref, win_sems):
        i = pl.program_id(0)

        @pl.when(i == 0)
        def _():
            carry_ref[...] = jnp.zeros_like(carry_ref)
            _join_blocks(win_hbm, win_ref, win_sems)

        x = h_ref[...]
        a = _rms_fwd(x, gpre_ref[...]).astype(BF16)
        proj_scr[...] = _dot(a, win_ref[...])
        proj_ref[...] = proj_scr[...].astype(BF16)
        u = proj_scr[:, D:2 * D] * proj_scr[:, 2 * D:3 * D]
        ext_ref[0:CONV_HALO, :] = carry_ref[...]
        ext_ref[CONV_HALO:, :] = u
        carry_ref[...] = u[tm - CONV_HALO:, :]
        e = ext_ref[...]
        conv = (taps_ref[2:3, :] * u + taps_ref[1:2, :] * pltpu.roll(e, 1, 0)[CONV_HALO:, :]
                + taps_ref[0:1, :] * pltpu.roll(e, 2, 0)[CONV_HALO:, :])
        conv_ref[...] = conv.astype(BF16)
        z = (proj_scr[:, 0:D] * conv).astype(BF16)
        z_ref[...] = z
        y = _dot(z, wout_ref[...])
        y_ref[...] = y.astype(BF16)
        out_ref[...] = x + _rms_fwd(y, gpost_ref[...])

    vec = _resident((1, D), lambda i: (0, 0))
    tile = pl.BlockSpec((tm, D), lambda i: (i, 0))
    fn, xa, xs = _ordered(body, 6, after)
    return pl.pallas_call(
        fn, name="conv_fwd", grid=(T // tm,),
        in_specs=[tile, vec, vec, pl.BlockSpec(memory_space=pl.ANY),
                  _resident(taps.shape, lambda i: (0, 0)), _resident(wout.shape, lambda i: (0, 0))] + xs,
        out_specs=[tile, pl.BlockSpec((tm, 3 * D), lambda i: (i, 0)), tile, tile, tile],
        out_shape=[jax.ShapeDtypeStruct((T, D), F32), jax.ShapeDtypeStruct((T, 3 * D), BF16),
                   jax.ShapeDtypeStruct((T, D), BF16), jax.ShapeDtypeStruct((T, D), BF16),
                   jax.ShapeDtypeStruct((T, D), BF16)],
        scratch_shapes=[pltpu.VMEM((tm, 3 * D), F32), pltpu.VMEM((tm + CONV_HALO, D), F32),
                        pltpu.VMEM((CONV_HALO, D), F32), pltpu.VMEM((D, nblk * cb), BF16),
                        pltpu.SemaphoreType.DMA((nblk,))],
        compiler_params=_params(dimension_semantics=("arbitrary",)),
    )(h, g_pre, g_post, win, taps, wout, *xa)


def _conv_bwd(dh, h, y, proj, conv, g_pre, g_post, win, taps, wout, *, tm, after=None):
    T, D = h.shape
    nblk, cb = win.shape[0], win.shape[2]
    nt = T // tm
    n_ext = tm + CONV_HALO

    def body(dh_ref, h_ref, y_ref, proj_ref, conv_ref, gpre_ref, gpost_ref, win_hbm, taps_ref, wout_ref,
             dx_ref, dproj_ref, a_ref, dy_ref, small_ref, ext2_ref, carry_ref, win_ref, win_sems):
        i = pl.program_id(0)

        @pl.when(i == 0)
        def _():
            small_ref[...] = jnp.zeros_like(small_ref)
            carry_ref[...] = jnp.zeros_like(carry_ref)
            _join_blocks(win_hbm, win_ref, win_sems)

        dout = dh_ref[...]
        dy, dg_post = _rms_bwd(y_ref[...].astype(F32), gpost_ref[...], dout)
        small_ref[1:2, :] += dg_post
        dyb = dy.astype(BF16)
        dy_ref[...] = dyb
        dz = _dot_nt(dyb, wout_ref[...])
        bgate = proj_ref[:, 0:D].astype(F32)
        cgate = proj_ref[:, D:2 * D].astype(F32)
        v = proj_ref[:, 2 * D:3 * D].astype(F32)
        u = cgate * v
        t0, t1, t2 = taps_ref[0:1, :], taps_ref[1:2, :], taps_ref[2:3, :]
        dconv = dz * bgate
        ext2_ref[0:tm, :] = dconv
        ext2_ref[tm:, :] = carry_ref[...]
        carry_ref[...] = dconv[0:CONV_HALO, :]
        e2 = ext2_ref[...]
        ahead1 = pltpu.roll(e2, n_ext - 1, 0)[0:tm, :]
        ahead2 = pltpu.roll(e2, n_ext - 2, 0)[0:tm, :]
        small_ref[2:3, :] += jnp.sum(ahead2 * u, axis=0, keepdims=True)
        small_ref[3:4, :] += jnp.sum(ahead1 * u, axis=0, keepdims=True)
        small_ref[4:5, :] += jnp.sum(dconv * u, axis=0, keepdims=True)
        du = t2 * dconv + t1 * ahead1 + t0 * ahead2
        dproj_ref[:, 0:D] = (dz * conv_ref[...].astype(F32)).astype(BF16)
        dproj_ref[:, D:2 * D] = (du * v).astype(BF16)
        dproj_ref[:, 2 * D:3 * D] = (du * cgate).astype(BF16)
        da = _dot_nt(dproj_ref[...], win_ref[...])
        x = h_ref[...]
        a_ref[...] = _rms_fwd(x, gpre_ref[...]).astype(BF16)
        dx, dg_pre = _rms_bwd(x, gpre_ref[...], da)
        small_ref[0:1, :] += dg_pre
        dx_ref[...] = dout + dx

    vec = _resident((1, D), lambda i: (0, 0))
    rev = lambda i: (nt - 1 - i, 0)
    tile = pl.BlockSpec((tm, D), rev)
    wide = pl.BlockSpec((tm, 3 * D), rev)
    fn, xa, xs = _ordered(body, 10, after)
    return pl.pallas_call(
        fn, name="conv_bwd", grid=(nt,),
        in_specs=[tile, tile, tile, wide, tile, vec, vec, pl.BlockSpec(memory_space=pl.ANY),
                  _resident(taps.shape, lambda i: (0, 0)), _resident(wout.shape, lambda i: (0, 0))] + xs,
        out_specs=[tile, wide, tile, tile, pl.BlockSpec((8, D), lambda i: (0, 0))],
        out_shape=[jax.ShapeDtypeStruct((T, D), F32), jax.ShapeDtypeStruct((T, 3 * D), BF16),
                   jax.ShapeDtypeStruct((T, D), BF16), jax.ShapeDtypeStruct((T, D), BF16),
                   jax.ShapeDtypeStruct((8, D), F32)],
        scratch_shapes=[pltpu.VMEM((n_ext, D), F32), pltpu.VMEM((CONV_HALO, D), F32),
                        pltpu.VMEM((D, nblk * cb), BF16), pltpu.SemaphoreType.DMA((nblk,))],
        compiler_params=_params(dimension_semantics=("arbitrary",)),
    )(dh, h, y, proj, conv, g_pre, g_post, win, taps, wout, *xa)


def _wgrad(name, a, b, a_spec, b_spec, block, n_blocks, *, tk, after=None):
    T = a.shape[-2]
    nk = T // tk
    M, N = block
    m = N_DEV // n_blocks
    R = M // m

    def body(a_ref, b_ref, out_ref, land_hbm, acc_ref, stage_ref, recv_ref, send_sems, recv_sems, land_sem):
        i, k = pl.program_id(0), pl.program_id(1)
        x, y, c = lax.axis_index("x"), lax.axis_index("y"), lax.axis_index("c")

        def sent(blk, p):
            owner = blk * m + p
            q = owner // 2
            return (owner % 2) != c, pltpu.make_async_remote_copy(
                src_ref=stage_ref.at[p * R:(p + 1) * R], dst_ref=recv_ref.at[q], send_sem=send_sems.at[q],
                recv_sem=recv_sems.at[q], device_id=(x, y, 1 - c), device_id_type=MESH)

        @pl.when(jnp.logical_and(i == 0, k == 0))
        def _():
            barrier = pltpu.get_barrier_semaphore()
            pl.semaphore_signal(barrier, inc=1, device_id=(x, y, 1 - c), device_id_type=MESH)
            pl.semaphore_wait(barrier, 1)

        @pl.when(k == 0)
        def _():
            acc_ref[...] = jnp.zeros_like(acc_ref)

        acc_ref[...] += _dot_tn(a_ref[...], b_ref[...])

        @pl.when(k == nk - 1)
        def _():
            for p in range(m):
                away, copy = sent(jnp.maximum(i - 1, 0), p)

                @pl.when(jnp.logical_and(i > 0, away))
                def _():
                    copy.wait_send()

            acc = acc_ref[...]
            stage_ref[...] = acc.astype(BF16)
            for p in range(m):
                away, copy = sent(i, p)

                @pl.when(away)
                def _():
                    copy.start()

                @pl.when(jnp.logical_not(away))
                def _():
                    out_ref[(i * m + p) // 2] = stage_ref[p * R:(p + 1) * R, :]

        @pl.when(jnp.logical_and(i == n_blocks - 1, k == nk - 1))
        def _():
            for p in range(m):
                away, copy = sent(i, p)

                @pl.when(away)
                def _():
                    copy.wait_send()

            for q in range(N_DEV // 2):
                pltpu.make_async_remote_copy(
                    src_ref=stage_ref.at[0:R], dst_ref=recv_ref.at[q], send_sem=send_sems.at[q],
                    recv_sem=recv_sems.at[q], device_id=(x, y, 1 - c), device_id_type=MESH).wait_recv()
                out_ref[q] = (out_ref[q].astype(F32) + recv_ref[q].astype(F32)).astype(BF16)
            mine = pltpu.make_async_copy(out_ref.at[2 * x + y], land_hbm.at[2 * x + y], land_sem)
            mine.start()
            mine.wait()

    fn, xa, xs = _ordered(body, 2, after)
    sums = jax.ShapeDtypeStruct((N_DEV // 2, R, N), BF16)
    return pl.pallas_call(
        fn, name=name, grid=(n_blocks, nk), in_specs=[a_spec, b_spec] + xs,
        out_specs=[pl.BlockSpec((N_DEV // 2, R, N), lambda i, k: (0, 0, 0)), pl.BlockSpec(memory_space=pl.ANY)],
        out_shape=[sums, sums],
        scratch_shapes=[pltpu.VMEM(block, F32), pltpu.VMEM(block, BF16), pltpu.VMEM((N_DEV // 2, R, N), BF16),
                        pltpu.SemaphoreType.DMA((N_DEV // 2,)), pltpu.SemaphoreType.DMA((N_DEV // 2,)),
                        pltpu.SemaphoreType.DMA],
        compiler_params=_params(dimension_semantics=("arbitrary", "arbitrary"), collective_id=SIBLING_PAIR_ID),
    )(a, b, *xa)


Copy = collections.namedtuple("Copy", "mask sb src db dst sem")
Local = collections.namedtuple("Local", "sb src db dst")

HBM_SPEC = pl.BlockSpec(memory_space=pltpu.HBM)
SEM_SPEC = pl.BlockSpec(memory_space=pltpu.SEMAPHORE)
SIBLING, X_PEER, Y_PEER, DIAGONAL = 1, 4, 2, 6
OTHER_CHIPS = (X_PEER, Y_PEER, DIAGONAL)


def _whole(ref, i):
    return ref


def _lead(ref, i):
    return ref.at[i]


def _second(ref, i):
    return ref.at[:, i]


def _place():
    x, y, c = lax.axis_index("x"), lax.axis_index("y"), lax.axis_index("c")
    return (x, y, c), 4 * x + 2 * y + c


def _descriptor(cp, bufs, xyc, me, sender, send_sems, recv_sems):
    x, y, c = xyc
    flip = lambda v, bit: (1 - v) if bit else v
    return pltpu.make_async_remote_copy(
        src_ref=cp.src(bufs[cp.sb], me), dst_ref=cp.dst(bufs[cp.db], sender),
        send_sem=send_sems.at[cp.sem], recv_sem=recv_sems.at[cp.sem],
        device_id=(flip(x, cp.mask & 4), flip(y, cp.mask & 2), flip(c, cp.mask & 1)), device_id_type=MESH)


def _exchange(name, bufs, plan, local=()):
    n = len(bufs)

    def body(*refs):
        ins = refs[:n]
        send_sems, recv_sems, local_sems = refs[2 * n:]
        xyc, me = _place()
        own = [pltpu.make_async_copy(lc.src(ins[lc.sb], me), lc.dst(ins[lc.db], me), local_sems.at[i])
               for i, lc in enumerate(local)]
        sends = [_descriptor(cp, ins, xyc, me, me, send_sems, recv_sems) for cp in plan]
        for cp in own + sends:
            cp.start()
        for cp in plan:
            _descriptor(cp, ins, xyc, me, me ^ cp.mask, send_sems, recv_sems).wait_recv()
        for cp in sends:
            cp.wait_send()
        for cp in own:
            cp.wait()

    return pl.pallas_call(
        body, name=name, in_specs=[HBM_SPEC] * n, out_specs=[HBM_SPEC] * n,
        out_shape=[jax.ShapeDtypeStruct(b.shape, b.dtype) for b in bufs],
        input_output_aliases={i: i for i in range(n)},
        scratch_shapes=[pltpu.SemaphoreType.DMA((len(plan),)), pltpu.SemaphoreType.DMA((len(plan),)),
                        pltpu.SemaphoreType.DMA((max(len(local), 1),))],
    )(*bufs)


def _place_own(me, items):
    def body(me_ref, *refs):
        for src, dst in zip(refs[:len(items)], refs[len(items):]):
            dst[...] = src[...].astype(dst.dtype)

    return pl.pallas_call(
        body, name="place_own",
        grid_spec=pltpu.PrefetchScalarGridSpec(
            num_scalar_prefetch=1, grid=(1,),
            in_specs=[pl.BlockSpec(blk, functools.partial(lambda i, m, idx: idx, idx=idx)) for _, blk, idx, _, _, _, _ in items],
            out_specs=[pl.BlockSpec(oblk, functools.partial(lambda i, m, at: at(m[0]), at=at)) for *_, oblk, at in items]),
        out_shape=[jax.ShapeDtypeStruct(shape, dtype) for _, _, _, shape, dtype, _, _ in items],
        compiler_params=_params(dimension_semantics=("arbitrary",)),
    )(jnp.reshape(me, (1,)).astype(jnp.int32), *[a for a, *_ in items])


def _split_call(name, bufs, *, wait=None, wait_sems=None, start=None, local=(), after=None, token=False):
    n = len(bufs)
    n_wait = 2 if wait else 0
    n_after = 1 if after is not None else 0
    n_start = 2 if start else 0

    def body(*refs):
        ins = refs[:n]
        wsend, wrecv = refs[n:n + n_wait] if wait else (None, None)
        outs = refs[n + n_wait + n_after:]
        ssend, srecv = outs[:n_start] if start else (None, None)
        rest = outs[n_start + n:]
        xyc, me = _place()
        for cp in wait or ():
            d = _descriptor(cp, ins, xyc, me, me ^ cp.mask, wsend, wrecv)
            d.wait_send()
            d.wait_recv()
        own = [pltpu.make_async_copy(lc.src(ins[lc.sb], me), lc.dst(ins[lc.db], me), rest[-1].at[i])
               for i, lc in enumerate(local)]
        for cp in own:
            cp.start()
        for cp in start or ():
            _descriptor(cp, ins, xyc, me, me, ssend, srecv).start()
        for cp in own:
            cp.wait()
        if token:
            rest[0][...] = jnp.zeros_like(rest[0])

    args = [pltpu.with_memory_space_constraint(b, pltpu.HBM) for b in bufs]
    in_specs = [HBM_SPEC] * n
    if wait:
        args += list(wait_sems)
        in_specs += [SEM_SPEC] * 2
    if after is not None:
        args.append(after)
        in_specs.append(pl.BlockSpec(memory_space=pl.ANY))
    out_shape, out_specs = [], []
    if start:
        out_shape += [pltpu.SemaphoreType.DMA((len(start),))] * 2
        out_specs += [SEM_SPEC] * 2
    out_shape += [pltpu.HBM(b.shape, b.dtype) for b in bufs]
    out_specs += [HBM_SPEC] * n
    if token:
        out_shape.append(jax.ShapeDtypeStruct((8, 128), F32))
        out_specs.append(pl.BlockSpec(memory_space=pltpu.VMEM))
    outs = pl.pallas_call(
        body, name=name, in_specs=in_specs, out_specs=out_specs, out_shape=out_shape,
        input_output_aliases={i: n_start + i for i in range(n)},
        scratch_shapes=[pltpu.SemaphoreType.DMA((len(local),))] if local else [],
        compiler_params=pltpu.CompilerParams(has_side_effects=pltpu.SideEffectType.DATAFLOW_SIDE_EFFECTING),
    )(*args)
    sems = tuple(outs[:n_start]) if start else None
    return sems, list(outs[n_start:n_start + n]), (outs[n_start + n] if token else None)


def _adamw(w, g, m, v):
    m = ADAM_B1 * m + (1.0 - ADAM_B1) * g
    v = ADAM_B2 * v + (1.0 - ADAM_B2) * (g * g)
    m_hat = m / (1.0 - ADAM_B1 ** ADAM_STEP)
    v_hat = v / (1.0 - ADAM_B2 ** ADAM_STEP)
    delta = -ADAM_LR * (m_hat / (jnp.sqrt(v_hat) + ADAM_EPS) + ADAM_WD * w)
    return delta, m, v


def _reduce_adam(name, parts, w, m, v, *, tr, layer=None, into=None, after=None, transposed=False):
    L, R, C = w.shape
    S = parts[0].shape[0]
    tr = min(tr, R)
    n_l = L if layer is None else 1
    first = 0 if layer is None else layer

    def body(*refs):
        p_refs = refs[:n_l]
        w_ref, m_ref, v_ref = refs[n_l:n_l + 3]
        g_ref, d_ref, nm_ref, nv_ref = refs[-4:]
        for l in range(n_l):
            g = p_refs[l][0].astype(F32)
            for s in range(1, S):
                g = g + p_refs[l][s].astype(F32)
            g = g.T if transposed else g
            g_ref[l] = g
            d_ref[l], nm_ref[l], nv_ref[l] = _adamw(w_ref[l], g, m_ref[l], v_ref[l])

    blk = pl.BlockSpec((n_l, tr, C), lambda r: (first, r, 0))
    out = jax.ShapeDtypeStruct((L, R, C), F32)
    extra = list(into or []) + ([after] if after is not None else [])
    return pl.pallas_call(
        body, name=name, grid=(R // tr,),
        in_specs=[pl.BlockSpec((S, C, tr), lambda r: (0, 0, r)) if transposed else pl.BlockSpec((S, tr, C), lambda r: (0, r, 0))] * n_l
        + [blk, blk, blk]
        + [pl.BlockSpec(memory_space=pl.ANY)] * len(extra),
        out_specs=[blk] * 4, out_shape=[out] * 4,
        input_output_aliases={n_l + 3 + i: i for i in range(4)} if into else {},
        compiler_params=_params(dimension_semantics=("arbitrary",)),
    )(*parts, w, m, v, *extra)


def _small_update(me, parts, gains, taps, scale):
    dcol = gains[0].shape[1]
    gain_rows, tap_rows, scale_row, loss_row = [0, 1, 8, 9, 16, 17, 24, 25], [18, 19, 20], 2, 32

    def body(me_ref, full_ref, col_ref, wg, mg, vg, wt, mt, vt, ws, ms, vs, *outs):
        col = col_ref[0]
        g_scale = full_ref[0, scale_row:scale_row + 1, :]
        loss = full_ref[0, loss_row:loss_row + 1, 0:128]
        for d in range(1, N_DEV):
            col = col + col_ref[d]
            g_scale = g_scale + full_ref[d, scale_row:scale_row + 1, :]
            loss = loss + full_ref[d, loss_row:loss_row + 1, 0:128]
        for r, src in enumerate(gain_rows):
            outs[0][r:r + 1, :] = col[src:src + 1, :]
        for r, src in enumerate(tap_rows):
            outs[4][r] = col[src:src + 1, :]
        outs[8][...] = g_scale
        for k, (w, m, v) in enumerate(((wg, mg, vg), (wt, mt, vt), (ws, ms, vs))):
            outs[4 * k + 1][...], outs[4 * k + 2][...], outs[4 * k + 3][...] = _adamw(w[...], outs[4 * k][...], m[...], v[...])
        outs[12][...] = jnp.broadcast_to(loss, outs[12].shape)

    whole = lambda a: pl.BlockSpec(a.shape, lambda i, m: (0,) * a.ndim)
    small = [a for t in (gains, taps, scale) for a in t]
    shapes = [jax.ShapeDtypeStruct(t[0].shape, F32) for t in (gains, taps, scale) for _ in range(4)]
    shapes.append(jax.ShapeDtypeStruct((8, 128), F32))
    return pl.pallas_call(
        body, name="small_update",
        grid_spec=pltpu.PrefetchScalarGridSpec(
            num_scalar_prefetch=1, grid=(1,),
            in_specs=[whole(parts), pl.BlockSpec(parts.shape[:2] + (dcol,), lambda i, m: (0, 0, m[0]))] + [whole(a) for a in small],
            out_specs=[pl.BlockSpec(sh.shape, functools.partial(lambda i, m, n: (0,) * n, n=len(sh.shape))) for sh in shapes]),
        out_shape=shapes, compiler_params=_params(dimension_semantics=("arbitrary",)),
    )(jnp.reshape(me, (1,)).astype(jnp.int32), parts, parts, *small)


def kernel(x, norm_gains, pool_w, pool_scale, conv_in_w, conv_w, conv_out_w, ffn_gate_up_w, ffn_down_w, loss_target, m_norm_gains, m_pool_w, m_pool_scale, m_conv_in_w, m_conv_w, m_conv_out_w, m_ffn_gate_up_w, m_ffn_down_w, v_norm_gains, v_pool_w, v_pool_scale, v_conv_in_w, v_conv_w, v_conv_out_w, v_ffn_gate_up_w, v_ffn_down_w):
    T, D = x.shape[1], x.shape[2]
    tm = min(512, T)
    tm_b = min(256, T)
    tk = min(2048, T)
    n_layers = ffn_gate_up_w.shape[0]
    fb = ffn_gate_up_w.shape[2]
    fr = ffn_down_w.shape[1]
    dcol = norm_gains.shape[2]
    cb = conv_in_w.shape[2]
    gw = pool_w.shape[3]
    me = 4 * lax.axis_index("x") + 2 * lax.axis_index("y") + lax.axis_index("c")

    small_w = jnp.concatenate([norm_gains.reshape(8, dcol), jnp.pad(conv_w[0], ((0, 5), (0, 0)))], axis=0)
    every = range(1, N_DEV)
    wgu_t, m_wgu_t, v_wgu_t = (jnp.swapaxes(a, 1, 2) for a in (ffn_gate_up_w, m_ffn_gate_up_w, v_ffn_gate_up_w))
    own_lead = lambda s: lax.dynamic_update_slice(lax.empty((N_DEV,) + s.shape, s.dtype), s[None], (me,) + (0,) * s.ndim)
    lead_item = lambda a, l, dtype: (a, (None,) + a.shape[1:], (l, 0, 0), (N_DEV,) + a.shape[1:], dtype,
                                     (None,) + a.shape[1:], lambda i: (i, 0, 0))
    lands = _place_own(me, [
        (pool_w, (None,) + pool_w.shape[1:], (0, 0, 0, 0), (4, N_DEV, gw // N_DEV, gw), BF16,
         (4, None, gw // N_DEV, gw), lambda i: (0, i, 0, 0)),
        (small_w[None], (None,) + small_w.shape, (0, 0, 0), (N_DEV,) + small_w.shape, F32, (None,) + small_w.shape, lambda i: (i, 0, 0)),
        lead_item(wgu_t, 0, BF16), lead_item(ffn_down_w, 0, BF16), lead_item(conv_in_w, 0, BF16),
        lead_item(conv_out_w, 0, BF16), lead_item(wgu_t, 1, BF16), lead_item(ffn_down_w, 1, BF16)])
    n_first, n_big = 2, len(lands) - 2
    direct = ([Copy(m, 0, _second, 0, _second, m - 1) for m in every]
              + [Copy(m, 1, _lead, 1, _lead, N_DEV - 2 + m) for m in every])
    level1 = [Copy(mask, n_first + n, _lead, n_first + n, _lead, len(direct) + 4 * n + j)
              for n in range(n_big) for j, mask in enumerate((SIBLING,) + OTHER_CHIPS)]
    sems1, bufs1, _ = _split_call("gather_start", lands, start=direct + level1)
    pw_g, small_g = _split_call("gather_small_done", bufs1[:n_first], wait=direct, wait_sems=sems1)[1]
    pw = pw_g.reshape(4, gw, gw)
    small_full = jnp.swapaxes(small_g, 0, 1).reshape(16, D)
    gain = lambda l, s: small_full[4 * l + s][None, :]
    taps = small_full[8:16]

    def forward_on(name, group, after, masks=(SIBLING,) + OTHER_CHIPS, bufs=None):
        k = len(group)
        landed = [Copy(cp.mask, i, cp.src, i, cp.dst, cp.sem)
                  for i, n in enumerate(group) for cp in level1 if cp.sb == n_first + n and cp.mask in masks]
        far = [m for m in OTHER_CHIPS if m in masks]
        onward = [Copy(SIBLING, i, (lambda ref, me, m=m: ref.at[me ^ m]), i, (lambda ref, sender, m=m: ref.at[sender ^ m]),
                       len(far) * i + j) for i in range(k) for j, m in enumerate(far)]
        sems2, bufs2, tok = _split_call(name + "_forward", bufs or [bufs1[n_first + n] for n in group], wait=landed,
                                        wait_sems=sems1, start=onward, after=after, token=True)
        return (name, onward, sems2, bufs2), tok

    def arrived(state, after=None, bufs=None):
        name, onward, sems2, lands2 = state
        return _split_call(name + "_done", bufs or lands2, wait=onward, wait_sems=sems2, after=after)[1]

    h0 = x[0]
    h1, pooled, mixed_pre = _pool_fwd(h0, pw, pool_scale, gain(0, 0), gain(0, 1), tm=tm)
    near, tok = forward_on("gather_gate_up_0_near", [0], h1, masks=(SIBLING, X_PEER, Y_PEER))
    far, tok = forward_on("gather_gate_up_0_far", [0], tok, masks=(DIAGONAL,), bufs=near[3])
    (wgu0,) = arrived(far, bufs=arrived(near, bufs=far[3]))
    wgu0 = wgu0.reshape(N_DEV * fb, D)
    gu0, act0 = _ffn_up(h1, gain(0, 2), wgu0, tm=tm)
    ag_down0, tok = forward_on("gather_down_0", [1], act0)
    ag_conv, tok = forward_on("gather_conv", [2, 3], tok)
    (wd0,) = arrived(ag_down0, tok)
    wd0 = wd0.reshape(N_DEV * fr, D)
    h2, ff0 = _ffn_down(h1, act0, gain(0, 3), wd0, tm=tm)
    win_g, wout_g = arrived(ag_conv, h2)
    wout = wout_g.reshape(D, D)
    h3, proj, y, conv, z = _conv_fwd(h2, gain(1, 0), gain(1, 1), win_g, taps, wout, tm=tm)
    ag_ffn1, tok = forward_on("gather_ffn1", [4, 5], h3)
    wgu1, wd1 = arrived(ag_ffn1, tok)
    wgu1, wd1 = wgu1.reshape(N_DEV * fb, D), wd1.reshape(N_DEV * fr, D)
    dh4, gu1, act1, ff1, loss_part = _ffn_fwd(h3, gain(1, 2), gain(1, 3), wgu1, wd1, 1, loss_target[0], tm=tm)


    def scatter_start(name, pairs):
        k = len(pairs)
        sums, lands = [p[0] for p in pairs], [p[1] for p in pairs]
        plan = [Copy(m, n, (lambda ref, i, m=m: ref.at[(i ^ m) >> 1]), k + n, (lambda ref, i: ref.at[i >> 1]), 3 * n + j)
                for n in range(k) for j, m in enumerate(OTHER_CHIPS)]
        sems, bufs, tok = _split_call(name + "_start", sums + lands, start=plan, token=True)
        return (name, plan, sems, bufs), tok

    def scatter_done(state, after):
        name, plan, sems, bufs = state
        return _split_call(name + "_done", bufs, wait=plan, wait_sems=sems, after=after)[1][len(bufs) // 2:]

    seq = lambda i, k: (k, 0)
    gu_pair = pl.BlockSpec((None, tk, 2 * fb), lambda i, k: (i // 2, k, i % 2))
    act_pair = pl.BlockSpec((tk, 2 * fb), lambda i, k: (k, i))
    rows = pl.BlockSpec((tk, D), seq)
    dh3, dgu1, dff1, c1, small_f1 = _ffn_bwd(dh4, h3, ff1, gu1, gain(1, 2), gain(1, 3), wgu1, wd1, 1, tm=tm_b)
    g_wgu1 = _wgrad("wgrad_gate_up_1", dgu1, c1, gu_pair, rows, (2 * fb, D), N_DEV // 2, tk=tk)
    g_wd1 = _wgrad("wgrad_down_1", act1, dff1, act_pair, rows, (2 * fb, D), N_DEV // 4, tk=tk)
    rs_ffn1, tok = scatter_start("scatter_ffn1", [g_wgu1, g_wd1])
    dh2, dproj, a1, dy, small_c = _conv_bwd(dh3, h2, y, proj, conv, gain(1, 0), gain(1, 1), win_g, taps, wout, tm=tm, after=tok)
    g_win = _wgrad("wgrad_conv_in", dproj, a1, pl.BlockSpec((tk, 2 * cb), lambda i, k: (k, i)), rows, (2 * cb, D), N_DEV // 2,
                   tk=tk)
    g_wout = _wgrad("wgrad_conv_out", z, dy, rows, rows, (D, D), 1, tk=tk)
    rs_conv, tok = scatter_start("scatter_conv", [g_win, g_wout])
    dh1, dgu0, dff0, c0, small_f0 = _ffn_bwd(dh2, h1, ff0, gu0, gain(0, 2), gain(0, 3), wgu0, wd0, 0, tm=tm_b, after=tok)
    g_wgu0 = _wgrad("wgrad_gate_up_0", dgu0, c0, gu_pair, rows, (2 * fb, D), N_DEV // 2, tk=tk)
    rs_wgu0, tok = scatter_start("scatter_gate_up_0", [g_wgu0])
    g_wd0 = _wgrad("wgrad_down_0", act0, dff0, act_pair, rows, (2 * fb, D), N_DEV // 4, tk=tk, after=tok)
    rs_wd0, tok = scatter_start("scatter_down_0", [g_wd0])
    grad_x, g_pw, small_p = _pool_bwd(dh1, h0, pooled, mixed_pre, pw, pool_scale, gain(0, 0), gain(0, 1), tm=tm, after=tok)

    loss_rows = jnp.broadcast_to(loss_part[0:1, 0:1], (8, D))
    small_part = jnp.concatenate([small_p, small_f0, small_c, small_f1, loss_rows], axis=0)
    g_pw = g_pw.reshape(4, N_DEV, gw // N_DEV, gw)
    pw_land = lax.dynamic_update_slice(lax.empty(g_pw.shape, BF16), lax.dynamic_slice_in_dim(g_pw, me, 1, 1), (0, me, 0, 0))
    last = ([Copy(m, 0, (lambda ref, i, m=m: ref.at[:, i ^ m]), 2, _second, m - 1) for m in every]
            + [Copy(m, 1, _whole, 3, _lead, N_DEV - 2 + m) for m in every])
    sems_l, bufs_l, tok = _split_call("scatter_small_start", [g_pw, small_part, pw_land, own_lead(small_part)],
                                      start=last, token=True)

    (r_wgu1, r_wd1), (r_win, r_wout) = scatter_done(rs_ffn1, tok), scatter_done(rs_conv, tok)
    o_win = _reduce_adam("adam_conv_in", [r_win], conv_in_w, m_conv_in_w, v_conv_in_w, tr=256, transposed=True)
    o_wout = _reduce_adam("adam_conv_out", [r_wout], conv_out_w, m_conv_out_w, v_conv_out_w, tr=128, after=o_win[0])
    o_wgu = _reduce_adam("adam_gate_up_1", [r_wgu1], wgu_t, m_wgu_t, v_wgu_t, tr=176, layer=1, after=o_wout[0])
    o_wd = _reduce_adam("adam_down_1", [r_wd1], ffn_down_w, m_ffn_down_w, v_ffn_down_w, tr=176, layer=1, after=o_wgu[0])
    r_pw, r_small = _split_call("scatter_small_done", bufs_l, wait=last, wait_sems=sems_l, after=o_wd[0])[1][2:]
    (r_wgu0,), (r_wd0,) = scatter_done(rs_wgu0, r_small), scatter_done(rs_wd0, r_small)
    o_wgu = _reduce_adam("adam_gate_up_0", [r_wgu0], wgu_t, m_wgu_t, v_wgu_t, tr=176, layer=0, into=o_wgu)
    o_wgu = [jnp.swapaxes(o, 1, 2) for o in o_wgu]
    o_wd = _reduce_adam("adam_down_0", [r_wd0], ffn_down_w, m_ffn_down_w, v_ffn_down_w, tr=176, layer=0, into=o_wd)
    o_pw = _reduce_adam("adam_pool_w", [r_pw[g] for g in range(4)], pool_w[0], m_pool_w[0], v_pool_w[0], tr=32)
    o_small = _small_update(
        me, r_small,
        (norm_gains.reshape(8, dcol), m_norm_gains.reshape(8, dcol), v_norm_gains.reshape(8, dcol)),
        tuple(jnp.swapaxes(a, 0, 1) for a in (conv_w, m_conv_w, v_conv_w)), (pool_scale, m_pool_scale, v_pool_scale))
    o_gain, o_taps, o_scale, loss = o_small[0:4], o_small[4:8], o_small[8:12], o_small[12][0, 0]

    gshape = norm_gains.shape
    per = lambda k: (o_gain[k].reshape(gshape), o_pw[k][None], o_scale[k], o_win[k], jnp.swapaxes(o_taps[k], 0, 1),
                     o_wout[k], o_wgu[k], o_wd[k])
    return (loss, grad_x[None], *per(0), *per(1), *per(2), *per(3))
```

```python
import collections
import functools

import jax
import jax.numpy as jnp
from jax import lax
from jax.experimental import pallas as pl
from jax.experimental.pallas import tpu as pltpu

N_DEV = 8
RMS_EPS = 1e-6
POOL_WINDOWS = (2, 4, 8, 16)
POOL_HALO = 16
CONV_HALO = 16
ADAM_LR, ADAM_B1, ADAM_B2, ADAM_EPS, ADAM_WD, ADAM_STEP = 0.001, 0.9, 0.999, 1e-08, 0.01, 10

VMEM_LIMIT = 56 * 2**20
MXU_COLUMNS = 256
BF16 = jnp.bfloat16
F32 = jnp.float32
MESH = pl.DeviceIdType.MESH
SIBLING_PAIR_ID = 0


def _params(**kw):
    return pltpu.CompilerParams(vmem_limit_bytes=VMEM_LIMIT, **kw)


def _resident(shape, index_map):
    return pl.BlockSpec(shape, index_map, pipeline_mode=pl.Buffered(1))


def _ordered(body, n_in, after):
    if after is None:
        return functools.partial(body), [], []
    return (lambda *refs: body(*refs[:n_in], *refs[n_in + 1:])), [after], [pl.BlockSpec(memory_space=pl.ANY)]


def _rms_fwd(x, g):
    r = lax.rsqrt(jnp.mean(x * x, axis=-1, keepdims=True) + RMS_EPS)
    return x * r * g


def _rms_bwd(x, g, dy):
    r = lax.rsqrt(jnp.mean(x * x, axis=-1, keepdims=True) + RMS_EPS)
    xhat = x * r
    dg = jnp.sum(dy * xhat, axis=0, keepdims=True)
    t = dy * g
    dx = r * (t - xhat * jnp.mean(t * xhat, axis=-1, keepdims=True))
    return dx, dg


def _sigmoid(x):
    return 0.5 * jnp.tanh(0.5 * x) + 0.5


def _dot(a, b):
    return jnp.dot(a, b, preferred_element_type=F32)


def _dot_nt(a, b):
    return lax.dot_general(a, b, (((1,), (1,)), ((), ())), preferred_element_type=F32)


def _dot_tn(a, b):
    return lax.dot_general(a, b, (((0,), (0,)), ((), ())), preferred_element_type=F32)


def _join_blocks(blocks_hbm, joined_ref, sems):
    n, _, C = blocks_hbm.shape
    copies = [pltpu.make_async_copy(blocks_hbm.at[k], joined_ref.at[:, k * C:(k + 1) * C], sems.at[k]) for k in range(n)]
    for cp in copies:
        cp.start()
    for cp in copies:
        cp.wait()


def _row_inverse_counts(tile, tm):
    pos = (lax.broadcasted_iota(jnp.int32, (tm, 1), 0) + tile * tm + 1).astype(F32)
    return [1.0 / jnp.minimum(pos, float(w)) for w in POOL_WINDOWS]


def _pool_from_ext(ext, a, invs, gw):
    s = ext
    outs = []
    for g, w in enumerate(POOL_WINDOWS):
        s = s[:, (gw if g else 0):]
        s = s + pltpu.roll(s, w // 2, 0)
        outs.append(s[POOL_HALO:, :gw] * invs[g] - a[:, g * gw:(g + 1) * gw])
    return outs


def _pool_fwd(h, pw, scale, g_pre, g_post, *, tm, after=None):
    T, D = h.shape
    gw = D // len(POOL_WINDOWS)
    hb = tm // POOL_HALO

    def body(h_ref, halo_ref, pw_ref, scale_ref, gpre_ref, gpost_ref, out_ref, pooled_ref, mixed_ref, ext_ref):
        i = pl.program_id(0)
        x = h_ref[...]
        a = _rms_fwd(x, gpre_ref[...])
        ah = _rms_fwd(halo_ref[...], gpre_ref[...])
        ext_ref[0:POOL_HALO, :] = jnp.where(i == 0, 0.0, ah)
        ext_ref[POOL_HALO:, :] = a
        pooled = [p.astype(BF16) for p in _pool_from_ext(ext_ref[...], a, _row_inverse_counts(i, tm), gw)]
        pooled_ref[...] = jnp.concatenate(pooled, axis=1)
        mixed = jnp.concatenate([_dot(p, pw_ref[g]) for g, p in enumerate(pooled)], axis=1)
        mixed_ref[...] = mixed.astype(BF16)
        out_ref[...] = x + _rms_fwd(mixed * scale_ref[...], gpost_ref[...])

    vec = _resident((1, D), lambda i: (0, 0))
    fn, xa, xs = _ordered(body, 6, after)
    return pl.pallas_call(
        fn, name="pool_fwd", grid=(T // tm,),
        in_specs=[pl.BlockSpec((tm, D), lambda i: (i, 0)),
                  pl.BlockSpec((POOL_HALO, D), lambda i: (jnp.maximum(i * hb - 1, 0), 0)),
                  _resident(pw.shape, lambda i: (0, 0, 0)), vec, vec, vec] + xs,
        out_specs=[pl.BlockSpec((tm, D), lambda i: (i, 0))] * 3,
        out_shape=[jax.ShapeDtypeStruct((T, D), F32), jax.ShapeDtypeStruct((T, D), BF16), jax.ShapeDtypeStruct((T, D), BF16)],
        scratch_shapes=[pltpu.VMEM((tm + POOL_HALO, D), F32)],
        compiler_params=_params(dimension_semantics=("arbitrary",)),
    )(h, h, pw, scale, g_pre, g_post, *xa)


def _pool_bwd(dh, h, pooled, mixed_pre, pw, scale, g_pre, g_post, *, tm, after=None):
    T, D = h.shape
    gw = D // len(POOL_WINDOWS)
    nt = T // tm
    n_ext = tm + POOL_HALO

    def body(dh_ref, h_ref, pooled_ref, mixed_ref, pw_ref, scale_ref, gpre_ref, gpost_ref,
             dx_ref, dpw_ref, small_ref, ext2_ref, carry_ref, dpw_acc):
        i = pl.program_id(0)
        tile = nt - 1 - i

        @pl.when(i == 0)
        def _():
            small_ref[...] = jnp.zeros_like(small_ref)
            dpw_acc[...] = jnp.zeros_like(dpw_acc)
            carry_ref[...] = jnp.zeros_like(carry_ref)

        x = h_ref[...]
        dout = dh_ref[...]
        invs = _row_inverse_counts(tile, tm)
        pooled = [pooled_ref[:, g * gw:(g + 1) * gw] for g in range(len(POOL_WINDOWS))]
        mixed_pre = mixed_ref[...].astype(F32)
        scale_v = scale_ref[...]
        dmixed, dg_post = _rms_bwd(mixed_pre * scale_v, gpost_ref[...], dout)
        small_ref[1:2, :] += dg_post
        small_ref[2:3, :] += jnp.sum(dmixed * mixed_pre, axis=0, keepdims=True)
        dpre = (dmixed * scale_v).astype(BF16)
        dpooled = []
        for g in range(len(POOL_WINDOWS)):
            dp = dpre[:, g * gw:(g + 1) * gw]
            dpw_acc[g] += _dot_tn(pooled[g], dp)
            dpooled.append(_dot_nt(dp, pw_ref[g]))
        q = jnp.concatenate([d * invs[g] for g, d in enumerate(dpooled)], axis=1)
        ext2_ref[0:tm, :] = q
        ext2_ref[tm:, :] = carry_ref[...]
        carry_ref[...] = q[0:POOL_HALO, :]
        s = ext2_ref[...]
        da = []
        for g, w in enumerate(POOL_WINDOWS):
            s = s[:, (gw if g else 0):]
            s = s + pltpu.roll(s, n_ext - w // 2, 0)
            da.append(s[0:tm, :gw] - dpooled[g])
        dx, dg_pre = _rms_bwd(x, gpre_ref[...], jnp.concatenate(da, axis=1))
        small_ref[0:1, :] += dg_pre
        dx_ref[...] = dout + dx

        @pl.when(i == nt - 1)
        def _():
            dpw_ref[...] = dpw_acc[...].astype(BF16)

    vec = _resident((1, D), lambda i: (0, 0))
    rev = lambda i: (nt - 1 - i, 0)
    fn, xa, xs = _ordered(body, 8, after)
    tile = pl.BlockSpec((tm, D), rev)
    return pl.pallas_call(
        fn, name="pool_bwd", grid=(nt,),
        in_specs=[tile, tile, tile, tile, _resident(pw.shape, lambda i: (0, 0, 0)), vec, vec, vec] + xs,
        out_specs=[tile, pl.BlockSpec(pw.shape, lambda i: (0, 0, 0)), pl.BlockSpec((8, D), lambda i: (0, 0))],
        out_shape=[jax.ShapeDtypeStruct((T, D), F32), jax.ShapeDtypeStruct(pw.shape, BF16),
                   jax.ShapeDtypeStruct((8, D), F32)],
        scratch_shapes=[pltpu.VMEM((n_ext, D), F32), pltpu.VMEM((POOL_HALO, D), F32), pltpu.VMEM(pw.shape, F32)],
        compiler_params=_params(dimension_semantics=("arbitrary",)),
    )(dh, h, pooled, mixed_pre, pw, scale, g_pre, g_post, *xa)


def _ffn_fwd(h, g_pre, g_post, wgu, wd, layer, target, *, tm, after=None):
    T, D = h.shape
    F = wd.shape[0]
    last = target is not None

    def body(*refs):
        if last:
            h_ref, gpre_ref, gpost_ref, wgu_ref, wd_ref, tgt_ref, out_ref, gu_ref, act_ref, ff_ref, loss_ref = refs
        else:
            h_ref, gpre_ref, gpost_ref, wgu_ref, wd_ref, out_ref, gu_ref, act_ref, ff_ref = refs
        x = h_ref[...]
        cb = _rms_fwd(x, gpre_ref[...]).astype(BF16)
        g = _dot_nt(cb, wgu_ref[0:F, :])
        u = _dot_nt(cb, wgu_ref[F:2 * F, :])
        gu_ref[0] = g.astype(BF16)
        gu_ref[1] = u.astype(BF16)
        act = (g * _sigmoid(g) * u).astype(BF16)
        act_ref[...] = act
        acc = _dot(act, wd_ref[...])
        ff_ref[...] = acc.astype(BF16)
        hout = x + _rms_fwd(acc, gpost_ref[...])
        if last:
            diff = hout - tgt_ref[...]
            out_ref[...] = diff * (1.0 / D)

            @pl.when(pl.program_id(0) == 0)
            def _():
                loss_ref[...] = jnp.zeros_like(loss_ref)

            loss_ref[...] += jnp.sum(diff * diff) * (0.5 / D)
        else:
            out_ref[...] = hout

    vec = _resident((1, D), lambda i: (0, 0))
    tile = pl.BlockSpec((tm, D), lambda i: (i, 0))
    in_specs = [tile, vec, vec, _resident(wgu.shape, lambda i: (0, 0)), _resident(wd.shape, lambda i: (0, 0))]
    out_specs = [tile, pl.BlockSpec((2, tm, F), lambda i: (0, i, 0)), pl.BlockSpec((tm, F), lambda i: (i, 0)), tile]
    out_shape = [jax.ShapeDtypeStruct((T, D), F32), jax.ShapeDtypeStruct((2, T, F), BF16),
                 jax.ShapeDtypeStruct((T, F), BF16), jax.ShapeDtypeStruct((T, D), BF16)]
    args = [h, g_pre, g_post, wgu, wd]
    if last:
        in_specs.append(tile)
        args.append(target)
        out_specs.append(pl.BlockSpec((8, 128), lambda i: (0, 0)))
        out_shape.append(jax.ShapeDtypeStruct((8, 128), F32))
    fn, xa, xs = _ordered(body, len(args), after)
    return pl.pallas_call(
        fn, name=f"ffn_fwd_{layer}", grid=(T // tm,), in_specs=in_specs + xs, out_specs=out_specs,
        out_shape=out_shape, compiler_params=_params(dimension_semantics=("arbitrary",)),
    )(*args, *xa)


def _ffn_up(h, g_pre, wgu, *, tm):
    T, D = h.shape
    F = wgu.shape[0] // 2

    def body(h_ref, gpre_ref, wgu_ref, gu_ref, act_ref):
        cb = _rms_fwd(h_ref[...], gpre_ref[...]).astype(BF16)
        g = _dot_nt(cb, wgu_ref[0:F, :])
        u = _dot_nt(cb, wgu_ref[F:2 * F, :])
        gu_ref[0] = g.astype(BF16)
        gu_ref[1] = u.astype(BF16)
        act_ref[...] = (g * _sigmoid(g) * u).astype(BF16)

    return pl.pallas_call(
        body, name="ffn_up_0", grid=(T // tm,),
        in_specs=[pl.BlockSpec((tm, D), lambda i: (i, 0)), _resident((1, D), lambda i: (0, 0)),
                  _resident(wgu.shape, lambda i: (0, 0))],
        out_specs=[pl.BlockSpec((2, tm, F), lambda i: (0, i, 0)), pl.BlockSpec((tm, F), lambda i: (i, 0))],
        out_shape=[jax.ShapeDtypeStruct((2, T, F), BF16), jax.ShapeDtypeStruct((T, F), BF16)],
        compiler_params=_params(dimension_semantics=("arbitrary",)),
    )(h, g_pre, wgu)


def _ffn_down(h, act, g_post, wd, *, tm):
    T, D = h.shape
    F = act.shape[1]

    def body(h_ref, act_ref, gpost_ref, wd_ref, out_ref, ff_ref):
        acc = _dot(act_ref[...], wd_ref[...])
        ff_ref[...] = acc.astype(BF16)
        out_ref[...] = h_ref[...] + _rms_fwd(acc, gpost_ref[...])

    tile = pl.BlockSpec((tm, D), lambda i: (i, 0))
    return pl.pallas_call(
        body, name="ffn_down_0", grid=(T // tm,),
        in_specs=[tile, pl.BlockSpec((tm, F), lambda i: (i, 0)), _resident((1, D), lambda i: (0, 0)),
                  _resident(wd.shape, lambda i: (0, 0))],
        out_specs=[tile, tile],
        out_shape=[jax.ShapeDtypeStruct((T, D), F32), jax.ShapeDtypeStruct((T, D), BF16)],
        compiler_params=_params(dimension_semantics=("arbitrary",)),
    )(h, act, g_post, wd)


def _ffn_bwd(dh, h, ff, gu, g_pre, g_post, wgu, wd, layer, *, tm, after=None):
    T, D = h.shape
    F = wd.shape[0]
    n_chunks = F // MXU_COLUMNS

    def body(dh_ref, h_ref, ff_ref, gu_ref, gpre_ref, gpost_ref, wgu_ref, wd_ref,
             dx_ref, dgu_ref, dff_ref, c_ref, small_ref):
        @pl.when(pl.program_id(0) == 0)
        def _():
            small_ref[...] = jnp.zeros_like(small_ref)

        dout = dh_ref[...]
        dff, dg_post = _rms_bwd(ff_ref[...].astype(F32), gpost_ref[...], dout)
        small_ref[1:2, :] += dg_post
        dffb = dff.astype(BF16)
        dff_ref[...] = dffb
        dc = jnp.zeros((tm, D), F32)
        for j in range(n_chunks + 1):
            lo, hi = j * MXU_COLUMNS, (j + 1) * MXU_COLUMNS
            if j < n_chunks:
                dact = _dot_nt(dffb, wd_ref[lo:hi, :])
            if j > 0:
                lo0 = lo - MXU_COLUMNS
                dc = dc + _dot(dgu_ref[0, :, lo0:lo], wgu_ref[lo0:lo, :]) + _dot(dgu_ref[1, :, lo0:lo], wgu_ref[F + lo0:F + lo, :])
            if j < n_chunks:
                g = gu_ref[0, :, lo:hi].astype(F32)
                u = gu_ref[1, :, lo:hi].astype(F32)
                s = _sigmoid(g)
                dgu_ref[0, :, lo:hi] = (dact * u * (s * (1.0 + g * (1.0 - s)))).astype(BF16)
                dgu_ref[1, :, lo:hi] = (dact * (g * s)).astype(BF16)
        x = h_ref[...]
        c_ref[...] = _rms_fwd(x, gpre_ref[...]).astype(BF16)
        dx, dg_pre = _rms_bwd(x, gpre_ref[...], dc)
        small_ref[0:1, :] += dg_pre
        dx_ref[...] = dout + dx

    vec = _resident((1, D), lambda i: (0, 0))
    tile = pl.BlockSpec((tm, D), lambda i: (i, 0))
    blk = pl.BlockSpec((2, tm, F), lambda i: (0, i, 0))
    fn, xa, xs = _ordered(body, 8, after)
    return pl.pallas_call(
        fn, name=f"ffn_bwd_{layer}", grid=(T // tm,),
        in_specs=[tile, tile, tile, blk, vec, vec,
                  _resident(wgu.shape, lambda i: (0, 0)), _resident(wd.shape, lambda i: (0, 0))] + xs,
        out_specs=[tile, blk, tile, tile, pl.BlockSpec((8, D), lambda i: (0, 0))],
        out_shape=[jax.ShapeDtypeStruct((T, D), F32), jax.ShapeDtypeStruct((2, T, F), BF16),
                   jax.ShapeDtypeStruct((T, D), BF16), jax.ShapeDtypeStruct((T, D), BF16),
                   jax.ShapeDtypeStruct((8, D), F32)],
        compiler_params=_params(dimension_semantics=("arbitrary",)),
    )(dh, h, ff, gu, g_pre, g_post, wgu, wd, *xa)


def _conv_fwd(h, g_pre, g_post, win, taps, wout, *, tm, after=None):
    T, D = h.shape
    nblk, cb = win.shape[0], win.shape[2]

    def body(h_ref, gpre_ref, gpost_ref, win_hbm, taps_ref, wout_ref,
             out_ref, proj_ref, y_ref, conv_ref, z_ref, proj_scr, ext_ref, carry_ref, win_ref, win_sems):
        i = pl.program_id(0)

        @pl.when(i == 0)
        def _():
            carry_ref[...] = jnp.zeros_like(carry_ref)
            _join_blocks(win_hbm, win_ref, win_sems)

        x = h_ref[...]
        a = _rms_fwd(x, gpre_ref[...]).astype(BF16)
        proj_scr[...] = _dot(a, win_ref[...])
        proj_ref[...] = proj_scr[...].astype(BF16)
        u = proj_scr[:, D:2 * D] * proj_scr[:, 2 * D:3 * D]
        ext_ref[0:CONV_HALO, :] = carry_ref[...]
        ext_ref[CONV_HALO:, :] = u
        carry_ref[...] = u[tm - CONV_HALO:, :]
        e = ext_ref[...]
        conv = (taps_ref[2:3, :] * u + taps_ref[1:2, :] * pltpu.roll(e, 1, 0)[CONV_HALO:, :]
                + taps_ref[0:1, :] * pltpu.roll(e, 2, 0)[CONV_HALO:, :])
        conv_ref[...] = conv.astype(BF16)
        z = (proj_scr[:, 0:D] * conv).astype(BF16)
        z_ref[...] = z
        y = _dot(z, wout_ref[...])
        y_ref[...] = y.astype(BF16)
        out_ref[...] = x + _rms_fwd(y, gpost_ref[...])

    vec = _resident((1, D), lambda i: (0, 0))
    tile = pl.BlockSpec((tm, D), lambda i: (i, 0))
    fn, xa, xs = _ordered(body, 6, after)
    return pl.pallas_call(
        fn, name="conv_fwd", grid=(T // tm,),
        in_specs=[tile, vec, vec, pl.BlockSpec(memory_space=pl.ANY),
                  _resident(taps.shape, lambda i: (0, 0)), _resident(wout.shape, lambda i: (0, 0))] + xs,
        out_specs=[tile, pl.BlockSpec((tm, 3 * D), lambda i: (i, 0)), tile, tile, tile],
        out_shape=[jax.ShapeDtypeStruct((T, D), F32), jax.ShapeDtypeStruct((T, 3 * D), BF16),
                   jax.ShapeDtypeStruct((T, D), BF16), jax.ShapeDtypeStruct((T, D), BF16),
                   jax.ShapeDtypeStruct((T, D), BF16)],
        scratch_shapes=[pltpu.VMEM((tm, 3 * D), F32), pltpu.VMEM((tm + CONV_HALO, D), F32),
                        pltpu.VMEM((CONV_HALO, D), F32), pltpu.VMEM((D, nblk * cb), BF16),
                        pltpu.SemaphoreType.DMA((nblk,))],
        compiler_params=_params(dimension_semantics=("arbitrary",)),
    )(h, g_pre, g_post, win, taps, wout, *xa)


def _conv_bwd(dh, h, y, proj, conv, g_pre, g_post, win, taps, wout, *, tm, after=None):
    T, D = h.shape
    nblk, cb = win.shape[0], win.shape[2]
    nt = T // tm
    n_ext = tm + CONV_HALO

    def body(dh_ref, h_ref, y_ref, proj_ref, conv_ref, gpre_ref, gpost_ref, win_hbm, taps_ref, wout_ref,
             dx_ref, dproj_ref, a_ref, dy_ref, small_ref, ext2_ref, carry_ref, win_ref, win_sems):
        i = pl.program_id(0)

        @pl.when(i == 0)
        def _():
            small_ref[...] = jnp.zeros_like(small_ref)
            carry_ref[...] = jnp.zeros_like(carry_ref)
            _join_blocks(win_hbm, win_ref, win_sems)

        dout = dh_ref[...]
        dy, dg_post = _rms_bwd(y_ref[...].astype(F32), gpost_ref[...], dout)
        small_ref[1:2, :] += dg_post
        dyb = dy.astype(BF16)
        dy_ref[...] = dyb
        dz = _dot_nt(dyb, wout_ref[...])
        bgate = proj_ref[:, 0:D].astype(F32)
        cgate = proj_ref[:, D:2 * D].astype(F32)
        v = proj_ref[:, 2 * D:3 * D].astype(F32)
        u = cgate * v
        t0, t1, t2 = taps_ref[0:1, :], taps_ref[1:2, :], taps_ref[2:3, :]
        dconv = dz * bgate
        ext2_ref[0:tm, :] = dconv
        ext2_ref[tm:, :] = carry_ref[...]
        carry_ref[...] = dconv[0:CONV_HALO, :]
        e2 = ext2_ref[...]
        ahead1 = pltpu.roll(e2, n_ext - 1, 0)[0:tm, :]
        ahead2 = pltpu.roll(e2, n_ext - 2, 0)[0:tm, :]
        small_ref[2:3, :] += jnp.sum(ahead2 * u, axis=0, keepdims=True)
        small_ref[3:4, :] += jnp.sum(ahead1 * u, axis=0, keepdims=True)
        small_ref[4:5, :] += jnp.sum(dconv * u, axis=0, keepdims=True)
        du = t2 * dconv + t1 * ahead1 + t0 * ahead2
        dproj_ref[:, 0:D] = (dz * conv_ref[...].astype(F32)).astype(BF16)
        dproj_ref[:, D:2 * D] = (du * v).astype(BF16)
        dproj_ref[:, 2 * D:3 * D] = (du * cgate).astype(BF16)
        da = _dot_nt(dproj_ref[...], win_ref[...])
        x = h_ref[...]
        a_ref[...] = _rms_fwd(x, gpre_ref[...]).astype(BF16)
        dx, dg_pre = _rms_bwd(x, gpre_ref[...], da)
        small_ref[0:1, :] += dg_pre
        dx_ref[...] = dout + dx

    vec = _resident((1, D), lambda i: (0, 0))
    rev = lambda i: (nt - 1 - i, 0)
    tile = pl.BlockSpec((tm, D), rev)
    wide = pl.BlockSpec((tm, 3 * D), rev)
    fn, xa, xs = _ordered(body, 10, after)
    return pl.pallas_call(
        fn, name="conv_bwd", grid=(nt,),
        in_specs=[tile, tile, tile, wide, tile, vec, vec, pl.BlockSpec(memory_space=pl.ANY),
                  _resident(taps.shape, lambda i: (0, 0)), _resident(wout.shape, lambda i: (0, 0))] + xs,
        out_specs=[tile, wide, tile, tile, pl.BlockSpec((8, D), lambda i: (0, 0))],
        out_shape=[jax.ShapeDtypeStruct((T, D), F32), jax.ShapeDtypeStruct((T, 3 * D), BF16),
                   jax.ShapeDtypeStruct((T, D), BF16), jax.ShapeDtypeStruct((T, D), BF16),
                   jax.ShapeDtypeStruct((8, D), F32)],
        scratch_shapes=[pltpu.VMEM((n_ext, D), F32), pltpu.VMEM((CONV_HALO, D), F32),
                        pltpu.VMEM((D, nblk * cb), BF16), pltpu.SemaphoreType.DMA((nblk,))],
        compiler_params=_params(dimension_semantics=("arbitrary",)),
    )(dh, h, y, proj, conv, g_pre, g_post, win, taps, wout, *xa)


def _wgrad(name, a, b, a_spec, b_spec, block, n_blocks, *, tk, after=None):
    T = a.shape[-2]
    nk = T // tk
    M, N = block
    m = N_DEV // n_blocks
    R = M // m

    def body(a_ref, b_ref, out_ref, land_hbm, acc_ref, stage_ref, recv_ref, send_sems, recv_sems, land_sem):
        i, k = pl.program_id(0), pl.program_id(1)
        x, y, c = lax.axis_index("x"), lax.axis_index("y"), lax.axis_index("c")

        def sent(blk, p):
            owner = blk * m + p
            q = owner // 2
            return (owner % 2) != c, pltpu.make_async_remote_copy(
                src_ref=stage_ref.at[p * R:(p + 1) * R], dst_ref=recv_ref.at[q], send_sem=send_sems.at[q],
                recv_sem=recv_sems.at[q], device_id=(x, y, 1 - c), device_id_type=MESH)

        @pl.when(jnp.logical_and(i == 0, k == 0))
        def _():
            barrier = pltpu.get_barrier_semaphore()
            pl.semaphore_signal(barrier, inc=1, device_id=(x, y, 1 - c), device_id_type=MESH)
            pl.semaphore_wait(barrier, 1)

        @pl.when(k == 0)
        def _():
            acc_ref[...] = jnp.zeros_like(acc_ref)

        acc_ref[...] += _dot_tn(a_ref[...], b_ref[...])

        @pl.when(k == nk - 1)
        def _():
            for p in range(m):
                away, copy = sent(jnp.maximum(i - 1, 0), p)

                @pl.when(jnp.logical_and(i > 0, away))
                def _():
                    copy.wait_send()

            acc = acc_ref[...]
            stage_ref[...] = acc.astype(BF16)
            for p in range(m):
                away, copy = sent(i, p)

                @pl.when(away)
                def _():
                    copy.start()

                @pl.when(jnp.logical_not(away))
                def _():
                    out_ref[(i * m + p) // 2] = stage_ref[p * R:(p + 1) * R, :]

        @pl.when(jnp.logical_and(i == n_blocks - 1, k == nk - 1))
        def _():
            for p in range(m):
                away, copy = sent(i, p)

                @pl.when(away)
                def _():
                    copy.wait_send()

            for q in range(N_DEV // 2):
                pltpu.make_async_remote_copy(
                    src_ref=stage_ref.at[0:R], dst_ref=recv_ref.at[q], send_sem=send_sems.at[q],
                    recv_sem=recv_sems.at[q], device_id=(x, y, 1 - c), device_id_type=MESH).wait_recv()
                out_ref[q] = (out_ref[q].astype(F32) + recv_ref[q].astype(F32)).astype(BF16)
            mine = pltpu.make_async_copy(out_ref.at[2 * x + y], land_hbm.at[2 * x + y], land_sem)
            mine.start()
            mine.wait()

    fn, xa, xs = _ordered(body, 2, after)
    sums = jax.ShapeDtypeStruct((N_DEV // 2, R, N), BF16)
    return pl.pallas_call(
        fn, name=name, grid=(n_blocks, nk), in_specs=[a_spec, b_spec] + xs,
        out_specs=[pl.BlockSpec((N_DEV // 2, R, N), lambda i, k: (0, 0, 0)), pl.BlockSpec(memory_space=pl.ANY)],
        out_shape=[sums, sums],
        scratch_shapes=[pltpu.VMEM(block, F32), pltpu.VMEM(block, BF16), pltpu.VMEM((N_DEV // 2, R, N), BF16),
                        pltpu.SemaphoreType.DMA((N_DEV // 2,)), pltpu.SemaphoreType.DMA((N_DEV // 2,)),
                        pltpu.SemaphoreType.DMA],
        compiler_params=_params(dimension_semantics=("arbitrary", "arbitrary"), collective_id=SIBLING_PAIR_ID),
    )(a, b, *xa)


Copy = collections.namedtuple("Copy", "mask sb src db dst sem")
Local = collections.namedtuple("Local", "sb src db dst")

HBM_SPEC = pl.BlockSpec(memory_space=pltpu.HBM)
SEM_SPEC = pl.BlockSpec(memory_space=pltpu.SEMAPHORE)
SIBLING, X_PEER, Y_PEER, DIAGONAL = 1, 4, 2, 6
OTHER_CHIPS = (X_PEER, Y_PEER, DIAGONAL)


def _whole(ref, i):
    return ref


def _lead(ref, i):
    return ref.at[i]


def _second(ref, i):
    return ref.at[:, i]


def _place():
    x, y, c = lax.axis_index("x"), lax.axis_index("y"), lax.axis_index("c")
    return (x, y, c), 4 * x + 2 * y + c


def _descriptor(cp, bufs, xyc, me, sender, send_sems, recv_sems):
    x, y, c = xyc
    flip = lambda v, bit: (1 - v) if bit else v
    return pltpu.make_async_remote_copy(
        src_ref=cp.src(bufs[cp.sb], me), dst_ref=cp.dst(bufs[cp.db], sender),
        send_sem=send_sems.at[cp.sem], recv_sem=recv_sems.at[cp.sem],
        device_id=(flip(x, cp.mask & 4), flip(y, cp.mask & 2), flip(c, cp.mask & 1)), device_id_type=MESH)


def _exchange(name, bufs, plan, local=()):
    n = len(bufs)

    def body(*refs):
        ins = refs[:n]
        send_sems, recv_sems, local_sems = refs[2 * n:]
        xyc, me = _place()
        own = [pltpu.make_async_copy(lc.src(ins[lc.sb], me), lc.dst(ins[lc.db], me), local_sems.at[i])
               for i, lc in enumerate(local)]
        sends = [_descriptor(cp, ins, xyc, me, me, send_sems, recv_sems) for cp in plan]
        for cp in own + sends:
            cp.start()
        for cp in plan:
            _descriptor(cp, ins, xyc, me, me ^ cp.mask, send_sems, recv_sems).wait_recv()
        for cp in sends:
            cp.wait_send()
        for cp in own:
            cp.wait()

    return pl.pallas_call(
        body, name=name, in_specs=[HBM_SPEC] * n, out_specs=[HBM_SPEC] * n,
        out_shape=[jax.ShapeDtypeStruct(b.shape, b.dtype) for b in bufs],
        input_output_aliases={i: i for i in range(n)},
        scratch_shapes=[pltpu.SemaphoreType.DMA((len(plan),)), pltpu.SemaphoreType.DMA((len(plan),)),
                        pltpu.SemaphoreType.DMA((max(len(local), 1),))],
    )(*bufs)


def _place_own(me, items):
    def body(me_ref, *refs):
        for src, dst in zip(refs[:len(items)], refs[len(items):]):
            dst[...] = src[...].astype(dst.dtype)

    return pl.pallas_call(
        body, name="place_own",
        grid_spec=pltpu.PrefetchScalarGridSpec(
            num_scalar_prefetch=1, grid=(1,),
            in_specs=[pl.BlockSpec(blk, functools.partial(lambda i, m, idx: idx, idx=idx)) for _, blk, idx, _, _, _, _ in items],
            out_specs=[pl.BlockSpec(oblk, functools.partial(lambda i, m, at: at(m[0]), at=at)) for *_, oblk, at in items]),
        out_shape=[jax.ShapeDtypeStruct(shape, dtype) for _, _, _, shape, dtype, _, _ in items],
        compiler_params=_params(dimension_semantics=("arbitrary",)),
    )(jnp.reshape(me, (1,)).astype(jnp.int32), *[a for a, *_ in items])


def _split_call(name, bufs, *, wait=None, wait_sems=None, start=None, local=(), after=None, token=False):
    n = len(bufs)
    n_wait = 2 if wait else 0
    n_after = 1 if after is not None else 0
    n_start = 2 if start else 0

    def body(*refs):
        ins = refs[:n]
        wsend, wrecv = refs[n:n + n_wait] if wait else (None, None)
        outs = refs[n + n_wait + n_after:]
        ssend, srecv = outs[:n_start] if start else (None, None)
        rest = outs[n_start + n:]
        xyc, me = _place()
        for cp in wait or ():
            d = _descriptor(cp, ins, xyc, me, me ^ cp.mask, wsend, wrecv)
            d.wait_send()
            d.wait_recv()
        own = [pltpu.make_async_copy(lc.src(ins[lc.sb], me), lc.dst(ins[lc.db], me), rest[-1].at[i])
               for i, lc in enumerate(local)]
        for cp in own:
            cp.start()
        for cp in start or ():
            _descriptor(cp, ins, xyc, me, me, ssend, srecv).start()
        for cp in own:
            cp.wait()
        if token:
            rest[0][...] = jnp.zeros_like(rest[0])

    args = [pltpu.with_memory_space_constraint(b, pltpu.HBM) for b in bufs]
    in_specs = [HBM_SPEC] * n
    if wait:
        args += list(wait_sems)
        in_specs += [SEM_SPEC] * 2
    if after is not None:
        args.append(after)
        in_specs.append(pl.BlockSpec(memory_space=pl.ANY))
    out_shape, out_specs = [], []
    if start:
        out_shape += [pltpu.SemaphoreType.DMA((len(start),))] * 2
        out_specs += [SEM_SPEC] * 2
    out_shape += [pltpu.HBM(b.shape, b.dtype) for b in bufs]
    out_specs += [HBM_SPEC] * n
    if token:
        out_shape.append(jax.ShapeDtypeStruct((8, 128), F32))
        out_specs.append(pl.BlockSpec(memory_space=pltpu.VMEM))
    outs = pl.pallas_call(
        body, name=name, in_specs=in_specs, out_specs=out_specs, out_shape=out_shape,
        input_output_aliases={i: n_start + i for i in range(n)},
        scratch_shapes=[pltpu.SemaphoreType.DMA((len(local),))] if local else [],
        compiler_params=pltpu.CompilerParams(has_side_effects=pltpu.SideEffectType.DATAFLOW_SIDE_EFFECTING),
    )(*args)
    sems = tuple(outs[:n_start]) if start else None
    return sems, list(outs[n_start:n_start + n]), (outs[n_start + n] if token else None)


def _adamw(w, g, m, v):
    m = ADAM_B1 * m + (1.0 - ADAM_B1) * g
    v = ADAM_B2 * v + (1.0 - ADAM_B2) * (g * g)
    m_hat = m / (1.0 - ADAM_B1 ** ADAM_STEP)
    v_hat = v / (1.0 - ADAM_B2 ** ADAM_STEP)
    delta = -ADAM_LR * (m_hat / (jnp.sqrt(v_hat) + ADAM_EPS) + ADAM_WD * w)
    return delta, m, v


def _reduce_adam(name, parts, w, m, v, *, tr, layer=None, into=None, after=None, transposed=False):
    L, R, C = w.shape
    S = parts[0].shape[0]
    tr = min(tr, R)
    n_l = L if layer is None else 1
    first = 0 if layer is None else layer

    def body(*refs):
        p_refs = refs[:n_l]
        w_ref, m_ref, v_ref = refs[n_l:n_l + 3]
        g_ref, d_ref, nm_ref, nv_ref = refs[-4:]
        for l in range(n_l):
            g = p_refs[l][0].astype(F32)
            for s in range(1, S):
                g = g + p_refs[l][s].astype(F32)
            g = g.T if transposed else g
            g_ref[l] = g
            d_ref[l], nm_ref[l], nv_ref[l] = _adamw(w_ref[l], g, m_ref[l], v_ref[l])

    blk = pl.BlockSpec((n_l, tr, C), lambda r: (first, r, 0))
    out = jax.ShapeDtypeStruct((L, R, C), F32)
    extra = list(into or []) + ([after] if after is not None else [])
    return pl.pallas_call(
        body, name=name, grid=(R // tr,),
        in_specs=[pl.BlockSpec((S, C, tr), lambda r: (0, 0, r)) if transposed else pl.BlockSpec((S, tr, C), lambda r: (0, r, 0))] * n_l
        + [blk, blk, blk]
        + [pl.BlockSpec(memory_space=pl.ANY)] * len(extra),
        out_specs=[blk] * 4, out_shape=[out] * 4,
        input_output_aliases={n_l + 3 + i: i for i in range(4)} if into else {},
        compiler_params=_params(dimension_semantics=("arbitrary",)),
    )(*parts, w, m, v, *extra)


def _small_update(me, parts, gains, taps, scale):
    dcol = gains[0].shape[1]
    gain_rows, tap_rows, scale_row, loss_row = [0, 1, 8, 9, 16, 17, 24, 25], [18, 19, 20], 2, 32

    def body(me_ref, full_ref, col_ref, wg, mg, vg, wt, mt, vt, ws, ms, vs, *outs):
        col = col_ref[0]
        g_scale = full_ref[0, scale_row:scale_row + 1, :]
        loss = full_ref[0, loss_row:loss_row + 1, 0:128]
        for d in range(1, N_DEV):
            col = col + col_ref[d]
            g_scale = g_scale + full_ref[d, scale_row:scale_row + 1, :]
            loss = loss + full_ref[d, loss_row:loss_row + 1, 0:128]
        for r, src in enumerate(gain_rows):
            outs[0][r:r + 1, :] = col[src:src + 1, :]
        for r, src in enumerate(tap_rows):
            outs[4][r] = col[src:src + 1, :]
        outs[8][...] = g_scale
        for k, (w, m, v) in enumerate(((wg, mg, vg), (wt, mt, vt), (ws, ms, vs))):
            outs[4 * k + 1][...], outs[4 * k + 2][...], outs[4 * k + 3][...] = _adamw(w[...], outs[4 * k][...], m[...], v[...])
        outs[12][...] = jnp.broadcast_to(loss, outs[12].shape)

    whole = lambda a: pl.BlockSpec(a.shape, lambda i, m: (0,) * a.ndim)
    small = [a for t in (gains, taps, scale) for a in t]
    shapes = [jax.ShapeDtypeStruct(t[0].shape, F32) for t in (gains, taps, scale) for _ in range(4)]
    shapes.append(jax.ShapeDtypeStruct((8, 128), F32))
    return pl.pallas_call(
        body, name="small_update",
        grid_spec=pltpu.PrefetchScalarGridSpec(
            num_scalar_prefetch=1, grid=(1,),
            in_specs=[whole(parts), pl.BlockSpec(parts.shape[:2] + (dcol,), lambda i, m: (0, 0, m[0]))] + [whole(a) for a in small],
            out_specs=[pl.BlockSpec(sh.shape, functools.partial(lambda i, m, n: (0,) * n, n=len(sh.shape))) for sh in shapes]),
        out_shape=shapes, compiler_params=_params(dimension_semantics=("arbitrary",)),
    )(jnp.reshape(me, (1,)).astype(jnp.int32), parts, parts, *small)


def kernel(x, norm_gains, pool_w, pool_scale, conv_in_w, conv_w, conv_out_w, ffn_gate_up_w, ffn_down_w, loss_target, m_norm_gains, m_pool_w, m_pool_scale, m_conv_in_w, m_conv_w, m_conv_out_w, m_ffn_gate_up_w, m_ffn_down_w, v_norm_gains, v_pool_w, v_pool_scale, v_conv_in_w, v_conv_w, v_conv_out_w, v_ffn_gate_up_w, v_ffn_down_w):
    T, D = x.shape[1], x.shape[2]
    tm = min(512, T)
    tm_b = min(256, T)
    tk = min(2048, T)
    n_layers = ffn_gate_up_w.shape[0]
    fb = ffn_gate_up_w.shape[2]
    fr = ffn_down_w.shape[1]
    dcol = norm_gains.shape[2]
    cb = conv_in_w.shape[2]
    gw = pool_w.shape[3]
    me = 4 * lax.axis_index("x") + 2 * lax.axis_index("y") + lax.axis_index("c")

    small_w = jnp.concatenate([norm_gains.reshape(8, dcol), jnp.pad(conv_w[0], ((0, 5), (0, 0)))], axis=0)
    every = range(1, N_DEV)
    wgu_t, m_wgu_t, v_wgu_t = (jnp.swapaxes(a, 1, 2) for a in (ffn_gate_up_w, m_ffn_gate_up_w, v_ffn_gate_up_w))
    own_lead = lambda s: lax.dynamic_update_slice(lax.empty((N_DEV,) + s.shape, s.dtype), s[None], (me,) + (0,) * s.ndim)
    lead_item = lambda a, l, dtype: (a, (None,) + a.shape[1:], (l, 0, 0), (N_DEV,) + a.shape[1:], dtype,
                                     (None,) + a.shape[1:], lambda i: (i, 0, 0))
    lands = _place_own(me, [
        (pool_w, (None,) + pool_w.shape[1:], (0, 0, 0, 0), (4, N_DEV, gw // N_DEV, gw), BF16,
         (4, None, gw // N_DEV, gw), lambda i: (0, i, 0, 0)),
        (small_w[None], (None,) + small_w.shape, (0, 0, 0), (N_DEV,) + small_w.shape, F32, (None,) + small_w.shape, lambda i: (i, 0, 0)),
        lead_item(wgu_t, 0, BF16), lead_item(ffn_down_w, 0, BF16), lead_item(conv_in_w, 0, BF16),
        lead_item(conv_out_w, 0, BF16), lead_item(wgu_t, 1, BF16), lead_item(ffn_down_w, 1, BF16)])
    n_first, n_big = 2, len(lands) - 2
    direct = ([Copy(m, 0, _second, 0, _second, m - 1) for m in every]
              + [Copy(m, 1, _lead, 1, _lead, N_DEV - 2 + m) for m in every])
    level1 = [Copy(mask, n_first + n, _lead, n_first + n, _lead, len(direct) + 4 * n + j)
              for n in range(n_big) for j, mask in enumerate((SIBLING,) + OTHER_CHIPS)]
    sems1, bufs1, _ = _split_call("gather_start", lands, start=direct + level1)
    pw_g, small_g = _split_call("gather_small_done", bufs1[:n_first], wait=direct, wait_sems=sems1)[1]
    pw = pw_g.reshape(4, gw, gw)
    small_full = jnp.swapaxes(small_g, 0, 1).reshape(16, D)
    gain = lambda l, s: small_full[4 * l + s][None, :]
    taps = small_full[8:16]

    def forward_on(name, group, after):
        k = len(group)
        landed = [Copy(cp.mask, i, cp.src, i, cp.dst, cp.sem)
                  for i, n in enumerate(group) for cp in level1 if cp.sb == n_first + n]
        onward = [Copy(SIBLING, i, (lambda ref, me, m=m: ref.at[me ^ m]), i, (lambda ref, sender, m=m: ref.at[sender ^ m]), 3 * i + j)
                  for i in range(k) for j, m in enumerate(OTHER_CHIPS)]
        sems2, bufs2, tok = _split_call(name + "_forward", [bufs1[n_first + n] for n in group], wait=landed,
                                        wait_sems=sems1, start=onward, after=after, token=True)
        return (name, onward, sems2, bufs2), tok

    def arrived(state, after=None):
        name, onward, sems2, lands2 = state
        return _split_call(name + "_done", lands2, wait=onward, wait_sems=sems2, after=after)[1]

    h0 = x[0]
    h1, pooled, mixed_pre = _pool_fwd(h0, pw, pool_scale, gain(0, 0), gain(0, 1), tm=tm)
    (wgu0,) = arrived(forward_on("gather_gate_up_0", [0], h1)[0])
    wgu0 = wgu0.reshape(N_DEV * fb, D)
    gu0, act0 = _ffn_up(h1, gain(0, 2), wgu0, tm=tm)
    ag_down0, tok = forward_on("gather_down_0", [1], act0)
    ag_conv, tok = forward_on("gather_conv", [2, 3], tok)
    (wd0,) = arrived(ag_down0, tok)
    wd0 = wd0.reshape(N_DEV * fr, D)
    h2, ff0 = _ffn_down(h1, act0, gain(0, 3), wd0, tm=tm)
    win_g, wout_g = arrived(ag_conv, h2)
    wout = wout_g.reshape(D, D)
    h3, proj, y, conv, z = _conv_fwd(h2, gain(1, 0), gain(1, 1), win_g, taps, wout, tm=tm)
    ag_ffn1, tok = forward_on("gather_ffn1", [4, 5], h3)
    wgu1, wd1 = arrived(ag_ffn1, tok)
    wgu1, wd1 = wgu1.reshape(N_DEV * fb, D), wd1.reshape(N_DEV * fr, D)
    dh4, gu1, act1, ff1, loss_part = _ffn_fwd(h3, gain(1, 2), gain(1, 3), wgu1, wd1, 1, loss_target[0], tm=tm)


    def scatter_start(name, pairs):
        k = len(pairs)
        sums, lands = [p[0] for p in pairs], [p[1] for p in pairs]
        plan = [Copy(m, n, (lambda ref, i, m=m: ref.at[(i ^ m) >> 1]), k + n, (lambda ref, i: ref.at[i >> 1]), 3 * n + j)
                for n in range(k) for j, m in enumerate(OTHER_CHIPS)]
        sems, bufs, tok = _split_call(name + "_start", sums + lands, start=plan, token=True)
        return (name, plan, sems, bufs), tok

    def scatter_done(state, after):
        name, plan, sems, bufs = state
        return _split_call(name + "_done", bufs, wait=plan, wait_sems=sems, after=after)[1][len(bufs) // 2:]

    seq = lambda i, k: (k, 0)
    gu_pair = pl.BlockSpec((None, tk, 2 * fb), lambda i, k: (i // 2, k, i % 2))
    act_pair = pl.BlockSpec((tk, 2 * fb), lambda i, k: (k, i))
    rows = pl.BlockSpec((tk, D), seq)
    dh3, dgu1, dff1, c1, small_f1 = _ffn_bwd(dh4, h3, ff1, gu1, gain(1, 2), gain(1, 3), wgu1, wd1, 1, tm=tm_b)
    g_wgu1 = _wgrad("wgrad_gate_up_1", dgu1, c1, gu_pair, rows, (2 * fb, D), N_DEV // 2, tk=tk)
    g_wd1 = _wgrad("wgrad_down_1", act1, dff1, act_pair, rows, (2 * fb, D), N_DEV // 4, tk=tk)
    rs_ffn1, tok = scatter_start("scatter_ffn1", [g_wgu1, g_wd1])
    dh2, dproj, a1, dy, small_c = _conv_bwd(dh3, h2, y, proj, conv, gain(1, 0), gain(1, 1), win_g, taps, wout, tm=tm, after=tok)
    whole_seq = pl.BlockSpec((T, D), seq)
    g_win = _wgrad("wgrad_conv_in", dproj, a1, pl.BlockSpec((T, 2 * cb), lambda i, k: (k, i)), whole_seq, (2 * cb, D),
                   N_DEV // 2, tk=T)
    g_wout = _wgrad("wgrad_conv_out", z, dy, whole_seq, whole_seq, (D, D), 1, tk=T)
    rs_conv, tok = scatter_start("scatter_conv", [g_win, g_wout])
    dh1, dgu0, dff0, c0, small_f0 = _ffn_bwd(dh2, h1, ff0, gu0, gain(0, 2), gain(0, 3), wgu0, wd0, 0, tm=tm_b, after=tok)
    g_wgu0 = _wgrad("wgrad_gate_up_0", dgu0, c0, gu_pair, rows, (2 * fb, D), N_DEV // 2, tk=tk)
    rs_wgu0, tok = scatter_start("scatter_gate_up_0", [g_wgu0])
    g_wd0 = _wgrad("wgrad_down_0", act0, dff0, act_pair, rows, (2 * fb, D), N_DEV // 4, tk=tk, after=tok)
    rs_wd0, tok = scatter_start("scatter_down_0", [g_wd0])
    grad_x, g_pw, small_p = _pool_bwd(dh1, h0, pooled, mixed_pre, pw, pool_scale, gain(0, 0), gain(0, 1), tm=tm, after=tok)

    loss_rows = jnp.broadcast_to(loss_part[0:1, 0:1], (8, D))
    small_part = jnp.concatenate([small_p, small_f0, small_c, small_f1, loss_rows], axis=0)
    g_pw = g_pw.reshape(4, N_DEV, gw // N_DEV, gw)
    pw_land = lax.dynamic_update_slice(lax.empty(g_pw.shape, BF16), lax.dynamic_slice_in_dim(g_pw, me, 1, 1), (0, me, 0, 0))
    last = ([Copy(m, 0, (lambda ref, i, m=m: ref.at[:, i ^ m]), 2, _second, m - 1) for m in every]
            + [Copy(m, 1, _whole, 3, _lead, N_DEV - 2 + m) for m in every])
    sems_l, bufs_l, tok = _split_call("scatter_small_start", [g_pw, small_part, pw_land, own_lead(small_part)],
                                      start=last, token=True)

    (r_wgu1, r_wd1), (r_win, r_wout) = scatter_done(rs_ffn1, tok), scatter_done(rs_conv, tok)
    o_win = _reduce_adam("adam_conv_in", [r_win], conv_in_w, m_conv_in_w, v_conv_in_w, tr=256, transposed=True)
    o_wout = _reduce_adam("adam_conv_out", [r_wout], conv_out_w, m_conv_out_w, v_conv_out_w, tr=128, after=o_win[0])
    o_wgu = _reduce_adam("adam_gate_up_1", [r_wgu1], wgu_t, m_wgu_t, v_wgu_t, tr=176, layer=1, after=o_wout[0])
    o_wd = _reduce_adam("adam_down_1", [r_wd1], ffn_down_w, m_ffn_down_w, v_ffn_down_w, tr=176, layer=1, after=o_wgu[0])
    r_pw, r_small = _split_call("scatter_small_done", bufs_l, wait=last, wait_sems=sems_l, after=o_wd[0])[1][2:]
    (r_wgu0,), (r_wd0,) = scatter_done(rs_wgu0, r_small), scatter_done(rs_wd0, r_small)
    o_wgu = _reduce_adam("adam_gate_up_0", [r_wgu0], wgu_t, m_wgu_t, v_wgu_t, tr=176, layer=0, into=o_wgu)
    o_wgu = [jnp.swapaxes(o, 1, 2) for o in o_wgu]
    o_wd = _reduce_adam("adam_down_0", [r_wd0], ffn_down_w, m_ffn_down_w, v_ffn_down_w, tr=176, layer=0, into=o_wd)
    o_pw = _reduce_adam("adam_pool_w", [r_pw[g] for g in range(4)], pool_w[0], m_pool_w[0], v_pool_w[0], tr=32)
    o_small = _small_update(
        me, r_small,
        (norm_gains.reshape(8, dcol), m_norm_gains.reshape(8, dcol), v_norm_gains.reshape(8, dcol)),
        tuple(jnp.swapaxes(a, 0, 1) for a in (conv_w, m_conv_w, v_conv_w)), (pool_scale, m_pool_scale, v_pool_scale))
    o_gain, o_taps, o_scale, loss = o_small[0:4], o_small[4:8], o_small[8:12], o_small[12][0, 0]

    gshape = norm_gains.shape
    per = lambda k: (o_gain[k].reshape(gshape), o_pw[k][None], o_scale[k], o_win[k], jnp.swapaxes(o_taps[k], 0, 1),
                     o_wout[k], o_wgu[k], o_wd[k])
    return (loss, grad_x[None], *per(0), *per(1), *per(2), *per(3))
```
